```python
import jax, jax.numpy as jnp
from jax import lax
import numpy as np

D_MODEL = 2048
BATCH = 8
SEQ = 4096
DEPTH = 4

HEAD_DIM = 128
A_WIDTH = D_MODEL // 2
A_HEADS = A_WIDTH // HEAD_DIM
DILATED_PATTERNS = ((128, 1), (512, 4), (2048, 16))
ATTN_BLOCK = 64
ROPE_THETA = 10000.0
NEG_INF = -1e30
B_WIDTH = D_MODEL // 2
B_CONV = 3
C_WIDTH = D_MODEL
C_GROUPS = 8
C_CHUNK = 128
AB_IN_WIDTH = 4 * A_WIDTH + 4 * B_WIDTH
SG_IN_WIDTH = 3 * C_WIDTH
N_EVEN = (DEPTH + 1) // 2
N_ODD = DEPTH // 2
EPS = 1e-6

kernel_name = 'hybrid_dilated_attn_shortconv_sgu_adaln'


def rms_norm(x, g):
    xf = x.astype(jnp.float32)
    y = xf * lax.rsqrt(jnp.mean(xf * xf, axis=-1, keepdims=True) + EPS)
    return (y * g.astype(jnp.float32)).astype(x.dtype)


def layer_norm(x, g, b):
    xf = x.astype(jnp.float32)
    mu = jnp.mean(xf, axis=-1, keepdims=True)
    xc = xf - mu
    y = xc * lax.rsqrt(jnp.mean(xc * xc, axis=-1, keepdims=True) + EPS)
    return (y * g.astype(jnp.float32) + b.astype(jnp.float32)).astype(x.dtype)


def ada_modulation(c, w_mod, b_mod):
    m = jax.nn.silu(c) @ w_mod + b_mod
    shift, scale, gate = jnp.split(m, 3, axis=-1)
    return shift[:, None, :], scale[:, None, :], gate[:, None, :]


def rope(t, pos):
    half = t.shape[-1] // 2
    inv = ROPE_THETA ** (-jnp.arange(half, dtype=jnp.float32) / half)
    ang = pos[:, None] * inv[None, :]
    cos = jnp.cos(ang)[None, :, None, :]
    sin = jnp.sin(ang)[None, :, None, :]
    tf = t.astype(jnp.float32)
    t1, t2 = tf[..., :half], tf[..., half:]
    out = jnp.concatenate([t1 * cos - t2 * sin, t2 * cos + t1 * sin], axis=-1)
    return out.astype(t.dtype)


def dilated_window_attention(q, k, v, dilation, radius):
    b, h, s, hd = q.shape
    sub_len = s // dilation
    n_blk = -(-sub_len // ATTN_BLOCK)
    lp = n_blk * ATTN_BLOCK
    pad = lp - sub_len

    def to_sub(t):
        return t.reshape(b, h, sub_len, dilation, hd).transpose(0, 1, 3, 2, 4)

    qs = jnp.pad(to_sub(q), ((0, 0), (0, 0), (0, 0), (0, pad), (0, 0)))
    qb = qs.reshape(b, h, dilation, n_blk, ATTN_BLOCK, hd)
    halo = ((0, 0), (0, 0), (0, 0), (ATTN_BLOCK, pad + ATTN_BLOCK), (0, 0))
    kp = jnp.pad(to_sub(k), halo)
    vp = jnp.pad(to_sub(v), halo)

    def band(t):
        return jnp.concatenate(
            [t[:, :, :, o:o + lp].reshape(b, h, dilation, n_blk, ATTN_BLOCK, hd)
             for o in (0, ATTN_BLOCK, 2 * ATTN_BLOCK)], axis=-2)

    kb, vb = band(kp), band(vp)
    blk = jnp.arange(n_blk)[:, None, None] * ATTN_BLOCK
    q_idx = blk + jnp.arange(ATTN_BLOCK)[None, :, None]
    k_idx = blk - ATTN_BLOCK + jnp.arange(3 * ATTN_BLOCK)[None, None, :]
    valid = (jnp.abs(q_idx - k_idx) <= radius) & (k_idx >= 0) & (k_idx < sub_len)

    scores = jnp.einsum('bhrnqd,bhrnkd->bhrnqk', qb, kb,
                        preferred_element_type=jnp.float32) * (hd ** -0.5)
    scores = jnp.where(valid, scores, NEG_INF)
    m = jnp.max(scores, axis=-1, keepdims=True)
    p = jnp.exp(scores - m)
    den = jnp.sum(p, axis=-1, keepdims=True)
    o = jnp.einsum('bhrnqk,bhrnkd->bhrnqd', p, vb.astype(jnp.float32)) / den
    lse = (m + jnp.log(den))[..., 0]
    o = o.reshape(b, h, dilation, lp, hd)[:, :, :, :sub_len]
    o = o.transpose(0, 1, 3, 2, 4).reshape(b, h, s, hd)
    lse = lse.reshape(b, h, dilation, lp)[..., :sub_len].transpose(0, 1, 3, 2).reshape(b, h, s)
    return o, lse


def dilated_mixture_attention(q, k, v):
    outs, lses = [], []
    for window, dilation in DILATED_PATTERNS:
        o, lse = dilated_window_attention(q, k, v, dilation, window // (2 * dilation))
        outs.append(o)
        lses.append(lse)
    w = jax.nn.softmax(jnp.stack(lses, axis=0), axis=0)
    return jnp.einsum('pbhs,pbhsd->bhsd', w, jnp.stack(outs, axis=0))


def short_conv(u, w):
    return lax.conv_general_dilated(
        u, w[:, None, :].astype(u.dtype), window_strides=(1,), padding=((1, 1),),
        dimension_numbers=('NWC', 'WIO', 'NWC'), feature_group_count=u.shape[-1])


def mixer_ab(h, w_in, conv_w, w_out):
    b, s, _ = h.shape
    proj = h @ w_in
    cuts = np.cumsum([A_WIDTH] * 4 + [B_WIDTH] * 3).tolist()
    q, k, v, z_a, u_b, g_b, g_c, z_b = jnp.split(proj, cuts, axis=-1)
    pos = jnp.arange(s, dtype=jnp.float32)
    q = rope(q.reshape(b, s, A_HEADS, HEAD_DIM), pos).transpose(0, 2, 1, 3)
    k = rope(k.reshape(b, s, A_HEADS, HEAD_DIM), pos).transpose(0, 2, 1, 3)
    v = v.reshape(b, s, A_HEADS, HEAD_DIM).transpose(0, 2, 1, 3)
    attn = dilated_mixture_attention(q, k, v)
    y_a = attn.transpose(0, 2, 1, 3).reshape(b, s, A_WIDTH).astype(h.dtype) * jax.nn.silu(z_a)
    y_b = g_b * short_conv(g_c * u_b, conv_w) * jax.nn.silu(z_b)
    return jnp.concatenate([y_a, y_b], axis=-1) @ w_out


def mixer_sgu(h, w_in, ln_g, ln_b, w_s, b_s, w_out):
    b, s, _ = h.shape
    u, v, z = jnp.split(h @ w_in, 3, axis=-1)
    u = jax.nn.gelu(u)
    v = layer_norm(jax.nn.gelu(v), ln_g, ln_b)
    v = v.reshape(b, s // C_CHUNK, C_CHUNK, C_GROUPS, C_WIDTH // C_GROUPS)
    mixed = jnp.einsum('gts,bnsgc->bntgc', w_s, v) + b_s.T[None, None, :, :, None]
    y = u * mixed.reshape(b, s, C_WIDTH) * jax.nn.silu(z)
    return y @ w_out


def _fwd_setup_inputs(seed: int = 0) -> dict:
    key = jax.random.key(seed)
    ks = jax.random.split(key, 20)
    D = D_MODEL

    def nrm(k, shape, scale):
        return jax.random.normal(k, shape, jnp.float32) * scale

    return {
        'x': nrm(ks[0], (BATCH, SEQ, D), 1.0),
        'c': nrm(ks[1], (BATCH, D), 1.0),
        'ab_norm_g': 1.0 + nrm(ks[2], (N_EVEN, D), 0.02),
        'ab_w_mod': nrm(ks[3], (N_EVEN, D, 3 * D), 0.5 * D ** -0.5),
        'ab_b_mod': nrm(ks[4], (N_EVEN, 3 * D), 0.01),
        'ab_w_in': nrm(ks[5], (N_EVEN, D, AB_IN_WIDTH), D ** -0.5),
        'ab_conv_w': nrm(ks[6], (N_EVEN, B_CONV, B_WIDTH), B_CONV ** -0.5),
        'ab_w_out': nrm(ks[7], (N_EVEN, A_WIDTH + B_WIDTH, D), (A_WIDTH + B_WIDTH) ** -0.5),
        'sg_norm_g': 1.0 + nrm(ks[8], (N_ODD, D), 0.02),
        'sg_w_mod': nrm(ks[9], (N_ODD, D, 3 * D), 0.5 * D ** -0.5),
        'sg_b_mod': nrm(ks[10], (N_ODD, 3 * D), 0.01),
        'sg_w_in': nrm(ks[11], (N_ODD, D, SG_IN_WIDTH), D ** -0.5),
        'sg_ln_g': 1.0 + nrm(ks[12], (N_ODD, C_WIDTH), 0.02),
        'sg_ln_b': nrm(ks[13], (N_ODD, C_WIDTH), 0.01),
        'sg_w_s': nrm(ks[14], (N_ODD, C_GROUPS, C_CHUNK, C_CHUNK), C_CHUNK ** -0.5),
        'sg_b_s': 1.0 + nrm(ks[15], (N_ODD, C_GROUPS, C_CHUNK), 0.01),
        'sg_w_out': nrm(ks[16], (N_ODD, C_WIDTH, D), C_WIDTH ** -0.5),
        'final_norm_g': 1.0 + nrm(ks[17], (D,), 0.02),
    }


def _fwd_reference(x, c, ab_norm_g, ab_w_mod, ab_b_mod, ab_w_in, ab_conv_w, ab_w_out,
              sg_norm_g, sg_w_mod, sg_b_mod, sg_w_in, sg_ln_g, sg_ln_b, sg_w_s, sg_b_s,
              sg_w_out, final_norm_g):
    for layer in range(DEPTH):
        i = layer // 2
        if layer % 2 == 0:
            shift, scale, gate = ada_modulation(c, ab_w_mod[i], ab_b_mod[i])
            h = rms_norm(x, ab_norm_g[i]) * (1.0 + scale) + shift
            out = mixer_ab(h, ab_w_in[i], ab_conv_w[i], ab_w_out[i])
        else:
            shift, scale, gate = ada_modulation(c, sg_w_mod[i], sg_b_mod[i])
            h = rms_norm(x, sg_norm_g[i]) * (1.0 + scale) + shift
            out = mixer_sgu(h, sg_w_in[i], sg_ln_g[i], sg_ln_b[i], sg_w_s[i], sg_b_s[i], sg_w_out[i])
        x = x + gate * out
    return rms_norm(x, final_norm_g)


import jax as _jax
import jax.numpy as _jnp

TWIN_FORMAT = 'train_step'
FWD_PARAMS = ['x', 'c', 'ab_norm_g', 'ab_w_mod', 'ab_b_mod', 'ab_w_in', 'ab_conv_w', 'ab_w_out', 'sg_norm_g', 'sg_w_mod', 'sg_b_mod', 'sg_w_in', 'sg_ln_g', 'sg_ln_b', 'sg_w_s', 'sg_b_s', 'sg_w_out', 'final_norm_g']
TWIN_WEIGHTS = ['ab_norm_g', 'ab_w_mod', 'ab_b_mod', 'ab_w_in', 'ab_conv_w', 'ab_w_out', 'sg_norm_g', 'sg_w_mod', 'sg_b_mod', 'sg_w_in', 'sg_ln_g', 'sg_ln_b', 'sg_w_s', 'sg_b_s', 'sg_w_out', 'final_norm_g']
TWIN_DIFF_INPUT = 'x'
TWIN_INPUTS = ['x', 'c', 'ab_norm_g', 'ab_w_mod', 'ab_b_mod', 'ab_w_in', 'ab_conv_w', 'ab_w_out', 'sg_norm_g', 'sg_w_mod', 'sg_b_mod', 'sg_w_in', 'sg_ln_g', 'sg_ln_b', 'sg_w_s', 'sg_b_s', 'sg_w_out', 'final_norm_g', 'loss_target', 'm_ab_norm_g', 'm_ab_w_mod', 'm_ab_b_mod', 'm_ab_w_in', 'm_ab_conv_w', 'm_ab_w_out', 'm_sg_norm_g', 'm_sg_w_mod', 'm_sg_b_mod', 'm_sg_w_in', 'm_sg_ln_g', 'm_sg_ln_b', 'm_sg_w_s', 'm_sg_b_s', 'm_sg_w_out', 'm_final_norm_g', 'v_ab_norm_g', 'v_ab_w_mod', 'v_ab_b_mod', 'v_ab_w_in', 'v_ab_conv_w', 'v_ab_w_out', 'v_sg_norm_g', 'v_sg_w_mod', 'v_sg_b_mod', 'v_sg_w_in', 'v_sg_ln_g', 'v_sg_ln_b', 'v_sg_w_s', 'v_sg_b_s', 'v_sg_w_out', 'v_final_norm_g']
TWIN_OUTPUTS = ['loss', 'grad_x', 'grad_ab_norm_g', 'grad_ab_w_mod', 'grad_ab_b_mod', 'grad_ab_w_in', 'grad_ab_conv_w', 'grad_ab_w_out', 'grad_sg_norm_g', 'grad_sg_w_mod', 'grad_sg_b_mod', 'grad_sg_w_in', 'grad_sg_ln_g', 'grad_sg_ln_b', 'grad_sg_w_s', 'grad_sg_b_s', 'grad_sg_w_out', 'grad_final_norm_g', 'delta_ab_norm_g', 'delta_ab_w_mod', 'delta_ab_b_mod', 'delta_ab_w_in', 'delta_ab_conv_w', 'delta_ab_w_out', 'delta_sg_norm_g', 'delta_sg_w_mod', 'delta_sg_b_mod', 'delta_sg_w_in', 'delta_sg_ln_g', 'delta_sg_ln_b', 'delta_sg_w_s', 'delta_sg_b_s', 'delta_sg_w_out', 'delta_final_norm_g', 'new_m_ab_norm_g', 'new_m_ab_w_mod', 'new_m_ab_b_mod', 'new_m_ab_w_in', 'new_m_ab_conv_w', 'new_m_ab_w_out', 'new_m_sg_norm_g', 'new_m_sg_w_mod', 'new_m_sg_b_mod', 'new_m_sg_w_in', 'new_m_sg_ln_g', 'new_m_sg_ln_b', 'new_m_sg_w_s', 'new_m_sg_b_s', 'new_m_sg_w_out', 'new_m_final_norm_g', 'new_v_ab_norm_g', 'new_v_ab_w_mod', 'new_v_ab_b_mod', 'new_v_ab_w_in', 'new_v_ab_conv_w', 'new_v_ab_w_out', 'new_v_sg_norm_g', 'new_v_sg_w_mod', 'new_v_sg_b_mod', 'new_v_sg_w_in', 'new_v_sg_ln_g', 'new_v_sg_ln_b', 'new_v_sg_w_s', 'new_v_sg_b_s', 'new_v_sg_w_out', 'new_v_final_norm_g']
TWIN_LEAF_KINDS = {'loss': 'loss', 'grad_x': 'grad_x', 'grad_ab_norm_g': 'grad_w', 'grad_ab_w_mod': 'grad_w', 'grad_ab_b_mod': 'grad_w', 'grad_ab_w_in': 'grad_w', 'grad_ab_conv_w': 'grad_w', 'grad_ab_w_out': 'grad_w', 'grad_sg_norm_g': 'grad_w', 'grad_sg_w_mod': 'grad_w', 'grad_sg_b_mod': 'grad_w', 'grad_sg_w_in': 'grad_w', 'grad_sg_ln_g': 'grad_w', 'grad_sg_ln_b': 'grad_w', 'grad_sg_w_s': 'grad_w', 'grad_sg_b_s': 'grad_w', 'grad_sg_w_out': 'grad_w', 'grad_final_norm_g': 'grad_w', 'delta_ab_norm_g': 'delta_w', 'delta_ab_w_mod': 'delta_w', 'delta_ab_b_mod': 'delta_w', 'delta_ab_w_in': 'delta_w', 'delta_ab_conv_w': 'delta_w', 'delta_ab_w_out': 'delta_w', 'delta_sg_norm_g': 'delta_w', 'delta_sg_w_mod': 'delta_w', 'delta_sg_b_mod': 'delta_w', 'delta_sg_w_in': 'delta_w', 'delta_sg_ln_g': 'delta_w', 'delta_sg_ln_b': 'delta_w', 'delta_sg_w_s': 'delta_w', 'delta_sg_b_s': 'delta_w', 'delta_sg_w_out': 'delta_w', 'delta_final_norm_g': 'delta_w', 'new_m_ab_norm_g': 'new_m', 'new_m_ab_w_mod': 'new_m', 'new_m_ab_b_mod': 'new_m', 'new_m_ab_w_in': 'new_m', 'new_m_ab_conv_w': 'new_m', 'new_m_ab_w_out': 'new_m', 'new_m_sg_norm_g': 'new_m', 'new_m_sg_w_mod': 'new_m', 'new_m_sg_b_mod': 'new_m', 'new_m_sg_w_in': 'new_m', 'new_m_sg_ln_g': 'new_m', 'new_m_sg_ln_b': 'new_m', 'new_m_sg_w_s': 'new_m', 'new_m_sg_b_s': 'new_m', 'new_m_sg_w_out': 'new_m', 'new_m_final_norm_g': 'new_m', 'new_v_ab_norm_g': 'new_v', 'new_v_ab_w_mod': 'new_v', 'new_v_ab_b_mod': 'new_v', 'new_v_ab_w_in': 'new_v', 'new_v_ab_conv_w': 'new_v', 'new_v_ab_w_out': 'new_v', 'new_v_sg_norm_g': 'new_v', 'new_v_sg_w_mod': 'new_v', 'new_v_sg_b_mod': 'new_v', 'new_v_sg_w_in': 'new_v', 'new_v_sg_ln_g': 'new_v', 'new_v_sg_ln_b': 'new_v', 'new_v_sg_w_s': 'new_v', 'new_v_sg_b_s': 'new_v', 'new_v_sg_w_out': 'new_v', 'new_v_final_norm_g': 'new_v'}


def _forward(args):
    return _fwd_reference(*[args[k] for k in FWD_PARAMS])


def _output_shape():
    def fwd():
        inp = _fwd_setup_inputs(0)
        return _fwd_reference(*[inp[k] for k in FWD_PARAMS])
    out = _jax.eval_shape(fwd)
    return out.shape, out.dtype

N_MICROBATCH = 1
ADAM_LR = 0.001
ADAM_B1 = 0.9
ADAM_B2 = 0.999
ADAM_EPS = 1e-08
ADAM_WD = 0.01
ADAM_STEP = 10
PER_EXAMPLE_BATCH_AXIS = {'x': 0, 'c': 0, 'loss_target': 0}
SHARED_INPUTS = []
_WEIGHT_DTYPES = {'ab_norm_g': _jnp.float32, 'ab_w_mod': _jnp.float32, 'ab_b_mod': _jnp.float32, 'ab_w_in': _jnp.float32, 'ab_conv_w': _jnp.float32, 'ab_w_out': _jnp.float32, 'sg_norm_g': _jnp.float32, 'sg_w_mod': _jnp.float32, 'sg_b_mod': _jnp.float32, 'sg_w_in': _jnp.float32, 'sg_ln_g': _jnp.float32, 'sg_ln_b': _jnp.float32, 'sg_w_s': _jnp.float32, 'sg_b_s': _jnp.float32, 'sg_w_out': _jnp.float32, 'final_norm_g': _jnp.float32}
MOMENT_SCALE = {'ab_norm_g': 3.220218e-02, 'ab_w_mod': 2.555504e-02, 'ab_b_mod': 4.300484e-02, 'ab_w_in': 1.667853e-02, 'ab_conv_w': 2.282339e-02, 'ab_w_out': 1.672882e-02, 'sg_norm_g': 2.802078e-02, 'sg_w_mod': 2.445771e-02, 'sg_b_mod': 4.087418e-02, 'sg_w_in': 1.672868e-02, 'sg_ln_g': 1.222811e-02, 'sg_ln_b': 1.204963e-02, 'sg_w_s': 1.744000e-02, 'sg_b_s': 1.767747e-02, 'sg_w_out': 1.729681e-02, 'final_norm_g': 1.599449e+01}


def _to_microbatches(a, axis):
    t = _jnp.moveaxis(a, axis, 0)
    t = t.reshape((N_MICROBATCH, t.shape[0] // N_MICROBATCH) + t.shape[1:])
    return _jnp.moveaxis(t, 1, axis + 1)


def setup_inputs(seed: int = 0) -> dict:
    inp = _fwd_setup_inputs(seed)
    key = _jax.random.fold_in(_jax.random.key(seed), 7919)
    shape, _ = _output_shape()
    out = dict(inp)
    out["loss_target"] = _jax.random.normal(_jax.random.fold_in(key, 0), shape, _jnp.float32)
    for i, name in enumerate(TWIN_WEIGHTS):
        w = inp[name].astype(_jnp.float32)
        if MOMENT_SCALE is None:
            s = _jnp.sqrt(_jnp.mean(_jnp.square(w)) + 1e-30)
        else:
            s = MOMENT_SCALE[name]
        km, kv = _jax.random.split(_jax.random.fold_in(key, i + 1))
        out[name] = w
        out["m_" + name] = s * _jax.random.normal(km, w.shape, _jnp.float32)
        out["v_" + name] = (s * s) * _jax.random.uniform(kv, w.shape, _jnp.float32, 0.5, 1.5)
    if N_MICROBATCH > 1:
        for name, axis in PER_EXAMPLE_BATCH_AXIS.items():
            out[name] = _to_microbatches(out[name], axis)
    return {'x': out['x'], 'c': out['c'], 'ab_norm_g': out['ab_norm_g'], 'ab_w_mod': out['ab_w_mod'], 'ab_b_mod': out['ab_b_mod'], 'ab_w_in': out['ab_w_in'], 'ab_conv_w': out['ab_conv_w'], 'ab_w_out': out['ab_w_out'], 'sg_norm_g': out['sg_norm_g'], 'sg_w_mod': out['sg_w_mod'], 'sg_b_mod': out['sg_b_mod'], 'sg_w_in': out['sg_w_in'], 'sg_ln_g': out['sg_ln_g'], 'sg_ln_b': out['sg_ln_b'], 'sg_w_s': out['sg_w_s'], 'sg_b_s': out['sg_b_s'], 'sg_w_out': out['sg_w_out'], 'final_norm_g': out['final_norm_g'], 'loss_target': out['loss_target'], 'm_ab_norm_g': out['m_ab_norm_g'], 'm_ab_w_mod': out['m_ab_w_mod'], 'm_ab_b_mod': out['m_ab_b_mod'], 'm_ab_w_in': out['m_ab_w_in'], 'm_ab_conv_w': out['m_ab_conv_w'], 'm_ab_w_out': out['m_ab_w_out'], 'm_sg_norm_g': out['m_sg_norm_g'], 'm_sg_w_mod': out['m_sg_w_mod'], 'm_sg_b_mod': out['m_sg_b_mod'], 'm_sg_w_in': out['m_sg_w_in'], 'm_sg_ln_g': out['m_sg_ln_g'], 'm_sg_ln_b': out['m_sg_ln_b'], 'm_sg_w_s': out['m_sg_w_s'], 'm_sg_b_s': out['m_sg_b_s'], 'm_sg_w_out': out['m_sg_w_out'], 'm_final_norm_g': out['m_final_norm_g'], 'v_ab_norm_g': out['v_ab_norm_g'], 'v_ab_w_mod': out['v_ab_w_mod'], 'v_ab_b_mod': out['v_ab_b_mod'], 'v_ab_w_in': out['v_ab_w_in'], 'v_ab_conv_w': out['v_ab_conv_w'], 'v_ab_w_out': out['v_ab_w_out'], 'v_sg_norm_g': out['v_sg_norm_g'], 'v_sg_w_mod': out['v_sg_w_mod'], 'v_sg_b_mod': out['v_sg_b_mod'], 'v_sg_w_in': out['v_sg_w_in'], 'v_sg_ln_g': out['v_sg_ln_g'], 'v_sg_ln_b': out['v_sg_ln_b'], 'v_sg_w_s': out['v_sg_w_s'], 'v_sg_b_s': out['v_sg_b_s'], 'v_sg_w_out': out['v_sg_w_out'], 'v_final_norm_g': out['v_final_norm_g']}


def _loss(weights, diff, rest, loss_target):
    with _jax.named_scope("forward"):
        args = {**rest, TWIN_DIFF_INPUT: diff, **{k: w.astype(_WEIGHT_DTYPES[k]) for k, w in weights.items()}}
        y = _forward(args)
    with _jax.named_scope("loss_head"):
        err = _jnp.square(y.astype(_jnp.float32) - loss_target)
        return 0.5 * _jnp.sum(_jnp.mean(err, axis=-1)) if err.ndim else 0.5 * err


def _adamw(w, g, m, v):
    m = ADAM_B1 * m + (1.0 - ADAM_B1) * g
    v = ADAM_B2 * v + (1.0 - ADAM_B2) * _jnp.square(g)
    m_hat = m / (1.0 - ADAM_B1 ** ADAM_STEP)
    v_hat = v / (1.0 - ADAM_B2 ** ADAM_STEP)
    delta = -ADAM_LR * (m_hat / (_jnp.sqrt(v_hat) + ADAM_EPS) + ADAM_WD * w)
    return delta, m, v


def reference(x, c, ab_norm_g, ab_w_mod, ab_b_mod, ab_w_in, ab_conv_w, ab_w_out, sg_norm_g, sg_w_mod, sg_b_mod, sg_w_in, sg_ln_g, sg_ln_b, sg_w_s, sg_b_s, sg_w_out, final_norm_g, loss_target, m_ab_norm_g, m_ab_w_mod, m_ab_b_mod, m_ab_w_in, m_ab_conv_w, m_ab_w_out, m_sg_norm_g, m_sg_w_mod, m_sg_b_mod, m_sg_w_in, m_sg_ln_g, m_sg_ln_b, m_sg_w_s, m_sg_b_s, m_sg_w_out, m_final_norm_g, v_ab_norm_g, v_ab_w_mod, v_ab_b_mod, v_ab_w_in, v_ab_conv_w, v_ab_w_out, v_sg_norm_g, v_sg_w_mod, v_sg_b_mod, v_sg_w_in, v_sg_ln_g, v_sg_ln_b, v_sg_w_s, v_sg_b_s, v_sg_w_out, v_final_norm_g):
    given = dict(x=x, c=c, ab_norm_g=ab_norm_g, ab_w_mod=ab_w_mod, ab_b_mod=ab_b_mod, ab_w_in=ab_w_in, ab_conv_w=ab_conv_w, ab_w_out=ab_w_out, sg_norm_g=sg_norm_g, sg_w_mod=sg_w_mod, sg_b_mod=sg_b_mod, sg_w_in=sg_w_in, sg_ln_g=sg_ln_g, sg_ln_b=sg_ln_b, sg_w_s=sg_w_s, sg_b_s=sg_b_s, sg_w_out=sg_w_out, final_norm_g=final_norm_g, loss_target=loss_target, m_ab_norm_g=m_ab_norm_g, m_ab_w_mod=m_ab_w_mod, m_ab_b_mod=m_ab_b_mod, m_ab_w_in=m_ab_w_in, m_ab_conv_w=m_ab_conv_w, m_ab_w_out=m_ab_w_out, m_sg_norm_g=m_sg_norm_g, m_sg_w_mod=m_sg_w_mod, m_sg_b_mod=m_sg_b_mod, m_sg_w_in=m_sg_w_in, m_sg_ln_g=m_sg_ln_g, m_sg_ln_b=m_sg_ln_b, m_sg_w_s=m_sg_w_s, m_sg_b_s=m_sg_b_s, m_sg_w_out=m_sg_w_out, m_final_norm_g=m_final_norm_g, v_ab_norm_g=v_ab_norm_g, v_ab_w_mod=v_ab_w_mod, v_ab_b_mod=v_ab_b_mod, v_ab_w_in=v_ab_w_in, v_ab_conv_w=v_ab_conv_w, v_ab_w_out=v_ab_w_out, v_sg_norm_g=v_sg_norm_g, v_sg_w_mod=v_sg_w_mod, v_sg_b_mod=v_sg_b_mod, v_sg_w_in=v_sg_w_in, v_sg_ln_g=v_sg_ln_g, v_sg_ln_b=v_sg_ln_b, v_sg_w_s=v_sg_w_s, v_sg_b_s=v_sg_b_s, v_sg_w_out=v_sg_w_out, v_final_norm_g=v_final_norm_g)
    weights = {n: given[n] for n in TWIN_WEIGHTS}
    shared = {n: given[n] for n in SHARED_INPUTS}
    per_example = {n: given[n] for n in ['x', 'c']}
    grad_fn = _jax.value_and_grad(_loss, argnums=(0, 1))

    def one_microbatch(ex, loss_target):
        ex = dict(ex)
        diff = ex.pop(TWIN_DIFF_INPUT)
        return grad_fn(weights, diff, {**shared, **ex}, loss_target)

    if N_MICROBATCH == 1:
        loss, (grad_w, grad_x) = one_microbatch(per_example, given["loss_target"])
    else:
        def body(carry, xs):
            loss_sum, grad_sum = carry
            l_k, (gw_k, gx_k) = one_microbatch(xs[0], xs[1])
            with _jax.named_scope("update"):
                return (loss_sum + l_k, _jax.tree.map(_jnp.add, grad_sum, gw_k)), gx_k

        init = (_jnp.zeros((), _jnp.float32), _jax.tree.map(_jnp.zeros_like, weights))
        (loss, grad_w), grad_x = _jax.lax.scan(body, init, (per_example, given["loss_target"]))
    with _jax.named_scope("update"):
        delta_w, new_m, new_v = {}, {}, {}
        for n in TWIN_WEIGHTS:
            delta_w[n], new_m[n], new_v[n] = _adamw(weights[n], grad_w[n], given["m_" + n], given["v_" + n])
    return (loss, grad_x, *[grad_w[n] for n in TWIN_WEIGHTS], *[delta_w[n] for n in TWIN_WEIGHTS],
            *[new_m[n] for n in TWIN_WEIGHTS], *[new_v[n] for n in TWIN_WEIGHTS])
```

```python
import functools
import math

import jax
import jax.numpy as jnp
from jax import lax
from jax.experimental import pallas as pl
from jax.experimental.pallas import tpu as pltpu

F32 = jnp.float32
BF16 = jnp.bfloat16

HEAD_DIM = 128
RADIUS = 64
DILATIONS = (1, 4, 16)
Q_BLOCK = 128
K_WINDOW = Q_BLOCK + 2 * RADIUS
ROPE_THETA = 10000.0
NEG_INF = -1e30
N_GROUPS = 8
CHUNK = 128
EPS = 1e-6
CONV_ROWS = 512
CONV_HALO = 16
LANES = 128
ELEMENTWISE_BLOCK = 256 * 1024
N_DEV = 8
N_CHIPS = 4

ADAM_LR = 0.001
ADAM_B1 = 0.9
ADAM_B2 = 0.999
ADAM_EPS = 1e-08
ADAM_WD = 0.01
ADAM_STEP = 10

VMEM_LIMIT_V7X = 56 * 1024 * 1024

MESH_ID = pl.DeviceIdType.MESH
HBM_SPEC = pl.BlockSpec(memory_space=pltpu.HBM)

NN = (((1,), (0,)), ((), ()))
NT = (((1,), (1,)), ((), ()))
TN = (((0,), (0,)), ((), ()))


def _params(n_grid, parallel=0):
    sem = tuple(["parallel"] * parallel + ["arbitrary"] * (n_grid - parallel))
    return pltpu.CompilerParams(dimension_semantics=sem, vmem_limit_bytes=VMEM_LIMIT_V7X)


def _tile(n, prefs):
    for p in prefs:
        if n % p == 0:
            return p
    return n


def _sigmoid(z):
    return 1.0 / (1.0 + jnp.exp(-z))


def _silu(z):
    return z * _sigmoid(z)


def _dsilu(z):
    s = _sigmoid(z)
    return s * (1.0 + z * (1.0 - s))


_GELU_K = math.sqrt(2.0 / math.pi)
_GELU_C = 0.044715


def _gelu(u):
    return 0.5 * u * (1.0 + jnp.tanh(_GELU_K * (u + _GELU_C * u * u * u)))


def _dgelu(u):
    t = jnp.tanh(_GELU_K * (u + _GELU_C * u * u * u))
    return 0.5 * (1.0 + t) + 0.5 * u * (1.0 - t * t) * _GELU_K * (1.0 + 3.0 * _GELU_C * u * u)


def _my_place():
    return lax.axis_index("x"), lax.axis_index("y"), lax.axis_index("c")


def _flip(v, bit):
    return 1 - v if bit else v


def _all_to_all(x, name):
    def body(x_ref, y_ref, send_sems, recv_sems, own_sem):
        mx, my, mc = _my_place()
        me = 4 * mx + 2 * my + mc
        own = pltpu.make_async_copy(x_ref.at[me], y_ref.at[me], own_sem)
        own.start()
        copies = []
        for k in range(1, N_DEV):
            px, py, pc = _flip(mx, (k >> 2) & 1), _flip(my, (k >> 1) & 1), _flip(mc, k & 1)
            peer = 4 * px + 2 * py + pc
            cp = pltpu.make_async_remote_copy(
                src_ref=x_ref.at[peer], dst_ref=y_ref.at[me],
                send_sem=send_sems.at[k - 1], recv_sem=recv_sems.at[k - 1],
                device_id=(px, py, pc), device_id_type=MESH_ID)
            cp.start()
            copies.append(cp)
        for cp in copies:
            cp.wait()
        own.wait()

    return pl.pallas_call(
        body, name=name,
        out_shape=jax.ShapeDtypeStruct(x.shape, x.dtype),
        in_specs=[HBM_SPEC], out_specs=HBM_SPEC,
        scratch_shapes=[pltpu.SemaphoreType.DMA((N_DEV - 1,)), pltpu.SemaphoreType.DMA((N_DEV - 1,)),
                        pltpu.SemaphoreType.DMA],
    )(x)


def _other_chips(mx, my):
    return [(1 - mx, my), (mx, 1 - my), (1 - mx, 1 - my)]


def _gather_weights(shards):
    n = len(shards)

    def body(*refs):
        w = refs[:n]
        g = refs[n:2 * n]
        send_sems, recv_sems, fsend_sems, frecv_sems, own_sems = refs[2 * n:]
        mx, my, mc = _my_place()
        j = 2 * mx + my
        chips = _other_chips(mx, my)
        own = []
        for t in range(n):
            for i in range(2):
                cp = pltpu.make_async_copy(w[t].at[i], g[t].at[i, j], own_sems.at[2 * t + i])
                cp.start()
                own.append(cp)
        first, passed = [], []
        for t in range(n):
            for q, (px, py) in enumerate(chips):
                cp = pltpu.make_async_remote_copy(
                    src_ref=w[t].at[mc], dst_ref=g[t].at[mc, j],
                    send_sem=send_sems.at[3 * t + q], recv_sem=recv_sems.at[3 * t + q],
                    device_id=(px, py, mc), device_id_type=MESH_ID)
                cp.start()
                first.append(cp)
        for t in range(n):
            for q, (px, py) in enumerate(chips):
                jq = 2 * px + py
                landed = pltpu.make_async_remote_copy(
                    src_ref=w[t].at[mc], dst_ref=g[t].at[mc, jq],
                    send_sem=send_sems.at[3 * t + q], recv_sem=recv_sems.at[3 * t + q],
                    device_id=(px, py, mc), device_id_type=MESH_ID)
                landed.wait_recv()
                fwd = pltpu.make_async_remote_copy(
                    src_ref=g[t].at[mc, jq], dst_ref=g[t].at[mc, jq],
                    send_sem=fsend_sems.at[3 * t + q], recv_sem=frecv_sems.at[3 * t + q],
                    device_id=(mx, my, 1 - mc), device_id_type=MESH_ID)
                fwd.start()
                passed.append(fwd)
        for t in range(n):
            for q, (px, py) in enumerate(chips):
                jq = 2 * px + py
                from_sibling = pltpu.make_async_remote_copy(
                    src_ref=g[t].at[1 - mc, jq], dst_ref=g[t].at[1 - mc, jq],
                    send_sem=fsend_sems.at[3 * t + q], recv_sem=frecv_sems.at[3 * t + q],
                    device_id=(mx, my, 1 - mc), device_id_type=MESH_ID)
                from_sibling.wait_recv()
        for cp in first + passed:
            cp.wait_send()
        for cp in own:
            cp.wait()

    out_shape = [jax.ShapeDtypeStruct((2, N_CHIPS) + s.shape[1:], s.dtype) for s in shards]
    return pl.pallas_call(
        body, name="gather_weights",
        out_shape=out_shape,
        in_specs=[HBM_SPEC] * n, out_specs=[HBM_SPEC] * n,
        scratch_shapes=[pltpu.SemaphoreType.DMA((3 * n,)), pltpu.SemaphoreType.DMA((3 * n,)),
                        pltpu.SemaphoreType.DMA((3 * n,)), pltpu.SemaphoreType.DMA((3 * n,)),
                        pltpu.SemaphoreType.DMA((2 * n,))],
    )(*shards)


def _swap_layers_with_sibling(grads):
    n = len(grads)

    def body(*refs):
        g = refs[:n]
        r = refs[n:2 * n]
        send_sems, recv_sems = refs[2 * n:]
        mx, my, mc = _my_place()
        copies = []
        for t in range(n):
            cp = pltpu.make_async_remote_copy(
                src_ref=g[t].at[1 - mc], dst_ref=r[t],
                send_sem=send_sems.at[t], recv_sem=recv_sems.at[t],
                device_id=(mx, my, 1 - mc), device_id_type=MESH_ID)
            cp.start()
            copies.append(cp)
        for cp in copies:
            cp.wait()

    return pl.pallas_call(
        body, name="grads_to_sibling",
        out_shape=[jax.ShapeDtypeStruct(a.shape[1:], a.dtype) for a in grads],
        in_specs=[HBM_SPEC] * n, out_specs=[HBM_SPEC] * n,
        scratch_shapes=[pltpu.SemaphoreType.DMA((n,)), pltpu.SemaphoreType.DMA((n,))],
    )(*grads)


def _scatter_chip_sums(sums):
    n = len(sums)

    def body(*refs):
        s = refs[:n]
        r = refs[n:2 * n]
        send_sems, recv_sems, own_sems = refs[2 * n:]
        mx, my, mc = _my_place()
        j = 2 * mx + my
        chips = _other_chips(mx, my)
        own, copies = [], []
        for t in range(n):
            cp = pltpu.make_async_copy(s[t].at[j], r[t].at[j], own_sems.at[t])
            cp.start()
            own.append(cp)
        for t in range(n):
            for q, (px, py) in enumerate(chips):
                jq = 2 * px + py
                cp = pltpu.make_async_remote_copy(
                    src_ref=s[t].at[jq], dst_ref=r[t].at[j],
                    send_sem=send_sems.at[3 * t + q], recv_sem=recv_sems.at[3 * t + q],
                    device_id=(px, py, mc), device_id_type=MESH_ID)
                cp.start()
                copies.append(cp)
        for cp in copies:
            cp.wait()
        for cp in own:
            cp.wait()

    return pl.pallas_call(
        body, name="chip_sums_to_owners",
        out_shape=[jax.ShapeDtypeStruct(a.shape, a.dtype) for a in sums],
        in_specs=[HBM_SPEC] * n, out_specs=[HBM_SPEC] * n,
        scratch_shapes=[pltpu.SemaphoreType.DMA((3 * n,)), pltpu.SemaphoreType.DMA((3 * n,)),
                        pltpu.SemaphoreType.DMA((n,))],
    )(*sums)


def _share_layer_with_sibling(halves):
    n = len(halves)

    def body(*refs):
        h = refs[:n]
        f = refs[n:2 * n]
        send_sems, recv_sems, own_sems = refs[2 * n:]
        mx, my, mc = _my_place()
        own, copies = [], []
        for t in range(n):
            cp = pltpu.make_async_copy(h[t], f[t].at[mc], own_sems.at[t])
            cp.start()
            own.append(cp)
            rc = pltpu.make_async_remote_copy(
                src_ref=h[t], dst_ref=f[t].at[mc],
                send_sem=send_sems.at[t], recv_sem=recv_sems.at[t],
                device_id=(mx, my, 1 - mc), device_id_type=MESH_ID)
            rc.start()
            copies.append(rc)
        for cp in copies:
            cp.wait()
        for cp in own:
            cp.wait()

    return pl.pallas_call(
        body, name="reduced_grads_to_sibling",
        out_shape=[jax.ShapeDtypeStruct((2,) + a.shape, a.dtype) for a in halves],
        in_specs=[HBM_SPEC] * n, out_specs=[HBM_SPEC] * n,
        scratch_shapes=[pltpu.SemaphoreType.DMA((n,)), pltpu.SemaphoreType.DMA((n,)),
                        pltpu.SemaphoreType.DMA((n,))],
    )(*halves)


def _matmul(name, operands, in_specs, grid, nk, dims, out_shape, out_specs, acc_shape,
            epilogue, a_prologue=None, aliases=None):
    n_in = len(operands)

    def body(*refs):
        a_ref, b_ref = refs[0], refs[1]
        extra = refs[2:n_in]
        outs = refs[n_in:-1] if nk > 1 else refs[n_in:]
        a = a_ref[...]
        if a_prologue is not None:
            a = a_prologue(a)
        part = lax.dot_general(a.astype(BF16), b_ref[...].astype(BF16), dims, preferred_element_type=F32)
        if nk == 1:
            epilogue(part, extra, outs)
        else:
            acc = refs[-1]
            k = pl.program_id(2)

            @pl.when(k == 0)
            def _():
                acc[...] = part

            @pl.when(k > 0)
            def _():
                acc[...] += part

            @pl.when(k == nk - 1)
            def _():
                epilogue(acc[...], extra, outs)

    return pl.pallas_call(
        body, name=name, grid=grid, in_specs=in_specs, out_specs=out_specs, out_shape=out_shape,
        scratch_shapes=[pltpu.VMEM(acc_shape, F32)] if nk > 1 else [],
        input_output_aliases=aliases or {},
        compiler_params=_params(3, parallel=2),
    )(*operands)


def _store_cast(acc, extra, outs):
    outs[0][...] = acc.astype(outs[0].dtype)


def _in_proj(h, w_g, layer):
    s, d = h.shape
    nl = w_g.shape[-1]
    tm = _tile(s, (1024, 512, 256))
    tn = _tile(nl, (1024, 768, 512, 384, 256, 128))
    per = nl // tn
    return _matmul(
        "in_proj", (h, w_g),
        [pl.BlockSpec((tm, d), lambda i, j, k: (i, 0)),
         pl.BlockSpec((None, None, d, tn), lambda i, j, k: (layer, j // per, 0, j % per))],
        (s // tm, N_CHIPS * per, 1), 1, NN,
        jax.ShapeDtypeStruct((s, N_CHIPS * nl), BF16),
        pl.BlockSpec((tm, tn), lambda i, j, k: (i, j)), (tm, tn), _store_cast)


def _out_proj_residual(y, w2, layer, x, gate):
    s, wdt = y.shape
    d = w2.shape[-1]
    tm = _tile(s, (1024, 512, 256))
    tn = _tile(d, (1024, 512, 256, 128))

    def epilogue(acc, extra, outs):
        x_ref, gate_ref = extra
        outs[0][...] = x_ref[...] + gate_ref[...] * acc
        outs[1][...] = acc.astype(BF16)

    return _matmul(
        "out_proj", (y, w2, x, gate),
        [pl.BlockSpec((tm, wdt), lambda i, j, k: (i, 0)),
         pl.BlockSpec((None, wdt, tn), lambda i, j, k: (layer, 0, j)),
         pl.BlockSpec((tm, tn), lambda i, j, k: (i, j)),
         pl.BlockSpec((1, tn), lambda i, j, k: (0, j))],
        (s // tm, d // tn, 1), 1, NN,
        [jax.ShapeDtypeStruct((s, d), F32), jax.ShapeDtypeStruct((s, d), BF16)],
        [pl.BlockSpec((tm, tn), lambda i, j, k: (i, j)), pl.BlockSpec((tm, tn), lambda i, j, k: (i, j))],
        (tm, tn), epilogue)


def _out_proj_bwd_act(dout, w2, layer):
    s, d = dout.shape
    wdt = w2.shape[1]
    tm = _tile(s, (1024, 512, 256))
    tn = _tile(wdt, (1024, 512, 256, 128))
    return _matmul(
        "out_proj_dy", (dout, w2),
        [pl.BlockSpec((tm, d), lambda i, j, k: (i, 0)),
         pl.BlockSpec((None, tn, d), lambda i, j, k: (layer, j, 0))],
        (s // tm, wdt // tn, 1), 1, NT,
        jax.ShapeDtypeStruct((s, wdt), BF16),
        pl.BlockSpec((tm, tn), lambda i, j, k: (i, j)), (tm, tn), _store_cast)


def _out_proj_bwd_w(y, dout, buf, layer):
    s, wdt = y.shape
    d = dout.shape[1]
    tm = _tile(wdt, (1024, 512, 256, 128))
    tn = _tile(d, (1024, 512, 256, 128))
    ts = _tile(s, (1024, 512, 256))
    nk = s // ts

    def epilogue(acc, extra, outs):
        outs[0][...] = acc.astype(BF16)

    return _matmul(
        "out_proj_dw", (y, dout, buf),
        [pl.BlockSpec((ts, tm), lambda i, j, k: (k, i)),
         pl.BlockSpec((ts, tn), lambda i, j, k: (k, j)),
         HBM_SPEC],
        (wdt // tm, d // tn, nk), nk, TN,
        jax.ShapeDtypeStruct(buf.shape, buf.dtype),
        pl.BlockSpec((None, tm, tn), lambda i, j, k: (layer, i, j)), (tm, tn), epilogue,
        aliases={2: 0})


def _in_proj_bwd_act(dproj, w_g, layer):
    s, n_all = dproj.shape
    d, nl = w_g.shape[2], w_g.shape[3]
    tm = _tile(s, (1024, 512, 256))
    tn = _tile(d, (1024, 512, 256, 128))
    tc = _tile(nl, (1024, 768, 512, 384, 256, 128))
    per = nl // tc
    nk = N_CHIPS * per
    return _matmul(
        "in_proj_dh", (dproj, w_g),
        [pl.BlockSpec((tm, tc), lambda i, j, k: (i, k)),
         pl.BlockSpec((None, None, tn, tc), lambda i, j, k: (layer, k // per, j, k % per))],
        (s // tm, d // tn, nk), nk, NT,
        jax.ShapeDtypeStruct((s, d), BF16),
        pl.BlockSpec((tm, tn), lambda i, j, k: (i, j)), (tm, tn), _store_cast)


def _in_proj_bwd_w(h, dproj, buf, layer):
    s, d = h.shape
    nl = buf.shape[-1]
    tm = _tile(d, (1024, 512, 256, 128))
    tn = _tile(nl, (1024, 768, 512, 384, 256, 128))
    ts = _tile(s, (1024, 512, 256))
    per = nl // tn
    nk = s // ts

    def epilogue(acc, extra, outs):
        outs[0][...] = acc.astype(BF16)

    return _matmul(
        "in_proj_dw", (h, dproj, buf),
        [pl.BlockSpec((ts, tm), lambda i, j, k: (k, i)),
         pl.BlockSpec((ts, tn), lambda i, j, k: (k, j)),
         HBM_SPEC],
        (d // tm, N_CHIPS * per, nk), nk, TN,
        jax.ShapeDtypeStruct(buf.shape, buf.dtype),
        pl.BlockSpec((None, None, tm, tn), lambda i, j, k: (layer, j // per, i, j % per)), (tm, tn), epilogue,
        aliases={2: 0})


def _mod_fwd(c_all, w_mod, bias, layer):
    nb, d = c_all.shape
    nl = w_mod.shape[-1]
    tn = _tile(nl, (768, 512, 384, 256, 128))
    tk = _tile(d, (1024, 512, 256, 128))
    nk = d // tk

    def epilogue(acc, extra, outs):
        outs[0][...] = acc + extra[0][...]

    return _matmul(
        "mod_fwd", (c_all, w_mod, bias),
        [pl.BlockSpec((nb, tk), lambda i, j, k: (0, k)),
         pl.BlockSpec((None, tk, tn), lambda i, j, k: (layer, k, j)),
         pl.BlockSpec((1, tn), lambda i, j, k: (0, j))],
        (1, nl // tn, nk), nk, NN,
        jax.ShapeDtypeStruct((nb, nl), F32),
        pl.BlockSpec((nb, tn), lambda i, j, k: (0, j)), (nb, tn), epilogue, a_prologue=_silu)


def _mod_bwd_w(c_all, dm_local):
    nb, d = c_all.shape
    nl = dm_local.shape[-1]
    tm = _tile(d, (1024, 512, 256, 128))
    tn = _tile(nl, (768, 512, 384, 256, 128))

    def epilogue(acc, extra, outs):
        outs[0][...] = acc

    return _matmul(
        "mod_dw", (c_all, dm_local),
        [pl.BlockSpec((nb, tm), lambda i, j, k: (0, i)),
         pl.BlockSpec((nb, tn), lambda i, j, k: (0, j))],
        (d // tm, nl // tn, 1), 1, TN,
        jax.ShapeDtypeStruct((d, nl), F32),
        pl.BlockSpec((tm, tn), lambda i, j, k: (i, j)), (tm, tn), epilogue, a_prologue=_silu)


def _rows_call(name, body, operands, in_specs, out_shape, out_specs, n_tiles):
    return pl.pallas_call(
        body, name=name, grid=(n_tiles,), in_specs=in_specs, out_specs=out_specs, out_shape=out_shape,
        compiler_params=_params(1),
    )(*operands)


def _row_spec(tr, width):
    return pl.BlockSpec((tr, width), lambda i: (i, 0))


def _vec_spec(width):
    return pl.BlockSpec((1, width), lambda i: (0, 0))


def _accumulate(ref, val):
    first = pl.program_id(0) == 0

    @pl.when(first)
    def _():
        ref[...] = val

    @pl.when(jnp.logical_not(first))
    def _():
        ref[...] += val


def _prenorm(x, g, scale, shift):
    s, d = x.shape
    tr = _tile(s, (256, 128))

    def body(x_ref, g_ref, sc_ref, sh_ref, h_ref):
        xv = x_ref[...]
        rstd = lax.rsqrt(jnp.mean(xv * xv, axis=-1, keepdims=True) + EPS)
        h_ref[...] = ((xv * rstd) * g_ref[...] * (1.0 + sc_ref[...]) + sh_ref[...]).astype(BF16)

    return _rows_call("prenorm", body, (x, g, scale, shift),
                      [_row_spec(tr, d), _vec_spec(d), _vec_spec(d), _vec_spec(d)],
                      jax.ShapeDtypeStruct((s, d), BF16), _row_spec(tr, d), s // tr)


def _prenorm_bwd(x, dh, dres, g, scale):
    s, d = x.shape
    tr = _tile(s, (256, 128))

    def body(x_ref, dh_ref, dres_ref, g_ref, sc_ref, dx_ref, dshift_ref, dscale_ref, dg_ref):
        xv = x_ref[...]
        dhv = dh_ref[...].astype(F32)
        rstd = lax.rsqrt(jnp.mean(xv * xv, axis=-1, keepdims=True) + EPS)
        xhat = xv * rstd
        gv = g_ref[...]
        one_sc = 1.0 + sc_ref[...]
        dxhat = dhv * gv * one_sc
        dx_ref[...] = dres_ref[...] + rstd * (dxhat - xhat * jnp.mean(dxhat * xhat, axis=-1, keepdims=True))
        _accumulate(dshift_ref, jnp.sum(dhv, axis=0, keepdims=True))
        _accumulate(dscale_ref, jnp.sum(dhv * xhat * gv, axis=0, keepdims=True))
        _accumulate(dg_ref, jnp.sum(dhv * xhat * one_sc, axis=0, keepdims=True))

    vec = jax.ShapeDtypeStruct((1, d), F32)
    return _rows_call("prenorm_bwd", body, (x, dh, dres, g, scale),
                      [_row_spec(tr, d), _row_spec(tr, d), _row_spec(tr, d), _vec_spec(d), _vec_spec(d)],
                      [jax.ShapeDtypeStruct((s, d), F32), vec, vec, vec],
                      [_row_spec(tr, d), _vec_spec(d), _vec_spec(d), _vec_spec(d)], s // tr)


def _gate_bwd(dx, out, gate):
    s, d = dx.shape
    tr = _tile(s, (256, 128))

    def body(dx_ref, out_ref, gate_ref, dout_ref, dgate_ref):
        dxv = dx_ref[...]
        dout_ref[...] = (gate_ref[...] * dxv).astype(BF16)
        _accumulate(dgate_ref, jnp.sum(dxv * out_ref[...].astype(F32), axis=0, keepdims=True))

    return _rows_call("gate_bwd", body, (dx, out, gate),
                      [_row_spec(tr, d), _row_spec(tr, d), _vec_spec(d)],
                      [jax.ShapeDtypeStruct((s, d), BF16), jax.ShapeDtypeStruct((1, d), F32)],
                      [_row_spec(tr, d), _vec_spec(d)], s // tr)


def _final_loss(x, target, g):
    s, d = x.shape
    tr = _tile(s, (256, 128))
    n_tiles = s // tr

    def body(x_ref, t_ref, g_ref, loss_ref, dx_ref, dg_ref, acc_ref):
        xv = x_ref[...]
        rstd = lax.rsqrt(jnp.mean(xv * xv, axis=-1, keepdims=True) + EPS)
        xhat = xv * rstd
        gv = g_ref[...]
        err = xhat * gv - t_ref[...]
        dy = err * (1.0 / d)
        dxhat = dy * gv
        dx_ref[...] = rstd * (dxhat - xhat * jnp.mean(dxhat * xhat, axis=-1, keepdims=True))
        _accumulate(dg_ref, jnp.sum(dy * xhat, axis=0, keepdims=True))
        _accumulate(acc_ref, jnp.sum(err * err, axis=0, keepdims=True))

        @pl.when(pl.program_id(0) == n_tiles - 1)
        def _():
            loss_ref[...] = (0.5 / d) * jnp.sum(acc_ref[...], axis=1, keepdims=True)

    return pl.pallas_call(
        body, name="final_loss", grid=(n_tiles,),
        in_specs=[_row_spec(tr, d), _row_spec(tr, d), _vec_spec(d)],
        out_specs=[pl.BlockSpec((1, 1), lambda i: (0, 0)), _row_spec(tr, d), _vec_spec(d)],
        out_shape=[jax.ShapeDtypeStruct((1, 1), F32), jax.ShapeDtypeStruct((s, d), F32),
                   jax.ShapeDtypeStruct((1, d), F32)],
        scratch_shapes=[pltpu.VMEM((1, d), F32)],
        compiler_params=_params(1),
    )(x, target, g)


def _rope(t, cos, sin):
    return t * cos + pltpu.roll(t, HEAD_DIM // 2, axis=1) * sin


def _unrope(dt, cos, sin):
    return dt * cos + pltpu.roll(dt * sin, HEAD_DIM // 2, axis=1)


def _band_window(b, sub):
    q0 = pl.multiple_of(b * Q_BLOCK, Q_BLOCK)
    start = pl.multiple_of(jnp.clip(b * Q_BLOCK - RADIUS, 0, sub - K_WINDOW), RADIUS)
    qi = q0 + lax.broadcasted_iota(jnp.int32, (Q_BLOCK, K_WINDOW), 0)
    ki = start + lax.broadcasted_iota(jnp.int32, (Q_BLOCK, K_WINDOW), 1)
    return q0, start, jnp.abs(qi - ki) <= RADIUS


def _attn_fwd(qkv, cos, sin):
    h3, dil, sub, _ = qkv.shape
    nh = h3 // 3
    nb = sub // Q_BLOCK
    scale = HEAD_DIM ** -0.5

    def body(q_ref, k_ref, v_ref, cos_ref, sin_ref, o_ref, lse_ref, qs, ks):
        cosv, sinv = cos_ref[...], sin_ref[...]
        qs[...] = _rope(q_ref[...].astype(F32), cosv, sinv).astype(BF16)
        ks[...] = _rope(k_ref[...].astype(F32), cosv, sinv).astype(BF16)

        def block(b, carry):
            q0, start, valid = _band_window(b, sub)
            q = qs[pl.ds(q0, Q_BLOCK), :]
            kk = ks[pl.ds(start, K_WINDOW), :]
            vv = v_ref[pl.ds(start, K_WINDOW), :]
            sc = lax.dot_general(q, kk, NT, preferred_element_type=F32) * scale
            sc = jnp.where(valid, sc, NEG_INF)
            m = jnp.max(sc, axis=1, keepdims=True)
            p = jnp.exp(sc - m)
            den = jnp.sum(p, axis=1, keepdims=True)
            o = lax.dot_general(p.astype(BF16), vv, NN, preferred_element_type=F32) / den
            o_ref[pl.ds(q0, Q_BLOCK), :] = o
            lse_ref[pl.ds(q0, Q_BLOCK), :] = jnp.broadcast_to(m + jnp.log(den), (Q_BLOCK, HEAD_DIM))
            return carry

        lax.fori_loop(0, nb, block, 0)

    seg = (None, None, sub, HEAD_DIM)
    out = jax.ShapeDtypeStruct((nh, dil, sub, HEAD_DIM), F32)
    return pl.pallas_call(
        body, name="attn_fwd", grid=(nh, dil),
        in_specs=[pl.BlockSpec(seg, lambda h, r: (h, r, 0, 0)),
                  pl.BlockSpec(seg, lambda h, r: (nh + h, r, 0, 0)),
                  pl.BlockSpec(seg, lambda h, r: (2 * nh + h, r, 0, 0)),
                  pl.BlockSpec((None, sub, HEAD_DIM), lambda h, r: (r, 0, 0)),
                  pl.BlockSpec((None, sub, HEAD_DIM), lambda h, r: (r, 0, 0))],
        out_specs=[pl.BlockSpec(seg, lambda h, r: (h, r, 0, 0)), pl.BlockSpec(seg, lambda h, r: (h, r, 0, 0))],
        out_shape=[out, out],
        scratch_shapes=[pltpu.VMEM((sub, HEAD_DIM), BF16), pltpu.VMEM((sub, HEAD_DIM), BF16)],
        compiler_params=_params(2, parallel=2),
    )(qkv, qkv, qkv, cos, sin)


def _attn_bwd(qkv, cos, sin, do, attn, lse):
    h3, dil, sub, _ = qkv.shape
    nh = h3 // 3
    nb = sub // Q_BLOCK
    scale = HEAD_DIM ** -0.5

    def body(q_ref, k_ref, v_ref, cos_ref, sin_ref, do_ref, attn_ref, lse_ref,
             dq_ref, dk_ref, dv_ref, qs, ks, dqs, dks, dvs):
        cosv, sinv = cos_ref[...], sin_ref[...]
        qs[...] = _rope(q_ref[...].astype(F32), cosv, sinv).astype(BF16)
        ks[...] = _rope(k_ref[...].astype(F32), cosv, sinv).astype(BF16)
        dks[...] = jnp.zeros_like(dks)
        dvs[...] = jnp.zeros_like(dvs)

        def block(b, carry):
            q0, start, valid = _band_window(b, sub)
            q = qs[pl.ds(q0, Q_BLOCK), :]
            kk = ks[pl.ds(start, K_WINDOW), :]
            vv = v_ref[pl.ds(start, K_WINDOW), :]
            dov = do_ref[pl.ds(q0, Q_BLOCK), :]
            delta = jnp.sum(dov.astype(F32) * attn_ref[pl.ds(q0, Q_BLOCK), :].astype(F32), axis=1, keepdims=True)
            lse_q = lse_ref[pl.ds(q0, Q_BLOCK), :][:, 0:1]
            sc = lax.dot_general(q, kk, NT, preferred_element_type=F32) * scale
            p = jnp.where(valid, jnp.exp(sc - lse_q), 0.0)
            dp = lax.dot_general(dov, vv, NT, preferred_element_type=F32)
            ds = (p * (dp - delta) * scale).astype(BF16)
            dqs[pl.ds(q0, Q_BLOCK), :] = lax.dot_general(ds, kk, NN, preferred_element_type=F32)
            dks[pl.ds(start, K_WINDOW), :] += lax.dot_general(ds, q, TN, preferred_element_type=F32)
            dvs[pl.ds(start, K_WINDOW), :] += lax.dot_general(p.astype(BF16), dov, TN, preferred_element_type=F32)
            return carry

        lax.fori_loop(0, nb, block, 0)
        dq_ref[...] = _unrope(dqs[...], cosv, sinv).astype(BF16)
        dk_ref[...] = _unrope(dks[...], cosv, sinv).astype(BF16)
        dv_ref[...] = dvs[...].astype(BF16)

    seg = (None, None, sub, HEAD_DIM)
    own = pl.BlockSpec(seg, lambda h, r: (h, r, 0, 0))
    tab = pl.BlockSpec((None, sub, HEAD_DIM), lambda h, r: (r, 0, 0))
    out = jax.ShapeDtypeStruct((nh, dil, sub, HEAD_DIM), BF16)
    return pl.pallas_call(
        body, name="attn_bwd", grid=(nh, dil),
        in_specs=[own,
                  pl.BlockSpec(seg, lambda h, r: (nh + h, r, 0, 0)),
                  pl.BlockSpec(seg, lambda h, r: (2 * nh + h, r, 0, 0)),
                  tab, tab, own, own, own],
        out_specs=[own, own, own],
        out_shape=[out, out, out],
        scratch_shapes=[pltpu.VMEM((sub, HEAD_DIM), BF16), pltpu.VMEM((sub, HEAD_DIM), BF16),
                        pltpu.VMEM((sub, HEAD_DIM), F32), pltpu.VMEM((sub, HEAD_DIM), F32),
                        pltpu.VMEM((sub, HEAD_DIM), F32)],
        compiler_params=_params(2, parallel=2),
    )(qkv, qkv, qkv, cos, sin, do, attn, lse)


def _attn_combine(outs, lses):
    nh, s, _ = outs[0].shape
    tr = _tile(s, (512, 256, 128))
    n_pat = len(outs)

    def body(*refs):
        o = refs[:n_pat]
        l = refs[n_pat:2 * n_pat]
        attn_ref, lse_ref = refs[2 * n_pat:]
        lv = [r[...] for r in l]
        m = functools.reduce(jnp.maximum, lv)
        ws = [jnp.exp(v - m) for v in lv]
        tot = functools.reduce(lambda a, b: a + b, ws)
        acc = functools.reduce(lambda a, b: a + b, [w * r[...] for w, r in zip(ws, o)])
        attn_ref[...] = (acc / tot).astype(BF16)
        lse_ref[...] = m + jnp.log(tot)

    spec = pl.BlockSpec((None, tr, HEAD_DIM), lambda h, i: (h, i, 0))
    return pl.pallas_call(
        body, name="attn_combine", grid=(nh, s // tr),
        in_specs=[spec] * (2 * n_pat), out_specs=[spec, spec],
        out_shape=[jax.ShapeDtypeStruct((nh, s, HEAD_DIM), BF16), jax.ShapeDtypeStruct((nh, s, HEAD_DIM), F32)],
        compiler_params=_params(2, parallel=2),
    )(*outs, *lses)


def _rope_tables(s):
    half = HEAD_DIM // 2
    inv = ROPE_THETA ** (-jnp.arange(half, dtype=F32) / half)
    ang = jnp.arange(s, dtype=F32)[:, None] * inv[None, :]
    cos, sin = jnp.cos(ang), jnp.sin(ang)
    return jnp.concatenate([cos, cos], axis=-1), jnp.concatenate([-sin, sin], axis=-1)


def _to_pattern(t, dil):
    n, s, w = t.shape
    return t.reshape(n, s // dil, dil, w).transpose(0, 2, 1, 3)


def _from_pattern(t):
    n, dil, sub, w = t.shape
    return t.transpose(0, 2, 1, 3).reshape(n, dil * sub, w)


def _heads_major(t):
    s, w = t.shape
    return t.reshape(s, w // HEAD_DIM, HEAD_DIM).transpose(1, 0, 2)


def _heads_minor(t):
    n, s, w = t.shape
    return t.transpose(1, 0, 2).reshape(s, n * w)


def _conv_chunks(s):
    for k in range(s // CONV_ROWS):
        lo = max(0, k * CONV_ROWS - CONV_HALO)
        hi = min(s, (k + 1) * CONV_ROWS + CONV_HALO)
        yield k * CONV_ROWS, lo, hi


def _neighbours(p, lo, s):
    n = p.shape[0]
    row = lo + lax.broadcasted_iota(jnp.int32, p.shape, 0)
    prev = jnp.where(row == 0, 0.0, pltpu.roll(p, 1, axis=0))
    nxt = jnp.where(row == s - 1, 0.0, pltpu.roll(p, n - 1, axis=0))
    return prev, nxt


def _ab_mix(attn, proj, conv_w, aw):
    s = proj.shape[0]
    nt = aw // LANES

    def col(group, sel):
        return pl.BlockSpec((s, LANES), lambda i: (0, group * nt + sel(i)))

    a_sel = lambda i: jnp.minimum(i, nt - 1)
    b_sel = lambda i: jnp.maximum(i - nt, 0)

    def body(attn_ref, za_ref, ub_ref, gb_ref, gc_ref, zb_ref, w_ref, y_ref):
        i = pl.program_id(0)

        @pl.when(i < nt)
        def _():
            y_ref[...] = (attn_ref[...].astype(F32) * _silu(za_ref[...].astype(F32))).astype(BF16)

        @pl.when(i >= nt)
        def _():
            w = w_ref[...]
            for c0, lo, hi in _conv_chunks(s):
                p = gc_ref[lo:hi, :].astype(F32) * ub_ref[lo:hi, :].astype(F32)
                prev, nxt = _neighbours(p, lo, s)
                cv = w[0:1, :] * prev + w[1:2, :] * p + w[2:3, :] * nxt
                yb = gb_ref[lo:hi, :].astype(F32) * cv * _silu(zb_ref[lo:hi, :].astype(F32))
                y_ref[c0:c0 + CONV_ROWS, :] = yb[c0 - lo:c0 - lo + CONV_ROWS, :].astype(BF16)

    return pl.pallas_call(
        body, name="ab_mix", grid=(2 * nt,),
        in_specs=[pl.BlockSpec((s, LANES), lambda i: (0, a_sel(i))),
                  col(3, a_sel), col(4, b_sel), col(5, b_sel), col(6, b_sel), col(7, b_sel),
                  pl.BlockSpec((3, LANES), lambda i: (0, b_sel(i)))],
        out_specs=pl.BlockSpec((s, LANES), lambda i: (0, i)),
        out_shape=jax.ShapeDtypeStruct((s, 2 * aw), BF16),
        compiler_params=_params(1),
    )(attn, proj, proj, proj, proj, proj, conv_w)


def _ab_dattn(dy, proj, aw):
    s = proj.shape[0]
    tr = _tile(s, (512, 256, 128))

    def body(dy_ref, za_ref, o_ref):
        o_ref[...] = (dy_ref[...].astype(F32) * _silu(za_ref[...].astype(F32))).astype(BF16)

    return pl.pallas_call(
        body, name="ab_dattn", grid=(s // tr,),
        in_specs=[pl.BlockSpec((tr, aw), lambda i: (i, 0)), pl.BlockSpec((tr, aw), lambda i: (i, 3))],
        out_specs=pl.BlockSpec((tr, aw), lambda i: (i, 0)),
        out_shape=jax.ShapeDtypeStruct((s, aw), BF16),
        compiler_params=_params(1, parallel=1),
    )(dy, proj)


def _ab_dproj(dqkv, dy, attn, proj, conv_w, aw):
    s = proj.shape[0]
    nt = aw // LANES
    n_pat = len(dqkv)

    def col(group, sel):
        return pl.BlockSpec((s, LANES), lambda i: (0, group * nt + sel(i)))

    qkv_sel = lambda i: jnp.minimum(i, 3 * nt - 1)
    a_sel = lambda i: jnp.clip(i - 3 * nt, 0, nt - 1)
    b_sel = lambda i: jnp.maximum(i - 4 * nt, 0) % nt
    w_sel = lambda i: jnp.clip(i - 4 * nt, 0, nt - 1)

    def body(*refs):
        g_refs = refs[:n_pat]
        dya_ref, attn_ref, za_ref, dyb_ref, ub_ref, gb_ref, gc_ref, zb_ref, w_ref, out_ref, dw_ref = refs[n_pat:]
        i = pl.program_id(0)

        @pl.when(i < 3 * nt)
        def _():
            acc = g_refs[0][...].astype(F32)
            for r in g_refs[1:]:
                acc = acc + r[...].astype(F32)
            out_ref[...] = acc.astype(BF16)

        @pl.when(jnp.logical_and(i >= 3 * nt, i < 4 * nt))
        def _():
            out_ref[...] = (dya_ref[...].astype(F32) * attn_ref[...].astype(F32)
                            * _dsilu(za_ref[...].astype(F32))).astype(BF16)

        for which in range(4):
            @pl.when(jnp.logical_and(i >= (4 + which) * nt, i < (5 + which) * nt))
            def _(which=which):
                w = w_ref[...]
                dw = [jnp.zeros((1, LANES), F32) for _ in range(3)]
                for c0, lo, hi in _conv_chunks(s):
                    ctr = slice(c0 - lo, c0 - lo + CONV_ROWS)
                    ub = ub_ref[lo:hi, :].astype(F32)
                    gc = gc_ref[lo:hi, :].astype(F32)
                    gb = gb_ref[lo:hi, :].astype(F32)
                    zb = zb_ref[lo:hi, :].astype(F32)
                    dyb = dyb_ref[lo:hi, :].astype(F32)
                    p = gc * ub
                    prev, nxt = _neighbours(p, lo, s)
                    if which == 1:
                        cv = w[0:1, :] * prev + w[1:2, :] * p + w[2:3, :] * nxt
                        res = dyb * cv * _silu(zb)
                    elif which == 3:
                        cv = w[0:1, :] * prev + w[1:2, :] * p + w[2:3, :] * nxt
                        res = dyb * gb * cv * _dsilu(zb)
                    else:
                        dcv = dyb * gb * _silu(zb)
                        dprev, dnxt = _neighbours(dcv, lo, s)
                        dp = w[0:1, :] * dnxt + w[1:2, :] * dcv + w[2:3, :] * dprev
                        res = dp * (gc if which == 0 else ub)
                        if which == 0:
                            for t, nb in enumerate((prev, p, nxt)):
                                dw[t] = dw[t] + jnp.sum((dcv * nb)[ctr, :], axis=0, keepdims=True)
                    out_ref[c0:c0 + CONV_ROWS, :] = res[ctr, :].astype(BF16)
                if which == 0:
                    dw_ref[...] = jnp.concatenate(dw, axis=0)

    return pl.pallas_call(
        body, name="ab_dproj", grid=(8 * nt,),
        in_specs=[pl.BlockSpec((s, LANES), lambda i: (0, qkv_sel(i)))] * n_pat + [
            pl.BlockSpec((s, LANES), lambda i: (0, a_sel(i))),
            pl.BlockSpec((s, LANES), lambda i: (0, a_sel(i))),
            col(3, a_sel),
            pl.BlockSpec((s, LANES), lambda i: (0, nt + b_sel(i))),
            col(4, b_sel), col(5, b_sel), col(6, b_sel), col(7, b_sel),
            pl.BlockSpec((3, LANES), lambda i: (0, b_sel(i)))],
        out_specs=[pl.BlockSpec((s, LANES), lambda i: (0, i)),
                   pl.BlockSpec((3, LANES), lambda i: (0, w_sel(i)))],
        out_shape=[jax.ShapeDtypeStruct((s, 8 * aw), BF16), jax.ShapeDtypeStruct((3, aw), F32)],
        compiler_params=_params(1),
    )(*dqkv, dy, attn, proj, dy, proj, proj, proj, proj, conv_w)


def _sgu_norm(v, ln_g, ln_b):
    gv = _gelu(v)
    mu = jnp.mean(gv, axis=-1, keepdims=True)
    xc = gv - mu
    rstd = lax.rsqrt(jnp.mean(xc * xc, axis=-1, keepdims=True) + EPS)
    vhat = xc * rstd
    return vhat, rstd, vhat * ln_g + ln_b


def _sgu_fwd(uvz, ln_g, ln_b, w_s, b_s, cw):
    s = uvz.shape[0]
    tr = 2 * CHUNK if s % (2 * CHUNK) == 0 else CHUNK
    gw = cw // N_GROUPS

    def body(u_ref, v_ref, z_ref, g_ref, b_ref, ws_ref, bs_ref, y_ref):
        _, _, vn = _sgu_norm(v_ref[...].astype(F32), g_ref[...], b_ref[...])
        vn = vn.astype(BF16)
        for ch in range(tr // CHUNK):
            rows = slice(ch * CHUNK, (ch + 1) * CHUNK)
            for grp in range(N_GROUPS):
                cols = slice(grp * gw, (grp + 1) * gw)
                mixed = lax.dot_general(ws_ref[grp], vn[rows, cols], NN, preferred_element_type=F32) + bs_ref[grp]
                y_ref[rows, cols] = (_gelu(u_ref[rows, cols].astype(F32)) * mixed
                                     * _silu(z_ref[rows, cols].astype(F32))).astype(BF16)

    full3 = lambda shape: pl.BlockSpec(shape, lambda i: (0, 0, 0))
    return pl.pallas_call(
        body, name="sgu_fwd", grid=(s // tr,),
        in_specs=[pl.BlockSpec((tr, cw), lambda i: (i, 0)), pl.BlockSpec((tr, cw), lambda i: (i, 1)),
                  pl.BlockSpec((tr, cw), lambda i: (i, 2)), _vec_spec(cw), _vec_spec(cw),
                  full3(w_s.shape), full3(b_s.shape)],
        out_specs=pl.BlockSpec((tr, cw), lambda i: (i, 0)),
        out_shape=jax.ShapeDtypeStruct((s, cw), BF16),
        compiler_params=_params(1, parallel=1),
    )(uvz, uvz, uvz, ln_g, ln_b, w_s, b_s)


def _sgu_bwd(uvz, dy, ln_g, ln_b, w_s, b_s, cw):
    s = uvz.shape[0]
    tr = 2 * CHUNK if s % (2 * CHUNK) == 0 else CHUNK
    gw = cw // N_GROUPS

    def body(u_ref, v_ref, z_ref, dy_ref, g_ref, b_ref, ws_ref, bs_ref,
             du_ref, dv_ref, dz_ref, dws_ref, dbs_ref, dg_ref, db_ref, dvn_ref):
        vv = v_ref[...].astype(F32)
        gvec = g_ref[...]
        vhat, rstd, vn = _sgu_norm(vv, gvec, b_ref[...])
        vn = vn.astype(BF16)
        first = pl.program_id(0) == 0

        @pl.when(first)
        def _():
            dws_ref[...] = jnp.zeros_like(dws_ref)
            dbs_ref[...] = jnp.zeros_like(dbs_ref)

        for ch in range(tr // CHUNK):
            rows = slice(ch * CHUNK, (ch + 1) * CHUNK)
            for grp in range(N_GROUPS):
                cols = slice(grp * gw, (grp + 1) * gw)
                vn_g = vn[rows, cols]
                mixed = lax.dot_general(ws_ref[grp], vn_g, NN, preferred_element_type=F32) + bs_ref[grp]
                uu = u_ref[rows, cols].astype(F32)
                zz = z_ref[rows, cols].astype(F32)
                dyv = dy_ref[rows, cols].astype(F32)
                gu, sz = _gelu(uu), _silu(zz)
                du_ref[rows, cols] = (dyv * mixed * sz * _dgelu(uu)).astype(BF16)
                dz_ref[rows, cols] = (dyv * gu * mixed * _dsilu(zz)).astype(BF16)
                dmixed = dyv * gu * sz
                dm16 = dmixed.astype(BF16)
                dws_ref[grp] += lax.dot_general(dm16, vn_g, NT, preferred_element_type=F32)
                dbs_ref[grp] += jnp.broadcast_to(jnp.sum(dmixed, axis=1, keepdims=True), (CHUNK, LANES))
                dvn_ref[rows, cols] = lax.dot_general(ws_ref[grp], dm16, TN, preferred_element_type=F32)

        dvn = dvn_ref[...]
        _accumulate(dg_ref, jnp.sum(dvn * vhat, axis=0, keepdims=True))
        _accumulate(db_ref, jnp.sum(dvn, axis=0, keepdims=True))
        dvhat = dvn * gvec
        dgv = rstd * (dvhat - jnp.mean(dvhat, axis=-1, keepdims=True)
                      - vhat * jnp.mean(dvhat * vhat, axis=-1, keepdims=True))
        dv_ref[...] = (dgv * _dgelu(vv)).astype(BF16)

    full3 = lambda shape: pl.BlockSpec(shape, lambda i: (0, 0, 0))
    acc3 = jax.ShapeDtypeStruct((N_GROUPS, CHUNK, LANES), F32)
    vec = jax.ShapeDtypeStruct((1, cw), F32)
    act = jax.ShapeDtypeStruct((s, cw), BF16)
    row = pl.BlockSpec((tr, cw), lambda i: (i, 0))
    return pl.pallas_call(
        body, name="sgu_bwd", grid=(s // tr,),
        in_specs=[row, pl.BlockSpec((tr, cw), lambda i: (i, 1)), pl.BlockSpec((tr, cw), lambda i: (i, 2)),
                  row, _vec_spec(cw), _vec_spec(cw), full3(w_s.shape), full3(b_s.shape)],
        out_specs=[row, row, row, full3((N_GROUPS, CHUNK, LANES)), full3((N_GROUPS, CHUNK, LANES)),
                   _vec_spec(cw), _vec_spec(cw)],
        out_shape=[act, act, act, acc3, acc3, vec, vec],
        scratch_shapes=[pltpu.VMEM((tr, cw), F32)],
        compiler_params=_params(1),
    )(uvz, uvz, uvz, dy, ln_g, ln_b, w_s, b_s)


def _flat_rows(a):
    return a.reshape(-1, a.shape[-1])


def _add_sibling(grads, recv, layer_idx):
    _, nchip, k, n = grads.shape
    g2 = grads.reshape(2, nchip * k, n)
    r2 = recv.reshape(nchip * k, n)
    tr = _tile(nchip * k, (512, 256, 128))

    def body(c_ref, g_ref, r_ref, o_ref):
        o_ref[...] = (g_ref[...].astype(F32) + r_ref[...].astype(F32)).astype(BF16)

    out = pl.pallas_call(
        body, name="add_sibling",
        grid_spec=pltpu.PrefetchScalarGridSpec(
            num_scalar_prefetch=1, grid=(nchip * k // tr,),
            in_specs=[pl.BlockSpec((None, tr, n), lambda i, c: (c[0], i, 0)),
                      pl.BlockSpec((tr, n), lambda i, c: (i, 0))],
            out_specs=pl.BlockSpec((tr, n), lambda i, c: (i, 0))),
        out_shape=jax.ShapeDtypeStruct((nchip * k, n), BF16),
        compiler_params=_params(1, parallel=1),
    )(layer_idx, g2, r2)
    return out.reshape(nchip, k, n)


def _sum_chips(parts):
    nchip, k, n = parts.shape
    tr = _tile(k, (256, 128))

    def body(p_ref, o_ref):
        acc = p_ref[0].astype(F32)
        for q in range(1, nchip):
            acc = acc + p_ref[q].astype(F32)
        o_ref[...] = acc

    return pl.pallas_call(
        body, name="sum_chips", grid=(k // tr,),
        in_specs=[pl.BlockSpec((nchip, tr, n), lambda i: (0, i, 0))],
        out_specs=pl.BlockSpec((tr, n), lambda i: (i, 0)),
        out_shape=jax.ShapeDtypeStruct((k, n), F32),
        compiler_params=_params(1, parallel=1),
    )(parts)


def _sum_devices(parts):
    nd, r, _ = parts.shape
    tr = _tile(r, (512, 256, 128, 64, 32, 16, 8))

    def body(p_ref, o_ref):
        acc = p_ref[0]
        for q in range(1, nd):
            acc = acc + p_ref[q]
        o_ref[...] = acc

    return pl.pallas_call(
        body, name="sum_devices", grid=(r // tr,),
        in_specs=[pl.BlockSpec((nd, tr, LANES), lambda i: (0, i, 0))],
        out_specs=pl.BlockSpec((tr, LANES), lambda i: (i, 0)),
        out_shape=jax.ShapeDtypeStruct((r, LANES), F32),
        compiler_params=_params(1, parallel=1),
    )(parts)


def _adamw(w, g, m, v):
    r, n = w.shape
    tr = _tile(r, [p for p in (1024, 512, 256, 128, 64, 32, 16, 8) if p * n <= ELEMENTWISE_BLOCK])

    def body(w_ref, g_ref, m_ref, v_ref, d_ref, nm_ref, nv_ref):
        gv = g_ref[...]
        nm = ADAM_B1 * m_ref[...] + (1.0 - ADAM_B1) * gv
        nv = ADAM_B2 * v_ref[...] + (1.0 - ADAM_B2) * (gv * gv)
        m_hat = nm / (1.0 - ADAM_B1 ** ADAM_STEP)
        v_hat = nv / (1.0 - ADAM_B2 ** ADAM_STEP)
        d_ref[...] = -ADAM_LR * (m_hat / (jnp.sqrt(v_hat) + ADAM_EPS) + ADAM_WD * w_ref[...])
        nm_ref[...] = nm
        nv_ref[...] = nv

    spec = pl.BlockSpec((tr, n), lambda i: (i, 0))
    shp = jax.ShapeDtypeStruct((r, n), F32)
    return pl.pallas_call(
        body, name="adamw", grid=(r // tr,),
        in_specs=[spec] * 4, out_specs=[spec] * 3, out_shape=[shp] * 3,
        compiler_params=_params(1, parallel=1),
    )(w, g, m, v)


def _pack(arrays, row_multiple=8):
    flat = [a.reshape(-1) for a in arrays]
    sizes = [f.shape[0] for f in flat]
    total = sum(sizes)
    unit = LANES * row_multiple
    padded = -(-total // unit) * unit
    if padded > total:
        flat.append(jnp.zeros((padded - total,), F32))
    offsets = [sum(sizes[:i]) for i in range(len(sizes))]
    return jnp.concatenate(flat).reshape(-1, LANES), offsets


def _unpack(packed, offsets, shapes):
    flat = packed.reshape(-1)
    return [flat[o:o + math.prod(s)].reshape(s) for o, s in zip(offsets, shapes)]


def kernel(x, c, ab_norm_g, ab_w_mod, ab_b_mod, ab_w_in, ab_conv_w, ab_w_out, sg_norm_g, sg_w_mod, sg_b_mod, sg_w_in, sg_ln_g, sg_ln_b, sg_w_s, sg_b_s, sg_w_out, final_norm_g, loss_target, m_ab_norm_g, m_ab_w_mod, m_ab_b_mod, m_ab_w_in, m_ab_conv_w, m_ab_w_out, m_sg_norm_g, m_sg_w_mod, m_sg_b_mod, m_sg_w_in, m_sg_ln_g, m_sg_ln_b, m_sg_w_s, m_sg_b_s, m_sg_w_out, m_final_norm_g, v_ab_norm_g, v_ab_w_mod, v_ab_b_mod, v_ab_w_in, v_ab_conv_w, v_ab_w_out, v_sg_norm_g, v_sg_w_mod, v_sg_b_mod, v_sg_w_in, v_sg_ln_g, v_sg_ln_b, v_sg_w_s, v_sg_b_s, v_sg_w_out, v_final_norm_g):
    s, d = x.shape[1], x.shape[2]
    aw = d // 2
    nh = aw // HEAD_DIM
    cw = d
    mod_l = ab_w_mod.shape[-1]
    x0 = x[0]
    target = loss_target[0]
    mx, my, mc = lax.axis_index("x"), lax.axis_index("y"), lax.axis_index("c")
    chip = 2 * mx + my
    me = 2 * chip + mc

    small_local = [c[0], ab_conv_w, sg_norm_g, sg_ln_g, sg_ln_b]
    small_shapes = [a.shape for a in small_local]
    payload, small_off = _pack(small_local)
    gathered = _all_to_all(jnp.broadcast_to(payload[None], (N_DEV,) + payload.shape), "gather_small")
    per_dev = [_unpack(gathered[b], small_off, small_shapes) for b in range(N_DEV)]
    c_all = jnp.stack([per_dev[b][0] for b in range(N_DEV)])

    def from_chips(idx, axis):
        return jnp.concatenate([per_dev[2 * q][idx] for q in range(N_CHIPS)], axis=axis)

    conv_w_full = from_chips(1, 2)
    sg_norm_g_full = from_chips(2, 1)
    sg_ln_g_full = from_chips(3, 1)
    sg_ln_b_full = from_chips(4, 1)

    ab_b_local = lax.dynamic_slice_in_dim(ab_b_mod, chip * mod_l, mod_l, axis=1)
    mod_rows = []
    for layer in range(4):
        i = layer // 2
        w_mod, bias = (ab_w_mod, ab_b_local) if layer % 2 == 0 else (sg_w_mod, sg_b_mod)
        mod_rows.append(_mod_fwd(c_all, w_mod, bias[i:i + 1], i))
    mod_local = jnp.stack(mod_rows, axis=1)
    mod_recv = _all_to_all(mod_local.reshape(N_DEV, -1, LANES), "exchange_mod")
    mod_recv = mod_recv.reshape(N_DEV, 4, mod_l)
    mod_full = jnp.concatenate([mod_recv[2 * q] for q in range(N_CHIPS)], axis=-1)
    shifts = [mod_full[l:l + 1, :d] for l in range(4)]
    scales = [mod_full[l:l + 1, d:2 * d] for l in range(4)]
    gates = [mod_full[l:l + 1, 2 * d:] for l in range(4)]

    w_in_ab, w_out_ab, w_in_sg, w_out_sg = _gather_weights(
        [ab_w_in.astype(BF16), ab_w_out.astype(BF16), sg_w_in.astype(BF16), sg_w_out.astype(BF16)])
    w_out_ab2 = w_out_ab.reshape(2, -1, d)
    w_out_sg2 = w_out_sg.reshape(2, -1, d)

    cos, sin = _rope_tables(s)
    tables = {dil: (_to_pattern(cos[None], dil)[0], _to_pattern(sin[None], dil)[0]) for dil in DILATIONS}
    w_s16 = sg_w_s.astype(BF16)
    b_s3 = sg_b_s[..., None]

    saved = []
    xs = x0
    for layer in range(4):
        i = layer // 2
        if layer % 2 == 0:
            h = _prenorm(xs, ab_norm_g[i:i + 1], scales[layer], shifts[layer])
            proj = _in_proj(h, w_in_ab, i)
            qkv = _heads_major(proj[:, :3 * aw])
            outs, lses = [], []
            for dil in DILATIONS:
                o_p, l_p = _attn_fwd(_to_pattern(qkv, dil), *tables[dil])
                outs.append(_from_pattern(o_p))
                lses.append(_from_pattern(l_p))
            attn_hm, lse_hm = _attn_combine(outs, lses)
            attn = _heads_minor(attn_hm)
            y = _ab_mix(attn, proj, conv_w_full[i], aw)
            x_next, out = _out_proj_residual(y, w_out_ab2, i, xs, gates[layer])
            saved.append((xs, h, proj, y, out, qkv, attn_hm, lse_hm, attn))
        else:
            h = _prenorm(xs, sg_norm_g_full[i:i + 1], scales[layer], shifts[layer])
            uvz = _in_proj(h, w_in_sg, i)
            y = _sgu_fwd(uvz, sg_ln_g_full[i:i + 1], sg_ln_b_full[i:i + 1], w_s16[i], b_s3[i], cw)
            x_next, out = _out_proj_residual(y, w_out_sg2, i, xs, gates[layer])
            saved.append((xs, h, uvz, y, out))
        xs = x_next

    loss11, dx, d_final_g = _final_loss(xs, target, final_norm_g[None])
    loss = lax.psum(loss11[0, 0], ("x", "y", "c"))

    g_in_ab = jnp.zeros(w_in_ab.shape, BF16)
    g_out_ab = jnp.zeros(w_out_ab2.shape, BF16)
    g_in_sg = jnp.zeros(w_in_sg.shape, BF16)
    g_out_sg = jnp.zeros(w_out_sg2.shape, BF16)
    dmods = [None] * 4
    d_ab_norm_g, d_sg_norm_g = [None, None], [None, None]
    d_conv_w, d_ln_g, d_ln_b, d_w_s, d_b_s = ([None, None] for _ in range(5))
    for layer in reversed(range(4)):
        i = layer // 2
        if layer % 2 == 0:
            xs, h, proj, y, out, qkv, attn_hm, lse_hm, attn = saved[layer]
            dout, dgate = _gate_bwd(dx, out, gates[layer])
            dy = _out_proj_bwd_act(dout, w_out_ab2, i)
            g_out_ab = _out_proj_bwd_w(y, dout, g_out_ab, i)
            do_hm = _heads_major(_ab_dattn(dy, proj, aw))
            dqkv = []
            for dil in DILATIONS:
                dq, dk, dv = _attn_bwd(_to_pattern(qkv, dil), *tables[dil], _to_pattern(do_hm, dil),
                                       _to_pattern(attn_hm, dil), _to_pattern(lse_hm, dil))
                dqkv.append(_heads_minor(jnp.concatenate(
                    [_from_pattern(dq), _from_pattern(dk), _from_pattern(dv)], axis=0)))
            dproj, d_conv_w[i] = _ab_dproj(dqkv, dy, attn, proj, conv_w_full[i], aw)
            dh = _in_proj_bwd_act(dproj, w_in_ab, i)
            g_in_ab = _in_proj_bwd_w(h, dproj, g_in_ab, i)
            dx, dshift, dscale, d_ab_norm_g[i] = _prenorm_bwd(xs, dh, dx, ab_norm_g[i:i + 1], scales[layer])
        else:
            xs, h, uvz, y, out = saved[layer]
            dout, dgate = _gate_bwd(dx, out, gates[layer])
            dy = _out_proj_bwd_act(dout, w_out_sg2, i)
            g_out_sg = _out_proj_bwd_w(y, dout, g_out_sg, i)
            du, dv, dz, d_w_s[i], db_wide, d_ln_g[i], d_ln_b[i] = _sgu_bwd(
                uvz, dy, sg_ln_g_full[i:i + 1], sg_ln_b_full[i:i + 1], w_s16[i], b_s3[i], cw)
            d_b_s[i] = db_wide[:, :, 0]
            duvz = jnp.concatenate([du, dv, dz], axis=1)
            dh = _in_proj_bwd_act(duvz, w_in_sg, i)
            g_in_sg = _in_proj_bwd_w(h, duvz, g_in_sg, i)
            dx, dshift, dscale, d_sg_norm_g[i] = _prenorm_bwd(xs, dh, dx, sg_norm_g_full[i:i + 1], scales[layer])
        dmods[layer] = jnp.concatenate([dshift, dscale, dgate], axis=1)
    grad_x = dx[None]

    partial_list = [jnp.concatenate(dmods, axis=0),
                    jnp.concatenate(d_ab_norm_g, axis=0), jnp.concatenate(d_sg_norm_g, axis=0), d_final_g[0],
                    jnp.stack(d_conv_w), jnp.concatenate(d_ln_g, axis=0), jnp.concatenate(d_ln_b, axis=0),
                    jnp.stack(d_w_s), jnp.stack(d_b_s)]
    partial_shapes = [a.shape for a in partial_list]
    partials, part_off = _pack(partial_list)
    all_partials = _all_to_all(jnp.broadcast_to(partials[None], (N_DEV,) + partials.shape), "gather_partials")
    reduced = _unpack(_sum_devices(all_partials), part_off, partial_shapes)
    (g_mod_bias, g_ab_norm_g, g_sg_norm_g_full, g_final_g, g_conv_full, g_ln_g_full, g_ln_b_full,
     g_w_s, g_b_s) = reduced
    dm_all = jnp.stack([_unpack(all_partials[b], part_off[:1], partial_shapes[:1])[0] for b in range(N_DEV)])
    dm_local = lax.dynamic_slice_in_dim(dm_all, chip * mod_l, mod_l, axis=2)

    def chip_cols(a, axis):
        width = a.shape[axis] // N_CHIPS
        return lax.dynamic_slice_in_dim(a, chip * width, width, axis=axis)

    g_ab_b_mod = jnp.stack([g_mod_bias[0], g_mod_bias[2]])
    g_sg_b_mod = chip_cols(jnp.stack([g_mod_bias[1], g_mod_bias[3]]), 1)
    g_ab_w_mod = jnp.stack([_mod_bwd_w(c_all, dm_local[:, 0]), _mod_bwd_w(c_all, dm_local[:, 2])])
    g_sg_w_mod = jnp.stack([_mod_bwd_w(c_all, dm_local[:, 1]), _mod_bwd_w(c_all, dm_local[:, 3])])
    g_conv = chip_cols(g_conv_full, 2)
    g_sg_norm_g = chip_cols(g_sg_norm_g_full, 1)
    g_ln_g = chip_cols(g_ln_g_full, 1)
    g_ln_b = chip_cols(g_ln_b_full, 1)

    big = [g_in_ab, g_out_ab.reshape(w_out_ab.shape), g_in_sg, g_out_sg.reshape(w_out_sg.shape)]
    from_sib = _swap_layers_with_sibling(big)
    layer_idx = jnp.reshape(mc, (1,)).astype(jnp.int32)
    chip_sums = [_add_sibling(g, r, layer_idx) for g, r in zip(big, from_sib)]
    owned = _scatter_chip_sums(chip_sums)
    halves = [_sum_chips(p) for p in owned]
    g_ab_w_in, g_ab_w_out, g_sg_w_in, g_sg_w_out = _share_layer_with_sibling(halves)

    def step_big(w, g, m, v):
        dl, nm, nv = _adamw(_flat_rows(w), _flat_rows(g), _flat_rows(m), _flat_rows(v))
        return dl.reshape(w.shape), nm.reshape(w.shape), nv.reshape(w.shape)

    big_out = {
        "ab_w_mod": step_big(ab_w_mod, g_ab_w_mod, m_ab_w_mod, v_ab_w_mod),
        "ab_w_in": step_big(ab_w_in, g_ab_w_in, m_ab_w_in, v_ab_w_in),
        "ab_w_out": step_big(ab_w_out, g_ab_w_out, m_ab_w_out, v_ab_w_out),
        "sg_w_mod": step_big(sg_w_mod, g_sg_w_mod, m_sg_w_mod, v_sg_w_mod),
        "sg_w_in": step_big(sg_w_in, g_sg_w_in, m_sg_w_in, v_sg_w_in),
        "sg_w_out": step_big(sg_w_out, g_sg_w_out, m_sg_w_out, v_sg_w_out),
    }
    small_names = ["ab_norm_g", "ab_b_mod", "ab_conv_w", "sg_norm_g", "sg_b_mod", "sg_ln_g", "sg_ln_b",
                   "sg_w_s", "sg_b_s", "final_norm_g"]
    small_w = [ab_norm_g, ab_b_mod, ab_conv_w, sg_norm_g, sg_b_mod, sg_ln_g, sg_ln_b, sg_w_s, sg_b_s, final_norm_g]
    small_g = [g_ab_norm_g, g_ab_b_mod, g_conv, g_sg_norm_g, g_sg_b_mod, g_ln_g, g_ln_b, g_w_s, g_b_s, g_final_g]
    small_m = [m_ab_norm_g, m_ab_b_mod, m_ab_conv_w, m_sg_norm_g, m_sg_b_mod, m_sg_ln_g, m_sg_ln_b, m_sg_w_s,
               m_sg_b_s, m_final_norm_g]
    small_v = [v_ab_norm_g, v_ab_b_mod, v_ab_conv_w, v_sg_norm_g, v_sg_b_mod, v_sg_ln_g, v_sg_ln_b, v_sg_w_s,
               v_sg_b_s, v_final_norm_g]
    shapes = [a.shape for a in small_w]
    pw, off = _pack(small_w)
    pg, _ = _pack(small_g)
    pm, _ = _pack(small_m)
    pv, _ = _pack(small_v)
    pd, pnm, pnv = _adamw(pw, pg, pm, pv)
    small_out = {}
    for name, dl, nm, nv in zip(small_names, _unpack(pd, off, shapes), _unpack(pnm, off, shapes),
                                _unpack(pnv, off, shapes)):
        small_out[name] = (dl, nm, nv)

    grads = {
        "ab_norm_g": g_ab_norm_g, "ab_w_mod": g_ab_w_mod, "ab_b_mod": g_ab_b_mod, "ab_w_in": g_ab_w_in,
        "ab_conv_w": g_conv, "ab_w_out": g_ab_w_out, "sg_norm_g": g_sg_norm_g, "sg_w_mod": g_sg_w_mod,
        "sg_b_mod": g_sg_b_mod, "sg_w_in": g_sg_w_in, "sg_ln_g": g_ln_g, "sg_ln_b": g_ln_b, "sg_w_s": g_w_s,
        "sg_b_s": g_b_s, "sg_w_out": g_sg_w_out, "final_norm_g": g_final_g,
    }
    order = ["ab_norm_g", "ab_w_mod", "ab_b_mod", "ab_w_in", "ab_conv_w", "ab_w_out", "sg_norm_g", "sg_w_mod",
             "sg_b_mod", "sg_w_in", "sg_ln_g", "sg_ln_b", "sg_w_s", "sg_b_s", "sg_w_out", "final_norm_g"]
    steps = {**big_out, **small_out}
    return (loss, grad_x, *[grads[n] for n in order], *[steps[n][0] for n in order],
            *[steps[n][1] for n in order], *[steps[n][2] for n in order])
```

```python
import functools
import math

import jax
import jax.numpy as jnp
from jax import lax
from jax.experimental import pallas as pl
from jax.experimental.pallas import tpu as pltpu

F32 = jnp.float32
BF16 = jnp.bfloat16

HEAD_DIM = 128
RADIUS = 64
DILATIONS = (1, 4, 16)
Q_BLOCK = 128
K_WINDOW = Q_BLOCK + 2 * RADIUS
ROPE_THETA = 10000.0
NEG_INF = -1e30
N_GROUPS = 8
CHUNK = 128
EPS = 1e-6
CONV_ROWS = 512
CONV_HALO = 16
LANES = 128
ELEMENTWISE_BLOCK = 256 * 1024
N_DEV = 8
N_CHIPS = 4

ADAM_LR = 0.001
ADAM_B1 = 0.9
ADAM_B2 = 0.999
ADAM_EPS = 1e-08
ADAM_WD = 0.01
ADAM_STEP = 10

VMEM_LIMIT_V7X = 56 * 1024 * 1024

MESH_ID = pl.DeviceIdType.MESH
HBM_SPEC = pl.BlockSpec(memory_space=pltpu.HBM)

NN = (((1,), (0,)), ((), ()))
NT = (((1,), (1,)), ((), ()))
TN = (((0,), (0,)), ((), ()))


def _params(n_grid, parallel=0):
    sem = tuple(["parallel"] * parallel + ["arbitrary"] * (n_grid - parallel))
    return pltpu.CompilerParams(dimension_semantics=sem, vmem_limit_bytes=VMEM_LIMIT_V7X)


def _tile(n, prefs):
    for p in prefs:
        if n % p == 0:
            return p
    return n


def _sigmoid(z):
    return 1.0 / (1.0 + jnp.exp(-z))


def _silu(z):
    return z * _sigmoid(z)


def _dsilu(z):
    s = _sigmoid(z)
    return s * (1.0 + z * (1.0 - s))


_GELU_K = math.sqrt(2.0 / math.pi)
_GELU_C = 0.044715


def _gelu(u):
    return 0.5 * u * (1.0 + jnp.tanh(_GELU_K * (u + _GELU_C * u * u * u)))


def _dgelu(u):
    t = jnp.tanh(_GELU_K * (u + _GELU_C * u * u * u))
    return 0.5 * (1.0 + t) + 0.5 * u * (1.0 - t * t) * _GELU_K * (1.0 + 3.0 * _GELU_C * u * u)


def _my_place():
    return lax.axis_index("x"), lax.axis_index("y"), lax.axis_index("c")


def _flip(v, bit):
    return 1 - v if bit else v


def _all_to_all(x, name):
    def body(x_ref, y_ref, send_sems, recv_sems, own_sem):
        mx, my, mc = _my_place()
        me = 4 * mx + 2 * my + mc
        own = pltpu.make_async_copy(x_ref.at[me], y_ref.at[me], own_sem)
        own.start()
        copies = []
        for k in range(1, N_DEV):
            px, py, pc = _flip(mx, (k >> 2) & 1), _flip(my, (k >> 1) & 1), _flip(mc, k & 1)
            peer = 4 * px + 2 * py + pc
            cp = pltpu.make_async_remote_copy(
                src_ref=x_ref.at[peer], dst_ref=y_ref.at[me],
                send_sem=send_sems.at[k - 1], recv_sem=recv_sems.at[k - 1],
                device_id=(px, py, pc), device_id_type=MESH_ID)
            cp.start()
            copies.append(cp)
        for cp in copies:
            cp.wait()
        own.wait()

    return pl.pallas_call(
        body, name=name,
        out_shape=jax.ShapeDtypeStruct(x.shape, x.dtype),
        in_specs=[HBM_SPEC], out_specs=HBM_SPEC,
        scratch_shapes=[pltpu.SemaphoreType.DMA((N_DEV - 1,)), pltpu.SemaphoreType.DMA((N_DEV - 1,)),
                        pltpu.SemaphoreType.DMA],
    )(x)


def _other_chips(mx, my):
    return [(1 - mx, my), (mx, 1 - my), (1 - mx, 1 - my)]


def _place_own_shard(w, chip_idx):
    _, k, n = w.shape
    tr = _tile(k, (512, 256, 128))

    def body(c_ref, w_ref, g_ref):
        g_ref[...] = w_ref[...].astype(BF16)

    return pl.pallas_call(
        body, name="place_own_shard",
        grid_spec=pltpu.PrefetchScalarGridSpec(
            num_scalar_prefetch=1, grid=(2, k // tr),
            in_specs=[pl.BlockSpec((None, tr, n), lambda i, r, c: (i, r, 0))],
            out_specs=pl.BlockSpec((None, None, tr, n), lambda i, r, c: (i, c[0], r, 0))),
        out_shape=jax.ShapeDtypeStruct((2, N_CHIPS, k, n), BF16),
        compiler_params=_params(2, parallel=2),
    )(chip_idx, w)


def _gather_weights(placed):
    n = len(placed)

    def body(*refs):
        w = refs[:n]
        g = refs[n:2 * n]
        send_sems, recv_sems, fsend_sems, frecv_sems = refs[2 * n:]
        mx, my, mc = _my_place()
        j = 2 * mx + my
        chips = _other_chips(mx, my)
        first, passed = [], []
        for t in range(n):
            for q, (px, py) in enumerate(chips):
                cp = pltpu.make_async_remote_copy(
                    src_ref=w[t].at[mc, j], dst_ref=g[t].at[mc, j],
                    send_sem=send_sems.at[3 * t + q], recv_sem=recv_sems.at[3 * t + q],
                    device_id=(px, py, mc), device_id_type=MESH_ID)
                cp.start()
                first.append(cp)
        for t in range(n):
            for q, (px, py) in enumerate(chips):
                jq = 2 * px + py
                landed = pltpu.make_async_remote_copy(
                    src_ref=w[t].at[mc, j], dst_ref=g[t].at[mc, jq],
                    send_sem=send_sems.at[3 * t + q], recv_sem=recv_sems.at[3 * t + q],
                    device_id=(px, py, mc), device_id_type=MESH_ID)
                landed.wait_recv()
                fwd = pltpu.make_async_remote_copy(
                    src_ref=g[t].at[mc, jq], dst_ref=g[t].at[mc, jq],
                    send_sem=fsend_sems.at[3 * t + q], recv_sem=frecv_sems.at[3 * t + q],
                    device_id=(mx, my, 1 - mc), device_id_type=MESH_ID)
                fwd.start()
                passed.append(fwd)
        for t in range(n):
            for q, (px, py) in enumerate(chips):
                jq = 2 * px + py
                from_sibling = pltpu.make_async_remote_copy(
                    src_ref=g[t].at[1 - mc, jq], dst_ref=g[t].at[1 - mc, jq],
                    send_sem=fsend_sems.at[3 * t + q], recv_sem=frecv_sems.at[3 * t + q],
                    device_id=(mx, my, 1 - mc), device_id_type=MESH_ID)
                from_sibling.wait_recv()
        for cp in first + passed:
            cp.wait_send()

    return pl.pallas_call(
        body, name="gather_weights",
        out_shape=[jax.ShapeDtypeStruct(a.shape, a.dtype) for a in placed],
        in_specs=[HBM_SPEC] * n, out_specs=[HBM_SPEC] * n,
        input_output_aliases={t: t for t in range(n)},
        scratch_shapes=[pltpu.SemaphoreType.DMA((3 * n,)), pltpu.SemaphoreType.DMA((3 * n,)),
                        pltpu.SemaphoreType.DMA((3 * n,)), pltpu.SemaphoreType.DMA((3 * n,))],
    )(*placed)


def _swap_layers_with_sibling(grads):
    n = len(grads)

    def body(*refs):
        g = refs[:n]
        r = refs[n:2 * n]
        send_sems, recv_sems = refs[2 * n:]
        mx, my, mc = _my_place()
        copies = []
        for t in range(n):
            cp = pltpu.make_async_remote_copy(
                src_ref=g[t].at[1 - mc], dst_ref=r[t],
                send_sem=send_sems.at[t], recv_sem=recv_sems.at[t],
                device_id=(mx, my, 1 - mc), device_id_type=MESH_ID)
            cp.start()
            copies.append(cp)
        for cp in copies:
            cp.wait()

    return pl.pallas_call(
        body, name="grads_to_sibling",
        out_shape=[jax.ShapeDtypeStruct(a.shape[1:], a.dtype) for a in grads],
        in_specs=[HBM_SPEC] * n, out_specs=[HBM_SPEC] * n,
        scratch_shapes=[pltpu.SemaphoreType.DMA((n,)), pltpu.SemaphoreType.DMA((n,))],
    )(*grads)


def _scatter_chip_sums(sums):
    n = len(sums)

    def body(*refs):
        s = refs[:n]
        r = refs[n:2 * n]
        send_sems, recv_sems = refs[2 * n:]
        mx, my, mc = _my_place()
        copies = []
        for t in range(n):
            for q, (px, py) in enumerate(_other_chips(mx, my)):
                cp = pltpu.make_async_remote_copy(
                    src_ref=s[t].at[2 * px + py], dst_ref=r[t].at[q],
                    send_sem=send_sems.at[3 * t + q], recv_sem=recv_sems.at[3 * t + q],
                    device_id=(px, py, mc), device_id_type=MESH_ID)
                cp.start()
                copies.append(cp)
        for cp in copies:
            cp.wait()

    return pl.pallas_call(
        body, name="chip_sums_to_owners",
        out_shape=[jax.ShapeDtypeStruct((3,) + a.shape[1:], a.dtype) for a in sums],
        in_specs=[HBM_SPEC] * n, out_specs=[HBM_SPEC] * n,
        scratch_shapes=[pltpu.SemaphoreType.DMA((3 * n,)), pltpu.SemaphoreType.DMA((3 * n,))],
    )(*sums)


def _share_layer_with_sibling(reduced):
    n = len(reduced)

    def body(*refs):
        h = refs[:n]
        f = refs[n:2 * n]
        send_sems, recv_sems = refs[2 * n:]
        mx, my, mc = _my_place()
        copies = []
        for t in range(n):
            rc = pltpu.make_async_remote_copy(
                src_ref=h[t].at[mc], dst_ref=f[t].at[mc],
                send_sem=send_sems.at[t], recv_sem=recv_sems.at[t],
                device_id=(mx, my, 1 - mc), device_id_type=MESH_ID)
            rc.start()
            copies.append(rc)
        for cp in copies:
            cp.wait()

    return pl.pallas_call(
        body, name="reduced_grads_to_sibling",
        out_shape=[jax.ShapeDtypeStruct(a.shape, a.dtype) for a in reduced],
        in_specs=[HBM_SPEC] * n, out_specs=[HBM_SPEC] * n,
        input_output_aliases={t: t for t in range(n)},
        scratch_shapes=[pltpu.SemaphoreType.DMA((n,)), pltpu.SemaphoreType.DMA((n,))],
    )(*reduced)


def _matmul(name, operands, in_specs, grid, nk, dims, out_shape, out_specs, acc_shape,
            epilogue, a_prologue=None, aliases=None):
    n_in = len(operands)

    def body(*refs):
        a_ref, b_ref = refs[0], refs[1]
        extra = refs[2:n_in]
        outs = refs[n_in:-1] if nk > 1 else refs[n_in:]
        a = a_ref[...]
        if a_prologue is not None:
            a = a_prologue(a)
        part = lax.dot_general(a.astype(BF16), b_ref[...].astype(BF16), dims, preferred_element_type=F32)
        if nk == 1:
            epilogue(part, extra, outs)
        else:
            acc = refs[-1]
            k = pl.program_id(2)

            @pl.when(k == 0)
            def _():
                acc[...] = part

            @pl.when(k > 0)
            def _():
                acc[...] += part

            @pl.when(k == nk - 1)
            def _():
                epilogue(acc[...], extra, outs)

    return pl.pallas_call(
        body, name=name, grid=grid, in_specs=in_specs, out_specs=out_specs, out_shape=out_shape,
        scratch_shapes=[pltpu.VMEM(acc_shape, F32)] if nk > 1 else [],
        input_output_aliases=aliases or {},
        compiler_params=_params(3, parallel=2),
    )(*operands)


def _store_cast(acc, extra, outs):
    outs[0][...] = acc.astype(outs[0].dtype)


def _in_proj(h, w_g, layer):
    s, d = h.shape
    nl = w_g.shape[-1]
    tm = _tile(s, (1024, 512, 256))
    tn = _tile(nl, (1024, 768, 512, 384, 256, 128))
    per = nl // tn
    return _matmul(
        "in_proj", (h, w_g),
        [pl.BlockSpec((tm, d), lambda i, j, k: (i, 0)),
         pl.BlockSpec((None, None, d, tn), lambda i, j, k: (layer, j // per, 0, j % per))],
        (s // tm, N_CHIPS * per, 1), 1, NN,
        jax.ShapeDtypeStruct((s, N_CHIPS * nl), BF16),
        pl.BlockSpec((tm, tn), lambda i, j, k: (i, j)), (tm, tn), _store_cast)


def _out_proj_residual(y, w2, layer, x, gate):
    s, wdt = y.shape
    d = w2.shape[-1]
    tm = _tile(s, (1024, 512, 256))
    tn = _tile(d, (1024, 512, 256, 128))

    def epilogue(acc, extra, outs):
        x_ref, gate_ref = extra
        outs[0][...] = x_ref[...] + gate_ref[...] * acc
        outs[1][...] = acc.astype(BF16)

    return _matmul(
        "out_proj", (y, w2, x, gate),
        [pl.BlockSpec((tm, wdt), lambda i, j, k: (i, 0)),
         pl.BlockSpec((None, wdt, tn), lambda i, j, k: (layer, 0, j)),
         pl.BlockSpec((tm, tn), lambda i, j, k: (i, j)),
         pl.BlockSpec((1, tn), lambda i, j, k: (0, j))],
        (s // tm, d // tn, 1), 1, NN,
        [jax.ShapeDtypeStruct((s, d), F32), jax.ShapeDtypeStruct((s, d), BF16)],
        [pl.BlockSpec((tm, tn), lambda i, j, k: (i, j)), pl.BlockSpec((tm, tn), lambda i, j, k: (i, j))],
        (tm, tn), epilogue)


def _out_proj_bwd_act(dout, w2, layer):
    s, d = dout.shape
    wdt = w2.shape[1]
    tm = _tile(s, (1024, 512, 256))
    tn = _tile(wdt, (1024, 512, 256, 128))
    return _matmul(
        "out_proj_dy", (dout, w2),
        [pl.BlockSpec((tm, d), lambda i, j, k: (i, 0)),
         pl.BlockSpec((None, tn, d), lambda i, j, k: (layer, j, 0))],
        (s // tm, wdt // tn, 1), 1, NT,
        jax.ShapeDtypeStruct((s, wdt), BF16),
        pl.BlockSpec((tm, tn), lambda i, j, k: (i, j)), (tm, tn), _store_cast)


def _out_proj_bwd_w(y, dout, buf, layer):
    s, wdt = y.shape
    d = dout.shape[1]
    tm = _tile(wdt, (1024, 512, 256, 128))
    tn = _tile(d, (1024, 512, 256, 128))
    ts = _tile(s, (1024, 512, 256))
    nk = s // ts

    def epilogue(acc, extra, outs):
        outs[0][...] = acc.astype(BF16)

    return _matmul(
        "out_proj_dw", (y, dout, buf),
        [pl.BlockSpec((ts, tm), lambda i, j, k: (k, i)),
         pl.BlockSpec((ts, tn), lambda i, j, k: (k, j)),
         HBM_SPEC],
        (wdt // tm, d // tn, nk), nk, TN,
        jax.ShapeDtypeStruct(buf.shape, buf.dtype),
        pl.BlockSpec((None, tm, tn), lambda i, j, k: (layer, i, j)), (tm, tn), epilogue,
        aliases={2: 0})


def _in_proj_bwd_act(dproj, w_g, layer):
    s, n_all = dproj.shape
    d, nl = w_g.shape[2], w_g.shape[3]
    tm = _tile(s, (1024, 512, 256))
    tn = _tile(d, (1024, 512, 256, 128))
    tc = _tile(nl, (1024, 768, 512, 384, 256, 128))
    per = nl // tc
    nk = N_CHIPS * per
    return _matmul(
        "in_proj_dh", (dproj, w_g),
        [pl.BlockSpec((tm, tc), lambda i, j, k: (i, k)),
         pl.BlockSpec((None, None, tn, tc), lambda i, j, k: (layer, k // per, j, k % per))],
        (s // tm, d // tn, nk), nk, NT,
        jax.ShapeDtypeStruct((s, d), BF16),
        pl.BlockSpec((tm, tn), lambda i, j, k: (i, j)), (tm, tn), _store_cast)


def _in_proj_bwd_w(h, dproj, buf, layer):
    s, d = h.shape
    nl = buf.shape[-1]
    tm = _tile(d, (1024, 512, 256, 128))
    tn = _tile(nl, (1024, 768, 512, 384, 256, 128))
    ts = _tile(s, (1024, 512, 256))
    per = nl // tn
    nk = s // ts

    def epilogue(acc, extra, outs):
        outs[0][...] = acc.astype(BF16)

    return _matmul(
        "in_proj_dw", (h, dproj, buf),
        [pl.BlockSpec((ts, tm), lambda i, j, k: (k, i)),
         pl.BlockSpec((ts, tn), lambda i, j, k: (k, j)),
         HBM_SPEC],
        (d // tm, N_CHIPS * per, nk), nk, TN,
        jax.ShapeDtypeStruct(buf.shape, buf.dtype),
        pl.BlockSpec((None, None, tm, tn), lambda i, j, k: (layer, j // per, i, j % per)), (tm, tn), epilogue,
        aliases={2: 0})


def _mod_fwd(c_all, w_mod, bias, layer):
    nb, d = c_all.shape
    nl = w_mod.shape[-1]
    tn = _tile(nl, (768, 512, 384, 256, 128))
    tk = _tile(d, (1024, 512, 256, 128))
    nk = d // tk

    def epilogue(acc, extra, outs):
        outs[0][...] = acc + extra[0][...]

    return _matmul(
        "mod_fwd", (c_all, w_mod, bias),
        [pl.BlockSpec((nb, tk), lambda i, j, k: (0, k)),
         pl.BlockSpec((None, tk, tn), lambda i, j, k: (layer, k, j)),
         pl.BlockSpec((1, tn), lambda i, j, k: (0, j))],
        (1, nl // tn, nk), nk, NN,
        jax.ShapeDtypeStruct((nb, nl), F32),
        pl.BlockSpec((nb, tn), lambda i, j, k: (0, j)), (nb, tn), epilogue, a_prologue=_silu)


def _mod_bwd_w(c_all, dm_local):
    nb, d = c_all.shape
    nl = dm_local.shape[-1]
    tm = _tile(d, (1024, 512, 256, 128))
    tn = _tile(nl, (768, 512, 384, 256, 128))

    def epilogue(acc, extra, outs):
        outs[0][...] = acc

    return _matmul(
        "mod_dw", (c_all, dm_local),
        [pl.BlockSpec((nb, tm), lambda i, j, k: (0, i)),
         pl.BlockSpec((nb, tn), lambda i, j, k: (0, j))],
        (d // tm, nl // tn, 1), 1, TN,
        jax.ShapeDtypeStruct((d, nl), F32),
        pl.BlockSpec((tm, tn), lambda i, j, k: (i, j)), (tm, tn), epilogue, a_prologue=_silu)


def _rows_call(name, body, operands, in_specs, out_shape, out_specs, n_tiles):
    return pl.pallas_call(
        body, name=name, grid=(n_tiles,), in_specs=in_specs, out_specs=out_specs, out_shape=out_shape,
        compiler_params=_params(1),
    )(*operands)


def _row_spec(tr, width):
    return pl.BlockSpec((tr, width), lambda i: (i, 0))


def _vec_spec(width):
    return pl.BlockSpec((1, width), lambda i: (0, 0))


def _accumulate(ref, val):
    first = pl.program_id(0) == 0

    @pl.when(first)
    def _():
        ref[...] = val

    @pl.when(jnp.logical_not(first))
    def _():
        ref[...] += val


def _prenorm(x, g, scale, shift):
    s, d = x.shape
    tr = _tile(s, (256, 128))

    def body(x_ref, g_ref, sc_ref, sh_ref, h_ref):
        xv = x_ref[...]
        rstd = lax.rsqrt(jnp.mean(xv * xv, axis=-1, keepdims=True) + EPS)
        h_ref[...] = ((xv * rstd) * g_ref[...] * (1.0 + sc_ref[...]) + sh_ref[...]).astype(BF16)

    return _rows_call("prenorm", body, (x, g, scale, shift),
                      [_row_spec(tr, d), _vec_spec(d), _vec_spec(d), _vec_spec(d)],
                      jax.ShapeDtypeStruct((s, d), BF16), _row_spec(tr, d), s // tr)


def _prenorm_bwd(x, dh, dres, g, scale):
    s, d = x.shape
    tr = _tile(s, (256, 128))

    def body(x_ref, dh_ref, dres_ref, g_ref, sc_ref, dx_ref, dshift_ref, dscale_ref, dg_ref):
        xv = x_ref[...]
        dhv = dh_ref[...].astype(F32)
        rstd = lax.rsqrt(jnp.mean(xv * xv, axis=-1, keepdims=True) + EPS)
        xhat = xv * rstd
        gv = g_ref[...]
        one_sc = 1.0 + sc_ref[...]
        dxhat = dhv * gv * one_sc
        dx_ref[...] = dres_ref[...] + rstd * (dxhat - xhat * jnp.mean(dxhat * xhat, axis=-1, keepdims=True))
        _accumulate(dshift_ref, jnp.sum(dhv, axis=0, keepdims=True))
        _accumulate(dscale_ref, jnp.sum(dhv * xhat * gv, axis=0, keepdims=True))
        _accumulate(dg_ref, jnp.sum(dhv * xhat * one_sc, axis=0, keepdims=True))

    vec = jax.ShapeDtypeStruct((1, d), F32)
    return _rows_call("prenorm_bwd", body, (x, dh, dres, g, scale),
                      [_row_spec(tr, d), _row_spec(tr, d), _row_spec(tr, d), _vec_spec(d), _vec_spec(d)],
                      [jax.ShapeDtypeStruct((s, d), F32), vec, vec, vec],
                      [_row_spec(tr, d), _vec_spec(d), _vec_spec(d), _vec_spec(d)], s // tr)


def _gate_bwd(dx, out, gate):
    s, d = dx.shape
    tr = _tile(s, (256, 128))

    def body(dx_ref, out_ref, gate_ref, dout_ref, dgate_ref):
        dxv = dx_ref[...]
        dout_ref[...] = (gate_ref[...] * dxv).astype(BF16)
        _accumulate(dgate_ref, jnp.sum(dxv * out_ref[...].astype(F32), axis=0, keepdims=True))

    return _rows_call("gate_bwd", body, (dx, out, gate),
                      [_row_spec(tr, d), _row_spec(tr, d), _vec_spec(d)],
                      [jax.ShapeDtypeStruct((s, d), BF16), jax.ShapeDtypeStruct((1, d), F32)],
                      [_row_spec(tr, d), _vec_spec(d)], s // tr)


def _final_loss(x, target, g):
    s, d = x.shape
    tr = _tile(s, (256, 128))
    n_tiles = s // tr

    def body(x_ref, t_ref, g_ref, loss_ref, dx_ref, dg_ref, acc_ref):
        xv = x_ref[...]
        rstd = lax.rsqrt(jnp.mean(xv * xv, axis=-1, keepdims=True) + EPS)
        xhat = xv * rstd
        gv = g_ref[...]
        err = xhat * gv - t_ref[...]
        dy = err * (1.0 / d)
        dxhat = dy * gv
        dx_ref[...] = rstd * (dxhat - xhat * jnp.mean(dxhat * xhat, axis=-1, keepdims=True))
        _accumulate(dg_ref, jnp.sum(dy * xhat, axis=0, keepdims=True))
        _accumulate(acc_ref, jnp.sum(err * err, axis=0, keepdims=True))

        @pl.when(pl.program_id(0) == n_tiles - 1)
        def _():
            loss_ref[...] = (0.5 / d) * jnp.sum(acc_ref[...], axis=1, keepdims=True)

    return pl.pallas_call(
        body, name="final_loss", grid=(n_tiles,),
        in_specs=[_row_spec(tr, d), _row_spec(tr, d), _vec_spec(d)],
        out_specs=[pl.BlockSpec((1, 1), lambda i: (0, 0)), _row_spec(tr, d), _vec_spec(d)],
        out_shape=[jax.ShapeDtypeStruct((1, 1), F32), jax.ShapeDtypeStruct((s, d), F32),
                   jax.ShapeDtypeStruct((1, d), F32)],
        scratch_shapes=[pltpu.VMEM((1, d), F32)],
        compiler_params=_params(1),
    )(x, target, g)


def _rope(t, cos, sin):
    return t * cos + pltpu.roll(t, HEAD_DIM // 2, axis=1) * sin


def _unrope(dt, cos, sin):
    return dt * cos + pltpu.roll(dt * sin, HEAD_DIM // 2, axis=1)


def _band_window(b, sub):
    q0 = pl.multiple_of(b * Q_BLOCK, Q_BLOCK)
    start = pl.multiple_of(jnp.clip(b * Q_BLOCK - RADIUS, 0, sub - K_WINDOW), RADIUS)
    qi = q0 + lax.broadcasted_iota(jnp.int32, (Q_BLOCK, K_WINDOW), 0)
    ki = start + lax.broadcasted_iota(jnp.int32, (Q_BLOCK, K_WINDOW), 1)
    return q0, start, jnp.abs(qi - ki) <= RADIUS


def _attn_fwd(qkv, cos, sin):
    h3, dil, sub, _ = qkv.shape
    nh = h3 // 3
    nb = sub // Q_BLOCK
    scale = HEAD_DIM ** -0.5

    def body(q_ref, k_ref, v_ref, cos_ref, sin_ref, o_ref, lse_ref, qs, ks):
        cosv, sinv = cos_ref[...], sin_ref[...]
        qs[...] = _rope(q_ref[...].astype(F32), cosv, sinv).astype(BF16)
        ks[...] = _rope(k_ref[...].astype(F32), cosv, sinv).astype(BF16)

        def block(b, carry):
            q0, start, valid = _band_window(b, sub)
            q = qs[pl.ds(q0, Q_BLOCK), :]
            kk = ks[pl.ds(start, K_WINDOW), :]
            vv = v_ref[pl.ds(start, K_WINDOW), :]
            sc = lax.dot_general(q, kk, NT, preferred_element_type=F32) * scale
            sc = jnp.where(valid, sc, NEG_INF)
            m = jnp.max(sc, axis=1, keepdims=True)
            p = jnp.exp(sc - m)
            den = jnp.sum(p, axis=1, keepdims=True)
            o = lax.dot_general(p.astype(BF16), vv, NN, preferred_element_type=F32) / den
            o_ref[pl.ds(q0, Q_BLOCK), :] = o
            lse_ref[pl.ds(q0, Q_BLOCK), :] = jnp.broadcast_to(m + jnp.log(den), (Q_BLOCK, HEAD_DIM))
            return carry

        lax.fori_loop(0, nb, block, 0)

    seg = (None, None, sub, HEAD_DIM)
    out = jax.ShapeDtypeStruct((nh, dil, sub, HEAD_DIM), F32)
    return pl.pallas_call(
        body, name="attn_fwd", grid=(nh, dil),
        in_specs=[pl.BlockSpec(seg, lambda h, r: (h, r, 0, 0)),
                  pl.BlockSpec(seg, lambda h, r: (nh + h, r, 0, 0)),
                  pl.BlockSpec(seg, lambda h, r: (2 * nh + h, r, 0, 0)),
                  pl.BlockSpec((None, sub, HEAD_DIM), lambda h, r: (r, 0, 0)),
                  pl.BlockSpec((None, sub, HEAD_DIM), lambda h, r: (r, 0, 0))],
        out_specs=[pl.BlockSpec(seg, lambda h, r: (h, r, 0, 0)), pl.BlockSpec(seg, lambda h, r: (h, r, 0, 0))],
        out_shape=[out, out],
        scratch_shapes=[pltpu.VMEM((sub, HEAD_DIM), BF16), pltpu.VMEM((sub, HEAD_DIM), BF16)],
        compiler_params=_params(2, parallel=2),
    )(qkv, qkv, qkv, cos, sin)


def _attn_bwd(qkv, cos, sin, do, attn, lse):
    h3, dil, sub, _ = qkv.shape
    nh = h3 // 3
    nb = sub // Q_BLOCK
    scale = HEAD_DIM ** -0.5

    def body(q_ref, k_ref, v_ref, cos_ref, sin_ref, do_ref, attn_ref, lse_ref,
             dq_ref, dk_ref, dv_ref, qs, ks, dqs, dks, dvs):
        cosv, sinv = cos_ref[...], sin_ref[...]
        qs[...] = _rope(q_ref[...].astype(F32), cosv, sinv).astype(BF16)
        ks[...] = _rope(k_ref[...].astype(F32), cosv, sinv).astype(BF16)
        dks[...] = jnp.zeros_like(dks)
        dvs[...] = jnp.zeros_like(dvs)

        def block(b, carry):
            q0, start, valid = _band_window(b, sub)
            q = qs[pl.ds(q0, Q_BLOCK), :]
            kk = ks[pl.ds(start, K_WINDOW), :]
            vv = v_ref[pl.ds(start, K_WINDOW), :]
            dov = do_ref[pl.ds(q0, Q_BLOCK), :]
            delta = jnp.sum(dov.astype(F32) * attn_ref[pl.ds(q0, Q_BLOCK), :].astype(F32), axis=1, keepdims=True)
            lse_q = lse_ref[pl.ds(q0, Q_BLOCK), :][:, 0:1]
            sc = lax.dot_general(q, kk, NT, preferred_element_type=F32) * scale
            p = jnp.where(valid, jnp.exp(sc - lse_q), 0.0)
            dp = lax.dot_general(dov, vv, NT, preferred_element_type=F32)
            ds = (p * (dp - delta) * scale).astype(BF16)
            dqs[pl.ds(q0, Q_BLOCK), :] = lax.dot_general(ds, kk, NN, preferred_element_type=F32)
            dks[pl.ds(start, K_WINDOW), :] += lax.dot_general(ds, q, TN, preferred_element_type=F32)
            dvs[pl.ds(start, K_WINDOW), :] += lax.dot_general(p.astype(BF16), dov, TN, preferred_element_type=F32)
            return carry

        lax.fori_loop(0, nb, block, 0)
        dq_ref[...] = _unrope(dqs[...], cosv, sinv).astype(BF16)
        dk_ref[...] = _unrope(dks[...], cosv, sinv).astype(BF16)
        dv_ref[...] = dvs[...].astype(BF16)

    seg = (None, None, sub, HEAD_DIM)
    own = pl.BlockSpec(seg, lambda h, r: (h, r, 0, 0))
    tab = pl.BlockSpec((None, sub, HEAD_DIM), lambda h, r: (r, 0, 0))
    out = jax.ShapeDtypeStruct((nh, dil, sub, HEAD_DIM), BF16)
    return pl.pallas_call(
        body, name="attn_bwd", grid=(nh, dil),
        in_specs=[own,
                  pl.BlockSpec(seg, lambda h, r: (nh + h, r, 0, 0)),
                  pl.BlockSpec(seg, lambda h, r: (2 * nh + h, r, 0, 0)),
                  tab, tab, own, own, own],
        out_specs=[own, own, own],
        out_shape=[out, out, out],
        scratch_shapes=[pltpu.VMEM((sub, HEAD_DIM), BF16), pltpu.VMEM((sub, HEAD_DIM), BF16),
                        pltpu.VMEM((sub, HEAD_DIM), F32), pltpu.VMEM((sub, HEAD_DIM), F32),
                        pltpu.VMEM((sub, HEAD_DIM), F32)],
        compiler_params=_params(2, parallel=2),
    )(qkv, qkv, qkv, cos, sin, do, attn, lse)


def _attn_combine(outs, lses):
    nh, s, _ = outs[0].shape
    tr = _tile(s, (512, 256, 128))
    n_pat = len(outs)

    def body(*refs):
        o = refs[:n_pat]
        l = refs[n_pat:2 * n_pat]
        attn_ref, lse_ref = refs[2 * n_pat:]
        lv = [r[...] for r in l]
        m = functools.reduce(jnp.maximum, lv)
        ws = [jnp.exp(v - m) for v in lv]
        tot = functools.reduce(lambda a, b: a + b, ws)
        acc = functools.reduce(lambda a, b: a + b, [w * r[...] for w, r in zip(ws, o)])
        attn_ref[...] = (acc / tot).astype(BF16)
        lse_ref[...] = m + jnp.log(tot)

    spec = pl.BlockSpec((None, tr, HEAD_DIM), lambda h, i: (h, i, 0))
    return pl.pallas_call(
        body, name="attn_combine", grid=(nh, s // tr),
        in_specs=[spec] * (2 * n_pat), out_specs=[spec, spec],
        out_shape=[jax.ShapeDtypeStruct((nh, s, HEAD_DIM), BF16), jax.ShapeDtypeStruct((nh, s, HEAD_DIM), F32)],
        compiler_params=_params(2, parallel=2),
    )(*outs, *lses)


def _rope_tables(s):
    half = HEAD_DIM // 2
    inv = ROPE_THETA ** (-jnp.arange(half, dtype=F32) / half)
    ang = jnp.arange(s, dtype=F32)[:, None] * inv[None, :]
    cos, sin = jnp.cos(ang), jnp.sin(ang)
    return jnp.concatenate([cos, cos], axis=-1), jnp.concatenate([-sin, sin], axis=-1)


def _to_pattern(t, dil):
    n, s, w = t.shape
    return t.reshape(n, s // dil, dil, w).transpose(0, 2, 1, 3)


def _from_pattern(t):
    n, dil, sub, w = t.shape
    return t.transpose(0, 2, 1, 3).reshape(n, dil * sub, w)


def _heads_major(t):
    s, w = t.shape
    return t.reshape(s, w // HEAD_DIM, HEAD_DIM).transpose(1, 0, 2)


def _heads_minor(t):
    n, s, w = t.shape
    return t.transpose(1, 0, 2).reshape(s, n * w)


def _conv_chunks(s):
    for k in range(s // CONV_ROWS):
        lo = max(0, k * CONV_ROWS - CONV_HALO)
        hi = min(s, (k + 1) * CONV_ROWS + CONV_HALO)
        yield k * CONV_ROWS, lo, hi


def _neighbours(p, lo, s):
    n = p.shape[0]
    row = lo + lax.broadcasted_iota(jnp.int32, p.shape, 0)
    prev = jnp.where(row == 0, 0.0, pltpu.roll(p, 1, axis=0))
    nxt = jnp.where(row == s - 1, 0.0, pltpu.roll(p, n - 1, axis=0))
    return prev, nxt


def _ab_mix(attn, proj, conv_w, aw):
    s = proj.shape[0]
    nt = aw // LANES

    def col(group, sel):
        return pl.BlockSpec((s, LANES), lambda i: (0, group * nt + sel(i)))

    a_sel = lambda i: jnp.minimum(i, nt - 1)
    b_sel = lambda i: jnp.maximum(i - nt, 0)

    def body(attn_ref, za_ref, ub_ref, gb_ref, gc_ref, zb_ref, w_ref, y_ref):
        i = pl.program_id(0)

        @pl.when(i < nt)
        def _():
            y_ref[...] = (attn_ref[...].astype(F32) * _silu(za_ref[...].astype(F32))).astype(BF16)

        @pl.when(i >= nt)
        def _():
            w = w_ref[...]
            for c0, lo, hi in _conv_chunks(s):
                p = gc_ref[lo:hi, :].astype(F32) * ub_ref[lo:hi, :].astype(F32)
                prev, nxt = _neighbours(p, lo, s)
                cv = w[0:1, :] * prev + w[1:2, :] * p + w[2:3, :] * nxt
                yb = gb_ref[lo:hi, :].astype(F32) * cv * _silu(zb_ref[lo:hi, :].astype(F32))
                y_ref[c0:c0 + CONV_ROWS, :] = yb[c0 - lo:c0 - lo + CONV_ROWS, :].astype(BF16)

    return pl.pallas_call(
        body, name="ab_mix", grid=(2 * nt,),
        in_specs=[pl.BlockSpec((s, LANES), lambda i: (0, a_sel(i))),
                  col(3, a_sel), col(4, b_sel), col(5, b_sel), col(6, b_sel), col(7, b_sel),
                  pl.BlockSpec((3, LANES), lambda i: (0, b_sel(i)))],
        out_specs=pl.BlockSpec((s, LANES), lambda i: (0, i)),
        out_shape=jax.ShapeDtypeStruct((s, 2 * aw), BF16),
        compiler_params=_params(1),
    )(attn, proj, proj, proj, proj, proj, conv_w)


def _ab_dattn(dy, proj, aw):
    s = proj.shape[0]
    tr = _tile(s, (512, 256, 128))

    def body(dy_ref, za_ref, o_ref):
        o_ref[...] = (dy_ref[...].astype(F32) * _silu(za_ref[...].astype(F32))).astype(BF16)

    return pl.pallas_call(
        body, name="ab_dattn", grid=(s // tr,),
        in_specs=[pl.BlockSpec((tr, aw), lambda i: (i, 0)), pl.BlockSpec((tr, aw), lambda i: (i, 3))],
        out_specs=pl.BlockSpec((tr, aw), lambda i: (i, 0)),
        out_shape=jax.ShapeDtypeStruct((s, aw), BF16),
        compiler_params=_params(1, parallel=1),
    )(dy, proj)


def _ab_dproj(dqkv, dy, attn, proj, conv_w, aw):
    s = proj.shape[0]
    nt = aw // LANES
    n_pat = len(dqkv)

    def col(group, sel):
        return pl.BlockSpec((s, LANES), lambda i: (0, group * nt + sel(i)))

    qkv_sel = lambda i: jnp.minimum(i, 3 * nt - 1)
    a_sel = lambda i: jnp.clip(i - 3 * nt, 0, nt - 1)
    b_sel = lambda i: jnp.maximum(i - 4 * nt, 0) % nt
    w_sel = lambda i: jnp.clip(i - 4 * nt, 0, nt - 1)

    def body(*refs):
        g_refs = refs[:n_pat]
        dya_ref, attn_ref, za_ref, dyb_ref, ub_ref, gb_ref, gc_ref, zb_ref, w_ref, out_ref, dw_ref = refs[n_pat:]
        i = pl.program_id(0)

        @pl.when(i < 3 * nt)
        def _():
            acc = g_refs[0][...].astype(F32)
            for r in g_refs[1:]:
                acc = acc + r[...].astype(F32)
            out_ref[...] = acc.astype(BF16)

        @pl.when(jnp.logical_and(i >= 3 * nt, i < 4 * nt))
        def _():
            out_ref[...] = (dya_ref[...].astype(F32) * attn_ref[...].astype(F32)
                            * _dsilu(za_ref[...].astype(F32))).astype(BF16)

        for which in range(4):
            @pl.when(jnp.logical_and(i >= (4 + which) * nt, i < (5 + which) * nt))
            def _(which=which):
                w = w_ref[...]
                dw = [jnp.zeros((1, LANES), F32) for _ in range(3)]
                for c0, lo, hi in _conv_chunks(s):
                    ctr = slice(c0 - lo, c0 - lo + CONV_ROWS)
                    ub = ub_ref[lo:hi, :].astype(F32)
                    gc = gc_ref[lo:hi, :].astype(F32)
                    gb = gb_ref[lo:hi, :].astype(F32)
                    zb = zb_ref[lo:hi, :].astype(F32)
                    dyb = dyb_ref[lo:hi, :].astype(F32)
                    p = gc * ub
                    prev, nxt = _neighbours(p, lo, s)
                    if which == 1:
                        cv = w[0:1, :] * prev + w[1:2, :] * p + w[2:3, :] * nxt
                        res = dyb * cv * _silu(zb)
                    elif which == 3:
                        cv = w[0:1, :] * prev + w[1:2, :] * p + w[2:3, :] * nxt
                        res = dyb * gb * cv * _dsilu(zb)
                    else:
                        dcv = dyb * gb * _silu(zb)
                        dprev, dnxt = _neighbours(dcv, lo, s)
                        dp = w[0:1, :] * dnxt + w[1:2, :] * dcv + w[2:3, :] * dprev
                        res = dp * (gc if which == 0 else ub)
                        if which == 0:
                            for t, nb in enumerate((prev, p, nxt)):
                                dw[t] = dw[t] + jnp.sum((dcv * nb)[ctr, :], axis=0, keepdims=True)
                    out_ref[c0:c0 + CONV_ROWS, :] = res[ctr, :].astype(BF16)
                if which == 0:
                    dw_ref[...] = jnp.concatenate(dw, axis=0)

    return pl.pallas_call(
        body, name="ab_dproj", grid=(8 * nt,),
        in_specs=[pl.BlockSpec((s, LANES), lambda i: (0, qkv_sel(i)))] * n_pat + [
            pl.BlockSpec((s, LANES), lambda i: (0, a_sel(i))),
            pl.BlockSpec((s, LANES), lambda i: (0, a_sel(i))),
            col(3, a_sel),
            pl.BlockSpec((s, LANES), lambda i: (0, nt + b_sel(i))),
            col(4, b_sel), col(5, b_sel), col(6, b_sel), col(7, b_sel),
            pl.BlockSpec((3, LANES), lambda i: (0, b_sel(i)))],
        out_specs=[pl.BlockSpec((s, LANES), lambda i: (0, i)),
                   pl.BlockSpec((3, LANES), lambda i: (0, w_sel(i)))],
        out_shape=[jax.ShapeDtypeStruct((s, 8 * aw), BF16), jax.ShapeDtypeStruct((3, aw), F32)],
        compiler_params=_params(1),
    )(*dqkv, dy, attn, proj, dy, proj, proj, proj, proj, conv_w)


def _sgu_norm(v, ln_g, ln_b):
    gv = _gelu(v)
    mu = jnp.mean(gv, axis=-1, keepdims=True)
    xc = gv - mu
    rstd = lax.rsqrt(jnp.mean(xc * xc, axis=-1, keepdims=True) + EPS)
    vhat = xc * rstd
    return vhat, rstd, vhat * ln_g + ln_b


def _sgu_fwd(uvz, ln_g, ln_b, w_s, b_s, cw):
    s = uvz.shape[0]
    tr = 2 * CHUNK if s % (2 * CHUNK) == 0 else CHUNK
    gw = cw // N_GROUPS

    def body(u_ref, v_ref, z_ref, g_ref, b_ref, ws_ref, bs_ref, y_ref):
        _, _, vn = _sgu_norm(v_ref[...].astype(F32), g_ref[...], b_ref[...])
        vn = vn.astype(BF16)
        for ch in range(tr // CHUNK):
            rows = slice(ch * CHUNK, (ch + 1) * CHUNK)
            for grp in range(N_GROUPS):
                cols = slice(grp * gw, (grp + 1) * gw)
                mixed = lax.dot_general(ws_ref[grp], vn[rows, cols], NN, preferred_element_type=F32) + bs_ref[grp]
                y_ref[rows, cols] = (_gelu(u_ref[rows, cols].astype(F32)) * mixed
                                     * _silu(z_ref[rows, cols].astype(F32))).astype(BF16)

    full3 = lambda shape: pl.BlockSpec(shape, lambda i: (0, 0, 0))
    return pl.pallas_call(
        body, name="sgu_fwd", grid=(s // tr,),
        in_specs=[pl.BlockSpec((tr, cw), lambda i: (i, 0)), pl.BlockSpec((tr, cw), lambda i: (i, 1)),
                  pl.BlockSpec((tr, cw), lambda i: (i, 2)), _vec_spec(cw), _vec_spec(cw),
                  full3(w_s.shape), full3(b_s.shape)],
        out_specs=pl.BlockSpec((tr, cw), lambda i: (i, 0)),
        out_shape=jax.ShapeDtypeStruct((s, cw), BF16),
        compiler_params=_params(1, parallel=1),
    )(uvz, uvz, uvz, ln_g, ln_b, w_s, b_s)


def _sgu_bwd(uvz, dy, ln_g, ln_b, w_s, b_s, cw):
    s = uvz.shape[0]
    tr = 2 * CHUNK if s % (2 * CHUNK) == 0 else CHUNK
    gw = cw // N_GROUPS

    def body(u_ref, v_ref, z_ref, dy_ref, g_ref, b_ref, ws_ref, bs_ref,
             du_ref, dv_ref, dz_ref, dws_ref, dbs_ref, dg_ref, db_ref, dvn_ref):
        vv = v_ref[...].astype(F32)
        gvec = g_ref[...]
        vhat, rstd, vn = _sgu_norm(vv, gvec, b_ref[...])
        vn = vn.astype(BF16)
        first = pl.program_id(0) == 0

        @pl.when(first)
        def _():
            dws_ref[...] = jnp.zeros_like(dws_ref)
            dbs_ref[...] = jnp.zeros_like(dbs_ref)

        for ch in range(tr // CHUNK):
            rows = slice(ch * CHUNK, (ch + 1) * CHUNK)
            for grp in range(N_GROUPS):
                cols = slice(grp * gw, (grp + 1) * gw)
                vn_g = vn[rows, cols]
                mixed = lax.dot_general(ws_ref[grp], vn_g, NN, preferred_element_type=F32) + bs_ref[grp]
                uu = u_ref[rows, cols].astype(F32)
                zz = z_ref[rows, cols].astype(F32)
                dyv = dy_ref[rows, cols].astype(F32)
                gu, sz = _gelu(uu), _silu(zz)
                du_ref[rows, cols] = (dyv * mixed * sz * _dgelu(uu)).astype(BF16)
                dz_ref[rows, cols] = (dyv * gu * mixed * _dsilu(zz)).astype(BF16)
                dmixed = dyv * gu * sz
                dm16 = dmixed.astype(BF16)
                dws_ref[grp] += lax.dot_general(dm16, vn_g, NT, preferred_element_type=F32)
                dbs_ref[grp] += jnp.broadcast_to(jnp.sum(dmixed, axis=1, keepdims=True), (CHUNK, LANES))
                dvn_ref[rows, cols] = lax.dot_general(ws_ref[grp], dm16, TN, preferred_element_type=F32)

        dvn = dvn_ref[...]
        _accumulate(dg_ref, jnp.sum(dvn * vhat, axis=0, keepdims=True))
        _accumulate(db_ref, jnp.sum(dvn, axis=0, keepdims=True))
        dvhat = dvn * gvec
        dgv = rstd * (dvhat - jnp.mean(dvhat, axis=-1, keepdims=True)
                      - vhat * jnp.mean(dvhat * vhat, axis=-1, keepdims=True))
        dv_ref[...] = (dgv * _dgelu(vv)).astype(BF16)

    full3 = lambda shape: pl.BlockSpec(shape, lambda i: (0, 0, 0))
    acc3 = jax.ShapeDtypeStruct((N_GROUPS, CHUNK, LANES), F32)
    vec = jax.ShapeDtypeStruct((1, cw), F32)
    act = jax.ShapeDtypeStruct((s, cw), BF16)
    row = pl.BlockSpec((tr, cw), lambda i: (i, 0))
    return pl.pallas_call(
        body, name="sgu_bwd", grid=(s // tr,),
        in_specs=[row, pl.BlockSpec((tr, cw), lambda i: (i, 1)), pl.BlockSpec((tr, cw), lambda i: (i, 2)),
                  row, _vec_spec(cw), _vec_spec(cw), full3(w_s.shape), full3(b_s.shape)],
        out_specs=[row, row, row, full3((N_GROUPS, CHUNK, LANES)), full3((N_GROUPS, CHUNK, LANES)),
                   _vec_spec(cw), _vec_spec(cw)],
        out_shape=[act, act, act, acc3, acc3, vec, vec],
        scratch_shapes=[pltpu.VMEM((tr, cw), F32)],
        compiler_params=_params(1),
    )(uvz, uvz, uvz, dy, ln_g, ln_b, w_s, b_s)


def _flat_rows(a):
    return a.reshape(-1, a.shape[-1])


def _add_sibling(grads, recv, layer_idx):
    _, nchip, k, n = grads.shape
    g2 = grads.reshape(2, nchip * k, n)
    r2 = recv.reshape(nchip * k, n)
    tr = _tile(nchip * k, (512, 256, 128))

    def body(c_ref, g_ref, r_ref, o_ref):
        o_ref[...] = (g_ref[...].astype(F32) + r_ref[...].astype(F32)).astype(BF16)

    out = pl.pallas_call(
        body, name="add_sibling",
        grid_spec=pltpu.PrefetchScalarGridSpec(
            num_scalar_prefetch=1, grid=(nchip * k // tr,),
            in_specs=[pl.BlockSpec((None, tr, n), lambda i, c: (c[0], i, 0)),
                      pl.BlockSpec((tr, n), lambda i, c: (i, 0))],
            out_specs=pl.BlockSpec((tr, n), lambda i, c: (i, 0))),
        out_shape=jax.ShapeDtypeStruct((nchip * k, n), BF16),
        compiler_params=_params(1, parallel=1),
    )(layer_idx, g2, r2)
    return out.reshape(nchip, k, n)


def _sum_chips(own, others, place):
    _, k, n = own.shape
    tr = _tile(k, (256, 128))

    def body(place_ref, own_ref, oth_ref, o_ref):
        acc = own_ref[...].astype(F32)
        for q in range(3):
            acc = acc + oth_ref[q].astype(F32)
        o_ref[...] = acc

    return pl.pallas_call(
        body, name="sum_chips",
        grid_spec=pltpu.PrefetchScalarGridSpec(
            num_scalar_prefetch=1, grid=(k // tr,),
            in_specs=[pl.BlockSpec((None, tr, n), lambda i, p: (p[0], i, 0)),
                      pl.BlockSpec((3, tr, n), lambda i, p: (0, i, 0))],
            out_specs=pl.BlockSpec((None, tr, n), lambda i, p: (p[1], i, 0))),
        out_shape=jax.ShapeDtypeStruct((2, k, n), F32),
        compiler_params=_params(1, parallel=1),
    )(place, own, others)


def _sum_devices(parts):
    nd, r, _ = parts.shape
    tr = _tile(r, (512, 256, 128, 64, 32, 16, 8))

    def body(p_ref, o_ref):
        acc = p_ref[0]
        for q in range(1, nd):
            acc = acc + p_ref[q]
        o_ref[...] = acc

    return pl.pallas_call(
        body, name="sum_devices", grid=(r // tr,),
        in_specs=[pl.BlockSpec((nd, tr, LANES), lambda i: (0, i, 0))],
        out_specs=pl.BlockSpec((tr, LANES), lambda i: (i, 0)),
        out_shape=jax.ShapeDtypeStruct((r, LANES), F32),
        compiler_params=_params(1, parallel=1),
    )(parts)


def _adamw(w, g, m, v):
    r, n = w.shape
    tr = _tile(r, [p for p in (1024, 512, 256, 128, 64, 32, 16, 8) if p * n <= ELEMENTWISE_BLOCK])

    def body(w_ref, g_ref, m_ref, v_ref, d_ref, nm_ref, nv_ref):
        gv = g_ref[...]
        nm = ADAM_B1 * m_ref[...] + (1.0 - ADAM_B1) * gv
        nv = ADAM_B2 * v_ref[...] + (1.0 - ADAM_B2) * (gv * gv)
        m_hat = nm / (1.0 - ADAM_B1 ** ADAM_STEP)
        v_hat = nv / (1.0 - ADAM_B2 ** ADAM_STEP)
        d_ref[...] = -ADAM_LR * (m_hat / (jnp.sqrt(v_hat) + ADAM_EPS) + ADAM_WD * w_ref[...])
        nm_ref[...] = nm
        nv_ref[...] = nv

    spec = pl.BlockSpec((tr, n), lambda i: (i, 0))
    shp = jax.ShapeDtypeStruct((r, n), F32)
    return pl.pallas_call(
        body, name="adamw", grid=(r // tr,),
        in_specs=[spec] * 4, out_specs=[spec] * 3, out_shape=[shp] * 3,
        compiler_params=_params(1, parallel=1),
    )(w, g, m, v)


def _pack(arrays, row_multiple=8):
    flat = [a.reshape(-1) for a in arrays]
    sizes = [f.shape[0] for f in flat]
    total = sum(sizes)
    unit = LANES * row_multiple
    padded = -(-total // unit) * unit
    if padded > total:
        flat.append(jnp.zeros((padded - total,), F32))
    offsets = [sum(sizes[:i]) for i in range(len(sizes))]
    return jnp.concatenate(flat).reshape(-1, LANES), offsets


def _unpack(packed, offsets, shapes):
    flat = packed.reshape(-1)
    return [flat[o:o + math.prod(s)].reshape(s) for o, s in zip(offsets, shapes)]


def kernel(x, c, ab_norm_g, ab_w_mod, ab_b_mod, ab_w_in, ab_conv_w, ab_w_out, sg_norm_g, sg_w_mod, sg_b_mod, sg_w_in, sg_ln_g, sg_ln_b, sg_w_s, sg_b_s, sg_w_out, final_norm_g, loss_target, m_ab_norm_g, m_ab_w_mod, m_ab_b_mod, m_ab_w_in, m_ab_conv_w, m_ab_w_out, m_sg_norm_g, m_sg_w_mod, m_sg_b_mod, m_sg_w_in, m_sg_ln_g, m_sg_ln_b, m_sg_w_s, m_sg_b_s, m_sg_w_out, m_final_norm_g, v_ab_norm_g, v_ab_w_mod, v_ab_b_mod, v_ab_w_in, v_ab_conv_w, v_ab_w_out, v_sg_norm_g, v_sg_w_mod, v_sg_b_mod, v_sg_w_in, v_sg_ln_g, v_sg_ln_b, v_sg_w_s, v_sg_b_s, v_sg_w_out, v_final_norm_g):
    s, d = x.shape[1], x.shape[2]
    aw = d // 2
    nh = aw // HEAD_DIM
    cw = d
    mod_l = ab_w_mod.shape[-1]
    x0 = x[0]
    target = loss_target[0]
    mx, my, mc = lax.axis_index("x"), lax.axis_index("y"), lax.axis_index("c")
    chip = 2 * mx + my
    me = 2 * chip + mc

    small_local = [c[0], ab_conv_w, sg_norm_g, sg_ln_g, sg_ln_b]
    small_shapes = [a.shape for a in small_local]
    payload, small_off = _pack(small_local)
    gathered = _all_to_all(jnp.broadcast_to(payload[None], (N_DEV,) + payload.shape), "gather_small")
    per_dev = [_unpack(gathered[b], small_off, small_shapes) for b in range(N_DEV)]
    c_all = jnp.stack([per_dev[b][0] for b in range(N_DEV)])

    def from_chips(idx, axis):
        return jnp.concatenate([per_dev[2 * q][idx] for q in range(N_CHIPS)], axis=axis)

    conv_w_full = from_chips(1, 2)
    sg_norm_g_full = from_chips(2, 1)
    sg_ln_g_full = from_chips(3, 1)
    sg_ln_b_full = from_chips(4, 1)

    ab_b_local = lax.dynamic_slice_in_dim(ab_b_mod, chip * mod_l, mod_l, axis=1)
    mod_rows = []
    for layer in range(4):
        i = layer // 2
        w_mod, bias = (ab_w_mod, ab_b_local) if layer % 2 == 0 else (sg_w_mod, sg_b_mod)
        mod_rows.append(_mod_fwd(c_all, w_mod, bias[i:i + 1], i))
    mod_local = jnp.stack(mod_rows, axis=1)
    mod_recv = _all_to_all(mod_local.reshape(N_DEV, -1, LANES), "exchange_mod")
    mod_recv = mod_recv.reshape(N_DEV, 4, mod_l)
    mod_full = jnp.concatenate([mod_recv[2 * q] for q in range(N_CHIPS)], axis=-1)
    shifts = [mod_full[l:l + 1, :d] for l in range(4)]
    scales = [mod_full[l:l + 1, d:2 * d] for l in range(4)]
    gates = [mod_full[l:l + 1, 2 * d:] for l in range(4)]

    chip_idx = jnp.reshape(chip, (1,)).astype(jnp.int32)
    w_in_ab, w_out_ab, w_in_sg, w_out_sg = _gather_weights(
        [_place_own_shard(w, chip_idx) for w in (ab_w_in, ab_w_out, sg_w_in, sg_w_out)])
    w_out_ab2 = w_out_ab.reshape(2, -1, d)
    w_out_sg2 = w_out_sg.reshape(2, -1, d)

    cos, sin = _rope_tables(s)
    tables = {dil: (_to_pattern(cos[None], dil)[0], _to_pattern(sin[None], dil)[0]) for dil in DILATIONS}
    w_s16 = sg_w_s.astype(BF16)
    b_s3 = sg_b_s[..., None]

    saved = []
    xs = x0
    for layer in range(4):
        i = layer // 2
        if layer % 2 == 0:
            h = _prenorm(xs, ab_norm_g[i:i + 1], scales[layer], shifts[layer])
            proj = _in_proj(h, w_in_ab, i)
            qkv = _heads_major(proj[:, :3 * aw])
            outs, lses = [], []
            for dil in DILATIONS:
                o_p, l_p = _attn_fwd(_to_pattern(qkv, dil), *tables[dil])
                outs.append(_from_pattern(o_p))
                lses.append(_from_pattern(l_p))
            attn_hm, lse_hm = _attn_combine(outs, lses)
            attn = _heads_minor(attn_hm)
            y = _ab_mix(attn, proj, conv_w_full[i], aw)
            x_next, out = _out_proj_residual(y, w_out_ab2, i, xs, gates[layer])
            saved.append((xs, h, proj, y, out, qkv, attn_hm, lse_hm, attn))
        else:
            h = _prenorm(xs, sg_norm_g_full[i:i + 1], scales[layer], shifts[layer])
            uvz = _in_proj(h, w_in_sg, i)
            y = _sgu_fwd(uvz, sg_ln_g_full[i:i + 1], sg_ln_b_full[i:i + 1], w_s16[i], b_s3[i], cw)
            x_next, out = _out_proj_residual(y, w_out_sg2, i, xs, gates[layer])
            saved.append((xs, h, uvz, y, out))
        xs = x_next

    loss11, dx, d_final_g = _final_loss(xs, target, final_norm_g[None])
    loss = lax.psum(loss11[0, 0], ("x", "y", "c"))

    g_in_ab = jnp.zeros(w_in_ab.shape, BF16)
    g_out_ab = jnp.zeros(w_out_ab2.shape, BF16)
    g_in_sg = jnp.zeros(w_in_sg.shape, BF16)
    g_out_sg = jnp.zeros(w_out_sg2.shape, BF16)
    dmods = [None] * 4
    d_ab_norm_g, d_sg_norm_g = [None, None], [None, None]
    d_conv_w, d_ln_g, d_ln_b, d_w_s, d_b_s = ([None, None] for _ in range(5))
    for layer in reversed(range(4)):
        i = layer // 2
        if layer % 2 == 0:
            xs, h, proj, y, out, qkv, attn_hm, lse_hm, attn = saved[layer]
            dout, dgate = _gate_bwd(dx, out, gates[layer])
            dy = _out_proj_bwd_act(dout, w_out_ab2, i)
            g_out_ab = _out_proj_bwd_w(y, dout, g_out_ab, i)
            do_hm = _heads_major(_ab_dattn(dy, proj, aw))
            dqkv = []
            for dil in DILATIONS:
                dq, dk, dv = _attn_bwd(_to_pattern(qkv, dil), *tables[dil], _to_pattern(do_hm, dil),
                                       _to_pattern(attn_hm, dil), _to_pattern(lse_hm, dil))
                dqkv.append(_heads_minor(jnp.concatenate(
                    [_from_pattern(dq), _from_pattern(dk), _from_pattern(dv)], axis=0)))
            dproj, d_conv_w[i] = _ab_dproj(dqkv, dy, attn, proj, conv_w_full[i], aw)
            dh = _in_proj_bwd_act(dproj, w_in_ab, i)
            g_in_ab = _in_proj_bwd_w(h, dproj, g_in_ab, i)
            dx, dshift, dscale, d_ab_norm_g[i] = _prenorm_bwd(xs, dh, dx, ab_norm_g[i:i + 1], scales[layer])
        else:
            xs, h, uvz, y, out = saved[layer]
            dout, dgate = _gate_bwd(dx, out, gates[layer])
            dy = _out_proj_bwd_act(dout, w_out_sg2, i)
            g_out_sg = _out_proj_bwd_w(y, dout, g_out_sg, i)
            du, dv, dz, d_w_s[i], db_wide, d_ln_g[i], d_ln_b[i] = _sgu_bwd(
                uvz, dy, sg_ln_g_full[i:i + 1], sg_ln_b_full[i:i + 1], w_s16[i], b_s3[i], cw)
            d_b_s[i] = db_wide[:, :, 0]
            duvz = jnp.concatenate([du, dv, dz], axis=1)
            dh = _in_proj_bwd_act(duvz, w_in_sg, i)
            g_in_sg = _in_proj_bwd_w(h, duvz, g_in_sg, i)
            dx, dshift, dscale, d_sg_norm_g[i] = _prenorm_bwd(xs, dh, dx, sg_norm_g_full[i:i + 1], scales[layer])
        dmods[layer] = jnp.concatenate([dshift, dscale, dgate], axis=1)
    grad_x = dx[None]

    partial_list = [jnp.concatenate(dmods, axis=0),
                    jnp.concatenate(d_ab_norm_g, axis=0), jnp.concatenate(d_sg_norm_g, axis=0), d_final_g[0],
                    jnp.stack(d_conv_w), jnp.concatenate(d_ln_g, axis=0), jnp.concatenate(d_ln_b, axis=0),
                    jnp.stack(d_w_s), jnp.stack(d_b_s)]
    partial_shapes = [a.shape for a in partial_list]
    partials, part_off = _pack(partial_list)
    all_partials = _all_to_all(jnp.broadcast_to(partials[None], (N_DEV,) + partials.shape), "gather_partials")
    reduced = _unpack(_sum_devices(all_partials), part_off, partial_shapes)
    (g_mod_bias, g_ab_norm_g, g_sg_norm_g_full, g_final_g, g_conv_full, g_ln_g_full, g_ln_b_full,
     g_w_s, g_b_s) = reduced
    dm_all = jnp.stack([_unpack(all_partials[b], part_off[:1], partial_shapes[:1])[0] for b in range(N_DEV)])
    dm_local = lax.dynamic_slice_in_dim(dm_all, chip * mod_l, mod_l, axis=2)

    def chip_cols(a, axis):
        width = a.shape[axis] // N_CHIPS
        return lax.dynamic_slice_in_dim(a, chip * width, width, axis=axis)

    g_ab_b_mod = jnp.stack([g_mod_bias[0], g_mod_bias[2]])
    g_sg_b_mod = chip_cols(jnp.stack([g_mod_bias[1], g_mod_bias[3]]), 1)
    g_ab_w_mod = jnp.stack([_mod_bwd_w(c_all, dm_local[:, 0]), _mod_bwd_w(c_all, dm_local[:, 2])])
    g_sg_w_mod = jnp.stack([_mod_bwd_w(c_all, dm_local[:, 1]), _mod_bwd_w(c_all, dm_local[:, 3])])
    g_conv = chip_cols(g_conv_full, 2)
    g_sg_norm_g = chip_cols(g_sg_norm_g_full, 1)
    g_ln_g = chip_cols(g_ln_g_full, 1)
    g_ln_b = chip_cols(g_ln_b_full, 1)

    big = [g_in_ab, g_out_ab.reshape(w_out_ab.shape), g_in_sg, g_out_sg.reshape(w_out_sg.shape)]
    from_sib = _swap_layers_with_sibling(big)
    layer_idx = jnp.reshape(mc, (1,)).astype(jnp.int32)
    chip_sums = [_add_sibling(g, r, layer_idx) for g, r in zip(big, from_sib)]
    from_chips_sums = _scatter_chip_sums(chip_sums)
    place = jnp.stack([chip, mc]).astype(jnp.int32)
    reduced_big = [_sum_chips(own, oth, place) for own, oth in zip(chip_sums, from_chips_sums)]
    g_ab_w_in, g_ab_w_out, g_sg_w_in, g_sg_w_out = _share_layer_with_sibling(reduced_big)

    def step_big(w, g, m, v):
        dl, nm, nv = _adamw(_flat_rows(w), _flat_rows(g), _flat_rows(m), _flat_rows(v))
        return dl.reshape(w.shape), nm.reshape(w.shape), nv.reshape(w.shape)

    big_out = {
        "ab_w_mod": step_big(ab_w_mod, g_ab_w_mod, m_ab_w_mod, v_ab_w_mod),
        "ab_w_in": step_big(ab_w_in, g_ab_w_in, m_ab_w_in, v_ab_w_in),
        "ab_w_out": step_big(ab_w_out, g_ab_w_out, m_ab_w_out, v_ab_w_out),
        "sg_w_mod": step_big(sg_w_mod, g_sg_w_mod, m_sg_w_mod, v_sg_w_mod),
        "sg_w_in": step_big(sg_w_in, g_sg_w_in, m_sg_w_in, v_sg_w_in),
        "sg_w_out": step_big(sg_w_out, g_sg_w_out, m_sg_w_out, v_sg_w_out),
    }
    small_names = ["ab_norm_g", "ab_b_mod", "ab_conv_w", "sg_norm_g", "sg_b_mod", "sg_ln_g", "sg_ln_b",
                   "sg_w_s", "sg_b_s", "final_norm_g"]
    small_w = [ab_norm_g, ab_b_mod, ab_conv_w, sg_norm_g, sg_b_mod, sg_ln_g, sg_ln_b, sg_w_s, sg_b_s, final_norm_g]
    small_g = [g_ab_norm_g, g_ab_b_mod, g_conv, g_sg_norm_g, g_sg_b_mod, g_ln_g, g_ln_b, g_w_s, g_b_s, g_final_g]
    small_m = [m_ab_norm_g, m_ab_b_mod, m_ab_conv_w, m_sg_norm_g, m_sg_b_mod, m_sg_ln_g, m_sg_ln_b, m_sg_w_s,
               m_sg_b_s, m_final_norm_g]
    small_v = [v_ab_norm_g, v_ab_b_mod, v_ab_conv_w, v_sg_norm_g, v_sg_b_mod, v_sg_ln_g, v_sg_ln_b, v_sg_w_s,
               v_sg_b_s, v_final_norm_g]
    shapes = [a.shape for a in small_w]
    pw, off = _pack(small_w)
    pg, _ = _pack(small_g)
    pm, _ = _pack(small_m)
    pv, _ = _pack(small_v)
    pd, pnm, pnv = _adamw(pw, pg, pm, pv)
    small_out = {}
    for name, dl, nm, nv in zip(small_names, _unpack(pd, off, shapes), _unpack(pnm, off, shapes),
                                _unpack(pnv, off, shapes)):
        small_out[name] = (dl, nm, nv)

    grads = {
        "ab_norm_g": g_ab_norm_g, "ab_w_mod": g_ab_w_mod, "ab_b_mod": g_ab_b_mod, "ab_w_in": g_ab_w_in,
        "ab_conv_w": g_conv, "ab_w_out": g_ab_w_out, "sg_norm_g": g_sg_norm_g, "sg_w_mod": g_sg_w_mod,
        "sg_b_mod": g_sg_b_mod, "sg_w_in": g_sg_w_in, "sg_ln_g": g_ln_g, "sg_ln_b": g_ln_b, "sg_w_s": g_w_s,
        "sg_b_s": g_b_s, "sg_w_out": g_sg_w_out, "final_norm_g": g_final_g,
    }
    order = ["ab_norm_g", "ab_w_mod", "ab_b_mod", "ab_w_in", "ab_conv_w", "ab_w_out", "sg_norm_g", "sg_w_mod",
             "sg_b_mod", "sg_w_in", "sg_ln_g", "sg_ln_b", "sg_w_s", "sg_b_s", "sg_w_out", "final_norm_g"]
    steps = {**big_out, **small_out}
    return (loss, grad_x, *[grads[n] for n in order], *[steps[n][0] for n in order],
            *[steps[n][1] for n in order], *[steps[n][2] for n in order])
```

```python
import math

import jax
import jax.numpy as jnp
from jax import lax
from jax.experimental import pallas as pl
from jax.experimental.pallas import tpu as pltpu

F32 = jnp.float32
BF16 = jnp.bfloat16

HEAD_DIM = 128
RADIUS = 64
DILATIONS = (1, 4, 16)
Q_BLOCK = 256
K_WINDOW = Q_BLOCK + 2 * RADIUS
ROPE_THETA = 10000.0
NEG_INF = -1e30
N_GROUPS = 8
CHUNK = 128
EPS = 1e-6
CONV_ROWS = 512
CONV_HALO = 16
LANES = 128
ELEMENTWISE_BLOCK = 256 * 1024
N_DEV = 8
N_CHIPS = 4

ADAM_LR = 0.001
ADAM_B1 = 0.9
ADAM_B2 = 0.999
ADAM_EPS = 1e-08
ADAM_WD = 0.01
ADAM_STEP = 10

VMEM_LIMIT_V7X = 56 * 1024 * 1024

MESH_ID = pl.DeviceIdType.MESH
HBM_SPEC = pl.BlockSpec(memory_space=pltpu.HBM)

NN = (((1,), (0,)), ((), ()))
NT = (((1,), (1,)), ((), ()))
TN = (((0,), (0,)), ((), ()))


def _params(n_grid, parallel=0):
    sem = tuple(["parallel"] * parallel + ["arbitrary"] * (n_grid - parallel))
    return pltpu.CompilerParams(dimension_semantics=sem, vmem_limit_bytes=VMEM_LIMIT_V7X)


def _tile(n, prefs):
    for p in prefs:
        if n % p == 0:
            return p
    return n


def _sigmoid(z):
    return 1.0 / (1.0 + jnp.exp(-z))


def _silu(z):
    return z * _sigmoid(z)


def _dsilu(z):
    s = _sigmoid(z)
    return s * (1.0 + z * (1.0 - s))


_GELU_K = math.sqrt(2.0 / math.pi)
_GELU_C = 0.044715


def _gelu(u):
    return 0.5 * u * (1.0 + jnp.tanh(_GELU_K * (u + _GELU_C * u * u * u)))


def _dgelu(u):
    t = jnp.tanh(_GELU_K * (u + _GELU_C * u * u * u))
    return 0.5 * (1.0 + t) + 0.5 * u * (1.0 - t * t) * _GELU_K * (1.0 + 3.0 * _GELU_C * u * u)


def _my_place():
    return lax.axis_index("x"), lax.axis_index("y"), lax.axis_index("c")


def _flip(v, bit):
    return 1 - v if bit else v


def _all_to_all(x, name):
    def body(x_ref, y_ref, send_sems, recv_sems, own_sem):
        mx, my, mc = _my_place()
        me = 4 * mx + 2 * my + mc
        own = pltpu.make_async_copy(x_ref.at[me], y_ref.at[me], own_sem)
        own.start()
        copies = []
        for k in range(1, N_DEV):
            px, py, pc = _flip(mx, (k >> 2) & 1), _flip(my, (k >> 1) & 1), _flip(mc, k & 1)
            peer = 4 * px + 2 * py + pc
            cp = pltpu.make_async_remote_copy(
                src_ref=x_ref.at[peer], dst_ref=y_ref.at[me],
                send_sem=send_sems.at[k - 1], recv_sem=recv_sems.at[k - 1],
                device_id=(px, py, pc), device_id_type=MESH_ID)
            cp.start()
            copies.append(cp)
        for cp in copies:
            cp.wait()
        own.wait()

    return pl.pallas_call(
        body, name=name,
        out_shape=jax.ShapeDtypeStruct(x.shape, x.dtype),
        in_specs=[HBM_SPEC], out_specs=HBM_SPEC,
        scratch_shapes=[pltpu.SemaphoreType.DMA((N_DEV - 1,)), pltpu.SemaphoreType.DMA((N_DEV - 1,)),
                        pltpu.SemaphoreType.DMA],
    )(x)


def _other_chips(mx, my):
    return [(1 - mx, my), (mx, 1 - my), (1 - mx, 1 - my)]


def _place_own_shard(w, chip_idx):
    _, k, n = w.shape
    tr = _tile(k, (512, 256, 128))

    def body(c_ref, w_ref, g_ref):
        g_ref[...] = w_ref[...].astype(BF16)

    return pl.pallas_call(
        body, name="place_own_shard",
        grid_spec=pltpu.PrefetchScalarGridSpec(
            num_scalar_prefetch=1, grid=(2, k // tr),
            in_specs=[pl.BlockSpec((None, tr, n), lambda i, r, c: (i, r, 0))],
            out_specs=pl.BlockSpec((None, None, tr, n), lambda i, r, c: (i, c[0], r, 0))),
        out_shape=jax.ShapeDtypeStruct((2, N_CHIPS, k, n), BF16),
        compiler_params=_params(2, parallel=2),
    )(chip_idx, w)


def _gather_weights(placed):
    n = len(placed)

    def body(*refs):
        w = refs[:n]
        g = refs[n:2 * n]
        send_sems, recv_sems, fsend_sems, frecv_sems = refs[2 * n:]
        mx, my, mc = _my_place()
        j = 2 * mx + my
        chips = _other_chips(mx, my)
        first, passed = [], []
        for t in range(n):
            for q, (px, py) in enumerate(chips):
                cp = pltpu.make_async_remote_copy(
                    src_ref=w[t].at[mc, j], dst_ref=g[t].at[mc, j],
                    send_sem=send_sems.at[3 * t + q], recv_sem=recv_sems.at[3 * t + q],
                    device_id=(px, py, mc), device_id_type=MESH_ID)
                cp.start()
                first.append(cp)
        for t in range(n):
            for q, (px, py) in enumerate(chips):
                jq = 2 * px + py
                landed = pltpu.make_async_remote_copy(
                    src_ref=w[t].at[mc, j], dst_ref=g[t].at[mc, jq],
                    send_sem=send_sems.at[3 * t + q], recv_sem=recv_sems.at[3 * t + q],
                    device_id=(px, py, mc), device_id_type=MESH_ID)
                landed.wait_recv()
                fwd = pltpu.make_async_remote_copy(
                    src_ref=g[t].at[mc, jq], dst_ref=g[t].at[mc, jq],
                    send_sem=fsend_sems.at[3 * t + q], recv_sem=frecv_sems.at[3 * t + q],
                    device_id=(mx, my, 1 - mc), device_id_type=MESH_ID)
                fwd.start()
                passed.append(fwd)
        for t in range(n):
            for q, (px, py) in enumerate(chips):
                jq = 2 * px + py
                from_sibling = pltpu.make_async_remote_copy(
                    src_ref=g[t].at[1 - mc, jq], dst_ref=g[t].at[1 - mc, jq],
                    send_sem=fsend_sems.at[3 * t + q], recv_sem=frecv_sems.at[3 * t + q],
                    device_id=(mx, my, 1 - mc), device_id_type=MESH_ID)
                from_sibling.wait_recv()
        for cp in first + passed:
            cp.wait_send()

    return pl.pallas_call(
        body, name="gather_weights",
        out_shape=[jax.ShapeDtypeStruct(a.shape, a.dtype) for a in placed],
        in_specs=[HBM_SPEC] * n, out_specs=[HBM_SPEC] * n,
        input_output_aliases={t: t for t in range(n)},
        scratch_shapes=[pltpu.SemaphoreType.DMA((3 * n,)), pltpu.SemaphoreType.DMA((3 * n,)),
                        pltpu.SemaphoreType.DMA((3 * n,)), pltpu.SemaphoreType.DMA((3 * n,))],
    )(*placed)


def _swap_layers_with_sibling(grads):
    n = len(grads)

    def body(*refs):
        g = refs[:n]
        r = refs[n:2 * n]
        send_sems, recv_sems = refs[2 * n:]
        mx, my, mc = _my_place()
        copies = []
        for t in range(n):
            cp = pltpu.make_async_remote_copy(
                src_ref=g[t].at[1 - mc], dst_ref=r[t],
                send_sem=send_sems.at[t], recv_sem=recv_sems.at[t],
                device_id=(mx, my, 1 - mc), device_id_type=MESH_ID)
            cp.start()
            copies.append(cp)
        for cp in copies:
            cp.wait()

    return pl.pallas_call(
        body, name="grads_to_sibling",
        out_shape=[jax.ShapeDtypeStruct(a.shape[1:], a.dtype) for a in grads],
        in_specs=[HBM_SPEC] * n, out_specs=[HBM_SPEC] * n,
        scratch_shapes=[pltpu.SemaphoreType.DMA((n,)), pltpu.SemaphoreType.DMA((n,))],
    )(*grads)


def _scatter_chip_sums(sums):
    n = len(sums)

    def body(*refs):
        s = refs[:n]
        r = refs[n:2 * n]
        send_sems, recv_sems = refs[2 * n:]
        mx, my, mc = _my_place()
        copies = []
        for t in range(n):
            for q, (px, py) in enumerate(_other_chips(mx, my)):
                cp = pltpu.make_async_remote_copy(
                    src_ref=s[t].at[2 * px + py], dst_ref=r[t].at[q],
                    send_sem=send_sems.at[3 * t + q], recv_sem=recv_sems.at[3 * t + q],
                    device_id=(px, py, mc), device_id_type=MESH_ID)
                cp.start()
                copies.append(cp)
        for cp in copies:
            cp.wait()

    return pl.pallas_call(
        body, name="chip_sums_to_owners",
        out_shape=[jax.ShapeDtypeStruct((3,) + a.shape[1:], a.dtype) for a in sums],
        in_specs=[HBM_SPEC] * n, out_specs=[HBM_SPEC] * n,
        scratch_shapes=[pltpu.SemaphoreType.DMA((3 * n,)), pltpu.SemaphoreType.DMA((3 * n,))],
    )(*sums)


def _share_layer_with_sibling(reduced):
    n = len(reduced)

    def body(*refs):
        h = refs[:n]
        f = refs[n:2 * n]
        send_sems, recv_sems = refs[2 * n:]
        mx, my, mc = _my_place()
        copies = []
        for t in range(n):
            rc = pltpu.make_async_remote_copy(
                src_ref=h[t].at[mc], dst_ref=f[t].at[mc],
                send_sem=send_sems.at[t], recv_sem=recv_sems.at[t],
                device_id=(mx, my, 1 - mc), device_id_type=MESH_ID)
            rc.start()
            copies.append(rc)
        for cp in copies:
            cp.wait()

    return pl.pallas_call(
        body, name="reduced_grads_to_sibling",
        out_shape=[jax.ShapeDtypeStruct(a.shape, a.dtype) for a in reduced],
        in_specs=[HBM_SPEC] * n, out_specs=[HBM_SPEC] * n,
        input_output_aliases={t: t for t in range(n)},
        scratch_shapes=[pltpu.SemaphoreType.DMA((n,)), pltpu.SemaphoreType.DMA((n,))],
    )(*reduced)


def _matmul(name, operands, in_specs, grid, nk, dims, out_shape, out_specs, acc_shape,
            epilogue, a_prologue=None, aliases=None):
    n_in = len(operands)

    def body(*refs):
        a_ref, b_ref = refs[0], refs[1]
        extra = refs[2:n_in]
        outs = refs[n_in:-1] if nk > 1 else refs[n_in:]
        a = a_ref[...]
        if a_prologue is not None:
            a = a_prologue(a)
        part = lax.dot_general(a.astype(BF16), b_ref[...].astype(BF16), dims, preferred_element_type=F32)
        if nk == 1:
            epilogue(part, extra, outs)
        else:
            acc = refs[-1]
            k = pl.program_id(2)

            @pl.when(k == 0)
            def _():
                acc[...] = part

            @pl.when(k > 0)
            def _():
                acc[...] += part

            @pl.when(k == nk - 1)
            def _():
                epilogue(acc[...], extra, outs)

    return pl.pallas_call(
        body, name=name, grid=grid, in_specs=in_specs, out_specs=out_specs, out_shape=out_shape,
        scratch_shapes=[pltpu.VMEM(acc_shape, F32)] if nk > 1 else [],
        input_output_aliases=aliases or {},
        compiler_params=_params(3, parallel=2),
    )(*operands)


def _store_cast(acc, extra, outs):
    outs[0][...] = acc.astype(outs[0].dtype)


def _in_proj(h, w_g, layer):
    s, d = h.shape
    nl = w_g.shape[-1]
    tm = _tile(s, (1024, 512, 256))
    tn = _tile(nl, (1024, 768, 512, 384, 256, 128))
    per = nl // tn
    return _matmul(
        "in_proj", (h, w_g),
        [pl.BlockSpec((tm, d), lambda i, j, k: (i, 0)),
         pl.BlockSpec((None, None, d, tn), lambda i, j, k: (layer, j // per, 0, j % per))],
        (s // tm, N_CHIPS * per, 1), 1, NN,
        jax.ShapeDtypeStruct((s, N_CHIPS * nl), BF16),
        pl.BlockSpec((tm, tn), lambda i, j, k: (i, j)), (tm, tn), _store_cast)


def _out_proj_residual(y, w2, layer, x, gate):
    s, wdt = y.shape
    d = w2.shape[-1]
    tm = _tile(s, (1024, 512, 256))
    tn = _tile(d, (1024, 512, 256, 128))

    def epilogue(acc, extra, outs):
        x_ref, gate_ref = extra
        outs[0][...] = x_ref[...] + gate_ref[...] * acc
        outs[1][...] = acc.astype(BF16)

    return _matmul(
        "out_proj", (y, w2, x, gate),
        [pl.BlockSpec((tm, wdt), lambda i, j, k: (i, 0)),
         pl.BlockSpec((None, wdt, tn), lambda i, j, k: (layer, 0, j)),
         pl.BlockSpec((tm, tn), lambda i, j, k: (i, j)),
         pl.BlockSpec((1, tn), lambda i, j, k: (0, j))],
        (s // tm, d // tn, 1), 1, NN,
        [jax.ShapeDtypeStruct((s, d), F32), jax.ShapeDtypeStruct((s, d), BF16)],
        [pl.BlockSpec((tm, tn), lambda i, j, k: (i, j)), pl.BlockSpec((tm, tn), lambda i, j, k: (i, j))],
        (tm, tn), epilogue)


def _out_proj_bwd_act(dout, w2, layer):
    s, d = dout.shape
    wdt = w2.shape[1]
    tm = _tile(s, (1024, 512, 256))
    tn = _tile(wdt, (1024, 512, 256, 128))
    return _matmul(
        "out_proj_dy", (dout, w2),
        [pl.BlockSpec((tm, d), lambda i, j, k: (i, 0)),
         pl.BlockSpec((None, tn, d), lambda i, j, k: (layer, j, 0))],
        (s // tm, wdt // tn, 1), 1, NT,
        jax.ShapeDtypeStruct((s, wdt), BF16),
        pl.BlockSpec((tm, tn), lambda i, j, k: (i, j)), (tm, tn), _store_cast)


def _out_proj_bwd_w(y, dout, buf, layer):
    s, wdt = y.shape
    d = dout.shape[1]
    tm = _tile(wdt, (1024, 512, 256, 128))
    tn = _tile(d, (1024, 512, 256, 128))
    ts = _tile(s, (1024, 512, 256))
    nk = s // ts

    def epilogue(acc, extra, outs):
        outs[0][...] = acc.astype(BF16)

    return _matmul(
        "out_proj_dw", (y, dout, buf),
        [pl.BlockSpec((ts, tm), lambda i, j, k: (k, i)),
         pl.BlockSpec((ts, tn), lambda i, j, k: (k, j)),
         HBM_SPEC],
        (wdt // tm, d // tn, nk), nk, TN,
        jax.ShapeDtypeStruct(buf.shape, buf.dtype),
        pl.BlockSpec((None, tm, tn), lambda i, j, k: (layer, i, j)), (tm, tn), epilogue,
        aliases={2: 0})


def _in_proj_bwd_act(dproj, w_g, layer):
    s, n_all = dproj.shape
    d, nl = w_g.shape[2], w_g.shape[3]
    tm = _tile(s, (1024, 512, 256))
    tn = _tile(d, (1024, 512, 256, 128))
    tc = _tile(nl, (1024, 768, 512, 384, 256, 128))
    per = nl // tc
    nk = N_CHIPS * per
    return _matmul(
        "in_proj_dh", (dproj, w_g),
        [pl.BlockSpec((tm, tc), lambda i, j, k: (i, k)),
         pl.BlockSpec((None, None, tn, tc), lambda i, j, k: (layer, k // per, j, k % per))],
        (s // tm, d // tn, nk), nk, NT,
        jax.ShapeDtypeStruct((s, d), BF16),
        pl.BlockSpec((tm, tn), lambda i, j, k: (i, j)), (tm, tn), _store_cast)


def _in_proj_bwd_w(h, dproj, buf, layer):
    s, d = h.shape
    nl = buf.shape[-1]
    tm = _tile(d, (1024, 512, 256, 128))
    tn = _tile(nl, (1024, 768, 512, 384, 256, 128))
    ts = _tile(s, (1024, 512, 256))
    per = nl // tn
    nk = s // ts

    def epilogue(acc, extra, outs):
        outs[0][...] = acc.astype(BF16)

    return _matmul(
        "in_proj_dw", (h, dproj, buf),
        [pl.BlockSpec((ts, tm), lambda i, j, k: (k, i)),
         pl.BlockSpec((ts, tn), lambda i, j, k: (k, j)),
         HBM_SPEC],
        (d // tm, N_CHIPS * per, nk), nk, TN,
        jax.ShapeDtypeStruct(buf.shape, buf.dtype),
        pl.BlockSpec((None, None, tm, tn), lambda i, j, k: (layer, j // per, i, j % per)), (tm, tn), epilogue,
        aliases={2: 0})


def _mod_fwd(c_all, w_mod, bias, layer):
    nb, d = c_all.shape
    nl = w_mod.shape[-1]
    tn = _tile(nl, (768, 512, 384, 256, 128))
    tk = _tile(d, (1024, 512, 256, 128))
    nk = d // tk

    def epilogue(acc, extra, outs):
        outs[0][...] = acc + extra[0][...]

    return _matmul(
        "mod_fwd", (c_all, w_mod, bias),
        [pl.BlockSpec((nb, tk), lambda i, j, k: (0, k)),
         pl.BlockSpec((None, tk, tn), lambda i, j, k: (layer, k, j)),
         pl.BlockSpec((1, tn), lambda i, j, k: (0, j))],
        (1, nl // tn, nk), nk, NN,
        jax.ShapeDtypeStruct((nb, nl), F32),
        pl.BlockSpec((nb, tn), lambda i, j, k: (0, j)), (nb, tn), epilogue, a_prologue=_silu)


def _mod_bwd_w(c_all, dm_local):
    nb, d = c_all.shape
    nl = dm_local.shape[-1]
    tm = _tile(d, (1024, 512, 256, 128))
    tn = _tile(nl, (768, 512, 384, 256, 128))

    def epilogue(acc, extra, outs):
        outs[0][...] = acc

    return _matmul(
        "mod_dw", (c_all, dm_local),
        [pl.BlockSpec((nb, tm), lambda i, j, k: (0, i)),
         pl.BlockSpec((nb, tn), lambda i, j, k: (0, j))],
        (d // tm, nl // tn, 1), 1, TN,
        jax.ShapeDtypeStruct((d, nl), F32),
        pl.BlockSpec((tm, tn), lambda i, j, k: (i, j)), (tm, tn), epilogue, a_prologue=_silu)


def _rows_call(name, body, operands, in_specs, out_shape, out_specs, n_tiles):
    return pl.pallas_call(
        body, name=name, grid=(n_tiles,), in_specs=in_specs, out_specs=out_specs, out_shape=out_shape,
        compiler_params=_params(1),
    )(*operands)


def _row_spec(tr, width):
    return pl.BlockSpec((tr, width), lambda i: (i, 0))


def _vec_spec(width):
    return pl.BlockSpec((1, width), lambda i: (0, 0))


def _accumulate(ref, val):
    first = pl.program_id(0) == 0

    @pl.when(first)
    def _():
        ref[...] = val

    @pl.when(jnp.logical_not(first))
    def _():
        ref[...] += val


def _prenorm(x, g, scale, shift):
    s, d = x.shape
    tr = _tile(s, (256, 128))

    def body(x_ref, g_ref, sc_ref, sh_ref, h_ref):
        xv = x_ref[...]
        rstd = lax.rsqrt(jnp.mean(xv * xv, axis=-1, keepdims=True) + EPS)
        h_ref[...] = ((xv * rstd) * g_ref[...] * (1.0 + sc_ref[...]) + sh_ref[...]).astype(BF16)

    return _rows_call("prenorm", body, (x, g, scale, shift),
                      [_row_spec(tr, d), _vec_spec(d), _vec_spec(d), _vec_spec(d)],
                      jax.ShapeDtypeStruct((s, d), BF16), _row_spec(tr, d), s // tr)


def _prenorm_bwd(x, dh, dres, g, scale):
    s, d = x.shape
    tr = _tile(s, (256, 128))

    def body(x_ref, dh_ref, dres_ref, g_ref, sc_ref, dx_ref, dshift_ref, dscale_ref, dg_ref):
        xv = x_ref[...]
        dhv = dh_ref[...].astype(F32)
        rstd = lax.rsqrt(jnp.mean(xv * xv, axis=-1, keepdims=True) + EPS)
        xhat = xv * rstd
        gv = g_ref[...]
        one_sc = 1.0 + sc_ref[...]
        dxhat = dhv * gv * one_sc
        dx_ref[...] = dres_ref[...] + rstd * (dxhat - xhat * jnp.mean(dxhat * xhat, axis=-1, keepdims=True))
        _accumulate(dshift_ref, jnp.sum(dhv, axis=0, keepdims=True))
        _accumulate(dscale_ref, jnp.sum(dhv * xhat * gv, axis=0, keepdims=True))
        _accumulate(dg_ref, jnp.sum(dhv * xhat * one_sc, axis=0, keepdims=True))

    vec = jax.ShapeDtypeStruct((1, d), F32)
    return _rows_call("prenorm_bwd", body, (x, dh, dres, g, scale),
                      [_row_spec(tr, d), _row_spec(tr, d), _row_spec(tr, d), _vec_spec(d), _vec_spec(d)],
                      [jax.ShapeDtypeStruct((s, d), F32), vec, vec, vec],
                      [_row_spec(tr, d), _vec_spec(d), _vec_spec(d), _vec_spec(d)], s // tr)


def _gate_bwd(dx, out, gate):
    s, d = dx.shape
    tr = _tile(s, (256, 128))

    def body(dx_ref, out_ref, gate_ref, dout_ref, dgate_ref):
        dxv = dx_ref[...]
        dout_ref[...] = (gate_ref[...] * dxv).astype(BF16)
        _accumulate(dgate_ref, jnp.sum(dxv * out_ref[...].astype(F32), axis=0, keepdims=True))

    return _rows_call("gate_bwd", body, (dx, out, gate),
                      [_row_spec(tr, d), _row_spec(tr, d), _vec_spec(d)],
                      [jax.ShapeDtypeStruct((s, d), BF16), jax.ShapeDtypeStruct((1, d), F32)],
                      [_row_spec(tr, d), _vec_spec(d)], s // tr)


def _final_loss(x, target, g):
    s, d = x.shape
    tr = _tile(s, (256, 128))
    n_tiles = s // tr

    def body(x_ref, t_ref, g_ref, loss_ref, dx_ref, dg_ref, acc_ref):
        xv = x_ref[...]
        rstd = lax.rsqrt(jnp.mean(xv * xv, axis=-1, keepdims=True) + EPS)
        xhat = xv * rstd
        gv = g_ref[...]
        err = xhat * gv - t_ref[...]
        dy = err * (1.0 / d)
        dxhat = dy * gv
        dx_ref[...] = rstd * (dxhat - xhat * jnp.mean(dxhat * xhat, axis=-1, keepdims=True))
        _accumulate(dg_ref, jnp.sum(dy * xhat, axis=0, keepdims=True))
        _accumulate(acc_ref, jnp.sum(err * err, axis=0, keepdims=True))

        @pl.when(pl.program_id(0) == n_tiles - 1)
        def _():
            loss_ref[...] = (0.5 / d) * jnp.sum(acc_ref[...], axis=1, keepdims=True)

    return pl.pallas_call(
        body, name="final_loss", grid=(n_tiles,),
        in_specs=[_row_spec(tr, d), _row_spec(tr, d), _vec_spec(d)],
        out_specs=[pl.BlockSpec((1, 1), lambda i: (0, 0)), _row_spec(tr, d), _vec_spec(d)],
        out_shape=[jax.ShapeDtypeStruct((1, 1), F32), jax.ShapeDtypeStruct((s, d), F32),
                   jax.ShapeDtypeStruct((1, d), F32)],
        scratch_shapes=[pltpu.VMEM((1, d), F32)],
        compiler_params=_params(1),
    )(x, target, g)


def _rope(t, cos, sin):
    return t * cos + pltpu.roll(t, HEAD_DIM // 2, axis=1) * sin


def _unrope(dt, cos, sin):
    return dt * cos + pltpu.roll(dt * sin, HEAD_DIM // 2, axis=1)


def _band_blocks(s, dil):
    sub = s // dil
    kw = min(K_WINDOW, sub)

    def rows(r, start, n):
        if dil == 1:
            return pl.ds(pl.multiple_of(start, RADIUS), n)
        return pl.ds(r + dil * start, n, stride=dil)

    def window(idx):
        nb = sub // Q_BLOCK
        r, b = idx // nb, idx % nb
        q0 = b * Q_BLOCK
        start = jnp.clip(q0 - RADIUS, 0, sub - kw)
        qi = q0 + lax.broadcasted_iota(jnp.int32, (Q_BLOCK, kw), 0)
        ki = start + lax.broadcasted_iota(jnp.int32, (Q_BLOCK, kw), 1)
        return rows(r, q0, Q_BLOCK), rows(r, start, kw), jnp.abs(qi - ki) <= RADIUS

    return window


def _head_col(s, group, nh):
    return pl.BlockSpec((s, HEAD_DIM), lambda h: (0, group * nh + h), pipeline_mode=pl.Buffered(1))


def _attn_fwd(proj, cos, sin, aw):
    s = proj.shape[0]
    nh = aw // HEAD_DIM
    scale = HEAD_DIM ** -0.5
    n_blocks = s // Q_BLOCK

    def body(q_ref, k_ref, v_ref, cos_ref, sin_ref, attn_ref, lse_ref, qf, kf, vf, acc):
        cosv, sinv = cos_ref[...], sin_ref[...]
        qf[...] = _rope(q_ref[...].astype(F32), cosv, sinv)
        kf[...] = _rope(k_ref[...].astype(F32), cosv, sinv)
        vf[...] = v_ref[...].astype(F32)

        for pattern, dil in enumerate(DILATIONS):
            window = _band_blocks(s, dil)

            def block(idx, carry, window=window, first=(pattern == 0)):
                q_rows, k_rows, valid = window(idx)
                q = qf[q_rows, :].astype(BF16)
                kk = kf[k_rows, :].astype(BF16)
                vv = vf[k_rows, :].astype(BF16)
                sc = lax.dot_general(q, kk, NT, preferred_element_type=F32) * scale
                sc = jnp.where(valid, sc, NEG_INF)
                m = jnp.max(sc, axis=1, keepdims=True)
                p = jnp.exp(sc - m)
                den = jnp.sum(p, axis=1, keepdims=True)
                o = lax.dot_general(p.astype(BF16), vv, NN, preferred_element_type=F32) / den
                lse = jnp.broadcast_to(m + jnp.log(den), (Q_BLOCK, HEAD_DIM))
                if first:
                    acc[q_rows, :] = o
                    lse_ref[q_rows, :] = lse
                else:
                    lse_old = lse_ref[q_rows, :]
                    top = jnp.maximum(lse_old, lse)
                    w_old, w_new = jnp.exp(lse_old - top), jnp.exp(lse - top)
                    tot = w_old + w_new
                    acc[q_rows, :] = (acc[q_rows, :] * w_old + o * w_new) / tot
                    lse_ref[q_rows, :] = top + jnp.log(tot)
                return carry

            lax.fori_loop(0, n_blocks, block, 0)

        attn_ref[...] = acc[...].astype(BF16)

    table = pl.BlockSpec((s, HEAD_DIM), lambda h: (0, 0), pipeline_mode=pl.Buffered(1))
    out = pl.BlockSpec((s, HEAD_DIM), lambda h: (0, h))
    return pl.pallas_call(
        body, name="attn_fwd", grid=(nh,),
        in_specs=[_head_col(s, 0, nh), _head_col(s, 1, nh), _head_col(s, 2, nh), table, table],
        out_specs=[out, out],
        out_shape=[jax.ShapeDtypeStruct((s, aw), BF16), jax.ShapeDtypeStruct((s, aw), F32)],
        scratch_shapes=[pltpu.VMEM((s, HEAD_DIM), F32)] * 4,
        compiler_params=_params(1, parallel=1),
    )(proj, proj, proj, cos, sin)


def _attn_bwd(proj, cos, sin, dy, attn, lse, aw):
    s = proj.shape[0]
    nh = aw // HEAD_DIM
    scale = HEAD_DIM ** -0.5
    n_blocks = s // Q_BLOCK

    def body(q_ref, k_ref, v_ref, za_ref, cos_ref, sin_ref, dy_ref, attn_ref, lse_ref,
             dq_ref, dk_ref, dv_ref, qf, kf, vf, dof, delta, dqa, dka, dva):
        cosv, sinv = cos_ref[...], sin_ref[...]
        qf[...] = _rope(q_ref[...].astype(F32), cosv, sinv)
        kf[...] = _rope(k_ref[...].astype(F32), cosv, sinv)
        vf[...] = v_ref[...].astype(F32)
        do_all = dy_ref[...].astype(F32) * _silu(za_ref[...].astype(F32))
        dof[...] = do_all
        delta[...] = jnp.broadcast_to(
            jnp.sum(do_all * attn_ref[...].astype(F32), axis=1, keepdims=True), (s, HEAD_DIM))
        dqa[...] = jnp.zeros_like(dqa)
        dka[...] = jnp.zeros_like(dka)
        dva[...] = jnp.zeros_like(dva)

        for dil in DILATIONS:
            window = _band_blocks(s, dil)

            def block(idx, carry, window=window):
                q_rows, k_rows, valid = window(idx)
                q = qf[q_rows, :].astype(BF16)
                kk = kf[k_rows, :].astype(BF16)
                vv = vf[k_rows, :].astype(BF16)
                dov = dof[q_rows, :].astype(BF16)
                lse_q = lse_ref[q_rows, :][:, 0:1]
                delta_q = delta[q_rows, :][:, 0:1]
                sc = lax.dot_general(q, kk, NT, preferred_element_type=F32) * scale
                p = jnp.where(valid, jnp.exp(sc - lse_q), 0.0)
                dp = lax.dot_general(dov, vv, NT, preferred_element_type=F32)
                ds = (p * (dp - delta_q) * scale).astype(BF16)
                dqa[q_rows, :] += lax.dot_general(ds, kk, NN, preferred_element_type=F32)
                dka[k_rows, :] += lax.dot_general(ds, q, TN, preferred_element_type=F32)
                dva[k_rows, :] += lax.dot_general(p.astype(BF16), dov, TN, preferred_element_type=F32)
                return carry

            lax.fori_loop(0, n_blocks, block, 0)

        dq_ref[...] = _unrope(dqa[...], cosv, sinv).astype(BF16)
        dk_ref[...] = _unrope(dka[...], cosv, sinv).astype(BF16)
        dv_ref[...] = dva[...].astype(BF16)

    own = pl.BlockSpec((s, HEAD_DIM), lambda h: (0, h), pipeline_mode=pl.Buffered(1))
    table = pl.BlockSpec((s, HEAD_DIM), lambda h: (0, 0), pipeline_mode=pl.Buffered(1))
    out = pl.BlockSpec((s, HEAD_DIM), lambda h: (0, h))
    shape = jax.ShapeDtypeStruct((s, aw), BF16)
    return pl.pallas_call(
        body, name="attn_bwd", grid=(nh,),
        in_specs=[_head_col(s, 0, nh), _head_col(s, 1, nh), _head_col(s, 2, nh), _head_col(s, 3, nh),
                  table, table, own, own, own],
        out_specs=[out, out, out],
        out_shape=[shape, shape, shape],
        scratch_shapes=[pltpu.VMEM((s, HEAD_DIM), F32)] * 8,
        compiler_params=_params(1, parallel=1),
    )(proj, proj, proj, proj, cos, sin, dy, attn, lse)


def _rope_tables(s):
    half = HEAD_DIM // 2
    inv = ROPE_THETA ** (-jnp.arange(half, dtype=F32) / half)
    ang = jnp.arange(s, dtype=F32)[:, None] * inv[None, :]
    cos, sin = jnp.cos(ang), jnp.sin(ang)
    return jnp.concatenate([cos, cos], axis=-1), jnp.concatenate([-sin, sin], axis=-1)


def _conv_chunks(s):
    for k in range(s // CONV_ROWS):
        lo = max(0, k * CONV_ROWS - CONV_HALO)
        hi = min(s, (k + 1) * CONV_ROWS + CONV_HALO)
        yield k * CONV_ROWS, lo, hi


def _neighbours(p, lo, s):
    n = p.shape[0]
    row = lo + lax.broadcasted_iota(jnp.int32, p.shape, 0)
    prev = jnp.where(row == 0, 0.0, pltpu.roll(p, 1, axis=0))
    nxt = jnp.where(row == s - 1, 0.0, pltpu.roll(p, n - 1, axis=0))
    return prev, nxt


def _ab_mix(attn, proj, conv_w, aw):
    s = proj.shape[0]
    nt = aw // LANES

    def col(group, sel):
        return pl.BlockSpec((s, LANES), lambda i: (0, group * nt + sel(i)))

    a_sel = lambda i: jnp.minimum(i, nt - 1)
    b_sel = lambda i: jnp.maximum(i - nt, 0)

    def body(attn_ref, za_ref, ub_ref, gb_ref, gc_ref, zb_ref, w_ref, y_ref):
        i = pl.program_id(0)

        @pl.when(i < nt)
        def _():
            y_ref[...] = (attn_ref[...].astype(F32) * _silu(za_ref[...].astype(F32))).astype(BF16)

        @pl.when(i >= nt)
        def _():
            w = w_ref[...]
            for c0, lo, hi in _conv_chunks(s):
                p = gc_ref[lo:hi, :].astype(F32) * ub_ref[lo:hi, :].astype(F32)
                prev, nxt = _neighbours(p, lo, s)
                cv = w[0:1, :] * prev + w[1:2, :] * p + w[2:3, :] * nxt
                yb = gb_ref[lo:hi, :].astype(F32) * cv * _silu(zb_ref[lo:hi, :].astype(F32))
                y_ref[c0:c0 + CONV_ROWS, :] = yb[c0 - lo:c0 - lo + CONV_ROWS, :].astype(BF16)

    return pl.pallas_call(
        body, name="ab_mix", grid=(2 * nt,),
        in_specs=[pl.BlockSpec((s, LANES), lambda i: (0, a_sel(i))),
                  col(3, a_sel), col(4, b_sel), col(5, b_sel), col(6, b_sel), col(7, b_sel),
                  pl.BlockSpec((3, LANES), lambda i: (0, b_sel(i)))],
        out_specs=pl.BlockSpec((s, LANES), lambda i: (0, i)),
        out_shape=jax.ShapeDtypeStruct((s, 2 * aw), BF16),
        compiler_params=_params(1),
    )(attn, proj, proj, proj, proj, proj, conv_w)


def _ab_dproj(dqkv, dy, attn, proj, conv_w, aw):
    s = proj.shape[0]
    nt = aw // LANES

    def col(group, sel):
        return pl.BlockSpec((s, LANES), lambda i: (0, group * nt + sel(i)))

    def qkv_spec(part):
        return pl.BlockSpec((s, LANES), lambda i: (0, jnp.clip(i - part * nt, 0, nt - 1)))

    a_sel = lambda i: jnp.clip(i - 3 * nt, 0, nt - 1)
    b_sel = lambda i: jnp.maximum(i - 4 * nt, 0) % nt
    w_sel = lambda i: jnp.clip(i - 4 * nt, 0, nt - 1)

    def body(*refs):
        g_refs = refs[:3]
        dya_ref, attn_ref, za_ref, dyb_ref, ub_ref, gb_ref, gc_ref, zb_ref, w_ref, out_ref, dw_ref = refs[3:]
        i = pl.program_id(0)

        for part in range(3):
            @pl.when(jnp.logical_and(i >= part * nt, i < (part + 1) * nt))
            def _(part=part):
                out_ref[...] = g_refs[part][...]

        @pl.when(jnp.logical_and(i >= 3 * nt, i < 4 * nt))
        def _():
            out_ref[...] = (dya_ref[...].astype(F32) * attn_ref[...].astype(F32)
                            * _dsilu(za_ref[...].astype(F32))).astype(BF16)

        for which in range(4):
            @pl.when(jnp.logical_and(i >= (4 + which) * nt, i < (5 + which) * nt))
            def _(which=which):
                w = w_ref[...]
                dw = [jnp.zeros((1, LANES), F32) for _ in range(3)]
                for c0, lo, hi in _conv_chunks(s):
                    ctr = slice(c0 - lo, c0 - lo + CONV_ROWS)
                    ub = ub_ref[lo:hi, :].astype(F32)
                    gc = gc_ref[lo:hi, :].astype(F32)
                    gb = gb_ref[lo:hi, :].astype(F32)
                    zb = zb_ref[lo:hi, :].astype(F32)
                    dyb = dyb_ref[lo:hi, :].astype(F32)
                    p = gc * ub
                    prev, nxt = _neighbours(p, lo, s)
                    if which == 1:
                        cv = w[0:1, :] * prev + w[1:2, :] * p + w[2:3, :] * nxt
                        res = dyb * cv * _silu(zb)
                    elif which == 3:
                        cv = w[0:1, :] * prev + w[1:2, :] * p + w[2:3, :] * nxt
                        res = dyb * gb * cv * _dsilu(zb)
                    else:
                        dcv = dyb * gb * _silu(zb)
                        dprev, dnxt = _neighbours(dcv, lo, s)
                        dp = w[0:1, :] * dnxt + w[1:2, :] * dcv + w[2:3, :] * dprev
                        res = dp * (gc if which == 0 else ub)
                        if which == 0:
                            for t, nb in enumerate((prev, p, nxt)):
                                dw[t] = dw[t] + jnp.sum((dcv * nb)[ctr, :], axis=0, keepdims=True)
                    out_ref[c0:c0 + CONV_ROWS, :] = res[ctr, :].astype(BF16)
                if which == 0:
                    dw_ref[...] = jnp.concatenate(dw, axis=0)

    return pl.pallas_call(
        body, name="ab_dproj", grid=(8 * nt,),
        in_specs=[qkv_spec(0), qkv_spec(1), qkv_spec(2),
            pl.BlockSpec((s, LANES), lambda i: (0, a_sel(i))),
            pl.BlockSpec((s, LANES), lambda i: (0, a_sel(i))),
            col(3, a_sel),
            pl.BlockSpec((s, LANES), lambda i: (0, nt + b_sel(i))),
            col(4, b_sel), col(5, b_sel), col(6, b_sel), col(7, b_sel),
            pl.BlockSpec((3, LANES), lambda i: (0, b_sel(i)))],
        out_specs=[pl.BlockSpec((s, LANES), lambda i: (0, i)),
                   pl.BlockSpec((3, LANES), lambda i: (0, w_sel(i)))],
        out_shape=[jax.ShapeDtypeStruct((s, 8 * aw), BF16), jax.ShapeDtypeStruct((3, aw), F32)],
        compiler_params=_params(1),
    )(*dqkv, dy, attn, proj, dy, proj, proj, proj, proj, conv_w)


def _sgu_norm(v, ln_g, ln_b):
    gv = _gelu(v)
    mu = jnp.mean(gv, axis=-1, keepdims=True)
    xc = gv - mu
    rstd = lax.rsqrt(jnp.mean(xc * xc, axis=-1, keepdims=True) + EPS)
    vhat = xc * rstd
    return vhat, rstd, vhat * ln_g + ln_b


def _sgu_fwd(uvz, ln_g, ln_b, w_s, b_s, cw):
    s = uvz.shape[0]
    tr = 2 * CHUNK if s % (2 * CHUNK) == 0 else CHUNK
    gw = cw // N_GROUPS

    def body(u_ref, v_ref, z_ref, g_ref, b_ref, ws_ref, bs_ref, y_ref):
        _, _, vn = _sgu_norm(v_ref[...].astype(F32), g_ref[...], b_ref[...])
        vn = vn.astype(BF16)
        for ch in range(tr // CHUNK):
            rows = slice(ch * CHUNK, (ch + 1) * CHUNK)
            for grp in range(N_GROUPS):
                cols = slice(grp * gw, (grp + 1) * gw)
                mixed = lax.dot_general(ws_ref[grp], vn[rows, cols], NN, preferred_element_type=F32) + bs_ref[grp]
                y_ref[rows, cols] = (_gelu(u_ref[rows, cols].astype(F32)) * mixed
                                     * _silu(z_ref[rows, cols].astype(F32))).astype(BF16)

    full3 = lambda shape: pl.BlockSpec(shape, lambda i: (0, 0, 0))
    return pl.pallas_call(
        body, name="sgu_fwd", grid=(s // tr,),
        in_specs=[pl.BlockSpec((tr, cw), lambda i: (i, 0)), pl.BlockSpec((tr, cw), lambda i: (i, 1)),
                  pl.BlockSpec((tr, cw), lambda i: (i, 2)), _vec_spec(cw), _vec_spec(cw),
                  full3(w_s.shape), full3(b_s.shape)],
        out_specs=pl.BlockSpec((tr, cw), lambda i: (i, 0)),
        out_shape=jax.ShapeDtypeStruct((s, cw), BF16),
        compiler_params=_params(1, parallel=1),
    )(uvz, uvz, uvz, ln_g, ln_b, w_s, b_s)


def _sgu_bwd(uvz, dy, ln_g, ln_b, w_s, b_s, cw):
    s = uvz.shape[0]
    tr = 2 * CHUNK if s % (2 * CHUNK) == 0 else CHUNK
    gw = cw // N_GROUPS

    def body(u_ref, v_ref, z_ref, dy_ref, g_ref, b_ref, ws_ref, bs_ref,
             du_ref, dv_ref, dz_ref, dws_ref, dbs_ref, dg_ref, db_ref, dvn_ref):
        vv = v_ref[...].astype(F32)
        gvec = g_ref[...]
        vhat, rstd, vn = _sgu_norm(vv, gvec, b_ref[...])
        vn = vn.astype(BF16)
        first = pl.program_id(0) == 0

        @pl.when(first)
        def _():
            dws_ref[...] = jnp.zeros_like(dws_ref)
            dbs_ref[...] = jnp.zeros_like(dbs_ref)

        for ch in range(tr // CHUNK):
            rows = slice(ch * CHUNK, (ch + 1) * CHUNK)
            for grp in range(N_GROUPS):
                cols = slice(grp * gw, (grp + 1) * gw)
                vn_g = vn[rows, cols]
                mixed = lax.dot_general(ws_ref[grp], vn_g, NN, preferred_element_type=F32) + bs_ref[grp]
                uu = u_ref[rows, cols].astype(F32)
                zz = z_ref[rows, cols].astype(F32)
                dyv = dy_ref[rows, cols].astype(F32)
                gu, sz = _gelu(uu), _silu(zz)
                du_ref[rows, cols] = (dyv * mixed * sz * _dgelu(uu)).astype(BF16)
                dz_ref[rows, cols] = (dyv * gu * mixed * _dsilu(zz)).astype(BF16)
                dmixed = dyv * gu * sz
                dm16 = dmixed.astype(BF16)
                dws_ref[grp] += lax.dot_general(dm16, vn_g, NT, preferred_element_type=F32)
                dbs_ref[grp] += jnp.broadcast_to(jnp.sum(dmixed, axis=1, keepdims=True), (CHUNK, LANES))
                dvn_ref[rows, cols] = lax.dot_general(ws_ref[grp], dm16, TN, preferred_element_type=F32)

        dvn = dvn_ref[...]
        _accumulate(dg_ref, jnp.sum(dvn * vhat, axis=0, keepdims=True))
        _accumulate(db_ref, jnp.sum(dvn, axis=0, keepdims=True))
        dvhat = dvn * gvec
        dgv = rstd * (dvhat - jnp.mean(dvhat, axis=-1, keepdims=True)
                      - vhat * jnp.mean(dvhat * vhat, axis=-1, keepdims=True))
        dv_ref[...] = (dgv * _dgelu(vv)).astype(BF16)

    full3 = lambda shape: pl.BlockSpec(shape, lambda i: (0, 0, 0))
    acc3 = jax.ShapeDtypeStruct((N_GROUPS, CHUNK, LANES), F32)
    vec = jax.ShapeDtypeStruct((1, cw), F32)
    act = jax.ShapeDtypeStruct((s, cw), BF16)
    row = pl.BlockSpec((tr, cw), lambda i: (i, 0))
    return pl.pallas_call(
        body, name="sgu_bwd", grid=(s // tr,),
        in_specs=[row, pl.BlockSpec((tr, cw), lambda i: (i, 1)), pl.BlockSpec((tr, cw), lambda i: (i, 2)),
                  row, _vec_spec(cw), _vec_spec(cw), full3(w_s.shape), full3(b_s.shape)],
        out_specs=[row, row, row, full3((N_GROUPS, CHUNK, LANES)), full3((N_GROUPS, CHUNK, LANES)),
                   _vec_spec(cw), _vec_spec(cw)],
        out_shape=[act, act, act, acc3, acc3, vec, vec],
        scratch_shapes=[pltpu.VMEM((tr, cw), F32)],
        compiler_params=_params(1),
    )(uvz, uvz, uvz, dy, ln_g, ln_b, w_s, b_s)


def _flat_rows(a):
    return a.reshape(-1, a.shape[-1])


def _add_sibling(grads, recv, layer_idx):
    _, nchip, k, n = grads.shape
    g2 = grads.reshape(2, nchip * k, n)
    r2 = recv.reshape(nchip * k, n)
    tr = _tile(nchip * k, (512, 256, 128))

    def body(c_ref, g_ref, r_ref, o_ref):
        o_ref[...] = (g_ref[...].astype(F32) + r_ref[...].astype(F32)).astype(BF16)

    out = pl.pallas_call(
        body, name="add_sibling",
        grid_spec=pltpu.PrefetchScalarGridSpec(
            num_scalar_prefetch=1, grid=(nchip * k // tr,),
            in_specs=[pl.BlockSpec((None, tr, n), lambda i, c: (c[0], i, 0)),
                      pl.BlockSpec((tr, n), lambda i, c: (i, 0))],
            out_specs=pl.BlockSpec((tr, n), lambda i, c: (i, 0))),
        out_shape=jax.ShapeDtypeStruct((nchip * k, n), BF16),
        compiler_params=_params(1, parallel=1),
    )(layer_idx, g2, r2)
    return out.reshape(nchip, k, n)


def _sum_chips(own, others, place):
    _, k, n = own.shape
    tr = _tile(k, (256, 128))

    def body(place_ref, own_ref, oth_ref, o_ref):
        acc = own_ref[...].astype(F32)
        for q in range(3):
            acc = acc + oth_ref[q].astype(F32)
        o_ref[...] = acc

    return pl.pallas_call(
        body, name="sum_chips",
        grid_spec=pltpu.PrefetchScalarGridSpec(
            num_scalar_prefetch=1, grid=(k // tr,),
            in_specs=[pl.BlockSpec((None, tr, n), lambda i, p: (p[0], i, 0)),
                      pl.BlockSpec((3, tr, n), lambda i, p: (0, i, 0))],
            out_specs=pl.BlockSpec((None, tr, n), lambda i, p: (p[1], i, 0))),
        out_shape=jax.ShapeDtypeStruct((2, k, n), F32),
        compiler_params=_params(1, parallel=1),
    )(place, own, others)


def _sum_devices(parts):
    nd, r, _ = parts.shape
    tr = _tile(r, (512, 256, 128, 64, 32, 16, 8))

    def body(p_ref, o_ref):
        acc = p_ref[0]
        for q in range(1, nd):
            acc = acc + p_ref[q]
        o_ref[...] = acc

    return pl.pallas_call(
        body, name="sum_devices", grid=(r // tr,),
        in_specs=[pl.BlockSpec((nd, tr, LANES), lambda i: (0, i, 0))],
        out_specs=pl.BlockSpec((tr, LANES), lambda i: (i, 0)),
        out_shape=jax.ShapeDtypeStruct((r, LANES), F32),
        compiler_params=_params(1, parallel=1),
    )(parts)


def _adamw(w, g, m, v):
    r, n = w.shape
    tr = _tile(r, [p for p in (1024, 512, 256, 128, 64, 32, 16, 8) if p * n <= ELEMENTWISE_BLOCK])

    def body(w_ref, g_ref, m_ref, v_ref, d_ref, nm_ref, nv_ref):
        gv = g_ref[...]
        nm = ADAM_B1 * m_ref[...] + (1.0 - ADAM_B1) * gv
        nv = ADAM_B2 * v_ref[...] + (1.0 - ADAM_B2) * (gv * gv)
        m_hat = nm / (1.0 - ADAM_B1 ** ADAM_STEP)
        v_hat = nv / (1.0 - ADAM_B2 ** ADAM_STEP)
        d_ref[...] = -ADAM_LR * (m_hat / (jnp.sqrt(v_hat) + ADAM_EPS) + ADAM_WD * w_ref[...])
        nm_ref[...] = nm
        nv_ref[...] = nv

    spec = pl.BlockSpec((tr, n), lambda i: (i, 0))
    shp = jax.ShapeDtypeStruct((r, n), F32)
    return pl.pallas_call(
        body, name="adamw", grid=(r // tr,),
        in_specs=[spec] * 4, out_specs=[spec] * 3, out_shape=[shp] * 3,
        compiler_params=_params(1, parallel=1),
    )(w, g, m, v)


def _pack(arrays, row_multiple=8):
    flat = [a.reshape(-1) for a in arrays]
    sizes = [f.shape[0] for f in flat]
    total = sum(sizes)
    unit = LANES * row_multiple
    padded = -(-total // unit) * unit
    if padded > total:
        flat.append(jnp.zeros((padded - total,), F32))
    offsets = [sum(sizes[:i]) for i in range(len(sizes))]
    return jnp.concatenate(flat).reshape(-1, LANES), offsets


def _unpack(packed, offsets, shapes):
    flat = packed.reshape(-1)
    return [flat[o:o + math.prod(s)].reshape(s) for o, s in zip(offsets, shapes)]


def kernel(x, c, ab_norm_g, ab_w_mod, ab_b_mod, ab_w_in, ab_conv_w, ab_w_out, sg_norm_g, sg_w_mod, sg_b_mod, sg_w_in, sg_ln_g, sg_ln_b, sg_w_s, sg_b_s, sg_w_out, final_norm_g, loss_target, m_ab_norm_g, m_ab_w_mod, m_ab_b_mod, m_ab_w_in, m_ab_conv_w, m_ab_w_out, m_sg_norm_g, m_sg_w_mod, m_sg_b_mod, m_sg_w_in, m_sg_ln_g, m_sg_ln_b, m_sg_w_s, m_sg_b_s, m_sg_w_out, m_final_norm_g, v_ab_norm_g, v_ab_w_mod, v_ab_b_mod, v_ab_w_in, v_ab_conv_w, v_ab_w_out, v_sg_norm_g, v_sg_w_mod, v_sg_b_mod, v_sg_w_in, v_sg_ln_g, v_sg_ln_b, v_sg_w_s, v_sg_b_s, v_sg_w_out, v_final_norm_g):
    s, d = x.shape[1], x.shape[2]
    aw = d // 2
    nh = aw // HEAD_DIM
    cw = d
    mod_l = ab_w_mod.shape[-1]
    x0 = x[0]
    target = loss_target[0]
    mx, my, mc = lax.axis_index("x"), lax.axis_index("y"), lax.axis_index("c")
    chip = 2 * mx + my
    me = 2 * chip + mc

    small_local = [c[0], ab_conv_w, sg_norm_g, sg_ln_g, sg_ln_b]
    small_shapes = [a.shape for a in small_local]
    payload, small_off = _pack(small_local)
    gathered = _all_to_all(jnp.broadcast_to(payload[None], (N_DEV,) + payload.shape), "gather_small")
    per_dev = [_unpack(gathered[b], small_off, small_shapes) for b in range(N_DEV)]
    c_all = jnp.stack([per_dev[b][0] for b in range(N_DEV)])

    def from_chips(idx, axis):
        return jnp.concatenate([per_dev[2 * q][idx] for q in range(N_CHIPS)], axis=axis)

    conv_w_full = from_chips(1, 2)
    sg_norm_g_full = from_chips(2, 1)
    sg_ln_g_full = from_chips(3, 1)
    sg_ln_b_full = from_chips(4, 1)

    ab_b_local = lax.dynamic_slice_in_dim(ab_b_mod, chip * mod_l, mod_l, axis=1)
    mod_rows = []
    for layer in range(4):
        i = layer // 2
        w_mod, bias = (ab_w_mod, ab_b_local) if layer % 2 == 0 else (sg_w_mod, sg_b_mod)
        mod_rows.append(_mod_fwd(c_all, w_mod, bias[i:i + 1], i))
    mod_local = jnp.stack(mod_rows, axis=1)
    mod_recv = _all_to_all(mod_local.reshape(N_DEV, -1, LANES), "exchange_mod")
    mod_recv = mod_recv.reshape(N_DEV, 4, mod_l)
    mod_full = jnp.concatenate([mod_recv[2 * q] for q in range(N_CHIPS)], axis=-1)
    shifts = [mod_full[l:l + 1, :d] for l in range(4)]
    scales = [mod_full[l:l + 1, d:2 * d] for l in range(4)]
    gates = [mod_full[l:l + 1, 2 * d:] for l in range(4)]

    chip_idx = jnp.reshape(chip, (1,)).astype(jnp.int32)
    w_in_ab, w_out_ab, w_in_sg, w_out_sg = _gather_weights(
        [_place_own_shard(w, chip_idx) for w in (ab_w_in, ab_w_out, sg_w_in, sg_w_out)])
    w_out_ab2 = w_out_ab.reshape(2, -1, d)
    w_out_sg2 = w_out_sg.reshape(2, -1, d)

    cos, sin = _rope_tables(s)
    w_s16 = sg_w_s.astype(BF16)
    b_s3 = sg_b_s[..., None]

    saved = []
    xs = x0
    for layer in range(4):
        i = layer // 2
        if layer % 2 == 0:
            h = _prenorm(xs, ab_norm_g[i:i + 1], scales[layer], shifts[layer])
            proj = _in_proj(h, w_in_ab, i)
            attn, lse = _attn_fwd(proj, cos, sin, aw)
            y = _ab_mix(attn, proj, conv_w_full[i], aw)
            x_next, out = _out_proj_residual(y, w_out_ab2, i, xs, gates[layer])
            saved.append((xs, h, proj, y, out, attn, lse))
        else:
            h = _prenorm(xs, sg_norm_g_full[i:i + 1], scales[layer], shifts[layer])
            uvz = _in_proj(h, w_in_sg, i)
            y = _sgu_fwd(uvz, sg_ln_g_full[i:i + 1], sg_ln_b_full[i:i + 1], w_s16[i], b_s3[i], cw)
            x_next, out = _out_proj_residual(y, w_out_sg2, i, xs, gates[layer])
            saved.append((xs, h, uvz, y, out))
        xs = x_next

    loss11, dx, d_final_g = _final_loss(xs, target, final_norm_g[None])
    loss = lax.psum(loss11[0, 0], ("x", "y", "c"))

    g_in_ab = jnp.zeros(w_in_ab.shape, BF16)
    g_out_ab = jnp.zeros(w_out_ab2.shape, BF16)
    g_in_sg = jnp.zeros(w_in_sg.shape, BF16)
    g_out_sg = jnp.zeros(w_out_sg2.shape, BF16)
    dmods = [None] * 4
    d_ab_norm_g, d_sg_norm_g = [None, None], [None, None]
    d_conv_w, d_ln_g, d_ln_b, d_w_s, d_b_s = ([None, None] for _ in range(5))
    for layer in reversed(range(4)):
        i = layer // 2
        if layer % 2 == 0:
            xs, h, proj, y, out, attn, lse = saved[layer]
            dout, dgate = _gate_bwd(dx, out, gates[layer])
            dy = _out_proj_bwd_act(dout, w_out_ab2, i)
            g_out_ab = _out_proj_bwd_w(y, dout, g_out_ab, i)
            dqkv = _attn_bwd(proj, cos, sin, dy, attn, lse, aw)
            dproj, d_conv_w[i] = _ab_dproj(dqkv, dy, attn, proj, conv_w_full[i], aw)
            dh = _in_proj_bwd_act(dproj, w_in_ab, i)
            g_in_ab = _in_proj_bwd_w(h, dproj, g_in_ab, i)
            dx, dshift, dscale, d_ab_norm_g[i] = _prenorm_bwd(xs, dh, dx, ab_norm_g[i:i + 1], scales[layer])
        else:
            xs, h, uvz, y, out = saved[layer]
            dout, dgate = _gate_bwd(dx, out, gates[layer])
            dy = _out_proj_bwd_act(dout, w_out_sg2, i)
            g_out_sg = _out_proj_bwd_w(y, dout, g_out_sg, i)
            du, dv, dz, d_w_s[i], db_wide, d_ln_g[i], d_ln_b[i] = _sgu_bwd(
                uvz, dy, sg_ln_g_full[i:i + 1], sg_ln_b_full[i:i + 1], w_s16[i], b_s3[i], cw)
            d_b_s[i] = db_wide[:, :, 0]
            duvz = jnp.concatenate([du, dv, dz], axis=1)
            dh = _in_proj_bwd_act(duvz, w_in_sg, i)
            g_in_sg = _in_proj_bwd_w(h, duvz, g_in_sg, i)
            dx, dshift, dscale, d_sg_norm_g[i] = _prenorm_bwd(xs, dh, dx, sg_norm_g_full[i:i + 1], scales[layer])
        dmods[layer] = jnp.concatenate([dshift, dscale, dgate], axis=1)
    grad_x = dx[None]

    partial_list = [jnp.concatenate(dmods, axis=0),
                    jnp.concatenate(d_ab_norm_g, axis=0), jnp.concatenate(d_sg_norm_g, axis=0), d_final_g[0],
                    jnp.stack(d_conv_w), jnp.concatenate(d_ln_g, axis=0), jnp.concatenate(d_ln_b, axis=0),
                    jnp.stack(d_w_s), jnp.stack(d_b_s)]
    partial_shapes = [a.shape for a in partial_list]
    partials, part_off = _pack(partial_list)
    all_partials = _all_to_all(jnp.broadcast_to(partials[None], (N_DEV,) + partials.shape), "gather_partials")
    reduced = _unpack(_sum_devices(all_partials), part_off, partial_shapes)
    (g_mod_bias, g_ab_norm_g, g_sg_norm_g_full, g_final_g, g_conv_full, g_ln_g_full, g_ln_b_full,
     g_w_s, g_b_s) = reduced
    dm_all = jnp.stack([_unpack(all_partials[b], part_off[:1], partial_shapes[:1])[0] for b in range(N_DEV)])
    dm_local = lax.dynamic_slice_in_dim(dm_all, chip * mod_l, mod_l, axis=2)

    def chip_cols(a, axis):
        width = a.shape[axis] // N_CHIPS
        return lax.dynamic_slice_in_dim(a, chip * width, width, axis=axis)

    g_ab_b_mod = jnp.stack([g_mod_bias[0], g_mod_bias[2]])
    g_sg_b_mod = chip_cols(jnp.stack([g_mod_bias[1], g_mod_bias[3]]), 1)
    g_ab_w_mod = jnp.stack([_mod_bwd_w(c_all, dm_local[:, 0]), _mod_bwd_w(c_all, dm_local[:, 2])])
    g_sg_w_mod = jnp.stack([_mod_bwd_w(c_all, dm_local[:, 1]), _mod_bwd_w(c_all, dm_local[:, 3])])
    g_conv = chip_cols(g_conv_full, 2)
    g_sg_norm_g = chip_cols(g_sg_norm_g_full, 1)
    g_ln_g = chip_cols(g_ln_g_full, 1)
    g_ln_b = chip_cols(g_ln_b_full, 1)

    big = [g_in_ab, g_out_ab.reshape(w_out_ab.shape), g_in_sg, g_out_sg.reshape(w_out_sg.shape)]
    from_sib = _swap_layers_with_sibling(big)
    layer_idx = jnp.reshape(mc, (1,)).astype(jnp.int32)
    chip_sums = [_add_sibling(g, r, layer_idx) for g, r in zip(big, from_sib)]
    from_chips_sums = _scatter_chip_sums(chip_sums)
    place = jnp.stack([chip, mc]).astype(jnp.int32)
    reduced_big = [_sum_chips(own, oth, place) for own, oth in zip(chip_sums, from_chips_sums)]
    g_ab_w_in, g_ab_w_out, g_sg_w_in, g_sg_w_out = _share_layer_with_sibling(reduced_big)

    def step_big(w, g, m, v):
        dl, nm, nv = _adamw(_flat_rows(w), _flat_rows(g), _flat_rows(m), _flat_rows(v))
        return dl.reshape(w.shape), nm.reshape(w.shape), nv.reshape(w.shape)

    big_out = {
        "ab_w_mod": step_big(ab_w_mod, g_ab_w_mod, m_ab_w_mod, v_ab_w_mod),
        "ab_w_in": step_big(ab_w_in, g_ab_w_in, m_ab_w_in, v_ab_w_in),
        "ab_w_out": step_big(ab_w_out, g_ab_w_out, m_ab_w_out, v_ab_w_out),
        "sg_w_mod": step_big(sg_w_mod, g_sg_w_mod, m_sg_w_mod, v_sg_w_mod),
        "sg_w_in": step_big(sg_w_in, g_sg_w_in, m_sg_w_in, v_sg_w_in),
        "sg_w_out": step_big(sg_w_out, g_sg_w_out, m_sg_w_out, v_sg_w_out),
    }
    small_names = ["ab_norm_g", "ab_b_mod", "ab_conv_w", "sg_norm_g", "sg_b_mod", "sg_ln_g", "sg_ln_b",
                   "sg_w_s", "sg_b_s", "final_norm_g"]
    small_w = [ab_norm_g, ab_b_mod, ab_conv_w, sg_norm_g, sg_b_mod, sg_ln_g, sg_ln_b, sg_w_s, sg_b_s, final_norm_g]
    small_g = [g_ab_norm_g, g_ab_b_mod, g_conv, g_sg_norm_g, g_sg_b_mod, g_ln_g, g_ln_b, g_w_s, g_b_s, g_final_g]
    small_m = [m_ab_norm_g, m_ab_b_mod, m_ab_conv_w, m_sg_norm_g, m_sg_b_mod, m_sg_ln_g, m_sg_ln_b, m_sg_w_s,
               m_sg_b_s, m_final_norm_g]
    small_v = [v_ab_norm_g, v_ab_b_mod, v_ab_conv_w, v_sg_norm_g, v_sg_b_mod, v_sg_ln_g, v_sg_ln_b, v_sg_w_s,
               v_sg_b_s, v_final_norm_g]
    shapes = [a.shape for a in small_w]
    pw, off = _pack(small_w)
    pg, _ = _pack(small_g)
    pm, _ = _pack(small_m)
    pv, _ = _pack(small_v)
    pd, pnm, pnv = _adamw(pw, pg, pm, pv)
    small_out = {}
    for name, dl, nm, nv in zip(small_names, _unpack(pd, off, shapes), _unpack(pnm, off, shapes),
                                _unpack(pnv, off, shapes)):
        small_out[name] = (dl, nm, nv)

    grads = {
        "ab_norm_g": g_ab_norm_g, "ab_w_mod": g_ab_w_mod, "ab_b_mod": g_ab_b_mod, "ab_w_in": g_ab_w_in,
        "ab_conv_w": g_conv, "ab_w_out": g_ab_w_out, "sg_norm_g": g_sg_norm_g, "sg_w_mod": g_sg_w_mod,
        "sg_b_mod": g_sg_b_mod, "sg_w_in": g_sg_w_in, "sg_ln_g": g_ln_g, "sg_ln_b": g_ln_b, "sg_w_s": g_w_s,
        "sg_b_s": g_b_s, "sg_w_out": g_sg_w_out, "final_norm_g": g_final_g,
    }
    order = ["ab_norm_g", "ab_w_mod", "ab_b_mod", "ab_w_in", "ab_conv_w", "ab_w_out", "sg_norm_g", "sg_w_mod",
             "sg_b_mod", "sg_w_in", "sg_ln_g", "sg_ln_b", "sg_w_s", "sg_b_s", "sg_w_out", "final_norm_g"]
    steps = {**big_out, **small_out}
    return (loss, grad_x, *[grads[n] for n in order], *[steps[n][0] for n in order],
            *[steps[n][1] for n in order], *[steps[n][2] for n in order])
```

```python
import math

import jax
import jax.numpy as jnp
from jax import lax
from jax.experimental import pallas as pl
from jax.experimental.pallas import tpu as pltpu

F32 = jnp.float32
BF16 = jnp.bfloat16

HEAD_DIM = 128
RADIUS = 64
DILATIONS = (1, 4, 16)
Q_BLOCK = 256
K_WINDOW = Q_BLOCK + 2 * RADIUS
ROPE_THETA = 10000.0
NEG_INF = -1e30
N_GROUPS = 8
CHUNK = 128
EPS = 1e-6
CONV_ROWS = 512
CONV_HALO = 16
LANES = 128
ELEMENTWISE_BLOCK = 256 * 1024
N_DEV = 8
N_CHIPS = 4

ADAM_LR = 0.001
ADAM_B1 = 0.9
ADAM_B2 = 0.999
ADAM_EPS = 1e-08
ADAM_WD = 0.01
ADAM_STEP = 10

VMEM_LIMIT_V7X = 56 * 1024 * 1024

MESH_ID = pl.DeviceIdType.MESH
HBM_SPEC = pl.BlockSpec(memory_space=pltpu.HBM)

NN = (((1,), (0,)), ((), ()))
NT = (((1,), (1,)), ((), ()))
TN = (((0,), (0,)), ((), ()))


def _params(n_grid, parallel=0):
    sem = tuple(["parallel"] * parallel + ["arbitrary"] * (n_grid - parallel))
    return pltpu.CompilerParams(dimension_semantics=sem, vmem_limit_bytes=VMEM_LIMIT_V7X)


def _tile(n, prefs):
    for p in prefs:
        if n % p == 0:
            return p
    return n


def _sigmoid(z):
    return 1.0 / (1.0 + jnp.exp(-z))


def _silu(z):
    return z * _sigmoid(z)


def _dsilu(z):
    s = _sigmoid(z)
    return s * (1.0 + z * (1.0 - s))


_GELU_K = math.sqrt(2.0 / math.pi)
_GELU_C = 0.044715


def _gelu(u):
    return 0.5 * u * (1.0 + jnp.tanh(_GELU_K * (u + _GELU_C * u * u * u)))


def _dgelu(u):
    t = jnp.tanh(_GELU_K * (u + _GELU_C * u * u * u))
    return 0.5 * (1.0 + t) + 0.5 * u * (1.0 - t * t) * _GELU_K * (1.0 + 3.0 * _GELU_C * u * u)


def _my_place():
    return lax.axis_index("x"), lax.axis_index("y"), lax.axis_index("c")


def _flip(v, bit):
    return 1 - v if bit else v


def _all_to_all(x, name):
    def body(x_ref, y_ref, send_sems, recv_sems, own_sem):
        mx, my, mc = _my_place()
        me = 4 * mx + 2 * my + mc
        own = pltpu.make_async_copy(x_ref.at[me], y_ref.at[me], own_sem)
        own.start()
        copies = []
        for k in range(1, N_DEV):
            px, py, pc = _flip(mx, (k >> 2) & 1), _flip(my, (k >> 1) & 1), _flip(mc, k & 1)
            peer = 4 * px + 2 * py + pc
            cp = pltpu.make_async_remote_copy(
                src_ref=x_ref.at[peer], dst_ref=y_ref.at[me],
                send_sem=send_sems.at[k - 1], recv_sem=recv_sems.at[k - 1],
                device_id=(px, py, pc), device_id_type=MESH_ID)
            cp.start()
            copies.append(cp)
        for cp in copies:
            cp.wait()
        own.wait()

    return pl.pallas_call(
        body, name=name,
        out_shape=jax.ShapeDtypeStruct(x.shape, x.dtype),
        in_specs=[HBM_SPEC], out_specs=HBM_SPEC,
        scratch_shapes=[pltpu.SemaphoreType.DMA((N_DEV - 1,)), pltpu.SemaphoreType.DMA((N_DEV - 1,)),
                        pltpu.SemaphoreType.DMA],
    )(x)


def _other_chips(mx, my):
    return [(1 - mx, my), (mx, 1 - my), (1 - mx, 1 - my)]


def _place_own_shard(w, chip_idx):
    _, k, n = w.shape
    tr = _tile(k, (512, 256, 128))

    def body(c_ref, w_ref, g_ref):
        g_ref[...] = w_ref[...].astype(BF16)

    return pl.pallas_call(
        body, name="place_own_shard",
        grid_spec=pltpu.PrefetchScalarGridSpec(
            num_scalar_prefetch=1, grid=(2, k // tr),
            in_specs=[pl.BlockSpec((None, tr, n), lambda i, r, c: (i, r, 0))],
            out_specs=pl.BlockSpec((None, None, tr, n), lambda i, r, c: (i, c[0], r, 0))),
        out_shape=jax.ShapeDtypeStruct((2, N_CHIPS, k, n), BF16),
        compiler_params=_params(2, parallel=2),
    )(chip_idx, w)


def _gather_weights(placed):
    n = len(placed)

    def body(*refs):
        w = refs[:n]
        g = refs[n:2 * n]
        send_sems, recv_sems, fsend_sems, frecv_sems = refs[2 * n:]
        mx, my, mc = _my_place()
        j = 2 * mx + my
        chips = _other_chips(mx, my)
        first, passed = [], []
        for t in range(n):
            for q, (px, py) in enumerate(chips):
                cp = pltpu.make_async_remote_copy(
                    src_ref=w[t].at[mc, j], dst_ref=g[t].at[mc, j],
                    send_sem=send_sems.at[3 * t + q], recv_sem=recv_sems.at[3 * t + q],
                    device_id=(px, py, mc), device_id_type=MESH_ID)
                cp.start()
                first.append(cp)
        for t in range(n):
            for q, (px, py) in enumerate(chips):
                jq = 2 * px + py
                landed = pltpu.make_async_remote_copy(
                    src_ref=w[t].at[mc, j], dst_ref=g[t].at[mc, jq],
                    send_sem=send_sems.at[3 * t + q], recv_sem=recv_sems.at[3 * t + q],
                    device_id=(px, py, mc), device_id_type=MESH_ID)
                landed.wait_recv()
                fwd = pltpu.make_async_remote_copy(
                    src_ref=g[t].at[mc, jq], dst_ref=g[t].at[mc, jq],
                    send_sem=fsend_sems.at[3 * t + q], recv_sem=frecv_sems.at[3 * t + q],
                    device_id=(mx, my, 1 - mc), device_id_type=MESH_ID)
                fwd.start()
                passed.append(fwd)
        for t in range(n):
            for q, (px, py) in enumerate(chips):
                jq = 2 * px + py
                from_sibling = pltpu.make_async_remote_copy(
                    src_ref=g[t].at[1 - mc, jq], dst_ref=g[t].at[1 - mc, jq],
                    send_sem=fsend_sems.at[3 * t + q], recv_sem=frecv_sems.at[3 * t + q],
                    device_id=(mx, my, 1 - mc), device_id_type=MESH_ID)
                from_sibling.wait_recv()
        for cp in first + passed:
            cp.wait_send()

    return pl.pallas_call(
        body, name="gather_weights",
        out_shape=[jax.ShapeDtypeStruct(a.shape, a.dtype) for a in placed],
        in_specs=[HBM_SPEC] * n, out_specs=[HBM_SPEC] * n,
        input_output_aliases={t: t for t in range(n)},
        scratch_shapes=[pltpu.SemaphoreType.DMA((3 * n,)), pltpu.SemaphoreType.DMA((3 * n,)),
                        pltpu.SemaphoreType.DMA((3 * n,)), pltpu.SemaphoreType.DMA((3 * n,))],
    )(*placed)


def _swap_layers_with_sibling(grads):
    n = len(grads)

    def body(*refs):
        g = refs[:n]
        r = refs[n:2 * n]
        send_sems, recv_sems = refs[2 * n:]
        mx, my, mc = _my_place()
        copies = []
        for t in range(n):
            cp = pltpu.make_async_remote_copy(
                src_ref=g[t].at[1 - mc], dst_ref=r[t],
                send_sem=send_sems.at[t], recv_sem=recv_sems.at[t],
                device_id=(mx, my, 1 - mc), device_id_type=MESH_ID)
            cp.start()
            copies.append(cp)
        for cp in copies:
            cp.wait()

    return pl.pallas_call(
        body, name="grads_to_sibling",
        out_shape=[jax.ShapeDtypeStruct(a.shape[1:], a.dtype) for a in grads],
        in_specs=[HBM_SPEC] * n, out_specs=[HBM_SPEC] * n,
        scratch_shapes=[pltpu.SemaphoreType.DMA((n,)), pltpu.SemaphoreType.DMA((n,))],
    )(*grads)


def _scatter_chip_sums(sums):
    n = len(sums)

    def body(*refs):
        s = refs[:n]
        r = refs[n:2 * n]
        send_sems, recv_sems = refs[2 * n:]
        mx, my, mc = _my_place()
        copies = []
        for t in range(n):
            for q, (px, py) in enumerate(_other_chips(mx, my)):
                cp = pltpu.make_async_remote_copy(
                    src_ref=s[t].at[2 * px + py], dst_ref=r[t].at[q],
                    send_sem=send_sems.at[3 * t + q], recv_sem=recv_sems.at[3 * t + q],
                    device_id=(px, py, mc), device_id_type=MESH_ID)
                cp.start()
                copies.append(cp)
        for cp in copies:
            cp.wait()

    return pl.pallas_call(
        body, name="chip_sums_to_owners",
        out_shape=[jax.ShapeDtypeStruct((3,) + a.shape[1:], a.dtype) for a in sums],
        in_specs=[HBM_SPEC] * n, out_specs=[HBM_SPEC] * n,
        scratch_shapes=[pltpu.SemaphoreType.DMA((3 * n,)), pltpu.SemaphoreType.DMA((3 * n,))],
    )(*sums)


def _share_layer_with_sibling(reduced):
    n = len(reduced)

    def body(*refs):
        h = refs[:n]
        f = refs[n:2 * n]
        send_sems, recv_sems = refs[2 * n:]
        mx, my, mc = _my_place()
        copies = []
        for t in range(n):
            rc = pltpu.make_async_remote_copy(
                src_ref=h[t].at[mc], dst_ref=f[t].at[mc],
                send_sem=send_sems.at[t], recv_sem=recv_sems.at[t],
                device_id=(mx, my, 1 - mc), device_id_type=MESH_ID)
            rc.start()
            copies.append(rc)
        for cp in copies:
            cp.wait()

    return pl.pallas_call(
        body, name="reduced_grads_to_sibling",
        out_shape=[jax.ShapeDtypeStruct(a.shape, a.dtype) for a in reduced],
        in_specs=[HBM_SPEC] * n, out_specs=[HBM_SPEC] * n,
        input_output_aliases={t: t for t in range(n)},
        scratch_shapes=[pltpu.SemaphoreType.DMA((n,)), pltpu.SemaphoreType.DMA((n,))],
    )(*reduced)


def _matmul(name, operands, in_specs, grid, nk, dims, out_shape, out_specs, acc_shape,
            epilogue, a_prologue=None, aliases=None):
    n_in = len(operands)

    def body(*refs):
        a_ref, b_ref = refs[0], refs[1]
        extra = refs[2:n_in]
        outs = refs[n_in:-1] if nk > 1 else refs[n_in:]
        a = a_ref[...]
        if a_prologue is not None:
            a = a_prologue(a)
        part = lax.dot_general(a.astype(BF16), b_ref[...].astype(BF16), dims, preferred_element_type=F32)
        if nk == 1:
            epilogue(part, extra, outs)
        else:
            acc = refs[-1]
            k = pl.program_id(2)

            @pl.when(k == 0)
            def _():
                acc[...] = part

            @pl.when(k > 0)
            def _():
                acc[...] += part

            @pl.when(k == nk - 1)
            def _():
                epilogue(acc[...], extra, outs)

    return pl.pallas_call(
        body, name=name, grid=grid, in_specs=in_specs, out_specs=out_specs, out_shape=out_shape,
        scratch_shapes=[pltpu.VMEM(acc_shape, F32)] if nk > 1 else [],
        input_output_aliases=aliases or {},
        compiler_params=_params(3, parallel=2),
    )(*operands)


def _store_cast(acc, extra, outs):
    outs[0][...] = acc.astype(outs[0].dtype)


def _in_proj(h, w_g, layer):
    s, d = h.shape
    nl = w_g.shape[-1]
    tm = _tile(s, (1024, 512, 256))
    tn = _tile(nl, (1024, 768, 512, 384, 256, 128))
    per = nl // tn
    return _matmul(
        "in_proj", (h, w_g),
        [pl.BlockSpec((tm, d), lambda i, j, k: (i, 0)),
         pl.BlockSpec((None, None, d, tn), lambda i, j, k: (layer, j // per, 0, j % per))],
        (s // tm, N_CHIPS * per, 1), 1, NN,
        jax.ShapeDtypeStruct((s, N_CHIPS * nl), BF16),
        pl.BlockSpec((tm, tn), lambda i, j, k: (i, j)), (tm, tn), _store_cast)


def _out_proj_residual(y, w2, layer, x, gate):
    s, wdt = y.shape
    d = w2.shape[-1]
    tm = _tile(s, (1024, 512, 256))
    tn = _tile(d, (1024, 512, 256, 128))

    def epilogue(acc, extra, outs):
        x_ref, gate_ref = extra
        outs[0][...] = x_ref[...] + gate_ref[...] * acc
        outs[1][...] = acc.astype(BF16)

    return _matmul(
        "out_proj", (y, w2, x, gate),
        [pl.BlockSpec((tm, wdt), lambda i, j, k: (i, 0)),
         pl.BlockSpec((None, wdt, tn), lambda i, j, k: (layer, 0, j)),
         pl.BlockSpec((tm, tn), lambda i, j, k: (i, j)),
         pl.BlockSpec((1, tn), lambda i, j, k: (0, j))],
        (s // tm, d // tn, 1), 1, NN,
        [jax.ShapeDtypeStruct((s, d), F32), jax.ShapeDtypeStruct((s, d), BF16)],
        [pl.BlockSpec((tm, tn), lambda i, j, k: (i, j)), pl.BlockSpec((tm, tn), lambda i, j, k: (i, j))],
        (tm, tn), epilogue)


def _out_proj_bwd_act(dout, w2, layer):
    s, d = dout.shape
    wdt = w2.shape[1]
    tm = _tile(s, (1024, 512, 256))
    tn = _tile(wdt, (1024, 512, 256, 128))
    return _matmul(
        "out_proj_dy", (dout, w2),
        [pl.BlockSpec((tm, d), lambda i, j, k: (i, 0)),
         pl.BlockSpec((None, tn, d), lambda i, j, k: (layer, j, 0))],
        (s // tm, wdt // tn, 1), 1, NT,
        jax.ShapeDtypeStruct((s, wdt), BF16),
        pl.BlockSpec((tm, tn), lambda i, j, k: (i, j)), (tm, tn), _store_cast)


def _out_proj_bwd_w(y, dout, buf, layer):
    s, wdt = y.shape
    d = dout.shape[1]
    tm = _tile(wdt, (1024, 512, 256, 128))
    tn = _tile(d, (1024, 512, 256, 128))

    def epilogue(acc, extra, outs):
        outs[0][...] = acc.astype(BF16)

    return _matmul(
        "out_proj_dw", (y, dout, buf),
        [pl.BlockSpec((s, tm), lambda i, j, k: (0, i)),
         pl.BlockSpec((s, tn), lambda i, j, k: (0, j)),
         HBM_SPEC],
        (wdt // tm, d // tn, 1), 1, TN,
        jax.ShapeDtypeStruct(buf.shape, buf.dtype),
        pl.BlockSpec((None, tm, tn), lambda i, j, k: (layer, i, j)), (tm, tn), epilogue,
        aliases={2: 0})


def _in_proj_bwd_act(dproj, w_g, layer):
    s, n_all = dproj.shape
    d, nl = w_g.shape[2], w_g.shape[3]
    tm = _tile(s, (1024, 512, 256))
    tn = _tile(d, (512, 256, 128))

    def body(a_ref, w_ref, o_ref):
        acc = None
        for q in range(N_CHIPS):
            part = lax.dot_general(a_ref[:, q * nl:(q + 1) * nl], w_ref[q], NT, preferred_element_type=F32)
            acc = part if acc is None else acc + part
        o_ref[...] = acc.astype(BF16)

    return pl.pallas_call(
        body, name="in_proj_dh", grid=(s // tm, d // tn),
        in_specs=[pl.BlockSpec((tm, n_all), lambda i, j: (i, 0), pipeline_mode=pl.Buffered(1)),
                  pl.BlockSpec((None, N_CHIPS, tn, nl), lambda i, j: (layer, 0, j, 0))],
        out_specs=pl.BlockSpec((tm, tn), lambda i, j: (i, j)),
        out_shape=jax.ShapeDtypeStruct((s, d), BF16),
        compiler_params=_params(2, parallel=2),
    )(dproj, w_g)


def _in_proj_bwd_w(h, dproj, buf, layer):
    s, d = h.shape
    nl = buf.shape[-1]
    tm = _tile(d, (1024, 512, 256, 128))
    tn = _tile(nl, (1024, 768, 512, 384, 256, 128))
    per = nl // tn

    def epilogue(acc, extra, outs):
        outs[0][...] = acc.astype(BF16)

    return _matmul(
        "in_proj_dw", (h, dproj, buf),
        [pl.BlockSpec((s, tm), lambda i, j, k: (0, i)),
         pl.BlockSpec((s, tn), lambda i, j, k: (0, j)),
         HBM_SPEC],
        (d // tm, N_CHIPS * per, 1), 1, TN,
        jax.ShapeDtypeStruct(buf.shape, buf.dtype),
        pl.BlockSpec((None, None, tm, tn), lambda i, j, k: (layer, j // per, i, j % per)), (tm, tn), epilogue,
        aliases={2: 0})


def _mod_fwd(c_all, w_mod, bias, layer):
    nb, d = c_all.shape
    nl = w_mod.shape[-1]
    tn = _tile(nl, (768, 512, 384, 256, 128))
    tk = _tile(d, (1024, 512, 256, 128))
    nk = d // tk

    def epilogue(acc, extra, outs):
        outs[0][...] = acc + extra[0][...]

    return _matmul(
        "mod_fwd", (c_all, w_mod, bias),
        [pl.BlockSpec((nb, tk), lambda i, j, k: (0, k)),
         pl.BlockSpec((None, tk, tn), lambda i, j, k: (layer, k, j)),
         pl.BlockSpec((1, tn), lambda i, j, k: (0, j))],
        (1, nl // tn, nk), nk, NN,
        jax.ShapeDtypeStruct((nb, nl), F32),
        pl.BlockSpec((nb, tn), lambda i, j, k: (0, j)), (nb, tn), epilogue, a_prologue=_silu)


def _mod_bwd_w(c_all, dm_local):
    nb, d = c_all.shape
    nl = dm_local.shape[-1]
    tm = _tile(d, (1024, 512, 256, 128))
    tn = _tile(nl, (768, 512, 384, 256, 128))

    def epilogue(acc, extra, outs):
        outs[0][...] = acc

    return _matmul(
        "mod_dw", (c_all, dm_local),
        [pl.BlockSpec((nb, tm), lambda i, j, k: (0, i)),
         pl.BlockSpec((nb, tn), lambda i, j, k: (0, j))],
        (d // tm, nl // tn, 1), 1, TN,
        jax.ShapeDtypeStruct((d, nl), F32),
        pl.BlockSpec((tm, tn), lambda i, j, k: (i, j)), (tm, tn), epilogue, a_prologue=_silu)


def _rows_call(name, body, operands, in_specs, out_shape, out_specs, n_tiles):
    return pl.pallas_call(
        body, name=name, grid=(n_tiles,), in_specs=in_specs, out_specs=out_specs, out_shape=out_shape,
        compiler_params=_params(1),
    )(*operands)


def _row_spec(tr, width):
    return pl.BlockSpec((tr, width), lambda i: (i, 0))


def _vec_spec(width):
    return pl.BlockSpec((1, width), lambda i: (0, 0))


def _accumulate(ref, val):
    first = pl.program_id(0) == 0

    @pl.when(first)
    def _():
        ref[...] = val

    @pl.when(jnp.logical_not(first))
    def _():
        ref[...] += val


def _prenorm(x, g, scale, shift):
    s, d = x.shape
    tr = _tile(s, (256, 128))

    def body(x_ref, g_ref, sc_ref, sh_ref, h_ref):
        xv = x_ref[...]
        rstd = lax.rsqrt(jnp.mean(xv * xv, axis=-1, keepdims=True) + EPS)
        h_ref[...] = ((xv * rstd) * g_ref[...] * (1.0 + sc_ref[...]) + sh_ref[...]).astype(BF16)

    return _rows_call("prenorm", body, (x, g, scale, shift),
                      [_row_spec(tr, d), _vec_spec(d), _vec_spec(d), _vec_spec(d)],
                      jax.ShapeDtypeStruct((s, d), BF16), _row_spec(tr, d), s // tr)


def _prenorm_bwd(x, dh, dres, g, scale):
    s, d = x.shape
    tr = _tile(s, (256, 128))

    def body(x_ref, dh_ref, dres_ref, g_ref, sc_ref, dx_ref, dshift_ref, dscale_ref, dg_ref):
        xv = x_ref[...]
        dhv = dh_ref[...].astype(F32)
        rstd = lax.rsqrt(jnp.mean(xv * xv, axis=-1, keepdims=True) + EPS)
        xhat = xv * rstd
        gv = g_ref[...]
        one_sc = 1.0 + sc_ref[...]
        dxhat = dhv * gv * one_sc
        dx_ref[...] = dres_ref[...] + rstd * (dxhat - xhat * jnp.mean(dxhat * xhat, axis=-1, keepdims=True))
        _accumulate(dshift_ref, jnp.sum(dhv, axis=0, keepdims=True))
        _accumulate(dscale_ref, jnp.sum(dhv * xhat * gv, axis=0, keepdims=True))
        _accumulate(dg_ref, jnp.sum(dhv * xhat * one_sc, axis=0, keepdims=True))

    vec = jax.ShapeDtypeStruct((1, d), F32)
    return _rows_call("prenorm_bwd", body, (x, dh, dres, g, scale),
                      [_row_spec(tr, d), _row_spec(tr, d), _row_spec(tr, d), _vec_spec(d), _vec_spec(d)],
                      [jax.ShapeDtypeStruct((s, d), F32), vec, vec, vec],
                      [_row_spec(tr, d), _vec_spec(d), _vec_spec(d), _vec_spec(d)], s // tr)


def _gate_bwd(dx, out, gate):
    s, d = dx.shape
    tr = _tile(s, (256, 128))

    def body(dx_ref, out_ref, gate_ref, dout_ref, dgate_ref):
        dxv = dx_ref[...]
        dout_ref[...] = (gate_ref[...] * dxv).astype(BF16)
        _accumulate(dgate_ref, jnp.sum(dxv * out_ref[...].astype(F32), axis=0, keepdims=True))

    return _rows_call("gate_bwd", body, (dx, out, gate),
                      [_row_spec(tr, d), _row_spec(tr, d), _vec_spec(d)],
                      [jax.ShapeDtypeStruct((s, d), BF16), jax.ShapeDtypeStruct((1, d), F32)],
                      [_row_spec(tr, d), _vec_spec(d)], s // tr)


def _final_loss(x, target, g):
    s, d = x.shape
    tr = _tile(s, (256, 128))
    n_tiles = s // tr

    def body(x_ref, t_ref, g_ref, loss_ref, dx_ref, dg_ref, acc_ref):
        xv = x_ref[...]
        rstd = lax.rsqrt(jnp.mean(xv * xv, axis=-1, keepdims=True) + EPS)
        xhat = xv * rstd
        gv = g_ref[...]
        err = xhat * gv - t_ref[...]
        dy = err * (1.0 / d)
        dxhat = dy * gv
        dx_ref[...] = rstd * (dxhat - xhat * jnp.mean(dxhat * xhat, axis=-1, keepdims=True))
        _accumulate(dg_ref, jnp.sum(dy * xhat, axis=0, keepdims=True))
        _accumulate(acc_ref, jnp.sum(err * err, axis=0, keepdims=True))

        @pl.when(pl.program_id(0) == n_tiles - 1)
        def _():
            loss_ref[...] = (0.5 / d) * jnp.sum(acc_ref[...], axis=1, keepdims=True)

    return pl.pallas_call(
        body, name="final_loss", grid=(n_tiles,),
        in_specs=[_row_spec(tr, d), _row_spec(tr, d), _vec_spec(d)],
        out_specs=[pl.BlockSpec((1, 1), lambda i: (0, 0)), _row_spec(tr, d), _vec_spec(d)],
        out_shape=[jax.ShapeDtypeStruct((1, 1), F32), jax.ShapeDtypeStruct((s, d), F32),
                   jax.ShapeDtypeStruct((1, d), F32)],
        scratch_shapes=[pltpu.VMEM((1, d), F32)],
        compiler_params=_params(1),
    )(x, target, g)


def _rope(t, cos, sin):
    return t * cos + pltpu.roll(t, HEAD_DIM // 2, axis=1) * sin


def _unrope(dt, cos, sin):
    return dt * cos + pltpu.roll(dt * sin, HEAD_DIM // 2, axis=1)


def _band_blocks(s, dil):
    sub = s // dil
    kw = min(K_WINDOW, sub)

    def rows(r, start, n):
        if dil == 1:
            return pl.ds(pl.multiple_of(start, RADIUS), n)
        return pl.ds(r + dil * start, n, stride=dil)

    def window(idx):
        nb = sub // Q_BLOCK
        r, b = idx // nb, idx % nb
        q0 = b * Q_BLOCK
        start = jnp.clip(q0 - RADIUS, 0, sub - kw)
        qi = q0 + lax.broadcasted_iota(jnp.int32, (Q_BLOCK, kw), 0)
        ki = start + lax.broadcasted_iota(jnp.int32, (Q_BLOCK, kw), 1)
        return rows(r, q0, Q_BLOCK), rows(r, start, kw), jnp.abs(qi - ki) <= RADIUS

    return window


def _head_col(s, group, nh):
    return pl.BlockSpec((s, HEAD_DIM), lambda h: (0, group * nh + h), pipeline_mode=pl.Buffered(1))


def _attn_fwd(proj, cos, sin, aw):
    s = proj.shape[0]
    nh = aw // HEAD_DIM
    scale = HEAD_DIM ** -0.5
    n_blocks = s // Q_BLOCK

    def body(q_ref, k_ref, v_ref, cos_ref, sin_ref, attn_ref, lse_ref, qf, kf, vf, acc):
        cosv, sinv = cos_ref[...], sin_ref[...]
        qf[...] = _rope(q_ref[...].astype(F32), cosv, sinv)
        kf[...] = _rope(k_ref[...].astype(F32), cosv, sinv)
        vf[...] = v_ref[...].astype(F32)

        for pattern, dil in enumerate(DILATIONS):
            window = _band_blocks(s, dil)

            def block(idx, carry, window=window, first=(pattern == 0)):
                q_rows, k_rows, valid = window(idx)
                q = qf[q_rows, :].astype(BF16)
                kk = kf[k_rows, :].astype(BF16)
                vv = vf[k_rows, :].astype(BF16)
                sc = lax.dot_general(q, kk, NT, preferred_element_type=F32) * scale
                sc = jnp.where(valid, sc, NEG_INF)
                m = jnp.max(sc, axis=1, keepdims=True)
                p = jnp.exp(sc - m)
                den = jnp.sum(p, axis=1, keepdims=True)
                o = lax.dot_general(p.astype(BF16), vv, NN, preferred_element_type=F32) / den
                lse = jnp.broadcast_to(m + jnp.log(den), (Q_BLOCK, HEAD_DIM))
                if first:
                    acc[q_rows, :] = o
                    lse_ref[q_rows, :] = lse
                else:
                    lse_old = lse_ref[q_rows, :]
                    top = jnp.maximum(lse_old, lse)
                    w_old, w_new = jnp.exp(lse_old - top), jnp.exp(lse - top)
                    tot = w_old + w_new
                    acc[q_rows, :] = (acc[q_rows, :] * w_old + o * w_new) / tot
                    lse_ref[q_rows, :] = top + jnp.log(tot)
                return carry

            lax.fori_loop(0, n_blocks, block, 0)

        attn_ref[...] = acc[...].astype(BF16)

    table = pl.BlockSpec((s, HEAD_DIM), lambda h: (0, 0), pipeline_mode=pl.Buffered(1))
    out = pl.BlockSpec((s, HEAD_DIM), lambda h: (0, h))
    return pl.pallas_call(
        body, name="attn_fwd", grid=(nh,),
        in_specs=[_head_col(s, 0, nh), _head_col(s, 1, nh), _head_col(s, 2, nh), table, table],
        out_specs=[out, out],
        out_shape=[jax.ShapeDtypeStruct((s, aw), BF16), jax.ShapeDtypeStruct((s, aw), F32)],
        scratch_shapes=[pltpu.VMEM((s, HEAD_DIM), F32)] * 4,
        compiler_params=_params(1, parallel=1),
    )(proj, proj, proj, cos, sin)


def _attn_bwd(proj, cos, sin, dy, attn, lse, aw):
    s = proj.shape[0]
    nh = aw // HEAD_DIM
    scale = HEAD_DIM ** -0.5
    n_blocks = s // Q_BLOCK

    def body(q_ref, k_ref, v_ref, za_ref, cos_ref, sin_ref, dy_ref, attn_ref, lse_ref,
             dq_ref, dk_ref, dv_ref, qf, kf, vf, dof, delta, dqa, dka, dva):
        cosv, sinv = cos_ref[...], sin_ref[...]
        qf[...] = _rope(q_ref[...].astype(F32), cosv, sinv)
        kf[...] = _rope(k_ref[...].astype(F32), cosv, sinv)
        vf[...] = v_ref[...].astype(F32)
        do_all = dy_ref[...].astype(F32) * _silu(za_ref[...].astype(F32))
        dof[...] = do_all
        delta[...] = jnp.broadcast_to(
            jnp.sum(do_all * attn_ref[...].astype(F32), axis=1, keepdims=True), (s, HEAD_DIM))
        dqa[...] = jnp.zeros_like(dqa)
        dka[...] = jnp.zeros_like(dka)
        dva[...] = jnp.zeros_like(dva)

        for dil in DILATIONS:
            window = _band_blocks(s, dil)

            def block(idx, carry, window=window):
                q_rows, k_rows, valid = window(idx)
                q = qf[q_rows, :].astype(BF16)
                kk = kf[k_rows, :].astype(BF16)
                vv = vf[k_rows, :].astype(BF16)
                dov = dof[q_rows, :].astype(BF16)
                lse_q = lse_ref[q_rows, :][:, 0:1]
                delta_q = delta[q_rows, :][:, 0:1]
                sc = lax.dot_general(q, kk, NT, preferred_element_type=F32) * scale
                p = jnp.where(valid, jnp.exp(sc - lse_q), 0.0)
                dp = lax.dot_general(dov, vv, NT, preferred_element_type=F32)
                ds = (p * (dp - delta_q) * scale).astype(BF16)
                dqa[q_rows, :] += lax.dot_general(ds, kk, NN, preferred_element_type=F32)
                dka[k_rows, :] += lax.dot_general(ds, q, TN, preferred_element_type=F32)
                dva[k_rows, :] += lax.dot_general(p.astype(BF16), dov, TN, preferred_element_type=F32)
                return carry

            lax.fori_loop(0, n_blocks, block, 0)

        dq_ref[...] = _unrope(dqa[...], cosv, sinv).astype(BF16)
        dk_ref[...] = _unrope(dka[...], cosv, sinv).astype(BF16)
        dv_ref[...] = dva[...].astype(BF16)

    own = pl.BlockSpec((s, HEAD_DIM), lambda h: (0, h), pipeline_mode=pl.Buffered(1))
    table = pl.BlockSpec((s, HEAD_DIM), lambda h: (0, 0), pipeline_mode=pl.Buffered(1))
    out = pl.BlockSpec((s, HEAD_DIM), lambda h: (0, h))
    shape = jax.ShapeDtypeStruct((s, aw), BF16)
    return pl.pallas_call(
        body, name="attn_bwd", grid=(nh,),
        in_specs=[_head_col(s, 0, nh), _head_col(s, 1, nh), _head_col(s, 2, nh), _head_col(s, 3, nh),
                  table, table, own, own, own],
        out_specs=[out, out, out],
        out_shape=[shape, shape, shape],
        scratch_shapes=[pltpu.VMEM((s, HEAD_DIM), F32)] * 8,
        compiler_params=_params(1, parallel=1),
    )(proj, proj, proj, proj, cos, sin, dy, attn, lse)


def _rope_tables(s):
    half = HEAD_DIM // 2
    inv = ROPE_THETA ** (-jnp.arange(half, dtype=F32) / half)
    ang = jnp.arange(s, dtype=F32)[:, None] * inv[None, :]
    cos, sin = jnp.cos(ang), jnp.sin(ang)
    return jnp.concatenate([cos, cos], axis=-1), jnp.concatenate([-sin, sin], axis=-1)


def _conv_chunks(s):
    for k in range(s // CONV_ROWS):
        lo = max(0, k * CONV_ROWS - CONV_HALO)
        hi = min(s, (k + 1) * CONV_ROWS + CONV_HALO)
        yield k * CONV_ROWS, lo, hi


def _neighbours(p, lo, s):
    n = p.shape[0]
    row = lo + lax.broadcasted_iota(jnp.int32, p.shape, 0)
    prev = jnp.where(row == 0, 0.0, pltpu.roll(p, 1, axis=0))
    nxt = jnp.where(row == s - 1, 0.0, pltpu.roll(p, n - 1, axis=0))
    return prev, nxt


def _ab_mix(attn, proj, conv_w, aw):
    s = proj.shape[0]
    nt = aw // LANES

    def col(group, sel):
        return pl.BlockSpec((s, LANES), lambda i: (0, group * nt + sel(i)))

    a_sel = lambda i: jnp.minimum(i, nt - 1)
    b_sel = lambda i: jnp.maximum(i - nt, 0)

    def body(attn_ref, za_ref, ub_ref, gb_ref, gc_ref, zb_ref, w_ref, y_ref):
        i = pl.program_id(0)

        @pl.when(i < nt)
        def _():
            y_ref[...] = (attn_ref[...].astype(F32) * _silu(za_ref[...].astype(F32))).astype(BF16)

        @pl.when(i >= nt)
        def _():
            w = w_ref[...]
            for c0, lo, hi in _conv_chunks(s):
                p = gc_ref[lo:hi, :].astype(F32) * ub_ref[lo:hi, :].astype(F32)
                prev, nxt = _neighbours(p, lo, s)
                cv = w[0:1, :] * prev + w[1:2, :] * p + w[2:3, :] * nxt
                yb = gb_ref[lo:hi, :].astype(F32) * cv * _silu(zb_ref[lo:hi, :].astype(F32))
                y_ref[c0:c0 + CONV_ROWS, :] = yb[c0 - lo:c0 - lo + CONV_ROWS, :].astype(BF16)

    return pl.pallas_call(
        body, name="ab_mix", grid=(2 * nt,),
        in_specs=[pl.BlockSpec((s, LANES), lambda i: (0, a_sel(i))),
                  col(3, a_sel), col(4, b_sel), col(5, b_sel), col(6, b_sel), col(7, b_sel),
                  pl.BlockSpec((3, LANES), lambda i: (0, b_sel(i)))],
        out_specs=pl.BlockSpec((s, LANES), lambda i: (0, i)),
        out_shape=jax.ShapeDtypeStruct((s, 2 * aw), BF16),
        compiler_params=_params(1),
    )(attn, proj, proj, proj, proj, proj, conv_w)


def _ab_dproj(dqkv, dy, attn, proj, conv_w, aw):
    s = proj.shape[0]
    nt = aw // LANES

    def col(group, sel):
        return pl.BlockSpec((s, LANES), lambda i: (0, group * nt + sel(i)))

    def qkv_spec(part):
        return pl.BlockSpec((s, LANES), lambda i: (0, jnp.clip(i - part * nt, 0, nt - 1)))

    a_sel = lambda i: jnp.clip(i - 3 * nt, 0, nt - 1)
    b_sel = lambda i: jnp.maximum(i - 4 * nt, 0) % nt
    w_sel = lambda i: jnp.clip(i - 4 * nt, 0, nt - 1)

    def body(*refs):
        g_refs = refs[:3]
        dya_ref, attn_ref, za_ref, dyb_ref, ub_ref, gb_ref, gc_ref, zb_ref, w_ref, out_ref, dw_ref = refs[3:]
        i = pl.program_id(0)

        for part in range(3):
            @pl.when(jnp.logical_and(i >= part * nt, i < (part + 1) * nt))
            def _(part=part):
                out_ref[...] = g_refs[part][...]

        @pl.when(jnp.logical_and(i >= 3 * nt, i < 4 * nt))
        def _():
            out_ref[...] = (dya_ref[...].astype(F32) * attn_ref[...].astype(F32)
                            * _dsilu(za_ref[...].astype(F32))).astype(BF16)

        for which in range(4):
            @pl.when(jnp.logical_and(i >= (4 + which) * nt, i < (5 + which) * nt))
            def _(which=which):
                w = w_ref[...]
                dw = [jnp.zeros((1, LANES), F32) for _ in range(3)]
                for c0, lo, hi in _conv_chunks(s):
                    ctr = slice(c0 - lo, c0 - lo + CONV_ROWS)
                    ub = ub_ref[lo:hi, :].astype(F32)
                    gc = gc_ref[lo:hi, :].astype(F32)
                    gb = gb_ref[lo:hi, :].astype(F32)
                    zb = zb_ref[lo:hi, :].astype(F32)
                    dyb = dyb_ref[lo:hi, :].astype(F32)
                    p = gc * ub
                    prev, nxt = _neighbours(p, lo, s)
                    if which == 1:
                        cv = w[0:1, :] * prev + w[1:2, :] * p + w[2:3, :] * nxt
                        res = dyb * cv * _silu(zb)
                    elif which == 3:
                        cv = w[0:1, :] * prev + w[1:2, :] * p + w[2:3, :] * nxt
                        res = dyb * gb * cv * _dsilu(zb)
                    else:
                        dcv = dyb * gb * _silu(zb)
                        dprev, dnxt = _neighbours(dcv, lo, s)
                        dp = w[0:1, :] * dnxt + w[1:2, :] * dcv + w[2:3, :] * dprev
                        res = dp * (gc if which == 0 else ub)
                        if which == 0:
                            for t, nb in enumerate((prev, p, nxt)):
                                dw[t] = dw[t] + jnp.sum((dcv * nb)[ctr, :], axis=0, keepdims=True)
                    out_ref[c0:c0 + CONV_ROWS, :] = res[ctr, :].astype(BF16)
                if which == 0:
                    dw_ref[...] = jnp.concatenate(dw, axis=0)

    return pl.pallas_call(
        body, name="ab_dproj", grid=(8 * nt,),
        in_specs=[qkv_spec(0), qkv_spec(1), qkv_spec(2),
            pl.BlockSpec((s, LANES), lambda i: (0, a_sel(i))),
            pl.BlockSpec((s, LANES), lambda i: (0, a_sel(i))),
            col(3, a_sel),
            pl.BlockSpec((s, LANES), lambda i: (0, nt + b_sel(i))),
            col(4, b_sel), col(5, b_sel), col(6, b_sel), col(7, b_sel),
            pl.BlockSpec((3, LANES), lambda i: (0, b_sel(i)))],
        out_specs=[pl.BlockSpec((s, LANES), lambda i: (0, i)),
                   pl.BlockSpec((3, LANES), lambda i: (0, w_sel(i)))],
        out_shape=[jax.ShapeDtypeStruct((s, 8 * aw), BF16), jax.ShapeDtypeStruct((3, aw), F32)],
        compiler_params=_params(1),
    )(*dqkv, dy, attn, proj, dy, proj, proj, proj, proj, conv_w)


def _sgu_norm(v, ln_g, ln_b):
    gv = _gelu(v)
    mu = jnp.mean(gv, axis=-1, keepdims=True)
    xc = gv - mu
    rstd = lax.rsqrt(jnp.mean(xc * xc, axis=-1, keepdims=True) + EPS)
    vhat = xc * rstd
    return vhat, rstd, vhat * ln_g + ln_b


def _sgu_fwd(uvz, ln_g, ln_b, w_s, b_s, cw):
    s = uvz.shape[0]
    tr = 2 * CHUNK if s % (2 * CHUNK) == 0 else CHUNK
    gw = cw // N_GROUPS

    def body(u_ref, v_ref, z_ref, g_ref, b_ref, ws_ref, bs_ref, y_ref):
        _, _, vn = _sgu_norm(v_ref[...].astype(F32), g_ref[...], b_ref[...])
        vn = vn.astype(BF16)
        for ch in range(tr // CHUNK):
            rows = slice(ch * CHUNK, (ch + 1) * CHUNK)
            for grp in range(N_GROUPS):
                cols = slice(grp * gw, (grp + 1) * gw)
                mixed = lax.dot_general(ws_ref[grp], vn[rows, cols], NN, preferred_element_type=F32) + bs_ref[grp]
                y_ref[rows, cols] = (_gelu(u_ref[rows, cols].astype(F32)) * mixed
                                     * _silu(z_ref[rows, cols].astype(F32))).astype(BF16)

    full3 = lambda shape: pl.BlockSpec(shape, lambda i: (0, 0, 0))
    return pl.pallas_call(
        body, name="sgu_fwd", grid=(s // tr,),
        in_specs=[pl.BlockSpec((tr, cw), lambda i: (i, 0)), pl.BlockSpec((tr, cw), lambda i: (i, 1)),
                  pl.BlockSpec((tr, cw), lambda i: (i, 2)), _vec_spec(cw), _vec_spec(cw),
                  full3(w_s.shape), full3(b_s.shape)],
        out_specs=pl.BlockSpec((tr, cw), lambda i: (i, 0)),
        out_shape=jax.ShapeDtypeStruct((s, cw), BF16),
        compiler_params=_params(1, parallel=1),
    )(uvz, uvz, uvz, ln_g, ln_b, w_s, b_s)


def _sgu_bwd(uvz, dy, ln_g, ln_b, w_s, b_s, cw):
    s = uvz.shape[0]
    tr = 2 * CHUNK if s % (2 * CHUNK) == 0 else CHUNK
    gw = cw // N_GROUPS

    def body(u_ref, v_ref, z_ref, dy_ref, g_ref, b_ref, ws_ref, bs_ref,
             du_ref, dv_ref, dz_ref, dws_ref, dbs_ref, dg_ref, db_ref, dvn_ref):
        vv = v_ref[...].astype(F32)
        gvec = g_ref[...]
        vhat, rstd, vn = _sgu_norm(vv, gvec, b_ref[...])
        vn = vn.astype(BF16)
        first = pl.program_id(0) == 0

        @pl.when(first)
        def _():
            dws_ref[...] = jnp.zeros_like(dws_ref)
            dbs_ref[...] = jnp.zeros_like(dbs_ref)

        for ch in range(tr // CHUNK):
            rows = slice(ch * CHUNK, (ch + 1) * CHUNK)
            for grp in range(N_GROUPS):
                cols = slice(grp * gw, (grp + 1) * gw)
                vn_g = vn[rows, cols]
                mixed = lax.dot_general(ws_ref[grp], vn_g, NN, preferred_element_type=F32) + bs_ref[grp]
                uu = u_ref[rows, cols].astype(F32)
                zz = z_ref[rows, cols].astype(F32)
                dyv = dy_ref[rows, cols].astype(F32)
                gu, sz = _gelu(uu), _silu(zz)
                du_ref[rows, cols] = (dyv * mixed * sz * _dgelu(uu)).astype(BF16)
                dz_ref[rows, cols] = (dyv * gu * mixed * _dsilu(zz)).astype(BF16)
                dmixed = dyv * gu * sz
                dm16 = dmixed.astype(BF16)
                dws_ref[grp] += lax.dot_general(dm16, vn_g, NT, preferred_element_type=F32)
                dbs_ref[grp] += jnp.broadcast_to(jnp.sum(dmixed, axis=1, keepdims=True), (CHUNK, LANES))
                dvn_ref[rows, cols] = lax.dot_general(ws_ref[grp], dm16, TN, preferred_element_type=F32)

        dvn = dvn_ref[...]
        _accumulate(dg_ref, jnp.sum(dvn * vhat, axis=0, keepdims=True))
        _accumulate(db_ref, jnp.sum(dvn, axis=0, keepdims=True))
        dvhat = dvn * gvec
        dgv = rstd * (dvhat - jnp.mean(dvhat, axis=-1, keepdims=True)
                      - vhat * jnp.mean(dvhat * vhat, axis=-1, keepdims=True))
        dv_ref[...] = (dgv * _dgelu(vv)).astype(BF16)

    full3 = lambda shape: pl.BlockSpec(shape, lambda i: (0, 0, 0))
    acc3 = jax.ShapeDtypeStruct((N_GROUPS, CHUNK, LANES), F32)
    vec = jax.ShapeDtypeStruct((1, cw), F32)
    act = jax.ShapeDtypeStruct((s, cw), BF16)
    row = pl.BlockSpec((tr, cw), lambda i: (i, 0))
    return pl.pallas_call(
        body, name="sgu_bwd", grid=(s // tr,),
        in_specs=[row, pl.BlockSpec((tr, cw), lambda i: (i, 1)), pl.BlockSpec((tr, cw), lambda i: (i, 2)),
                  row, _vec_spec(cw), _vec_spec(cw), full3(w_s.shape), full3(b_s.shape)],
        out_specs=[row, row, row, full3((N_GROUPS, CHUNK, LANES)), full3((N_GROUPS, CHUNK, LANES)),
                   _vec_spec(cw), _vec_spec(cw)],
        out_shape=[act, act, act, acc3, acc3, vec, vec],
        scratch_shapes=[pltpu.VMEM((tr, cw), F32)],
        compiler_params=_params(1),
    )(uvz, uvz, uvz, dy, ln_g, ln_b, w_s, b_s)


def _flat_rows(a):
    return a.reshape(-1, a.shape[-1])


def _add_sibling(grads, recv, layer_idx):
    _, nchip, k, n = grads.shape
    g2 = grads.reshape(2, nchip * k, n)
    r2 = recv.reshape(nchip * k, n)
    tr = _tile(nchip * k, (512, 256, 128))

    def body(c_ref, g_ref, r_ref, o_ref):
        o_ref[...] = (g_ref[...].astype(F32) + r_ref[...].astype(F32)).astype(BF16)

    out = pl.pallas_call(
        body, name="add_sibling",
        grid_spec=pltpu.PrefetchScalarGridSpec(
            num_scalar_prefetch=1, grid=(nchip * k // tr,),
            in_specs=[pl.BlockSpec((None, tr, n), lambda i, c: (c[0], i, 0)),
                      pl.BlockSpec((tr, n), lambda i, c: (i, 0))],
            out_specs=pl.BlockSpec((tr, n), lambda i, c: (i, 0))),
        out_shape=jax.ShapeDtypeStruct((nchip * k, n), BF16),
        compiler_params=_params(1, parallel=1),
    )(layer_idx, g2, r2)
    return out.reshape(nchip, k, n)


def _sum_chips(own, others, place):
    _, k, n = own.shape
    tr = _tile(k, (256, 128))

    def body(place_ref, own_ref, oth_ref, o_ref):
        acc = own_ref[...].astype(F32)
        for q in range(3):
            acc = acc + oth_ref[q].astype(F32)
        o_ref[...] = acc

    return pl.pallas_call(
        body, name="sum_chips",
        grid_spec=pltpu.PrefetchScalarGridSpec(
            num_scalar_prefetch=1, grid=(k // tr,),
            in_specs=[pl.BlockSpec((None, tr, n), lambda i, p: (p[0], i, 0)),
                      pl.BlockSpec((3, tr, n), lambda i, p: (0, i, 0))],
            out_specs=pl.BlockSpec((None, tr, n), lambda i, p: (p[1], i, 0))),
        out_shape=jax.ShapeDtypeStruct((2, k, n), F32),
        compiler_params=_params(1, parallel=1),
    )(place, own, others)


def _sum_devices(parts):
    nd, r, _ = parts.shape
    tr = _tile(r, (512, 256, 128, 64, 32, 16, 8))

    def body(p_ref, o_ref):
        acc = p_ref[0]
        for q in range(1, nd):
            acc = acc + p_ref[q]
        o_ref[...] = acc

    return pl.pallas_call(
        body, name="sum_devices", grid=(r // tr,),
        in_specs=[pl.BlockSpec((nd, tr, LANES), lambda i: (0, i, 0))],
        out_specs=pl.BlockSpec((tr, LANES), lambda i: (i, 0)),
        out_shape=jax.ShapeDtypeStruct((r, LANES), F32),
        compiler_params=_params(1, parallel=1),
    )(parts)


def _adamw(w, g, m, v):
    r, n = w.shape
    tr = _tile(r, [p for p in (1024, 512, 256, 128, 64, 32, 16, 8) if p * n <= ELEMENTWISE_BLOCK])

    def body(w_ref, g_ref, m_ref, v_ref, d_ref, nm_ref, nv_ref):
        gv = g_ref[...]
        nm = ADAM_B1 * m_ref[...] + (1.0 - ADAM_B1) * gv
        nv = ADAM_B2 * v_ref[...] + (1.0 - ADAM_B2) * (gv * gv)
        m_hat = nm / (1.0 - ADAM_B1 ** ADAM_STEP)
        v_hat = nv / (1.0 - ADAM_B2 ** ADAM_STEP)
        d_ref[...] = -ADAM_LR * (m_hat / (jnp.sqrt(v_hat) + ADAM_EPS) + ADAM_WD * w_ref[...])
        nm_ref[...] = nm
        nv_ref[...] = nv

    spec = pl.BlockSpec((tr, n), lambda i: (i, 0))
    shp = jax.ShapeDtypeStruct((r, n), F32)
    return pl.pallas_call(
        body, name="adamw", grid=(r // tr,),
        in_specs=[spec] * 4, out_specs=[spec] * 3, out_shape=[shp] * 3,
        compiler_params=_params(1, parallel=1),
    )(w, g, m, v)


def _pack(arrays, row_multiple=8):
    flat = [a.reshape(-1) for a in arrays]
    sizes = [f.shape[0] for f in flat]
    total = sum(sizes)
    unit = LANES * row_multiple
    padded = -(-total // unit) * unit
    if padded > total:
        flat.append(jnp.zeros((padded - total,), F32))
    offsets = [sum(sizes[:i]) for i in range(len(sizes))]
    return jnp.concatenate(flat).reshape(-1, LANES), offsets


def _unpack(packed, offsets, shapes):
    flat = packed.reshape(-1)
    return [flat[o:o + math.prod(s)].reshape(s) for o, s in zip(offsets, shapes)]


def kernel(x, c, ab_norm_g, ab_w_mod, ab_b_mod, ab_w_in, ab_conv_w, ab_w_out, sg_norm_g, sg_w_mod, sg_b_mod, sg_w_in, sg_ln_g, sg_ln_b, sg_w_s, sg_b_s, sg_w_out, final_norm_g, loss_target, m_ab_norm_g, m_ab_w_mod, m_ab_b_mod, m_ab_w_in, m_ab_conv_w, m_ab_w_out, m_sg_norm_g, m_sg_w_mod, m_sg_b_mod, m_sg_w_in, m_sg_ln_g, m_sg_ln_b, m_sg_w_s, m_sg_b_s, m_sg_w_out, m_final_norm_g, v_ab_norm_g, v_ab_w_mod, v_ab_b_mod, v_ab_w_in, v_ab_conv_w, v_ab_w_out, v_sg_norm_g, v_sg_w_mod, v_sg_b_mod, v_sg_w_in, v_sg_ln_g, v_sg_ln_b, v_sg_w_s, v_sg_b_s, v_sg_w_out, v_final_norm_g):
    s, d = x.shape[1], x.shape[2]
    aw = d // 2
    nh = aw // HEAD_DIM
    cw = d
    mod_l = ab_w_mod.shape[-1]
    x0 = x[0]
    target = loss_target[0]
    mx, my, mc = lax.axis_index("x"), lax.axis_index("y"), lax.axis_index("c")
    chip = 2 * mx + my
    me = 2 * chip + mc

    small_local = [c[0], ab_conv_w, sg_norm_g, sg_ln_g, sg_ln_b]
    small_shapes = [a.shape for a in small_local]
    payload, small_off = _pack(small_local)
    gathered = _all_to_all(jnp.broadcast_to(payload[None], (N_DEV,) + payload.shape), "gather_small")
    per_dev = [_unpack(gathered[b], small_off, small_shapes) for b in range(N_DEV)]
    c_all = jnp.stack([per_dev[b][0] for b in range(N_DEV)])

    def from_chips(idx, axis):
        return jnp.concatenate([per_dev[2 * q][idx] for q in range(N_CHIPS)], axis=axis)

    conv_w_full = from_chips(1, 2)
    sg_norm_g_full = from_chips(2, 1)
    sg_ln_g_full = from_chips(3, 1)
    sg_ln_b_full = from_chips(4, 1)

    ab_b_local = lax.dynamic_slice_in_dim(ab_b_mod, chip * mod_l, mod_l, axis=1)
    mod_rows = []
    for layer in range(4):
        i = layer // 2
        w_mod, bias = (ab_w_mod, ab_b_local) if layer % 2 == 0 else (sg_w_mod, sg_b_mod)
        mod_rows.append(_mod_fwd(c_all, w_mod, bias[i:i + 1], i))
    mod_local = jnp.stack(mod_rows, axis=1)
    mod_recv = _all_to_all(mod_local.reshape(N_DEV, -1, LANES), "exchange_mod")
    mod_recv = mod_recv.reshape(N_DEV, 4, mod_l)
    mod_full = jnp.concatenate([mod_recv[2 * q] for q in range(N_CHIPS)], axis=-1)
    shifts = [mod_full[l:l + 1, :d] for l in range(4)]
    scales = [mod_full[l:l + 1, d:2 * d] for l in range(4)]
    gates = [mod_full[l:l + 1, 2 * d:] for l in range(4)]

    chip_idx = jnp.reshape(chip, (1,)).astype(jnp.int32)
    w_in_ab, w_out_ab, w_in_sg, w_out_sg = _gather_weights(
        [_place_own_shard(w, chip_idx) for w in (ab_w_in, ab_w_out, sg_w_in, sg_w_out)])
    w_out_ab2 = w_out_ab.reshape(2, -1, d)
    w_out_sg2 = w_out_sg.reshape(2, -1, d)

    cos, sin = _rope_tables(s)
    w_s16 = sg_w_s.astype(BF16)
    b_s3 = sg_b_s[..., None]

    saved = []
    xs = x0
    for layer in range(4):
        i = layer // 2
        if layer % 2 == 0:
            h = _prenorm(xs, ab_norm_g[i:i + 1], scales[layer], shifts[layer])
            proj = _in_proj(h, w_in_ab, i)
            attn, lse = _attn_fwd(proj, cos, sin, aw)
            y = _ab_mix(attn, proj, conv_w_full[i], aw)
            x_next, out = _out_proj_residual(y, w_out_ab2, i, xs, gates[layer])
            saved.append((xs, h, proj, y, out, attn, lse))
        else:
            h = _prenorm(xs, sg_norm_g_full[i:i + 1], scales[layer], shifts[layer])
            uvz = _in_proj(h, w_in_sg, i)
            y = _sgu_fwd(uvz, sg_ln_g_full[i:i + 1], sg_ln_b_full[i:i + 1], w_s16[i], b_s3[i], cw)
            x_next, out = _out_proj_residual(y, w_out_sg2, i, xs, gates[layer])
            saved.append((xs, h, uvz, y, out))
        xs = x_next

    loss11, dx, d_final_g = _final_loss(xs, target, final_norm_g[None])
    loss = lax.psum(loss11[0, 0], ("x", "y", "c"))

    g_in_ab = lax.empty(w_in_ab.shape, BF16)
    g_out_ab = lax.empty(w_out_ab2.shape, BF16)
    g_in_sg = lax.empty(w_in_sg.shape, BF16)
    g_out_sg = lax.empty(w_out_sg2.shape, BF16)
    dmods = [None] * 4
    d_ab_norm_g, d_sg_norm_g = [None, None], [None, None]
    d_conv_w, d_ln_g, d_ln_b, d_w_s, d_b_s = ([None, None] for _ in range(5))
    for layer in reversed(range(4)):
        i = layer // 2
        if layer % 2 == 0:
            xs, h, proj, y, out, attn, lse = saved[layer]
            dout, dgate = _gate_bwd(dx, out, gates[layer])
            dy = _out_proj_bwd_act(dout, w_out_ab2, i)
            g_out_ab = _out_proj_bwd_w(y, dout, g_out_ab, i)
            dqkv = _attn_bwd(proj, cos, sin, dy, attn, lse, aw)
            dproj, d_conv_w[i] = _ab_dproj(dqkv, dy, attn, proj, conv_w_full[i], aw)
            dh = _in_proj_bwd_act(dproj, w_in_ab, i)
            g_in_ab = _in_proj_bwd_w(h, dproj, g_in_ab, i)
            dx, dshift, dscale, d_ab_norm_g[i] = _prenorm_bwd(xs, dh, dx, ab_norm_g[i:i + 1], scales[layer])
        else:
            xs, h, uvz, y, out = saved[layer]
            dout, dgate = _gate_bwd(dx, out, gates[layer])
            dy = _out_proj_bwd_act(dout, w_out_sg2, i)
            g_out_sg = _out_proj_bwd_w(y, dout, g_out_sg, i)
            du, dv, dz, d_w_s[i], db_wide, d_ln_g[i], d_ln_b[i] = _sgu_bwd(
                uvz, dy, sg_ln_g_full[i:i + 1], sg_ln_b_full[i:i + 1], w_s16[i], b_s3[i], cw)
            d_b_s[i] = db_wide[:, :, 0]
            duvz = jnp.concatenate([du, dv, dz], axis=1)
            dh = _in_proj_bwd_act(duvz, w_in_sg, i)
            g_in_sg = _in_proj_bwd_w(h, duvz, g_in_sg, i)
            dx, dshift, dscale, d_sg_norm_g[i] = _prenorm_bwd(xs, dh, dx, sg_norm_g_full[i:i + 1], scales[layer])
        dmods[layer] = jnp.concatenate([dshift, dscale, dgate], axis=1)
    grad_x = dx[None]

    partial_list = [jnp.concatenate(dmods, axis=0),
                    jnp.concatenate(d_ab_norm_g, axis=0), jnp.concatenate(d_sg_norm_g, axis=0), d_final_g[0],
                    jnp.stack(d_conv_w), jnp.concatenate(d_ln_g, axis=0), jnp.concatenate(d_ln_b, axis=0),
                    jnp.stack(d_w_s), jnp.stack(d_b_s)]
    partial_shapes = [a.shape for a in partial_list]
    partials, part_off = _pack(partial_list)
    all_partials = _all_to_all(jnp.broadcast_to(partials[None], (N_DEV,) + partials.shape), "gather_partials")
    reduced = _unpack(_sum_devices(all_partials), part_off, partial_shapes)
    (g_mod_bias, g_ab_norm_g, g_sg_norm_g_full, g_final_g, g_conv_full, g_ln_g_full, g_ln_b_full,
     g_w_s, g_b_s) = reduced
    dm_all = jnp.stack([_unpack(all_partials[b], part_off[:1], partial_shapes[:1])[0] for b in range(N_DEV)])
    dm_local = lax.dynamic_slice_in_dim(dm_all, chip * mod_l, mod_l, axis=2)

    def chip_cols(a, axis):
        width = a.shape[axis] // N_CHIPS
        return lax.dynamic_slice_in_dim(a, chip * width, width, axis=axis)

    g_ab_b_mod = jnp.stack([g_mod_bias[0], g_mod_bias[2]])
    g_sg_b_mod = chip_cols(jnp.stack([g_mod_bias[1], g_mod_bias[3]]), 1)
    g_ab_w_mod = jnp.stack([_mod_bwd_w(c_all, dm_local[:, 0]), _mod_bwd_w(c_all, dm_local[:, 2])])
    g_sg_w_mod = jnp.stack([_mod_bwd_w(c_all, dm_local[:, 1]), _mod_bwd_w(c_all, dm_local[:, 3])])
    g_conv = chip_cols(g_conv_full, 2)
    g_sg_norm_g = chip_cols(g_sg_norm_g_full, 1)
    g_ln_g = chip_cols(g_ln_g_full, 1)
    g_ln_b = chip_cols(g_ln_b_full, 1)

    big = [g_in_ab, g_out_ab.reshape(w_out_ab.shape), g_in_sg, g_out_sg.reshape(w_out_sg.shape)]
    from_sib = _swap_layers_with_sibling(big)
    layer_idx = jnp.reshape(mc, (1,)).astype(jnp.int32)
    chip_sums = [_add_sibling(g, r, layer_idx) for g, r in zip(big, from_sib)]
    from_chips_sums = _scatter_chip_sums(chip_sums)
    place = jnp.stack([chip, mc]).astype(jnp.int32)
    reduced_big = [_sum_chips(own, oth, place) for own, oth in zip(chip_sums, from_chips_sums)]
    g_ab_w_in, g_ab_w_out, g_sg_w_in, g_sg_w_out = _share_layer_with_sibling(reduced_big)

    def step_big(w, g, m, v):
        dl, nm, nv = _adamw(_flat_rows(w), _flat_rows(g), _flat_rows(m), _flat_rows(v))
        return dl.reshape(w.shape), nm.reshape(w.shape), nv.reshape(w.shape)

    big_out = {
        "ab_w_mod": step_big(ab_w_mod, g_ab_w_mod, m_ab_w_mod, v_ab_w_mod),
        "ab_w_in": step_big(ab_w_in, g_ab_w_in, m_ab_w_in, v_ab_w_in),
        "ab_w_out": step_big(ab_w_out, g_ab_w_out, m_ab_w_out, v_ab_w_out),
        "sg_w_mod": step_big(sg_w_mod, g_sg_w_mod, m_sg_w_mod, v_sg_w_mod),
        "sg_w_in": step_big(sg_w_in, g_sg_w_in, m_sg_w_in, v_sg_w_in),
        "sg_w_out": step_big(sg_w_out, g_sg_w_out, m_sg_w_out, v_sg_w_out),
    }
    small_names = ["ab_norm_g", "ab_b_mod", "ab_conv_w", "sg_norm_g", "sg_b_mod", "sg_ln_g", "sg_ln_b",
                   "sg_w_s", "sg_b_s", "final_norm_g"]
    small_w = [ab_norm_g, ab_b_mod, ab_conv_w, sg_norm_g, sg_b_mod, sg_ln_g, sg_ln_b, sg_w_s, sg_b_s, final_norm_g]
    small_g = [g_ab_norm_g, g_ab_b_mod, g_conv, g_sg_norm_g, g_sg_b_mod, g_ln_g, g_ln_b, g_w_s, g_b_s, g_final_g]
    small_m = [m_ab_norm_g, m_ab_b_mod, m_ab_conv_w, m_sg_norm_g, m_sg_b_mod, m_sg_ln_g, m_sg_ln_b, m_sg_w_s,
               m_sg_b_s, m_final_norm_g]
    small_v = [v_ab_norm_g, v_ab_b_mod, v_ab_conv_w, v_sg_norm_g, v_sg_b_mod, v_sg_ln_g, v_sg_ln_b, v_sg_w_s,
               v_sg_b_s, v_final_norm_g]
    shapes = [a.shape for a in small_w]
    pw, off = _pack(small_w)
    pg, _ = _pack(small_g)
    pm, _ = _pack(small_m)
    pv, _ = _pack(small_v)
    pd, pnm, pnv = _adamw(pw, pg, pm, pv)
    small_out = {}
    for name, dl, nm, nv in zip(small_names, _unpack(pd, off, shapes), _unpack(pnm, off, shapes),
                                _unpack(pnv, off, shapes)):
        small_out[name] = (dl, nm, nv)

    grads = {
        "ab_norm_g": g_ab_norm_g, "ab_w_mod": g_ab_w_mod, "ab_b_mod": g_ab_b_mod, "ab_w_in": g_ab_w_in,
        "ab_conv_w": g_conv, "ab_w_out": g_ab_w_out, "sg_norm_g": g_sg_norm_g, "sg_w_mod": g_sg_w_mod,
        "sg_b_mod": g_sg_b_mod, "sg_w_in": g_sg_w_in, "sg_ln_g": g_ln_g, "sg_ln_b": g_ln_b, "sg_w_s": g_w_s,
        "sg_b_s": g_b_s, "sg_w_out": g_sg_w_out, "final_norm_g": g_final_g,
    }
    order = ["ab_norm_g", "ab_w_mod", "ab_b_mod", "ab_w_in", "ab_conv_w", "ab_w_out", "sg_norm_g", "sg_w_mod",
             "sg_b_mod", "sg_w_in", "sg_ln_g", "sg_ln_b", "sg_w_s", "sg_b_s", "sg_w_out", "final_norm_g"]
    steps = {**big_out, **small_out}
    return (loss, grad_x, *[grads[n] for n in order], *[steps[n][0] for n in order],
            *[steps[n][1] for n in order], *[steps[n][2] for n in order])
```

```python
import math
from typing import Any, Callable, NamedTuple

import jax
import jax.numpy as jnp
from jax import lax
from jax.experimental import pallas as pl
from jax.experimental.pallas import tpu as pltpu

F32 = jnp.float32
BF16 = jnp.bfloat16

HEAD_DIM = 128
RADIUS = 64
DILATIONS = (1, 4, 16)
Q_BLOCK = 256
K_WINDOW = Q_BLOCK + 2 * RADIUS
ROPE_THETA = 10000.0
NEG_INF = -1e30
N_GROUPS = 8
CHUNK = 128
EPS = 1e-6
CONV_ROWS = 512
CONV_HALO = 16
LANES = 128
ELEMENTWISE_BLOCK = 256 * 1024
N_DEV = 8
N_CHIPS = 4

ADAM_LR = 0.001
ADAM_B1 = 0.9
ADAM_B2 = 0.999
ADAM_EPS = 1e-08
ADAM_WD = 0.01
ADAM_STEP = 10

VMEM_LIMIT_V7X = 56 * 1024 * 1024

MESH_ID = pl.DeviceIdType.MESH
HBM_SPEC = pl.BlockSpec(memory_space=pltpu.HBM)

NN = (((1,), (0,)), ((), ()))
NT = (((1,), (1,)), ((), ()))
TN = (((0,), (0,)), ((), ()))


def _params(n_grid, parallel=0):
    sem = tuple(["parallel"] * parallel + ["arbitrary"] * (n_grid - parallel))
    return pltpu.CompilerParams(dimension_semantics=sem, vmem_limit_bytes=VMEM_LIMIT_V7X)


def _tile(n, prefs):
    for p in prefs:
        if n % p == 0:
            return p
    return n


def _sigmoid(z):
    return 1.0 / (1.0 + jnp.exp(-z))


def _silu(z):
    return z * _sigmoid(z)


def _dsilu(z):
    s = _sigmoid(z)
    return s * (1.0 + z * (1.0 - s))


_GELU_K = math.sqrt(2.0 / math.pi)
_GELU_C = 0.044715


def _gelu(u):
    return 0.5 * u * (1.0 + jnp.tanh(_GELU_K * (u + _GELU_C * u * u * u)))


def _dgelu(u):
    t = jnp.tanh(_GELU_K * (u + _GELU_C * u * u * u))
    return 0.5 * (1.0 + t) + 0.5 * u * (1.0 - t * t) * _GELU_K * (1.0 + 3.0 * _GELU_C * u * u)


class _Place(NamedTuple):
    x: Any
    y: Any
    c: Any
    chip: Any


def _my_place():
    mx, my, mc = lax.axis_index("x"), lax.axis_index("y"), lax.axis_index("c")
    return _Place(mx, my, mc, 2 * mx + my)


def _other_chips(p):
    return [(1 - p.x, p.y), (p.x, 1 - p.y), (1 - p.x, 1 - p.y)]


class _Copy(NamedTuple):
    src: int
    src_at: Callable
    dst: int
    dst_at: Callable
    peer: Callable


class _Plan(NamedTuple):
    arrays: tuple
    copies: tuple


def _view(ref, index):
    return ref if index is None else ref.at[index]


def _plan_io(plan):
    ins = [k for k, a in enumerate(plan.arrays) if not isinstance(a, jax.ShapeDtypeStruct)]
    written = sorted({cp.dst for cp in plan.copies})
    return ins, written


def _descriptors(plan, in_refs, out_refs, send_sems, recv_sems):
    ins, written = _plan_io(plan)
    place = _my_place()
    return [
        pltpu.make_async_remote_copy(
            src_ref=_view(in_refs[ins.index(cp.src)], cp.src_at(place)),
            dst_ref=_view(out_refs[written.index(cp.dst)], cp.dst_at(place)),
            send_sem=send_sems.at[k], recv_sem=recv_sems.at[k],
            device_id=cp.peer(place), device_id_type=MESH_ID)
        for k, cp in enumerate(plan.copies)]


def _plan_operands(plan, n_in, n_out):
    ins, written = _plan_io(plan)
    operands = [plan.arrays[k] for k in ins]
    out_shape = [jax.ShapeDtypeStruct(plan.arrays[k].shape, plan.arrays[k].dtype) for k in written]
    aliases = {n_in + ins.index(k): n_out + pos for pos, k in enumerate(written) if k in ins}
    n = len(plan.copies)
    sems = [pltpu.SemaphoreType.DMA((n,)), pltpu.SemaphoreType.DMA((n,))]
    return operands, out_shape, aliases, sems, written


def _call(body, *, name, grid, in_specs, out_specs, out_shape, operands, scratch_shapes=(), aliases=None,
          parallel=0, plan=None):
    single = not isinstance(out_shape, (list, tuple))
    out_shape = [out_shape] if single else list(out_shape)
    out_specs = [out_specs] if single else list(out_specs)
    if plan is None:
        res = pl.pallas_call(
            body, name=name, grid=grid, in_specs=list(in_specs), out_specs=out_specs, out_shape=out_shape,
            scratch_shapes=list(scratch_shapes), input_output_aliases=aliases or {},
            compiler_params=_params(len(grid), parallel=parallel),
        )(*operands)
        return res[0] if single else res

    n_in, n_out, n_scr = len(operands), len(out_shape), len(scratch_shapes)
    p_operands, p_out_shape, p_aliases, sems, written = _plan_operands(plan, n_in, n_out)
    n_pin, n_pout = len(p_operands), len(p_out_shape)

    def wrapped(*refs):
        ins = refs[:n_in]
        p_in = refs[n_in:n_in + n_pin]
        outs = refs[n_in + n_pin:n_in + n_pin + n_out]
        p_out = refs[n_in + n_pin + n_out:n_in + n_pin + n_out + n_pout]
        scratch = refs[n_in + n_pin + n_out + n_pout:n_in + n_pin + n_out + n_pout + n_scr]
        send_sems, recv_sems = refs[-2:]
        ids = [pl.program_id(a) for a in range(len(grid))]
        first = ids[0] == 0
        last = ids[0] == grid[0] - 1
        for a in range(1, len(grid)):
            first = jnp.logical_and(first, ids[a] == 0)
            last = jnp.logical_and(last, ids[a] == grid[a] - 1)

        @pl.when(first)
        def _():
            for cp in _descriptors(plan, p_in, p_out, send_sems, recv_sems):
                cp.start()

        body(*ins, *outs, *scratch)

        @pl.when(last)
        def _():
            for cp in _descriptors(plan, p_in, p_out, send_sems, recv_sems):
                cp.wait()

    res = pl.pallas_call(
        wrapped, name=name, grid=grid,
        in_specs=list(in_specs) + [HBM_SPEC] * n_pin,
        out_specs=out_specs + [HBM_SPEC] * n_pout,
        out_shape=out_shape + p_out_shape,
        scratch_shapes=list(scratch_shapes) + sems,
        input_output_aliases={**(aliases or {}), **p_aliases},
        compiler_params=_params(len(grid)),
    )(*operands, *p_operands)
    outs = res[0] if single else res[:n_out]
    return outs, dict(zip(written, res[n_out:]))


def _comm_stages(name, arrays, stages):
    plan = _Plan(tuple(arrays), tuple(cp for st in stages for cp in st))
    p_operands, p_out_shape, p_aliases, sems, written = _plan_operands(plan, 0, 0)
    n_pin = len(p_operands)

    def body(*refs):
        p_in = refs[:n_pin]
        p_out = refs[n_pin:n_pin + len(written)]
        send_sems, recv_sems = refs[-2:]
        all_copies = _descriptors(plan, p_in, p_out, send_sems, recv_sems)
        base = 0
        for st in stages:
            for cp in all_copies[base:base + len(st)]:
                cp.start()
            for cp in all_copies[base:base + len(st)]:
                cp.wait()
            base += len(st)

    res = pl.pallas_call(
        body, name=name, in_specs=[HBM_SPEC] * n_pin, out_specs=[HBM_SPEC] * len(written),
        out_shape=p_out_shape, scratch_shapes=sems, input_output_aliases=p_aliases,
    )(*p_operands)
    return dict(zip(written, res))


def _half_rows(k, c):
    return pl.ds(c * (k // 2), k // 2)


def _gather_ici(a, k):
    own = lambda p: (p.chip, _half_rows(k, p.c))
    return [_Copy(a, own, a, own, lambda p, q=q: (*_other_chips(p)[q], p.c)) for q in range(3)]


def _gather_pass_on(a, k):
    def at(q):
        def index(p):
            px, py = _other_chips(p)[q]
            return (2 * px + py, _half_rows(k, p.c))
        return index
    return [_Copy(a, at(q), a, at(q), lambda p: (p.x, p.y, 1 - p.c)) for q in range(3)]


def _reduce_swap(src, dst, k):
    return [_Copy(src, lambda p: (pl.ds(0, N_CHIPS), _half_rows(k, 1 - p.c)), dst, lambda p: None,
                  lambda p: (p.x, p.y, 1 - p.c))]


def _reduce_ici(src, dst, only=(0, 1, 2)):
    def slab(q):
        def index(p):
            px, py = _other_chips(p)[q]
            return 2 * px + py
        return index
    return [_Copy(src, slab(q), dst, lambda p, q=q: q, lambda p, q=q: (*_other_chips(p)[q], p.c)) for q in only]


def _reduce_share(a, layer, k):
    at = lambda p: (layer, _half_rows(k, p.c))
    return [_Copy(a, at, a, at, lambda p: (p.x, p.y, 1 - p.c))]


def _all_to_all(x, name):
    def body(x_ref, y_ref, send_sems, recv_sems, own_sem):
        p = _my_place()
        me = 2 * p.chip + p.c
        own = pltpu.make_async_copy(x_ref.at[me], y_ref.at[me], own_sem)
        own.start()
        copies = []
        for k in range(1, N_DEV):
            px = 1 - p.x if (k >> 2) & 1 else p.x
            py = 1 - p.y if (k >> 1) & 1 else p.y
            pc = 1 - p.c if k & 1 else p.c
            peer = 4 * px + 2 * py + pc
            cp = pltpu.make_async_remote_copy(
                src_ref=x_ref.at[peer], dst_ref=y_ref.at[me],
                send_sem=send_sems.at[k - 1], recv_sem=recv_sems.at[k - 1],
                device_id=(px, py, pc), device_id_type=MESH_ID)
            cp.start()
            copies.append(cp)
        for cp in copies:
            cp.wait()
        own.wait()

    return pl.pallas_call(
        body, name=name,
        out_shape=jax.ShapeDtypeStruct(x.shape, x.dtype),
        in_specs=[HBM_SPEC], out_specs=HBM_SPEC,
        scratch_shapes=[pltpu.SemaphoreType.DMA((N_DEV - 1,)), pltpu.SemaphoreType.DMA((N_DEV - 1,)),
                        pltpu.SemaphoreType.DMA],
    )(x)


def _place_own_shard(w, layer, chip_idx):
    _, k, n = w.shape
    tr = _tile(k, (512, 256, 128))

    def body(c_ref, w_ref, g_ref):
        g_ref[...] = w_ref[...].astype(BF16)

    return pl.pallas_call(
        body, name="place_own_shard",
        grid_spec=pltpu.PrefetchScalarGridSpec(
            num_scalar_prefetch=1, grid=(k // tr,),
            in_specs=[pl.BlockSpec((None, tr, n), lambda r, c: (layer, r, 0))],
            out_specs=pl.BlockSpec((None, tr, n), lambda r, c: (c[0], r, 0))),
        out_shape=jax.ShapeDtypeStruct((N_CHIPS, k, n), BF16),
        compiler_params=_params(1, parallel=1),
    )(chip_idx, w)


def _matmul(name, operands, in_specs, grid, dims, out_shape, out_specs, epilogue, a_prologue=None,
            aliases=None, plan=None):
    n_in = len(operands)

    def body(*refs):
        a = refs[0][...]
        if a_prologue is not None:
            a = a_prologue(a)
        acc = lax.dot_general(a.astype(BF16), refs[1][...].astype(BF16), dims, preferred_element_type=F32)
        epilogue(acc, refs[2:n_in], refs[n_in:])

    return _call(body, name=name, grid=grid, in_specs=in_specs, out_specs=out_specs, out_shape=out_shape,
                 operands=operands, aliases=aliases, parallel=2, plan=plan)


def _store_cast(acc, extra, outs):
    outs[0][...] = acc.astype(outs[0].dtype)


def _in_proj(h, w, plan=None):
    s, d = h.shape
    nl = w.shape[-1]
    tm = _tile(s, (1024, 512, 256))
    tn = _tile(nl, (1024, 768, 512, 384, 256, 128))
    per = nl // tn
    return _matmul(
        "in_proj", (h, w),
        [pl.BlockSpec((tm, d), lambda i, j: (i, 0)),
         pl.BlockSpec((None, d, tn), lambda i, j: (j // per, 0, j % per))],
        (s // tm, N_CHIPS * per), NN,
        jax.ShapeDtypeStruct((s, N_CHIPS * nl), BF16),
        pl.BlockSpec((tm, tn), lambda i, j: (i, j)), _store_cast, plan=plan)


def _out_proj_residual(y, w2, x, gate, plan=None):
    s, wdt = y.shape
    d = w2.shape[-1]
    tm = _tile(s, (1024, 512, 256))
    tn = _tile(d, (1024, 512, 256, 128))

    def epilogue(acc, extra, outs):
        x_ref, gate_ref = extra
        outs[0][...] = x_ref[...] + gate_ref[...] * acc
        outs[1][...] = acc.astype(BF16)

    blk = pl.BlockSpec((tm, tn), lambda i, j: (i, j))
    return _matmul(
        "out_proj", (y, w2, x, gate),
        [pl.BlockSpec((tm, wdt), lambda i, j: (i, 0)),
         pl.BlockSpec((wdt, tn), lambda i, j: (0, j)),
         blk, pl.BlockSpec((1, tn), lambda i, j: (0, j))],
        (s // tm, d // tn), NN,
        [jax.ShapeDtypeStruct((s, d), F32), jax.ShapeDtypeStruct((s, d), BF16)],
        [blk, blk], epilogue, plan=plan)


def _out_proj_bwd_act(dout, w2, plan=None):
    s, d = dout.shape
    wdt = w2.shape[0]
    tm = _tile(s, (1024, 512, 256))
    tn = _tile(wdt, (1024, 512, 256, 128))
    return _matmul(
        "out_proj_dy", (dout, w2),
        [pl.BlockSpec((tm, d), lambda i, j: (i, 0)),
         pl.BlockSpec((tn, d), lambda i, j: (j, 0))],
        (s // tm, wdt // tn), NT,
        jax.ShapeDtypeStruct((s, wdt), BF16),
        pl.BlockSpec((tm, tn), lambda i, j: (i, j)), _store_cast, plan=plan)


def _out_proj_bwd_w(y, dout, plan=None):
    s, wdt = y.shape
    d = dout.shape[1]
    tm = _tile(wdt, (1024, 512, 256, 128))
    tn = _tile(d, (1024, 512, 256, 128))
    return _matmul(
        "out_proj_dw", (y, dout),
        [pl.BlockSpec((s, tm), lambda i, j: (0, i)),
         pl.BlockSpec((s, tn), lambda i, j: (0, j))],
        (wdt // tm, d // tn), TN,
        jax.ShapeDtypeStruct((wdt, d), BF16),
        pl.BlockSpec((tm, tn), lambda i, j: (i, j)), _store_cast, plan=plan)


def _in_proj_bwd_act(dproj, w, plan=None):
    s, n_all = dproj.shape
    d, nl = w.shape[1], w.shape[2]
    tm = _tile(s, (1024, 512, 256))
    tn = _tile(d, (512, 256, 128))

    def body(a_ref, w_ref, o_ref):
        acc = None
        for q in range(N_CHIPS):
            part = lax.dot_general(a_ref[:, q * nl:(q + 1) * nl], w_ref[q], NT, preferred_element_type=F32)
            acc = part if acc is None else acc + part
        o_ref[...] = acc.astype(BF16)

    return _call(
        body, name="in_proj_dh", grid=(s // tm, d // tn),
        in_specs=[pl.BlockSpec((tm, n_all), lambda i, j: (i, 0), pipeline_mode=pl.Buffered(1)),
                  pl.BlockSpec((N_CHIPS, tn, nl), lambda i, j: (0, j, 0))],
        out_specs=pl.BlockSpec((tm, tn), lambda i, j: (i, j)),
        out_shape=jax.ShapeDtypeStruct((s, d), BF16),
        operands=(dproj, w), parallel=2, plan=plan)


def _in_proj_bwd_w(h, dproj, nl, plan=None):
    s, d = h.shape
    tm = _tile(d, (1024, 512, 256, 128))
    tn = _tile(nl, (1024, 768, 512, 384, 256, 128))
    per = nl // tn
    return _matmul(
        "in_proj_dw", (h, dproj),
        [pl.BlockSpec((s, tm), lambda i, j: (0, i)),
         pl.BlockSpec((s, tn), lambda i, j: (0, j))],
        (d // tm, N_CHIPS * per), TN,
        jax.ShapeDtypeStruct((N_CHIPS, d, nl), BF16),
        pl.BlockSpec((None, tm, tn), lambda i, j: (j // per, i, j % per)), _store_cast, plan=plan)


def _mod_fwd(c_all, w_mod, bias, layer):
    nb, d = c_all.shape
    nl = w_mod.shape[-1]
    tn = _tile(nl, (768, 512, 384, 256, 128))

    def epilogue(acc, extra, outs):
        outs[0][...] = acc + extra[0][...]

    return _matmul(
        "mod_fwd", (c_all, w_mod, bias),
        [pl.BlockSpec((nb, d), lambda i, j: (0, 0)),
         pl.BlockSpec((None, d, tn), lambda i, j: (layer, 0, j)),
         pl.BlockSpec((1, tn), lambda i, j: (0, j))],
        (1, nl // tn), NN,
        jax.ShapeDtypeStruct((nb, nl), F32),
        pl.BlockSpec((nb, tn), lambda i, j: (0, j)), epilogue, a_prologue=_silu)


def _mod_bwd_w(c_all, dm_local):
    nb, d = c_all.shape
    nl = dm_local.shape[-1]
    tm = _tile(d, (1024, 512, 256, 128))
    tn = _tile(nl, (768, 512, 384, 256, 128))

    def epilogue(acc, extra, outs):
        outs[0][...] = acc

    return _matmul(
        "mod_dw", (c_all, dm_local),
        [pl.BlockSpec((nb, tm), lambda i, j: (0, i)),
         pl.BlockSpec((nb, tn), lambda i, j: (0, j))],
        (d // tm, nl // tn), TN,
        jax.ShapeDtypeStruct((d, nl), F32),
        pl.BlockSpec((tm, tn), lambda i, j: (i, j)), epilogue, a_prologue=_silu)


def _rows_call(name, body, operands, in_specs, out_shape, out_specs, n_tiles):
    return pl.pallas_call(
        body, name=name, grid=(n_tiles,), in_specs=in_specs, out_specs=out_specs, out_shape=out_shape,
        compiler_params=_params(1),
    )(*operands)


def _row_spec(tr, width):
    return pl.BlockSpec((tr, width), lambda i: (i, 0))


def _vec_spec(width):
    return pl.BlockSpec((1, width), lambda i: (0, 0))


def _accumulate(ref, val):
    first = pl.program_id(0) == 0

    @pl.when(first)
    def _():
        ref[...] = val

    @pl.when(jnp.logical_not(first))
    def _():
        ref[...] += val


def _prenorm(x, g, scale, shift):
    s, d = x.shape
    tr = _tile(s, (256, 128))

    def body(x_ref, g_ref, sc_ref, sh_ref, h_ref):
        xv = x_ref[...]
        rstd = lax.rsqrt(jnp.mean(xv * xv, axis=-1, keepdims=True) + EPS)
        h_ref[...] = ((xv * rstd) * g_ref[...] * (1.0 + sc_ref[...]) + sh_ref[...]).astype(BF16)

    return _rows_call("prenorm", body, (x, g, scale, shift),
                      [_row_spec(tr, d), _vec_spec(d), _vec_spec(d), _vec_spec(d)],
                      jax.ShapeDtypeStruct((s, d), BF16), _row_spec(tr, d), s // tr)


def _prenorm_bwd(x, dh, dres, g, scale):
    s, d = x.shape
    tr = _tile(s, (256, 128))

    def body(x_ref, dh_ref, dres_ref, g_ref, sc_ref, dx_ref, dshift_ref, dscale_ref, dg_ref):
        xv = x_ref[...]
        dhv = dh_ref[...].astype(F32)
        rstd = lax.rsqrt(jnp.mean(xv * xv, axis=-1, keepdims=True) + EPS)
        xhat = xv * rstd
        gv = g_ref[...]
        one_sc = 1.0 + sc_ref[...]
        dxhat = dhv * gv * one_sc
        dx_ref[...] = dres_ref[...] + rstd * (dxhat - xhat * jnp.mean(dxhat * xhat, axis=-1, keepdims=True))
        _accumulate(dshift_ref, jnp.sum(dhv, axis=0, keepdims=True))
        _accumulate(dscale_ref, jnp.sum(dhv * xhat * gv, axis=0, keepdims=True))
        _accumulate(dg_ref, jnp.sum(dhv * xhat * one_sc, axis=0, keepdims=True))

    vec = jax.ShapeDtypeStruct((1, d), F32)
    return _rows_call("prenorm_bwd", body, (x, dh, dres, g, scale),
                      [_row_spec(tr, d), _row_spec(tr, d), _row_spec(tr, d), _vec_spec(d), _vec_spec(d)],
                      [jax.ShapeDtypeStruct((s, d), F32), vec, vec, vec],
                      [_row_spec(tr, d), _vec_spec(d), _vec_spec(d), _vec_spec(d)], s // tr)


def _gate_bwd(dx, out, gate):
    s, d = dx.shape
    tr = _tile(s, (256, 128))

    def body(dx_ref, out_ref, gate_ref, dout_ref, dgate_ref):
        dxv = dx_ref[...]
        dout_ref[...] = (gate_ref[...] * dxv).astype(BF16)
        _accumulate(dgate_ref, jnp.sum(dxv * out_ref[...].astype(F32), axis=0, keepdims=True))

    return _rows_call("gate_bwd", body, (dx, out, gate),
                      [_row_spec(tr, d), _row_spec(tr, d), _vec_spec(d)],
                      [jax.ShapeDtypeStruct((s, d), BF16), jax.ShapeDtypeStruct((1, d), F32)],
                      [_row_spec(tr, d), _vec_spec(d)], s // tr)


def _final_loss(x, target, g):
    s, d = x.shape
    tr = _tile(s, (256, 128))
    n_tiles = s // tr

    def body(x_ref, t_ref, g_ref, loss_ref, dx_ref, dg_ref, acc_ref):
        xv = x_ref[...]
        rstd = lax.rsqrt(jnp.mean(xv * xv, axis=-1, keepdims=True) + EPS)
        xhat = xv * rstd
        gv = g_ref[...]
        err = xhat * gv - t_ref[...]
        dy = err * (1.0 / d)
        dxhat = dy * gv
        dx_ref[...] = rstd * (dxhat - xhat * jnp.mean(dxhat * xhat, axis=-1, keepdims=True))
        _accumulate(dg_ref, jnp.sum(dy * xhat, axis=0, keepdims=True))
        _accumulate(acc_ref, jnp.sum(err * err, axis=0, keepdims=True))

        @pl.when(pl.program_id(0) == n_tiles - 1)
        def _():
            loss_ref[...] = (0.5 / d) * jnp.sum(acc_ref[...], axis=1, keepdims=True)

    return pl.pallas_call(
        body, name="final_loss", grid=(n_tiles,),
        in_specs=[_row_spec(tr, d), _row_spec(tr, d), _vec_spec(d)],
        out_specs=[pl.BlockSpec((1, 1), lambda i: (0, 0)), _row_spec(tr, d), _vec_spec(d)],
        out_shape=[jax.ShapeDtypeStruct((1, 1), F32), jax.ShapeDtypeStruct((s, d), F32),
                   jax.ShapeDtypeStruct((1, d), F32)],
        scratch_shapes=[pltpu.VMEM((1, d), F32)],
        compiler_params=_params(1),
    )(x, target, g)


def _rope(t, cos, sin):
    return t * cos + pltpu.roll(t, HEAD_DIM // 2, axis=1) * sin


def _unrope(dt, cos, sin):
    return dt * cos + pltpu.roll(dt * sin, HEAD_DIM // 2, axis=1)


def _band_blocks(s, dil):
    sub = s // dil
    kw = min(K_WINDOW, sub)

    def rows(r, start, n):
        if dil == 1:
            return pl.ds(pl.multiple_of(start, RADIUS), n)
        return pl.ds(r + dil * start, n, stride=dil)

    def window(idx):
        nb = sub // Q_BLOCK
        r, b = idx // nb, idx % nb
        q0 = b * Q_BLOCK
        start = jnp.clip(q0 - RADIUS, 0, sub - kw)
        qi = q0 + lax.broadcasted_iota(jnp.int32, (Q_BLOCK, kw), 0)
        ki = start + lax.broadcasted_iota(jnp.int32, (Q_BLOCK, kw), 1)
        return rows(r, q0, Q_BLOCK), rows(r, start, kw), jnp.abs(qi - ki) <= RADIUS

    return window


def _head_col(s, group, nh):
    return pl.BlockSpec((s, HEAD_DIM), lambda h: (0, group * nh + h), pipeline_mode=pl.Buffered(1))


def _attn_fwd(proj, cos, sin, aw, plan=None):
    s = proj.shape[0]
    nh = aw // HEAD_DIM
    scale = HEAD_DIM ** -0.5
    n_blocks = s // Q_BLOCK

    def body(q_ref, k_ref, v_ref, cos_ref, sin_ref, attn_ref, lse_ref, qf, kf, vf, acc):
        cosv, sinv = cos_ref[...], sin_ref[...]
        qf[...] = _rope(q_ref[...].astype(F32), cosv, sinv)
        kf[...] = _rope(k_ref[...].astype(F32), cosv, sinv)
        vf[...] = v_ref[...].astype(F32)

        for pattern, dil in enumerate(DILATIONS):
            window = _band_blocks(s, dil)

            def block(idx, carry, window=window, first=(pattern == 0)):
                q_rows, k_rows, valid = window(idx)
                q = qf[q_rows, :].astype(BF16)
                kk = kf[k_rows, :].astype(BF16)
                vv = vf[k_rows, :].astype(BF16)
                sc = lax.dot_general(q, kk, NT, preferred_element_type=F32) * scale
                sc = jnp.where(valid, sc, NEG_INF)
                m = jnp.max(sc, axis=1, keepdims=True)
                p = jnp.exp(sc - m)
                den = jnp.sum(p, axis=1, keepdims=True)
                o = lax.dot_general(p.astype(BF16), vv, NN, preferred_element_type=F32) / den
                lse = jnp.broadcast_to(m + jnp.log(den), (Q_BLOCK, HEAD_DIM))
                if first:
                    acc[q_rows, :] = o
                    lse_ref[q_rows, :] = lse
                else:
                    lse_old = lse_ref[q_rows, :]
                    top = jnp.maximum(lse_old, lse)
                    w_old, w_new = jnp.exp(lse_old - top), jnp.exp(lse - top)
                    tot = w_old + w_new
                    acc[q_rows, :] = (acc[q_rows, :] * w_old + o * w_new) / tot
                    lse_ref[q_rows, :] = top + jnp.log(tot)
                return carry

            lax.fori_loop(0, n_blocks, block, 0)

        attn_ref[...] = acc[...].astype(BF16)

    table = pl.BlockSpec((s, HEAD_DIM), lambda h: (0, 0), pipeline_mode=pl.Buffered(1))
    out = pl.BlockSpec((s, HEAD_DIM), lambda h: (0, h))
    return _call(
        body, name="attn_fwd", grid=(nh,),
        in_specs=[_head_col(s, 0, nh), _head_col(s, 1, nh), _head_col(s, 2, nh), table, table],
        out_specs=[out, out],
        out_shape=[jax.ShapeDtypeStruct((s, aw), BF16), jax.ShapeDtypeStruct((s, aw), F32)],
        scratch_shapes=[pltpu.VMEM((s, HEAD_DIM), F32)] * 4,
        operands=(proj, proj, proj, cos, sin), parallel=1, plan=plan)


def _attn_bwd(proj, cos, sin, dy, attn, lse, aw, plan=None):
    s = proj.shape[0]
    nh = aw // HEAD_DIM
    scale = HEAD_DIM ** -0.5
    n_blocks = s // Q_BLOCK

    def body(q_ref, k_ref, v_ref, za_ref, cos_ref, sin_ref, dy_ref, attn_ref, lse_ref,
             dq_ref, dk_ref, dv_ref, qf, kf, vf, dof, delta, dqa, dka, dva):
        cosv, sinv = cos_ref[...], sin_ref[...]
        qf[...] = _rope(q_ref[...].astype(F32), cosv, sinv)
        kf[...] = _rope(k_ref[...].astype(F32), cosv, sinv)
        vf[...] = v_ref[...].astype(F32)
        do_all = dy_ref[...].astype(F32) * _silu(za_ref[...].astype(F32))
        dof[...] = do_all
        delta[...] = jnp.broadcast_to(
            jnp.sum(do_all * attn_ref[...].astype(F32), axis=1, keepdims=True), (s, HEAD_DIM))
        dqa[...] = jnp.zeros_like(dqa)
        dka[...] = jnp.zeros_like(dka)
        dva[...] = jnp.zeros_like(dva)

        for dil in DILATIONS:
            window = _band_blocks(s, dil)

            def block(idx, carry, window=window):
                q_rows, k_rows, valid = window(idx)
                q = qf[q_rows, :].astype(BF16)
                kk = kf[k_rows, :].astype(BF16)
                vv = vf[k_rows, :].astype(BF16)
                dov = dof[q_rows, :].astype(BF16)
                lse_q = lse_ref[q_rows, :][:, 0:1]
                delta_q = delta[q_rows, :][:, 0:1]
                sc = lax.dot_general(q, kk, NT, preferred_element_type=F32) * scale
                p = jnp.where(valid, jnp.exp(sc - lse_q), 0.0)
                dp = lax.dot_general(dov, vv, NT, preferred_element_type=F32)
                ds = (p * (dp - delta_q) * scale).astype(BF16)
                dqa[q_rows, :] += lax.dot_general(ds, kk, NN, preferred_element_type=F32)
                dka[k_rows, :] += lax.dot_general(ds, q, TN, preferred_element_type=F32)
                dva[k_rows, :] += lax.dot_general(p.astype(BF16), dov, TN, preferred_element_type=F32)
                return carry

            lax.fori_loop(0, n_blocks, block, 0)

        dq_ref[...] = _unrope(dqa[...], cosv, sinv).astype(BF16)
        dk_ref[...] = _unrope(dka[...], cosv, sinv).astype(BF16)
        dv_ref[...] = dva[...].astype(BF16)

    own = pl.BlockSpec((s, HEAD_DIM), lambda h: (0, h), pipeline_mode=pl.Buffered(1))
    table = pl.BlockSpec((s, HEAD_DIM), lambda h: (0, 0), pipeline_mode=pl.Buffered(1))
    out = pl.BlockSpec((s, HEAD_DIM), lambda h: (0, h))
    shape = jax.ShapeDtypeStruct((s, aw), BF16)
    return _call(
        body, name="attn_bwd", grid=(nh,),
        in_specs=[_head_col(s, 0, nh), _head_col(s, 1, nh), _head_col(s, 2, nh), _head_col(s, 3, nh),
                  table, table, own, own, own],
        out_specs=[out, out, out],
        out_shape=[shape, shape, shape],
        scratch_shapes=[pltpu.VMEM((s, HEAD_DIM), F32)] * 8,
        operands=(proj, proj, proj, proj, cos, sin, dy, attn, lse), parallel=1, plan=plan)


def _rope_tables(s):
    half = HEAD_DIM // 2
    inv = ROPE_THETA ** (-jnp.arange(half, dtype=F32) / half)
    ang = jnp.arange(s, dtype=F32)[:, None] * inv[None, :]
    cos, sin = jnp.cos(ang), jnp.sin(ang)
    return jnp.concatenate([cos, cos], axis=-1), jnp.concatenate([-sin, sin], axis=-1)


def _conv_chunks(s):
    for k in range(s // CONV_ROWS):
        lo = max(0, k * CONV_ROWS - CONV_HALO)
        hi = min(s, (k + 1) * CONV_ROWS + CONV_HALO)
        yield k * CONV_ROWS, lo, hi


def _neighbours(p, lo, s):
    n = p.shape[0]
    row = lo + lax.broadcasted_iota(jnp.int32, p.shape, 0)
    prev = jnp.where(row == 0, 0.0, pltpu.roll(p, 1, axis=0))
    nxt = jnp.where(row == s - 1, 0.0, pltpu.roll(p, n - 1, axis=0))
    return prev, nxt


def _ab_mix(attn, proj, conv_w, aw):
    s = proj.shape[0]
    nt = aw // LANES

    def col(group, sel):
        return pl.BlockSpec((s, LANES), lambda i: (0, group * nt + sel(i)))

    a_sel = lambda i: jnp.minimum(i, nt - 1)
    b_sel = lambda i: jnp.maximum(i - nt, 0)

    def body(attn_ref, za_ref, ub_ref, gb_ref, gc_ref, zb_ref, w_ref, y_ref):
        i = pl.program_id(0)

        @pl.when(i < nt)
        def _():
            y_ref[...] = (attn_ref[...].astype(F32) * _silu(za_ref[...].astype(F32))).astype(BF16)

        @pl.when(i >= nt)
        def _():
            w = w_ref[...]
            for c0, lo, hi in _conv_chunks(s):
                p = gc_ref[lo:hi, :].astype(F32) * ub_ref[lo:hi, :].astype(F32)
                prev, nxt = _neighbours(p, lo, s)
                cv = w[0:1, :] * prev + w[1:2, :] * p + w[2:3, :] * nxt
                yb = gb_ref[lo:hi, :].astype(F32) * cv * _silu(zb_ref[lo:hi, :].astype(F32))
                y_ref[c0:c0 + CONV_ROWS, :] = yb[c0 - lo:c0 - lo + CONV_ROWS, :].astype(BF16)

    return pl.pallas_call(
        body, name="ab_mix", grid=(2 * nt,),
        in_specs=[pl.BlockSpec((s, LANES), lambda i: (0, a_sel(i))),
                  col(3, a_sel), col(4, b_sel), col(5, b_sel), col(6, b_sel), col(7, b_sel),
                  pl.BlockSpec((3, LANES), lambda i: (0, b_sel(i)))],
        out_specs=pl.BlockSpec((s, LANES), lambda i: (0, i)),
        out_shape=jax.ShapeDtypeStruct((s, 2 * aw), BF16),
        compiler_params=_params(1),
    )(attn, proj, proj, proj, proj, proj, conv_w)


def _ab_dproj(dqkv, dy, attn, proj, conv_w, aw):
    s = proj.shape[0]
    nt = aw // LANES

    def col(group, sel):
        return pl.BlockSpec((s, LANES), lambda i: (0, group * nt + sel(i)))

    def qkv_spec(part):
        return pl.BlockSpec((s, LANES), lambda i: (0, jnp.clip(i - part * nt, 0, nt - 1)))

    a_sel = lambda i: jnp.clip(i - 3 * nt, 0, nt - 1)
    b_sel = lambda i: jnp.maximum(i - 4 * nt, 0) % nt
    w_sel = lambda i: jnp.clip(i - 4 * nt, 0, nt - 1)

    def body(*refs):
        g_refs = refs[:3]
        dya_ref, attn_ref, za_ref, dyb_ref, ub_ref, gb_ref, gc_ref, zb_ref, w_ref, out_ref, dw_ref = refs[3:]
        i = pl.program_id(0)

        for part in range(3):
            @pl.when(jnp.logical_and(i >= part * nt, i < (part + 1) * nt))
            def _(part=part):
                out_ref[...] = g_refs[part][...]

        @pl.when(jnp.logical_and(i >= 3 * nt, i < 4 * nt))
        def _():
            out_ref[...] = (dya_ref[...].astype(F32) * attn_ref[...].astype(F32)
                            * _dsilu(za_ref[...].astype(F32))).astype(BF16)

        for which in range(4):
            @pl.when(jnp.logical_and(i >= (4 + which) * nt, i < (5 + which) * nt))
            def _(which=which):
                w = w_ref[...]
                dw = [jnp.zeros((1, LANES), F32) for _ in range(3)]
                for c0, lo, hi in _conv_chunks(s):
                    ctr = slice(c0 - lo, c0 - lo + CONV_ROWS)
                    ub = ub_ref[lo:hi, :].astype(F32)
                    gc = gc_ref[lo:hi, :].astype(F32)
                    gb = gb_ref[lo:hi, :].astype(F32)
                    zb = zb_ref[lo:hi, :].astype(F32)
                    dyb = dyb_ref[lo:hi, :].astype(F32)
                    p = gc * ub
                    prev, nxt = _neighbours(p, lo, s)
                    if which == 1:
                        cv = w[0:1, :] * prev + w[1:2, :] * p + w[2:3, :] * nxt
                        res = dyb * cv * _silu(zb)
                    elif which == 3:
                        cv = w[0:1, :] * prev + w[1:2, :] * p + w[2:3, :] * nxt
                        res = dyb * gb * cv * _dsilu(zb)
                    else:
                        dcv = dyb * gb * _silu(zb)
                        dprev, dnxt = _neighbours(dcv, lo, s)
                        dp = w[0:1, :] * dnxt + w[1:2, :] * dcv + w[2:3, :] * dprev
                        res = dp * (gc if which == 0 else ub)
                        if which == 0:
                            for t, nb in enumerate((prev, p, nxt)):
                                dw[t] = dw[t] + jnp.sum((dcv * nb)[ctr, :], axis=0, keepdims=True)
                    out_ref[c0:c0 + CONV_ROWS, :] = res[ctr, :].astype(BF16)
                if which == 0:
                    dw_ref[...] = jnp.concatenate(dw, axis=0)

    return pl.pallas_call(
        body, name="ab_dproj", grid=(8 * nt,),
        in_specs=[qkv_spec(0), qkv_spec(1), qkv_spec(2),
                  pl.BlockSpec((s, LANES), lambda i: (0, a_sel(i))),
                  pl.BlockSpec((s, LANES), lambda i: (0, a_sel(i))),
                  col(3, a_sel),
                  pl.BlockSpec((s, LANES), lambda i: (0, nt + b_sel(i))),
                  col(4, b_sel), col(5, b_sel), col(6, b_sel), col(7, b_sel),
                  pl.BlockSpec((3, LANES), lambda i: (0, b_sel(i)))],
        out_specs=[pl.BlockSpec((s, LANES), lambda i: (0, i)),
                   pl.BlockSpec((3, LANES), lambda i: (0, w_sel(i)))],
        out_shape=[jax.ShapeDtypeStruct((s, 8 * aw), BF16), jax.ShapeDtypeStruct((3, aw), F32)],
        compiler_params=_params(1),
    )(*dqkv, dy, attn, proj, dy, proj, proj, proj, proj, conv_w)


def _sgu_norm(v, ln_g, ln_b):
    gv = _gelu(v)
    mu = jnp.mean(gv, axis=-1, keepdims=True)
    xc = gv - mu
    rstd = lax.rsqrt(jnp.mean(xc * xc, axis=-1, keepdims=True) + EPS)
    vhat = xc * rstd
    return vhat, rstd, vhat * ln_g + ln_b


def _sgu_fwd(uvz, ln_g, ln_b, w_s, b_s, cw):
    s = uvz.shape[0]
    tr = 2 * CHUNK if s % (2 * CHUNK) == 0 else CHUNK
    gw = cw // N_GROUPS

    def body(u_ref, v_ref, z_ref, g_ref, b_ref, ws_ref, bs_ref, y_ref):
        _, _, vn = _sgu_norm(v_ref[...].astype(F32), g_ref[...], b_ref[...])
        vn = vn.astype(BF16)
        for ch in range(tr // CHUNK):
            rows = slice(ch * CHUNK, (ch + 1) * CHUNK)
            for grp in range(N_GROUPS):
                cols = slice(grp * gw, (grp + 1) * gw)
                mixed = lax.dot_general(ws_ref[grp], vn[rows, cols], NN, preferred_element_type=F32) + bs_ref[grp]
                y_ref[rows, cols] = (_gelu(u_ref[rows, cols].astype(F32)) * mixed
                                     * _silu(z_ref[rows, cols].astype(F32))).astype(BF16)

    full3 = lambda shape: pl.BlockSpec(shape, lambda i: (0, 0, 0))
    return pl.pallas_call(
        body, name="sgu_fwd", grid=(s // tr,),
        in_specs=[pl.BlockSpec((tr, cw), lambda i: (i, 0)), pl.BlockSpec((tr, cw), lambda i: (i, 1)),
                  pl.BlockSpec((tr, cw), lambda i: (i, 2)), _vec_spec(cw), _vec_spec(cw),
                  full3(w_s.shape), full3(b_s.shape)],
        out_specs=pl.BlockSpec((tr, cw), lambda i: (i, 0)),
        out_shape=jax.ShapeDtypeStruct((s, cw), BF16),
        compiler_params=_params(1, parallel=1),
    )(uvz, uvz, uvz, ln_g, ln_b, w_s, b_s)


def _sgu_bwd(uvz, dy, ln_g, ln_b, w_s, b_s, cw, plan=None):
    s = uvz.shape[0]
    tr = 2 * CHUNK if s % (2 * CHUNK) == 0 else CHUNK
    gw = cw // N_GROUPS

    def body(u_ref, v_ref, z_ref, dy_ref, g_ref, b_ref, ws_ref, bs_ref,
             duvz_ref, dws_ref, dbs_ref, dg_ref, db_ref, dvn_ref):
        vv = v_ref[...].astype(F32)
        gvec = g_ref[...]
        vhat, rstd, vn = _sgu_norm(vv, gvec, b_ref[...])
        vn = vn.astype(BF16)
        first = pl.program_id(0) == 0

        @pl.when(first)
        def _():
            dws_ref[...] = jnp.zeros_like(dws_ref)
            dbs_ref[...] = jnp.zeros_like(dbs_ref)

        for ch in range(tr // CHUNK):
            rows = slice(ch * CHUNK, (ch + 1) * CHUNK)
            for grp in range(N_GROUPS):
                cols = slice(grp * gw, (grp + 1) * gw)
                vn_g = vn[rows, cols]
                mixed = lax.dot_general(ws_ref[grp], vn_g, NN, preferred_element_type=F32) + bs_ref[grp]
                uu = u_ref[rows, cols].astype(F32)
                zz = z_ref[rows, cols].astype(F32)
                dyv = dy_ref[rows, cols].astype(F32)
                gu, sz = _gelu(uu), _silu(zz)
                duvz_ref[rows, grp * gw:(grp + 1) * gw] = (dyv * mixed * sz * _dgelu(uu)).astype(BF16)
                duvz_ref[rows, 2 * cw + grp * gw:2 * cw + (grp + 1) * gw] = (
                    dyv * gu * mixed * _dsilu(zz)).astype(BF16)
                dmixed = dyv * gu * sz
                dm16 = dmixed.astype(BF16)
                dws_ref[grp] += lax.dot_general(dm16, vn_g, NT, preferred_element_type=F32)
                dbs_ref[grp] += jnp.broadcast_to(jnp.sum(dmixed, axis=1, keepdims=True), (CHUNK, LANES))
                dvn_ref[rows, cols] = lax.dot_general(ws_ref[grp], dm16, TN, preferred_element_type=F32)

        dvn = dvn_ref[...]
        _accumulate(dg_ref, jnp.sum(dvn * vhat, axis=0, keepdims=True))
        _accumulate(db_ref, jnp.sum(dvn, axis=0, keepdims=True))
        dvhat = dvn * gvec
        dgv = rstd * (dvhat - jnp.mean(dvhat, axis=-1, keepdims=True)
                      - vhat * jnp.mean(dvhat * vhat, axis=-1, keepdims=True))
        duvz_ref[:, cw:2 * cw] = (dgv * _dgelu(vv)).astype(BF16)

    full3 = lambda shape: pl.BlockSpec(shape, lambda i: (0, 0, 0))
    acc3 = jax.ShapeDtypeStruct((N_GROUPS, CHUNK, LANES), F32)
    vec = jax.ShapeDtypeStruct((1, cw), F32)
    row = pl.BlockSpec((tr, cw), lambda i: (i, 0))
    return _call(
        body, name="sgu_bwd", grid=(s // tr,),
        in_specs=[row, pl.BlockSpec((tr, cw), lambda i: (i, 1)), pl.BlockSpec((tr, cw), lambda i: (i, 2)),
                  row, _vec_spec(cw), _vec_spec(cw), full3(w_s.shape), full3(b_s.shape)],
        out_specs=[pl.BlockSpec((tr, 3 * cw), lambda i: (i, 0)), full3((N_GROUPS, CHUNK, LANES)),
                   full3((N_GROUPS, CHUNK, LANES)), _vec_spec(cw), _vec_spec(cw)],
        out_shape=[jax.ShapeDtypeStruct((s, 3 * cw), BF16), acc3, acc3, vec, vec],
        scratch_shapes=[pltpu.VMEM((tr, cw), F32)],
        operands=(uvz, uvz, uvz, dy, ln_g, ln_b, w_s, b_s), plan=plan)


def _flat_rows(a):
    return a.reshape(-1, a.shape[-1])


def _add_sibling(grad, recv, core_idx):
    nchip, k, n = grad.shape
    tr = _tile(k // 2, (256, 128))
    nb = (k // 2) // tr

    def body(c_ref, g_ref, r_ref, o_ref):
        o_ref[...] = (g_ref[...].astype(F32) + r_ref[...].astype(F32)).astype(BF16)

    return pl.pallas_call(
        body, name="add_sibling",
        grid_spec=pltpu.PrefetchScalarGridSpec(
            num_scalar_prefetch=1, grid=(nchip, nb),
            in_specs=[pl.BlockSpec((None, tr, n), lambda q, i, c: (q, c[0] * nb + i, 0)),
                      pl.BlockSpec((None, tr, n), lambda q, i, c: (q, i, 0))],
            out_specs=pl.BlockSpec((None, tr, n), lambda q, i, c: (q, i, 0))),
        out_shape=jax.ShapeDtypeStruct((nchip, k // 2, n), BF16),
        compiler_params=_params(2, parallel=2),
    )(core_idx, grad, recv)


def _sum_chips(own, others, reduced, layer, place_idx):
    _, kh, n = own.shape
    tr = _tile(kh, (256, 128))
    nb = kh // tr

    def body(place_ref, own_ref, oth_ref, red_ref, o_ref):
        acc = own_ref[...].astype(F32)
        for q in range(3):
            acc = acc + oth_ref[q].astype(F32)
        o_ref[...] = acc

    return pl.pallas_call(
        body, name="sum_chips",
        grid_spec=pltpu.PrefetchScalarGridSpec(
            num_scalar_prefetch=1, grid=(nb,),
            in_specs=[pl.BlockSpec((None, tr, n), lambda i, p: (p[0], i, 0)),
                      pl.BlockSpec((3, tr, n), lambda i, p: (0, i, 0)),
                      HBM_SPEC],
            out_specs=pl.BlockSpec((None, tr, n), lambda i, p: (layer, p[1] * nb + i, 0))),
        out_shape=jax.ShapeDtypeStruct(reduced.shape, reduced.dtype),
        input_output_aliases={3: 0},
        compiler_params=_params(1, parallel=1),
    )(place_idx, own, others, reduced)


def _sum_devices(parts):
    nd, r, _ = parts.shape
    tr = _tile(r, (512, 256, 128, 64, 32, 16, 8))

    def body(p_ref, o_ref):
        acc = p_ref[0]
        for q in range(1, nd):
            acc = acc + p_ref[q]
        o_ref[...] = acc

    return pl.pallas_call(
        body, name="sum_devices", grid=(r // tr,),
        in_specs=[pl.BlockSpec((nd, tr, LANES), lambda i: (0, i, 0))],
        out_specs=pl.BlockSpec((tr, LANES), lambda i: (i, 0)),
        out_shape=jax.ShapeDtypeStruct((r, LANES), F32),
        compiler_params=_params(1, parallel=1),
    )(parts)


def _adamw(w, g, m, v):
    r, n = w.shape
    tr = _tile(r, [p for p in (1024, 512, 256, 128, 64, 32, 16, 8) if p * n <= ELEMENTWISE_BLOCK])

    def body(w_ref, g_ref, m_ref, v_ref, d_ref, nm_ref, nv_ref):
        gv = g_ref[...]
        nm = ADAM_B1 * m_ref[...] + (1.0 - ADAM_B1) * gv
        nv = ADAM_B2 * v_ref[...] + (1.0 - ADAM_B2) * (gv * gv)
        m_hat = nm / (1.0 - ADAM_B1 ** ADAM_STEP)
        v_hat = nv / (1.0 - ADAM_B2 ** ADAM_STEP)
        d_ref[...] = -ADAM_LR * (m_hat / (jnp.sqrt(v_hat) + ADAM_EPS) + ADAM_WD * w_ref[...])
        nm_ref[...] = nm
        nv_ref[...] = nv

    spec = pl.BlockSpec((tr, n), lambda i: (i, 0))
    shp = jax.ShapeDtypeStruct((r, n), F32)
    return pl.pallas_call(
        body, name="adamw", grid=(r // tr,),
        in_specs=[spec] * 4, out_specs=[spec] * 3, out_shape=[shp] * 3,
        compiler_params=_params(1, parallel=1),
    )(w, g, m, v)


def _pack(arrays, row_multiple=8):
    flat = [a.reshape(-1) for a in arrays]
    sizes = [f.shape[0] for f in flat]
    total = sum(sizes)
    unit = LANES * row_multiple
    padded = -(-total // unit) * unit
    if padded > total:
        flat.append(jnp.zeros((padded - total,), F32))
    offsets = [sum(sizes[:i]) for i in range(len(sizes))]
    return jnp.concatenate(flat).reshape(-1, LANES), offsets


def _unpack(packed, offsets, shapes):
    flat = packed.reshape(-1)
    return [flat[o:o + math.prod(s)].reshape(s) for o, s in zip(offsets, shapes)]


def kernel(x, c, ab_norm_g, ab_w_mod, ab_b_mod, ab_w_in, ab_conv_w, ab_w_out, sg_norm_g, sg_w_mod, sg_b_mod, sg_w_in, sg_ln_g, sg_ln_b, sg_w_s, sg_b_s, sg_w_out, final_norm_g, loss_target, m_ab_norm_g, m_ab_w_mod, m_ab_b_mod, m_ab_w_in, m_ab_conv_w, m_ab_w_out, m_sg_norm_g, m_sg_w_mod, m_sg_b_mod, m_sg_w_in, m_sg_ln_g, m_sg_ln_b, m_sg_w_s, m_sg_b_s, m_sg_w_out, m_final_norm_g, v_ab_norm_g, v_ab_w_mod, v_ab_b_mod, v_ab_w_in, v_ab_conv_w, v_ab_w_out, v_sg_norm_g, v_sg_w_mod, v_sg_b_mod, v_sg_w_in, v_sg_ln_g, v_sg_ln_b, v_sg_w_s, v_sg_b_s, v_sg_w_out, v_final_norm_g):
    s, d = x.shape[1], x.shape[2]
    aw = d // 2
    cw = d
    mod_l = ab_w_mod.shape[-1]
    x0 = x[0]
    target = loss_target[0]
    mx, my, mc = lax.axis_index("x"), lax.axis_index("y"), lax.axis_index("c")
    chip = 2 * mx + my
    chip_idx = jnp.reshape(chip, (1,)).astype(jnp.int32)
    core_idx = jnp.reshape(mc, (1,)).astype(jnp.int32)
    place_idx = jnp.stack([chip, mc]).astype(jnp.int32)

    win = [_place_own_shard(ab_w_in if L % 2 == 0 else sg_w_in, L // 2, chip_idx) for L in range(4)]
    wout = [_place_own_shard(ab_w_out if L % 2 == 0 else sg_w_out, L // 2, chip_idx) for L in range(4)]
    k_in, k_out = d, wout[0].shape[1]

    def gather_plan(ici=(), pass_on=()):
        arrays, copies = [], []
        for stage, make in ((ici, _gather_ici), (pass_on, _gather_pass_on)):
            for kind, L in stage:
                arr = win[L] if kind == "in" else wout[L]
                arrays.append(arr)
                copies += make(len(arrays) - 1, k_in if kind == "in" else k_out)
        return _Plan(tuple(arrays), tuple(copies)), [(kind, L) for kind, L in tuple(ici) + tuple(pass_on)]

    def absorb(plan_and_names, updated):
        _, names = plan_and_names
        for pos, (kind, L) in enumerate(names):
            if kind == "in":
                win[L] = updated[pos]
            else:
                wout[L] = updated[pos]

    first = _comm_stages("gather_first_layer", [win[0], wout[0]],
                         [_gather_ici(0, k_in) + _gather_ici(1, k_out),
                          _gather_pass_on(0, k_in) + _gather_pass_on(1, k_out)])
    win[0], wout[0] = first[0], first[1]

    small_local = [c[0], ab_conv_w, sg_norm_g, sg_ln_g, sg_ln_b]
    small_shapes = [a.shape for a in small_local]
    payload, small_off = _pack(small_local)
    gathered = _all_to_all(jnp.broadcast_to(payload[None], (N_DEV,) + payload.shape), "gather_small")
    per_dev = [_unpack(gathered[b], small_off, small_shapes) for b in range(N_DEV)]
    c_all = jnp.stack([per_dev[b][0] for b in range(N_DEV)])

    def from_chips(idx, axis):
        return jnp.concatenate([per_dev[2 * q][idx] for q in range(N_CHIPS)], axis=axis)

    conv_w_full = from_chips(1, 2)
    sg_norm_g_full = from_chips(2, 1)
    sg_ln_g_full = from_chips(3, 1)
    sg_ln_b_full = from_chips(4, 1)

    ab_b_local = lax.dynamic_slice_in_dim(ab_b_mod, chip * mod_l, mod_l, axis=1)
    mod_rows = []
    for layer in range(4):
        i = layer // 2
        w_mod, bias = (ab_w_mod, ab_b_local) if layer % 2 == 0 else (sg_w_mod, sg_b_mod)
        mod_rows.append(_mod_fwd(c_all, w_mod, bias[i:i + 1], i))
    mod_local = jnp.stack(mod_rows, axis=1)
    mod_recv = _all_to_all(mod_local.reshape(N_DEV, -1, LANES), "exchange_mod")
    mod_recv = mod_recv.reshape(N_DEV, 4, mod_l)
    mod_full = jnp.concatenate([mod_recv[2 * q] for q in range(N_CHIPS)], axis=-1)
    shifts = [mod_full[l:l + 1, :d] for l in range(4)]
    scales = [mod_full[l:l + 1, d:2 * d] for l in range(4)]
    gates = [mod_full[l:l + 1, 2 * d:] for l in range(4)]

    cos, sin = _rope_tables(s)
    w_s16 = sg_w_s.astype(BF16)
    b_s3 = sg_b_s[..., None]

    fwd_comm = {
        ("in_proj", 0): ([("in", 1)], []),
        ("attn", 0): ([("out", 1), ("in", 2)], []),
        ("out_proj", 0): ([], [("in", 1), ("out", 1)]),
        ("in_proj", 1): ([("out", 2)], [("in", 2)]),
        ("out_proj", 1): ([], [("out", 2)]),
        ("in_proj", 2): ([("in", 3)], []),
        ("attn", 2): ([("out", 3)], []),
        ("out_proj", 2): ([], [("in", 3), ("out", 3)]),
    }

    def carried(key, fn, *args):
        if key not in fwd_comm:
            return fn(*args)
        pn = gather_plan(*fwd_comm[key])
        res, updated = fn(*args, plan=pn[0])
        absorb(pn, updated)
        return res

    saved = []
    xs = x0
    for layer in range(4):
        i = layer // 2
        if layer % 2 == 0:
            h = _prenorm(xs, ab_norm_g[i:i + 1], scales[layer], shifts[layer])
            proj = carried(("in_proj", layer), _in_proj, h, win[layer])
            attn, lse = carried(("attn", layer), _attn_fwd, proj, cos, sin, aw)
            y = _ab_mix(attn, proj, conv_w_full[i], aw)
            x_next, out = carried(("out_proj", layer), _out_proj_residual, y, wout[layer].reshape(-1, d), xs,
                                  gates[layer])
            saved.append((xs, h, proj, y, out, attn, lse))
        else:
            h = _prenorm(xs, sg_norm_g_full[i:i + 1], scales[layer], shifts[layer])
            uvz = carried(("in_proj", layer), _in_proj, h, win[layer])
            y = _sgu_fwd(uvz, sg_ln_g_full[i:i + 1], sg_ln_b_full[i:i + 1], w_s16[i], b_s3[i], cw)
            x_next, out = carried(("out_proj", layer), _out_proj_residual, y, wout[layer].reshape(-1, d), xs,
                                  gates[layer])
            saved.append((xs, h, uvz, y, out))
        xs = x_next

    loss11, dx, d_final_g = _final_loss(xs, target, final_norm_g[None])
    loss = lax.psum(loss11[0, 0], ("x", "y", "c"))

    reduced = {"in": [lax.empty((2,) + w.shape[1:], F32) for w in (ab_w_in, sg_w_in)],
               "out": [lax.empty((2,) + w.shape[1:], F32) for w in (ab_w_out, sg_w_out)]}
    grads = {}
    stage = {}
    k_of = {"in": k_in, "out": k_out}

    def swap_plan(L):
        arrays, copies = [], []
        for kind in ("in", "out"):
            g = grads[kind, L]
            arrays += [g, jax.ShapeDtypeStruct((N_CHIPS, g.shape[1] // 2, g.shape[2]), BF16)]
            copies += _reduce_swap(len(arrays) - 2, len(arrays) - 1, k_of[kind])
        return _Plan(tuple(arrays), tuple(copies))

    def after_swap(L, updated):
        for pos, kind in enumerate(("in", "out")):
            stage[kind, L] = _add_sibling(grads[kind, L], updated[2 * pos + 1], core_idx)

    def ici_plan(L, pieces):
        arrays, copies = [], []
        for kind, only in pieces:
            cs = stage[kind, L]
            arrays += [cs, stage.get((kind, L, "recv"), jax.ShapeDtypeStruct((3,) + cs.shape[1:], BF16))]
            copies += _reduce_ici(len(arrays) - 2, len(arrays) - 1, only)
        return _Plan(tuple(arrays), tuple(copies))

    def after_ici(L, pieces, updated):
        for pos, (kind, _) in enumerate(pieces):
            stage[kind, L, "recv"] = updated[2 * pos + 1]

    def sum_layer(L):
        for kind in ("in", "out"):
            reduced[kind][L % 2] = _sum_chips(stage[kind, L], stage[kind, L, "recv"], reduced[kind][L % 2],
                                              L // 2, place_idx)

    def share_plan(L):
        arrays = (reduced["in"][L % 2], reduced["out"][L % 2])
        copies = _reduce_share(0, L // 2, k_in) + _reduce_share(1, L // 2, k_out)
        return _Plan(arrays, tuple(copies))

    def after_share(L, updated):
        reduced["in"][L % 2], reduced["out"][L % 2] = updated[0], updated[1]

    all_chips = (0, 1, 2)
    dmods = [None] * 4
    d_ab_norm_g, d_sg_norm_g = [None, None], [None, None]
    d_conv_w, d_ln_g, d_ln_b, d_w_s, d_b_s = ([None, None] for _ in range(5))
    for layer in reversed(range(4)):
        i = layer // 2
        prev = layer + 1
        busy = prev < 4
        if layer % 2 == 0:
            xs, h, proj, y, out, attn, lse = saved[layer]
        else:
            xs, h, uvz, y, out = saved[layer]
        dout, dgate = _gate_bwd(dx, out, gates[layer])
        w2 = wout[layer].reshape(-1, d)
        if busy:
            dy, updated = _out_proj_bwd_act(dout, w2, plan=swap_plan(prev))
            after_swap(prev, updated)
        else:
            dy = _out_proj_bwd_act(dout, w2)
        grads["out", layer] = _out_proj_bwd_w(y, dout).reshape(N_CHIPS, -1, d)
        if layer % 2 == 0:
            if busy:
                pieces = [("in", all_chips), ("out", all_chips)]
                dqkv, updated = _attn_bwd(proj, cos, sin, dy, attn, lse, aw, plan=ici_plan(prev, pieces))
                after_ici(prev, pieces, updated)
                sum_layer(prev)
            else:
                dqkv = _attn_bwd(proj, cos, sin, dy, attn, lse, aw)
            dact, d_conv_w[i] = _ab_dproj(dqkv, dy, attn, proj, conv_w_full[i], aw)
            if busy:
                dh, updated = _in_proj_bwd_act(dact, win[layer], plan=share_plan(prev))
                after_share(prev, updated)
            else:
                dh = _in_proj_bwd_act(dact, win[layer])
            grads["in", layer] = _in_proj_bwd_w(h, dact, win[layer].shape[-1])
            dx, dshift, dscale, d_ab_norm_g[i] = _prenorm_bwd(xs, dh, dx, ab_norm_g[i:i + 1], scales[layer])
        else:
            sgu_args = (uvz, dy, sg_ln_g_full[i:i + 1], sg_ln_b_full[i:i + 1], w_s16[i], b_s3[i], cw)
            if busy:
                pieces = [("in", (0, 1)), ("out", all_chips)]
                res, updated = _sgu_bwd(*sgu_args, plan=ici_plan(prev, pieces))
                after_ici(prev, pieces, updated)
            else:
                res = _sgu_bwd(*sgu_args)
            dact, d_w_s[i], db_wide, d_ln_g[i], d_ln_b[i] = res
            d_b_s[i] = db_wide[:, :, 0]
            if busy:
                pieces = [("in", (2,))]
                dh, updated = _in_proj_bwd_act(dact, win[layer], plan=ici_plan(prev, pieces))
                after_ici(prev, pieces, updated)
                sum_layer(prev)
                grads["in", layer], updated = _in_proj_bwd_w(h, dact, win[layer].shape[-1], plan=share_plan(prev))
                after_share(prev, updated)
            else:
                dh = _in_proj_bwd_act(dact, win[layer])
                grads["in", layer] = _in_proj_bwd_w(h, dact, win[layer].shape[-1])
            dx, dshift, dscale, d_sg_norm_g[i] = _prenorm_bwd(xs, dh, dx, sg_norm_g_full[i:i + 1], scales[layer])
        dmods[layer] = jnp.concatenate([dshift, dscale, dgate], axis=1)
    grad_x = dx[None]

    plan = swap_plan(0)
    after_swap(0, _comm_stages("grads_to_sibling", plan.arrays, [plan.copies]))
    pieces = [("in", all_chips), ("out", all_chips)]
    plan = ici_plan(0, pieces)
    after_ici(0, pieces, _comm_stages("chip_sums_to_owners", plan.arrays, [plan.copies]))
    sum_layer(0)
    plan = share_plan(0)
    after_share(0, _comm_stages("reduced_grads_to_sibling", plan.arrays, [plan.copies]))
    g_ab_w_in, g_sg_w_in = reduced["in"]
    g_ab_w_out, g_sg_w_out = reduced["out"]

    partial_list = [jnp.concatenate(dmods, axis=0),
                    jnp.concatenate(d_ab_norm_g, axis=0), jnp.concatenate(d_sg_norm_g, axis=0), d_final_g[0],
                    jnp.stack(d_conv_w), jnp.concatenate(d_ln_g, axis=0), jnp.concatenate(d_ln_b, axis=0),
                    jnp.stack(d_w_s), jnp.stack(d_b_s)]
    partial_shapes = [a.shape for a in partial_list]
    partials, part_off = _pack(partial_list)
    all_partials = _all_to_all(jnp.broadcast_to(partials[None], (N_DEV,) + partials.shape), "gather_partials")
    summed = _unpack(_sum_devices(all_partials), part_off, partial_shapes)
    (g_mod_bias, g_ab_norm_g, g_sg_norm_g_full, g_final_g, g_conv_full, g_ln_g_full, g_ln_b_full,
     g_w_s, g_b_s) = summed
    dm_all = jnp.stack([_unpack(all_partials[b], part_off[:1], partial_shapes[:1])[0] for b in range(N_DEV)])
    dm_local = lax.dynamic_slice_in_dim(dm_all, chip * mod_l, mod_l, axis=2)

    def chip_cols(a, axis):
        width = a.shape[axis] // N_CHIPS
        return lax.dynamic_slice_in_dim(a, chip * width, width, axis=axis)

    g_ab_b_mod = jnp.stack([g_mod_bias[0], g_mod_bias[2]])
    g_sg_b_mod = chip_cols(jnp.stack([g_mod_bias[1], g_mod_bias[3]]), 1)
    g_ab_w_mod = jnp.stack([_mod_bwd_w(c_all, dm_local[:, 0]), _mod_bwd_w(c_all, dm_local[:, 2])])
    g_sg_w_mod = jnp.stack([_mod_bwd_w(c_all, dm_local[:, 1]), _mod_bwd_w(c_all, dm_local[:, 3])])
    g_conv = chip_cols(g_conv_full, 2)
    g_sg_norm_g = chip_cols(g_sg_norm_g_full, 1)
    g_ln_g = chip_cols(g_ln_g_full, 1)
    g_ln_b = chip_cols(g_ln_b_full, 1)

    def step_big(w, g, m, v):
        dl, nm, nv = _adamw(_flat_rows(w), _flat_rows(g), _flat_rows(m), _flat_rows(v))
        return dl.reshape(w.shape), nm.reshape(w.shape), nv.reshape(w.shape)

    big_out = {
        "ab_w_mod": step_big(ab_w_mod, g_ab_w_mod, m_ab_w_mod, v_ab_w_mod),
        "ab_w_in": step_big(ab_w_in, g_ab_w_in, m_ab_w_in, v_ab_w_in),
        "ab_w_out": step_big(ab_w_out, g_ab_w_out, m_ab_w_out, v_ab_w_out),
        "sg_w_mod": step_big(sg_w_mod, g_sg_w_mod, m_sg_w_mod, v_sg_w_mod),
        "sg_w_in": step_big(sg_w_in, g_sg_w_in, m_sg_w_in, v_sg_w_in),
        "sg_w_out": step_big(sg_w_out, g_sg_w_out, m_sg_w_out, v_sg_w_out),
    }
    small_names = ["ab_norm_g", "ab_b_mod", "ab_conv_w", "sg_norm_g", "sg_b_mod", "sg_ln_g", "sg_ln_b",
                   "sg_w_s", "sg_b_s", "final_norm_g"]
    small_w = [ab_norm_g, ab_b_mod, ab_conv_w, sg_norm_g, sg_b_mod, sg_ln_g, sg_ln_b, sg_w_s, sg_b_s, final_norm_g]
    small_g = [g_ab_norm_g, g_ab_b_mod, g_conv, g_sg_norm_g, g_sg_b_mod, g_ln_g, g_ln_b, g_w_s, g_b_s, g_final_g]
    small_m = [m_ab_norm_g, m_ab_b_mod, m_ab_conv_w, m_sg_norm_g, m_sg_b_mod, m_sg_ln_g, m_sg_ln_b, m_sg_w_s,
               m_sg_b_s, m_final_norm_g]
    small_v = [v_ab_norm_g, v_ab_b_mod, v_ab_conv_w, v_sg_norm_g, v_sg_b_mod, v_sg_ln_g, v_sg_ln_b, v_sg_w_s,
               v_sg_b_s, v_final_norm_g]
    shapes = [a.shape for a in small_w]
    pw, off = _pack(small_w)
    pg, _ = _pack(small_g)
    pm, _ = _pack(small_m)
    pv, _ = _pack(small_v)
    pd, pnm, pnv = _adamw(pw, pg, pm, pv)
    small_out = {}
    for name, dl, nm, nv in zip(small_names, _unpack(pd, off, shapes), _unpack(pnm, off, shapes),
                                _unpack(pnv, off, shapes)):
        small_out[name] = (dl, nm, nv)

    grad_of = {
        "ab_norm_g": g_ab_norm_g, "ab_w_mod": g_ab_w_mod, "ab_b_mod": g_ab_b_mod, "ab_w_in": g_ab_w_in,
        "ab_conv_w": g_conv, "ab_w_out": g_ab_w_out, "sg_norm_g": g_sg_norm_g, "sg_w_mod": g_sg_w_mod,
        "sg_b_mod": g_sg_b_mod, "sg_w_in": g_sg_w_in, "sg_ln_g": g_ln_g, "sg_ln_b": g_ln_b, "sg_w_s": g_w_s,
        "sg_b_s": g_b_s, "sg_w_out": g_sg_w_out, "final_norm_g": g_final_g,
    }
    order = ["ab_norm_g", "ab_w_mod", "ab_b_mod", "ab_w_in", "ab_conv_w", "ab_w_out", "sg_norm_g", "sg_w_mod",
             "sg_b_mod", "sg_w_in", "sg_ln_g", "sg_ln_b", "sg_w_s", "sg_b_s", "sg_w_out", "final_norm_g"]
    steps = {**big_out, **small_out}
    return (loss, grad_x, *[grad_of[n] for n in order], *[steps[n][0] for n in order],
            *[steps[n][1] for n in order], *[steps[n][2] for n in order])
```

```python
import math
from typing import Any, Callable, NamedTuple

import jax
import jax.numpy as jnp
from jax import lax
from jax.experimental import pallas as pl
from jax.experimental.pallas import tpu as pltpu

F32 = jnp.float32
BF16 = jnp.bfloat16

HEAD_DIM = 128
RADIUS = 64
DILATIONS = (1, 4, 16)
Q_BLOCK = 256
K_WINDOW = Q_BLOCK + 2 * RADIUS
ROPE_THETA = 10000.0
NEG_INF = -1e30
N_GROUPS = 8
CHUNK = 128
EPS = 1e-6
CONV_ROWS = 512
CONV_HALO = 16
LANES = 128
ELEMENTWISE_BLOCK = 256 * 1024
N_DEV = 8
N_CHIPS = 4

ADAM_LR = 0.001
ADAM_B1 = 0.9
ADAM_B2 = 0.999
ADAM_EPS = 1e-08
ADAM_WD = 0.01
ADAM_STEP = 10

VMEM_LIMIT_V7X = 56 * 1024 * 1024

MESH_ID = pl.DeviceIdType.MESH
HBM_SPEC = pl.BlockSpec(memory_space=pltpu.HBM)

NN = (((1,), (0,)), ((), ()))
NT = (((1,), (1,)), ((), ()))
TN = (((0,), (0,)), ((), ()))


def _params(n_grid, parallel=0):
    sem = tuple(["parallel"] * parallel + ["arbitrary"] * (n_grid - parallel))
    return pltpu.CompilerParams(dimension_semantics=sem, vmem_limit_bytes=VMEM_LIMIT_V7X)


def _tile(n, prefs):
    for p in prefs:
        if n % p == 0:
            return p
    return n


def _sigmoid(z):
    return 1.0 / (1.0 + jnp.exp(-z))


def _silu(z):
    return z * _sigmoid(z)


def _dsilu(z):
    s = _sigmoid(z)
    return s * (1.0 + z * (1.0 - s))


_GELU_K = math.sqrt(2.0 / math.pi)
_GELU_C = 0.044715


def _gelu(u):
    return 0.5 * u * (1.0 + jnp.tanh(_GELU_K * (u + _GELU_C * u * u * u)))


def _dgelu(u):
    t = jnp.tanh(_GELU_K * (u + _GELU_C * u * u * u))
    return 0.5 * (1.0 + t) + 0.5 * u * (1.0 - t * t) * _GELU_K * (1.0 + 3.0 * _GELU_C * u * u)


class _Place(NamedTuple):
    x: Any
    y: Any
    c: Any
    chip: Any


def _my_place():
    mx, my, mc = lax.axis_index("x"), lax.axis_index("y"), lax.axis_index("c")
    return _Place(mx, my, mc, 2 * mx + my)


def _other_chips(p):
    return [(1 - p.x, p.y), (p.x, 1 - p.y), (1 - p.x, 1 - p.y)]


class _Copy(NamedTuple):
    src: int
    src_at: Callable
    dst: int
    dst_at: Callable
    peer: Callable


class _Plan(NamedTuple):
    arrays: tuple
    copies: tuple


def _view(ref, index):
    return ref if index is None else ref.at[index]


def _plan_io(plan):
    ins = [k for k, a in enumerate(plan.arrays) if not isinstance(a, jax.ShapeDtypeStruct)]
    written = sorted({cp.dst for cp in plan.copies})
    return ins, written


def _descriptors(plan, in_refs, out_refs, send_sems, recv_sems):
    ins, written = _plan_io(plan)
    place = _my_place()
    return [
        pltpu.make_async_remote_copy(
            src_ref=_view(in_refs[ins.index(cp.src)], cp.src_at(place)),
            dst_ref=_view(out_refs[written.index(cp.dst)], cp.dst_at(place)),
            send_sem=send_sems.at[k], recv_sem=recv_sems.at[k],
            device_id=cp.peer(place), device_id_type=MESH_ID)
        for k, cp in enumerate(plan.copies)]


def _plan_operands(plan, n_in, n_out):
    ins, written = _plan_io(plan)
    operands = [plan.arrays[k] for k in ins]
    out_shape = [jax.ShapeDtypeStruct(plan.arrays[k].shape, plan.arrays[k].dtype) for k in written]
    aliases = {n_in + ins.index(k): n_out + pos for pos, k in enumerate(written) if k in ins}
    n = len(plan.copies)
    sems = [pltpu.SemaphoreType.DMA((n,)), pltpu.SemaphoreType.DMA((n,))]
    return operands, out_shape, aliases, sems, written


def _call(body, *, name, grid, in_specs, out_specs, out_shape, operands, scratch_shapes=(), aliases=None,
          parallel=0, plan=None):
    single = not isinstance(out_shape, (list, tuple))
    out_shape = [out_shape] if single else list(out_shape)
    out_specs = [out_specs] if single else list(out_specs)
    if plan is None:
        res = pl.pallas_call(
            body, name=name, grid=grid, in_specs=list(in_specs), out_specs=out_specs, out_shape=out_shape,
            scratch_shapes=list(scratch_shapes), input_output_aliases=aliases or {},
            compiler_params=_params(len(grid), parallel=parallel),
        )(*operands)
        return res[0] if single else res

    n_in, n_out, n_scr = len(operands), len(out_shape), len(scratch_shapes)
    p_operands, p_out_shape, p_aliases, sems, written = _plan_operands(plan, n_in, n_out)
    n_pin, n_pout = len(p_operands), len(p_out_shape)

    def wrapped(*refs):
        ins = refs[:n_in]
        p_in = refs[n_in:n_in + n_pin]
        outs = refs[n_in + n_pin:n_in + n_pin + n_out]
        p_out = refs[n_in + n_pin + n_out:n_in + n_pin + n_out + n_pout]
        scratch = refs[n_in + n_pin + n_out + n_pout:n_in + n_pin + n_out + n_pout + n_scr]
        send_sems, recv_sems = refs[-2:]
        ids = [pl.program_id(a) for a in range(len(grid))]
        first = ids[0] == 0
        last = ids[0] == grid[0] - 1
        for a in range(1, len(grid)):
            first = jnp.logical_and(first, ids[a] == 0)
            last = jnp.logical_and(last, ids[a] == grid[a] - 1)

        @pl.when(first)
        def _():
            for cp in _descriptors(plan, p_in, p_out, send_sems, recv_sems):
                cp.start()

        body(*ins, *outs, *scratch)

        @pl.when(last)
        def _():
            for cp in _descriptors(plan, p_in, p_out, send_sems, recv_sems):
                cp.wait()

    res = pl.pallas_call(
        wrapped, name=name, grid=grid,
        in_specs=list(in_specs) + [HBM_SPEC] * n_pin,
        out_specs=out_specs + [HBM_SPEC] * n_pout,
        out_shape=out_shape + p_out_shape,
        scratch_shapes=list(scratch_shapes) + sems,
        input_output_aliases={**(aliases or {}), **p_aliases},
        compiler_params=_params(len(grid)),
    )(*operands, *p_operands)
    outs = res[0] if single else res[:n_out]
    return outs, dict(zip(written, res[n_out:]))


def _comm_stages(name, arrays, stages):
    plan = _Plan(tuple(arrays), tuple(cp for st in stages for cp in st))
    p_operands, p_out_shape, p_aliases, sems, written = _plan_operands(plan, 0, 0)
    n_pin = len(p_operands)

    def body(*refs):
        p_in = refs[:n_pin]
        p_out = refs[n_pin:n_pin + len(written)]
        send_sems, recv_sems = refs[-2:]
        all_copies = _descriptors(plan, p_in, p_out, send_sems, recv_sems)
        base = 0
        for st in stages:
            for cp in all_copies[base:base + len(st)]:
                cp.start()
            for cp in all_copies[base:base + len(st)]:
                cp.wait()
            base += len(st)

    res = pl.pallas_call(
        body, name=name, in_specs=[HBM_SPEC] * n_pin, out_specs=[HBM_SPEC] * len(written),
        out_shape=p_out_shape, scratch_shapes=sems, input_output_aliases=p_aliases,
    )(*p_operands)
    return dict(zip(written, res))


def _half_rows(k, c):
    return pl.ds(c * (k // 2), k // 2)


def _gather_ici(a, k):
    own = lambda p: (p.chip, _half_rows(k, p.c))
    return [_Copy(a, own, a, own, lambda p, q=q: (*_other_chips(p)[q], p.c)) for q in range(3)]


def _gather_pass_on(a, k):
    def at(q):
        def index(p):
            px, py = _other_chips(p)[q]
            return (2 * px + py, _half_rows(k, p.c))
        return index
    return [_Copy(a, at(q), a, at(q), lambda p: (p.x, p.y, 1 - p.c)) for q in range(3)]


def _reduce_swap(src, dst, k):
    return [_Copy(src, lambda p: (pl.ds(0, N_CHIPS), _half_rows(k, 1 - p.c)), dst, lambda p: None,
                  lambda p: (p.x, p.y, 1 - p.c))]


def _reduce_ici(src, dst, only=(0, 1, 2)):
    def slab(q):
        def index(p):
            px, py = _other_chips(p)[q]
            return 2 * px + py
        return index
    return [_Copy(src, slab(q), dst, lambda p, q=q: q, lambda p, q=q: (*_other_chips(p)[q], p.c)) for q in only]


def _reduce_share(a, layer, k):
    at = lambda p: (layer, _half_rows(k, p.c))
    return [_Copy(a, at, a, at, lambda p: (p.x, p.y, 1 - p.c))]


def _all_to_all(x, name):
    def body(x_ref, y_ref, send_sems, recv_sems, own_sem):
        p = _my_place()
        me = 2 * p.chip + p.c
        own = pltpu.make_async_copy(x_ref.at[me], y_ref.at[me], own_sem)
        own.start()
        copies = []
        for k in range(1, N_DEV):
            px = 1 - p.x if (k >> 2) & 1 else p.x
            py = 1 - p.y if (k >> 1) & 1 else p.y
            pc = 1 - p.c if k & 1 else p.c
            peer = 4 * px + 2 * py + pc
            cp = pltpu.make_async_remote_copy(
                src_ref=x_ref.at[peer], dst_ref=y_ref.at[me],
                send_sem=send_sems.at[k - 1], recv_sem=recv_sems.at[k - 1],
                device_id=(px, py, pc), device_id_type=MESH_ID)
            cp.start()
            copies.append(cp)
        for cp in copies:
            cp.wait()
        own.wait()

    return pl.pallas_call(
        body, name=name,
        out_shape=jax.ShapeDtypeStruct(x.shape, x.dtype),
        in_specs=[HBM_SPEC], out_specs=HBM_SPEC,
        scratch_shapes=[pltpu.SemaphoreType.DMA((N_DEV - 1,)), pltpu.SemaphoreType.DMA((N_DEV - 1,)),
                        pltpu.SemaphoreType.DMA],
    )(x)


def _place_own_shard(w, layer, chip_idx):
    _, k, n = w.shape
    tr = _tile(k, (512, 256, 128))

    def body(c_ref, w_ref, g_ref):
        g_ref[...] = w_ref[...].astype(BF16)

    return pl.pallas_call(
        body, name="place_own_shard",
        grid_spec=pltpu.PrefetchScalarGridSpec(
            num_scalar_prefetch=1, grid=(k // tr,),
            in_specs=[pl.BlockSpec((None, tr, n), lambda r, c: (layer, r, 0))],
            out_specs=pl.BlockSpec((None, tr, n), lambda r, c: (c[0], r, 0))),
        out_shape=jax.ShapeDtypeStruct((N_CHIPS, k, n), BF16),
        compiler_params=_params(1, parallel=1),
    )(chip_idx, w)


def _matmul(name, operands, in_specs, grid, dims, out_shape, out_specs, epilogue, a_prologue=None,
            aliases=None, plan=None):
    n_in = len(operands)

    def body(*refs):
        a = refs[0][...]
        if a_prologue is not None:
            a = a_prologue(a)
        acc = lax.dot_general(a.astype(BF16), refs[1][...].astype(BF16), dims, preferred_element_type=F32)
        epilogue(acc, refs[2:n_in], refs[n_in:])

    return _call(body, name=name, grid=grid, in_specs=in_specs, out_specs=out_specs, out_shape=out_shape,
                 operands=operands, aliases=aliases, parallel=2, plan=plan)


def _store_cast(acc, extra, outs):
    outs[0][...] = acc.astype(outs[0].dtype)


def _in_proj(h, w, plan=None):
    s, d = h.shape
    nl = w.shape[-1]
    tm = _tile(s, (1024, 512, 256))
    tn = _tile(nl, (1024, 768, 512, 384, 256, 128))
    per = nl // tn
    return _matmul(
        "in_proj", (h, w),
        [pl.BlockSpec((tm, d), lambda i, j: (i, 0)),
         pl.BlockSpec((None, d, tn), lambda i, j: (j // per, 0, j % per))],
        (s // tm, N_CHIPS * per), NN,
        jax.ShapeDtypeStruct((s, N_CHIPS * nl), BF16),
        pl.BlockSpec((tm, tn), lambda i, j: (i, j)), _store_cast, plan=plan)


def _out_proj_residual(y, w2, x, gate, plan=None):
    s, wdt = y.shape
    d = w2.shape[-1]
    tm = _tile(s, (1024, 512, 256))
    tn = _tile(d, (1024, 512, 256, 128))

    def epilogue(acc, extra, outs):
        x_ref, gate_ref = extra
        outs[0][...] = x_ref[...] + gate_ref[...] * acc
        outs[1][...] = acc.astype(BF16)

    blk = pl.BlockSpec((tm, tn), lambda i, j: (i, j))
    return _matmul(
        "out_proj", (y, w2, x, gate),
        [pl.BlockSpec((tm, wdt), lambda i, j: (i, 0)),
         pl.BlockSpec((wdt, tn), lambda i, j: (0, j)),
         blk, pl.BlockSpec((1, tn), lambda i, j: (0, j))],
        (s // tm, d // tn), NN,
        [jax.ShapeDtypeStruct((s, d), F32), jax.ShapeDtypeStruct((s, d), BF16)],
        [blk, blk], epilogue, plan=plan)


def _out_proj_bwd_act(dout, w2, plan=None):
    s, d = dout.shape
    wdt = w2.shape[0]
    tm = _tile(s, (1024, 512, 256))
    tn = _tile(wdt, (1024, 512, 256, 128))
    return _matmul(
        "out_proj_dy", (dout, w2),
        [pl.BlockSpec((tm, d), lambda i, j: (i, 0)),
         pl.BlockSpec((tn, d), lambda i, j: (j, 0))],
        (s // tm, wdt // tn), NT,
        jax.ShapeDtypeStruct((s, wdt), BF16),
        pl.BlockSpec((tm, tn), lambda i, j: (i, j)), _store_cast, plan=plan)


def _out_proj_bwd_w(y, dout, plan=None):
    s, wdt = y.shape
    d = dout.shape[1]
    tm = _tile(wdt, (1024, 512, 256, 128))
    tn = _tile(d, (1024, 512, 256, 128))
    return _matmul(
        "out_proj_dw", (y, dout),
        [pl.BlockSpec((s, tm), lambda i, j: (0, i)),
         pl.BlockSpec((s, tn), lambda i, j: (0, j))],
        (wdt // tm, d // tn), TN,
        jax.ShapeDtypeStruct((wdt, d), BF16),
        pl.BlockSpec((tm, tn), lambda i, j: (i, j)), _store_cast, plan=plan)


def _in_proj_bwd_act(dproj, w, plan=None):
    s, n_all = dproj.shape
    d, nl = w.shape[1], w.shape[2]
    tm = _tile(s, (1024, 512, 256))
    tn = _tile(d, (512, 256, 128))

    def body(a_ref, w_ref, o_ref):
        acc = None
        for q in range(N_CHIPS):
            part = lax.dot_general(a_ref[:, q * nl:(q + 1) * nl], w_ref[q], NT, preferred_element_type=F32)
            acc = part if acc is None else acc + part
        o_ref[...] = acc.astype(BF16)

    return _call(
        body, name="in_proj_dh", grid=(s // tm, d // tn),
        in_specs=[pl.BlockSpec((tm, n_all), lambda i, j: (i, 0), pipeline_mode=pl.Buffered(1)),
                  pl.BlockSpec((N_CHIPS, tn, nl), lambda i, j: (0, j, 0))],
        out_specs=pl.BlockSpec((tm, tn), lambda i, j: (i, j)),
        out_shape=jax.ShapeDtypeStruct((s, d), BF16),
        operands=(dproj, w), parallel=2, plan=plan)


def _in_proj_bwd_w(h, dproj, nl, plan=None):
    s, d = h.shape
    tm = _tile(d, (1024, 512, 256, 128))
    tn = _tile(nl, (1024, 768, 512, 384, 256, 128))
    per = nl // tn
    return _matmul(
        "in_proj_dw", (h, dproj),
        [pl.BlockSpec((s, tm), lambda i, j: (0, i)),
         pl.BlockSpec((s, tn), lambda i, j: (0, j))],
        (d // tm, N_CHIPS * per), TN,
        jax.ShapeDtypeStruct((N_CHIPS, d, nl), BF16),
        pl.BlockSpec((None, tm, tn), lambda i, j: (j // per, i, j % per)), _store_cast, plan=plan)


def _mod_fwd(c_all, w_mod, bias, layer):
    nb, d = c_all.shape
    nl = w_mod.shape[-1]
    tn = _tile(nl, (768, 512, 384, 256, 128))

    def epilogue(acc, extra, outs):
        outs[0][...] = acc + extra[0][...]

    return _matmul(
        "mod_fwd", (c_all, w_mod, bias),
        [pl.BlockSpec((nb, d), lambda i, j: (0, 0)),
         pl.BlockSpec((None, d, tn), lambda i, j: (layer, 0, j)),
         pl.BlockSpec((1, tn), lambda i, j: (0, j))],
        (1, nl // tn), NN,
        jax.ShapeDtypeStruct((nb, nl), F32),
        pl.BlockSpec((nb, tn), lambda i, j: (0, j)), epilogue, a_prologue=_silu)


def _mod_bwd_w(c_all, dm_local):
    nb, d = c_all.shape
    nl = dm_local.shape[-1]
    tm = _tile(d, (1024, 512, 256, 128))
    tn = _tile(nl, (768, 512, 384, 256, 128))

    def epilogue(acc, extra, outs):
        outs[0][...] = acc

    return _matmul(
        "mod_dw", (c_all, dm_local),
        [pl.BlockSpec((nb, tm), lambda i, j: (0, i)),
         pl.BlockSpec((nb, tn), lambda i, j: (0, j))],
        (d // tm, nl // tn), TN,
        jax.ShapeDtypeStruct((d, nl), F32),
        pl.BlockSpec((tm, tn), lambda i, j: (i, j)), epilogue, a_prologue=_silu)


def _rows_call(name, body, operands, in_specs, out_shape, out_specs, n_tiles):
    return pl.pallas_call(
        body, name=name, grid=(n_tiles,), in_specs=in_specs, out_specs=out_specs, out_shape=out_shape,
        compiler_params=_params(1),
    )(*operands)


def _row_spec(tr, width):
    return pl.BlockSpec((tr, width), lambda i: (i, 0))


def _vec_spec(width):
    return pl.BlockSpec((1, width), lambda i: (0, 0))


def _accumulate(ref, val):
    first = pl.program_id(0) == 0

    @pl.when(first)
    def _():
        ref[...] = val

    @pl.when(jnp.logical_not(first))
    def _():
        ref[...] += val


def _prenorm(x, g, scale, shift):
    s, d = x.shape
    tr = _tile(s, (256, 128))

    def body(x_ref, g_ref, sc_ref, sh_ref, h_ref):
        xv = x_ref[...]
        rstd = lax.rsqrt(jnp.mean(xv * xv, axis=-1, keepdims=True) + EPS)
        h_ref[...] = ((xv * rstd) * g_ref[...] * (1.0 + sc_ref[...]) + sh_ref[...]).astype(BF16)

    return _rows_call("prenorm", body, (x, g, scale, shift),
                      [_row_spec(tr, d), _vec_spec(d), _vec_spec(d), _vec_spec(d)],
                      jax.ShapeDtypeStruct((s, d), BF16), _row_spec(tr, d), s // tr)


def _gate_grads(dxv, out_ref, gate_ref, dout_ref, dgate_ref):
    dout_ref[...] = (gate_ref[...] * dxv).astype(BF16)
    _accumulate(dgate_ref, jnp.sum(dxv * out_ref[...].astype(F32), axis=0, keepdims=True))


def _prenorm_bwd(x, dh, dres, g, scale, below=None):
    s, d = x.shape
    tr = _tile(s, (256, 128))

    def body(x_ref, dh_ref, dres_ref, g_ref, sc_ref, *rest):
        dx_ref, dshift_ref, dscale_ref, dg_ref = rest[-6:-2] if below else rest
        xv = x_ref[...]
        dhv = dh_ref[...].astype(F32)
        rstd = lax.rsqrt(jnp.mean(xv * xv, axis=-1, keepdims=True) + EPS)
        xhat = xv * rstd
        gv = g_ref[...]
        one_sc = 1.0 + sc_ref[...]
        dxhat = dhv * gv * one_sc
        dxv = dres_ref[...] + rstd * (dxhat - xhat * jnp.mean(dxhat * xhat, axis=-1, keepdims=True))
        dx_ref[...] = dxv
        _accumulate(dshift_ref, jnp.sum(dhv, axis=0, keepdims=True))
        _accumulate(dscale_ref, jnp.sum(dhv * xhat * gv, axis=0, keepdims=True))
        _accumulate(dg_ref, jnp.sum(dhv * xhat * one_sc, axis=0, keepdims=True))
        if below:
            _gate_grads(dxv, rest[0], rest[1], rest[-2], rest[-1])

    vec = jax.ShapeDtypeStruct((1, d), F32)
    operands = (x, dh, dres, g, scale) + (tuple(below) if below else ())
    in_specs = [_row_spec(tr, d), _row_spec(tr, d), _row_spec(tr, d), _vec_spec(d), _vec_spec(d)]
    out_shape = [jax.ShapeDtypeStruct((s, d), F32), vec, vec, vec]
    out_specs = [_row_spec(tr, d), _vec_spec(d), _vec_spec(d), _vec_spec(d)]
    if below:
        in_specs += [_row_spec(tr, d), _vec_spec(d)]
        out_shape += [jax.ShapeDtypeStruct((s, d), BF16), vec]
        out_specs += [_row_spec(tr, d), _vec_spec(d)]
    return _rows_call("prenorm_bwd", body, operands, in_specs, out_shape, out_specs, s // tr)


def _final_loss(x, target, g, out_below, gate_below):
    s, d = x.shape
    tr = _tile(s, (256, 128))
    n_tiles = s // tr

    def body(x_ref, t_ref, g_ref, out_ref, gate_ref, loss_ref, dx_ref, dg_ref, dout_ref, dgate_ref, acc_ref):
        xv = x_ref[...]
        rstd = lax.rsqrt(jnp.mean(xv * xv, axis=-1, keepdims=True) + EPS)
        xhat = xv * rstd
        gv = g_ref[...]
        err = xhat * gv - t_ref[...]
        dy = err * (1.0 / d)
        dxhat = dy * gv
        dxv = rstd * (dxhat - xhat * jnp.mean(dxhat * xhat, axis=-1, keepdims=True))
        dx_ref[...] = dxv
        _accumulate(dg_ref, jnp.sum(dy * xhat, axis=0, keepdims=True))
        _accumulate(acc_ref, jnp.sum(err * err, axis=0, keepdims=True))
        _gate_grads(dxv, out_ref, gate_ref, dout_ref, dgate_ref)

        @pl.when(pl.program_id(0) == n_tiles - 1)
        def _():
            loss_ref[...] = (0.5 / d) * jnp.sum(acc_ref[...], axis=1, keepdims=True)

    vec = jax.ShapeDtypeStruct((1, d), F32)
    return pl.pallas_call(
        body, name="final_loss", grid=(n_tiles,),
        in_specs=[_row_spec(tr, d), _row_spec(tr, d), _vec_spec(d), _row_spec(tr, d), _vec_spec(d)],
        out_specs=[pl.BlockSpec((1, 1), lambda i: (0, 0)), _row_spec(tr, d), _vec_spec(d), _row_spec(tr, d),
                   _vec_spec(d)],
        out_shape=[jax.ShapeDtypeStruct((1, 1), F32), jax.ShapeDtypeStruct((s, d), F32), vec,
                   jax.ShapeDtypeStruct((s, d), BF16), vec],
        scratch_shapes=[pltpu.VMEM((1, d), F32)],
        compiler_params=_params(1),
    )(x, target, g, out_below, gate_below)


def _rope(t, cos, sin):
    return t * cos + pltpu.roll(t, HEAD_DIM // 2, axis=1) * sin


def _unrope(dt, cos, sin):
    return dt * cos + pltpu.roll(dt * sin, HEAD_DIM // 2, axis=1)


def _band_blocks(s, dil):
    sub = s // dil
    kw = min(K_WINDOW, sub)

    def rows(r, start, n):
        if dil == 1:
            return pl.ds(pl.multiple_of(start, RADIUS), n)
        return pl.ds(r + dil * start, n, stride=dil)

    def window(idx):
        nb = sub // Q_BLOCK
        r, b = idx // nb, idx % nb
        q0 = b * Q_BLOCK
        start = jnp.clip(q0 - RADIUS, 0, sub - kw)
        qi = q0 + lax.broadcasted_iota(jnp.int32, (Q_BLOCK, kw), 0)
        ki = start + lax.broadcasted_iota(jnp.int32, (Q_BLOCK, kw), 1)
        return rows(r, q0, Q_BLOCK), rows(r, start, kw), jnp.abs(qi - ki) <= RADIUS

    return window


def _head_col(s, group, nh):
    return pl.BlockSpec((s, HEAD_DIM), lambda h: (0, group * nh + h), pipeline_mode=pl.Buffered(1))


def _attn_fwd(proj, cos, sin, aw, plan=None):
    s = proj.shape[0]
    nh = aw // HEAD_DIM
    scale = HEAD_DIM ** -0.5
    n_blocks = s // Q_BLOCK

    def body(q_ref, k_ref, v_ref, cos_ref, sin_ref, attn_ref, lse_ref, qf, kf, vf, acc):
        cosv, sinv = cos_ref[...], sin_ref[...]
        qf[...] = _rope(q_ref[...].astype(F32), cosv, sinv)
        kf[...] = _rope(k_ref[...].astype(F32), cosv, sinv)
        vf[...] = v_ref[...].astype(F32)

        for pattern, dil in enumerate(DILATIONS):
            window = _band_blocks(s, dil)

            def block(idx, carry, window=window, first=(pattern == 0)):
                q_rows, k_rows, valid = window(idx)
                q = qf[q_rows, :].astype(BF16)
                kk = kf[k_rows, :].astype(BF16)
                vv = vf[k_rows, :].astype(BF16)
                sc = lax.dot_general(q, kk, NT, preferred_element_type=F32) * scale
                sc = jnp.where(valid, sc, NEG_INF)
                m = jnp.max(sc, axis=1, keepdims=True)
                p = jnp.exp(sc - m)
                den = jnp.sum(p, axis=1, keepdims=True)
                o = lax.dot_general(p.astype(BF16), vv, NN, preferred_element_type=F32) / den
                lse = jnp.broadcast_to(m + jnp.log(den), (Q_BLOCK, HEAD_DIM))
                if first:
                    acc[q_rows, :] = o
                    lse_ref[q_rows, :] = lse
                else:
                    lse_old = lse_ref[q_rows, :]
                    top = jnp.maximum(lse_old, lse)
                    w_old, w_new = jnp.exp(lse_old - top), jnp.exp(lse - top)
                    tot = w_old + w_new
                    acc[q_rows, :] = (acc[q_rows, :] * w_old + o * w_new) / tot
                    lse_ref[q_rows, :] = top + jnp.log(tot)
                return carry

            lax.fori_loop(0, n_blocks, block, 0)

        attn_ref[...] = acc[...].astype(BF16)

    table = pl.BlockSpec((s, HEAD_DIM), lambda h: (0, 0), pipeline_mode=pl.Buffered(1))
    out = pl.BlockSpec((s, HEAD_DIM), lambda h: (0, h))
    return _call(
        body, name="attn_fwd", grid=(nh,),
        in_specs=[_head_col(s, 0, nh), _head_col(s, 1, nh), _head_col(s, 2, nh), table, table],
        out_specs=[out, out],
        out_shape=[jax.ShapeDtypeStruct((s, aw), BF16), jax.ShapeDtypeStruct((s, aw), F32)],
        scratch_shapes=[pltpu.VMEM((s, HEAD_DIM), F32)] * 4,
        operands=(proj, proj, proj, cos, sin), parallel=1, plan=plan)


def _attn_bwd(proj, cos, sin, dy, attn, lse, aw, plan=None):
    s = proj.shape[0]
    nh = aw // HEAD_DIM
    scale = HEAD_DIM ** -0.5
    n_blocks = s // Q_BLOCK

    def body(q_ref, k_ref, v_ref, za_ref, cos_ref, sin_ref, dy_ref, attn_ref, lse_ref,
             dq_ref, dk_ref, dv_ref, qf, kf, vf, dof, delta, dqa, dka, dva):
        cosv, sinv = cos_ref[...], sin_ref[...]
        qf[...] = _rope(q_ref[...].astype(F32), cosv, sinv)
        kf[...] = _rope(k_ref[...].astype(F32), cosv, sinv)
        vf[...] = v_ref[...].astype(F32)
        do_all = dy_ref[...].astype(F32) * _silu(za_ref[...].astype(F32))
        dof[...] = do_all
        delta[...] = jnp.broadcast_to(
            jnp.sum(do_all * attn_ref[...].astype(F32), axis=1, keepdims=True), (s, HEAD_DIM))
        dqa[...] = jnp.zeros_like(dqa)
        dka[...] = jnp.zeros_like(dka)
        dva[...] = jnp.zeros_like(dva)

        for dil in DILATIONS:
            window = _band_blocks(s, dil)

            def block(idx, carry, window=window):
                q_rows, k_rows, valid = window(idx)
                q = qf[q_rows, :].astype(BF16)
                kk = kf[k_rows, :].astype(BF16)
                vv = vf[k_rows, :].astype(BF16)
                dov = dof[q_rows, :].astype(BF16)
                lse_q = lse_ref[q_rows, :][:, 0:1]
                delta_q = delta[q_rows, :][:, 0:1]
                sc = lax.dot_general(q, kk, NT, preferred_element_type=F32) * scale
                p = jnp.where(valid, jnp.exp(sc - lse_q), 0.0)
                dp = lax.dot_general(dov, vv, NT, preferred_element_type=F32)
                ds = (p * (dp - delta_q) * scale).astype(BF16)
                dqa[q_rows, :] += lax.dot_general(ds, kk, NN, preferred_element_type=F32)
                dka[k_rows, :] += lax.dot_general(ds, q, TN, preferred_element_type=F32)
                dva[k_rows, :] += lax.dot_general(p.astype(BF16), dov, TN, preferred_element_type=F32)
                return carry

            lax.fori_loop(0, n_blocks, block, 0)

        dq_ref[...] = _unrope(dqa[...], cosv, sinv).astype(BF16)
        dk_ref[...] = _unrope(dka[...], cosv, sinv).astype(BF16)
        dv_ref[...] = dva[...].astype(BF16)

    own = pl.BlockSpec((s, HEAD_DIM), lambda h: (0, h), pipeline_mode=pl.Buffered(1))
    table = pl.BlockSpec((s, HEAD_DIM), lambda h: (0, 0), pipeline_mode=pl.Buffered(1))
    out = pl.BlockSpec((s, HEAD_DIM), lambda h: (0, h))
    shape = jax.ShapeDtypeStruct((s, aw), BF16)
    return _call(
        body, name="attn_bwd", grid=(nh,),
        in_specs=[_head_col(s, 0, nh), _head_col(s, 1, nh), _head_col(s, 2, nh), _head_col(s, 3, nh),
                  table, table, own, own, own],
        out_specs=[out, out, out],
        out_shape=[shape, shape, shape],
        scratch_shapes=[pltpu.VMEM((s, HEAD_DIM), F32)] * 8,
        operands=(proj, proj, proj, proj, cos, sin, dy, attn, lse), parallel=1, plan=plan)


def _rope_tables(s):
    half = HEAD_DIM // 2
    inv = ROPE_THETA ** (-jnp.arange(half, dtype=F32) / half)
    ang = jnp.arange(s, dtype=F32)[:, None] * inv[None, :]
    cos, sin = jnp.cos(ang), jnp.sin(ang)
    return jnp.concatenate([cos, cos], axis=-1), jnp.concatenate([-sin, sin], axis=-1)


def _conv_chunks(s):
    for k in range(s // CONV_ROWS):
        lo = max(0, k * CONV_ROWS - CONV_HALO)
        hi = min(s, (k + 1) * CONV_ROWS + CONV_HALO)
        yield k * CONV_ROWS, lo, hi


def _neighbours(p, lo, s):
    n = p.shape[0]
    row = lo + lax.broadcasted_iota(jnp.int32, p.shape, 0)
    prev = jnp.where(row == 0, 0.0, pltpu.roll(p, 1, axis=0))
    nxt = jnp.where(row == s - 1, 0.0, pltpu.roll(p, n - 1, axis=0))
    return prev, nxt


def _ab_mix(attn, proj, conv_w, aw):
    s = proj.shape[0]
    nt = aw // LANES

    def col(group, sel):
        return pl.BlockSpec((s, LANES), lambda i: (0, group * nt + sel(i)))

    a_sel = lambda i: jnp.minimum(i, nt - 1)
    b_sel = lambda i: jnp.maximum(i - nt, 0)

    def body(attn_ref, za_ref, ub_ref, gb_ref, gc_ref, zb_ref, w_ref, y_ref):
        i = pl.program_id(0)

        @pl.when(i < nt)
        def _():
            y_ref[...] = (attn_ref[...].astype(F32) * _silu(za_ref[...].astype(F32))).astype(BF16)

        @pl.when(i >= nt)
        def _():
            w = w_ref[...]
            for c0, lo, hi in _conv_chunks(s):
                p = gc_ref[lo:hi, :].astype(F32) * ub_ref[lo:hi, :].astype(F32)
                prev, nxt = _neighbours(p, lo, s)
                cv = w[0:1, :] * prev + w[1:2, :] * p + w[2:3, :] * nxt
                yb = gb_ref[lo:hi, :].astype(F32) * cv * _silu(zb_ref[lo:hi, :].astype(F32))
                y_ref[c0:c0 + CONV_ROWS, :] = yb[c0 - lo:c0 - lo + CONV_ROWS, :].astype(BF16)

    return pl.pallas_call(
        body, name="ab_mix", grid=(2 * nt,),
        in_specs=[pl.BlockSpec((s, LANES), lambda i: (0, a_sel(i))),
                  col(3, a_sel), col(4, b_sel), col(5, b_sel), col(6, b_sel), col(7, b_sel),
                  pl.BlockSpec((3, LANES), lambda i: (0, b_sel(i)))],
        out_specs=pl.BlockSpec((s, LANES), lambda i: (0, i)),
        out_shape=jax.ShapeDtypeStruct((s, 2 * aw), BF16),
        compiler_params=_params(1),
    )(attn, proj, proj, proj, proj, proj, conv_w)


def _ab_dproj(dqkv, dy, attn, proj, conv_w, aw):
    s = proj.shape[0]
    nt = aw // LANES

    def col(group, sel):
        return pl.BlockSpec((s, LANES), lambda i: (0, group * nt + sel(i)))

    def qkv_spec(part):
        return pl.BlockSpec((s, LANES), lambda i: (0, jnp.clip(i - part * nt, 0, nt - 1)))

    a_sel = lambda i: jnp.clip(i - 3 * nt, 0, nt - 1)
    b_sel = lambda i: jnp.maximum(i - 4 * nt, 0) % nt
    w_sel = lambda i: jnp.clip(i - 4 * nt, 0, nt - 1)

    def body(*refs):
        g_refs = refs[:3]
        dya_ref, attn_ref, za_ref, dyb_ref, ub_ref, gb_ref, gc_ref, zb_ref, w_ref, out_ref, dw_ref = refs[3:]
        i = pl.program_id(0)

        for part in range(3):
            @pl.when(jnp.logical_and(i >= part * nt, i < (part + 1) * nt))
            def _(part=part):
                out_ref[...] = g_refs[part][...]

        @pl.when(jnp.logical_and(i >= 3 * nt, i < 4 * nt))
        def _():
            out_ref[...] = (dya_ref[...].astype(F32) * attn_ref[...].astype(F32)
                            * _dsilu(za_ref[...].astype(F32))).astype(BF16)

        for which in range(4):
            @pl.when(jnp.logical_and(i >= (4 + which) * nt, i < (5 + which) * nt))
            def _(which=which):
                w = w_ref[...]
                dw = [jnp.zeros((1, LANES), F32) for _ in range(3)]
                for c0, lo, hi in _conv_chunks(s):
                    ctr = slice(c0 - lo, c0 - lo + CONV_ROWS)
                    ub = ub_ref[lo:hi, :].astype(F32)
                    gc = gc_ref[lo:hi, :].astype(F32)
                    gb = gb_ref[lo:hi, :].astype(F32)
                    zb = zb_ref[lo:hi, :].astype(F32)
                    dyb = dyb_ref[lo:hi, :].astype(F32)
                    p = gc * ub
                    prev, nxt = _neighbours(p, lo, s)
                    if which == 1:
                        cv = w[0:1, :] * prev + w[1:2, :] * p + w[2:3, :] * nxt
                        res = dyb * cv * _silu(zb)
                    elif which == 3:
                        cv = w[0:1, :] * prev + w[1:2, :] * p + w[2:3, :] * nxt
                        res = dyb * gb * cv * _dsilu(zb)
                    else:
                        dcv = dyb * gb * _silu(zb)
                        dprev, dnxt = _neighbours(dcv, lo, s)
                        dp = w[0:1, :] * dnxt + w[1:2, :] * dcv + w[2:3, :] * dprev
                        res = dp * (gc if which == 0 else ub)
                        if which == 0:
                            for t, nb in enumerate((prev, p, nxt)):
                                dw[t] = dw[t] + jnp.sum((dcv * nb)[ctr, :], axis=0, keepdims=True)
                    out_ref[c0:c0 + CONV_ROWS, :] = res[ctr, :].astype(BF16)
                if which == 0:
                    dw_ref[...] = jnp.concatenate(dw, axis=0)

    return pl.pallas_call(
        body, name="ab_dproj", grid=(8 * nt,),
        in_specs=[qkv_spec(0), qkv_spec(1), qkv_spec(2),
                  pl.BlockSpec((s, LANES), lambda i: (0, a_sel(i))),
                  pl.BlockSpec((s, LANES), lambda i: (0, a_sel(i))),
                  col(3, a_sel),
                  pl.BlockSpec((s, LANES), lambda i: (0, nt + b_sel(i))),
                  col(4, b_sel), col(5, b_sel), col(6, b_sel), col(7, b_sel),
                  pl.BlockSpec((3, LANES), lambda i: (0, b_sel(i)))],
        out_specs=[pl.BlockSpec((s, LANES), lambda i: (0, i)),
                   pl.BlockSpec((3, LANES), lambda i: (0, w_sel(i)))],
        out_shape=[jax.ShapeDtypeStruct((s, 8 * aw), BF16), jax.ShapeDtypeStruct((3, aw), F32)],
        compiler_params=_params(1),
    )(*dqkv, dy, attn, proj, dy, proj, proj, proj, proj, conv_w)


def _sgu_norm(v, ln_g, ln_b):
    gv = _gelu(v)
    mu = jnp.mean(gv, axis=-1, keepdims=True)
    xc = gv - mu
    rstd = lax.rsqrt(jnp.mean(xc * xc, axis=-1, keepdims=True) + EPS)
    vhat = xc * rstd
    return vhat, rstd, vhat * ln_g + ln_b


def _sgu_fwd(uvz, ln_g, ln_b, w_s, b_s, cw):
    s = uvz.shape[0]
    tr = 2 * CHUNK if s % (2 * CHUNK) == 0 else CHUNK
    gw = cw // N_GROUPS

    def body(u_ref, v_ref, z_ref, g_ref, b_ref, ws_ref, bs_ref, y_ref):
        _, _, vn = _sgu_norm(v_ref[...].astype(F32), g_ref[...], b_ref[...])
        vn = vn.astype(BF16)
        for ch in range(tr // CHUNK):
            rows = slice(ch * CHUNK, (ch + 1) * CHUNK)
            for grp in range(N_GROUPS):
                cols = slice(grp * gw, (grp + 1) * gw)
                mixed = lax.dot_general(ws_ref[grp], vn[rows, cols], NN, preferred_element_type=F32) + bs_ref[grp]
                y_ref[rows, cols] = (_gelu(u_ref[rows, cols].astype(F32)) * mixed
                                     * _silu(z_ref[rows, cols].astype(F32))).astype(BF16)

    full3 = lambda shape: pl.BlockSpec(shape, lambda i: (0, 0, 0))
    return pl.pallas_call(
        body, name="sgu_fwd", grid=(s // tr,),
        in_specs=[pl.BlockSpec((tr, cw), lambda i: (i, 0)), pl.BlockSpec((tr, cw), lambda i: (i, 1)),
                  pl.BlockSpec((tr, cw), lambda i: (i, 2)), _vec_spec(cw), _vec_spec(cw),
                  full3(w_s.shape), full3(b_s.shape)],
        out_specs=pl.BlockSpec((tr, cw), lambda i: (i, 0)),
        out_shape=jax.ShapeDtypeStruct((s, cw), BF16),
        compiler_params=_params(1, parallel=1),
    )(uvz, uvz, uvz, ln_g, ln_b, w_s, b_s)


def _sgu_bwd(uvz, dy, ln_g, ln_b, w_s, b_s, cw, plan=None):
    s = uvz.shape[0]
    tr = 2 * CHUNK if s % (2 * CHUNK) == 0 else CHUNK
    gw = cw // N_GROUPS

    def body(u_ref, v_ref, z_ref, dy_ref, g_ref, b_ref, ws_ref, bs_ref,
             duvz_ref, dws_ref, dbs_ref, dg_ref, db_ref, dvn_ref):
        vv = v_ref[...].astype(F32)
        gvec = g_ref[...]
        vhat, rstd, vn = _sgu_norm(vv, gvec, b_ref[...])
        vn = vn.astype(BF16)
        first = pl.program_id(0) == 0

        @pl.when(first)
        def _():
            dws_ref[...] = jnp.zeros_like(dws_ref)
            dbs_ref[...] = jnp.zeros_like(dbs_ref)

        for ch in range(tr // CHUNK):
            rows = slice(ch * CHUNK, (ch + 1) * CHUNK)
            for grp in range(N_GROUPS):
                cols = slice(grp * gw, (grp + 1) * gw)
                vn_g = vn[rows, cols]
                mixed = lax.dot_general(ws_ref[grp], vn_g, NN, preferred_element_type=F32) + bs_ref[grp]
                uu = u_ref[rows, cols].astype(F32)
                zz = z_ref[rows, cols].astype(F32)
                dyv = dy_ref[rows, cols].astype(F32)
                gu, sz = _gelu(uu), _silu(zz)
                duvz_ref[rows, grp * gw:(grp + 1) * gw] = (dyv * mixed * sz * _dgelu(uu)).astype(BF16)
                duvz_ref[rows, 2 * cw + grp * gw:2 * cw + (grp + 1) * gw] = (
                    dyv * gu * mixed * _dsilu(zz)).astype(BF16)
                dmixed = dyv * gu * sz
                dm16 = dmixed.astype(BF16)
                dws_ref[grp] += lax.dot_general(dm16, vn_g, NT, preferred_element_type=F32)
                dbs_ref[grp] += jnp.broadcast_to(jnp.sum(dmixed, axis=1, keepdims=True), (CHUNK, LANES))
                dvn_ref[rows, cols] = lax.dot_general(ws_ref[grp], dm16, TN, preferred_element_type=F32)

        dvn = dvn_ref[...]
        _accumulate(dg_ref, jnp.sum(dvn * vhat, axis=0, keepdims=True))
        _accumulate(db_ref, jnp.sum(dvn, axis=0, keepdims=True))
        dvhat = dvn * gvec
        dgv = rstd * (dvhat - jnp.mean(dvhat, axis=-1, keepdims=True)
                      - vhat * jnp.mean(dvhat * vhat, axis=-1, keepdims=True))
        duvz_ref[:, cw:2 * cw] = (dgv * _dgelu(vv)).astype(BF16)

    full3 = lambda shape: pl.BlockSpec(shape, lambda i: (0, 0, 0))
    acc3 = jax.ShapeDtypeStruct((N_GROUPS, CHUNK, LANES), F32)
    vec = jax.ShapeDtypeStruct((1, cw), F32)
    row = pl.BlockSpec((tr, cw), lambda i: (i, 0))
    return _call(
        body, name="sgu_bwd", grid=(s // tr,),
        in_specs=[row, pl.BlockSpec((tr, cw), lambda i: (i, 1)), pl.BlockSpec((tr, cw), lambda i: (i, 2)),
                  row, _vec_spec(cw), _vec_spec(cw), full3(w_s.shape), full3(b_s.shape)],
        out_specs=[pl.BlockSpec((tr, 3 * cw), lambda i: (i, 0)), full3((N_GROUPS, CHUNK, LANES)),
                   full3((N_GROUPS, CHUNK, LANES)), _vec_spec(cw), _vec_spec(cw)],
        out_shape=[jax.ShapeDtypeStruct((s, 3 * cw), BF16), acc3, acc3, vec, vec],
        scratch_shapes=[pltpu.VMEM((tr, cw), F32)],
        operands=(uvz, uvz, uvz, dy, ln_g, ln_b, w_s, b_s), plan=plan)


def _flat_rows(a):
    return a.reshape(-1, a.shape[-1])


def _add_sibling(grad, recv, core_idx):
    nchip, k, n = grad.shape
    tr = _tile(k // 2, (256, 128))
    nb = (k // 2) // tr

    def body(c_ref, g_ref, r_ref, o_ref):
        o_ref[...] = (g_ref[...].astype(F32) + r_ref[...].astype(F32)).astype(BF16)

    return pl.pallas_call(
        body, name="add_sibling",
        grid_spec=pltpu.PrefetchScalarGridSpec(
            num_scalar_prefetch=1, grid=(nchip, nb),
            in_specs=[pl.BlockSpec((None, tr, n), lambda q, i, c: (q, c[0] * nb + i, 0)),
                      pl.BlockSpec((None, tr, n), lambda q, i, c: (q, i, 0))],
            out_specs=pl.BlockSpec((None, tr, n), lambda q, i, c: (q, i, 0))),
        out_shape=jax.ShapeDtypeStruct((nchip, k // 2, n), BF16),
        compiler_params=_params(2, parallel=2),
    )(core_idx, grad, recv)


def _sum_chips(own, others, reduced, layer, place_idx):
    _, kh, n = own.shape
    tr = _tile(kh, (256, 128))
    nb = kh // tr

    def body(place_ref, own_ref, oth_ref, red_ref, o_ref):
        acc = own_ref[...].astype(F32)
        for q in range(3):
            acc = acc + oth_ref[q].astype(F32)
        o_ref[...] = acc

    return pl.pallas_call(
        body, name="sum_chips",
        grid_spec=pltpu.PrefetchScalarGridSpec(
            num_scalar_prefetch=1, grid=(nb,),
            in_specs=[pl.BlockSpec((None, tr, n), lambda i, p: (p[0], i, 0)),
                      pl.BlockSpec((3, tr, n), lambda i, p: (0, i, 0)),
                      HBM_SPEC],
            out_specs=pl.BlockSpec((None, tr, n), lambda i, p: (layer, p[1] * nb + i, 0))),
        out_shape=jax.ShapeDtypeStruct(reduced.shape, reduced.dtype),
        input_output_aliases={3: 0},
        compiler_params=_params(1, parallel=1),
    )(place_idx, own, others, reduced)


def _sum_devices(parts, plan=None):
    nd, r, _ = parts.shape
    tr = _tile(r, (512, 256, 128, 64, 32, 16, 8))

    def body(p_ref, o_ref):
        acc = p_ref[0]
        for q in range(1, nd):
            acc = acc + p_ref[q]
        o_ref[...] = acc

    return _call(
        body, name="sum_devices", grid=(r // tr,),
        in_specs=[pl.BlockSpec((nd, tr, LANES), lambda i: (0, i, 0))],
        out_specs=pl.BlockSpec((tr, LANES), lambda i: (i, 0)),
        out_shape=jax.ShapeDtypeStruct((r, LANES), F32),
        operands=(parts,), parallel=1, plan=plan)


def _adamw(w, g, m, v, plan=None):
    r, n = w.shape
    tr = _tile(r, [p for p in (1024, 512, 256, 128, 64, 32, 16, 8) if p * n <= ELEMENTWISE_BLOCK])

    def body(w_ref, g_ref, m_ref, v_ref, d_ref, nm_ref, nv_ref):
        gv = g_ref[...]
        nm = ADAM_B1 * m_ref[...] + (1.0 - ADAM_B1) * gv
        nv = ADAM_B2 * v_ref[...] + (1.0 - ADAM_B2) * (gv * gv)
        m_hat = nm / (1.0 - ADAM_B1 ** ADAM_STEP)
        v_hat = nv / (1.0 - ADAM_B2 ** ADAM_STEP)
        d_ref[...] = -ADAM_LR * (m_hat / (jnp.sqrt(v_hat) + ADAM_EPS) + ADAM_WD * w_ref[...])
        nm_ref[...] = nm
        nv_ref[...] = nv

    spec = pl.BlockSpec((tr, n), lambda i: (i, 0))
    shp = jax.ShapeDtypeStruct((r, n), F32)
    return _call(
        body, name="adamw", grid=(r // tr,),
        in_specs=[spec] * 4, out_specs=[spec] * 3, out_shape=[shp] * 3,
        operands=(w, g, m, v), parallel=1, plan=plan)


def _pack(arrays, row_multiple=8):
    flat = [a.reshape(-1) for a in arrays]
    sizes = [f.shape[0] for f in flat]
    total = sum(sizes)
    unit = LANES * row_multiple
    padded = -(-total // unit) * unit
    if padded > total:
        flat.append(jnp.zeros((padded - total,), F32))
    offsets = [sum(sizes[:i]) for i in range(len(sizes))]
    return jnp.concatenate(flat).reshape(-1, LANES), offsets


def _unpack(packed, offsets, shapes):
    flat = packed.reshape(-1)
    return [flat[o:o + math.prod(s)].reshape(s) for o, s in zip(offsets, shapes)]


def kernel(x, c, ab_norm_g, ab_w_mod, ab_b_mod, ab_w_in, ab_conv_w, ab_w_out, sg_norm_g, sg_w_mod, sg_b_mod, sg_w_in, sg_ln_g, sg_ln_b, sg_w_s, sg_b_s, sg_w_out, final_norm_g, loss_target, m_ab_norm_g, m_ab_w_mod, m_ab_b_mod, m_ab_w_in, m_ab_conv_w, m_ab_w_out, m_sg_norm_g, m_sg_w_mod, m_sg_b_mod, m_sg_w_in, m_sg_ln_g, m_sg_ln_b, m_sg_w_s, m_sg_b_s, m_sg_w_out, m_final_norm_g, v_ab_norm_g, v_ab_w_mod, v_ab_b_mod, v_ab_w_in, v_ab_conv_w, v_ab_w_out, v_sg_norm_g, v_sg_w_mod, v_sg_b_mod, v_sg_w_in, v_sg_ln_g, v_sg_ln_b, v_sg_w_s, v_sg_b_s, v_sg_w_out, v_final_norm_g):
    s, d = x.shape[1], x.shape[2]
    aw = d // 2
    cw = d
    mod_l = ab_w_mod.shape[-1]
    x0 = x[0]
    target = loss_target[0]
    mx, my, mc = lax.axis_index("x"), lax.axis_index("y"), lax.axis_index("c")
    chip = 2 * mx + my
    chip_idx = jnp.reshape(chip, (1,)).astype(jnp.int32)
    core_idx = jnp.reshape(mc, (1,)).astype(jnp.int32)
    place_idx = jnp.stack([chip, mc]).astype(jnp.int32)

    win = [_place_own_shard(ab_w_in if L % 2 == 0 else sg_w_in, L // 2, chip_idx) for L in range(4)]
    wout = [_place_own_shard(ab_w_out if L % 2 == 0 else sg_w_out, L // 2, chip_idx) for L in range(4)]
    k_in, k_out = d, wout[0].shape[1]

    def gather_plan(ici=(), pass_on=()):
        arrays, copies = [], []
        for stage, make in ((ici, _gather_ici), (pass_on, _gather_pass_on)):
            for kind, L in stage:
                arr = win[L] if kind == "in" else wout[L]
                arrays.append(arr)
                copies += make(len(arrays) - 1, k_in if kind == "in" else k_out)
        return _Plan(tuple(arrays), tuple(copies)), [(kind, L) for kind, L in tuple(ici) + tuple(pass_on)]

    def absorb(plan_and_names, updated):
        _, names = plan_and_names
        for pos, (kind, L) in enumerate(names):
            if kind == "in":
                win[L] = updated[pos]
            else:
                wout[L] = updated[pos]

    first = _comm_stages("gather_first_layer", [win[0], wout[0]],
                         [_gather_ici(0, k_in) + _gather_ici(1, k_out),
                          _gather_pass_on(0, k_in) + _gather_pass_on(1, k_out)])
    win[0], wout[0] = first[0], first[1]

    small_local = [c[0], ab_conv_w, sg_norm_g, sg_ln_g, sg_ln_b]
    small_shapes = [a.shape for a in small_local]
    payload, small_off = _pack(small_local)
    gathered = _all_to_all(jnp.broadcast_to(payload[None], (N_DEV,) + payload.shape), "gather_small")
    per_dev = [_unpack(gathered[b], small_off, small_shapes) for b in range(N_DEV)]
    c_all = jnp.stack([per_dev[b][0] for b in range(N_DEV)])

    def from_chips(idx, axis):
        return jnp.concatenate([per_dev[2 * q][idx] for q in range(N_CHIPS)], axis=axis)

    conv_w_full = from_chips(1, 2)
    sg_norm_g_full = from_chips(2, 1)
    sg_ln_g_full = from_chips(3, 1)
    sg_ln_b_full = from_chips(4, 1)

    ab_b_local = lax.dynamic_slice_in_dim(ab_b_mod, chip * mod_l, mod_l, axis=1)
    mod_rows = []
    for layer in range(4):
        i = layer // 2
        w_mod, bias = (ab_w_mod, ab_b_local) if layer % 2 == 0 else (sg_w_mod, sg_b_mod)
        mod_rows.append(_mod_fwd(c_all, w_mod, bias[i:i + 1], i))
    mod_local = jnp.stack(mod_rows, axis=1)
    mod_recv = _all_to_all(mod_local.reshape(N_DEV, -1, LANES), "exchange_mod")
    mod_recv = mod_recv.reshape(N_DEV, 4, mod_l)
    mod_full = jnp.concatenate([mod_recv[2 * q] for q in range(N_CHIPS)], axis=-1)
    shifts = [mod_full[l:l + 1, :d] for l in range(4)]
    scales = [mod_full[l:l + 1, d:2 * d] for l in range(4)]
    gates = [mod_full[l:l + 1, 2 * d:] for l in range(4)]

    cos, sin = _rope_tables(s)
    w_s16 = sg_w_s.astype(BF16)
    b_s3 = sg_b_s[..., None]

    fwd_comm = {
        ("in_proj", 0): ([("in", 1)], []),
        ("attn", 0): ([("out", 1), ("in", 2)], []),
        ("out_proj", 0): ([], [("in", 1), ("out", 1)]),
        ("in_proj", 1): ([("out", 2)], [("in", 2)]),
        ("out_proj", 1): ([], [("out", 2)]),
        ("in_proj", 2): ([("in", 3)], []),
        ("attn", 2): ([("out", 3)], []),
        ("out_proj", 2): ([], [("in", 3), ("out", 3)]),
    }

    def carried(key, fn, *args):
        if key not in fwd_comm:
            return fn(*args)
        pn = gather_plan(*fwd_comm[key])
        res, updated = fn(*args, plan=pn[0])
        absorb(pn, updated)
        return res

    saved = []
    xs = x0
    for layer in range(4):
        i = layer // 2
        if layer % 2 == 0:
            h = _prenorm(xs, ab_norm_g[i:i + 1], scales[layer], shifts[layer])
            proj = carried(("in_proj", layer), _in_proj, h, win[layer])
            attn, lse = carried(("attn", layer), _attn_fwd, proj, cos, sin, aw)
            y = _ab_mix(attn, proj, conv_w_full[i], aw)
            x_next, out = carried(("out_proj", layer), _out_proj_residual, y, wout[layer].reshape(-1, d), xs,
                                  gates[layer])
            saved.append((xs, h, proj, y, out, attn, lse))
        else:
            h = _prenorm(xs, sg_norm_g_full[i:i + 1], scales[layer], shifts[layer])
            uvz = carried(("in_proj", layer), _in_proj, h, win[layer])
            y = _sgu_fwd(uvz, sg_ln_g_full[i:i + 1], sg_ln_b_full[i:i + 1], w_s16[i], b_s3[i], cw)
            x_next, out = carried(("out_proj", layer), _out_proj_residual, y, wout[layer].reshape(-1, d), xs,
                                  gates[layer])
            saved.append((xs, h, uvz, y, out))
        xs = x_next

    loss11, dx, d_final_g, dout, dgate = _final_loss(xs, target, final_norm_g[None], saved[3][4], gates[3])
    loss = lax.psum(loss11[0, 0], ("x", "y", "c"))

    reduced = {"in": [lax.empty((2,) + w.shape[1:], F32) for w in (ab_w_in, sg_w_in)],
               "out": [lax.empty((2,) + w.shape[1:], F32) for w in (ab_w_out, sg_w_out)]}
    grads = {}
    stage = {}
    k_of = {"in": k_in, "out": k_out}

    def swap_plan(L):
        arrays, copies = [], []
        for kind in ("in", "out"):
            g = grads[kind, L]
            arrays += [g, jax.ShapeDtypeStruct((N_CHIPS, g.shape[1] // 2, g.shape[2]), BF16)]
            copies += _reduce_swap(len(arrays) - 2, len(arrays) - 1, k_of[kind])
        return _Plan(tuple(arrays), tuple(copies))

    def after_swap(L, updated):
        for pos, kind in enumerate(("in", "out")):
            stage[kind, L] = _add_sibling(grads[kind, L], updated[2 * pos + 1], core_idx)

    def ici_plan(L, pieces):
        arrays, copies = [], []
        for kind, only in pieces:
            cs = stage[kind, L]
            arrays += [cs, stage.get((kind, L, "recv"), jax.ShapeDtypeStruct((3,) + cs.shape[1:], BF16))]
            copies += _reduce_ici(len(arrays) - 2, len(arrays) - 1, only)
        return _Plan(tuple(arrays), tuple(copies))

    def after_ici(L, pieces, updated):
        for pos, (kind, _) in enumerate(pieces):
            stage[kind, L, "recv"] = updated[2 * pos + 1]

    def sum_layer(L):
        for kind in ("in", "out"):
            reduced[kind][L % 2] = _sum_chips(stage[kind, L], stage[kind, L, "recv"], reduced[kind][L % 2],
                                              L // 2, place_idx)

    def share_plan(L):
        arrays = (reduced["in"][L % 2], reduced["out"][L % 2])
        copies = _reduce_share(0, L // 2, k_in) + _reduce_share(1, L // 2, k_out)
        return _Plan(arrays, tuple(copies))

    def after_share(L, updated):
        reduced["in"][L % 2], reduced["out"][L % 2] = updated[0], updated[1]

    all_chips = (0, 1, 2)
    dmods = [None] * 4
    d_ab_norm_g, d_sg_norm_g = [None, None], [None, None]
    d_conv_w, d_ln_g, d_ln_b, d_w_s, d_b_s = ([None, None] for _ in range(5))
    for layer in reversed(range(4)):
        i = layer // 2
        prev = layer + 1
        busy = prev < 4
        if layer % 2 == 0:
            xs, h, proj, y, out, attn, lse = saved[layer]
        else:
            xs, h, uvz, y, out = saved[layer]
        below = (saved[layer - 1][4], gates[layer - 1]) if layer > 0 else None
        w2 = wout[layer].reshape(-1, d)
        if busy:
            dy, updated = _out_proj_bwd_act(dout, w2, plan=swap_plan(prev))
            after_swap(prev, updated)
        else:
            dy = _out_proj_bwd_act(dout, w2)
        grads["out", layer] = _out_proj_bwd_w(y, dout).reshape(N_CHIPS, -1, d)
        if layer % 2 == 0:
            if busy:
                pieces = [("in", all_chips), ("out", all_chips)]
                dqkv, updated = _attn_bwd(proj, cos, sin, dy, attn, lse, aw, plan=ici_plan(prev, pieces))
                after_ici(prev, pieces, updated)
                sum_layer(prev)
            else:
                dqkv = _attn_bwd(proj, cos, sin, dy, attn, lse, aw)
            dact, d_conv_w[i] = _ab_dproj(dqkv, dy, attn, proj, conv_w_full[i], aw)
            if busy:
                dh, updated = _in_proj_bwd_act(dact, win[layer], plan=share_plan(prev))
                after_share(prev, updated)
            else:
                dh = _in_proj_bwd_act(dact, win[layer])
            grads["in", layer] = _in_proj_bwd_w(h, dact, win[layer].shape[-1])
            norm_g = ab_norm_g[i:i + 1]
        else:
            sgu_args = (uvz, dy, sg_ln_g_full[i:i + 1], sg_ln_b_full[i:i + 1], w_s16[i], b_s3[i], cw)
            if busy:
                pieces = [("in", (0, 1)), ("out", all_chips)]
                res, updated = _sgu_bwd(*sgu_args, plan=ici_plan(prev, pieces))
                after_ici(prev, pieces, updated)
            else:
                res = _sgu_bwd(*sgu_args)
            dact, d_w_s[i], db_wide, d_ln_g[i], d_ln_b[i] = res
            d_b_s[i] = db_wide[:, :, 0]
            if busy:
                pieces = [("in", (2,))]
                dh, updated = _in_proj_bwd_act(dact, win[layer], plan=ici_plan(prev, pieces))
                after_ici(prev, pieces, updated)
                sum_layer(prev)
                grads["in", layer], updated = _in_proj_bwd_w(h, dact, win[layer].shape[-1], plan=share_plan(prev))
                after_share(prev, updated)
            else:
                dh = _in_proj_bwd_act(dact, win[layer])
                grads["in", layer] = _in_proj_bwd_w(h, dact, win[layer].shape[-1])
            norm_g = sg_norm_g_full[i:i + 1]
        res = _prenorm_bwd(xs, dh, dx, norm_g, scales[layer], below)
        dx, dshift, dscale, d_norm_g = res[:4]
        (d_ab_norm_g if layer % 2 == 0 else d_sg_norm_g)[i] = d_norm_g
        dmods[layer] = jnp.concatenate([dshift, dscale, dgate], axis=1)
        if below:
            dout, dgate = res[4:]
    grad_x = dx[None]

    plan = swap_plan(0)
    after_swap(0, _comm_stages("grads_to_sibling", plan.arrays, [plan.copies]))

    partial_list = [jnp.concatenate(dmods, axis=0),
                    jnp.concatenate(d_ab_norm_g, axis=0), jnp.concatenate(d_sg_norm_g, axis=0), d_final_g[0],
                    jnp.stack(d_conv_w), jnp.concatenate(d_ln_g, axis=0), jnp.concatenate(d_ln_b, axis=0),
                    jnp.stack(d_w_s), jnp.stack(d_b_s)]
    partial_shapes = [a.shape for a in partial_list]
    partials, part_off = _pack(partial_list)
    all_partials = _all_to_all(jnp.broadcast_to(partials[None], (N_DEV,) + partials.shape), "gather_partials")
    pieces = [("out", all_chips)]
    summed_packed, updated = _sum_devices(all_partials, plan=ici_plan(0, pieces))
    after_ici(0, pieces, updated)
    summed = _unpack(summed_packed, part_off, partial_shapes)
    (g_mod_bias, g_ab_norm_g, g_sg_norm_g_full, g_final_g, g_conv_full, g_ln_g_full, g_ln_b_full,
     g_w_s, g_b_s) = summed
    dm_all = jnp.stack([_unpack(all_partials[b], part_off[:1], partial_shapes[:1])[0] for b in range(N_DEV)])
    dm_local = lax.dynamic_slice_in_dim(dm_all, chip * mod_l, mod_l, axis=2)

    def chip_cols(a, axis):
        width = a.shape[axis] // N_CHIPS
        return lax.dynamic_slice_in_dim(a, chip * width, width, axis=axis)

    g_ab_b_mod = jnp.stack([g_mod_bias[0], g_mod_bias[2]])
    g_sg_b_mod = chip_cols(jnp.stack([g_mod_bias[1], g_mod_bias[3]]), 1)
    g_ab_w_mod = jnp.stack([_mod_bwd_w(c_all, dm_local[:, 0]), _mod_bwd_w(c_all, dm_local[:, 2])])
    g_sg_w_mod = jnp.stack([_mod_bwd_w(c_all, dm_local[:, 1]), _mod_bwd_w(c_all, dm_local[:, 3])])
    g_conv = chip_cols(g_conv_full, 2)
    g_sg_norm_g = chip_cols(g_sg_norm_g_full, 1)
    g_ln_g = chip_cols(g_ln_g_full, 1)
    g_ln_b = chip_cols(g_ln_b_full, 1)

    def step_big(w, g, m, v, plan=None):
        res = _adamw(_flat_rows(w), _flat_rows(g), _flat_rows(m), _flat_rows(v), plan=plan)
        (dl, nm, nv), updated = res if plan is not None else (res, None)
        return (dl.reshape(w.shape), nm.reshape(w.shape), nv.reshape(w.shape)), updated

    big_out = {}
    pieces = [("in", (0, 1))]
    big_out["ab_w_mod"], updated = step_big(ab_w_mod, g_ab_w_mod, m_ab_w_mod, v_ab_w_mod, ici_plan(0, pieces))
    after_ici(0, pieces, updated)
    pieces = [("in", (2,))]
    big_out["sg_w_mod"], updated = step_big(sg_w_mod, g_sg_w_mod, m_sg_w_mod, v_sg_w_mod, ici_plan(0, pieces))
    after_ici(0, pieces, updated)
    sum_layer(0)
    big_out["sg_w_in"], updated = step_big(sg_w_in, reduced["in"][1], m_sg_w_in, v_sg_w_in, share_plan(0))
    after_share(0, updated)
    g_ab_w_in, g_sg_w_in = reduced["in"]
    g_ab_w_out, g_sg_w_out = reduced["out"]
    big_out["ab_w_in"], _ = step_big(ab_w_in, g_ab_w_in, m_ab_w_in, v_ab_w_in)
    big_out["ab_w_out"], _ = step_big(ab_w_out, g_ab_w_out, m_ab_w_out, v_ab_w_out)
    big_out["sg_w_out"], _ = step_big(sg_w_out, g_sg_w_out, m_sg_w_out, v_sg_w_out)
    small_names = ["ab_norm_g", "ab_b_mod", "ab_conv_w", "sg_norm_g", "sg_b_mod", "sg_ln_g", "sg_ln_b",
                   "sg_w_s", "sg_b_s", "final_norm_g"]
    small_w = [ab_norm_g, ab_b_mod, ab_conv_w, sg_norm_g, sg_b_mod, sg_ln_g, sg_ln_b, sg_w_s, sg_b_s, final_norm_g]
    small_g = [g_ab_norm_g, g_ab_b_mod, g_conv, g_sg_norm_g, g_sg_b_mod, g_ln_g, g_ln_b, g_w_s, g_b_s, g_final_g]
    small_m = [m_ab_norm_g, m_ab_b_mod, m_ab_conv_w, m_sg_norm_g, m_sg_b_mod, m_sg_ln_g, m_sg_ln_b, m_sg_w_s,
               m_sg_b_s, m_final_norm_g]
    small_v = [v_ab_norm_g, v_ab_b_mod, v_ab_conv_w, v_sg_norm_g, v_sg_b_mod, v_sg_ln_g, v_sg_ln_b, v_sg_w_s,
               v_sg_b_s, v_final_norm_g]
    shapes = [a.shape for a in small_w]
    pw, off = _pack(small_w)
    pg, _ = _pack(small_g)
    pm, _ = _pack(small_m)
    pv, _ = _pack(small_v)
    pd, pnm, pnv = _adamw(pw, pg, pm, pv)
    small_out = {}
    for name, dl, nm, nv in zip(small_names, _unpack(pd, off, shapes), _unpack(pnm, off, shapes),
                                _unpack(pnv, off, shapes)):
        small_out[name] = (dl, nm, nv)

    grad_of = {
        "ab_norm_g": g_ab_norm_g, "ab_w_mod": g_ab_w_mod, "ab_b_mod": g_ab_b_mod, "ab_w_in": g_ab_w_in,
        "ab_conv_w": g_conv, "ab_w_out": g_ab_w_out, "sg_norm_g": g_sg_norm_g, "sg_w_mod": g_sg_w_mod,
        "sg_b_mod": g_sg_b_mod, "sg_w_in": g_sg_w_in, "sg_ln_g": g_ln_g, "sg_ln_b": g_ln_b, "sg_w_s": g_w_s,
        "sg_b_s": g_b_s, "sg_w_out": g_sg_w_out, "final_norm_g": g_final_g,
    }
    order = ["ab_norm_g", "ab_w_mod", "ab_b_mod", "ab_w_in", "ab_conv_w", "ab_w_out", "sg_norm_g", "sg_w_mod",
             "sg_b_mod", "sg_w_in", "sg_ln_g", "sg_ln_b", "sg_w_s", "sg_b_s", "sg_w_out", "final_norm_g"]
    steps = {**big_out, **small_out}
    return (loss, grad_x, *[grad_of[n] for n in order], *[steps[n][0] for n in order],
            *[steps[n][1] for n in order], *[steps[n][2] for n in order])
```

```python
import math
from typing import Any, Callable, NamedTuple

import jax
import jax.numpy as jnp
from jax import lax
from jax.experimental import pallas as pl
from jax.experimental.pallas import tpu as pltpu

F32 = jnp.float32
BF16 = jnp.bfloat16

HEAD_DIM = 128
RADIUS = 64
DILATIONS = (1, 4, 16)
Q_BLOCK = 256
K_WINDOW = Q_BLOCK + 2 * RADIUS
ROPE_THETA = 10000.0
NEG_INF = -1e30
N_GROUPS = 8
CHUNK = 128
EPS = 1e-6
CONV_ROWS = 512
CONV_HALO = 16
LANES = 128
ELEMENTWISE_BLOCK = 256 * 1024
N_DEV = 8
N_CHIPS = 4

ADAM_LR = 0.001
ADAM_B1 = 0.9
ADAM_B2 = 0.999
ADAM_EPS = 1e-08
ADAM_WD = 0.01
ADAM_STEP = 10

VMEM_LIMIT_V7X = 56 * 1024 * 1024

MESH_ID = pl.DeviceIdType.MESH
HBM_SPEC = pl.BlockSpec(memory_space=pltpu.HBM)

NN = (((1,), (0,)), ((), ()))
NT = (((1,), (1,)), ((), ()))
TN = (((0,), (0,)), ((), ()))


def _params(n_grid, parallel=0):
    sem = tuple(["parallel"] * parallel + ["arbitrary"] * (n_grid - parallel))
    return pltpu.CompilerParams(dimension_semantics=sem, vmem_limit_bytes=VMEM_LIMIT_V7X)


def _tile(n, prefs):
    for p in prefs:
        if n % p == 0:
            return p
    return n


def _sigmoid(z):
    return 1.0 / (1.0 + jnp.exp(-z))


def _silu(z):
    return z * _sigmoid(z)


def _dsilu(z):
    s = _sigmoid(z)
    return s * (1.0 + z * (1.0 - s))


_GELU_K = math.sqrt(2.0 / math.pi)
_GELU_C = 0.044715


def _gelu(u):
    return 0.5 * u * (1.0 + jnp.tanh(_GELU_K * (u + _GELU_C * u * u * u)))


def _dgelu(u):
    t = jnp.tanh(_GELU_K * (u + _GELU_C * u * u * u))
    return 0.5 * (1.0 + t) + 0.5 * u * (1.0 - t * t) * _GELU_K * (1.0 + 3.0 * _GELU_C * u * u)


class _Place(NamedTuple):
    x: Any
    y: Any
    c: Any
    chip: Any


def _my_place():
    mx, my, mc = lax.axis_index("x"), lax.axis_index("y"), lax.axis_index("c")
    return _Place(mx, my, mc, 2 * mx + my)


def _other_chips(p):
    return [(1 - p.x, p.y), (p.x, 1 - p.y), (1 - p.x, 1 - p.y)]


class _Copy(NamedTuple):
    src: int
    src_at: Callable
    dst: int
    dst_at: Callable
    peer: Callable


class _Plan(NamedTuple):
    arrays: tuple
    copies: tuple


def _view(ref, index):
    return ref if index is None else ref.at[index]


def _plan_io(plan):
    ins = [k for k, a in enumerate(plan.arrays) if not isinstance(a, jax.ShapeDtypeStruct)]
    written = sorted({cp.dst for cp in plan.copies})
    return ins, written


def _descriptors(plan, in_refs, out_refs, send_sems, recv_sems):
    ins, written = _plan_io(plan)
    place = _my_place()
    return [
        pltpu.make_async_remote_copy(
            src_ref=_view(in_refs[ins.index(cp.src)], cp.src_at(place)),
            dst_ref=_view(out_refs[written.index(cp.dst)], cp.dst_at(place)),
            send_sem=send_sems.at[k], recv_sem=recv_sems.at[k],
            device_id=cp.peer(place), device_id_type=MESH_ID)
        for k, cp in enumerate(plan.copies)]


def _plan_operands(plan, n_in, n_out):
    ins, written = _plan_io(plan)
    operands = [plan.arrays[k] for k in ins]
    out_shape = [jax.ShapeDtypeStruct(plan.arrays[k].shape, plan.arrays[k].dtype) for k in written]
    aliases = {n_in + ins.index(k): n_out + pos for pos, k in enumerate(written) if k in ins}
    n = len(plan.copies)
    sems = [pltpu.SemaphoreType.DMA((n,)), pltpu.SemaphoreType.DMA((n,))]
    return operands, out_shape, aliases, sems, written


def _call(body, *, name, grid, in_specs, out_specs, out_shape, operands, scratch_shapes=(), aliases=None,
          parallel=0, plan=None):
    single = not isinstance(out_shape, (list, tuple))
    out_shape = [out_shape] if single else list(out_shape)
    out_specs = [out_specs] if single else list(out_specs)
    if plan is None:
        res = pl.pallas_call(
            body, name=name, grid=grid, in_specs=list(in_specs), out_specs=out_specs, out_shape=out_shape,
            scratch_shapes=list(scratch_shapes), input_output_aliases=aliases or {},
            compiler_params=_params(len(grid), parallel=parallel),
        )(*operands)
        return res[0] if single else res

    n_in, n_out, n_scr = len(operands), len(out_shape), len(scratch_shapes)
    p_operands, p_out_shape, p_aliases, sems, written = _plan_operands(plan, n_in, n_out)
    n_pin, n_pout = len(p_operands), len(p_out_shape)

    def wrapped(*refs):
        ins = refs[:n_in]
        p_in = refs[n_in:n_in + n_pin]
        outs = refs[n_in + n_pin:n_in + n_pin + n_out]
        p_out = refs[n_in + n_pin + n_out:n_in + n_pin + n_out + n_pout]
        scratch = refs[n_in + n_pin + n_out + n_pout:n_in + n_pin + n_out + n_pout + n_scr]
        send_sems, recv_sems = refs[-2:]
        ids = [pl.program_id(a) for a in range(len(grid))]
        first = ids[0] == 0
        last = ids[0] == grid[0] - 1
        for a in range(1, len(grid)):
            first = jnp.logical_and(first, ids[a] == 0)
            last = jnp.logical_and(last, ids[a] == grid[a] - 1)

        @pl.when(first)
        def _():
            for cp in _descriptors(plan, p_in, p_out, send_sems, recv_sems):
                cp.start()

        body(*ins, *outs, *scratch)

        @pl.when(last)
        def _():
            for cp in _descriptors(plan, p_in, p_out, send_sems, recv_sems):
                cp.wait()

    res = pl.pallas_call(
        wrapped, name=name, grid=grid,
        in_specs=list(in_specs) + [HBM_SPEC] * n_pin,
        out_specs=out_specs + [HBM_SPEC] * n_pout,
        out_shape=out_shape + p_out_shape,
        scratch_shapes=list(scratch_shapes) + sems,
        input_output_aliases={**(aliases or {}), **p_aliases},
        compiler_params=_params(len(grid)),
    )(*operands, *p_operands)
    outs = res[0] if single else res[:n_out]
    return outs, dict(zip(written, res[n_out:]))


def _comm_stages(name, arrays, stages):
    plan = _Plan(tuple(arrays), tuple(cp for st in stages for cp in st))
    p_operands, p_out_shape, p_aliases, sems, written = _plan_operands(plan, 0, 0)
    n_pin = len(p_operands)

    def body(*refs):
        p_in = refs[:n_pin]
        p_out = refs[n_pin:n_pin + len(written)]
        send_sems, recv_sems = refs[-2:]
        all_copies = _descriptors(plan, p_in, p_out, send_sems, recv_sems)
        base = 0
        for st in stages:
            for cp in all_copies[base:base + len(st)]:
                cp.start()
            for cp in all_copies[base:base + len(st)]:
                cp.wait()
            base += len(st)

    res = pl.pallas_call(
        body, name=name, in_specs=[HBM_SPEC] * n_pin, out_specs=[HBM_SPEC] * len(written),
        out_shape=p_out_shape, scratch_shapes=sems, input_output_aliases=p_aliases,
    )(*p_operands)
    return dict(zip(written, res))


def _half_rows(k, c):
    return pl.ds(c * (k // 2), k // 2)


def _gather_ici(a, k):
    own = lambda p: (p.chip, _half_rows(k, p.c))
    return [_Copy(a, own, a, own, lambda p, q=q: (*_other_chips(p)[q], p.c)) for q in range(3)]


def _gather_pass_on(a, k):
    def at(q):
        def index(p):
            px, py = _other_chips(p)[q]
            return (2 * px + py, _half_rows(k, p.c))
        return index
    return [_Copy(a, at(q), a, at(q), lambda p: (p.x, p.y, 1 - p.c)) for q in range(3)]


def _reduce_swap(src, dst, k):
    return [_Copy(src, lambda p: (pl.ds(0, N_CHIPS), _half_rows(k, 1 - p.c)), dst, lambda p: None,
                  lambda p: (p.x, p.y, 1 - p.c))]


def _reduce_ici(src, dst, only=(0, 1, 2)):
    def slab(q):
        def index(p):
            px, py = _other_chips(p)[q]
            return 2 * px + py
        return index
    return [_Copy(src, slab(q), dst, lambda p, q=q: q, lambda p, q=q: (*_other_chips(p)[q], p.c)) for q in only]


def _reduce_share(a, layer, k):
    at = lambda p: (layer, _half_rows(k, p.c))
    return [_Copy(a, at, a, at, lambda p: (p.x, p.y, 1 - p.c))]


def _all_to_all(x, name):
    def body(x_ref, y_ref, send_sems, recv_sems, own_sem):
        p = _my_place()
        me = 2 * p.chip + p.c
        own = pltpu.make_async_copy(x_ref.at[me], y_ref.at[me], own_sem)
        own.start()
        copies = []
        for k in range(1, N_DEV):
            px = 1 - p.x if (k >> 2) & 1 else p.x
            py = 1 - p.y if (k >> 1) & 1 else p.y
            pc = 1 - p.c if k & 1 else p.c
            peer = 4 * px + 2 * py + pc
            cp = pltpu.make_async_remote_copy(
                src_ref=x_ref.at[peer], dst_ref=y_ref.at[me],
                send_sem=send_sems.at[k - 1], recv_sem=recv_sems.at[k - 1],
                device_id=(px, py, pc), device_id_type=MESH_ID)
            cp.start()
            copies.append(cp)
        for cp in copies:
            cp.wait()
        own.wait()

    return pl.pallas_call(
        body, name=name,
        out_shape=jax.ShapeDtypeStruct(x.shape, x.dtype),
        in_specs=[HBM_SPEC], out_specs=HBM_SPEC,
        scratch_shapes=[pltpu.SemaphoreType.DMA((N_DEV - 1,)), pltpu.SemaphoreType.DMA((N_DEV - 1,)),
                        pltpu.SemaphoreType.DMA],
    )(x)


def _place_own_shard(w, layer, chip_idx):
    _, k, n = w.shape
    tr = _tile(k, (512, 256, 128))

    def body(c_ref, w_ref, g_ref):
        g_ref[...] = w_ref[...].astype(BF16)

    return pl.pallas_call(
        body, name="place_own_shard",
        grid_spec=pltpu.PrefetchScalarGridSpec(
            num_scalar_prefetch=1, grid=(k // tr,),
            in_specs=[pl.BlockSpec((None, tr, n), lambda r, c: (layer, r, 0))],
            out_specs=pl.BlockSpec((None, tr, n), lambda r, c: (c[0], r, 0))),
        out_shape=jax.ShapeDtypeStruct((N_CHIPS, k, n), BF16),
        compiler_params=_params(1, parallel=1),
    )(chip_idx, w)


def _matmul(name, operands, in_specs, grid, dims, out_shape, out_specs, epilogue, a_prologue=None,
            aliases=None, plan=None):
    n_in = len(operands)

    def body(*refs):
        a = refs[0][...]
        if a_prologue is not None:
            a = a_prologue(a)
        acc = lax.dot_general(a.astype(BF16), refs[1][...].astype(BF16), dims, preferred_element_type=F32)
        epilogue(acc, refs[2:n_in], refs[n_in:])

    return _call(body, name=name, grid=grid, in_specs=in_specs, out_specs=out_specs, out_shape=out_shape,
                 operands=operands, aliases=aliases, parallel=2, plan=plan)


def _store_cast(acc, extra, outs):
    outs[0][...] = acc.astype(outs[0].dtype)


def _in_proj(h, w, plan=None):
    s, d = h.shape
    nl = w.shape[-1]
    tm = _tile(s, (1024, 512, 256))
    tn = _tile(nl, (1024, 768, 512, 384, 256, 128))
    per = nl // tn
    return _matmul(
        "in_proj", (h, w),
        [pl.BlockSpec((tm, d), lambda i, j: (i, 0)),
         pl.BlockSpec((None, d, tn), lambda i, j: (j // per, 0, j % per))],
        (s // tm, N_CHIPS * per), NN,
        jax.ShapeDtypeStruct((s, N_CHIPS * nl), BF16),
        pl.BlockSpec((tm, tn), lambda i, j: (i, j)), _store_cast, plan=plan)


def _out_proj_residual(y, w2, x, gate, plan=None):
    s, wdt = y.shape
    d = w2.shape[-1]
    tm = _tile(s, (1024, 512, 256))
    tn = _tile(d, (1024, 512, 256, 128))

    def epilogue(acc, extra, outs):
        x_ref, gate_ref = extra
        outs[0][...] = x_ref[...] + gate_ref[...] * acc
        outs[1][...] = acc.astype(BF16)

    blk = pl.BlockSpec((tm, tn), lambda i, j: (i, j))
    return _matmul(
        "out_proj", (y, w2, x, gate),
        [pl.BlockSpec((tm, wdt), lambda i, j: (i, 0)),
         pl.BlockSpec((wdt, tn), lambda i, j: (0, j)),
         blk, pl.BlockSpec((1, tn), lambda i, j: (0, j))],
        (s // tm, d // tn), NN,
        [jax.ShapeDtypeStruct((s, d), F32), jax.ShapeDtypeStruct((s, d), BF16)],
        [blk, blk], epilogue, plan=plan)


def _out_proj_bwd_act(dout, w2, plan=None):
    s, d = dout.shape
    wdt = w2.shape[0]
    tm = _tile(s, (1024, 512, 256))
    tn = _tile(wdt, (1024, 512, 256, 128))
    return _matmul(
        "out_proj_dy", (dout, w2),
        [pl.BlockSpec((tm, d), lambda i, j: (i, 0)),
         pl.BlockSpec((tn, d), lambda i, j: (j, 0))],
        (s // tm, wdt // tn), NT,
        jax.ShapeDtypeStruct((s, wdt), BF16),
        pl.BlockSpec((tm, tn), lambda i, j: (i, j)), _store_cast, plan=plan)


def _out_proj_bwd_w(y, dout, plan=None):
    s, wdt = y.shape
    d = dout.shape[1]
    tm = _tile(wdt, (1024, 512, 256, 128))
    tn = _tile(d, (1024, 512, 256, 128))
    return _matmul(
        "out_proj_dw", (y, dout),
        [pl.BlockSpec((s, tm), lambda i, j: (0, i)),
         pl.BlockSpec((s, tn), lambda i, j: (0, j))],
        (wdt // tm, d // tn), TN,
        jax.ShapeDtypeStruct((wdt, d), BF16),
        pl.BlockSpec((tm, tn), lambda i, j: (i, j)), _store_cast, plan=plan)


def _in_proj_bwd_act(dproj, w, plan=None):
    s, n_all = dproj.shape
    d, nl = w.shape[1], w.shape[2]
    tm = _tile(s, (1024, 512, 256))
    tn = _tile(d, (512, 256, 128))

    def body(a_ref, w_ref, o_ref):
        acc = None
        for q in range(N_CHIPS):
            part = lax.dot_general(a_ref[:, q * nl:(q + 1) * nl], w_ref[q], NT, preferred_element_type=F32)
            acc = part if acc is None else acc + part
        o_ref[...] = acc.astype(BF16)

    return _call(
        body, name="in_proj_dh", grid=(s // tm, d // tn),
        in_specs=[pl.BlockSpec((tm, n_all), lambda i, j: (i, 0), pipeline_mode=pl.Buffered(1)),
                  pl.BlockSpec((N_CHIPS, tn, nl), lambda i, j: (0, j, 0))],
        out_specs=pl.BlockSpec((tm, tn), lambda i, j: (i, j)),
        out_shape=jax.ShapeDtypeStruct((s, d), BF16),
        operands=(dproj, w), parallel=2, plan=plan)


def _in_proj_bwd_w(h, dproj, nl, plan=None):
    s, d = h.shape
    tm = _tile(d, (1024, 512, 256, 128))
    tn = _tile(nl, (1024, 768, 512, 384, 256, 128))
    per = nl // tn
    return _matmul(
        "in_proj_dw", (h, dproj),
        [pl.BlockSpec((s, tm), lambda i, j: (0, i)),
         pl.BlockSpec((s, tn), lambda i, j: (0, j))],
        (d // tm, N_CHIPS * per), TN,
        jax.ShapeDtypeStruct((N_CHIPS, d, nl), BF16),
        pl.BlockSpec((None, tm, tn), lambda i, j: (j // per, i, j % per)), _store_cast, plan=plan)


def _mod_fwd(c_all, w_mod, bias, layer):
    nb, d = c_all.shape
    nl = w_mod.shape[-1]
    tn = _tile(nl, (768, 512, 384, 256, 128))

    def epilogue(acc, extra, outs):
        outs[0][...] = acc + extra[0][...]

    return _matmul(
        "mod_fwd", (c_all, w_mod, bias),
        [pl.BlockSpec((nb, d), lambda i, j: (0, 0)),
         pl.BlockSpec((None, d, tn), lambda i, j: (layer, 0, j)),
         pl.BlockSpec((1, tn), lambda i, j: (0, j))],
        (1, nl // tn), NN,
        jax.ShapeDtypeStruct((nb, nl), F32),
        pl.BlockSpec((nb, tn), lambda i, j: (0, j)), epilogue, a_prologue=_silu)


def _mod_bwd_w(c_all, dm_local):
    nb, d = c_all.shape
    nl = dm_local.shape[-1]
    tm = _tile(d, (1024, 512, 256, 128))
    tn = _tile(nl, (768, 512, 384, 256, 128))

    def epilogue(acc, extra, outs):
        outs[0][...] = acc

    return _matmul(
        "mod_dw", (c_all, dm_local),
        [pl.BlockSpec((nb, tm), lambda i, j: (0, i)),
         pl.BlockSpec((nb, tn), lambda i, j: (0, j))],
        (d // tm, nl // tn), TN,
        jax.ShapeDtypeStruct((d, nl), F32),
        pl.BlockSpec((tm, tn), lambda i, j: (i, j)), epilogue, a_prologue=_silu)


def _rows_call(name, body, operands, in_specs, out_shape, out_specs, n_tiles):
    return pl.pallas_call(
        body, name=name, grid=(n_tiles,), in_specs=in_specs, out_specs=out_specs, out_shape=out_shape,
        compiler_params=_params(1),
    )(*operands)


def _row_spec(tr, width):
    return pl.BlockSpec((tr, width), lambda i: (i, 0))


def _vec_spec(width):
    return pl.BlockSpec((1, width), lambda i: (0, 0))


def _accumulate(ref, val):
    first = pl.program_id(0) == 0

    @pl.when(first)
    def _():
        ref[...] = val

    @pl.when(jnp.logical_not(first))
    def _():
        ref[...] += val


def _prenorm(x, g, scale, shift):
    s, d = x.shape
    tr = _tile(s, (256, 128))

    def body(x_ref, g_ref, sc_ref, sh_ref, h_ref):
        xv = x_ref[...]
        rstd = lax.rsqrt(jnp.mean(xv * xv, axis=-1, keepdims=True) + EPS)
        h_ref[...] = ((xv * rstd) * g_ref[...] * (1.0 + sc_ref[...]) + sh_ref[...]).astype(BF16)

    return _rows_call("prenorm", body, (x, g, scale, shift),
                      [_row_spec(tr, d), _vec_spec(d), _vec_spec(d), _vec_spec(d)],
                      jax.ShapeDtypeStruct((s, d), BF16), _row_spec(tr, d), s // tr)


def _gate_grads(dxv, out_ref, gate_ref, dout_ref, dgate_ref):
    dout_ref[...] = (gate_ref[...] * dxv).astype(BF16)
    _accumulate(dgate_ref, jnp.sum(dxv * out_ref[...].astype(F32), axis=0, keepdims=True))


def _prenorm_bwd(x, dh, dres, g, scale, below=None):
    s, d = x.shape
    tr = _tile(s, (256, 128))

    def body(x_ref, dh_ref, dres_ref, g_ref, sc_ref, *rest):
        dx_ref, dshift_ref, dscale_ref, dg_ref = rest[-6:-2] if below else rest
        xv = x_ref[...]
        dhv = dh_ref[...].astype(F32)
        rstd = lax.rsqrt(jnp.mean(xv * xv, axis=-1, keepdims=True) + EPS)
        xhat = xv * rstd
        gv = g_ref[...]
        one_sc = 1.0 + sc_ref[...]
        dxhat = dhv * gv * one_sc
        dxv = dres_ref[...] + rstd * (dxhat - xhat * jnp.mean(dxhat * xhat, axis=-1, keepdims=True))
        dx_ref[...] = dxv
        _accumulate(dshift_ref, jnp.sum(dhv, axis=0, keepdims=True))
        _accumulate(dscale_ref, jnp.sum(dhv * xhat * gv, axis=0, keepdims=True))
        _accumulate(dg_ref, jnp.sum(dhv * xhat * one_sc, axis=0, keepdims=True))
        if below:
            _gate_grads(dxv, rest[0], rest[1], rest[-2], rest[-1])

    vec = jax.ShapeDtypeStruct((1, d), F32)
    operands = (x, dh, dres, g, scale) + (tuple(below) if below else ())
    in_specs = [_row_spec(tr, d), _row_spec(tr, d), _row_spec(tr, d), _vec_spec(d), _vec_spec(d)]
    out_shape = [jax.ShapeDtypeStruct((s, d), F32), vec, vec, vec]
    out_specs = [_row_spec(tr, d), _vec_spec(d), _vec_spec(d), _vec_spec(d)]
    if below:
        in_specs += [_row_spec(tr, d), _vec_spec(d)]
        out_shape += [jax.ShapeDtypeStruct((s, d), BF16), vec]
        out_specs += [_row_spec(tr, d), _vec_spec(d)]
    return _rows_call("prenorm_bwd", body, operands, in_specs, out_shape, out_specs, s // tr)


def _final_loss(x, target, g, out_below, gate_below):
    s, d = x.shape
    tr = _tile(s, (256, 128))
    n_tiles = s // tr

    def body(x_ref, t_ref, g_ref, out_ref, gate_ref, loss_ref, dx_ref, dg_ref, dout_ref, dgate_ref, acc_ref):
        xv = x_ref[...]
        rstd = lax.rsqrt(jnp.mean(xv * xv, axis=-1, keepdims=True) + EPS)
        xhat = xv * rstd
        gv = g_ref[...]
        err = xhat * gv - t_ref[...]
        dy = err * (1.0 / d)
        dxhat = dy * gv
        dxv = rstd * (dxhat - xhat * jnp.mean(dxhat * xhat, axis=-1, keepdims=True))
        dx_ref[...] = dxv
        _accumulate(dg_ref, jnp.sum(dy * xhat, axis=0, keepdims=True))
        _accumulate(acc_ref, jnp.sum(err * err, axis=0, keepdims=True))
        _gate_grads(dxv, out_ref, gate_ref, dout_ref, dgate_ref)

        @pl.when(pl.program_id(0) == n_tiles - 1)
        def _():
            loss_ref[...] = (0.5 / d) * jnp.sum(acc_ref[...], axis=1, keepdims=True)

    vec = jax.ShapeDtypeStruct((1, d), F32)
    return pl.pallas_call(
        body, name="final_loss", grid=(n_tiles,),
        in_specs=[_row_spec(tr, d), _row_spec(tr, d), _vec_spec(d), _row_spec(tr, d), _vec_spec(d)],
        out_specs=[pl.BlockSpec((1, 1), lambda i: (0, 0)), _row_spec(tr, d), _vec_spec(d), _row_spec(tr, d),
                   _vec_spec(d)],
        out_shape=[jax.ShapeDtypeStruct((1, 1), F32), jax.ShapeDtypeStruct((s, d), F32), vec,
                   jax.ShapeDtypeStruct((s, d), BF16), vec],
        scratch_shapes=[pltpu.VMEM((1, d), F32)],
        compiler_params=_params(1),
    )(x, target, g, out_below, gate_below)


def _rope(t, cos, sin):
    return t * cos + pltpu.roll(t, HEAD_DIM // 2, axis=1) * sin


def _unrope(dt, cos, sin):
    return dt * cos + pltpu.roll(dt * sin, HEAD_DIM // 2, axis=1)


def _band_blocks(s, dil):
    sub = s // dil
    kw = min(K_WINDOW, sub)

    def rows(r, start, n):
        if dil == 1:
            return pl.ds(pl.multiple_of(start, RADIUS), n)
        return pl.ds(r + dil * start, n, stride=dil)

    def window(idx):
        nb = sub // Q_BLOCK
        r, b = idx // nb, idx % nb
        q0 = b * Q_BLOCK
        start = jnp.clip(q0 - RADIUS, 0, sub - kw)
        ahead = (lax.broadcasted_iota(jnp.int32, (Q_BLOCK, kw), 1)
                 - lax.broadcasted_iota(jnp.int32, (Q_BLOCK, kw), 0)) + (start - q0 + RADIUS)
        valid = lax.bitcast_convert_type(ahead, jnp.uint32) <= 2 * RADIUS
        return rows(r, q0, Q_BLOCK), rows(r, start, kw), valid

    return window


def _head_col(s, group, nh):
    return pl.BlockSpec((s, HEAD_DIM), lambda h: (0, group * nh + h), pipeline_mode=pl.Buffered(1))


def _attn_fwd(proj, cos, sin, aw, plan=None):
    s = proj.shape[0]
    nh = aw // HEAD_DIM
    scale = HEAD_DIM ** -0.5
    n_blocks = s // Q_BLOCK

    def body(q_ref, k_ref, v_ref, cos_ref, sin_ref, attn_ref, lse_ref, qf, kf, vf, acc):
        cosv, sinv = cos_ref[...], sin_ref[...]
        qf[...] = _rope(q_ref[...].astype(F32), cosv, sinv) * scale
        kf[...] = _rope(k_ref[...].astype(F32), cosv, sinv)
        vf[...] = v_ref[...].astype(F32)

        for pattern, dil in enumerate(DILATIONS):
            window = _band_blocks(s, dil)

            def block(idx, carry, window=window, first=(pattern == 0)):
                q_rows, k_rows, valid = window(idx)
                q = qf[q_rows, :].astype(BF16)
                kk = kf[k_rows, :].astype(BF16)
                vv = vf[k_rows, :].astype(BF16)
                sc = lax.dot_general(q, kk, NT, preferred_element_type=F32)
                sc = jnp.where(valid, sc, NEG_INF)
                m = jnp.max(sc, axis=1, keepdims=True)
                p = jnp.exp(sc - m)
                den = jnp.sum(p, axis=1, keepdims=True)
                o = lax.dot_general(p.astype(BF16), vv, NN, preferred_element_type=F32) / den
                lse = jnp.broadcast_to(m + jnp.log(den), (Q_BLOCK, HEAD_DIM))
                if first:
                    acc[q_rows, :] = o
                    lse_ref[q_rows, :] = lse
                else:
                    lse_old = lse_ref[q_rows, :]
                    top = jnp.maximum(lse_old, lse)
                    w_old, w_new = jnp.exp(lse_old - top), jnp.exp(lse - top)
                    tot = w_old + w_new
                    acc[q_rows, :] = (acc[q_rows, :] * w_old + o * w_new) / tot
                    lse_ref[q_rows, :] = top + jnp.log(tot)
                return carry

            lax.fori_loop(0, n_blocks, block, 0, unroll=2)

        attn_ref[...] = acc[...].astype(BF16)

    table = pl.BlockSpec((s, HEAD_DIM), lambda h: (0, 0), pipeline_mode=pl.Buffered(1))
    out = pl.BlockSpec((s, HEAD_DIM), lambda h: (0, h))
    return _call(
        body, name="attn_fwd", grid=(nh,),
        in_specs=[_head_col(s, 0, nh), _head_col(s, 1, nh), _head_col(s, 2, nh), table, table],
        out_specs=[out, out],
        out_shape=[jax.ShapeDtypeStruct((s, aw), BF16), jax.ShapeDtypeStruct((s, aw), F32)],
        scratch_shapes=[pltpu.VMEM((s, HEAD_DIM), F32)] * 4,
        operands=(proj, proj, proj, cos, sin), parallel=1, plan=plan)


def _attn_bwd(proj, cos, sin, dy, attn, lse, aw, plan=None):
    s = proj.shape[0]
    nh = aw // HEAD_DIM
    scale = HEAD_DIM ** -0.5
    n_blocks = s // Q_BLOCK

    def body(q_ref, k_ref, v_ref, za_ref, cos_ref, sin_ref, dy_ref, attn_ref, lse_ref,
             dq_ref, dk_ref, dv_ref, qf, kf, vf, dof, delta, dqa, dka, dva):
        cosv, sinv = cos_ref[...], sin_ref[...]
        qf[...] = _rope(q_ref[...].astype(F32), cosv, sinv) * scale
        kf[...] = _rope(k_ref[...].astype(F32), cosv, sinv)
        vf[...] = v_ref[...].astype(F32)
        do_all = dy_ref[...].astype(F32) * _silu(za_ref[...].astype(F32))
        dof[...] = do_all
        delta[...] = jnp.broadcast_to(
            jnp.sum(do_all * attn_ref[...].astype(F32), axis=1, keepdims=True), (s, HEAD_DIM))
        dqa[...] = jnp.zeros_like(dqa)
        dka[...] = jnp.zeros_like(dka)
        dva[...] = jnp.zeros_like(dva)

        for dil in DILATIONS:
            window = _band_blocks(s, dil)

            def block(idx, carry, window=window):
                q_rows, k_rows, valid = window(idx)
                q = qf[q_rows, :].astype(BF16)
                kk = kf[k_rows, :].astype(BF16)
                vv = vf[k_rows, :].astype(BF16)
                dov = dof[q_rows, :].astype(BF16)
                lse_q = lse_ref[q_rows, :][:, 0:1]
                delta_q = delta[q_rows, :][:, 0:1]
                sc = lax.dot_general(q, kk, NT, preferred_element_type=F32)
                p = jnp.where(valid, jnp.exp(sc - lse_q), 0.0)
                dp = lax.dot_general(dov, vv, NT, preferred_element_type=F32)
                ds = (p * (dp - delta_q)).astype(BF16)
                dqa[q_rows, :] += lax.dot_general(ds, kk, NN, preferred_element_type=F32)
                dka[k_rows, :] += lax.dot_general(ds, q, TN, preferred_element_type=F32)
                dva[k_rows, :] += lax.dot_general(p.astype(BF16), dov, TN, preferred_element_type=F32)
                return carry

            lax.fori_loop(0, n_blocks, block, 0, unroll=2)

        dq_ref[...] = (_unrope(dqa[...], cosv, sinv) * scale).astype(BF16)
        dk_ref[...] = _unrope(dka[...], cosv, sinv).astype(BF16)
        dv_ref[...] = dva[...].astype(BF16)

    own = pl.BlockSpec((s, HEAD_DIM), lambda h: (0, h), pipeline_mode=pl.Buffered(1))
    table = pl.BlockSpec((s, HEAD_DIM), lambda h: (0, 0), pipeline_mode=pl.Buffered(1))
    out = pl.BlockSpec((s, HEAD_DIM), lambda h: (0, h))
    shape = jax.ShapeDtypeStruct((s, aw), BF16)
    return _call(
        body, name="attn_bwd", grid=(nh,),
        in_specs=[_head_col(s, 0, nh), _head_col(s, 1, nh), _head_col(s, 2, nh), _head_col(s, 3, nh),
                  table, table, own, own, own],
        out_specs=[out, out, out],
        out_shape=[shape, shape, shape],
        scratch_shapes=[pltpu.VMEM((s, HEAD_DIM), F32)] * 8,
        operands=(proj, proj, proj, proj, cos, sin, dy, attn, lse), parallel=1, plan=plan)


def _rope_tables(s):
    half = HEAD_DIM // 2
    inv = ROPE_THETA ** (-jnp.arange(half, dtype=F32) / half)
    ang = jnp.arange(s, dtype=F32)[:, None] * inv[None, :]
    cos, sin = jnp.cos(ang), jnp.sin(ang)
    return jnp.concatenate([cos, cos], axis=-1), jnp.concatenate([-sin, sin], axis=-1)


def _conv_chunks(s):
    for k in range(s // CONV_ROWS):
        lo = max(0, k * CONV_ROWS - CONV_HALO)
        hi = min(s, (k + 1) * CONV_ROWS + CONV_HALO)
        yield k * CONV_ROWS, lo, hi


def _neighbours(p, lo, s):
    n = p.shape[0]
    row = lo + lax.broadcasted_iota(jnp.int32, p.shape, 0)
    prev = jnp.where(row == 0, 0.0, pltpu.roll(p, 1, axis=0))
    nxt = jnp.where(row == s - 1, 0.0, pltpu.roll(p, n - 1, axis=0))
    return prev, nxt


def _ab_mix(attn, proj, conv_w, aw):
    s = proj.shape[0]
    nt = aw // LANES

    def col(group, sel):
        return pl.BlockSpec((s, LANES), lambda i: (0, group * nt + sel(i)))

    a_sel = lambda i: jnp.minimum(i, nt - 1)
    b_sel = lambda i: jnp.maximum(i - nt, 0)

    def body(attn_ref, za_ref, ub_ref, gb_ref, gc_ref, zb_ref, w_ref, y_ref):
        i = pl.program_id(0)

        @pl.when(i < nt)
        def _():
            y_ref[...] = (attn_ref[...].astype(F32) * _silu(za_ref[...].astype(F32))).astype(BF16)

        @pl.when(i >= nt)
        def _():
            w = w_ref[...]
            for c0, lo, hi in _conv_chunks(s):
                p = gc_ref[lo:hi, :].astype(F32) * ub_ref[lo:hi, :].astype(F32)
                prev, nxt = _neighbours(p, lo, s)
                cv = w[0:1, :] * prev + w[1:2, :] * p + w[2:3, :] * nxt
                yb = gb_ref[lo:hi, :].astype(F32) * cv * _silu(zb_ref[lo:hi, :].astype(F32))
                y_ref[c0:c0 + CONV_ROWS, :] = yb[c0 - lo:c0 - lo + CONV_ROWS, :].astype(BF16)

    return pl.pallas_call(
        body, name="ab_mix", grid=(2 * nt,),
        in_specs=[pl.BlockSpec((s, LANES), lambda i: (0, a_sel(i))),
                  col(3, a_sel), col(4, b_sel), col(5, b_sel), col(6, b_sel), col(7, b_sel),
                  pl.BlockSpec((3, LANES), lambda i: (0, b_sel(i)))],
        out_specs=pl.BlockSpec((s, LANES), lambda i: (0, i)),
        out_shape=jax.ShapeDtypeStruct((s, 2 * aw), BF16),
        compiler_params=_params(1),
    )(attn, proj, proj, proj, proj, proj, conv_w)


def _ab_dproj(dqkv, dy, attn, proj, conv_w, aw):
    s = proj.shape[0]
    nt = aw // LANES

    def col(group, sel):
        return pl.BlockSpec((s, LANES), lambda i: (0, group * nt + sel(i)))

    def qkv_spec(part):
        return pl.BlockSpec((s, LANES), lambda i: (0, jnp.clip(i - part * nt, 0, nt - 1)))

    a_sel = lambda i: jnp.clip(i - 3 * nt, 0, nt - 1)
    b_sel = lambda i: jnp.maximum(i - 4 * nt, 0) % nt
    w_sel = lambda i: jnp.clip(i - 4 * nt, 0, nt - 1)

    def body(*refs):
        g_refs = refs[:3]
        dya_ref, attn_ref, za_ref, dyb_ref, ub_ref, gb_ref, gc_ref, zb_ref, w_ref, out_ref, dw_ref = refs[3:]
        i = pl.program_id(0)

        for part in range(3):
            @pl.when(jnp.logical_and(i >= part * nt, i < (part + 1) * nt))
            def _(part=part):
                out_ref[...] = g_refs[part][...]

        @pl.when(jnp.logical_and(i >= 3 * nt, i < 4 * nt))
        def _():
            out_ref[...] = (dya_ref[...].astype(F32) * attn_ref[...].astype(F32)
                            * _dsilu(za_ref[...].astype(F32))).astype(BF16)

        for which in range(4):
            @pl.when(jnp.logical_and(i >= (4 + which) * nt, i < (5 + which) * nt))
            def _(which=which):
                w = w_ref[...]
                dw = [jnp.zeros((1, LANES), F32) for _ in range(3)]
                for c0, lo, hi in _conv_chunks(s):
                    ctr = slice(c0 - lo, c0 - lo + CONV_ROWS)
                    ub = ub_ref[lo:hi, :].astype(F32)
                    gc = gc_ref[lo:hi, :].astype(F32)
                    gb = gb_ref[lo:hi, :].astype(F32)
                    zb = zb_ref[lo:hi, :].astype(F32)
                    dyb = dyb_ref[lo:hi, :].astype(F32)
                    p = gc * ub
                    prev, nxt = _neighbours(p, lo, s)
                    if which == 1:
                        cv = w[0:1, :] * prev + w[1:2, :] * p + w[2:3, :] * nxt
                        res = dyb * cv * _silu(zb)
                    elif which == 3:
                        cv = w[0:1, :] * prev + w[1:2, :] * p + w[2:3, :] * nxt
                        res = dyb * gb * cv * _dsilu(zb)
                    else:
                        dcv = dyb * gb * _silu(zb)
                        dprev, dnxt = _neighbours(dcv, lo, s)
                        dp = w[0:1, :] * dnxt + w[1:2, :] * dcv + w[2:3, :] * dprev
                        res = dp * (gc if which == 0 else ub)
                        if which == 0:
                            for t, nb in enumerate((prev, p, nxt)):
                                dw[t] = dw[t] + jnp.sum((dcv * nb)[ctr, :], axis=0, keepdims=True)
                    out_ref[c0:c0 + CONV_ROWS, :] = res[ctr, :].astype(BF16)
                if which == 0:
                    dw_ref[...] = jnp.concatenate(dw, axis=0)

    return pl.pallas_call(
        body, name="ab_dproj", grid=(8 * nt,),
        in_specs=[qkv_spec(0), qkv_spec(1), qkv_spec(2),
                  pl.BlockSpec((s, LANES), lambda i: (0, a_sel(i))),
                  pl.BlockSpec((s, LANES), lambda i: (0, a_sel(i))),
                  col(3, a_sel),
                  pl.BlockSpec((s, LANES), lambda i: (0, nt + b_sel(i))),
                  col(4, b_sel), col(5, b_sel), col(6, b_sel), col(7, b_sel),
                  pl.BlockSpec((3, LANES), lambda i: (0, b_sel(i)))],
        out_specs=[pl.BlockSpec((s, LANES), lambda i: (0, i)),
                   pl.BlockSpec((3, LANES), lambda i: (0, w_sel(i)))],
        out_shape=[jax.ShapeDtypeStruct((s, 8 * aw), BF16), jax.ShapeDtypeStruct((3, aw), F32)],
        compiler_params=_params(1),
    )(*dqkv, dy, attn, proj, dy, proj, proj, proj, proj, conv_w)


def _sgu_norm(v, ln_g, ln_b):
    gv = _gelu(v)
    mu = jnp.mean(gv, axis=-1, keepdims=True)
    xc = gv - mu
    rstd = lax.rsqrt(jnp.mean(xc * xc, axis=-1, keepdims=True) + EPS)
    vhat = xc * rstd
    return vhat, rstd, vhat * ln_g + ln_b


def _sgu_fwd(uvz, ln_g, ln_b, w_s, b_s, cw):
    s = uvz.shape[0]
    tr = 2 * CHUNK if s % (2 * CHUNK) == 0 else CHUNK
    gw = cw // N_GROUPS

    def body(u_ref, v_ref, z_ref, g_ref, b_ref, ws_ref, bs_ref, y_ref):
        _, _, vn = _sgu_norm(v_ref[...].astype(F32), g_ref[...], b_ref[...])
        vn = vn.astype(BF16)
        for ch in range(tr // CHUNK):
            rows = slice(ch * CHUNK, (ch + 1) * CHUNK)
            for grp in range(N_GROUPS):
                cols = slice(grp * gw, (grp + 1) * gw)
                mixed = lax.dot_general(ws_ref[grp], vn[rows, cols], NN, preferred_element_type=F32) + bs_ref[grp]
                y_ref[rows, cols] = (_gelu(u_ref[rows, cols].astype(F32)) * mixed
                                     * _silu(z_ref[rows, cols].astype(F32))).astype(BF16)

    full3 = lambda shape: pl.BlockSpec(shape, lambda i: (0, 0, 0))
    return pl.pallas_call(
        body, name="sgu_fwd", grid=(s // tr,),
        in_specs=[pl.BlockSpec((tr, cw), lambda i: (i, 0)), pl.BlockSpec((tr, cw), lambda i: (i, 1)),
                  pl.BlockSpec((tr, cw), lambda i: (i, 2)), _vec_spec(cw), _vec_spec(cw),
                  full3(w_s.shape), full3(b_s.shape)],
        out_specs=pl.BlockSpec((tr, cw), lambda i: (i, 0)),
        out_shape=jax.ShapeDtypeStruct((s, cw), BF16),
        compiler_params=_params(1, parallel=1),
    )(uvz, uvz, uvz, ln_g, ln_b, w_s, b_s)


def _sgu_bwd(uvz, dy, ln_g, ln_b, w_s, b_s, cw, plan=None):
    s = uvz.shape[0]
    tr = 2 * CHUNK if s % (2 * CHUNK) == 0 else CHUNK
    gw = cw // N_GROUPS

    def body(u_ref, v_ref, z_ref, dy_ref, g_ref, b_ref, ws_ref, bs_ref,
             duvz_ref, dws_ref, dbs_ref, dg_ref, db_ref, dvn_ref):
        vv = v_ref[...].astype(F32)
        gvec = g_ref[...]
        vhat, rstd, vn = _sgu_norm(vv, gvec, b_ref[...])
        vn = vn.astype(BF16)
        first = pl.program_id(0) == 0

        @pl.when(first)
        def _():
            dws_ref[...] = jnp.zeros_like(dws_ref)
            dbs_ref[...] = jnp.zeros_like(dbs_ref)

        for ch in range(tr // CHUNK):
            rows = slice(ch * CHUNK, (ch + 1) * CHUNK)
            for grp in range(N_GROUPS):
                cols = slice(grp * gw, (grp + 1) * gw)
                vn_g = vn[rows, cols]
                mixed = lax.dot_general(ws_ref[grp], vn_g, NN, preferred_element_type=F32) + bs_ref[grp]
                uu = u_ref[rows, cols].astype(F32)
                zz = z_ref[rows, cols].astype(F32)
                dyv = dy_ref[rows, cols].astype(F32)
                gu, sz = _gelu(uu), _silu(zz)
                duvz_ref[rows, grp * gw:(grp + 1) * gw] = (dyv * mixed * sz * _dgelu(uu)).astype(BF16)
                duvz_ref[rows, 2 * cw + grp * gw:2 * cw + (grp + 1) * gw] = (
                    dyv * gu * mixed * _dsilu(zz)).astype(BF16)
                dmixed = dyv * gu * sz
                dm16 = dmixed.astype(BF16)
                dws_ref[grp] += lax.dot_general(dm16, vn_g, NT, preferred_element_type=F32)
                dbs_ref[grp] += jnp.broadcast_to(jnp.sum(dmixed, axis=1, keepdims=True), (CHUNK, LANES))
                dvn_ref[rows, cols] = lax.dot_general(ws_ref[grp], dm16, TN, preferred_element_type=F32)

        dvn = dvn_ref[...]
        _accumulate(dg_ref, jnp.sum(dvn * vhat, axis=0, keepdims=True))
        _accumulate(db_ref, jnp.sum(dvn, axis=0, keepdims=True))
        dvhat = dvn * gvec
        dgv = rstd * (dvhat - jnp.mean(dvhat, axis=-1, keepdims=True)
                      - vhat * jnp.mean(dvhat * vhat, axis=-1, keepdims=True))
        duvz_ref[:, cw:2 * cw] = (dgv * _dgelu(vv)).astype(BF16)

    full3 = lambda shape: pl.BlockSpec(shape, lambda i: (0, 0, 0))
    acc3 = jax.ShapeDtypeStruct((N_GROUPS, CHUNK, LANES), F32)
    vec = jax.ShapeDtypeStruct((1, cw), F32)
    row = pl.BlockSpec((tr, cw), lambda i: (i, 0))
    return _call(
        body, name="sgu_bwd", grid=(s // tr,),
        in_specs=[row, pl.BlockSpec((tr, cw), lambda i: (i, 1)), pl.BlockSpec((tr, cw), lambda i: (i, 2)),
                  row, _vec_spec(cw), _vec_spec(cw), full3(w_s.shape), full3(b_s.shape)],
        out_specs=[pl.BlockSpec((tr, 3 * cw), lambda i: (i, 0)), full3((N_GROUPS, CHUNK, LANES)),
                   full3((N_GROUPS, CHUNK, LANES)), _vec_spec(cw), _vec_spec(cw)],
        out_shape=[jax.ShapeDtypeStruct((s, 3 * cw), BF16), acc3, acc3, vec, vec],
        scratch_shapes=[pltpu.VMEM((tr, cw), F32)],
        operands=(uvz, uvz, uvz, dy, ln_g, ln_b, w_s, b_s), plan=plan)


def _flat_rows(a):
    return a.reshape(-1, a.shape[-1])


def _add_sibling(grad, recv, core_idx):
    nchip, k, n = grad.shape
    tr = _tile(k // 2, (256, 128))
    nb = (k // 2) // tr

    def body(c_ref, g_ref, r_ref, o_ref):
        o_ref[...] = (g_ref[...].astype(F32) + r_ref[...].astype(F32)).astype(BF16)

    return pl.pallas_call(
        body, name="add_sibling",
        grid_spec=pltpu.PrefetchScalarGridSpec(
            num_scalar_prefetch=1, grid=(nchip, nb),
            in_specs=[pl.BlockSpec((None, tr, n), lambda q, i, c: (q, c[0] * nb + i, 0)),
                      pl.BlockSpec((None, tr, n), lambda q, i, c: (q, i, 0))],
            out_specs=pl.BlockSpec((None, tr, n), lambda q, i, c: (q, i, 0))),
        out_shape=jax.ShapeDtypeStruct((nchip, k // 2, n), BF16),
        compiler_params=_params(2, parallel=2),
    )(core_idx, grad, recv)


def _sum_chips(own, others, reduced, layer, place_idx):
    _, kh, n = own.shape
    tr = _tile(kh, (256, 128))
    nb = kh // tr

    def body(place_ref, own_ref, oth_ref, red_ref, o_ref):
        acc = own_ref[...].astype(F32)
        for q in range(3):
            acc = acc + oth_ref[q].astype(F32)
        o_ref[...] = acc

    return pl.pallas_call(
        body, name="sum_chips",
        grid_spec=pltpu.PrefetchScalarGridSpec(
            num_scalar_prefetch=1, grid=(nb,),
            in_specs=[pl.BlockSpec((None, tr, n), lambda i, p: (p[0], i, 0)),
                      pl.BlockSpec((3, tr, n), lambda i, p: (0, i, 0)),
                      HBM_SPEC],
            out_specs=pl.BlockSpec((None, tr, n), lambda i, p: (layer, p[1] * nb + i, 0))),
        out_shape=jax.ShapeDtypeStruct(reduced.shape, reduced.dtype),
        input_output_aliases={3: 0},
        compiler_params=_params(1, parallel=1),
    )(place_idx, own, others, reduced)


def _sum_devices(parts, plan=None):
    nd, r, _ = parts.shape
    tr = _tile(r, (512, 256, 128, 64, 32, 16, 8))

    def body(p_ref, o_ref):
        acc = p_ref[0]
        for q in range(1, nd):
            acc = acc + p_ref[q]
        o_ref[...] = acc

    return _call(
        body, name="sum_devices", grid=(r // tr,),
        in_specs=[pl.BlockSpec((nd, tr, LANES), lambda i: (0, i, 0))],
        out_specs=pl.BlockSpec((tr, LANES), lambda i: (i, 0)),
        out_shape=jax.ShapeDtypeStruct((r, LANES), F32),
        operands=(parts,), parallel=1, plan=plan)


def _adamw(w, g, m, v, plan=None):
    r, n = w.shape
    tr = _tile(r, [p for p in (1024, 512, 256, 128, 64, 32, 16, 8) if p * n <= ELEMENTWISE_BLOCK])

    def body(w_ref, g_ref, m_ref, v_ref, d_ref, nm_ref, nv_ref):
        gv = g_ref[...]
        nm = ADAM_B1 * m_ref[...] + (1.0 - ADAM_B1) * gv
        nv = ADAM_B2 * v_ref[...] + (1.0 - ADAM_B2) * (gv * gv)
        m_hat = nm / (1.0 - ADAM_B1 ** ADAM_STEP)
        v_hat = nv / (1.0 - ADAM_B2 ** ADAM_STEP)
        d_ref[...] = -ADAM_LR * (m_hat / (jnp.sqrt(v_hat) + ADAM_EPS) + ADAM_WD * w_ref[...])
        nm_ref[...] = nm
        nv_ref[...] = nv

    spec = pl.BlockSpec((tr, n), lambda i: (i, 0))
    shp = jax.ShapeDtypeStruct((r, n), F32)
    return _call(
        body, name="adamw", grid=(r // tr,),
        in_specs=[spec] * 4, out_specs=[spec] * 3, out_shape=[shp] * 3,
        operands=(w, g, m, v), parallel=1, plan=plan)


def _pack(arrays, row_multiple=8):
    flat = [a.reshape(-1) for a in arrays]
    sizes = [f.shape[0] for f in flat]
    total = sum(sizes)
    unit = LANES * row_multiple
    padded = -(-total // unit) * unit
    if padded > total:
        flat.append(jnp.zeros((padded - total,), F32))
    offsets = [sum(sizes[:i]) for i in range(len(sizes))]
    return jnp.concatenate(flat).reshape(-1, LANES), offsets


def _unpack(packed, offsets, shapes):
    flat = packed.reshape(-1)
    return [flat[o:o + math.prod(s)].reshape(s) for o, s in zip(offsets, shapes)]


def kernel(x, c, ab_norm_g, ab_w_mod, ab_b_mod, ab_w_in, ab_conv_w, ab_w_out, sg_norm_g, sg_w_mod, sg_b_mod, sg_w_in, sg_ln_g, sg_ln_b, sg_w_s, sg_b_s, sg_w_out, final_norm_g, loss_target, m_ab_norm_g, m_ab_w_mod, m_ab_b_mod, m_ab_w_in, m_ab_conv_w, m_ab_w_out, m_sg_norm_g, m_sg_w_mod, m_sg_b_mod, m_sg_w_in, m_sg_ln_g, m_sg_ln_b, m_sg_w_s, m_sg_b_s, m_sg_w_out, m_final_norm_g, v_ab_norm_g, v_ab_w_mod, v_ab_b_mod, v_ab_w_in, v_ab_conv_w, v_ab_w_out, v_sg_norm_g, v_sg_w_mod, v_sg_b_mod, v_sg_w_in, v_sg_ln_g, v_sg_ln_b, v_sg_w_s, v_sg_b_s, v_sg_w_out, v_final_norm_g):
    s, d = x.shape[1], x.shape[2]
    aw = d // 2
    cw = d
    mod_l = ab_w_mod.shape[-1]
    x0 = x[0]
    target = loss_target[0]
    mx, my, mc = lax.axis_index("x"), lax.axis_index("y"), lax.axis_index("c")
    chip = 2 * mx + my
    chip_idx = jnp.reshape(chip, (1,)).astype(jnp.int32)
    core_idx = jnp.reshape(mc, (1,)).astype(jnp.int32)
    place_idx = jnp.stack([chip, mc]).astype(jnp.int32)

    win = [_place_own_shard(ab_w_in if L % 2 == 0 else sg_w_in, L // 2, chip_idx) for L in range(4)]
    wout = [_place_own_shard(ab_w_out if L % 2 == 0 else sg_w_out, L // 2, chip_idx) for L in range(4)]
    k_in, k_out = d, wout[0].shape[1]

    def gather_plan(ici=(), pass_on=()):
        arrays, copies = [], []
        for stage, make in ((ici, _gather_ici), (pass_on, _gather_pass_on)):
            for kind, L in stage:
                arr = win[L] if kind == "in" else wout[L]
                arrays.append(arr)
                copies += make(len(arrays) - 1, k_in if kind == "in" else k_out)
        return _Plan(tuple(arrays), tuple(copies)), [(kind, L) for kind, L in tuple(ici) + tuple(pass_on)]

    def absorb(plan_and_names, updated):
        _, names = plan_and_names
        for pos, (kind, L) in enumerate(names):
            if kind == "in":
                win[L] = updated[pos]
            else:
                wout[L] = updated[pos]

    win[0] = _comm_stages("gather_first_w_in", [win[0]], [_gather_ici(0, k_in), _gather_pass_on(0, k_in)])[0]

    small_local = [c[0], ab_conv_w, sg_norm_g, sg_ln_g, sg_ln_b]
    small_shapes = [a.shape for a in small_local]
    payload, small_off = _pack(small_local)
    gathered = _all_to_all(jnp.broadcast_to(payload[None], (N_DEV,) + payload.shape), "gather_small")
    per_dev = [_unpack(gathered[b], small_off, small_shapes) for b in range(N_DEV)]
    c_all = jnp.stack([per_dev[b][0] for b in range(N_DEV)])

    def from_chips(idx, axis):
        return jnp.concatenate([per_dev[2 * q][idx] for q in range(N_CHIPS)], axis=axis)

    conv_w_full = from_chips(1, 2)
    sg_norm_g_full = from_chips(2, 1)
    sg_ln_g_full = from_chips(3, 1)
    sg_ln_b_full = from_chips(4, 1)

    ab_b_local = lax.dynamic_slice_in_dim(ab_b_mod, chip * mod_l, mod_l, axis=1)
    mod_rows = []
    for layer in range(4):
        i = layer // 2
        w_mod, bias = (ab_w_mod, ab_b_local) if layer % 2 == 0 else (sg_w_mod, sg_b_mod)
        mod_rows.append(_mod_fwd(c_all, w_mod, bias[i:i + 1], i))
    mod_local = jnp.stack(mod_rows, axis=1)
    mod_recv = _all_to_all(mod_local.reshape(N_DEV, -1, LANES), "exchange_mod")
    mod_recv = mod_recv.reshape(N_DEV, 4, mod_l)
    mod_full = jnp.concatenate([mod_recv[2 * q] for q in range(N_CHIPS)], axis=-1)
    shifts = [mod_full[l:l + 1, :d] for l in range(4)]
    scales = [mod_full[l:l + 1, d:2 * d] for l in range(4)]
    gates = [mod_full[l:l + 1, 2 * d:] for l in range(4)]

    cos, sin = _rope_tables(s)
    w_s16 = sg_w_s.astype(BF16)
    b_s3 = sg_b_s[..., None]

    fwd_comm = {
        ("in_proj", 0): ([("in", 1), ("out", 0)], []),
        ("attn", 0): ([("out", 1), ("in", 2)], [("out", 0)]),
        ("out_proj", 0): ([], [("in", 1), ("out", 1)]),
        ("in_proj", 1): ([("out", 2)], [("in", 2)]),
        ("out_proj", 1): ([], [("out", 2)]),
        ("in_proj", 2): ([("in", 3)], []),
        ("attn", 2): ([("out", 3)], []),
        ("out_proj", 2): ([], [("in", 3), ("out", 3)]),
    }

    def carried(key, fn, *args):
        if key not in fwd_comm:
            return fn(*args)
        pn = gather_plan(*fwd_comm[key])
        res, updated = fn(*args, plan=pn[0])
        absorb(pn, updated)
        return res

    saved = []
    xs = x0
    for layer in range(4):
        i = layer // 2
        if layer % 2 == 0:
            h = _prenorm(xs, ab_norm_g[i:i + 1], scales[layer], shifts[layer])
            proj = carried(("in_proj", layer), _in_proj, h, win[layer])
            attn, lse = carried(("attn", layer), _attn_fwd, proj, cos, sin, aw)
            y = _ab_mix(attn, proj, conv_w_full[i], aw)
            x_next, out = carried(("out_proj", layer), _out_proj_residual, y, wout[layer].reshape(-1, d), xs,
                                  gates[layer])
            saved.append((xs, h, proj, y, out, attn, lse))
        else:
            h = _prenorm(xs, sg_norm_g_full[i:i + 1], scales[layer], shifts[layer])
            uvz = carried(("in_proj", layer), _in_proj, h, win[layer])
            y = _sgu_fwd(uvz, sg_ln_g_full[i:i + 1], sg_ln_b_full[i:i + 1], w_s16[i], b_s3[i], cw)
            x_next, out = carried(("out_proj", layer), _out_proj_residual, y, wout[layer].reshape(-1, d), xs,
                                  gates[layer])
            saved.append((xs, h, uvz, y, out))
        xs = x_next

    loss11, dx, d_final_g, dout, dgate = _final_loss(xs, target, final_norm_g[None], saved[3][4], gates[3])
    loss = lax.psum(loss11[0, 0], ("x", "y", "c"))

    reduced = {"in": [lax.empty((2,) + w.shape[1:], F32) for w in (ab_w_in, sg_w_in)],
               "out": [lax.empty((2,) + w.shape[1:], F32) for w in (ab_w_out, sg_w_out)]}
    grads = {}
    stage = {}
    k_of = {"in": k_in, "out": k_out}

    def swap_plan(L):
        arrays, copies = [], []
        for kind in ("in", "out"):
            g = grads[kind, L]
            arrays += [g, jax.ShapeDtypeStruct((N_CHIPS, g.shape[1] // 2, g.shape[2]), BF16)]
            copies += _reduce_swap(len(arrays) - 2, len(arrays) - 1, k_of[kind])
        return _Plan(tuple(arrays), tuple(copies))

    def after_swap(L, updated):
        for pos, kind in enumerate(("in", "out")):
            stage[kind, L] = _add_sibling(grads[kind, L], updated[2 * pos + 1], core_idx)

    def ici_plan(L, pieces):
        arrays, copies = [], []
        for kind, only in pieces:
            cs = stage[kind, L]
            arrays += [cs, stage.get((kind, L, "recv"), jax.ShapeDtypeStruct((3,) + cs.shape[1:], BF16))]
            copies += _reduce_ici(len(arrays) - 2, len(arrays) - 1, only)
        return _Plan(tuple(arrays), tuple(copies))

    def after_ici(L, pieces, updated):
        for pos, (kind, _) in enumerate(pieces):
            stage[kind, L, "recv"] = updated[2 * pos + 1]

    def sum_layer(L):
        for kind in ("in", "out"):
            reduced[kind][L % 2] = _sum_chips(stage[kind, L], stage[kind, L, "recv"], reduced[kind][L % 2],
                                              L // 2, place_idx)

    def share_plan(L):
        arrays = (reduced["in"][L % 2], reduced["out"][L % 2])
        copies = _reduce_share(0, L // 2, k_in) + _reduce_share(1, L // 2, k_out)
        return _Plan(arrays, tuple(copies))

    def after_share(L, updated):
        reduced["in"][L % 2], reduced["out"][L % 2] = updated[0], updated[1]

    all_chips = (0, 1, 2)
    dmods = [None] * 4
    d_ab_norm_g, d_sg_norm_g = [None, None], [None, None]
    d_conv_w, d_ln_g, d_ln_b, d_w_s, d_b_s = ([None, None] for _ in range(5))
    for layer in reversed(range(4)):
        i = layer // 2
        prev = layer + 1
        busy = prev < 4
        if layer % 2 == 0:
            xs, h, proj, y, out, attn, lse = saved[layer]
        else:
            xs, h, uvz, y, out = saved[layer]
        below = (saved[layer - 1][4], gates[layer - 1]) if layer > 0 else None
        w2 = wout[layer].reshape(-1, d)
        if busy:
            dy, updated = _out_proj_bwd_act(dout, w2, plan=swap_plan(prev))
            after_swap(prev, updated)
        else:
            dy = _out_proj_bwd_act(dout, w2)
        grads["out", layer] = _out_proj_bwd_w(y, dout).reshape(N_CHIPS, -1, d)
        if layer % 2 == 0:
            if busy:
                pieces = [("in", all_chips), ("out", all_chips)]
                dqkv, updated = _attn_bwd(proj, cos, sin, dy, attn, lse, aw, plan=ici_plan(prev, pieces))
                after_ici(prev, pieces, updated)
                sum_layer(prev)
            else:
                dqkv = _attn_bwd(proj, cos, sin, dy, attn, lse, aw)
            dact, d_conv_w[i] = _ab_dproj(dqkv, dy, attn, proj, conv_w_full[i], aw)
            if layer == 0:
                grads["in", 0], updated = _in_proj_bwd_w(h, dact, win[0].shape[-1], plan=share_plan(prev))
                after_share(prev, updated)
                plan = swap_plan(0)
                after_swap(0, _comm_stages("grads_to_sibling", plan.arrays, [plan.copies]))
                pieces = [("in", all_chips), ("out", all_chips)]
                dh, updated = _in_proj_bwd_act(dact, win[0], plan=ici_plan(0, pieces))
                after_ici(0, pieces, updated)
            elif busy:
                dh, updated = _in_proj_bwd_act(dact, win[layer], plan=share_plan(prev))
                after_share(prev, updated)
                grads["in", layer] = _in_proj_bwd_w(h, dact, win[layer].shape[-1])
            else:
                dh = _in_proj_bwd_act(dact, win[layer])
                grads["in", layer] = _in_proj_bwd_w(h, dact, win[layer].shape[-1])
            norm_g = ab_norm_g[i:i + 1]
        else:
            sgu_args = (uvz, dy, sg_ln_g_full[i:i + 1], sg_ln_b_full[i:i + 1], w_s16[i], b_s3[i], cw)
            if busy:
                pieces = [("in", (0, 1)), ("out", all_chips)]
                res, updated = _sgu_bwd(*sgu_args, plan=ici_plan(prev, pieces))
                after_ici(prev, pieces, updated)
            else:
                res = _sgu_bwd(*sgu_args)
            dact, d_w_s[i], db_wide, d_ln_g[i], d_ln_b[i] = res
            d_b_s[i] = db_wide[:, :, 0]
            if busy:
                pieces = [("in", (2,))]
                dh, updated = _in_proj_bwd_act(dact, win[layer], plan=ici_plan(prev, pieces))
                after_ici(prev, pieces, updated)
                sum_layer(prev)
                grads["in", layer], updated = _in_proj_bwd_w(h, dact, win[layer].shape[-1], plan=share_plan(prev))
                after_share(prev, updated)
            else:
                dh = _in_proj_bwd_act(dact, win[layer])
                grads["in", layer] = _in_proj_bwd_w(h, dact, win[layer].shape[-1])
            norm_g = sg_norm_g_full[i:i + 1]
        res = _prenorm_bwd(xs, dh, dx, norm_g, scales[layer], below)
        dx, dshift, dscale, d_norm_g = res[:4]
        (d_ab_norm_g if layer % 2 == 0 else d_sg_norm_g)[i] = d_norm_g
        dmods[layer] = jnp.concatenate([dshift, dscale, dgate], axis=1)
        if below:
            dout, dgate = res[4:]
    grad_x = dx[None]

    partial_list = [jnp.concatenate(dmods, axis=0),
                    jnp.concatenate(d_ab_norm_g, axis=0), jnp.concatenate(d_sg_norm_g, axis=0), d_final_g[0],
                    jnp.stack(d_conv_w), jnp.concatenate(d_ln_g, axis=0), jnp.concatenate(d_ln_b, axis=0),
                    jnp.stack(d_w_s), jnp.stack(d_b_s)]
    partial_shapes = [a.shape for a in partial_list]
    partials, part_off = _pack(partial_list)
    all_partials = _all_to_all(jnp.broadcast_to(partials[None], (N_DEV,) + partials.shape), "gather_partials")
    sum_layer(0)
    summed_packed, updated = _sum_devices(all_partials, plan=share_plan(0))
    after_share(0, updated)
    summed = _unpack(summed_packed, part_off, partial_shapes)
    (g_mod_bias, g_ab_norm_g, g_sg_norm_g_full, g_final_g, g_conv_full, g_ln_g_full, g_ln_b_full,
     g_w_s, g_b_s) = summed
    dm_all = jnp.stack([_unpack(all_partials[b], part_off[:1], partial_shapes[:1])[0] for b in range(N_DEV)])
    dm_local = lax.dynamic_slice_in_dim(dm_all, chip * mod_l, mod_l, axis=2)

    def chip_cols(a, axis):
        width = a.shape[axis] // N_CHIPS
        return lax.dynamic_slice_in_dim(a, chip * width, width, axis=axis)

    g_ab_b_mod = jnp.stack([g_mod_bias[0], g_mod_bias[2]])
    g_sg_b_mod = chip_cols(jnp.stack([g_mod_bias[1], g_mod_bias[3]]), 1)
    g_ab_w_mod = jnp.stack([_mod_bwd_w(c_all, dm_local[:, 0]), _mod_bwd_w(c_all, dm_local[:, 2])])
    g_sg_w_mod = jnp.stack([_mod_bwd_w(c_all, dm_local[:, 1]), _mod_bwd_w(c_all, dm_local[:, 3])])
    g_conv = chip_cols(g_conv_full, 2)
    g_sg_norm_g = chip_cols(g_sg_norm_g_full, 1)
    g_ln_g = chip_cols(g_ln_g_full, 1)
    g_ln_b = chip_cols(g_ln_b_full, 1)

    def step_big(w, g, m, v):
        dl, nm, nv = _adamw(_flat_rows(w), _flat_rows(g), _flat_rows(m), _flat_rows(v))
        return dl.reshape(w.shape), nm.reshape(w.shape), nv.reshape(w.shape)

    g_ab_w_in, g_sg_w_in = reduced["in"]
    g_ab_w_out, g_sg_w_out = reduced["out"]
    big_out = {
        "ab_w_mod": step_big(ab_w_mod, g_ab_w_mod, m_ab_w_mod, v_ab_w_mod),
        "ab_w_in": step_big(ab_w_in, g_ab_w_in, m_ab_w_in, v_ab_w_in),
        "ab_w_out": step_big(ab_w_out, g_ab_w_out, m_ab_w_out, v_ab_w_out),
        "sg_w_mod": step_big(sg_w_mod, g_sg_w_mod, m_sg_w_mod, v_sg_w_mod),
        "sg_w_in": step_big(sg_w_in, g_sg_w_in, m_sg_w_in, v_sg_w_in),
        "sg_w_out": step_big(sg_w_out, g_sg_w_out, m_sg_w_out, v_sg_w_out),
    }
    small_names = ["ab_norm_g", "ab_b_mod", "ab_conv_w", "sg_norm_g", "sg_b_mod", "sg_ln_g", "sg_ln_b",
                   "sg_w_s", "sg_b_s", "final_norm_g"]
    small_w = [ab_norm_g, ab_b_mod, ab_conv_w, sg_norm_g, sg_b_mod, sg_ln_g, sg_ln_b, sg_w_s, sg_b_s, final_norm_g]
    small_g = [g_ab_norm_g, g_ab_b_mod, g_conv, g_sg_norm_g, g_sg_b_mod, g_ln_g, g_ln_b, g_w_s, g_b_s, g_final_g]
    small_m = [m_ab_norm_g, m_ab_b_mod, m_ab_conv_w, m_sg_norm_g, m_sg_b_mod, m_sg_ln_g, m_sg_ln_b, m_sg_w_s,
               m_sg_b_s, m_final_norm_g]
    small_v = [v_ab_norm_g, v_ab_b_mod, v_ab_conv_w, v_sg_norm_g, v_sg_b_mod, v_sg_ln_g, v_sg_ln_b, v_sg_w_s,
               v_sg_b_s, v_final_norm_g]
    shapes = [a.shape for a in small_w]
    pw, off = _pack(small_w)
    pg, _ = _pack(small_g)
    pm, _ = _pack(small_m)
    pv, _ = _pack(small_v)
    pd, pnm, pnv = _adamw(pw, pg, pm, pv)
    small_out = {}
    for name, dl, nm, nv in zip(small_names, _unpack(pd, off, shapes), _unpack(pnm, off, shapes),
                                _unpack(pnv, off, shapes)):
        small_out[name] = (dl, nm, nv)

    grad_of = {
        "ab_norm_g": g_ab_norm_g, "ab_w_mod": g_ab_w_mod, "ab_b_mod": g_ab_b_mod, "ab_w_in": g_ab_w_in,
        "ab_conv_w": g_conv, "ab_w_out": g_ab_w_out, "sg_norm_g": g_sg_norm_g, "sg_w_mod": g_sg_w_mod,
        "sg_b_mod": g_sg_b_mod, "sg_w_in": g_sg_w_in, "sg_ln_g": g_ln_g, "sg_ln_b": g_ln_b, "sg_w_s": g_w_s,
        "sg_b_s": g_b_s, "sg_w_out": g_sg_w_out, "final_norm_g": g_final_g,
    }
    order = ["ab_norm_g", "ab_w_mod", "ab_b_mod", "ab_w_in", "ab_conv_w", "ab_w_out", "sg_norm_g", "sg_w_mod",
             "sg_b_mod", "sg_w_in", "sg_ln_g", "sg_ln_b", "sg_w_s", "sg_b_s", "sg_w_out", "final_norm_g"]
    steps = {**big_out, **small_out}
    return (loss, grad_x, *[grad_of[n] for n in order], *[steps[n][0] for n in order],
            *[steps[n][1] for n in order], *[steps[n][2] for n in order])
```

```python
import math
from typing import Any, Callable, NamedTuple

import jax
import jax.numpy as jnp
import numpy as np
from jax import lax
from jax.experimental import pallas as pl
from jax.experimental.pallas import tpu as pltpu

F32 = jnp.float32
BF16 = jnp.bfloat16

HEAD_DIM = 128
RADIUS = 64
DILATIONS = (1, 4, 16)
Q_BLOCK = 256
K_WINDOW = Q_BLOCK + 2 * RADIUS
ROPE_THETA = 10000.0
NEG_INF = -1e30
N_GROUPS = 8
CHUNK = 128
EPS = 1e-6
CONV_ROWS = 512
CONV_HALO = 16
LANES = 128
ELEMENTWISE_BLOCK = 512 * 1024
PACKED_ROW_BLOCK = 512
N_DEV = 8
N_CHIPS = 4

ADAM_LR = 0.001
ADAM_B1 = 0.9
ADAM_B2 = 0.999
ADAM_EPS = 1e-08
ADAM_WD = 0.01
ADAM_STEP = 10

VMEM_LIMIT_V7X = 56 * 1024 * 1024

MESH_ID = pl.DeviceIdType.MESH
HBM_SPEC = pl.BlockSpec(memory_space=pltpu.HBM)

NN = (((1,), (0,)), ((), ()))
NT = (((1,), (1,)), ((), ()))
TN = (((0,), (0,)), ((), ()))


def _params(n_grid, parallel=0):
    sem = tuple(["parallel"] * parallel + ["arbitrary"] * (n_grid - parallel))
    return pltpu.CompilerParams(dimension_semantics=sem, vmem_limit_bytes=VMEM_LIMIT_V7X)


def _tile(n, prefs):
    for p in prefs:
        if n % p == 0:
            return p
    return n


def _sigmoid(z):
    return 1.0 / (1.0 + jnp.exp(-z))


def _silu(z):
    return z * _sigmoid(z)


def _dsilu(z):
    s = _sigmoid(z)
    return s * (1.0 + z * (1.0 - s))


_GELU_K = math.sqrt(2.0 / math.pi)
_GELU_C = 0.044715


def _gelu(u):
    return 0.5 * u * (1.0 + jnp.tanh(_GELU_K * (u + _GELU_C * u * u * u)))


def _dgelu(u):
    t = jnp.tanh(_GELU_K * (u + _GELU_C * u * u * u))
    return 0.5 * (1.0 + t) + 0.5 * u * (1.0 - t * t) * _GELU_K * (1.0 + 3.0 * _GELU_C * u * u)


class _Place(NamedTuple):
    x: Any
    y: Any
    c: Any
    chip: Any


def _my_place():
    mx, my, mc = lax.axis_index("x"), lax.axis_index("y"), lax.axis_index("c")
    return _Place(mx, my, mc, 2 * mx + my)


def _other_chips(p):
    return [(1 - p.x, p.y), (p.x, 1 - p.y), (1 - p.x, 1 - p.y)]


class _Copy(NamedTuple):
    src: int
    src_at: Callable
    dst: int
    dst_at: Callable
    peer: Callable


class _Plan(NamedTuple):
    arrays: tuple
    copies: tuple


def _view(ref, index):
    return ref if index is None else ref.at[index]


def _plan_io(plan):
    ins = [k for k, a in enumerate(plan.arrays) if not isinstance(a, jax.ShapeDtypeStruct)]
    written = sorted({cp.dst for cp in plan.copies})
    return ins, written


def _descriptors(plan, in_refs, out_refs, send_sems, recv_sems):
    ins, written = _plan_io(plan)
    place = _my_place()
    return [
        pltpu.make_async_remote_copy(
            src_ref=_view(in_refs[ins.index(cp.src)], cp.src_at(place)),
            dst_ref=_view(out_refs[written.index(cp.dst)], cp.dst_at(place)),
            send_sem=send_sems.at[k], recv_sem=recv_sems.at[k],
            device_id=cp.peer(place), device_id_type=MESH_ID)
        for k, cp in enumerate(plan.copies)]


def _plan_operands(plan, n_in, n_out):
    ins, written = _plan_io(plan)
    operands = [plan.arrays[k] for k in ins]
    out_shape = [jax.ShapeDtypeStruct(plan.arrays[k].shape, plan.arrays[k].dtype) for k in written]
    aliases = {n_in + ins.index(k): n_out + pos for pos, k in enumerate(written) if k in ins}
    n = len(plan.copies)
    sems = [pltpu.SemaphoreType.DMA((n,)), pltpu.SemaphoreType.DMA((n,))]
    return operands, out_shape, aliases, sems, written


def _call(body, *, name, grid, in_specs, out_specs, out_shape, operands, scratch_shapes=(), aliases=None,
          parallel=0, plan=None):
    single = not isinstance(out_shape, (list, tuple))
    out_shape = [out_shape] if single else list(out_shape)
    out_specs = [out_specs] if single else list(out_specs)
    if plan is None:
        res = pl.pallas_call(
            body, name=name, grid=grid, in_specs=list(in_specs), out_specs=out_specs, out_shape=out_shape,
            scratch_shapes=list(scratch_shapes), input_output_aliases=aliases or {},
            compiler_params=_params(len(grid), parallel=parallel),
        )(*operands)
        return res[0] if single else res

    n_in, n_out, n_scr = len(operands), len(out_shape), len(scratch_shapes)
    p_operands, p_out_shape, p_aliases, sems, written = _plan_operands(plan, n_in, n_out)
    n_pin, n_pout = len(p_operands), len(p_out_shape)

    def wrapped(*refs):
        ins = refs[:n_in]
        p_in = refs[n_in:n_in + n_pin]
        outs = refs[n_in + n_pin:n_in + n_pin + n_out]
        p_out = refs[n_in + n_pin + n_out:n_in + n_pin + n_out + n_pout]
        scratch = refs[n_in + n_pin + n_out + n_pout:n_in + n_pin + n_out + n_pout + n_scr]
        send_sems, recv_sems = refs[-2:]
        ids = [pl.program_id(a) for a in range(len(grid))]
        first = ids[0] == 0
        last = ids[0] == grid[0] - 1
        for a in range(1, len(grid)):
            first = jnp.logical_and(first, ids[a] == 0)
            last = jnp.logical_and(last, ids[a] == grid[a] - 1)

        @pl.when(first)
        def _():
            for cp in _descriptors(plan, p_in, p_out, send_sems, recv_sems):
                cp.start()

        body(*ins, *outs, *scratch)

        @pl.when(last)
        def _():
            for cp in _descriptors(plan, p_in, p_out, send_sems, recv_sems):
                cp.wait()

    res = pl.pallas_call(
        wrapped, name=name, grid=grid,
        in_specs=list(in_specs) + [HBM_SPEC] * n_pin,
        out_specs=out_specs + [HBM_SPEC] * n_pout,
        out_shape=out_shape + p_out_shape,
        scratch_shapes=list(scratch_shapes) + sems,
        input_output_aliases={**(aliases or {}), **p_aliases},
        compiler_params=_params(len(grid)),
    )(*operands, *p_operands)
    outs = res[0] if single else res[:n_out]
    return outs, dict(zip(written, res[n_out:]))


def _comm_stages(name, arrays, stages):
    plan = _Plan(tuple(arrays), tuple(cp for st in stages for cp in st))
    p_operands, p_out_shape, p_aliases, sems, written = _plan_operands(plan, 0, 0)
    n_pin = len(p_operands)

    def body(*refs):
        p_in = refs[:n_pin]
        p_out = refs[n_pin:n_pin + len(written)]
        send_sems, recv_sems = refs[-2:]
        all_copies = _descriptors(plan, p_in, p_out, send_sems, recv_sems)
        base = 0
        for st in stages:
            for cp in all_copies[base:base + len(st)]:
                cp.start()
            for cp in all_copies[base:base + len(st)]:
                cp.wait()
            base += len(st)

    res = pl.pallas_call(
        body, name=name, in_specs=[HBM_SPEC] * n_pin, out_specs=[HBM_SPEC] * len(written),
        out_shape=p_out_shape, scratch_shapes=sems, input_output_aliases=p_aliases,
    )(*p_operands)
    return dict(zip(written, res))


def _half_rows(k, c):
    return pl.ds(c * (k // 2), k // 2)


def _gather_ici(a, k):
    own = lambda p: (p.chip, _half_rows(k, p.c))
    return [_Copy(a, own, a, own, lambda p, q=q: (*_other_chips(p)[q], p.c)) for q in range(3)]


def _gather_pass_on(a, k):
    def at(q):
        def index(p):
            px, py = _other_chips(p)[q]
            return (2 * px + py, _half_rows(k, p.c))
        return index
    return [_Copy(a, at(q), a, at(q), lambda p: (p.x, p.y, 1 - p.c)) for q in range(3)]


def _reduce_swap(src, dst, k):
    return [_Copy(src, lambda p: (pl.ds(0, N_CHIPS), _half_rows(k, 1 - p.c)), dst, lambda p: None,
                  lambda p: (p.x, p.y, 1 - p.c))]


def _reduce_ici(src, dst, only=(0, 1, 2)):
    def slab(q):
        def index(p):
            px, py = _other_chips(p)[q]
            return 2 * px + py
        return index
    return [_Copy(src, slab(q), dst, lambda p, q=q: q, lambda p, q=q: (*_other_chips(p)[q], p.c)) for q in only]


def _reduce_share(a, layer, k):
    at = lambda p: (layer, _half_rows(k, p.c))
    return [_Copy(a, at, a, at, lambda p: (p.x, p.y, 1 - p.c))]


def _all_to_all(x, name):
    def body(x_ref, y_ref, send_sems, recv_sems, own_sem):
        p = _my_place()
        me = 2 * p.chip + p.c
        own = pltpu.make_async_copy(x_ref.at[me], y_ref.at[me], own_sem)
        own.start()
        copies = []
        for k in range(1, N_DEV):
            px = 1 - p.x if (k >> 2) & 1 else p.x
            py = 1 - p.y if (k >> 1) & 1 else p.y
            pc = 1 - p.c if k & 1 else p.c
            peer = 4 * px + 2 * py + pc
            cp = pltpu.make_async_remote_copy(
                src_ref=x_ref.at[peer], dst_ref=y_ref.at[me],
                send_sem=send_sems.at[k - 1], recv_sem=recv_sems.at[k - 1],
                device_id=(px, py, pc), device_id_type=MESH_ID)
            cp.start()
            copies.append(cp)
        for cp in copies:
            cp.wait()
        own.wait()

    return pl.pallas_call(
        body, name=name,
        out_shape=jax.ShapeDtypeStruct(x.shape, x.dtype),
        in_specs=[HBM_SPEC], out_specs=HBM_SPEC,
        scratch_shapes=[pltpu.SemaphoreType.DMA((N_DEV - 1,)), pltpu.SemaphoreType.DMA((N_DEV - 1,)),
                        pltpu.SemaphoreType.DMA],
    )(x)


def _place_own_shard(w, layer, chip_idx):
    _, k, n = w.shape
    tr = _tile(k, (512, 256, 128))

    def body(c_ref, w_ref, g_ref):
        g_ref[...] = w_ref[...].astype(BF16)

    return pl.pallas_call(
        body, name="place_own_shard",
        grid_spec=pltpu.PrefetchScalarGridSpec(
            num_scalar_prefetch=1, grid=(k // tr,),
            in_specs=[pl.BlockSpec((None, tr, n), lambda r, c: (layer, r, 0))],
            out_specs=pl.BlockSpec((None, tr, n), lambda r, c: (c[0], r, 0))),
        out_shape=jax.ShapeDtypeStruct((N_CHIPS, k, n), BF16),
        compiler_params=_params(1, parallel=1),
    )(chip_idx, w)


def _matmul(name, operands, in_specs, grid, dims, out_shape, out_specs, epilogue, a_prologue=None,
            aliases=None, plan=None):
    n_in = len(operands)

    def body(*refs):
        a = refs[0][...]
        if a_prologue is not None:
            a = a_prologue(a)
        acc = lax.dot_general(a.astype(BF16), refs[1][...].astype(BF16), dims, preferred_element_type=F32)
        epilogue(acc, refs[2:n_in], refs[n_in:])

    return _call(body, name=name, grid=grid, in_specs=in_specs, out_specs=out_specs, out_shape=out_shape,
                 operands=operands, aliases=aliases, parallel=2, plan=plan)


def _store_cast(acc, extra, outs):
    outs[0][...] = acc.astype(outs[0].dtype)


def _in_proj(h, w, plan=None):
    s, d = h.shape
    nl = w.shape[-1]
    tm = _tile(s, (1024, 512, 256))
    tn = _tile(nl, (1024, 768, 512, 384, 256, 128))
    per = nl // tn
    return _matmul(
        "in_proj", (h, w),
        [pl.BlockSpec((tm, d), lambda i, j: (i, 0)),
         pl.BlockSpec((None, d, tn), lambda i, j: (j // per, 0, j % per))],
        (s // tm, N_CHIPS * per), NN,
        jax.ShapeDtypeStruct((s, N_CHIPS * nl), BF16),
        pl.BlockSpec((tm, tn), lambda i, j: (i, j)), _store_cast, plan=plan)


def _out_proj_residual(y, w2, x, gate, plan=None):
    s, wdt = y.shape
    d = w2.shape[-1]
    tm = _tile(s, (1024, 512, 256))
    tn = _tile(d, (1024, 512, 256, 128))

    def epilogue(acc, extra, outs):
        x_ref, gate_ref = extra
        outs[0][...] = x_ref[...] + gate_ref[...] * acc
        outs[1][...] = acc.astype(BF16)

    blk = pl.BlockSpec((tm, tn), lambda i, j: (i, j))
    return _matmul(
        "out_proj", (y, w2, x, gate),
        [pl.BlockSpec((tm, wdt), lambda i, j: (i, 0)),
         pl.BlockSpec((wdt, tn), lambda i, j: (0, j)),
         blk, pl.BlockSpec((1, tn), lambda i, j: (0, j))],
        (s // tm, d // tn), NN,
        [jax.ShapeDtypeStruct((s, d), F32), jax.ShapeDtypeStruct((s, d), BF16)],
        [blk, blk], epilogue, plan=plan)


def _out_proj_bwd_act(dout, w2, plan=None):
    s, d = dout.shape
    wdt = w2.shape[0]
    tm = _tile(s, (1024, 512, 256))
    tn = _tile(wdt, (1024, 512, 256, 128))
    return _matmul(
        "out_proj_dy", (dout, w2),
        [pl.BlockSpec((tm, d), lambda i, j: (i, 0)),
         pl.BlockSpec((tn, d), lambda i, j: (j, 0))],
        (s // tm, wdt // tn), NT,
        jax.ShapeDtypeStruct((s, wdt), BF16),
        pl.BlockSpec((tm, tn), lambda i, j: (i, j)), _store_cast, plan=plan)


def _out_proj_bwd_w(y, dout, plan=None):
    s, wdt = y.shape
    d = dout.shape[1]
    tm = _tile(wdt, (1024, 512, 256, 128))
    tn = _tile(d, (1024, 512, 256, 128))
    return _matmul(
        "out_proj_dw", (y, dout),
        [pl.BlockSpec((s, tm), lambda i, j: (0, i)),
         pl.BlockSpec((s, tn), lambda i, j: (0, j))],
        (wdt // tm, d // tn), TN,
        jax.ShapeDtypeStruct((wdt, d), BF16),
        pl.BlockSpec((tm, tn), lambda i, j: (i, j)), _store_cast, plan=plan)


def _in_proj_bwd_act(dproj, w, plan=None):
    s, n_all = dproj.shape
    d, nl = w.shape[1], w.shape[2]
    tm = _tile(s, (1024, 512, 256))
    tn = _tile(d, (512, 256, 128))

    def body(a_ref, w_ref, o_ref):
        acc = None
        for q in range(N_CHIPS):
            part = lax.dot_general(a_ref[:, q * nl:(q + 1) * nl], w_ref[q], NT, preferred_element_type=F32)
            acc = part if acc is None else acc + part
        o_ref[...] = acc.astype(BF16)

    return _call(
        body, name="in_proj_dh", grid=(s // tm, d // tn),
        in_specs=[pl.BlockSpec((tm, n_all), lambda i, j: (i, 0), pipeline_mode=pl.Buffered(1)),
                  pl.BlockSpec((N_CHIPS, tn, nl), lambda i, j: (0, j, 0))],
        out_specs=pl.BlockSpec((tm, tn), lambda i, j: (i, j)),
        out_shape=jax.ShapeDtypeStruct((s, d), BF16),
        operands=(dproj, w), parallel=2, plan=plan)


def _in_proj_bwd_w(h, dproj, nl, plan=None):
    s, d = h.shape
    tm = _tile(d, (1024, 512, 256, 128))
    tn = _tile(nl, (1024, 768, 512, 384, 256, 128))
    per = nl // tn
    return _matmul(
        "in_proj_dw", (h, dproj),
        [pl.BlockSpec((s, tm), lambda i, j: (0, i)),
         pl.BlockSpec((s, tn), lambda i, j: (0, j))],
        (d // tm, N_CHIPS * per), TN,
        jax.ShapeDtypeStruct((N_CHIPS, d, nl), BF16),
        pl.BlockSpec((None, tm, tn), lambda i, j: (j // per, i, j % per)), _store_cast, plan=plan)


def _mod_fwd(c_all, w_mod, bias, layer):
    nb, d = c_all.shape
    nl = w_mod.shape[-1]
    tn = _tile(nl, (768, 512, 384, 256, 128))

    def epilogue(acc, extra, outs):
        outs[0][...] = acc + extra[0][...]

    return _matmul(
        "mod_fwd", (c_all, w_mod, bias),
        [pl.BlockSpec((nb, d), lambda i, j: (0, 0)),
         pl.BlockSpec((None, d, tn), lambda i, j: (layer, 0, j)),
         pl.BlockSpec((1, tn), lambda i, j: (0, j))],
        (1, nl // tn), NN,
        jax.ShapeDtypeStruct((nb, nl), F32),
        pl.BlockSpec((nb, tn), lambda i, j: (0, j)), epilogue, a_prologue=_silu)


def _mod_bwd_w(c_all, dm_local):
    nb, d = c_all.shape
    nl = dm_local.shape[-1]
    tm = _tile(d, (1024, 512, 256, 128))
    tn = _tile(nl, (768, 512, 384, 256, 128))

    def epilogue(acc, extra, outs):
        outs[0][...] = acc

    return _matmul(
        "mod_dw", (c_all, dm_local),
        [pl.BlockSpec((nb, tm), lambda i, j: (0, i)),
         pl.BlockSpec((nb, tn), lambda i, j: (0, j))],
        (d // tm, nl // tn), TN,
        jax.ShapeDtypeStruct((d, nl), F32),
        pl.BlockSpec((tm, tn), lambda i, j: (i, j)), epilogue, a_prologue=_silu)


def _rows_call(name, body, operands, in_specs, out_shape, out_specs, n_tiles):
    return pl.pallas_call(
        body, name=name, grid=(n_tiles,), in_specs=in_specs, out_specs=out_specs, out_shape=out_shape,
        compiler_params=_params(1),
    )(*operands)


def _row_spec(tr, width):
    return pl.BlockSpec((tr, width), lambda i: (i, 0))


def _vec_spec(width):
    return pl.BlockSpec((1, width), lambda i: (0, 0))


def _accumulate(ref, val):
    first = pl.program_id(0) == 0

    @pl.when(first)
    def _():
        ref[...] = val

    @pl.when(jnp.logical_not(first))
    def _():
        ref[...] += val


def _prenorm(x, g, scale, shift):
    s, d = x.shape
    tr = _tile(s, (256, 128))

    def body(x_ref, g_ref, sc_ref, sh_ref, h_ref):
        xv = x_ref[...]
        rstd = lax.rsqrt(jnp.mean(xv * xv, axis=-1, keepdims=True) + EPS)
        h_ref[...] = ((xv * rstd) * g_ref[...] * (1.0 + sc_ref[...]) + sh_ref[...]).astype(BF16)

    return _rows_call("prenorm", body, (x, g, scale, shift),
                      [_row_spec(tr, d), _vec_spec(d), _vec_spec(d), _vec_spec(d)],
                      jax.ShapeDtypeStruct((s, d), BF16), _row_spec(tr, d), s // tr)


def _gate_grads(dxv, out_ref, gate_ref, dout_ref, dgate_ref):
    dout_ref[...] = (gate_ref[...] * dxv).astype(BF16)
    _accumulate(dgate_ref, jnp.sum(dxv * out_ref[...].astype(F32), axis=0, keepdims=True))


def _prenorm_bwd(x, dh, dres, g, scale, below=None):
    s, d = x.shape
    tr = _tile(s, (256, 128))

    def body(x_ref, dh_ref, dres_ref, g_ref, sc_ref, *rest):
        dx_ref, dshift_ref, dscale_ref, dg_ref = rest[-6:-2] if below else rest
        xv = x_ref[...]
        dhv = dh_ref[...].astype(F32)
        rstd = lax.rsqrt(jnp.mean(xv * xv, axis=-1, keepdims=True) + EPS)
        xhat = xv * rstd
        gv = g_ref[...]
        one_sc = 1.0 + sc_ref[...]
        dxhat = dhv * gv * one_sc
        dxv = dres_ref[...] + rstd * (dxhat - xhat * jnp.mean(dxhat * xhat, axis=-1, keepdims=True))
        dx_ref[...] = dxv
        _accumulate(dshift_ref, jnp.sum(dhv, axis=0, keepdims=True))
        _accumulate(dscale_ref, jnp.sum(dhv * xhat * gv, axis=0, keepdims=True))
        _accumulate(dg_ref, jnp.sum(dhv * xhat * one_sc, axis=0, keepdims=True))
        if below:
            _gate_grads(dxv, rest[0], rest[1], rest[-2], rest[-1])

    vec = jax.ShapeDtypeStruct((1, d), F32)
    operands = (x, dh, dres, g, scale) + (tuple(below) if below else ())
    in_specs = [_row_spec(tr, d), _row_spec(tr, d), _row_spec(tr, d), _vec_spec(d), _vec_spec(d)]
    out_shape = [jax.ShapeDtypeStruct((s, d), F32), vec, vec, vec]
    out_specs = [_row_spec(tr, d), _vec_spec(d), _vec_spec(d), _vec_spec(d)]
    if below:
        in_specs += [_row_spec(tr, d), _vec_spec(d)]
        out_shape += [jax.ShapeDtypeStruct((s, d), BF16), vec]
        out_specs += [_row_spec(tr, d), _vec_spec(d)]
    return _rows_call("prenorm_bwd", body, operands, in_specs, out_shape, out_specs, s // tr)


def _final_loss(x, target, g, out_below, gate_below):
    s, d = x.shape
    tr = _tile(s, (256, 128))
    n_tiles = s // tr

    def body(x_ref, t_ref, g_ref, out_ref, gate_ref, loss_ref, dx_ref, dg_ref, dout_ref, dgate_ref, acc_ref):
        xv = x_ref[...]
        rstd = lax.rsqrt(jnp.mean(xv * xv, axis=-1, keepdims=True) + EPS)
        xhat = xv * rstd
        gv = g_ref[...]
        err = xhat * gv - t_ref[...]
        dy = err * (1.0 / d)
        dxhat = dy * gv
        dxv = rstd * (dxhat - xhat * jnp.mean(dxhat * xhat, axis=-1, keepdims=True))
        dx_ref[...] = dxv
        _accumulate(dg_ref, jnp.sum(dy * xhat, axis=0, keepdims=True))
        _accumulate(acc_ref, jnp.sum(err * err, axis=0, keepdims=True))
        _gate_grads(dxv, out_ref, gate_ref, dout_ref, dgate_ref)

        @pl.when(pl.program_id(0) == n_tiles - 1)
        def _():
            loss_ref[...] = (0.5 / d) * jnp.sum(acc_ref[...], axis=1, keepdims=True)

    vec = jax.ShapeDtypeStruct((1, d), F32)
    return pl.pallas_call(
        body, name="final_loss", grid=(n_tiles,),
        in_specs=[_row_spec(tr, d), _row_spec(tr, d), _vec_spec(d), _row_spec(tr, d), _vec_spec(d)],
        out_specs=[pl.BlockSpec((1, 1), lambda i: (0, 0)), _row_spec(tr, d), _vec_spec(d), _row_spec(tr, d),
                   _vec_spec(d)],
        out_shape=[jax.ShapeDtypeStruct((1, 1), F32), jax.ShapeDtypeStruct((s, d), F32), vec,
                   jax.ShapeDtypeStruct((s, d), BF16), vec],
        scratch_shapes=[pltpu.VMEM((1, d), F32)],
        compiler_params=_params(1),
    )(x, target, g, out_below, gate_below)


def _rope(t, cos, sin):
    return t * cos + pltpu.roll(t, HEAD_DIM // 2, axis=1) * sin


def _unrope(dt, cos, sin):
    return dt * cos + pltpu.roll(dt * sin, HEAD_DIM // 2, axis=1)


def _band_blocks(s, dil):
    sub = s // dil
    kw = min(K_WINDOW, sub)

    def rows(r, start, n):
        if dil == 1:
            return pl.ds(pl.multiple_of(start, RADIUS), n)
        return pl.ds(r + dil * start, n, stride=dil)

    def window(idx):
        nb = sub // Q_BLOCK
        r, b = idx // nb, idx % nb
        q0 = b * Q_BLOCK
        start = jnp.clip(q0 - RADIUS, 0, sub - kw)
        ahead = (lax.broadcasted_iota(jnp.int32, (Q_BLOCK, kw), 1)
                 - lax.broadcasted_iota(jnp.int32, (Q_BLOCK, kw), 0)) + (start - q0 + RADIUS)
        valid = lax.bitcast_convert_type(ahead, jnp.uint32) <= 2 * RADIUS
        return rows(r, q0, Q_BLOCK), rows(r, start, kw), valid

    return window


def _head_col(s, group, nh):
    return pl.BlockSpec((s, HEAD_DIM), lambda h: (0, group * nh + h), pipeline_mode=pl.Buffered(1))


def _attn_fwd(proj, cos, sin, aw, plan=None):
    s = proj.shape[0]
    nh = aw // HEAD_DIM
    scale = HEAD_DIM ** -0.5
    n_blocks = s // Q_BLOCK

    def body(q_ref, k_ref, v_ref, cos_ref, sin_ref, attn_ref, lse_ref, qf, kf, vf, acc):
        cosv, sinv = cos_ref[...], sin_ref[...]
        qf[...] = _rope(q_ref[...].astype(F32), cosv, sinv) * scale
        kf[...] = _rope(k_ref[...].astype(F32), cosv, sinv)
        vf[...] = v_ref[...].astype(F32)

        for pattern, dil in enumerate(DILATIONS):
            window = _band_blocks(s, dil)

            def block(idx, carry, window=window, first=(pattern == 0)):
                q_rows, k_rows, valid = window(idx)
                q = qf[q_rows, :].astype(BF16)
                kk = kf[k_rows, :].astype(BF16)
                vv = vf[k_rows, :].astype(BF16)
                sc = lax.dot_general(q, kk, NT, preferred_element_type=F32)
                sc = jnp.where(valid, sc, NEG_INF)
                m = jnp.max(sc, axis=1, keepdims=True)
                p = jnp.exp(sc - m)
                den = jnp.sum(p, axis=1, keepdims=True)
                o = lax.dot_general(p.astype(BF16), vv, NN, preferred_element_type=F32) / den
                lse = jnp.broadcast_to(m + jnp.log(den), (Q_BLOCK, HEAD_DIM))
                if first:
                    acc[q_rows, :] = o
                    lse_ref[q_rows, :] = lse
                else:
                    lse_old = lse_ref[q_rows, :]
                    top = jnp.maximum(lse_old, lse)
                    w_old, w_new = jnp.exp(lse_old - top), jnp.exp(lse - top)
                    tot = w_old + w_new
                    acc[q_rows, :] = (acc[q_rows, :] * w_old + o * w_new) / tot
                    lse_ref[q_rows, :] = top + jnp.log(tot)
                return carry

            lax.fori_loop(0, n_blocks, block, 0, unroll=2)

        attn_ref[...] = acc[...].astype(BF16)

    table = pl.BlockSpec((s, HEAD_DIM), lambda h: (0, 0), pipeline_mode=pl.Buffered(1))
    out = pl.BlockSpec((s, HEAD_DIM), lambda h: (0, h))
    return _call(
        body, name="attn_fwd", grid=(nh,),
        in_specs=[_head_col(s, 0, nh), _head_col(s, 1, nh), _head_col(s, 2, nh), table, table],
        out_specs=[out, out],
        out_shape=[jax.ShapeDtypeStruct((s, aw), BF16), jax.ShapeDtypeStruct((s, aw), F32)],
        scratch_shapes=[pltpu.VMEM((s, HEAD_DIM), F32)] * 4,
        operands=(proj, proj, proj, cos, sin), parallel=1, plan=plan)


def _attn_bwd(proj, cos, sin, dy, attn, lse, aw, plan=None):
    s = proj.shape[0]
    nh = aw // HEAD_DIM
    scale = HEAD_DIM ** -0.5
    n_blocks = s // Q_BLOCK

    def body(q_ref, k_ref, v_ref, za_ref, cos_ref, sin_ref, dy_ref, attn_ref, lse_ref,
             dq_ref, dk_ref, dv_ref, qf, kf, vf, dof, delta, dqa, dka, dva):
        cosv, sinv = cos_ref[...], sin_ref[...]
        qf[...] = _rope(q_ref[...].astype(F32), cosv, sinv) * scale
        kf[...] = _rope(k_ref[...].astype(F32), cosv, sinv)
        vf[...] = v_ref[...].astype(F32)
        do_all = dy_ref[...].astype(F32) * _silu(za_ref[...].astype(F32))
        dof[...] = do_all
        delta[...] = jnp.broadcast_to(
            jnp.sum(do_all * attn_ref[...].astype(F32), axis=1, keepdims=True), (s, HEAD_DIM))
        dqa[...] = jnp.zeros_like(dqa)
        dka[...] = jnp.zeros_like(dka)
        dva[...] = jnp.zeros_like(dva)

        for dil in DILATIONS:
            window = _band_blocks(s, dil)

            def block(idx, carry, window=window):
                q_rows, k_rows, valid = window(idx)
                q = qf[q_rows, :].astype(BF16)
                kk = kf[k_rows, :].astype(BF16)
                vv = vf[k_rows, :].astype(BF16)
                dov = dof[q_rows, :].astype(BF16)
                lse_q = lse_ref[q_rows, :][:, 0:1]
                delta_q = delta[q_rows, :][:, 0:1]
                sc = lax.dot_general(q, kk, NT, preferred_element_type=F32)
                p = jnp.where(valid, jnp.exp(sc - lse_q), 0.0)
                dp = lax.dot_general(dov, vv, NT, preferred_element_type=F32)
                ds = (p * (dp - delta_q)).astype(BF16)
                dqa[q_rows, :] += lax.dot_general(ds, kk, NN, preferred_element_type=F32)
                dka[k_rows, :] += lax.dot_general(ds, q, TN, preferred_element_type=F32)
                dva[k_rows, :] += lax.dot_general(p.astype(BF16), dov, TN, preferred_element_type=F32)
                return carry

            lax.fori_loop(0, n_blocks, block, 0, unroll=2)

        dq_ref[...] = (_unrope(dqa[...], cosv, sinv) * scale).astype(BF16)
        dk_ref[...] = _unrope(dka[...], cosv, sinv).astype(BF16)
        dv_ref[...] = dva[...].astype(BF16)

    own = pl.BlockSpec((s, HEAD_DIM), lambda h: (0, h), pipeline_mode=pl.Buffered(1))
    table = pl.BlockSpec((s, HEAD_DIM), lambda h: (0, 0), pipeline_mode=pl.Buffered(1))
    out = pl.BlockSpec((s, HEAD_DIM), lambda h: (0, h))
    shape = jax.ShapeDtypeStruct((s, aw), BF16)
    return _call(
        body, name="attn_bwd", grid=(nh,),
        in_specs=[_head_col(s, 0, nh), _head_col(s, 1, nh), _head_col(s, 2, nh), _head_col(s, 3, nh),
                  table, table, own, own, own],
        out_specs=[out, out, out],
        out_shape=[shape, shape, shape],
        scratch_shapes=[pltpu.VMEM((s, HEAD_DIM), F32)] * 8,
        operands=(proj, proj, proj, proj, cos, sin, dy, attn, lse), parallel=1, plan=plan)


def _rope_tables(s):
    half = HEAD_DIM // 2
    inv = np.float32(ROPE_THETA) ** (-np.arange(half, dtype=np.float32) / np.float32(half))
    ang = np.arange(s, dtype=np.float32)[:, None] * inv[None, :]
    cos, sin = np.cos(ang), np.sin(ang)
    return (jnp.asarray(np.concatenate([cos, cos], axis=-1), F32),
            jnp.asarray(np.concatenate([-sin, sin], axis=-1), F32))


def _conv_chunks(s):
    for k in range(s // CONV_ROWS):
        lo = max(0, k * CONV_ROWS - CONV_HALO)
        hi = min(s, (k + 1) * CONV_ROWS + CONV_HALO)
        yield k * CONV_ROWS, lo, hi


def _neighbours(p, lo, s):
    n = p.shape[0]
    row = lo + lax.broadcasted_iota(jnp.int32, p.shape, 0)
    prev = jnp.where(row == 0, 0.0, pltpu.roll(p, 1, axis=0))
    nxt = jnp.where(row == s - 1, 0.0, pltpu.roll(p, n - 1, axis=0))
    return prev, nxt


def _ab_mix(attn, proj, conv_w, aw):
    s = proj.shape[0]
    nt = aw // LANES

    def col(group, sel):
        return pl.BlockSpec((s, LANES), lambda i: (0, group * nt + sel(i)))

    a_sel = lambda i: jnp.minimum(i, nt - 1)
    b_sel = lambda i: jnp.maximum(i - nt, 0)

    def body(attn_ref, za_ref, ub_ref, gb_ref, gc_ref, zb_ref, w_ref, y_ref):
        i = pl.program_id(0)

        @pl.when(i < nt)
        def _():
            y_ref[...] = (attn_ref[...].astype(F32) * _silu(za_ref[...].astype(F32))).astype(BF16)

        @pl.when(i >= nt)
        def _():
            w = w_ref[...]
            for c0, lo, hi in _conv_chunks(s):
                p = gc_ref[lo:hi, :].astype(F32) * ub_ref[lo:hi, :].astype(F32)
                prev, nxt = _neighbours(p, lo, s)
                cv = w[0:1, :] * prev + w[1:2, :] * p + w[2:3, :] * nxt
                yb = gb_ref[lo:hi, :].astype(F32) * cv * _silu(zb_ref[lo:hi, :].astype(F32))
                y_ref[c0:c0 + CONV_ROWS, :] = yb[c0 - lo:c0 - lo + CONV_ROWS, :].astype(BF16)

    return pl.pallas_call(
        body, name="ab_mix", grid=(2 * nt,),
        in_specs=[pl.BlockSpec((s, LANES), lambda i: (0, a_sel(i))),
                  col(3, a_sel), col(4, b_sel), col(5, b_sel), col(6, b_sel), col(7, b_sel),
                  pl.BlockSpec((3, LANES), lambda i: (0, b_sel(i)))],
        out_specs=pl.BlockSpec((s, LANES), lambda i: (0, i)),
        out_shape=jax.ShapeDtypeStruct((s, 2 * aw), BF16),
        compiler_params=_params(1),
    )(attn, proj, proj, proj, proj, proj, conv_w)


def _ab_dproj(dqkv, dy, attn, proj, conv_w, aw):
    s = proj.shape[0]
    nt = aw // LANES

    def col(group, sel):
        return pl.BlockSpec((s, LANES), lambda i: (0, group * nt + sel(i)))

    def qkv_spec(part):
        return pl.BlockSpec((s, LANES), lambda i: (0, jnp.clip(i - part * nt, 0, nt - 1)))

    a_sel = lambda i: jnp.clip(i - 3 * nt, 0, nt - 1)
    b_sel = lambda i: jnp.maximum(i - 4 * nt, 0) % nt
    w_sel = lambda i: jnp.clip(i - 4 * nt, 0, nt - 1)

    def body(*refs):
        g_refs = refs[:3]
        dya_ref, attn_ref, za_ref, dyb_ref, ub_ref, gb_ref, gc_ref, zb_ref, w_ref, out_ref, dw_ref = refs[3:]
        i = pl.program_id(0)

        for part in range(3):
            @pl.when(jnp.logical_and(i >= part * nt, i < (part + 1) * nt))
            def _(part=part):
                out_ref[...] = g_refs[part][...]

        @pl.when(jnp.logical_and(i >= 3 * nt, i < 4 * nt))
        def _():
            out_ref[...] = (dya_ref[...].astype(F32) * attn_ref[...].astype(F32)
                            * _dsilu(za_ref[...].astype(F32))).astype(BF16)

        for which in range(4):
            @pl.when(jnp.logical_and(i >= (4 + which) * nt, i < (5 + which) * nt))
            def _(which=which):
                w = w_ref[...]
                dw = [jnp.zeros((1, LANES), F32) for _ in range(3)]
                for c0, lo, hi in _conv_chunks(s):
                    ctr = slice(c0 - lo, c0 - lo + CONV_ROWS)
                    ub = ub_ref[lo:hi, :].astype(F32)
                    gc = gc_ref[lo:hi, :].astype(F32)
                    gb = gb_ref[lo:hi, :].astype(F32)
                    zb = zb_ref[lo:hi, :].astype(F32)
                    dyb = dyb_ref[lo:hi, :].astype(F32)
                    p = gc * ub
                    prev, nxt = _neighbours(p, lo, s)
                    if which == 1:
                        cv = w[0:1, :] * prev + w[1:2, :] * p + w[2:3, :] * nxt
                        res = dyb * cv * _silu(zb)
                    elif which == 3:
                        cv = w[0:1, :] * prev + w[1:2, :] * p + w[2:3, :] * nxt
                        res = dyb * gb * cv * _dsilu(zb)
                    else:
                        dcv = dyb * gb * _silu(zb)
                        dprev, dnxt = _neighbours(dcv, lo, s)
                        dp = w[0:1, :] * dnxt + w[1:2, :] * dcv + w[2:3, :] * dprev
                        res = dp * (gc if which == 0 else ub)
                        if which == 0:
                            for t, nb in enumerate((prev, p, nxt)):
                                dw[t] = dw[t] + jnp.sum((dcv * nb)[ctr, :], axis=0, keepdims=True)
                    out_ref[c0:c0 + CONV_ROWS, :] = res[ctr, :].astype(BF16)
                if which == 0:
                    dw_ref[...] = jnp.concatenate(dw, axis=0)

    return pl.pallas_call(
        body, name="ab_dproj", grid=(8 * nt,),
        in_specs=[qkv_spec(0), qkv_spec(1), qkv_spec(2),
                  pl.BlockSpec((s, LANES), lambda i: (0, a_sel(i))),
                  pl.BlockSpec((s, LANES), lambda i: (0, a_sel(i))),
                  col(3, a_sel),
                  pl.BlockSpec((s, LANES), lambda i: (0, nt + b_sel(i))),
                  col(4, b_sel), col(5, b_sel), col(6, b_sel), col(7, b_sel),
                  pl.BlockSpec((3, LANES), lambda i: (0, b_sel(i)))],
        out_specs=[pl.BlockSpec((s, LANES), lambda i: (0, i)),
                   pl.BlockSpec((3, LANES), lambda i: (0, w_sel(i)))],
        out_shape=[jax.ShapeDtypeStruct((s, 8 * aw), BF16), jax.ShapeDtypeStruct((3, aw), F32)],
        compiler_params=_params(1),
    )(*dqkv, dy, attn, proj, dy, proj, proj, proj, proj, conv_w)


def _sgu_norm(v, ln_g, ln_b):
    gv = _gelu(v)
    mu = jnp.mean(gv, axis=-1, keepdims=True)
    xc = gv - mu
    rstd = lax.rsqrt(jnp.mean(xc * xc, axis=-1, keepdims=True) + EPS)
    vhat = xc * rstd
    return vhat, rstd, vhat * ln_g + ln_b


def _sgu_fwd(uvz, ln_g, ln_b, w_s, b_s, cw):
    s = uvz.shape[0]
    tr = 2 * CHUNK if s % (2 * CHUNK) == 0 else CHUNK
    gw = cw // N_GROUPS

    def body(u_ref, v_ref, z_ref, g_ref, b_ref, ws_ref, bs_ref, y_ref):
        _, _, vn = _sgu_norm(v_ref[...].astype(F32), g_ref[...], b_ref[...])
        vn = vn.astype(BF16)
        for ch in range(tr // CHUNK):
            rows = slice(ch * CHUNK, (ch + 1) * CHUNK)
            for grp in range(N_GROUPS):
                cols = slice(grp * gw, (grp + 1) * gw)
                mixed = lax.dot_general(ws_ref[grp], vn[rows, cols], NN, preferred_element_type=F32) + bs_ref[grp]
                y_ref[rows, cols] = (_gelu(u_ref[rows, cols].astype(F32)) * mixed
                                     * _silu(z_ref[rows, cols].astype(F32))).astype(BF16)

    full3 = lambda shape: pl.BlockSpec(shape, lambda i: (0, 0, 0))
    return pl.pallas_call(
        body, name="sgu_fwd", grid=(s // tr,),
        in_specs=[pl.BlockSpec((tr, cw), lambda i: (i, 0)), pl.BlockSpec((tr, cw), lambda i: (i, 1)),
                  pl.BlockSpec((tr, cw), lambda i: (i, 2)), _vec_spec(cw), _vec_spec(cw),
                  full3(w_s.shape), full3(b_s.shape)],
        out_specs=pl.BlockSpec((tr, cw), lambda i: (i, 0)),
        out_shape=jax.ShapeDtypeStruct((s, cw), BF16),
        compiler_params=_params(1, parallel=1),
    )(uvz, uvz, uvz, ln_g, ln_b, w_s, b_s)


def _sgu_bwd(uvz, dy, ln_g, ln_b, w_s, b_s, cw, plan=None):
    s = uvz.shape[0]
    tr = 2 * CHUNK if s % (2 * CHUNK) == 0 else CHUNK
    gw = cw // N_GROUPS

    def body(u_ref, v_ref, z_ref, dy_ref, g_ref, b_ref, ws_ref, bs_ref,
             duvz_ref, dws_ref, dbs_ref, dg_ref, db_ref, dvn_ref):
        vv = v_ref[...].astype(F32)
        gvec = g_ref[...]
        vhat, rstd, vn = _sgu_norm(vv, gvec, b_ref[...])
        vn = vn.astype(BF16)
        first = pl.program_id(0) == 0

        @pl.when(first)
        def _():
            dws_ref[...] = jnp.zeros_like(dws_ref)
            dbs_ref[...] = jnp.zeros_like(dbs_ref)

        for ch in range(tr // CHUNK):
            rows = slice(ch * CHUNK, (ch + 1) * CHUNK)
            for grp in range(N_GROUPS):
                cols = slice(grp * gw, (grp + 1) * gw)
                vn_g = vn[rows, cols]
                mixed = lax.dot_general(ws_ref[grp], vn_g, NN, preferred_element_type=F32) + bs_ref[grp]
                uu = u_ref[rows, cols].astype(F32)
                zz = z_ref[rows, cols].astype(F32)
                dyv = dy_ref[rows, cols].astype(F32)
                gu, sz = _gelu(uu), _silu(zz)
                duvz_ref[rows, grp * gw:(grp + 1) * gw] = (dyv * mixed * sz * _dgelu(uu)).astype(BF16)
                duvz_ref[rows, 2 * cw + grp * gw:2 * cw + (grp + 1) * gw] = (
                    dyv * gu * mixed * _dsilu(zz)).astype(BF16)
                dmixed = dyv * gu * sz
                dm16 = dmixed.astype(BF16)
                dws_ref[grp] += lax.dot_general(dm16, vn_g, NT, preferred_element_type=F32)
                dbs_ref[grp] += jnp.broadcast_to(jnp.sum(dmixed, axis=1, keepdims=True), (CHUNK, LANES))
                dvn_ref[rows, cols] = lax.dot_general(ws_ref[grp], dm16, TN, preferred_element_type=F32)

        dvn = dvn_ref[...]
        _accumulate(dg_ref, jnp.sum(dvn * vhat, axis=0, keepdims=True))
        _accumulate(db_ref, jnp.sum(dvn, axis=0, keepdims=True))
        dvhat = dvn * gvec
        dgv = rstd * (dvhat - jnp.mean(dvhat, axis=-1, keepdims=True)
                      - vhat * jnp.mean(dvhat * vhat, axis=-1, keepdims=True))
        duvz_ref[:, cw:2 * cw] = (dgv * _dgelu(vv)).astype(BF16)

    full3 = lambda shape: pl.BlockSpec(shape, lambda i: (0, 0, 0))
    acc3 = jax.ShapeDtypeStruct((N_GROUPS, CHUNK, LANES), F32)
    vec = jax.ShapeDtypeStruct((1, cw), F32)
    row = pl.BlockSpec((tr, cw), lambda i: (i, 0))
    return _call(
        body, name="sgu_bwd", grid=(s // tr,),
        in_specs=[row, pl.BlockSpec((tr, cw), lambda i: (i, 1)), pl.BlockSpec((tr, cw), lambda i: (i, 2)),
                  row, _vec_spec(cw), _vec_spec(cw), full3(w_s.shape), full3(b_s.shape)],
        out_specs=[pl.BlockSpec((tr, 3 * cw), lambda i: (i, 0)), full3((N_GROUPS, CHUNK, LANES)),
                   full3((N_GROUPS, CHUNK, LANES)), _vec_spec(cw), _vec_spec(cw)],
        out_shape=[jax.ShapeDtypeStruct((s, 3 * cw), BF16), acc3, acc3, vec, vec],
        scratch_shapes=[pltpu.VMEM((tr, cw), F32)],
        operands=(uvz, uvz, uvz, dy, ln_g, ln_b, w_s, b_s), plan=plan)


def _flat_rows(a):
    return a.reshape(-1, a.shape[-1])


def _add_sibling(grad, recv, core_idx):
    nchip, k, n = grad.shape
    tr = _tile(k // 2, (256, 128))
    nb = (k // 2) // tr

    def body(c_ref, g_ref, r_ref, o_ref):
        o_ref[...] = (g_ref[...].astype(F32) + r_ref[...].astype(F32)).astype(BF16)

    return pl.pallas_call(
        body, name="add_sibling",
        grid_spec=pltpu.PrefetchScalarGridSpec(
            num_scalar_prefetch=1, grid=(nchip, nb),
            in_specs=[pl.BlockSpec((None, tr, n), lambda q, i, c: (q, c[0] * nb + i, 0)),
                      pl.BlockSpec((None, tr, n), lambda q, i, c: (q, i, 0))],
            out_specs=pl.BlockSpec((None, tr, n), lambda q, i, c: (q, i, 0))),
        out_shape=jax.ShapeDtypeStruct((nchip, k // 2, n), BF16),
        compiler_params=_params(2, parallel=2),
    )(core_idx, grad, recv)


def _sum_chips(own, others, reduced, layer, place_idx):
    _, kh, n = own.shape
    tr = _tile(kh, (256, 128))
    nb = kh // tr

    def body(place_ref, own_ref, oth_ref, red_ref, o_ref):
        acc = own_ref[...].astype(F32)
        for q in range(3):
            acc = acc + oth_ref[q].astype(F32)
        o_ref[...] = acc

    return pl.pallas_call(
        body, name="sum_chips",
        grid_spec=pltpu.PrefetchScalarGridSpec(
            num_scalar_prefetch=1, grid=(nb,),
            in_specs=[pl.BlockSpec((None, tr, n), lambda i, p: (p[0], i, 0)),
                      pl.BlockSpec((3, tr, n), lambda i, p: (0, i, 0)),
                      HBM_SPEC],
            out_specs=pl.BlockSpec((None, tr, n), lambda i, p: (layer, p[1] * nb + i, 0))),
        out_shape=jax.ShapeDtypeStruct(reduced.shape, reduced.dtype),
        input_output_aliases={3: 0},
        compiler_params=_params(1, parallel=1),
    )(place_idx, own, others, reduced)


def _sum_devices(parts, plan=None):
    nd, r, _ = parts.shape
    tr = _tile(r, (512, 256, 128, 64, 32, 16, 8))

    def body(p_ref, o_ref):
        acc = p_ref[0]
        for q in range(1, nd):
            acc = acc + p_ref[q]
        o_ref[...] = acc

    return _call(
        body, name="sum_devices", grid=(r // tr,),
        in_specs=[pl.BlockSpec((nd, tr, LANES), lambda i: (0, i, 0))],
        out_specs=pl.BlockSpec((tr, LANES), lambda i: (i, 0)),
        out_shape=jax.ShapeDtypeStruct((r, LANES), F32),
        operands=(parts,), parallel=1, plan=plan)


def _adamw(w, g, m, v, plan=None):
    r, n = w.shape
    tr = _tile(r, [p for p in (1024, 512, 256, 128, 64, 32, 16, 8) if p * n <= ELEMENTWISE_BLOCK])

    def body(w_ref, g_ref, m_ref, v_ref, d_ref, nm_ref, nv_ref):
        gv = g_ref[...]
        nm = ADAM_B1 * m_ref[...] + (1.0 - ADAM_B1) * gv
        nv = ADAM_B2 * v_ref[...] + (1.0 - ADAM_B2) * (gv * gv)
        m_hat = nm / (1.0 - ADAM_B1 ** ADAM_STEP)
        v_hat = nv / (1.0 - ADAM_B2 ** ADAM_STEP)
        d_ref[...] = -ADAM_LR * (m_hat / (jnp.sqrt(v_hat) + ADAM_EPS) + ADAM_WD * w_ref[...])
        nm_ref[...] = nm
        nv_ref[...] = nv

    spec = pl.BlockSpec((tr, n), lambda i: (i, 0))
    shp = jax.ShapeDtypeStruct((r, n), F32)
    return _call(
        body, name="adamw", grid=(r // tr,),
        in_specs=[spec] * 4, out_specs=[spec] * 3, out_shape=[shp] * 3,
        operands=(w, g, m, v), parallel=1, plan=plan)


def _pack(arrays, row_multiple=8):
    flat = [a.reshape(-1) for a in arrays]
    sizes = [f.shape[0] for f in flat]
    total = sum(sizes)
    unit = LANES * row_multiple
    padded = -(-total // unit) * unit
    if padded > total:
        flat.append(jnp.zeros((padded - total,), F32))
    offsets = [sum(sizes[:i]) for i in range(len(sizes))]
    return jnp.concatenate(flat).reshape(-1, LANES), offsets


def _unpack(packed, offsets, shapes):
    flat = packed.reshape(-1)
    return [flat[o:o + math.prod(s)].reshape(s) for o, s in zip(offsets, shapes)]


def kernel(x, c, ab_norm_g, ab_w_mod, ab_b_mod, ab_w_in, ab_conv_w, ab_w_out, sg_norm_g, sg_w_mod, sg_b_mod, sg_w_in, sg_ln_g, sg_ln_b, sg_w_s, sg_b_s, sg_w_out, final_norm_g, loss_target, m_ab_norm_g, m_ab_w_mod, m_ab_b_mod, m_ab_w_in, m_ab_conv_w, m_ab_w_out, m_sg_norm_g, m_sg_w_mod, m_sg_b_mod, m_sg_w_in, m_sg_ln_g, m_sg_ln_b, m_sg_w_s, m_sg_b_s, m_sg_w_out, m_final_norm_g, v_ab_norm_g, v_ab_w_mod, v_ab_b_mod, v_ab_w_in, v_ab_conv_w, v_ab_w_out, v_sg_norm_g, v_sg_w_mod, v_sg_b_mod, v_sg_w_in, v_sg_ln_g, v_sg_ln_b, v_sg_w_s, v_sg_b_s, v_sg_w_out, v_final_norm_g):
    s, d = x.shape[1], x.shape[2]
    aw = d // 2
    cw = d
    mod_l = ab_w_mod.shape[-1]
    x0 = x[0]
    target = loss_target[0]
    mx, my, mc = lax.axis_index("x"), lax.axis_index("y"), lax.axis_index("c")
    chip = 2 * mx + my
    chip_idx = jnp.reshape(chip, (1,)).astype(jnp.int32)
    core_idx = jnp.reshape(mc, (1,)).astype(jnp.int32)
    place_idx = jnp.stack([chip, mc]).astype(jnp.int32)

    win = [_place_own_shard(ab_w_in if L % 2 == 0 else sg_w_in, L // 2, chip_idx) for L in range(4)]
    wout = [_place_own_shard(ab_w_out if L % 2 == 0 else sg_w_out, L // 2, chip_idx) for L in range(4)]
    k_in, k_out = d, wout[0].shape[1]

    def gather_plan(ici=(), pass_on=()):
        arrays, copies = [], []
        for stage, make in ((ici, _gather_ici), (pass_on, _gather_pass_on)):
            for kind, L in stage:
                arr = win[L] if kind == "in" else wout[L]
                arrays.append(arr)
                copies += make(len(arrays) - 1, k_in if kind == "in" else k_out)
        return _Plan(tuple(arrays), tuple(copies)), [(kind, L) for kind, L in tuple(ici) + tuple(pass_on)]

    def absorb(plan_and_names, updated):
        _, names = plan_and_names
        for pos, (kind, L) in enumerate(names):
            if kind == "in":
                win[L] = updated[pos]
            else:
                wout[L] = updated[pos]

    win[0] = _comm_stages("gather_first_w_in", [win[0]], [_gather_ici(0, k_in), _gather_pass_on(0, k_in)])[0]

    small_local = [c[0], ab_conv_w, sg_norm_g, sg_ln_g, sg_ln_b]
    small_shapes = [a.shape for a in small_local]
    payload, small_off = _pack(small_local)
    gathered = _all_to_all(jnp.broadcast_to(payload[None], (N_DEV,) + payload.shape), "gather_small")
    per_dev = [_unpack(gathered[b], small_off, small_shapes) for b in range(N_DEV)]
    c_all = jnp.stack([per_dev[b][0] for b in range(N_DEV)])

    def from_chips(idx, axis):
        return jnp.concatenate([per_dev[2 * q][idx] for q in range(N_CHIPS)], axis=axis)

    conv_w_full = from_chips(1, 2)
    sg_norm_g_full = from_chips(2, 1)
    sg_ln_g_full = from_chips(3, 1)
    sg_ln_b_full = from_chips(4, 1)

    ab_b_local = lax.dynamic_slice_in_dim(ab_b_mod, chip * mod_l, mod_l, axis=1)
    mod_rows = []
    for layer in range(4):
        i = layer // 2
        w_mod, bias = (ab_w_mod, ab_b_local) if layer % 2 == 0 else (sg_w_mod, sg_b_mod)
        mod_rows.append(_mod_fwd(c_all, w_mod, bias[i:i + 1], i))
    mod_local = jnp.stack(mod_rows, axis=1)
    mod_recv = _all_to_all(mod_local.reshape(N_DEV, -1, LANES), "exchange_mod")
    mod_recv = mod_recv.reshape(N_DEV, 4, mod_l)
    mod_full = jnp.concatenate([mod_recv[2 * q] for q in range(N_CHIPS)], axis=-1)
    shifts = [mod_full[l:l + 1, :d] for l in range(4)]
    scales = [mod_full[l:l + 1, d:2 * d] for l in range(4)]
    gates = [mod_full[l:l + 1, 2 * d:] for l in range(4)]

    cos, sin = _rope_tables(s)
    w_s16 = sg_w_s.astype(BF16)
    b_s3 = sg_b_s[..., None]

    fwd_comm = {
        ("in_proj", 0): ([("in", 1), ("out", 0)], []),
        ("attn", 0): ([("out", 1), ("in", 2)], [("out", 0)]),
        ("out_proj", 0): ([], [("in", 1), ("out", 1)]),
        ("in_proj", 1): ([("out", 2)], [("in", 2)]),
        ("out_proj", 1): ([], [("out", 2)]),
        ("in_proj", 2): ([("in", 3)], []),
        ("attn", 2): ([("out", 3)], []),
        ("out_proj", 2): ([], [("in", 3), ("out", 3)]),
    }

    def carried(key, fn, *args):
        if key not in fwd_comm:
            return fn(*args)
        pn = gather_plan(*fwd_comm[key])
        res, updated = fn(*args, plan=pn[0])
        absorb(pn, updated)
        return res

    saved = []
    xs = x0
    for layer in range(4):
        i = layer // 2
        if layer % 2 == 0:
            h = _prenorm(xs, ab_norm_g[i:i + 1], scales[layer], shifts[layer])
            proj = carried(("in_proj", layer), _in_proj, h, win[layer])
            attn, lse = carried(("attn", layer), _attn_fwd, proj, cos, sin, aw)
            y = _ab_mix(attn, proj, conv_w_full[i], aw)
            x_next, out = carried(("out_proj", layer), _out_proj_residual, y, wout[layer].reshape(-1, d), xs,
                                  gates[layer])
            saved.append((xs, h, proj, y, out, attn, lse))
        else:
            h = _prenorm(xs, sg_norm_g_full[i:i + 1], scales[layer], shifts[layer])
            uvz = carried(("in_proj", layer), _in_proj, h, win[layer])
            y = _sgu_fwd(uvz, sg_ln_g_full[i:i + 1], sg_ln_b_full[i:i + 1], w_s16[i], b_s3[i], cw)
            x_next, out = carried(("out_proj", layer), _out_proj_residual, y, wout[layer].reshape(-1, d), xs,
                                  gates[layer])
            saved.append((xs, h, uvz, y, out))
        xs = x_next

    loss11, dx, d_final_g, dout, dgate = _final_loss(xs, target, final_norm_g[None], saved[3][4], gates[3])
    loss = lax.psum(loss11[0, 0], ("x", "y", "c"))

    reduced = {"in": [lax.empty((2,) + w.shape[1:], F32) for w in (ab_w_in, sg_w_in)],
               "out": [lax.empty((2,) + w.shape[1:], F32) for w in (ab_w_out, sg_w_out)]}
    grads = {}
    stage = {}
    k_of = {"in": k_in, "out": k_out}

    def swap_plan(which):
        arrays, copies = [], []
        for kind, L in which:
            g = grads[kind, L]
            arrays += [g, jax.ShapeDtypeStruct((N_CHIPS, g.shape[1] // 2, g.shape[2]), BF16)]
            copies += _reduce_swap(len(arrays) - 2, len(arrays) - 1, k_of[kind])
        return _Plan(tuple(arrays), tuple(copies))

    def after_swap(which, updated):
        for pos, (kind, L) in enumerate(which):
            stage[kind, L] = _add_sibling(grads[kind, L], updated[2 * pos + 1], core_idx)

    def ici_plan(pieces):
        arrays, copies = [], []
        for kind, L, only in pieces:
            cs = stage[kind, L]
            arrays += [cs, stage.get((kind, L, "recv"), jax.ShapeDtypeStruct((3,) + cs.shape[1:], BF16))]
            copies += _reduce_ici(len(arrays) - 2, len(arrays) - 1, only)
        return _Plan(tuple(arrays), tuple(copies))

    def after_ici(pieces, updated):
        for pos, (kind, L, _) in enumerate(pieces):
            stage[kind, L, "recv"] = updated[2 * pos + 1]

    def sum_layer(L):
        for kind in ("in", "out"):
            reduced[kind][L % 2] = _sum_chips(stage[kind, L], stage[kind, L, "recv"], reduced[kind][L % 2],
                                              L // 2, place_idx)

    def share_plan(L):
        arrays = (reduced["in"][L % 2], reduced["out"][L % 2])
        copies = _reduce_share(0, L // 2, k_in) + _reduce_share(1, L // 2, k_out)
        return _Plan(arrays, tuple(copies))

    def after_share(L, updated):
        reduced["in"][L % 2], reduced["out"][L % 2] = updated[0], updated[1]

    all_chips = (0, 1, 2)
    dmods = [None] * 4
    d_ab_norm_g, d_sg_norm_g = [None, None], [None, None]
    d_conv_w, d_ln_g, d_ln_b, d_w_s, d_b_s = ([None, None] for _ in range(5))
    for layer in reversed(range(4)):
        i = layer // 2
        prev = layer + 1
        busy = prev < 4
        if layer % 2 == 0:
            xs, h, proj, y, out, attn, lse = saved[layer]
        else:
            xs, h, uvz, y, out = saved[layer]
        below = (saved[layer - 1][4], gates[layer - 1]) if layer > 0 else None
        w2 = wout[layer].reshape(-1, d)
        grads["out", layer] = _out_proj_bwd_w(y, dout).reshape(N_CHIPS, -1, d)
        if busy:
            swapped = [("in", prev), ("out", prev)] + ([("out", 0)] if layer == 0 else [])
            dy, updated = _out_proj_bwd_act(dout, w2, plan=swap_plan(swapped))
            after_swap(swapped, updated)
        else:
            dy = _out_proj_bwd_act(dout, w2)
        if layer % 2 == 0:
            if busy:
                pieces = [("in", prev, all_chips), ("out", prev, all_chips)]
                pieces += [("out", 0, all_chips)] if layer == 0 else []
                dqkv, updated = _attn_bwd(proj, cos, sin, dy, attn, lse, aw, plan=ici_plan(pieces))
                after_ici(pieces, updated)
                sum_layer(prev)
            else:
                dqkv = _attn_bwd(proj, cos, sin, dy, attn, lse, aw)
            dact, d_conv_w[i] = _ab_dproj(dqkv, dy, attn, proj, conv_w_full[i], aw)
            if layer == 0:
                grads["in", 0], updated = _in_proj_bwd_w(h, dact, win[0].shape[-1], plan=share_plan(prev))
                after_share(prev, updated)
                plan = swap_plan([("in", 0)])
                after_swap([("in", 0)], _comm_stages("grads_to_sibling", plan.arrays, [plan.copies]))
                pieces = [("in", 0, all_chips)]
                dh, updated = _in_proj_bwd_act(dact, win[0], plan=ici_plan(pieces))
                after_ici(pieces, updated)
            elif busy:
                dh, updated = _in_proj_bwd_act(dact, win[layer], plan=share_plan(prev))
                after_share(prev, updated)
                grads["in", layer] = _in_proj_bwd_w(h, dact, win[layer].shape[-1])
            else:
                dh = _in_proj_bwd_act(dact, win[layer])
                grads["in", layer] = _in_proj_bwd_w(h, dact, win[layer].shape[-1])
            norm_g = ab_norm_g[i:i + 1]
        else:
            sgu_args = (uvz, dy, sg_ln_g_full[i:i + 1], sg_ln_b_full[i:i + 1], w_s16[i], b_s3[i], cw)
            if busy:
                pieces = [("in", prev, (0, 1)), ("out", prev, all_chips)]
                res, updated = _sgu_bwd(*sgu_args, plan=ici_plan(pieces))
                after_ici(pieces, updated)
            else:
                res = _sgu_bwd(*sgu_args)
            dact, d_w_s[i], db_wide, d_ln_g[i], d_ln_b[i] = res
            d_b_s[i] = db_wide[:, :, 0]
            if busy:
                pieces = [("in", prev, (2,))]
                dh, updated = _in_proj_bwd_act(dact, win[layer], plan=ici_plan(pieces))
                after_ici(pieces, updated)
                sum_layer(prev)
                grads["in", layer], updated = _in_proj_bwd_w(h, dact, win[layer].shape[-1], plan=share_plan(prev))
                after_share(prev, updated)
            else:
                dh = _in_proj_bwd_act(dact, win[layer])
                grads["in", layer] = _in_proj_bwd_w(h, dact, win[layer].shape[-1])
            norm_g = sg_norm_g_full[i:i + 1]
        res = _prenorm_bwd(xs, dh, dx, norm_g, scales[layer], below)
        dx, dshift, dscale, d_norm_g = res[:4]
        (d_ab_norm_g if layer % 2 == 0 else d_sg_norm_g)[i] = d_norm_g
        dmods[layer] = jnp.concatenate([dshift, dscale, dgate], axis=1)
        if below:
            dout, dgate = res[4:]
    grad_x = dx[None]

    partial_list = [jnp.concatenate(dmods, axis=0),
                    jnp.concatenate(d_ab_norm_g, axis=0), jnp.concatenate(d_sg_norm_g, axis=0), d_final_g[0],
                    jnp.stack(d_conv_w), jnp.concatenate(d_ln_g, axis=0), jnp.concatenate(d_ln_b, axis=0),
                    jnp.stack(d_w_s), jnp.stack(d_b_s)]
    partial_shapes = [a.shape for a in partial_list]
    partials, part_off = _pack(partial_list, PACKED_ROW_BLOCK)
    all_partials = _all_to_all(jnp.broadcast_to(partials[None], (N_DEV,) + partials.shape), "gather_partials")
    sum_layer(0)
    summed_packed, updated = _sum_devices(all_partials, plan=share_plan(0))
    after_share(0, updated)
    summed = _unpack(summed_packed, part_off, partial_shapes)
    (g_mod_bias, g_ab_norm_g, g_sg_norm_g_full, g_final_g, g_conv_full, g_ln_g_full, g_ln_b_full,
     g_w_s, g_b_s) = summed
    dm_all = jnp.stack([_unpack(all_partials[b], part_off[:1], partial_shapes[:1])[0] for b in range(N_DEV)])
    dm_local = lax.dynamic_slice_in_dim(dm_all, chip * mod_l, mod_l, axis=2)

    def chip_cols(a, axis):
        width = a.shape[axis] // N_CHIPS
        return lax.dynamic_slice_in_dim(a, chip * width, width, axis=axis)

    g_ab_b_mod = jnp.stack([g_mod_bias[0], g_mod_bias[2]])
    g_sg_b_mod = chip_cols(jnp.stack([g_mod_bias[1], g_mod_bias[3]]), 1)
    g_ab_w_mod = jnp.stack([_mod_bwd_w(c_all, dm_local[:, 0]), _mod_bwd_w(c_all, dm_local[:, 2])])
    g_sg_w_mod = jnp.stack([_mod_bwd_w(c_all, dm_local[:, 1]), _mod_bwd_w(c_all, dm_local[:, 3])])
    g_conv = chip_cols(g_conv_full, 2)
    g_sg_norm_g = chip_cols(g_sg_norm_g_full, 1)
    g_ln_g = chip_cols(g_ln_g_full, 1)
    g_ln_b = chip_cols(g_ln_b_full, 1)

    def step_big(w, g, m, v):
        dl, nm, nv = _adamw(_flat_rows(w), _flat_rows(g), _flat_rows(m), _flat_rows(v))
        return dl.reshape(w.shape), nm.reshape(w.shape), nv.reshape(w.shape)

    g_ab_w_in, g_sg_w_in = reduced["in"]
    g_ab_w_out, g_sg_w_out = reduced["out"]
    big_out = {
        "ab_w_mod": step_big(ab_w_mod, g_ab_w_mod, m_ab_w_mod, v_ab_w_mod),
        "ab_w_in": step_big(ab_w_in, g_ab_w_in, m_ab_w_in, v_ab_w_in),
        "ab_w_out": step_big(ab_w_out, g_ab_w_out, m_ab_w_out, v_ab_w_out),
        "sg_w_mod": step_big(sg_w_mod, g_sg_w_mod, m_sg_w_mod, v_sg_w_mod),
        "sg_w_in": step_big(sg_w_in, g_sg_w_in, m_sg_w_in, v_sg_w_in),
        "sg_w_out": step_big(sg_w_out, g_sg_w_out, m_sg_w_out, v_sg_w_out),
    }
    small_names = ["ab_norm_g", "ab_b_mod", "ab_conv_w", "sg_norm_g", "sg_b_mod", "sg_ln_g", "sg_ln_b",
                   "sg_w_s", "sg_b_s", "final_norm_g"]
    small_w = [ab_norm_g, ab_b_mod, ab_conv_w, sg_norm_g, sg_b_mod, sg_ln_g, sg_ln_b, sg_w_s, sg_b_s, final_norm_g]
    small_g = [g_ab_norm_g, g_ab_b_mod, g_conv, g_sg_norm_g, g_sg_b_mod, g_ln_g, g_ln_b, g_w_s, g_b_s, g_final_g]
    small_m = [m_ab_norm_g, m_ab_b_mod, m_ab_conv_w, m_sg_norm_g, m_sg_b_mod, m_sg_ln_g, m_sg_ln_b, m_sg_w_s,
               m_sg_b_s, m_final_norm_g]
    small_v = [v_ab_norm_g, v_ab_b_mod, v_ab_conv_w, v_sg_norm_g, v_sg_b_mod, v_sg_ln_g, v_sg_ln_b, v_sg_w_s,
               v_sg_b_s, v_final_norm_g]
    shapes = [a.shape for a in small_w]
    pw, off = _pack(small_w, PACKED_ROW_BLOCK)
    pg, _ = _pack(small_g, PACKED_ROW_BLOCK)
    pm, _ = _pack(small_m, PACKED_ROW_BLOCK)
    pv, _ = _pack(small_v, PACKED_ROW_BLOCK)
    pd, pnm, pnv = _adamw(pw, pg, pm, pv)
    small_out = {}
    for name, dl, nm, nv in zip(small_names, _unpack(pd, off, shapes), _unpack(pnm, off, shapes),
                                _unpack(pnv, off, shapes)):
        small_out[name] = (dl, nm, nv)

    grad_of = {
        "ab_norm_g": g_ab_norm_g, "ab_w_mod": g_ab_w_mod, "ab_b_mod": g_ab_b_mod, "ab_w_in": g_ab_w_in,
        "ab_conv_w": g_conv, "ab_w_out": g_ab_w_out, "sg_norm_g": g_sg_norm_g, "sg_w_mod": g_sg_w_mod,
        "sg_b_mod": g_sg_b_mod, "sg_w_in": g_sg_w_in, "sg_ln_g": g_ln_g, "sg_ln_b": g_ln_b, "sg_w_s": g_w_s,
        "sg_b_s": g_b_s, "sg_w_out": g_sg_w_out, "final_norm_g": g_final_g,
    }
    order = ["ab_norm_g", "ab_w_mod", "ab_b_mod", "ab_w_in", "ab_conv_w", "ab_w_out", "sg_norm_g", "sg_w_mod",
             "sg_b_mod", "sg_w_in", "sg_ln_g", "sg_ln_b", "sg_w_s", "sg_b_s", "sg_w_out", "final_norm_g"]
    steps = {**big_out, **small_out}
    return (loss, grad_x, *[grad_of[n] for n in order], *[steps[n][0] for n in order],
            *[steps[n][1] for n in order], *[steps[n][2] for n in order])
```

```python
import math
from typing import Any, Callable, NamedTuple

import jax
import jax.numpy as jnp
import numpy as np
from jax import lax
from jax.experimental import pallas as pl
from jax.experimental.pallas import tpu as pltpu

F32 = jnp.float32
BF16 = jnp.bfloat16

HEAD_DIM = 128
RADIUS = 64
DILATIONS = (1, 4, 16)
Q_BLOCK = 256
K_WINDOW = Q_BLOCK + 2 * RADIUS
ATTN_UNROLL = 4
ROPE_THETA = 10000.0
NEG_INF = -1e30
N_GROUPS = 8
CHUNK = 128
EPS = 1e-6
CONV_ROWS = 512
CONV_HALO = 16
LANES = 128
ELEMENTWISE_BLOCK = 512 * 1024
PACKED_ROW_BLOCK = 512
N_DEV = 8
N_CHIPS = 4

ADAM_LR = 0.001
ADAM_B1 = 0.9
ADAM_B2 = 0.999
ADAM_EPS = 1e-08
ADAM_WD = 0.01
ADAM_STEP = 10

VMEM_LIMIT_V7X = 56 * 1024 * 1024

MESH_ID = pl.DeviceIdType.MESH
HBM_SPEC = pl.BlockSpec(memory_space=pltpu.HBM)

NN = (((1,), (0,)), ((), ()))
NT = (((1,), (1,)), ((), ()))
TN = (((0,), (0,)), ((), ()))


def _params(n_grid, parallel=0):
    sem = tuple(["parallel"] * parallel + ["arbitrary"] * (n_grid - parallel))
    return pltpu.CompilerParams(dimension_semantics=sem, vmem_limit_bytes=VMEM_LIMIT_V7X)


def _tile(n, prefs):
    for p in prefs:
        if n % p == 0:
            return p
    return n


def _sigmoid(z):
    return 1.0 / (1.0 + jnp.exp(-z))


def _silu(z):
    return z * _sigmoid(z)


def _dsilu(z):
    s = _sigmoid(z)
    return s * (1.0 + z * (1.0 - s))


_GELU_K = math.sqrt(2.0 / math.pi)
_GELU_C = 0.044715


def _gelu(u):
    return 0.5 * u * (1.0 + jnp.tanh(_GELU_K * (u + _GELU_C * u * u * u)))


def _dgelu(u):
    t = jnp.tanh(_GELU_K * (u + _GELU_C * u * u * u))
    return 0.5 * (1.0 + t) + 0.5 * u * (1.0 - t * t) * _GELU_K * (1.0 + 3.0 * _GELU_C * u * u)


class _Place(NamedTuple):
    x: Any
    y: Any
    c: Any
    chip: Any


def _my_place():
    mx, my, mc = lax.axis_index("x"), lax.axis_index("y"), lax.axis_index("c")
    return _Place(mx, my, mc, 2 * mx + my)


def _other_chips(p):
    return [(1 - p.x, p.y), (p.x, 1 - p.y), (1 - p.x, 1 - p.y)]


class _Copy(NamedTuple):
    src: int
    src_at: Callable
    dst: int
    dst_at: Callable
    peer: Callable


class _Plan(NamedTuple):
    arrays: tuple
    copies: tuple


def _view(ref, index):
    return ref if index is None else ref.at[index]


def _plan_io(plan):
    ins = [k for k, a in enumerate(plan.arrays) if not isinstance(a, jax.ShapeDtypeStruct)]
    written = sorted({cp.dst for cp in plan.copies})
    return ins, written


def _descriptors(plan, in_refs, out_refs, send_sems, recv_sems):
    ins, written = _plan_io(plan)
    place = _my_place()
    return [
        pltpu.make_async_remote_copy(
            src_ref=_view(in_refs[ins.index(cp.src)], cp.src_at(place)),
            dst_ref=_view(out_refs[written.index(cp.dst)], cp.dst_at(place)),
            send_sem=send_sems.at[k], recv_sem=recv_sems.at[k],
            device_id=cp.peer(place), device_id_type=MESH_ID)
        for k, cp in enumerate(plan.copies)]


def _plan_operands(plan, n_in, n_out):
    ins, written = _plan_io(plan)
    operands = [plan.arrays[k] for k in ins]
    out_shape = [jax.ShapeDtypeStruct(plan.arrays[k].shape, plan.arrays[k].dtype) for k in written]
    aliases = {n_in + ins.index(k): n_out + pos for pos, k in enumerate(written) if k in ins}
    n = len(plan.copies)
    sems = [pltpu.SemaphoreType.DMA((n,)), pltpu.SemaphoreType.DMA((n,))]
    return operands, out_shape, aliases, sems, written


def _call(body, *, name, grid, in_specs, out_specs, out_shape, operands, scratch_shapes=(), aliases=None,
          parallel=0, plan=None):
    single = not isinstance(out_shape, (list, tuple))
    out_shape = [out_shape] if single else list(out_shape)
    out_specs = [out_specs] if single else list(out_specs)
    if plan is None:
        res = pl.pallas_call(
            body, name=name, grid=grid, in_specs=list(in_specs), out_specs=out_specs, out_shape=out_shape,
            scratch_shapes=list(scratch_shapes), input_output_aliases=aliases or {},
            compiler_params=_params(len(grid), parallel=parallel),
        )(*operands)
        return res[0] if single else res

    n_in, n_out, n_scr = len(operands), len(out_shape), len(scratch_shapes)
    p_operands, p_out_shape, p_aliases, sems, written = _plan_operands(plan, n_in, n_out)
    n_pin, n_pout = len(p_operands), len(p_out_shape)

    def wrapped(*refs):
        ins = refs[:n_in]
        p_in = refs[n_in:n_in + n_pin]
        outs = refs[n_in + n_pin:n_in + n_pin + n_out]
        p_out = refs[n_in + n_pin + n_out:n_in + n_pin + n_out + n_pout]
        scratch = refs[n_in + n_pin + n_out + n_pout:n_in + n_pin + n_out + n_pout + n_scr]
        send_sems, recv_sems = refs[-2:]
        ids = [pl.program_id(a) for a in range(len(grid))]
        first = ids[0] == 0
        last = ids[0] == grid[0] - 1
        for a in range(1, len(grid)):
            first = jnp.logical_and(first, ids[a] == 0)
            last = jnp.logical_and(last, ids[a] == grid[a] - 1)

        @pl.when(first)
        def _():
            for cp in _descriptors(plan, p_in, p_out, send_sems, recv_sems):
                cp.start()

        body(*ins, *outs, *scratch)

        @pl.when(last)
        def _():
            for cp in _descriptors(plan, p_in, p_out, send_sems, recv_sems):
                cp.wait()

    res = pl.pallas_call(
        wrapped, name=name, grid=grid,
        in_specs=list(in_specs) + [HBM_SPEC] * n_pin,
        out_specs=out_specs + [HBM_SPEC] * n_pout,
        out_shape=out_shape + p_out_shape,
        scratch_shapes=list(scratch_shapes) + sems,
        input_output_aliases={**(aliases or {}), **p_aliases},
        compiler_params=_params(len(grid)),
    )(*operands, *p_operands)
    outs = res[0] if single else res[:n_out]
    return outs, dict(zip(written, res[n_out:]))


def _comm_stages(name, arrays, stages):
    plan = _Plan(tuple(arrays), tuple(cp for st in stages for cp in st))
    p_operands, p_out_shape, p_aliases, sems, written = _plan_operands(plan, 0, 0)
    n_pin = len(p_operands)

    def body(*refs):
        p_in = refs[:n_pin]
        p_out = refs[n_pin:n_pin + len(written)]
        send_sems, recv_sems = refs[-2:]
        all_copies = _descriptors(plan, p_in, p_out, send_sems, recv_sems)
        base = 0
        for st in stages:
            for cp in all_copies[base:base + len(st)]:
                cp.start()
            for cp in all_copies[base:base + len(st)]:
                cp.wait()
            base += len(st)

    res = pl.pallas_call(
        body, name=name, in_specs=[HBM_SPEC] * n_pin, out_specs=[HBM_SPEC] * len(written),
        out_shape=p_out_shape, scratch_shapes=sems, input_output_aliases=p_aliases,
    )(*p_operands)
    return dict(zip(written, res))


def _half_rows(k, c):
    return pl.ds(c * (k // 2), k // 2)


def _gather_ici(a, k):
    own = lambda p: (p.chip, _half_rows(k, p.c))
    return [_Copy(a, own, a, own, lambda p, q=q: (*_other_chips(p)[q], p.c)) for q in range(3)]


def _gather_pass_on(a, k):
    def at(q):
        def index(p):
            px, py = _other_chips(p)[q]
            return (2 * px + py, _half_rows(k, p.c))
        return index
    return [_Copy(a, at(q), a, at(q), lambda p: (p.x, p.y, 1 - p.c)) for q in range(3)]


def _reduce_swap(src, dst, k):
    return [_Copy(src, lambda p: (pl.ds(0, N_CHIPS), _half_rows(k, 1 - p.c)), dst, lambda p: None,
                  lambda p: (p.x, p.y, 1 - p.c))]


def _reduce_ici(src, dst, only=(0, 1, 2)):
    def slab(q):
        def index(p):
            px, py = _other_chips(p)[q]
            return 2 * px + py
        return index
    return [_Copy(src, slab(q), dst, lambda p, q=q: q, lambda p, q=q: (*_other_chips(p)[q], p.c)) for q in only]


def _reduce_share(a, layer, k):
    at = lambda p: (layer, _half_rows(k, p.c))
    return [_Copy(a, at, a, at, lambda p: (p.x, p.y, 1 - p.c))]


def _all_to_all(x, name):
    def body(x_ref, y_ref, send_sems, recv_sems, own_sem):
        p = _my_place()
        me = 2 * p.chip + p.c
        own = pltpu.make_async_copy(x_ref.at[me], y_ref.at[me], own_sem)
        own.start()
        copies = []
        for k in range(1, N_DEV):
            px = 1 - p.x if (k >> 2) & 1 else p.x
            py = 1 - p.y if (k >> 1) & 1 else p.y
            pc = 1 - p.c if k & 1 else p.c
            peer = 4 * px + 2 * py + pc
            cp = pltpu.make_async_remote_copy(
                src_ref=x_ref.at[peer], dst_ref=y_ref.at[me],
                send_sem=send_sems.at[k - 1], recv_sem=recv_sems.at[k - 1],
                device_id=(px, py, pc), device_id_type=MESH_ID)
            cp.start()
            copies.append(cp)
        for cp in copies:
            cp.wait()
        own.wait()

    return pl.pallas_call(
        body, name=name,
        out_shape=jax.ShapeDtypeStruct(x.shape, x.dtype),
        in_specs=[HBM_SPEC], out_specs=HBM_SPEC,
        scratch_shapes=[pltpu.SemaphoreType.DMA((N_DEV - 1,)), pltpu.SemaphoreType.DMA((N_DEV - 1,)),
                        pltpu.SemaphoreType.DMA],
    )(x)


def _place_own_shard(w, layer, chip_idx):
    _, k, n = w.shape
    tr = _tile(k, (512, 256, 128))

    def body(c_ref, w_ref, g_ref):
        g_ref[...] = w_ref[...].astype(BF16)

    return pl.pallas_call(
        body, name="place_own_shard",
        grid_spec=pltpu.PrefetchScalarGridSpec(
            num_scalar_prefetch=1, grid=(k // tr,),
            in_specs=[pl.BlockSpec((None, tr, n), lambda r, c: (layer, r, 0))],
            out_specs=pl.BlockSpec((None, tr, n), lambda r, c: (c[0], r, 0))),
        out_shape=jax.ShapeDtypeStruct((N_CHIPS, k, n), BF16),
        compiler_params=_params(1, parallel=1),
    )(chip_idx, w)


def _matmul(name, operands, in_specs, grid, dims, out_shape, out_specs, epilogue, a_prologue=None,
            aliases=None, plan=None):
    n_in = len(operands)

    def body(*refs):
        a = refs[0][...]
        if a_prologue is not None:
            a = a_prologue(a)
        acc = lax.dot_general(a.astype(BF16), refs[1][...].astype(BF16), dims, preferred_element_type=F32)
        epilogue(acc, refs[2:n_in], refs[n_in:])

    return _call(body, name=name, grid=grid, in_specs=in_specs, out_specs=out_specs, out_shape=out_shape,
                 operands=operands, aliases=aliases, parallel=2, plan=plan)


def _store_cast(acc, extra, outs):
    outs[0][...] = acc.astype(outs[0].dtype)


def _in_proj(h, w, plan=None):
    s, d = h.shape
    nl = w.shape[-1]
    tm = _tile(s, (1024, 512, 256))
    tn = _tile(nl, (1024, 768, 512, 384, 256, 128))
    per = nl // tn
    return _matmul(
        "in_proj", (h, w),
        [pl.BlockSpec((tm, d), lambda i, j: (i, 0)),
         pl.BlockSpec((None, d, tn), lambda i, j: (j // per, 0, j % per))],
        (s // tm, N_CHIPS * per), NN,
        jax.ShapeDtypeStruct((s, N_CHIPS * nl), BF16),
        pl.BlockSpec((tm, tn), lambda i, j: (i, j)), _store_cast, plan=plan)


def _out_proj_residual(y, w2, x, gate, plan=None):
    s, wdt = y.shape
    d = w2.shape[-1]
    tm = _tile(s, (1024, 512, 256))
    tn = _tile(d, (1024, 512, 256, 128))

    def epilogue(acc, extra, outs):
        x_ref, gate_ref = extra
        outs[0][...] = x_ref[...] + gate_ref[...] * acc
        outs[1][...] = acc.astype(BF16)

    blk = pl.BlockSpec((tm, tn), lambda i, j: (i, j))
    return _matmul(
        "out_proj", (y, w2, x, gate),
        [pl.BlockSpec((tm, wdt), lambda i, j: (i, 0)),
         pl.BlockSpec((wdt, tn), lambda i, j: (0, j)),
         blk, pl.BlockSpec((1, tn), lambda i, j: (0, j))],
        (s // tm, d // tn), NN,
        [jax.ShapeDtypeStruct((s, d), F32), jax.ShapeDtypeStruct((s, d), BF16)],
        [blk, blk], epilogue, plan=plan)


def _out_proj_bwd_act(dout, w2, plan=None):
    s, d = dout.shape
    wdt = w2.shape[0]
    tm = _tile(s, (1024, 512, 256))
    tn = _tile(wdt, (1024, 512, 256, 128))
    return _matmul(
        "out_proj_dy", (dout, w2),
        [pl.BlockSpec((tm, d), lambda i, j: (i, 0)),
         pl.BlockSpec((tn, d), lambda i, j: (j, 0))],
        (s // tm, wdt // tn), NT,
        jax.ShapeDtypeStruct((s, wdt), BF16),
        pl.BlockSpec((tm, tn), lambda i, j: (i, j)), _store_cast, plan=plan)


def _out_proj_bwd_w(y, dout, plan=None):
    s, wdt = y.shape
    d = dout.shape[1]
    tm = _tile(wdt, (1024, 512, 256, 128))
    tn = _tile(d, (1024, 512, 256, 128))
    return _matmul(
        "out_proj_dw", (y, dout),
        [pl.BlockSpec((s, tm), lambda i, j: (0, i)),
         pl.BlockSpec((s, tn), lambda i, j: (0, j))],
        (wdt // tm, d // tn), TN,
        jax.ShapeDtypeStruct((wdt, d), BF16),
        pl.BlockSpec((tm, tn), lambda i, j: (i, j)), _store_cast, plan=plan)


def _in_proj_bwd_act(dproj, w, plan=None):
    s, n_all = dproj.shape
    d, nl = w.shape[1], w.shape[2]
    tm = _tile(s, (1024, 512, 256))
    tn = _tile(d, (512, 256, 128))

    def body(a_ref, w_ref, o_ref):
        acc = None
        for q in range(N_CHIPS):
            part = lax.dot_general(a_ref[:, q * nl:(q + 1) * nl], w_ref[q], NT, preferred_element_type=F32)
            acc = part if acc is None else acc + part
        o_ref[...] = acc.astype(BF16)

    return _call(
        body, name="in_proj_dh", grid=(s // tm, d // tn),
        in_specs=[pl.BlockSpec((tm, n_all), lambda i, j: (i, 0), pipeline_mode=pl.Buffered(1)),
                  pl.BlockSpec((N_CHIPS, tn, nl), lambda i, j: (0, j, 0))],
        out_specs=pl.BlockSpec((tm, tn), lambda i, j: (i, j)),
        out_shape=jax.ShapeDtypeStruct((s, d), BF16),
        operands=(dproj, w), parallel=2, plan=plan)


def _in_proj_bwd_w(h, dproj, nl, plan=None):
    s, d = h.shape
    tm = _tile(d, (1024, 512, 256, 128))
    tn = _tile(nl, (1024, 768, 512, 384, 256, 128))
    per = nl // tn
    return _matmul(
        "in_proj_dw", (h, dproj),
        [pl.BlockSpec((s, tm), lambda i, j: (0, i)),
         pl.BlockSpec((s, tn), lambda i, j: (0, j))],
        (d // tm, N_CHIPS * per), TN,
        jax.ShapeDtypeStruct((N_CHIPS, d, nl), BF16),
        pl.BlockSpec((None, tm, tn), lambda i, j: (j // per, i, j % per)), _store_cast, plan=plan)


def _mod_fwd(c_all, w_mod, bias, layer):
    nb, d = c_all.shape
    nl = w_mod.shape[-1]
    tn = _tile(nl, (768, 512, 384, 256, 128))

    def epilogue(acc, extra, outs):
        outs[0][...] = acc + extra[0][...]

    return _matmul(
        "mod_fwd", (c_all, w_mod, bias),
        [pl.BlockSpec((nb, d), lambda i, j: (0, 0)),
         pl.BlockSpec((None, d, tn), lambda i, j: (layer, 0, j)),
         pl.BlockSpec((1, tn), lambda i, j: (0, j))],
        (1, nl // tn), NN,
        jax.ShapeDtypeStruct((nb, nl), F32),
        pl.BlockSpec((nb, tn), lambda i, j: (0, j)), epilogue, a_prologue=_silu)


def _mod_bwd_w(c_all, dm_local):
    nb, d = c_all.shape
    nl = dm_local.shape[-1]
    tm = _tile(d, (1024, 512, 256, 128))
    tn = _tile(nl, (768, 512, 384, 256, 128))

    def epilogue(acc, extra, outs):
        outs[0][...] = acc

    return _matmul(
        "mod_dw", (c_all, dm_local),
        [pl.BlockSpec((nb, tm), lambda i, j: (0, i)),
         pl.BlockSpec((nb, tn), lambda i, j: (0, j))],
        (d // tm, nl // tn), TN,
        jax.ShapeDtypeStruct((d, nl), F32),
        pl.BlockSpec((tm, tn), lambda i, j: (i, j)), epilogue, a_prologue=_silu)


def _rows_call(name, body, operands, in_specs, out_shape, out_specs, n_tiles):
    return pl.pallas_call(
        body, name=name, grid=(n_tiles,), in_specs=in_specs, out_specs=out_specs, out_shape=out_shape,
        compiler_params=_params(1),
    )(*operands)


def _row_spec(tr, width):
    return pl.BlockSpec((tr, width), lambda i: (i, 0))


def _vec_spec(width):
    return pl.BlockSpec((1, width), lambda i: (0, 0))


def _accumulate(ref, val):
    first = pl.program_id(0) == 0

    @pl.when(first)
    def _():
        ref[...] = val

    @pl.when(jnp.logical_not(first))
    def _():
        ref[...] += val


def _prenorm(x, g, scale, shift):
    s, d = x.shape
    tr = _tile(s, (256, 128))

    def body(x_ref, g_ref, sc_ref, sh_ref, h_ref):
        xv = x_ref[...]
        rstd = lax.rsqrt(jnp.mean(xv * xv, axis=-1, keepdims=True) + EPS)
        h_ref[...] = ((xv * rstd) * g_ref[...] * (1.0 + sc_ref[...]) + sh_ref[...]).astype(BF16)

    return _rows_call("prenorm", body, (x, g, scale, shift),
                      [_row_spec(tr, d), _vec_spec(d), _vec_spec(d), _vec_spec(d)],
                      jax.ShapeDtypeStruct((s, d), BF16), _row_spec(tr, d), s // tr)


def _gate_grads(dxv, out_ref, gate_ref, dout_ref, dgate_ref):
    dout_ref[...] = (gate_ref[...] * dxv).astype(BF16)
    _accumulate(dgate_ref, jnp.sum(dxv * out_ref[...].astype(F32), axis=0, keepdims=True))


def _prenorm_bwd(x, dh, dres, g, scale, below=None):
    s, d = x.shape
    tr = _tile(s, (256, 128))

    def body(x_ref, dh_ref, dres_ref, g_ref, sc_ref, *rest):
        dx_ref, dshift_ref, dscale_ref, dg_ref = rest[-6:-2] if below else rest
        xv = x_ref[...]
        dhv = dh_ref[...].astype(F32)
        rstd = lax.rsqrt(jnp.mean(xv * xv, axis=-1, keepdims=True) + EPS)
        xhat = xv * rstd
        gv = g_ref[...]
        one_sc = 1.0 + sc_ref[...]
        dxhat = dhv * gv * one_sc
        dxv = dres_ref[...] + rstd * (dxhat - xhat * jnp.mean(dxhat * xhat, axis=-1, keepdims=True))
        dx_ref[...] = dxv
        _accumulate(dshift_ref, jnp.sum(dhv, axis=0, keepdims=True))
        _accumulate(dscale_ref, jnp.sum(dhv * xhat * gv, axis=0, keepdims=True))
        _accumulate(dg_ref, jnp.sum(dhv * xhat * one_sc, axis=0, keepdims=True))
        if below:
            _gate_grads(dxv, rest[0], rest[1], rest[-2], rest[-1])

    vec = jax.ShapeDtypeStruct((1, d), F32)
    operands = (x, dh, dres, g, scale) + (tuple(below) if below else ())
    in_specs = [_row_spec(tr, d), _row_spec(tr, d), _row_spec(tr, d), _vec_spec(d), _vec_spec(d)]
    out_shape = [jax.ShapeDtypeStruct((s, d), F32), vec, vec, vec]
    out_specs = [_row_spec(tr, d), _vec_spec(d), _vec_spec(d), _vec_spec(d)]
    if below:
        in_specs += [_row_spec(tr, d), _vec_spec(d)]
        out_shape += [jax.ShapeDtypeStruct((s, d), BF16), vec]
        out_specs += [_row_spec(tr, d), _vec_spec(d)]
    return _rows_call("prenorm_bwd", body, operands, in_specs, out_shape, out_specs, s // tr)


def _final_loss(x, target, g, out_below, gate_below):
    s, d = x.shape
    tr = _tile(s, (256, 128))
    n_tiles = s // tr

    def body(x_ref, t_ref, g_ref, out_ref, gate_ref, loss_ref, dx_ref, dg_ref, dout_ref, dgate_ref, acc_ref):
        xv = x_ref[...]
        rstd = lax.rsqrt(jnp.mean(xv * xv, axis=-1, keepdims=True) + EPS)
        xhat = xv * rstd
        gv = g_ref[...]
        err = xhat * gv - t_ref[...]
        dy = err * (1.0 / d)
        dxhat = dy * gv
        dxv = rstd * (dxhat - xhat * jnp.mean(dxhat * xhat, axis=-1, keepdims=True))
        dx_ref[...] = dxv
        _accumulate(dg_ref, jnp.sum(dy * xhat, axis=0, keepdims=True))
        _accumulate(acc_ref, jnp.sum(err * err, axis=0, keepdims=True))
        _gate_grads(dxv, out_ref, gate_ref, dout_ref, dgate_ref)

        @pl.when(pl.program_id(0) == n_tiles - 1)
        def _():
            loss_ref[...] = (0.5 / d) * jnp.sum(acc_ref[...], axis=1, keepdims=True)

    vec = jax.ShapeDtypeStruct((1, d), F32)
    return pl.pallas_call(
        body, name="final_loss", grid=(n_tiles,),
        in_specs=[_row_spec(tr, d), _row_spec(tr, d), _vec_spec(d), _row_spec(tr, d), _vec_spec(d)],
        out_specs=[pl.BlockSpec((1, 1), lambda i: (0, 0)), _row_spec(tr, d), _vec_spec(d), _row_spec(tr, d),
                   _vec_spec(d)],
        out_shape=[jax.ShapeDtypeStruct((1, 1), F32), jax.ShapeDtypeStruct((s, d), F32), vec,
                   jax.ShapeDtypeStruct((s, d), BF16), vec],
        scratch_shapes=[pltpu.VMEM((1, d), F32)],
        compiler_params=_params(1),
    )(x, target, g, out_below, gate_below)


def _rope(t, cos, sin):
    return t * cos + pltpu.roll(t, HEAD_DIM // 2, axis=1) * sin


def _unrope(dt, cos, sin):
    return dt * cos + pltpu.roll(dt * sin, HEAD_DIM // 2, axis=1)


def _band_blocks(s, dil):
    sub = s // dil
    kw = min(K_WINDOW, sub)

    def rows(r, start, n):
        if dil == 1:
            return pl.ds(pl.multiple_of(start, RADIUS), n)
        return pl.ds(r + dil * start, n, stride=dil)

    def window(idx):
        nb = sub // Q_BLOCK
        r, b = idx // nb, idx % nb
        q0 = b * Q_BLOCK
        start = jnp.clip(q0 - RADIUS, 0, sub - kw)
        ahead = (lax.broadcasted_iota(jnp.int32, (Q_BLOCK, kw), 1)
                 - lax.broadcasted_iota(jnp.int32, (Q_BLOCK, kw), 0)) + (start - q0 + RADIUS)
        valid = lax.bitcast_convert_type(ahead, jnp.uint32) <= 2 * RADIUS
        return rows(r, q0, Q_BLOCK), rows(r, start, kw), valid

    return window


def _store_column_tiles(tiles, dst_ref, sems, col_blocks):
    rows = tiles.shape[1]
    copies = []
    for g, cb in enumerate(col_blocks):
        cols = pl.ds(pl.multiple_of(cb * LANES, LANES), LANES)
        cp = pltpu.make_async_copy(tiles.at[g], dst_ref.at[pl.ds(0, rows), cols], sems.at[g])
        cp.start()
        copies.append(cp)
    for cp in copies:
        cp.wait()


def _head_col(s, group, nh):
    return pl.BlockSpec((s, HEAD_DIM), lambda h: (0, group * nh + h), pipeline_mode=pl.Buffered(1))


def _attn_fwd(proj, cos, sin, aw, plan=None):
    s = proj.shape[0]
    nh = aw // HEAD_DIM
    scale = HEAD_DIM ** -0.5
    n_blocks = s // Q_BLOCK

    def body(q_ref, k_ref, v_ref, cos_ref, sin_ref, attn_ref, lse_ref, qf, kf, vf, acc):
        cosv, sinv = cos_ref[...], sin_ref[...]
        qf[...] = _rope(q_ref[...].astype(F32), cosv, sinv) * scale
        kf[...] = _rope(k_ref[...].astype(F32), cosv, sinv)
        vf[...] = v_ref[...].astype(F32)

        for pattern, dil in enumerate(DILATIONS):
            window = _band_blocks(s, dil)

            def block(idx, carry, window=window, first=(pattern == 0)):
                q_rows, k_rows, valid = window(idx)
                q = qf[q_rows, :].astype(BF16)
                kk = kf[k_rows, :].astype(BF16)
                vv = vf[k_rows, :].astype(BF16)
                sc = lax.dot_general(q, kk, NT, preferred_element_type=F32)
                sc = jnp.where(valid, sc, NEG_INF)
                m = jnp.max(sc, axis=1, keepdims=True)
                p = jnp.exp(sc - m)
                den = jnp.sum(p, axis=1, keepdims=True)
                o = lax.dot_general(p.astype(BF16), vv, NN, preferred_element_type=F32) / den
                lse = jnp.broadcast_to(m + jnp.log(den), (Q_BLOCK, HEAD_DIM))
                if first:
                    acc[q_rows, :] = o
                    lse_ref[q_rows, :] = lse
                else:
                    lse_old = lse_ref[q_rows, :]
                    top = jnp.maximum(lse_old, lse)
                    w_old, w_new = jnp.exp(lse_old - top), jnp.exp(lse - top)
                    tot = w_old + w_new
                    acc[q_rows, :] = (acc[q_rows, :] * w_old + o * w_new) / tot
                    lse_ref[q_rows, :] = top + jnp.log(tot)
                return carry

            lax.fori_loop(0, n_blocks, block, 0, unroll=ATTN_UNROLL)

        attn_ref[...] = acc[...].astype(BF16)

    table = pl.BlockSpec((s, HEAD_DIM), lambda h: (0, 0), pipeline_mode=pl.Buffered(1))
    out = pl.BlockSpec((s, HEAD_DIM), lambda h: (0, h))
    return _call(
        body, name="attn_fwd", grid=(nh,),
        in_specs=[_head_col(s, 0, nh), _head_col(s, 1, nh), _head_col(s, 2, nh), table, table],
        out_specs=[out, out],
        out_shape=[jax.ShapeDtypeStruct((s, aw), BF16), jax.ShapeDtypeStruct((s, aw), F32)],
        scratch_shapes=[pltpu.VMEM((s, HEAD_DIM), F32)] * 4,
        operands=(proj, proj, proj, cos, sin), parallel=1, plan=plan)


def _attn_bwd(proj, cos, sin, dy, attn, lse, aw, plan=None):
    s = proj.shape[0]
    nh = aw // HEAD_DIM
    scale = HEAD_DIM ** -0.5
    n_blocks = s // Q_BLOCK

    def body(q_ref, k_ref, v_ref, za_ref, cos_ref, sin_ref, dy_ref, attn_ref, lse_ref,
             dproj_ref, qf, kf, vf, dof, delta, dqa, dka, dva, tiles, tile_sems):
        cosv, sinv = cos_ref[...], sin_ref[...]
        qf[...] = _rope(q_ref[...].astype(F32), cosv, sinv) * scale
        kf[...] = _rope(k_ref[...].astype(F32), cosv, sinv)
        vf[...] = v_ref[...].astype(F32)
        dyv, zav, attnv = dy_ref[...].astype(F32), za_ref[...].astype(F32), attn_ref[...].astype(F32)
        do_all = dyv * _silu(zav)
        dof[...] = do_all
        tiles[3] = (dyv * attnv * _dsilu(zav)).astype(BF16)
        delta[...] = jnp.broadcast_to(jnp.sum(do_all * attnv, axis=1, keepdims=True), (s, HEAD_DIM))
        dqa[...] = jnp.zeros_like(dqa)
        dka[...] = jnp.zeros_like(dka)
        dva[...] = jnp.zeros_like(dva)

        for dil in DILATIONS:
            window = _band_blocks(s, dil)

            def block(idx, carry, window=window):
                q_rows, k_rows, valid = window(idx)
                q = qf[q_rows, :].astype(BF16)
                kk = kf[k_rows, :].astype(BF16)
                vv = vf[k_rows, :].astype(BF16)
                dov = dof[q_rows, :].astype(BF16)
                lse_q = lse_ref[q_rows, :][:, 0:1]
                delta_q = delta[q_rows, :][:, 0:1]
                sc = lax.dot_general(q, kk, NT, preferred_element_type=F32)
                p = jnp.where(valid, jnp.exp(sc - lse_q), 0.0)
                dp = lax.dot_general(dov, vv, NT, preferred_element_type=F32)
                ds = (p * (dp - delta_q)).astype(BF16)
                dqa[q_rows, :] += lax.dot_general(ds, kk, NN, preferred_element_type=F32)
                dka[k_rows, :] += lax.dot_general(ds, q, TN, preferred_element_type=F32)
                dva[k_rows, :] += lax.dot_general(p.astype(BF16), dov, TN, preferred_element_type=F32)
                return carry

            lax.fori_loop(0, n_blocks, block, 0, unroll=ATTN_UNROLL)

        tiles[0] = (_unrope(dqa[...], cosv, sinv) * scale).astype(BF16)
        tiles[1] = _unrope(dka[...], cosv, sinv).astype(BF16)
        tiles[2] = dva[...].astype(BF16)
        _store_column_tiles(tiles, dproj_ref, tile_sems, [g * nh + pl.program_id(0) for g in range(4)])

    own = pl.BlockSpec((s, HEAD_DIM), lambda h: (0, h), pipeline_mode=pl.Buffered(1))
    table = pl.BlockSpec((s, HEAD_DIM), lambda h: (0, 0), pipeline_mode=pl.Buffered(1))
    return _call(
        body, name="attn_bwd", grid=(nh,),
        in_specs=[_head_col(s, 0, nh), _head_col(s, 1, nh), _head_col(s, 2, nh), _head_col(s, 3, nh),
                  table, table, own, own, own],
        out_specs=HBM_SPEC,
        out_shape=jax.ShapeDtypeStruct((s, 8 * aw), BF16),
        scratch_shapes=[pltpu.VMEM((s, HEAD_DIM), F32)] * 8 + [
            pltpu.VMEM((4, s, HEAD_DIM), BF16), pltpu.SemaphoreType.DMA((4,))],
        operands=(proj, proj, proj, proj, cos, sin, dy, attn, lse), plan=plan)


def _rope_tables(s):
    half = HEAD_DIM // 2
    inv = np.float32(ROPE_THETA) ** (-np.arange(half, dtype=np.float32) / np.float32(half))
    ang = np.arange(s, dtype=np.float32)[:, None] * inv[None, :]
    cos, sin = np.cos(ang), np.sin(ang)
    return (jnp.asarray(np.concatenate([cos, cos], axis=-1), F32),
            jnp.asarray(np.concatenate([-sin, sin], axis=-1), F32))


def _conv_chunks(s):
    for k in range(s // CONV_ROWS):
        lo = max(0, k * CONV_ROWS - CONV_HALO)
        hi = min(s, (k + 1) * CONV_ROWS + CONV_HALO)
        yield k * CONV_ROWS, lo, hi


def _neighbours(p, lo, s):
    n = p.shape[0]
    row = lo + lax.broadcasted_iota(jnp.int32, p.shape, 0)
    prev = jnp.where(row == 0, 0.0, pltpu.roll(p, 1, axis=0))
    nxt = jnp.where(row == s - 1, 0.0, pltpu.roll(p, n - 1, axis=0))
    return prev, nxt


def _ab_mix(attn, proj, conv_w, aw):
    s = proj.shape[0]
    nt = aw // LANES

    def col(group, sel):
        return pl.BlockSpec((s, LANES), lambda i: (0, group * nt + sel(i)))

    a_sel = lambda i: jnp.minimum(i, nt - 1)
    b_sel = lambda i: jnp.maximum(i - nt, 0)

    def body(attn_ref, za_ref, ub_ref, gb_ref, gc_ref, zb_ref, w_ref, y_ref):
        i = pl.program_id(0)

        @pl.when(i < nt)
        def _():
            y_ref[...] = (attn_ref[...].astype(F32) * _silu(za_ref[...].astype(F32))).astype(BF16)

        @pl.when(i >= nt)
        def _():
            w = w_ref[...]
            for c0, lo, hi in _conv_chunks(s):
                p = gc_ref[lo:hi, :].astype(F32) * ub_ref[lo:hi, :].astype(F32)
                prev, nxt = _neighbours(p, lo, s)
                cv = w[0:1, :] * prev + w[1:2, :] * p + w[2:3, :] * nxt
                yb = gb_ref[lo:hi, :].astype(F32) * cv * _silu(zb_ref[lo:hi, :].astype(F32))
                y_ref[c0:c0 + CONV_ROWS, :] = yb[c0 - lo:c0 - lo + CONV_ROWS, :].astype(BF16)

    return pl.pallas_call(
        body, name="ab_mix", grid=(2 * nt,),
        in_specs=[pl.BlockSpec((s, LANES), lambda i: (0, a_sel(i))),
                  col(3, a_sel), col(4, b_sel), col(5, b_sel), col(6, b_sel), col(7, b_sel),
                  pl.BlockSpec((3, LANES), lambda i: (0, b_sel(i)))],
        out_specs=pl.BlockSpec((s, LANES), lambda i: (0, i)),
        out_shape=jax.ShapeDtypeStruct((s, 2 * aw), BF16),
        compiler_params=_params(1),
    )(attn, proj, proj, proj, proj, proj, conv_w)


def _conv_bwd(dproj, dy, proj, conv_w, aw):
    s = proj.shape[0]
    nt = aw // LANES

    def col(group):
        return pl.BlockSpec((s, LANES), lambda i: (0, group * nt + i))

    def body(dyb_ref, ub_ref, gb_ref, gc_ref, zb_ref, w_ref, dproj_in, dproj_ref, dw_ref, tiles, tile_sems):
        w = w_ref[...]
        dw = [jnp.zeros((1, LANES), F32) for _ in range(3)]
        for c0, lo, hi in _conv_chunks(s):
            ctr = slice(c0 - lo, c0 - lo + CONV_ROWS)
            out_rows = slice(c0, c0 + CONV_ROWS)
            ub = ub_ref[lo:hi, :].astype(F32)
            gc = gc_ref[lo:hi, :].astype(F32)
            gb = gb_ref[lo:hi, :].astype(F32)
            zb = zb_ref[lo:hi, :].astype(F32)
            dyb = dyb_ref[lo:hi, :].astype(F32)
            p = gc * ub
            prev, nxt = _neighbours(p, lo, s)
            cv = w[0:1, :] * prev + w[1:2, :] * p + w[2:3, :] * nxt
            sz = _silu(zb)
            dcv = dyb * gb * sz
            dprev, dnxt = _neighbours(dcv, lo, s)
            dp = w[0:1, :] * dnxt + w[1:2, :] * dcv + w[2:3, :] * dprev
            for t, nb in enumerate((prev, p, nxt)):
                dw[t] = dw[t] + jnp.sum((dcv * nb)[ctr, :], axis=0, keepdims=True)
            tiles[0, out_rows, :] = (dp * gc)[ctr, :].astype(BF16)
            tiles[1, out_rows, :] = (dyb * cv * sz)[ctr, :].astype(BF16)
            tiles[2, out_rows, :] = (dp * ub)[ctr, :].astype(BF16)
            tiles[3, out_rows, :] = (dyb * gb * cv * _dsilu(zb))[ctr, :].astype(BF16)
        dw_ref[...] = jnp.concatenate(dw, axis=0)
        _store_column_tiles(tiles, dproj_ref, tile_sems, [(4 + g) * nt + pl.program_id(0) for g in range(4)])

    return pl.pallas_call(
        body, name="conv_bwd", grid=(nt,),
        in_specs=[pl.BlockSpec((s, LANES), lambda i: (0, nt + i)),
                  col(4), col(5), col(6), col(7),
                  pl.BlockSpec((3, LANES), lambda i: (0, i)), HBM_SPEC],
        out_specs=[HBM_SPEC, pl.BlockSpec((3, LANES), lambda i: (0, i))],
        out_shape=[jax.ShapeDtypeStruct(dproj.shape, dproj.dtype), jax.ShapeDtypeStruct((3, aw), F32)],
        scratch_shapes=[pltpu.VMEM((4, s, LANES), BF16), pltpu.SemaphoreType.DMA((4,))],
        input_output_aliases={6: 0},
        compiler_params=_params(1),
    )(dy, proj, proj, proj, proj, conv_w, dproj)


def _sgu_norm(v, ln_g, ln_b):
    gv = _gelu(v)
    mu = jnp.mean(gv, axis=-1, keepdims=True)
    xc = gv - mu
    rstd = lax.rsqrt(jnp.mean(xc * xc, axis=-1, keepdims=True) + EPS)
    vhat = xc * rstd
    return vhat, rstd, vhat * ln_g + ln_b


def _sgu_fwd(uvz, ln_g, ln_b, w_s, b_s, cw):
    s = uvz.shape[0]
    tr = 2 * CHUNK if s % (2 * CHUNK) == 0 else CHUNK
    gw = cw // N_GROUPS

    def body(u_ref, v_ref, z_ref, g_ref, b_ref, ws_ref, bs_ref, y_ref):
        _, _, vn = _sgu_norm(v_ref[...].astype(F32), g_ref[...], b_ref[...])
        vn = vn.astype(BF16)
        for ch in range(tr // CHUNK):
            rows = slice(ch * CHUNK, (ch + 1) * CHUNK)
            for grp in range(N_GROUPS):
                cols = slice(grp * gw, (grp + 1) * gw)
                mixed = lax.dot_general(ws_ref[grp], vn[rows, cols], NN, preferred_element_type=F32) + bs_ref[grp]
                y_ref[rows, cols] = (_gelu(u_ref[rows, cols].astype(F32)) * mixed
                                     * _silu(z_ref[rows, cols].astype(F32))).astype(BF16)

    full3 = lambda shape: pl.BlockSpec(shape, lambda i: (0, 0, 0))
    return pl.pallas_call(
        body, name="sgu_fwd", grid=(s // tr,),
        in_specs=[pl.BlockSpec((tr, cw), lambda i: (i, 0)), pl.BlockSpec((tr, cw), lambda i: (i, 1)),
                  pl.BlockSpec((tr, cw), lambda i: (i, 2)), _vec_spec(cw), _vec_spec(cw),
                  full3(w_s.shape), full3(b_s.shape)],
        out_specs=pl.BlockSpec((tr, cw), lambda i: (i, 0)),
        out_shape=jax.ShapeDtypeStruct((s, cw), BF16),
        compiler_params=_params(1, parallel=1),
    )(uvz, uvz, uvz, ln_g, ln_b, w_s, b_s)


def _sgu_bwd(uvz, dy, ln_g, ln_b, w_s, b_s, cw, plan=None):
    s = uvz.shape[0]
    tr = 2 * CHUNK if s % (2 * CHUNK) == 0 else CHUNK
    gw = cw // N_GROUPS

    def body(u_ref, v_ref, z_ref, dy_ref, g_ref, b_ref, ws_ref, bs_ref,
             duvz_ref, dws_ref, dbs_ref, dg_ref, db_ref, dvn_ref):
        vv = v_ref[...].astype(F32)
        gvec = g_ref[...]
        vhat, rstd, vn = _sgu_norm(vv, gvec, b_ref[...])
        vn = vn.astype(BF16)
        first = pl.program_id(0) == 0

        @pl.when(first)
        def _():
            dws_ref[...] = jnp.zeros_like(dws_ref)
            dbs_ref[...] = jnp.zeros_like(dbs_ref)

        for ch in range(tr // CHUNK):
            rows = slice(ch * CHUNK, (ch + 1) * CHUNK)
            for grp in range(N_GROUPS):
                cols = slice(grp * gw, (grp + 1) * gw)
                vn_g = vn[rows, cols]
                mixed = lax.dot_general(ws_ref[grp], vn_g, NN, preferred_element_type=F32) + bs_ref[grp]
                uu = u_ref[rows, cols].astype(F32)
                zz = z_ref[rows, cols].astype(F32)
                dyv = dy_ref[rows, cols].astype(F32)
                gu, sz = _gelu(uu), _silu(zz)
                duvz_ref[rows, grp * gw:(grp + 1) * gw] = (dyv * mixed * sz * _dgelu(uu)).astype(BF16)
                duvz_ref[rows, 2 * cw + grp * gw:2 * cw + (grp + 1) * gw] = (
                    dyv * gu * mixed * _dsilu(zz)).astype(BF16)
                dmixed = dyv * gu * sz
                dm16 = dmixed.astype(BF16)
                dws_ref[grp] += lax.dot_general(dm16, vn_g, NT, preferred_element_type=F32)
                dbs_ref[grp] += jnp.broadcast_to(jnp.sum(dmixed, axis=1, keepdims=True), (CHUNK, LANES))
                dvn_ref[rows, cols] = lax.dot_general(ws_ref[grp], dm16, TN, preferred_element_type=F32)

        dvn = dvn_ref[...]
        _accumulate(dg_ref, jnp.sum(dvn * vhat, axis=0, keepdims=True))
        _accumulate(db_ref, jnp.sum(dvn, axis=0, keepdims=True))
        dvhat = dvn * gvec
        dgv = rstd * (dvhat - jnp.mean(dvhat, axis=-1, keepdims=True)
                      - vhat * jnp.mean(dvhat * vhat, axis=-1, keepdims=True))
        duvz_ref[:, cw:2 * cw] = (dgv * _dgelu(vv)).astype(BF16)

    full3 = lambda shape: pl.BlockSpec(shape, lambda i: (0, 0, 0))
    acc3 = jax.ShapeDtypeStruct((N_GROUPS, CHUNK, LANES), F32)
    vec = jax.ShapeDtypeStruct((1, cw), F32)
    row = pl.BlockSpec((tr, cw), lambda i: (i, 0))
    return _call(
        body, name="sgu_bwd", grid=(s // tr,),
        in_specs=[row, pl.BlockSpec((tr, cw), lambda i: (i, 1)), pl.BlockSpec((tr, cw), lambda i: (i, 2)),
                  row, _vec_spec(cw), _vec_spec(cw), full3(w_s.shape), full3(b_s.shape)],
        out_specs=[pl.BlockSpec((tr, 3 * cw), lambda i: (i, 0)), full3((N_GROUPS, CHUNK, LANES)),
                   full3((N_GROUPS, CHUNK, LANES)), _vec_spec(cw), _vec_spec(cw)],
        out_shape=[jax.ShapeDtypeStruct((s, 3 * cw), BF16), acc3, acc3, vec, vec],
        scratch_shapes=[pltpu.VMEM((tr, cw), F32)],
        operands=(uvz, uvz, uvz, dy, ln_g, ln_b, w_s, b_s), plan=plan)


def _flat_rows(a):
    return a.reshape(-1, a.shape[-1])


def _add_sibling(grad, recv, core_idx):
    nchip, k, n = grad.shape
    tr = _tile(k // 2, (256, 128))
    nb = (k // 2) // tr

    def body(c_ref, g_ref, r_ref, o_ref):
        o_ref[...] = (g_ref[...].astype(F32) + r_ref[...].astype(F32)).astype(BF16)

    return pl.pallas_call(
        body, name="add_sibling",
        grid_spec=pltpu.PrefetchScalarGridSpec(
            num_scalar_prefetch=1, grid=(nchip, nb),
            in_specs=[pl.BlockSpec((None, tr, n), lambda q, i, c: (q, c[0] * nb + i, 0)),
                      pl.BlockSpec((None, tr, n), lambda q, i, c: (q, i, 0))],
            out_specs=pl.BlockSpec((None, tr, n), lambda q, i, c: (q, i, 0))),
        out_shape=jax.ShapeDtypeStruct((nchip, k // 2, n), BF16),
        compiler_params=_params(2, parallel=2),
    )(core_idx, grad, recv)


def _sum_chips(own, others, reduced, layer, place_idx):
    _, kh, n = own.shape
    tr = _tile(kh, (256, 128))
    nb = kh // tr

    def body(place_ref, own_ref, oth_ref, red_ref, o_ref):
        acc = own_ref[...].astype(F32)
        for q in range(3):
            acc = acc + oth_ref[q].astype(F32)
        o_ref[...] = acc

    return pl.pallas_call(
        body, name="sum_chips",
        grid_spec=pltpu.PrefetchScalarGridSpec(
            num_scalar_prefetch=1, grid=(nb,),
            in_specs=[pl.BlockSpec((None, tr, n), lambda i, p: (p[0], i, 0)),
                      pl.BlockSpec((3, tr, n), lambda i, p: (0, i, 0)),
                      HBM_SPEC],
            out_specs=pl.BlockSpec((None, tr, n), lambda i, p: (layer, p[1] * nb + i, 0))),
        out_shape=jax.ShapeDtypeStruct(reduced.shape, reduced.dtype),
        input_output_aliases={3: 0},
        compiler_params=_params(1, parallel=1),
    )(place_idx, own, others, reduced)


def _sum_devices(parts, plan=None):
    nd, r, _ = parts.shape
    tr = _tile(r, (512, 256, 128, 64, 32, 16, 8))

    def body(p_ref, o_ref):
        acc = p_ref[0]
        for q in range(1, nd):
            acc = acc + p_ref[q]
        o_ref[...] = acc

    return _call(
        body, name="sum_devices", grid=(r // tr,),
        in_specs=[pl.BlockSpec((nd, tr, LANES), lambda i: (0, i, 0))],
        out_specs=pl.BlockSpec((tr, LANES), lambda i: (i, 0)),
        out_shape=jax.ShapeDtypeStruct((r, LANES), F32),
        operands=(parts,), parallel=1, plan=plan)


def _adamw(w, g, m, v, plan=None):
    r, n = w.shape
    tr = _tile(r, [p for p in (1024, 512, 256, 128, 64, 32, 16, 8) if p * n <= ELEMENTWISE_BLOCK])

    def body(w_ref, g_ref, m_ref, v_ref, d_ref, nm_ref, nv_ref):
        gv = g_ref[...]
        nm = ADAM_B1 * m_ref[...] + (1.0 - ADAM_B1) * gv
        nv = ADAM_B2 * v_ref[...] + (1.0 - ADAM_B2) * (gv * gv)
        m_hat = nm / (1.0 - ADAM_B1 ** ADAM_STEP)
        v_hat = nv / (1.0 - ADAM_B2 ** ADAM_STEP)
        d_ref[...] = -ADAM_LR * (m_hat / (jnp.sqrt(v_hat) + ADAM_EPS) + ADAM_WD * w_ref[...])
        nm_ref[...] = nm
        nv_ref[...] = nv

    spec = pl.BlockSpec((tr, n), lambda i: (i, 0))
    shp = jax.ShapeDtypeStruct((r, n), F32)
    return _call(
        body, name="adamw", grid=(r // tr,),
        in_specs=[spec] * 4, out_specs=[spec] * 3, out_shape=[shp] * 3,
        operands=(w, g, m, v), parallel=1, plan=plan)


def _pack(arrays, row_multiple=8):
    flat = [a.reshape(-1) for a in arrays]
    sizes = [f.shape[0] for f in flat]
    total = sum(sizes)
    unit = LANES * row_multiple
    padded = -(-total // unit) * unit
    if padded > total:
        flat.append(jnp.zeros((padded - total,), F32))
    offsets = [sum(sizes[:i]) for i in range(len(sizes))]
    return jnp.concatenate(flat).reshape(-1, LANES), offsets


def _unpack(packed, offsets, shapes):
    flat = packed.reshape(-1)
    return [flat[o:o + math.prod(s)].reshape(s) for o, s in zip(offsets, shapes)]


def kernel(x, c, ab_norm_g, ab_w_mod, ab_b_mod, ab_w_in, ab_conv_w, ab_w_out, sg_norm_g, sg_w_mod, sg_b_mod, sg_w_in, sg_ln_g, sg_ln_b, sg_w_s, sg_b_s, sg_w_out, final_norm_g, loss_target, m_ab_norm_g, m_ab_w_mod, m_ab_b_mod, m_ab_w_in, m_ab_conv_w, m_ab_w_out, m_sg_norm_g, m_sg_w_mod, m_sg_b_mod, m_sg_w_in, m_sg_ln_g, m_sg_ln_b, m_sg_w_s, m_sg_b_s, m_sg_w_out, m_final_norm_g, v_ab_norm_g, v_ab_w_mod, v_ab_b_mod, v_ab_w_in, v_ab_conv_w, v_ab_w_out, v_sg_norm_g, v_sg_w_mod, v_sg_b_mod, v_sg_w_in, v_sg_ln_g, v_sg_ln_b, v_sg_w_s, v_sg_b_s, v_sg_w_out, v_final_norm_g):
    s, d = x.shape[1], x.shape[2]
    aw = d // 2
    cw = d
    mod_l = ab_w_mod.shape[-1]
    x0 = x[0]
    target = loss_target[0]
    mx, my, mc = lax.axis_index("x"), lax.axis_index("y"), lax.axis_index("c")
    chip = 2 * mx + my
    chip_idx = jnp.reshape(chip, (1,)).astype(jnp.int32)
    core_idx = jnp.reshape(mc, (1,)).astype(jnp.int32)
    place_idx = jnp.stack([chip, mc]).astype(jnp.int32)

    win = [_place_own_shard(ab_w_in if L % 2 == 0 else sg_w_in, L // 2, chip_idx) for L in range(4)]
    wout = [_place_own_shard(ab_w_out if L % 2 == 0 else sg_w_out, L // 2, chip_idx) for L in range(4)]
    k_in, k_out = d, wout[0].shape[1]

    def gather_plan(ici=(), pass_on=()):
        arrays, copies = [], []
        for stage, make in ((ici, _gather_ici), (pass_on, _gather_pass_on)):
            for kind, L in stage:
                arr = win[L] if kind == "in" else wout[L]
                arrays.append(arr)
                copies += make(len(arrays) - 1, k_in if kind == "in" else k_out)
        return _Plan(tuple(arrays), tuple(copies)), [(kind, L) for kind, L in tuple(ici) + tuple(pass_on)]

    def absorb(plan_and_names, updated):
        _, names = plan_and_names
        for pos, (kind, L) in enumerate(names):
            if kind == "in":
                win[L] = updated[pos]
            else:
                wout[L] = updated[pos]

    win[0] = _comm_stages("gather_first_w_in", [win[0]], [_gather_ici(0, k_in), _gather_pass_on(0, k_in)])[0]

    small_local = [c[0], ab_conv_w, sg_norm_g, sg_ln_g, sg_ln_b]
    small_shapes = [a.shape for a in small_local]
    payload, small_off = _pack(small_local)
    gathered = _all_to_all(jnp.broadcast_to(payload[None], (N_DEV,) + payload.shape), "gather_small")
    per_dev = [_unpack(gathered[b], small_off, small_shapes) for b in range(N_DEV)]
    c_all = jnp.stack([per_dev[b][0] for b in range(N_DEV)])

    def from_chips(idx, axis):
        return jnp.concatenate([per_dev[2 * q][idx] for q in range(N_CHIPS)], axis=axis)

    conv_w_full = from_chips(1, 2)
    sg_norm_g_full = from_chips(2, 1)
    sg_ln_g_full = from_chips(3, 1)
    sg_ln_b_full = from_chips(4, 1)

    ab_b_local = lax.dynamic_slice_in_dim(ab_b_mod, chip * mod_l, mod_l, axis=1)
    mod_rows = []
    for layer in range(4):
        i = layer // 2
        w_mod, bias = (ab_w_mod, ab_b_local) if layer % 2 == 0 else (sg_w_mod, sg_b_mod)
        mod_rows.append(_mod_fwd(c_all, w_mod, bias[i:i + 1], i))
    mod_local = jnp.stack(mod_rows, axis=1)
    mod_recv = _all_to_all(mod_local.reshape(N_DEV, -1, LANES), "exchange_mod")
    mod_recv = mod_recv.reshape(N_DEV, 4, mod_l)
    mod_full = jnp.concatenate([mod_recv[2 * q] for q in range(N_CHIPS)], axis=-1)
    shifts = [mod_full[l:l + 1, :d] for l in range(4)]
    scales = [mod_full[l:l + 1, d:2 * d] for l in range(4)]
    gates = [mod_full[l:l + 1, 2 * d:] for l in range(4)]

    cos, sin = _rope_tables(s)
    w_s16 = sg_w_s.astype(BF16)
    b_s3 = sg_b_s[..., None]

    fwd_comm = {
        ("in_proj", 0): ([("in", 1), ("out", 0)], []),
        ("attn", 0): ([("out", 1), ("in", 2)], [("out", 0)]),
        ("out_proj", 0): ([], [("in", 1), ("out", 1)]),
        ("in_proj", 1): ([("out", 2)], [("in", 2)]),
        ("out_proj", 1): ([], [("out", 2)]),
        ("in_proj", 2): ([("in", 3)], []),
        ("attn", 2): ([("out", 3)], []),
        ("out_proj", 2): ([], [("in", 3), ("out", 3)]),
    }

    def carried(key, fn, *args):
        if key not in fwd_comm:
            return fn(*args)
        pn = gather_plan(*fwd_comm[key])
        res, updated = fn(*args, plan=pn[0])
        absorb(pn, updated)
        return res

    saved = []
    xs = x0
    for layer in range(4):
        i = layer // 2
        if layer % 2 == 0:
            h = _prenorm(xs, ab_norm_g[i:i + 1], scales[layer], shifts[layer])
            proj = carried(("in_proj", layer), _in_proj, h, win[layer])
            attn, lse = carried(("attn", layer), _attn_fwd, proj, cos, sin, aw)
            y = _ab_mix(attn, proj, conv_w_full[i], aw)
            x_next, out = carried(("out_proj", layer), _out_proj_residual, y, wout[layer].reshape(-1, d), xs,
                                  gates[layer])
            saved.append((xs, h, proj, y, out, attn, lse))
        else:
            h = _prenorm(xs, sg_norm_g_full[i:i + 1], scales[layer], shifts[layer])
            uvz = carried(("in_proj", layer), _in_proj, h, win[layer])
            y = _sgu_fwd(uvz, sg_ln_g_full[i:i + 1], sg_ln_b_full[i:i + 1], w_s16[i], b_s3[i], cw)
            x_next, out = carried(("out_proj", layer), _out_proj_residual, y, wout[layer].reshape(-1, d), xs,
                                  gates[layer])
            saved.append((xs, h, uvz, y, out))
        xs = x_next

    loss11, dx, d_final_g, dout, dgate = _final_loss(xs, target, final_norm_g[None], saved[3][4], gates[3])
    loss = lax.psum(loss11[0, 0], ("x", "y", "c"))

    reduced = {"in": [lax.empty((2,) + w.shape[1:], F32) for w in (ab_w_in, sg_w_in)],
               "out": [lax.empty((2,) + w.shape[1:], F32) for w in (ab_w_out, sg_w_out)]}
    grads = {}
    stage = {}
    k_of = {"in": k_in, "out": k_out}

    def swap_plan(which):
        arrays, copies = [], []
        for kind, L in which:
            g = grads[kind, L]
            arrays += [g, jax.ShapeDtypeStruct((N_CHIPS, g.shape[1] // 2, g.shape[2]), BF16)]
            copies += _reduce_swap(len(arrays) - 2, len(arrays) - 1, k_of[kind])
        return _Plan(tuple(arrays), tuple(copies))

    def after_swap(which, updated):
        for pos, (kind, L) in enumerate(which):
            stage[kind, L] = _add_sibling(grads[kind, L], updated[2 * pos + 1], core_idx)

    def ici_plan(pieces):
        arrays, copies = [], []
        for kind, L, only in pieces:
            cs = stage[kind, L]
            arrays += [cs, stage.get((kind, L, "recv"), jax.ShapeDtypeStruct((3,) + cs.shape[1:], BF16))]
            copies += _reduce_ici(len(arrays) - 2, len(arrays) - 1, only)
        return _Plan(tuple(arrays), tuple(copies))

    def after_ici(pieces, updated):
        for pos, (kind, L, _) in enumerate(pieces):
            stage[kind, L, "recv"] = updated[2 * pos + 1]

    def sum_layer(L):
        for kind in ("in", "out"):
            reduced[kind][L % 2] = _sum_chips(stage[kind, L], stage[kind, L, "recv"], reduced[kind][L % 2],
                                              L // 2, place_idx)

    def share_plan(L):
        arrays = (reduced["in"][L % 2], reduced["out"][L % 2])
        copies = _reduce_share(0, L // 2, k_in) + _reduce_share(1, L // 2, k_out)
        return _Plan(arrays, tuple(copies))

    def after_share(L, updated):
        reduced["in"][L % 2], reduced["out"][L % 2] = updated[0], updated[1]

    all_chips = (0, 1, 2)
    dmods = [None] * 4
    d_ab_norm_g, d_sg_norm_g = [None, None], [None, None]
    d_conv_w, d_ln_g, d_ln_b, d_w_s, d_b_s = ([None, None] for _ in range(5))
    for layer in reversed(range(4)):
        i = layer // 2
        prev = layer + 1
        busy = prev < 4
        if layer % 2 == 0:
            xs, h, proj, y, out, attn, lse = saved[layer]
        else:
            xs, h, uvz, y, out = saved[layer]
        below = (saved[layer - 1][4], gates[layer - 1]) if layer > 0 else None
        w2 = wout[layer].reshape(-1, d)
        grads["out", layer] = _out_proj_bwd_w(y, dout).reshape(N_CHIPS, -1, d)
        if busy:
            swapped = [("in", prev), ("out", prev)] + ([("out", 0)] if layer == 0 else [])
            dy, updated = _out_proj_bwd_act(dout, w2, plan=swap_plan(swapped))
            after_swap(swapped, updated)
        else:
            dy = _out_proj_bwd_act(dout, w2)
        if layer % 2 == 0:
            if busy:
                pieces = [("in", prev, all_chips), ("out", prev, all_chips)]
                pieces += [("out", 0, all_chips)] if layer == 0 else []
                dact, updated = _attn_bwd(proj, cos, sin, dy, attn, lse, aw, plan=ici_plan(pieces))
                after_ici(pieces, updated)
                sum_layer(prev)
            else:
                dact = _attn_bwd(proj, cos, sin, dy, attn, lse, aw)
            dact, d_conv_w[i] = _conv_bwd(dact, dy, proj, conv_w_full[i], aw)
            if layer == 0:
                grads["in", 0], updated = _in_proj_bwd_w(h, dact, win[0].shape[-1], plan=share_plan(prev))
                after_share(prev, updated)
                plan = swap_plan([("in", 0)])
                after_swap([("in", 0)], _comm_stages("grads_to_sibling", plan.arrays, [plan.copies]))
                pieces = [("in", 0, all_chips)]
                dh, updated = _in_proj_bwd_act(dact, win[0], plan=ici_plan(pieces))
                after_ici(pieces, updated)
            elif busy:
                dh, updated = _in_proj_bwd_act(dact, win[layer], plan=share_plan(prev))
                after_share(prev, updated)
                grads["in", layer] = _in_proj_bwd_w(h, dact, win[layer].shape[-1])
            else:
                dh = _in_proj_bwd_act(dact, win[layer])
                grads["in", layer] = _in_proj_bwd_w(h, dact, win[layer].shape[-1])
            norm_g = ab_norm_g[i:i + 1]
        else:
            sgu_args = (uvz, dy, sg_ln_g_full[i:i + 1], sg_ln_b_full[i:i + 1], w_s16[i], b_s3[i], cw)
            if busy:
                pieces = [("in", prev, (0, 1)), ("out", prev, all_chips)]
                res, updated = _sgu_bwd(*sgu_args, plan=ici_plan(pieces))
                after_ici(pieces, updated)
            else:
                res = _sgu_bwd(*sgu_args)
            dact, d_w_s[i], db_wide, d_ln_g[i], d_ln_b[i] = res
            d_b_s[i] = db_wide[:, :, 0]
            if busy:
                pieces = [("in", prev, (2,))]
                dh, updated = _in_proj_bwd_act(dact, win[layer], plan=ici_plan(pieces))
                after_ici(pieces, updated)
                sum_layer(prev)
                grads["in", layer], updated = _in_proj_bwd_w(h, dact, win[layer].shape[-1], plan=share_plan(prev))
                after_share(prev, updated)
            else:
                dh = _in_proj_bwd_act(dact, win[layer])
                grads["in", layer] = _in_proj_bwd_w(h, dact, win[layer].shape[-1])
            norm_g = sg_norm_g_full[i:i + 1]
        res = _prenorm_bwd(xs, dh, dx, norm_g, scales[layer], below)
        dx, dshift, dscale, d_norm_g = res[:4]
        (d_ab_norm_g if layer % 2 == 0 else d_sg_norm_g)[i] = d_norm_g
        dmods[layer] = jnp.concatenate([dshift, dscale, dgate], axis=1)
        if below:
            dout, dgate = res[4:]
    grad_x = dx[None]

    partial_list = [jnp.concatenate(dmods, axis=0),
                    jnp.concatenate(d_ab_norm_g, axis=0), jnp.concatenate(d_sg_norm_g, axis=0), d_final_g[0],
                    jnp.stack(d_conv_w), jnp.concatenate(d_ln_g, axis=0), jnp.concatenate(d_ln_b, axis=0),
                    jnp.stack(d_w_s), jnp.stack(d_b_s)]
    partial_shapes = [a.shape for a in partial_list]
    partials, part_off = _pack(partial_list, PACKED_ROW_BLOCK)
    all_partials = _all_to_all(jnp.broadcast_to(partials[None], (N_DEV,) + partials.shape), "gather_partials")
    sum_layer(0)
    summed_packed, updated = _sum_devices(all_partials, plan=share_plan(0))
    after_share(0, updated)
    summed = _unpack(summed_packed, part_off, partial_shapes)
    (g_mod_bias, g_ab_norm_g, g_sg_norm_g_full, g_final_g, g_conv_full, g_ln_g_full, g_ln_b_full,
     g_w_s, g_b_s) = summed
    dm_all = jnp.stack([_unpack(all_partials[b], part_off[:1], partial_shapes[:1])[0] for b in range(N_DEV)])
    dm_local = lax.dynamic_slice_in_dim(dm_all, chip * mod_l, mod_l, axis=2)

    def chip_cols(a, axis):
        width = a.shape[axis] // N_CHIPS
        return lax.dynamic_slice_in_dim(a, chip * width, width, axis=axis)

    g_ab_b_mod = jnp.stack([g_mod_bias[0], g_mod_bias[2]])
    g_sg_b_mod = chip_cols(jnp.stack([g_mod_bias[1], g_mod_bias[3]]), 1)
    g_ab_w_mod = jnp.stack([_mod_bwd_w(c_all, dm_local[:, 0]), _mod_bwd_w(c_all, dm_local[:, 2])])
    g_sg_w_mod = jnp.stack([_mod_bwd_w(c_all, dm_local[:, 1]), _mod_bwd_w(c_all, dm_local[:, 3])])
    g_conv = chip_cols(g_conv_full, 2)
    g_sg_norm_g = chip_cols(g_sg_norm_g_full, 1)
    g_ln_g = chip_cols(g_ln_g_full, 1)
    g_ln_b = chip_cols(g_ln_b_full, 1)

    def step_big(w, g, m, v):
        dl, nm, nv = _adamw(_flat_rows(w), _flat_rows(g), _flat_rows(m), _flat_rows(v))
        return dl.reshape(w.shape), nm.reshape(w.shape), nv.reshape(w.shape)

    g_ab_w_in, g_sg_w_in = reduced["in"]
    g_ab_w_out, g_sg_w_out = reduced["out"]
    big_out = {
        "ab_w_mod": step_big(ab_w_mod, g_ab_w_mod, m_ab_w_mod, v_ab_w_mod),
        "ab_w_in": step_big(ab_w_in, g_ab_w_in, m_ab_w_in, v_ab_w_in),
        "ab_w_out": step_big(ab_w_out, g_ab_w_out, m_ab_w_out, v_ab_w_out),
        "sg_w_mod": step_big(sg_w_mod, g_sg_w_mod, m_sg_w_mod, v_sg_w_mod),
        "sg_w_in": step_big(sg_w_in, g_sg_w_in, m_sg_w_in, v_sg_w_in),
        "sg_w_out": step_big(sg_w_out, g_sg_w_out, m_sg_w_out, v_sg_w_out),
    }
    small_names = ["ab_norm_g", "ab_b_mod", "ab_conv_w", "sg_norm_g", "sg_b_mod", "sg_ln_g", "sg_ln_b",
                   "sg_w_s", "sg_b_s", "final_norm_g"]
    small_w = [ab_norm_g, ab_b_mod, ab_conv_w, sg_norm_g, sg_b_mod, sg_ln_g, sg_ln_b, sg_w_s, sg_b_s, final_norm_g]
    small_g = [g_ab_norm_g, g_ab_b_mod, g_conv, g_sg_norm_g, g_sg_b_mod, g_ln_g, g_ln_b, g_w_s, g_b_s, g_final_g]
    small_m = [m_ab_norm_g, m_ab_b_mod, m_ab_conv_w, m_sg_norm_g, m_sg_b_mod, m_sg_ln_g, m_sg_ln_b, m_sg_w_s,
               m_sg_b_s, m_final_norm_g]
    small_v = [v_ab_norm_g, v_ab_b_mod, v_ab_conv_w, v_sg_norm_g, v_sg_b_mod, v_sg_ln_g, v_sg_ln_b, v_sg_w_s,
               v_sg_b_s, v_final_norm_g]
    shapes = [a.shape for a in small_w]
    pw, off = _pack(small_w, PACKED_ROW_BLOCK)
    pg, _ = _pack(small_g, PACKED_ROW_BLOCK)
    pm, _ = _pack(small_m, PACKED_ROW_BLOCK)
    pv, _ = _pack(small_v, PACKED_ROW_BLOCK)
    pd, pnm, pnv = _adamw(pw, pg, pm, pv)
    small_out = {}
    for name, dl, nm, nv in zip(small_names, _unpack(pd, off, shapes), _unpack(pnm, off, shapes),
                                _unpack(pnv, off, shapes)):
        small_out[name] = (dl, nm, nv)

    grad_of = {
        "ab_norm_g": g_ab_norm_g, "ab_w_mod": g_ab_w_mod, "ab_b_mod": g_ab_b_mod, "ab_w_in": g_ab_w_in,
        "ab_conv_w": g_conv, "ab_w_out": g_ab_w_out, "sg_norm_g": g_sg_norm_g, "sg_w_mod": g_sg_w_mod,
        "sg_b_mod": g_sg_b_mod, "sg_w_in": g_sg_w_in, "sg_ln_g": g_ln_g, "sg_ln_b": g_ln_b, "sg_w_s": g_w_s,
        "sg_b_s": g_b_s, "sg_w_out": g_sg_w_out, "final_norm_g": g_final_g,
    }
    order = ["ab_norm_g", "ab_w_mod", "ab_b_mod", "ab_w_in", "ab_conv_w", "ab_w_out", "sg_norm_g", "sg_w_mod",
             "sg_b_mod", "sg_w_in", "sg_ln_g", "sg_ln_b", "sg_w_s", "sg_b_s", "sg_w_out", "final_norm_g"]
    steps = {**big_out, **small_out}
    return (loss, grad_x, *[grad_of[n] for n in order], *[steps[n][0] for n in order],
            *[steps[n][1] for n in order], *[steps[n][2] for n in order])
```

```python
import math
from typing import Any, Callable, NamedTuple

import jax
import jax.numpy as jnp
import numpy as np
from jax import lax
from jax.experimental import pallas as pl
from jax.experimental.pallas import tpu as pltpu

F32 = jnp.float32
BF16 = jnp.bfloat16

HEAD_DIM = 128
RADIUS = 64
DILATIONS = (1, 4, 16)
Q_BLOCK = 256
K_WINDOW = Q_BLOCK + 2 * RADIUS
ATTN_UNROLL = 4
ROPE_THETA = 10000.0
NEG_INF = -1e30
N_GROUPS = 8
CHUNK = 128
EPS = 1e-6
CONV_ROWS = 512
CONV_HALO = 16
LANES = 128
ELEMENTWISE_BLOCK = 512 * 1024
PACKED_ROW_BLOCK = 512
N_DEV = 8
N_CHIPS = 4

ADAM_LR = 0.001
ADAM_B1 = 0.9
ADAM_B2 = 0.999
ADAM_EPS = 1e-08
ADAM_WD = 0.01
ADAM_STEP = 10

VMEM_LIMIT_V7X = 56 * 1024 * 1024

MESH_ID = pl.DeviceIdType.MESH
HBM_SPEC = pl.BlockSpec(memory_space=pltpu.HBM)

NN = (((1,), (0,)), ((), ()))
NT = (((1,), (1,)), ((), ()))
TN = (((0,), (0,)), ((), ()))


def _params(n_grid, parallel=0):
    sem = tuple(["parallel"] * parallel + ["arbitrary"] * (n_grid - parallel))
    return pltpu.CompilerParams(dimension_semantics=sem, vmem_limit_bytes=VMEM_LIMIT_V7X)


def _tile(n, prefs):
    for p in prefs:
        if n % p == 0:
            return p
    return n


def _sigmoid(z):
    return 1.0 / (1.0 + jnp.exp(-z))


def _silu(z):
    return z * _sigmoid(z)


def _dsilu(z):
    s = _sigmoid(z)
    return s * (1.0 + z * (1.0 - s))


_GELU_K = math.sqrt(2.0 / math.pi)
_GELU_C = 0.044715


def _gelu(u):
    return 0.5 * u * (1.0 + jnp.tanh(_GELU_K * (u + _GELU_C * u * u * u)))


def _dgelu(u):
    t = jnp.tanh(_GELU_K * (u + _GELU_C * u * u * u))
    return 0.5 * (1.0 + t) + 0.5 * u * (1.0 - t * t) * _GELU_K * (1.0 + 3.0 * _GELU_C * u * u)


class _Place(NamedTuple):
    x: Any
    y: Any
    c: Any
    chip: Any


def _my_place():
    mx, my, mc = lax.axis_index("x"), lax.axis_index("y"), lax.axis_index("c")
    return _Place(mx, my, mc, 2 * mx + my)


def _other_chips(p):
    return [(1 - p.x, p.y), (p.x, 1 - p.y), (1 - p.x, 1 - p.y)]


class _Copy(NamedTuple):
    src: int
    src_at: Callable
    dst: int
    dst_at: Callable
    peer: Callable


class _Plan(NamedTuple):
    arrays: tuple
    copies: tuple


def _view(ref, index):
    return ref if index is None else ref.at[index]


def _plan_io(plan):
    ins = [k for k, a in enumerate(plan.arrays) if not isinstance(a, jax.ShapeDtypeStruct)]
    written = sorted({cp.dst for cp in plan.copies})
    return ins, written


def _descriptors(plan, in_refs, out_refs, send_sems, recv_sems):
    ins, written = _plan_io(plan)
    place = _my_place()
    return [
        pltpu.make_async_remote_copy(
            src_ref=_view(in_refs[ins.index(cp.src)], cp.src_at(place)),
            dst_ref=_view(out_refs[written.index(cp.dst)], cp.dst_at(place)),
            send_sem=send_sems.at[k], recv_sem=recv_sems.at[k],
            device_id=cp.peer(place), device_id_type=MESH_ID)
        for k, cp in enumerate(plan.copies)]


def _plan_operands(plan, n_in, n_out):
    ins, written = _plan_io(plan)
    operands = [plan.arrays[k] for k in ins]
    out_shape = [jax.ShapeDtypeStruct(plan.arrays[k].shape, plan.arrays[k].dtype) for k in written]
    aliases = {n_in + ins.index(k): n_out + pos for pos, k in enumerate(written) if k in ins}
    n = len(plan.copies)
    sems = [pltpu.SemaphoreType.DMA((n,)), pltpu.SemaphoreType.DMA((n,))]
    return operands, out_shape, aliases, sems, written


def _call(body, *, name, grid, in_specs, out_specs, out_shape, operands, scratch_shapes=(), aliases=None,
          parallel=0, plan=None):
    single = not isinstance(out_shape, (list, tuple))
    out_shape = [out_shape] if single else list(out_shape)
    out_specs = [out_specs] if single else list(out_specs)
    if plan is None:
        res = pl.pallas_call(
            body, name=name, grid=grid, in_specs=list(in_specs), out_specs=out_specs, out_shape=out_shape,
            scratch_shapes=list(scratch_shapes), input_output_aliases=aliases or {},
            compiler_params=_params(len(grid), parallel=parallel),
        )(*operands)
        return res[0] if single else res

    n_in, n_out, n_scr = len(operands), len(out_shape), len(scratch_shapes)
    p_operands, p_out_shape, p_aliases, sems, written = _plan_operands(plan, n_in, n_out)
    n_pin, n_pout = len(p_operands), len(p_out_shape)

    def wrapped(*refs):
        ins = refs[:n_in]
        p_in = refs[n_in:n_in + n_pin]
        outs = refs[n_in + n_pin:n_in + n_pin + n_out]
        p_out = refs[n_in + n_pin + n_out:n_in + n_pin + n_out + n_pout]
        scratch = refs[n_in + n_pin + n_out + n_pout:n_in + n_pin + n_out + n_pout + n_scr]
        send_sems, recv_sems = refs[-2:]
        ids = [pl.program_id(a) for a in range(len(grid))]
        first = ids[0] == 0
        last = ids[0] == grid[0] - 1
        for a in range(1, len(grid)):
            first = jnp.logical_and(first, ids[a] == 0)
            last = jnp.logical_and(last, ids[a] == grid[a] - 1)

        @pl.when(first)
        def _():
            for cp in _descriptors(plan, p_in, p_out, send_sems, recv_sems):
                cp.start()

        body(*ins, *outs, *scratch)

        @pl.when(last)
        def _():
            for cp in _descriptors(plan, p_in, p_out, send_sems, recv_sems):
                cp.wait()

    res = pl.pallas_call(
        wrapped, name=name, grid=grid,
        in_specs=list(in_specs) + [HBM_SPEC] * n_pin,
        out_specs=out_specs + [HBM_SPEC] * n_pout,
        out_shape=out_shape + p_out_shape,
        scratch_shapes=list(scratch_shapes) + sems,
        input_output_aliases={**(aliases or {}), **p_aliases},
        compiler_params=_params(len(grid)),
    )(*operands, *p_operands)
    outs = res[0] if single else res[:n_out]
    return outs, dict(zip(written, res[n_out:]))


def _comm_stages(name, arrays, stages):
    plan = _Plan(tuple(arrays), tuple(cp for st in stages for cp in st))
    p_operands, p_out_shape, p_aliases, sems, written = _plan_operands(plan, 0, 0)
    n_pin = len(p_operands)

    def body(*refs):
        p_in = refs[:n_pin]
        p_out = refs[n_pin:n_pin + len(written)]
        send_sems, recv_sems = refs[-2:]
        all_copies = _descriptors(plan, p_in, p_out, send_sems, recv_sems)
        base = 0
        for st in stages:
            for cp in all_copies[base:base + len(st)]:
                cp.start()
            for cp in all_copies[base:base + len(st)]:
                cp.wait()
            base += len(st)

    res = pl.pallas_call(
        body, name=name, in_specs=[HBM_SPEC] * n_pin, out_specs=[HBM_SPEC] * len(written),
        out_shape=p_out_shape, scratch_shapes=sems, input_output_aliases=p_aliases,
    )(*p_operands)
    return dict(zip(written, res))


def _half_rows(k, c):
    return pl.ds(c * (k // 2), k // 2)


def _gather_ici(a, k, only=(0, 1, 2)):
    own = lambda p: (p.chip, _half_rows(k, p.c))
    return [_Copy(a, own, a, own, lambda p, q=q: (*_other_chips(p)[q], p.c)) for q in only]


def _gather_pass_on(a, k):
    def at(q):
        def index(p):
            px, py = _other_chips(p)[q]
            return (2 * px + py, _half_rows(k, p.c))
        return index
    return [_Copy(a, at(q), a, at(q), lambda p: (p.x, p.y, 1 - p.c)) for q in range(3)]


def _reduce_swap(src, dst, k):
    return [_Copy(src, lambda p: (pl.ds(0, N_CHIPS), _half_rows(k, 1 - p.c)), dst, lambda p: None,
                  lambda p: (p.x, p.y, 1 - p.c))]


def _reduce_ici(src, dst, only=(0, 1, 2)):
    def slab(q):
        def index(p):
            px, py = _other_chips(p)[q]
            return 2 * px + py
        return index
    return [_Copy(src, slab(q), dst, lambda p, q=q: q, lambda p, q=q: (*_other_chips(p)[q], p.c)) for q in only]


def _reduce_share(a, layer, k):
    at = lambda p: (layer, _half_rows(k, p.c))
    return [_Copy(a, at, a, at, lambda p: (p.x, p.y, 1 - p.c))]


def _merge_plans(a, b):
    off = len(a.arrays)
    moved = tuple(cp._replace(src=cp.src + off, dst=cp.dst + off) for cp in b.copies)
    return _Plan(a.arrays + b.arrays, a.copies + moved)


def _broadcast_copies(src, dst):
    me = lambda p: 2 * p.chip + p.c

    def peer(k):
        return lambda p: (1 - p.x if (k >> 2) & 1 else p.x, 1 - p.y if (k >> 1) & 1 else p.y,
                          1 - p.c if k & 1 else p.c)

    return [_Copy(src, lambda p: None, dst, me, peer(k)) for k in range(1, N_DEV)]


def _place_own_slot(x, me_idx):
    r = x.shape[0]
    tr = _tile(r, (512, 256, 128, 64, 32, 16, 8))

    def body(me_ref, x_ref, o_ref):
        o_ref[...] = x_ref[...]

    return pl.pallas_call(
        body, name="place_own_slot",
        grid_spec=pltpu.PrefetchScalarGridSpec(
            num_scalar_prefetch=1, grid=(r // tr,),
            in_specs=[pl.BlockSpec((tr, LANES), lambda i, me: (i, 0))],
            out_specs=pl.BlockSpec((None, tr, LANES), lambda i, me: (me[0], i, 0))),
        out_shape=jax.ShapeDtypeStruct((N_DEV, r, LANES), F32),
        compiler_params=_params(1, parallel=1),
    )(me_idx, x)


def _all_to_all(x, name):
    def body(x_ref, y_ref, send_sems, recv_sems, own_sem):
        p = _my_place()
        me = 2 * p.chip + p.c
        own = pltpu.make_async_copy(x_ref.at[me], y_ref.at[me], own_sem)
        own.start()
        copies = []
        for k in range(1, N_DEV):
            px = 1 - p.x if (k >> 2) & 1 else p.x
            py = 1 - p.y if (k >> 1) & 1 else p.y
            pc = 1 - p.c if k & 1 else p.c
            peer = 4 * px + 2 * py + pc
            cp = pltpu.make_async_remote_copy(
                src_ref=x_ref.at[peer], dst_ref=y_ref.at[me],
                send_sem=send_sems.at[k - 1], recv_sem=recv_sems.at[k - 1],
                device_id=(px, py, pc), device_id_type=MESH_ID)
            cp.start()
            copies.append(cp)
        for cp in copies:
            cp.wait()
        own.wait()

    return pl.pallas_call(
        body, name=name,
        out_shape=jax.ShapeDtypeStruct(x.shape, x.dtype),
        in_specs=[HBM_SPEC], out_specs=HBM_SPEC,
        scratch_shapes=[pltpu.SemaphoreType.DMA((N_DEV - 1,)), pltpu.SemaphoreType.DMA((N_DEV - 1,)),
                        pltpu.SemaphoreType.DMA],
    )(x)


def _place_own_shard(w, layer, chip_idx):
    _, k, n = w.shape
    tr = _tile(k, (512, 256, 128))

    def body(c_ref, w_ref, g_ref):
        g_ref[...] = w_ref[...].astype(BF16)

    return pl.pallas_call(
        body, name="place_own_shard",
        grid_spec=pltpu.PrefetchScalarGridSpec(
            num_scalar_prefetch=1, grid=(k // tr,),
            in_specs=[pl.BlockSpec((None, tr, n), lambda r, c: (layer, r, 0))],
            out_specs=pl.BlockSpec((None, tr, n), lambda r, c: (c[0], r, 0))),
        out_shape=jax.ShapeDtypeStruct((N_CHIPS, k, n), BF16),
        compiler_params=_params(1, parallel=1),
    )(chip_idx, w)


def _matmul(name, operands, in_specs, grid, dims, out_shape, out_specs, epilogue, a_prologue=None,
            aliases=None, plan=None):
    n_in = len(operands)

    def body(*refs):
        a = refs[0][...]
        if a_prologue is not None:
            a = a_prologue(a)
        acc = lax.dot_general(a.astype(BF16), refs[1][...].astype(BF16), dims, preferred_element_type=F32)
        epilogue(acc, refs[2:n_in], refs[n_in:])

    return _call(body, name=name, grid=grid, in_specs=in_specs, out_specs=out_specs, out_shape=out_shape,
                 operands=operands, aliases=aliases, parallel=2, plan=plan)


def _store_cast(acc, extra, outs):
    outs[0][...] = acc.astype(outs[0].dtype)


def _in_proj(h, w, plan=None):
    s, d = h.shape
    nl = w.shape[-1]
    tm = _tile(s, (1024, 512, 256))
    tn = _tile(nl, (1024, 768, 512, 384, 256, 128))
    per = nl // tn
    return _matmul(
        "in_proj", (h, w),
        [pl.BlockSpec((tm, d), lambda i, j: (i, 0)),
         pl.BlockSpec((None, d, tn), lambda i, j: (j // per, 0, j % per))],
        (s // tm, N_CHIPS * per), NN,
        jax.ShapeDtypeStruct((s, N_CHIPS * nl), BF16),
        pl.BlockSpec((tm, tn), lambda i, j: (i, j)), _store_cast, plan=plan)


def _out_proj_residual(y, w2, x, gate, plan=None):
    s, wdt = y.shape
    d = w2.shape[-1]
    tm = _tile(s, (1024, 512, 256))
    tn = _tile(d, (1024, 512, 256, 128))

    def epilogue(acc, extra, outs):
        x_ref, gate_ref = extra
        outs[0][...] = x_ref[...] + gate_ref[...] * acc
        outs[1][...] = acc.astype(BF16)

    blk = pl.BlockSpec((tm, tn), lambda i, j: (i, j))
    return _matmul(
        "out_proj", (y, w2, x, gate),
        [pl.BlockSpec((tm, wdt), lambda i, j: (i, 0)),
         pl.BlockSpec((wdt, tn), lambda i, j: (0, j)),
         blk, pl.BlockSpec((1, tn), lambda i, j: (0, j))],
        (s // tm, d // tn), NN,
        [jax.ShapeDtypeStruct((s, d), F32), jax.ShapeDtypeStruct((s, d), BF16)],
        [blk, blk], epilogue, plan=plan)


def _out_proj_bwd_act(dout, w2, plan=None):
    s, d = dout.shape
    wdt = w2.shape[0]
    tm = _tile(s, (1024, 512, 256))
    tn = _tile(wdt, (1024, 512, 256, 128))
    return _matmul(
        "out_proj_dy", (dout, w2),
        [pl.BlockSpec((tm, d), lambda i, j: (i, 0)),
         pl.BlockSpec((tn, d), lambda i, j: (j, 0))],
        (s // tm, wdt // tn), NT,
        jax.ShapeDtypeStruct((s, wdt), BF16),
        pl.BlockSpec((tm, tn), lambda i, j: (i, j)), _store_cast, plan=plan)


def _out_proj_bwd_w(y, dout, plan=None):
    s, wdt = y.shape
    d = dout.shape[1]
    tm = _tile(wdt, (1024, 512, 256, 128))
    tn = _tile(d, (1024, 512, 256, 128))
    return _matmul(
        "out_proj_dw", (y, dout),
        [pl.BlockSpec((s, tm), lambda i, j: (0, i)),
         pl.BlockSpec((s, tn), lambda i, j: (0, j))],
        (wdt // tm, d // tn), TN,
        jax.ShapeDtypeStruct((wdt, d), BF16),
        pl.BlockSpec((tm, tn), lambda i, j: (i, j)), _store_cast, plan=plan)


def _in_proj_bwd_act(dproj, w, plan=None):
    s, n_all = dproj.shape
    d, nl = w.shape[1], w.shape[2]
    tm = _tile(s, (1024, 512, 256))
    tn = _tile(d, (512, 256, 128))

    def body(a_ref, w_ref, o_ref):
        acc = None
        for q in range(N_CHIPS):
            part = lax.dot_general(a_ref[:, q * nl:(q + 1) * nl], w_ref[q], NT, preferred_element_type=F32)
            acc = part if acc is None else acc + part
        o_ref[...] = acc.astype(BF16)

    return _call(
        body, name="in_proj_dh", grid=(s // tm, d // tn),
        in_specs=[pl.BlockSpec((tm, n_all), lambda i, j: (i, 0), pipeline_mode=pl.Buffered(1)),
                  pl.BlockSpec((N_CHIPS, tn, nl), lambda i, j: (0, j, 0))],
        out_specs=pl.BlockSpec((tm, tn), lambda i, j: (i, j)),
        out_shape=jax.ShapeDtypeStruct((s, d), BF16),
        operands=(dproj, w), parallel=2, plan=plan)


def _in_proj_bwd_w(h, dproj, nl, plan=None):
    s, d = h.shape
    tm = _tile(d, (1024, 512, 256, 128))
    tn = _tile(nl, (1024, 768, 512, 384, 256, 128))
    per = nl // tn
    return _matmul(
        "in_proj_dw", (h, dproj),
        [pl.BlockSpec((s, tm), lambda i, j: (0, i)),
         pl.BlockSpec((s, tn), lambda i, j: (0, j))],
        (d // tm, N_CHIPS * per), TN,
        jax.ShapeDtypeStruct((N_CHIPS, d, nl), BF16),
        pl.BlockSpec((None, tm, tn), lambda i, j: (j // per, i, j % per)), _store_cast, plan=plan)


def _mod_fwd(c_all, w_mod, bias, layer):
    nb, d = c_all.shape
    nl = w_mod.shape[-1]
    tn = _tile(nl, (768, 512, 384, 256, 128))

    def epilogue(acc, extra, outs):
        outs[0][...] = acc + extra[0][...]

    return _matmul(
        "mod_fwd", (c_all, w_mod, bias),
        [pl.BlockSpec((nb, d), lambda i, j: (0, 0)),
         pl.BlockSpec((None, d, tn), lambda i, j: (layer, 0, j)),
         pl.BlockSpec((1, tn), lambda i, j: (0, j))],
        (1, nl // tn), NN,
        jax.ShapeDtypeStruct((nb, nl), F32),
        pl.BlockSpec((nb, tn), lambda i, j: (0, j)), epilogue, a_prologue=_silu)


def _mod_bwd_w(c_all, dm_local):
    nb, d = c_all.shape
    nl = dm_local.shape[-1]
    tm = _tile(d, (1024, 512, 256, 128))
    tn = _tile(nl, (768, 512, 384, 256, 128))

    def epilogue(acc, extra, outs):
        outs[0][...] = acc

    return _matmul(
        "mod_dw", (c_all, dm_local),
        [pl.BlockSpec((nb, tm), lambda i, j: (0, i)),
         pl.BlockSpec((nb, tn), lambda i, j: (0, j))],
        (d // tm, nl // tn), TN,
        jax.ShapeDtypeStruct((d, nl), F32),
        pl.BlockSpec((tm, tn), lambda i, j: (i, j)), epilogue, a_prologue=_silu)


def _rows_call(name, body, operands, in_specs, out_shape, out_specs, n_tiles):
    return pl.pallas_call(
        body, name=name, grid=(n_tiles,), in_specs=in_specs, out_specs=out_specs, out_shape=out_shape,
        compiler_params=_params(1),
    )(*operands)


def _row_spec(tr, width):
    return pl.BlockSpec((tr, width), lambda i: (i, 0))


def _vec_spec(width):
    return pl.BlockSpec((1, width), lambda i: (0, 0))


def _accumulate(ref, val):
    first = pl.program_id(0) == 0

    @pl.when(first)
    def _():
        ref[...] = val

    @pl.when(jnp.logical_not(first))
    def _():
        ref[...] += val


def _prenorm(x, g, scale, shift):
    s, d = x.shape
    tr = _tile(s, (256, 128))

    def body(x_ref, g_ref, sc_ref, sh_ref, h_ref):
        xv = x_ref[...]
        rstd = lax.rsqrt(jnp.mean(xv * xv, axis=-1, keepdims=True) + EPS)
        h_ref[...] = ((xv * rstd) * g_ref[...] * (1.0 + sc_ref[...]) + sh_ref[...]).astype(BF16)

    return _rows_call("prenorm", body, (x, g, scale, shift),
                      [_row_spec(tr, d), _vec_spec(d), _vec_spec(d), _vec_spec(d)],
                      jax.ShapeDtypeStruct((s, d), BF16), _row_spec(tr, d), s // tr)


def _gate_grads(dxv, out_ref, gate_ref, dout_ref, dgate_ref):
    dout_ref[...] = (gate_ref[...] * dxv).astype(BF16)
    _accumulate(dgate_ref, jnp.sum(dxv * out_ref[...].astype(F32), axis=0, keepdims=True))


def _prenorm_bwd(x, dh, dres, g, scale, below=None):
    s, d = x.shape
    tr = _tile(s, (256, 128))

    def body(x_ref, dh_ref, dres_ref, g_ref, sc_ref, *rest):
        dx_ref, dshift_ref, dscale_ref, dg_ref = rest[-6:-2] if below else rest
        xv = x_ref[...]
        dhv = dh_ref[...].astype(F32)
        rstd = lax.rsqrt(jnp.mean(xv * xv, axis=-1, keepdims=True) + EPS)
        xhat = xv * rstd
        gv = g_ref[...]
        one_sc = 1.0 + sc_ref[...]
        dxhat = dhv * gv * one_sc
        dxv = dres_ref[...] + rstd * (dxhat - xhat * jnp.mean(dxhat * xhat, axis=-1, keepdims=True))
        dx_ref[...] = dxv
        _accumulate(dshift_ref, jnp.sum(dhv, axis=0, keepdims=True))
        _accumulate(dscale_ref, jnp.sum(dhv * xhat * gv, axis=0, keepdims=True))
        _accumulate(dg_ref, jnp.sum(dhv * xhat * one_sc, axis=0, keepdims=True))
        if below:
            _gate_grads(dxv, rest[0], rest[1], rest[-2], rest[-1])

    vec = jax.ShapeDtypeStruct((1, d), F32)
    operands = (x, dh, dres, g, scale) + (tuple(below) if below else ())
    in_specs = [_row_spec(tr, d), _row_spec(tr, d), _row_spec(tr, d), _vec_spec(d), _vec_spec(d)]
    out_shape = [jax.ShapeDtypeStruct((s, d), F32), vec, vec, vec]
    out_specs = [_row_spec(tr, d), _vec_spec(d), _vec_spec(d), _vec_spec(d)]
    if below:
        in_specs += [_row_spec(tr, d), _vec_spec(d)]
        out_shape += [jax.ShapeDtypeStruct((s, d), BF16), vec]
        out_specs += [_row_spec(tr, d), _vec_spec(d)]
    return _rows_call("prenorm_bwd", body, operands, in_specs, out_shape, out_specs, s // tr)


def _final_loss(x, target, g, out_below, gate_below):
    s, d = x.shape
    tr = _tile(s, (256, 128))
    n_tiles = s // tr

    def body(x_ref, t_ref, g_ref, out_ref, gate_ref, loss_ref, dx_ref, dg_ref, dout_ref, dgate_ref, acc_ref):
        xv = x_ref[...]
        rstd = lax.rsqrt(jnp.mean(xv * xv, axis=-1, keepdims=True) + EPS)
        xhat = xv * rstd
        gv = g_ref[...]
        err = xhat * gv - t_ref[...]
        dy = err * (1.0 / d)
        dxhat = dy * gv
        dxv = rstd * (dxhat - xhat * jnp.mean(dxhat * xhat, axis=-1, keepdims=True))
        dx_ref[...] = dxv
        _accumulate(dg_ref, jnp.sum(dy * xhat, axis=0, keepdims=True))
        _accumulate(acc_ref, jnp.sum(err * err, axis=0, keepdims=True))
        _gate_grads(dxv, out_ref, gate_ref, dout_ref, dgate_ref)

        @pl.when(pl.program_id(0) == n_tiles - 1)
        def _():
            loss_ref[...] = (0.5 / d) * jnp.sum(acc_ref[...], axis=1, keepdims=True)

    vec = jax.ShapeDtypeStruct((1, d), F32)
    return pl.pallas_call(
        body, name="final_loss", grid=(n_tiles,),
        in_specs=[_row_spec(tr, d), _row_spec(tr, d), _vec_spec(d), _row_spec(tr, d), _vec_spec(d)],
        out_specs=[pl.BlockSpec((1, 1), lambda i: (0, 0)), _row_spec(tr, d), _vec_spec(d), _row_spec(tr, d),
                   _vec_spec(d)],
        out_shape=[jax.ShapeDtypeStruct((1, 1), F32), jax.ShapeDtypeStruct((s, d), F32), vec,
                   jax.ShapeDtypeStruct((s, d), BF16), vec],
        scratch_shapes=[pltpu.VMEM((1, d), F32)],
        compiler_params=_params(1),
    )(x, target, g, out_below, gate_below)


def _rope(t, cos, sin):
    return t * cos + pltpu.roll(t, HEAD_DIM // 2, axis=1) * sin


def _unrope(dt, cos, sin):
    return dt * cos + pltpu.roll(dt * sin, HEAD_DIM // 2, axis=1)


def _band_blocks(s, dil):
    sub = s // dil
    kw = min(K_WINDOW, sub)

    def rows(r, start, n):
        if dil == 1:
            return pl.ds(pl.multiple_of(start, RADIUS), n)
        return pl.ds(r + dil * start, n, stride=dil)

    def window(idx):
        nb = sub // Q_BLOCK
        r, b = idx // nb, idx % nb
        q0 = b * Q_BLOCK
        start = jnp.clip(q0 - RADIUS, 0, sub - kw)
        ahead = (lax.broadcasted_iota(jnp.int32, (Q_BLOCK, kw), 1)
                 - lax.broadcasted_iota(jnp.int32, (Q_BLOCK, kw), 0)) + (start - q0 + RADIUS)
        valid = lax.bitcast_convert_type(ahead, jnp.uint32) <= 2 * RADIUS
        return rows(r, q0, Q_BLOCK), rows(r, start, kw), valid

    return window


def _store_column_tiles(tiles, dst_ref, sems, col_blocks):
    rows = tiles.shape[1]
    copies = []
    for g, cb in enumerate(col_blocks):
        cols = pl.ds(pl.multiple_of(cb * LANES, LANES), LANES)
        cp = pltpu.make_async_copy(tiles.at[g], dst_ref.at[pl.ds(0, rows), cols], sems.at[g])
        cp.start()
        copies.append(cp)
    for cp in copies:
        cp.wait()


def _head_col(s, group, nh):
    return pl.BlockSpec((s, HEAD_DIM), lambda h: (0, group * nh + h), pipeline_mode=pl.Buffered(1))


def _attn_fwd(proj, cos, sin, aw, plan=None):
    s = proj.shape[0]
    nh = aw // HEAD_DIM
    scale = HEAD_DIM ** -0.5
    n_blocks = s // Q_BLOCK

    def body(q_ref, k_ref, v_ref, cos_ref, sin_ref, attn_ref, lse_ref, qf, kf, vf, acc):
        cosv, sinv = cos_ref[...], sin_ref[...]
        qf[...] = _rope(q_ref[...].astype(F32), cosv, sinv) * scale
        kf[...] = _rope(k_ref[...].astype(F32), cosv, sinv)
        vf[...] = v_ref[...].astype(F32)

        for pattern, dil in enumerate(DILATIONS):
            window = _band_blocks(s, dil)

            def block(idx, carry, window=window, first=(pattern == 0)):
                q_rows, k_rows, valid = window(idx)
                q = qf[q_rows, :].astype(BF16)
                kk = kf[k_rows, :].astype(BF16)
                vv = vf[k_rows, :].astype(BF16)
                sc = lax.dot_general(q, kk, NT, preferred_element_type=F32)
                sc = jnp.where(valid, sc, NEG_INF)
                m = jnp.max(sc, axis=1, keepdims=True)
                p = jnp.exp(sc - m)
                den = jnp.sum(p, axis=1, keepdims=True)
                o = lax.dot_general(p.astype(BF16), vv, NN, preferred_element_type=F32) / den
                lse = jnp.broadcast_to(m + jnp.log(den), (Q_BLOCK, HEAD_DIM))
                if first:
                    acc[q_rows, :] = o
                    lse_ref[q_rows, :] = lse
                else:
                    lse_old = lse_ref[q_rows, :]
                    top = jnp.maximum(lse_old, lse)
                    w_old, w_new = jnp.exp(lse_old - top), jnp.exp(lse - top)
                    tot = w_old + w_new
                    acc[q_rows, :] = (acc[q_rows, :] * w_old + o * w_new) / tot
                    lse_ref[q_rows, :] = top + jnp.log(tot)
                return carry

            lax.fori_loop(0, n_blocks, block, 0, unroll=ATTN_UNROLL)

        attn_ref[...] = acc[...].astype(BF16)

    table = pl.BlockSpec((s, HEAD_DIM), lambda h: (0, 0), pipeline_mode=pl.Buffered(1))
    out = pl.BlockSpec((s, HEAD_DIM), lambda h: (0, h))
    return _call(
        body, name="attn_fwd", grid=(nh,),
        in_specs=[_head_col(s, 0, nh), _head_col(s, 1, nh), _head_col(s, 2, nh), table, table],
        out_specs=[out, out],
        out_shape=[jax.ShapeDtypeStruct((s, aw), BF16), jax.ShapeDtypeStruct((s, aw), F32)],
        scratch_shapes=[pltpu.VMEM((s, HEAD_DIM), F32)] * 4,
        operands=(proj, proj, proj, cos, sin), parallel=1, plan=plan)


def _attn_bwd(proj, cos, sin, dy, attn, lse, aw, plan=None):
    s = proj.shape[0]
    nh = aw // HEAD_DIM
    scale = HEAD_DIM ** -0.5
    n_blocks = s // Q_BLOCK

    def body(q_ref, k_ref, v_ref, za_ref, cos_ref, sin_ref, dy_ref, attn_ref, lse_ref,
             dproj_ref, qf, kf, vf, dof, delta, dqa, dka, dva, tiles, tile_sems):
        cosv, sinv = cos_ref[...], sin_ref[...]
        qf[...] = _rope(q_ref[...].astype(F32), cosv, sinv) * scale
        kf[...] = _rope(k_ref[...].astype(F32), cosv, sinv)
        vf[...] = v_ref[...].astype(F32)
        dyv, zav, attnv = dy_ref[...].astype(F32), za_ref[...].astype(F32), attn_ref[...].astype(F32)
        do_all = dyv * _silu(zav)
        dof[...] = do_all
        tiles[3] = (dyv * attnv * _dsilu(zav)).astype(BF16)
        delta[...] = jnp.broadcast_to(jnp.sum(do_all * attnv, axis=1, keepdims=True), (s, HEAD_DIM))
        dqa[...] = jnp.zeros_like(dqa)
        dka[...] = jnp.zeros_like(dka)
        dva[...] = jnp.zeros_like(dva)

        for dil in DILATIONS:
            window = _band_blocks(s, dil)

            def block(idx, carry, window=window):
                q_rows, k_rows, valid = window(idx)
                q = qf[q_rows, :].astype(BF16)
                kk = kf[k_rows, :].astype(BF16)
                vv = vf[k_rows, :].astype(BF16)
                dov = dof[q_rows, :].astype(BF16)
                lse_q = lse_ref[q_rows, :][:, 0:1]
                delta_q = delta[q_rows, :][:, 0:1]
                sc = lax.dot_general(q, kk, NT, preferred_element_type=F32)
                p = jnp.where(valid, jnp.exp(sc - lse_q), 0.0)
                dp = lax.dot_general(dov, vv, NT, preferred_element_type=F32)
                ds = (p * (dp - delta_q)).astype(BF16)
                dqa[q_rows, :] += lax.dot_general(ds, kk, NN, preferred_element_type=F32)
                dka[k_rows, :] += lax.dot_general(ds, q, TN, preferred_element_type=F32)
                dva[k_rows, :] += lax.dot_general(p.astype(BF16), dov, TN, preferred_element_type=F32)
                return carry

            lax.fori_loop(0, n_blocks, block, 0, unroll=ATTN_UNROLL)

        tiles[0] = (_unrope(dqa[...], cosv, sinv) * scale).astype(BF16)
        tiles[1] = _unrope(dka[...], cosv, sinv).astype(BF16)
        tiles[2] = dva[...].astype(BF16)
        _store_column_tiles(tiles, dproj_ref, tile_sems, [g * nh + pl.program_id(0) for g in range(4)])

    own = pl.BlockSpec((s, HEAD_DIM), lambda h: (0, h), pipeline_mode=pl.Buffered(1))
    table = pl.BlockSpec((s, HEAD_DIM), lambda h: (0, 0), pipeline_mode=pl.Buffered(1))
    return _call(
        body, name="attn_bwd", grid=(nh,),
        in_specs=[_head_col(s, 0, nh), _head_col(s, 1, nh), _head_col(s, 2, nh), _head_col(s, 3, nh),
                  table, table, own, own, own],
        out_specs=HBM_SPEC,
        out_shape=jax.ShapeDtypeStruct((s, 8 * aw), BF16),
        scratch_shapes=[pltpu.VMEM((s, HEAD_DIM), F32)] * 8 + [
            pltpu.VMEM((4, s, HEAD_DIM), BF16), pltpu.SemaphoreType.DMA((4,))],
        operands=(proj, proj, proj, proj, cos, sin, dy, attn, lse), plan=plan)


def _rope_tables(s):
    half = HEAD_DIM // 2
    inv = np.float32(ROPE_THETA) ** (-np.arange(half, dtype=np.float32) / np.float32(half))
    ang = np.arange(s, dtype=np.float32)[:, None] * inv[None, :]
    cos, sin = np.cos(ang), np.sin(ang)
    return (jnp.asarray(np.concatenate([cos, cos], axis=-1), F32),
            jnp.asarray(np.concatenate([-sin, sin], axis=-1), F32))


def _conv_chunks(s):
    for k in range(s // CONV_ROWS):
        lo = max(0, k * CONV_ROWS - CONV_HALO)
        hi = min(s, (k + 1) * CONV_ROWS + CONV_HALO)
        yield k * CONV_ROWS, lo, hi


def _neighbours(p, lo, s):
    n = p.shape[0]
    row = lo + lax.broadcasted_iota(jnp.int32, p.shape, 0)
    prev = jnp.where(row == 0, 0.0, pltpu.roll(p, 1, axis=0))
    nxt = jnp.where(row == s - 1, 0.0, pltpu.roll(p, n - 1, axis=0))
    return prev, nxt


def _ab_mix(attn, proj, conv_w, aw):
    s = proj.shape[0]
    nt = aw // LANES

    def col(group, sel):
        return pl.BlockSpec((s, LANES), lambda i: (0, group * nt + sel(i)))

    a_sel = lambda i: jnp.minimum(i, nt - 1)
    b_sel = lambda i: jnp.maximum(i - nt, 0)

    def body(attn_ref, za_ref, ub_ref, gb_ref, gc_ref, zb_ref, w_ref, y_ref):
        i = pl.program_id(0)

        @pl.when(i < nt)
        def _():
            y_ref[...] = (attn_ref[...].astype(F32) * _silu(za_ref[...].astype(F32))).astype(BF16)

        @pl.when(i >= nt)
        def _():
            w = w_ref[...]
            for c0, lo, hi in _conv_chunks(s):
                p = gc_ref[lo:hi, :].astype(F32) * ub_ref[lo:hi, :].astype(F32)
                prev, nxt = _neighbours(p, lo, s)
                cv = w[0:1, :] * prev + w[1:2, :] * p + w[2:3, :] * nxt
                yb = gb_ref[lo:hi, :].astype(F32) * cv * _silu(zb_ref[lo:hi, :].astype(F32))
                y_ref[c0:c0 + CONV_ROWS, :] = yb[c0 - lo:c0 - lo + CONV_ROWS, :].astype(BF16)

    return pl.pallas_call(
        body, name="ab_mix", grid=(2 * nt,),
        in_specs=[pl.BlockSpec((s, LANES), lambda i: (0, a_sel(i))),
                  col(3, a_sel), col(4, b_sel), col(5, b_sel), col(6, b_sel), col(7, b_sel),
                  pl.BlockSpec((3, LANES), lambda i: (0, b_sel(i)))],
        out_specs=pl.BlockSpec((s, LANES), lambda i: (0, i)),
        out_shape=jax.ShapeDtypeStruct((s, 2 * aw), BF16),
        compiler_params=_params(1),
    )(attn, proj, proj, proj, proj, proj, conv_w)


def _conv_bwd(dproj, dy, proj, conv_w, aw):
    s = proj.shape[0]
    nt = aw // LANES

    def col(group):
        return pl.BlockSpec((s, LANES), lambda i: (0, group * nt + i))

    def body(dyb_ref, ub_ref, gb_ref, gc_ref, zb_ref, w_ref, dproj_in, dproj_ref, dw_ref, tiles, tile_sems):
        w = w_ref[...]
        dw = [jnp.zeros((1, LANES), F32) for _ in range(3)]
        for c0, lo, hi in _conv_chunks(s):
            ctr = slice(c0 - lo, c0 - lo + CONV_ROWS)
            out_rows = slice(c0, c0 + CONV_ROWS)
            ub = ub_ref[lo:hi, :].astype(F32)
            gc = gc_ref[lo:hi, :].astype(F32)
            gb = gb_ref[lo:hi, :].astype(F32)
            zb = zb_ref[lo:hi, :].astype(F32)
            dyb = dyb_ref[lo:hi, :].astype(F32)
            p = gc * ub
            prev, nxt = _neighbours(p, lo, s)
            cv = w[0:1, :] * prev + w[1:2, :] * p + w[2:3, :] * nxt
            sz = _silu(zb)
            dcv = dyb * gb * sz
            dprev, dnxt = _neighbours(dcv, lo, s)
            dp = w[0:1, :] * dnxt + w[1:2, :] * dcv + w[2:3, :] * dprev
            for t, nb in enumerate((prev, p, nxt)):
                dw[t] = dw[t] + jnp.sum((dcv * nb)[ctr, :], axis=0, keepdims=True)
            tiles[0, out_rows, :] = (dp * gc)[ctr, :].astype(BF16)
            tiles[1, out_rows, :] = (dyb * cv * sz)[ctr, :].astype(BF16)
            tiles[2, out_rows, :] = (dp * ub)[ctr, :].astype(BF16)
            tiles[3, out_rows, :] = (dyb * gb * cv * _dsilu(zb))[ctr, :].astype(BF16)
        dw_ref[...] = jnp.concatenate(dw, axis=0)
        _store_column_tiles(tiles, dproj_ref, tile_sems, [(4 + g) * nt + pl.program_id(0) for g in range(4)])

    return pl.pallas_call(
        body, name="conv_bwd", grid=(nt,),
        in_specs=[pl.BlockSpec((s, LANES), lambda i: (0, nt + i)),
                  col(4), col(5), col(6), col(7),
                  pl.BlockSpec((3, LANES), lambda i: (0, i)), HBM_SPEC],
        out_specs=[HBM_SPEC, pl.BlockSpec((3, LANES), lambda i: (0, i))],
        out_shape=[jax.ShapeDtypeStruct(dproj.shape, dproj.dtype), jax.ShapeDtypeStruct((3, aw), F32)],
        scratch_shapes=[pltpu.VMEM((4, s, LANES), BF16), pltpu.SemaphoreType.DMA((4,))],
        input_output_aliases={6: 0},
        compiler_params=_params(1),
    )(dy, proj, proj, proj, proj, conv_w, dproj)


def _sgu_norm(v, ln_g, ln_b):
    gv = _gelu(v)
    mu = jnp.mean(gv, axis=-1, keepdims=True)
    xc = gv - mu
    rstd = lax.rsqrt(jnp.mean(xc * xc, axis=-1, keepdims=True) + EPS)
    vhat = xc * rstd
    return vhat, rstd, vhat * ln_g + ln_b


def _sgu_fwd(uvz, ln_g, ln_b, w_s, b_s, cw):
    s = uvz.shape[0]
    tr = 2 * CHUNK if s % (2 * CHUNK) == 0 else CHUNK
    gw = cw // N_GROUPS

    def body(u_ref, v_ref, z_ref, g_ref, b_ref, ws_ref, bs_ref, y_ref):
        _, _, vn = _sgu_norm(v_ref[...].astype(F32), g_ref[...], b_ref[...])
        vn = vn.astype(BF16)
        for ch in range(tr // CHUNK):
            rows = slice(ch * CHUNK, (ch + 1) * CHUNK)
            for grp in range(N_GROUPS):
                cols = slice(grp * gw, (grp + 1) * gw)
                mixed = lax.dot_general(ws_ref[grp], vn[rows, cols], NN, preferred_element_type=F32) + bs_ref[grp]
                y_ref[rows, cols] = (_gelu(u_ref[rows, cols].astype(F32)) * mixed
                                     * _silu(z_ref[rows, cols].astype(F32))).astype(BF16)

    full3 = lambda shape: pl.BlockSpec(shape, lambda i: (0, 0, 0))
    return pl.pallas_call(
        body, name="sgu_fwd", grid=(s // tr,),
        in_specs=[pl.BlockSpec((tr, cw), lambda i: (i, 0)), pl.BlockSpec((tr, cw), lambda i: (i, 1)),
                  pl.BlockSpec((tr, cw), lambda i: (i, 2)), _vec_spec(cw), _vec_spec(cw),
                  full3(w_s.shape), full3(b_s.shape)],
        out_specs=pl.BlockSpec((tr, cw), lambda i: (i, 0)),
        out_shape=jax.ShapeDtypeStruct((s, cw), BF16),
        compiler_params=_params(1, parallel=1),
    )(uvz, uvz, uvz, ln_g, ln_b, w_s, b_s)


def _sgu_bwd(uvz, dy, ln_g, ln_b, w_s, b_s, cw, plan=None):
    s = uvz.shape[0]
    tr = 2 * CHUNK if s % (2 * CHUNK) == 0 else CHUNK
    gw = cw // N_GROUPS

    def body(u_ref, v_ref, z_ref, dy_ref, g_ref, b_ref, ws_ref, bs_ref,
             duvz_ref, dws_ref, dbs_ref, dg_ref, db_ref, dvn_ref):
        vv = v_ref[...].astype(F32)
        gvec = g_ref[...]
        vhat, rstd, vn = _sgu_norm(vv, gvec, b_ref[...])
        vn = vn.astype(BF16)
        first = pl.program_id(0) == 0

        @pl.when(first)
        def _():
            dws_ref[...] = jnp.zeros_like(dws_ref)
            dbs_ref[...] = jnp.zeros_like(dbs_ref)

        for ch in range(tr // CHUNK):
            rows = slice(ch * CHUNK, (ch + 1) * CHUNK)
            for grp in range(N_GROUPS):
                cols = slice(grp * gw, (grp + 1) * gw)
                vn_g = vn[rows, cols]
                mixed = lax.dot_general(ws_ref[grp], vn_g, NN, preferred_element_type=F32) + bs_ref[grp]
                uu = u_ref[rows, cols].astype(F32)
                zz = z_ref[rows, cols].astype(F32)
                dyv = dy_ref[rows, cols].astype(F32)
                gu, sz = _gelu(uu), _silu(zz)
                duvz_ref[rows, grp * gw:(grp + 1) * gw] = (dyv * mixed * sz * _dgelu(uu)).astype(BF16)
                duvz_ref[rows, 2 * cw + grp * gw:2 * cw + (grp + 1) * gw] = (
                    dyv * gu * mixed * _dsilu(zz)).astype(BF16)
                dmixed = dyv * gu * sz
                dm16 = dmixed.astype(BF16)
                dws_ref[grp] += lax.dot_general(dm16, vn_g, NT, preferred_element_type=F32)
                dbs_ref[grp] += jnp.broadcast_to(jnp.sum(dmixed, axis=1, keepdims=True), (CHUNK, LANES))
                dvn_ref[rows, cols] = lax.dot_general(ws_ref[grp], dm16, TN, preferred_element_type=F32)

        dvn = dvn_ref[...]
        _accumulate(dg_ref, jnp.sum(dvn * vhat, axis=0, keepdims=True))
        _accumulate(db_ref, jnp.sum(dvn, axis=0, keepdims=True))
        dvhat = dvn * gvec
        dgv = rstd * (dvhat - jnp.mean(dvhat, axis=-1, keepdims=True)
                      - vhat * jnp.mean(dvhat * vhat, axis=-1, keepdims=True))
        duvz_ref[:, cw:2 * cw] = (dgv * _dgelu(vv)).astype(BF16)

    full3 = lambda shape: pl.BlockSpec(shape, lambda i: (0, 0, 0))
    acc3 = jax.ShapeDtypeStruct((N_GROUPS, CHUNK, LANES), F32)
    vec = jax.ShapeDtypeStruct((1, cw), F32)
    row = pl.BlockSpec((tr, cw), lambda i: (i, 0))
    return _call(
        body, name="sgu_bwd", grid=(s // tr,),
        in_specs=[row, pl.BlockSpec((tr, cw), lambda i: (i, 1)), pl.BlockSpec((tr, cw), lambda i: (i, 2)),
                  row, _vec_spec(cw), _vec_spec(cw), full3(w_s.shape), full3(b_s.shape)],
        out_specs=[pl.BlockSpec((tr, 3 * cw), lambda i: (i, 0)), full3((N_GROUPS, CHUNK, LANES)),
                   full3((N_GROUPS, CHUNK, LANES)), _vec_spec(cw), _vec_spec(cw)],
        out_shape=[jax.ShapeDtypeStruct((s, 3 * cw), BF16), acc3, acc3, vec, vec],
        scratch_shapes=[pltpu.VMEM((tr, cw), F32)],
        operands=(uvz, uvz, uvz, dy, ln_g, ln_b, w_s, b_s), plan=plan)


def _flat_rows(a):
    return a.reshape(-1, a.shape[-1])


def _add_sibling(grad, recv, core_idx):
    nchip, k, n = grad.shape
    tr = _tile(k // 2, (256, 128))
    nb = (k // 2) // tr

    def body(c_ref, g_ref, r_ref, o_ref):
        o_ref[...] = (g_ref[...].astype(F32) + r_ref[...].astype(F32)).astype(BF16)

    return pl.pallas_call(
        body, name="add_sibling",
        grid_spec=pltpu.PrefetchScalarGridSpec(
            num_scalar_prefetch=1, grid=(nchip, nb),
            in_specs=[pl.BlockSpec((None, tr, n), lambda q, i, c: (q, c[0] * nb + i, 0)),
                      pl.BlockSpec((None, tr, n), lambda q, i, c: (q, i, 0))],
            out_specs=pl.BlockSpec((None, tr, n), lambda q, i, c: (q, i, 0))),
        out_shape=jax.ShapeDtypeStruct((nchip, k // 2, n), BF16),
        compiler_params=_params(2, parallel=2),
    )(core_idx, grad, recv)


def _sum_chips(own, others, reduced, layer, place_idx):
    _, kh, n = own.shape
    tr = _tile(kh, (256, 128))
    nb = kh // tr

    def body(place_ref, own_ref, oth_ref, red_ref, o_ref):
        acc = own_ref[...].astype(F32)
        for q in range(3):
            acc = acc + oth_ref[q].astype(F32)
        o_ref[...] = acc

    return pl.pallas_call(
        body, name="sum_chips",
        grid_spec=pltpu.PrefetchScalarGridSpec(
            num_scalar_prefetch=1, grid=(nb,),
            in_specs=[pl.BlockSpec((None, tr, n), lambda i, p: (p[0], i, 0)),
                      pl.BlockSpec((3, tr, n), lambda i, p: (0, i, 0)),
                      HBM_SPEC],
            out_specs=pl.BlockSpec((None, tr, n), lambda i, p: (layer, p[1] * nb + i, 0))),
        out_shape=jax.ShapeDtypeStruct(reduced.shape, reduced.dtype),
        input_output_aliases={3: 0},
        compiler_params=_params(1, parallel=1),
    )(place_idx, own, others, reduced)


def _sum_devices(parts, plan=None):
    nd, r, _ = parts.shape
    tr = _tile(r, (512, 256, 128, 64, 32, 16, 8))

    def body(p_ref, o_ref):
        acc = p_ref[0]
        for q in range(1, nd):
            acc = acc + p_ref[q]
        o_ref[...] = acc

    return _call(
        body, name="sum_devices", grid=(r // tr,),
        in_specs=[pl.BlockSpec((nd, tr, LANES), lambda i: (0, i, 0))],
        out_specs=pl.BlockSpec((tr, LANES), lambda i: (i, 0)),
        out_shape=jax.ShapeDtypeStruct((r, LANES), F32),
        operands=(parts,), parallel=1, plan=plan)


def _adamw(w, g, m, v, also_grad=False):
    r, n = w.shape
    tr = _tile(r, [p for p in (1024, 512, 256, 128, 64, 32, 16, 8) if p * n <= ELEMENTWISE_BLOCK])
    n_out = 4 if also_grad else 3

    def body(w_ref, g_ref, m_ref, v_ref, d_ref, nm_ref, nv_ref, *g_out):
        gv = g_ref[...]
        if also_grad:
            g_out[0][...] = gv
        nm = ADAM_B1 * m_ref[...] + (1.0 - ADAM_B1) * gv
        nv = ADAM_B2 * v_ref[...] + (1.0 - ADAM_B2) * (gv * gv)
        m_hat = nm / (1.0 - ADAM_B1 ** ADAM_STEP)
        v_hat = nv / (1.0 - ADAM_B2 ** ADAM_STEP)
        d_ref[...] = -ADAM_LR * (m_hat / (jnp.sqrt(v_hat) + ADAM_EPS) + ADAM_WD * w_ref[...])
        nm_ref[...] = nm
        nv_ref[...] = nv

    spec = pl.BlockSpec((tr, n), lambda i: (i, 0))
    shp = jax.ShapeDtypeStruct((r, n), F32)
    return _call(
        body, name="adamw", grid=(r // tr,),
        in_specs=[spec] * 4, out_specs=[spec] * n_out, out_shape=[shp] * n_out,
        operands=(w, g, m, v), parallel=1)


def _pack(arrays, row_multiple=8):
    flat = [a.reshape(-1) for a in arrays]
    sizes = [f.shape[0] for f in flat]
    total = sum(sizes)
    unit = LANES * row_multiple
    padded = -(-total // unit) * unit
    if padded > total:
        flat.append(jnp.zeros((padded - total,), F32))
    offsets = [sum(sizes[:i]) for i in range(len(sizes))]
    return jnp.concatenate(flat).reshape(-1, LANES), offsets


def _unpack(packed, offsets, shapes):
    flat = packed.reshape(-1)
    return [flat[o:o + math.prod(s)].reshape(s) for o, s in zip(offsets, shapes)]


def kernel(x, c, ab_norm_g, ab_w_mod, ab_b_mod, ab_w_in, ab_conv_w, ab_w_out, sg_norm_g, sg_w_mod, sg_b_mod, sg_w_in, sg_ln_g, sg_ln_b, sg_w_s, sg_b_s, sg_w_out, final_norm_g, loss_target, m_ab_norm_g, m_ab_w_mod, m_ab_b_mod, m_ab_w_in, m_ab_conv_w, m_ab_w_out, m_sg_norm_g, m_sg_w_mod, m_sg_b_mod, m_sg_w_in, m_sg_ln_g, m_sg_ln_b, m_sg_w_s, m_sg_b_s, m_sg_w_out, m_final_norm_g, v_ab_norm_g, v_ab_w_mod, v_ab_b_mod, v_ab_w_in, v_ab_conv_w, v_ab_w_out, v_sg_norm_g, v_sg_w_mod, v_sg_b_mod, v_sg_w_in, v_sg_ln_g, v_sg_ln_b, v_sg_w_s, v_sg_b_s, v_sg_w_out, v_final_norm_g):
    s, d = x.shape[1], x.shape[2]
    aw = d // 2
    cw = d
    mod_l = ab_w_mod.shape[-1]
    x0 = x[0]
    target = loss_target[0]
    mx, my, mc = lax.axis_index("x"), lax.axis_index("y"), lax.axis_index("c")
    chip = 2 * mx + my
    chip_idx = jnp.reshape(chip, (1,)).astype(jnp.int32)
    core_idx = jnp.reshape(mc, (1,)).astype(jnp.int32)
    place_idx = jnp.stack([chip, mc]).astype(jnp.int32)

    win = [_place_own_shard(ab_w_in if L % 2 == 0 else sg_w_in, L // 2, chip_idx) for L in range(4)]
    wout = [_place_own_shard(ab_w_out if L % 2 == 0 else sg_w_out, L // 2, chip_idx) for L in range(4)]
    k_in, k_out = d, wout[0].shape[1]

    def gather_plan(ici=(), pass_on=()):
        arrays, copies, names = [], [], []
        for kind, L, only in ici:
            arrays.append(win[L] if kind == "in" else wout[L])
            names.append((kind, L))
            copies += _gather_ici(len(arrays) - 1, k_in if kind == "in" else k_out, only)
        for kind, L in pass_on:
            arrays.append(win[L] if kind == "in" else wout[L])
            names.append((kind, L))
            copies += _gather_pass_on(len(arrays) - 1, k_in if kind == "in" else k_out)
        return _Plan(tuple(arrays), tuple(copies)), names

    def absorb(plan_and_names, updated):
        _, names = plan_and_names
        for pos, (kind, L) in enumerate(names):
            if kind == "in":
                win[L] = updated[pos]
            else:
                wout[L] = updated[pos]

    win[0] = _comm_stages("gather_first_w_in", [win[0]], [_gather_ici(0, k_in), _gather_pass_on(0, k_in)])[0]

    small_local = [c[0], ab_conv_w, sg_norm_g, sg_ln_g, sg_ln_b]
    small_shapes = [a.shape for a in small_local]
    payload, small_off = _pack(small_local)
    gathered = _all_to_all(jnp.broadcast_to(payload[None], (N_DEV,) + payload.shape), "gather_small")
    per_dev = [_unpack(gathered[b], small_off, small_shapes) for b in range(N_DEV)]
    c_all = jnp.stack([per_dev[b][0] for b in range(N_DEV)])

    def from_chips(idx, axis):
        return jnp.concatenate([per_dev[2 * q][idx] for q in range(N_CHIPS)], axis=axis)

    conv_w_full = from_chips(1, 2)
    sg_norm_g_full = from_chips(2, 1)
    sg_ln_g_full = from_chips(3, 1)
    sg_ln_b_full = from_chips(4, 1)

    ab_b_local = lax.dynamic_slice_in_dim(ab_b_mod, chip * mod_l, mod_l, axis=1)
    mod_rows = []
    for layer in range(4):
        i = layer // 2
        w_mod, bias = (ab_w_mod, ab_b_local) if layer % 2 == 0 else (sg_w_mod, sg_b_mod)
        mod_rows.append(_mod_fwd(c_all, w_mod, bias[i:i + 1], i))
    mod_local = jnp.stack(mod_rows, axis=1)
    mod_recv = _all_to_all(mod_local.reshape(N_DEV, -1, LANES), "exchange_mod")
    mod_recv = mod_recv.reshape(N_DEV, 4, mod_l)
    mod_full = jnp.concatenate([mod_recv[2 * q] for q in range(N_CHIPS)], axis=-1)
    shifts = [mod_full[l:l + 1, :d] for l in range(4)]
    scales = [mod_full[l:l + 1, d:2 * d] for l in range(4)]
    gates = [mod_full[l:l + 1, 2 * d:] for l in range(4)]

    cos, sin = _rope_tables(s)
    w_s16 = sg_w_s.astype(BF16)
    b_s3 = sg_b_s[..., None]

    near, far, everyone = (0, 1), (2,), (0, 1, 2)
    fwd_comm = {
        ("in_proj", 0): ([("in", 1, near), ("out", 0, everyone)], []),
        ("attn", 0): ([("in", 1, far), ("out", 1, everyone), ("in", 2, near)], [("out", 0)]),
        ("out_proj", 0): ([], [("in", 1), ("out", 1)]),
        ("in_proj", 1): ([("in", 2, far), ("out", 2, everyone)], []),
        ("out_proj", 1): ([], [("in", 2), ("out", 2)]),
        ("in_proj", 2): ([("in", 3, everyone)], []),
        ("attn", 2): ([("out", 3, everyone)], []),
        ("out_proj", 2): ([], [("in", 3), ("out", 3)]),
    }

    def carried(key, fn, *args):
        if key not in fwd_comm:
            return fn(*args)
        pn = gather_plan(*fwd_comm[key])
        res, updated = fn(*args, plan=pn[0])
        absorb(pn, updated)
        return res

    saved = []
    xs = x0
    for layer in range(4):
        i = layer // 2
        if layer % 2 == 0:
            h = _prenorm(xs, ab_norm_g[i:i + 1], scales[layer], shifts[layer])
            proj = carried(("in_proj", layer), _in_proj, h, win[layer])
            attn, lse = carried(("attn", layer), _attn_fwd, proj, cos, sin, aw)
            y = _ab_mix(attn, proj, conv_w_full[i], aw)
            x_next, out = carried(("out_proj", layer), _out_proj_residual, y, wout[layer].reshape(-1, d), xs,
                                  gates[layer])
            saved.append((xs, h, proj, y, out, attn, lse))
        else:
            h = _prenorm(xs, sg_norm_g_full[i:i + 1], scales[layer], shifts[layer])
            uvz = carried(("in_proj", layer), _in_proj, h, win[layer])
            y = _sgu_fwd(uvz, sg_ln_g_full[i:i + 1], sg_ln_b_full[i:i + 1], w_s16[i], b_s3[i], cw)
            x_next, out = carried(("out_proj", layer), _out_proj_residual, y, wout[layer].reshape(-1, d), xs,
                                  gates[layer])
            saved.append((xs, h, uvz, y, out))
        xs = x_next

    loss11, dx, d_final_g, dout, dgate = _final_loss(xs, target, final_norm_g[None], saved[3][4], gates[3])
    loss = lax.psum(loss11[0, 0], ("x", "y", "c"))

    reduced = {"in": [lax.empty((2,) + w.shape[1:], F32) for w in (ab_w_in, sg_w_in)],
               "out": [lax.empty((2,) + w.shape[1:], F32) for w in (ab_w_out, sg_w_out)]}
    grads = {}
    stage = {}
    k_of = {"in": k_in, "out": k_out}

    def swap_plan(which):
        arrays, copies = [], []
        for kind, L in which:
            g = grads[kind, L]
            arrays += [g, jax.ShapeDtypeStruct((N_CHIPS, g.shape[1] // 2, g.shape[2]), BF16)]
            copies += _reduce_swap(len(arrays) - 2, len(arrays) - 1, k_of[kind])
        return _Plan(tuple(arrays), tuple(copies))

    def after_swap(which, updated):
        for pos, (kind, L) in enumerate(which):
            stage[kind, L] = _add_sibling(grads[kind, L], updated[2 * pos + 1], core_idx)

    def ici_plan(pieces):
        arrays, copies = [], []
        for kind, L, only in pieces:
            cs = stage[kind, L]
            arrays += [cs, stage.get((kind, L, "recv"), jax.ShapeDtypeStruct((3,) + cs.shape[1:], BF16))]
            copies += _reduce_ici(len(arrays) - 2, len(arrays) - 1, only)
        return _Plan(tuple(arrays), tuple(copies))

    def after_ici(pieces, updated):
        for pos, (kind, L, _) in enumerate(pieces):
            stage[kind, L, "recv"] = updated[2 * pos + 1]

    def sum_layer(L):
        for kind in ("in", "out"):
            reduced[kind][L % 2] = _sum_chips(stage[kind, L], stage[kind, L, "recv"], reduced[kind][L % 2],
                                              L // 2, place_idx)

    def share_plan(L):
        arrays = (reduced["in"][L % 2], reduced["out"][L % 2])
        copies = _reduce_share(0, L // 2, k_in) + _reduce_share(1, L // 2, k_out)
        return _Plan(arrays, tuple(copies))

    def after_share(L, updated):
        reduced["in"][L % 2], reduced["out"][L % 2] = updated[0], updated[1]

    all_chips = (0, 1, 2)
    dmods = [None] * 4
    d_ab_norm_g, d_sg_norm_g = [None, None], [None, None]
    d_conv_w, d_ln_g, d_ln_b, d_w_s, d_b_s = ([None, None] for _ in range(5))
    for layer in reversed(range(4)):
        i = layer // 2
        prev = layer + 1
        busy = prev < 4
        if layer % 2 == 0:
            xs, h, proj, y, out, attn, lse = saved[layer]
        else:
            xs, h, uvz, y, out = saved[layer]
        below = (saved[layer - 1][4], gates[layer - 1]) if layer > 0 else None
        w2 = wout[layer].reshape(-1, d)
        grads["out", layer] = _out_proj_bwd_w(y, dout).reshape(N_CHIPS, -1, d)
        if busy:
            swapped = [("in", prev), ("out", prev)] + ([("out", 0)] if layer == 0 else [])
            dy, updated = _out_proj_bwd_act(dout, w2, plan=swap_plan(swapped))
            after_swap(swapped, updated)
        else:
            dy = _out_proj_bwd_act(dout, w2)
        if layer % 2 == 0:
            if busy:
                pieces = [("in", prev, all_chips), ("out", prev, all_chips)]
                pieces += [("out", 0, all_chips)] if layer == 0 else []
                dact, updated = _attn_bwd(proj, cos, sin, dy, attn, lse, aw, plan=ici_plan(pieces))
                after_ici(pieces, updated)
                sum_layer(prev)
            else:
                dact = _attn_bwd(proj, cos, sin, dy, attn, lse, aw)
            dact, d_conv_w[i] = _conv_bwd(dact, dy, proj, conv_w_full[i], aw)
            if layer == 0:
                gating, gating_off = _pack([jnp.stack(d_w_s), jnp.stack(d_b_s)], PACKED_ROW_BLOCK)
                slots = _place_own_slot(gating, jnp.reshape(2 * chip + mc, (1,)).astype(jnp.int32))
                sharing = share_plan(prev)
                plan = _merge_plans(sharing, _Plan((gating, slots), tuple(_broadcast_copies(0, 1))))
                grads["in", 0], updated = _in_proj_bwd_w(h, dact, win[0].shape[-1], plan=plan)
                after_share(prev, updated)
                all_gating = updated[len(sharing.arrays) + 1]
                plan = swap_plan([("in", 0)])
                after_swap([("in", 0)], _comm_stages("grads_to_sibling", plan.arrays, [plan.copies]))
                pieces = [("in", 0, all_chips)]
                dh, updated = _in_proj_bwd_act(dact, win[0], plan=ici_plan(pieces))
                after_ici(pieces, updated)
            elif busy:
                dh, updated = _in_proj_bwd_act(dact, win[layer], plan=share_plan(prev))
                after_share(prev, updated)
                grads["in", layer] = _in_proj_bwd_w(h, dact, win[layer].shape[-1])
            else:
                dh = _in_proj_bwd_act(dact, win[layer])
                grads["in", layer] = _in_proj_bwd_w(h, dact, win[layer].shape[-1])
            norm_g = ab_norm_g[i:i + 1]
        else:
            sgu_args = (uvz, dy, sg_ln_g_full[i:i + 1], sg_ln_b_full[i:i + 1], w_s16[i], b_s3[i], cw)
            if busy:
                pieces = [("in", prev, (0, 1)), ("out", prev, all_chips)]
                res, updated = _sgu_bwd(*sgu_args, plan=ici_plan(pieces))
                after_ici(pieces, updated)
            else:
                res = _sgu_bwd(*sgu_args)
            dact, d_w_s[i], db_wide, d_ln_g[i], d_ln_b[i] = res
            d_b_s[i] = db_wide[:, :, 0]
            if busy:
                pieces = [("in", prev, (2,))]
                dh, updated = _in_proj_bwd_act(dact, win[layer], plan=ici_plan(pieces))
                after_ici(pieces, updated)
                sum_layer(prev)
                grads["in", layer], updated = _in_proj_bwd_w(h, dact, win[layer].shape[-1], plan=share_plan(prev))
                after_share(prev, updated)
            else:
                dh = _in_proj_bwd_act(dact, win[layer])
                grads["in", layer] = _in_proj_bwd_w(h, dact, win[layer].shape[-1])
            norm_g = sg_norm_g_full[i:i + 1]
        res = _prenorm_bwd(xs, dh, dx, norm_g, scales[layer], below)
        dx, dshift, dscale, d_norm_g = res[:4]
        (d_ab_norm_g if layer % 2 == 0 else d_sg_norm_g)[i] = d_norm_g
        dmods[layer] = jnp.concatenate([dshift, dscale, dgate], axis=1)
        if below:
            dout, dgate = res[4:]
    grad_x = dx[None]

    partial_list = [jnp.concatenate(dmods, axis=0),
                    jnp.concatenate(d_ab_norm_g, axis=0), jnp.concatenate(d_sg_norm_g, axis=0), d_final_g[0],
                    jnp.stack(d_conv_w), jnp.concatenate(d_ln_g, axis=0), jnp.concatenate(d_ln_b, axis=0)]
    partial_shapes = [a.shape for a in partial_list]
    partials, part_off = _pack(partial_list, PACKED_ROW_BLOCK)
    all_partials = _all_to_all(jnp.broadcast_to(partials[None], (N_DEV,) + partials.shape), "gather_partials")
    sum_layer(0)
    summed_packed, updated = _sum_devices(all_partials, plan=share_plan(0))
    after_share(0, updated)
    summed = _unpack(summed_packed, part_off, partial_shapes)
    g_mod_bias, g_ab_norm_g, g_sg_norm_g_full, g_final_g, g_conv_full, g_ln_g_full, g_ln_b_full = summed
    g_w_s, g_b_s = _unpack(_sum_devices(all_gating), gating_off, [sg_w_s.shape, sg_b_s.shape])
    dm_all = jnp.stack([_unpack(all_partials[b], part_off[:1], partial_shapes[:1])[0] for b in range(N_DEV)])
    dm_local = lax.dynamic_slice_in_dim(dm_all, chip * mod_l, mod_l, axis=2)

    def chip_cols(a, axis):
        width = a.shape[axis] // N_CHIPS
        return lax.dynamic_slice_in_dim(a, chip * width, width, axis=axis)

    g_ab_b_mod = jnp.stack([g_mod_bias[0], g_mod_bias[2]])
    g_sg_b_mod = chip_cols(jnp.stack([g_mod_bias[1], g_mod_bias[3]]), 1)
    g_ab_w_mod = jnp.stack([_mod_bwd_w(c_all, dm_local[:, 0]), _mod_bwd_w(c_all, dm_local[:, 2])])
    g_sg_w_mod = jnp.stack([_mod_bwd_w(c_all, dm_local[:, 1]), _mod_bwd_w(c_all, dm_local[:, 3])])
    g_conv = chip_cols(g_conv_full, 2)
    g_sg_norm_g = chip_cols(g_sg_norm_g_full, 1)
    g_ln_g = chip_cols(g_ln_g_full, 1)
    g_ln_b = chip_cols(g_ln_b_full, 1)

    def step_big(w, g, m, v, also_grad=False):
        res = _adamw(_flat_rows(w), _flat_rows(g), _flat_rows(m), _flat_rows(v), also_grad)
        return tuple(a.reshape(w.shape) for a in res)

    big_out = {
        "ab_w_mod": step_big(ab_w_mod, g_ab_w_mod, m_ab_w_mod, v_ab_w_mod),
        "ab_w_in": step_big(ab_w_in, reduced["in"][0], m_ab_w_in, v_ab_w_in, True),
        "ab_w_out": step_big(ab_w_out, reduced["out"][0], m_ab_w_out, v_ab_w_out, True),
        "sg_w_mod": step_big(sg_w_mod, g_sg_w_mod, m_sg_w_mod, v_sg_w_mod),
        "sg_w_in": step_big(sg_w_in, reduced["in"][1], m_sg_w_in, v_sg_w_in, True),
        "sg_w_out": step_big(sg_w_out, reduced["out"][1], m_sg_w_out, v_sg_w_out, True),
    }
    g_ab_w_in, g_ab_w_out = big_out["ab_w_in"][3], big_out["ab_w_out"][3]
    g_sg_w_in, g_sg_w_out = big_out["sg_w_in"][3], big_out["sg_w_out"][3]
    small_names = ["ab_norm_g", "ab_b_mod", "ab_conv_w", "sg_norm_g", "sg_b_mod", "sg_ln_g", "sg_ln_b",
                   "sg_w_s", "sg_b_s", "final_norm_g"]
    small_w = [ab_norm_g, ab_b_mod, ab_conv_w, sg_norm_g, sg_b_mod, sg_ln_g, sg_ln_b, sg_w_s, sg_b_s, final_norm_g]
    small_g = [g_ab_norm_g, g_ab_b_mod, g_conv, g_sg_norm_g, g_sg_b_mod, g_ln_g, g_ln_b, g_w_s, g_b_s, g_final_g]
    small_m = [m_ab_norm_g, m_ab_b_mod, m_ab_conv_w, m_sg_norm_g, m_sg_b_mod, m_sg_ln_g, m_sg_ln_b, m_sg_w_s,
               m_sg_b_s, m_final_norm_g]
    small_v = [v_ab_norm_g, v_ab_b_mod, v_ab_conv_w, v_sg_norm_g, v_sg_b_mod, v_sg_ln_g, v_sg_ln_b, v_sg_w_s,
               v_sg_b_s, v_final_norm_g]
    shapes = [a.shape for a in small_w]
    pw, off = _pack(small_w, PACKED_ROW_BLOCK)
    pg, _ = _pack(small_g, PACKED_ROW_BLOCK)
    pm, _ = _pack(small_m, PACKED_ROW_BLOCK)
    pv, _ = _pack(small_v, PACKED_ROW_BLOCK)
    pd, pnm, pnv = _adamw(pw, pg, pm, pv)
    small_out = {}
    for name, dl, nm, nv in zip(small_names, _unpack(pd, off, shapes), _unpack(pnm, off, shapes),
                                _unpack(pnv, off, shapes)):
        small_out[name] = (dl, nm, nv)

    grad_of = {
        "ab_norm_g": g_ab_norm_g, "ab_w_mod": g_ab_w_mod, "ab_b_mod": g_ab_b_mod, "ab_w_in": g_ab_w_in,
        "ab_conv_w": g_conv, "ab_w_out": g_ab_w_out, "sg_norm_g": g_sg_norm_g, "sg_w_mod": g_sg_w_mod,
        "sg_b_mod": g_sg_b_mod, "sg_w_in": g_sg_w_in, "sg_ln_g": g_ln_g, "sg_ln_b": g_ln_b, "sg_w_s": g_w_s,
        "sg_b_s": g_b_s, "sg_w_out": g_sg_w_out, "final_norm_g": g_final_g,
    }
    order = ["ab_norm_g", "ab_w_mod", "ab_b_mod", "ab_w_in", "ab_conv_w", "ab_w_out", "sg_norm_g", "sg_w_mod",
             "sg_b_mod", "sg_w_in", "sg_ln_g", "sg_ln_b", "sg_w_s", "sg_b_s", "sg_w_out", "final_norm_g"]
    steps = {**big_out, **small_out}
    return (loss, grad_x, *[grad_of[n] for n in order], *[steps[n][0] for n in order],
            *[steps[n][1] for n in order], *[steps[n][2] for n in order])
```

```python
import math
from typing import Any, Callable, NamedTuple

import jax
import jax.numpy as jnp
import numpy as np
from jax import lax
from jax.experimental import pallas as pl
from jax.experimental.pallas import tpu as pltpu

F32 = jnp.float32
BF16 = jnp.bfloat16

HEAD_DIM = 128
RADIUS = 64
DILATIONS = (1, 4, 16)
Q_BLOCK = 256
K_WINDOW = Q_BLOCK + 2 * RADIUS
ATTN_UNROLL = 4
ROPE_THETA = 10000.0
NEG_INF = -1e30
N_GROUPS = 8
CHUNK = 128
EPS = 1e-6
CONV_ROWS = 512
CONV_HALO = 16
LANES = 128
ELEMENTWISE_BLOCK = 512 * 1024
PACKED_ROW_BLOCK = 512
N_DEV = 8
N_CHIPS = 4

ADAM_LR = 0.001
ADAM_B1 = 0.9
ADAM_B2 = 0.999
ADAM_EPS = 1e-08
ADAM_WD = 0.01
ADAM_STEP = 10

VMEM_LIMIT_V7X = 56 * 1024 * 1024

MESH_ID = pl.DeviceIdType.MESH
HBM_SPEC = pl.BlockSpec(memory_space=pltpu.HBM)

NN = (((1,), (0,)), ((), ()))
NT = (((1,), (1,)), ((), ()))
TN = (((0,), (0,)), ((), ()))


def _params(n_grid, parallel=0):
    sem = tuple(["parallel"] * parallel + ["arbitrary"] * (n_grid - parallel))
    return pltpu.CompilerParams(dimension_semantics=sem, vmem_limit_bytes=VMEM_LIMIT_V7X)


def _tile(n, prefs):
    for p in prefs:
        if n % p == 0:
            return p
    return n


def _sigmoid(z):
    return 1.0 / (1.0 + jnp.exp(-z))


def _silu(z):
    return z * _sigmoid(z)


_GELU_K = math.sqrt(2.0 / math.pi)
_GELU_C = 0.044715


def _gelu(u):
    return 0.5 * u * (1.0 + jnp.tanh(_GELU_K * (u + _GELU_C * u * u * u)))


def _gelu_and_grad(u):
    t = jnp.tanh(_GELU_K * (u + _GELU_C * u * u * u))
    half = 0.5 * (1.0 + t)
    return u * half, half + 0.5 * u * (1.0 - t * t) * _GELU_K * (1.0 + 3.0 * _GELU_C * u * u)


def _silu_and_grad(z):
    s = _sigmoid(z)
    return z * s, s * (1.0 + z * (1.0 - s))


class _Place(NamedTuple):
    x: Any
    y: Any
    c: Any
    chip: Any


def _my_place():
    mx, my, mc = lax.axis_index("x"), lax.axis_index("y"), lax.axis_index("c")
    return _Place(mx, my, mc, 2 * mx + my)


def _other_chips(p):
    return [(1 - p.x, p.y), (p.x, 1 - p.y), (1 - p.x, 1 - p.y)]


class _Copy(NamedTuple):
    src: int
    src_at: Callable
    dst: int
    dst_at: Callable
    peer: Callable


class _Plan(NamedTuple):
    arrays: tuple
    copies: tuple


def _view(ref, index):
    return ref if index is None else ref.at[index]


def _plan_io(plan):
    ins = [k for k, a in enumerate(plan.arrays) if not isinstance(a, jax.ShapeDtypeStruct)]
    written = sorted({cp.dst for cp in plan.copies})
    return ins, written


def _descriptors(plan, in_refs, out_refs, send_sems, recv_sems):
    ins, written = _plan_io(plan)
    place = _my_place()
    return [
        pltpu.make_async_remote_copy(
            src_ref=_view(in_refs[ins.index(cp.src)], cp.src_at(place)),
            dst_ref=_view(out_refs[written.index(cp.dst)], cp.dst_at(place)),
            send_sem=send_sems.at[k], recv_sem=recv_sems.at[k],
            device_id=cp.peer(place), device_id_type=MESH_ID)
        for k, cp in enumerate(plan.copies)]


def _plan_operands(plan, n_in, n_out):
    ins, written = _plan_io(plan)
    operands = [plan.arrays[k] for k in ins]
    out_shape = [jax.ShapeDtypeStruct(plan.arrays[k].shape, plan.arrays[k].dtype) for k in written]
    aliases = {n_in + ins.index(k): n_out + pos for pos, k in enumerate(written) if k in ins}
    n = len(plan.copies)
    sems = [pltpu.SemaphoreType.DMA((n,)), pltpu.SemaphoreType.DMA((n,))]
    return operands, out_shape, aliases, sems, written


def _call(body, *, name, grid, in_specs, out_specs, out_shape, operands, scratch_shapes=(), aliases=None,
          parallel=0, plan=None):
    single = not isinstance(out_shape, (list, tuple))
    out_shape = [out_shape] if single else list(out_shape)
    out_specs = [out_specs] if single else list(out_specs)
    if plan is None:
        res = pl.pallas_call(
            body, name=name, grid=grid, in_specs=list(in_specs), out_specs=out_specs, out_shape=out_shape,
            scratch_shapes=list(scratch_shapes), input_output_aliases=aliases or {},
            compiler_params=_params(len(grid), parallel=parallel),
        )(*operands)
        return res[0] if single else res

    n_in, n_out, n_scr = len(operands), len(out_shape), len(scratch_shapes)
    p_operands, p_out_shape, p_aliases, sems, written = _plan_operands(plan, n_in, n_out)
    n_pin, n_pout = len(p_operands), len(p_out_shape)

    def wrapped(*refs):
        ins = refs[:n_in]
        p_in = refs[n_in:n_in + n_pin]
        outs = refs[n_in + n_pin:n_in + n_pin + n_out]
        p_out = refs[n_in + n_pin + n_out:n_in + n_pin + n_out + n_pout]
        scratch = refs[n_in + n_pin + n_out + n_pout:n_in + n_pin + n_out + n_pout + n_scr]
        send_sems, recv_sems = refs[-2:]
        ids = [pl.program_id(a) for a in range(len(grid))]
        first = ids[0] == 0
        last = ids[0] == grid[0] - 1
        for a in range(1, len(grid)):
            first = jnp.logical_and(first, ids[a] == 0)
            last = jnp.logical_and(last, ids[a] == grid[a] - 1)

        @pl.when(first)
        def _():
            for cp in _descriptors(plan, p_in, p_out, send_sems, recv_sems):
                cp.start()

        body(*ins, *outs, *scratch)

        @pl.when(last)
        def _():
            for cp in _descriptors(plan, p_in, p_out, send_sems, recv_sems):
                cp.wait()

    res = pl.pallas_call(
        wrapped, name=name, grid=grid,
        in_specs=list(in_specs) + [HBM_SPEC] * n_pin,
        out_specs=out_specs + [HBM_SPEC] * n_pout,
        out_shape=out_shape + p_out_shape,
        scratch_shapes=list(scratch_shapes) + sems,
        input_output_aliases={**(aliases or {}), **p_aliases},
        compiler_params=_params(len(grid)),
    )(*operands, *p_operands)
    outs = res[0] if single else res[:n_out]
    return outs, dict(zip(written, res[n_out:]))


def _comm_stages(name, arrays, stages, chained=False):
    plan = _Plan(tuple(arrays), tuple(cp for st in stages for cp in st))
    p_operands, p_out_shape, p_aliases, sems, written = _plan_operands(plan, 0, 0)
    n_pin = len(p_operands)

    def body(*refs):
        p_in = refs[:n_pin]
        p_out = refs[n_pin:n_pin + len(written)]
        send_sems, recv_sems = refs[-2:]
        all_copies = _descriptors(plan, p_in, p_out, send_sems, recv_sems)
        if chained:
            n = len(stages[0])
            for cp in all_copies[:n]:
                cp.start()
            for k in range(n):
                all_copies[k].wait()
                all_copies[n + k].start()
            for cp in all_copies[n:]:
                cp.wait()
            return
        base = 0
        for st in stages:
            for cp in all_copies[base:base + len(st)]:
                cp.start()
            for cp in all_copies[base:base + len(st)]:
                cp.wait()
            base += len(st)

    res = pl.pallas_call(
        body, name=name, in_specs=[HBM_SPEC] * n_pin, out_specs=[HBM_SPEC] * len(written),
        out_shape=p_out_shape, scratch_shapes=sems, input_output_aliases=p_aliases,
    )(*p_operands)
    return dict(zip(written, res))


def _half_rows(k, c):
    return pl.ds(c * (k // 2), k // 2)


def _gather_ici(a, k, only=(0, 1, 2)):
    own = lambda p: (p.chip, _half_rows(k, p.c))
    return [_Copy(a, own, a, own, lambda p, q=q: (*_other_chips(p)[q], p.c)) for q in only]


def _gather_pass_on(a, k):
    def at(q):
        def index(p):
            px, py = _other_chips(p)[q]
            return (2 * px + py, _half_rows(k, p.c))
        return index
    return [_Copy(a, at(q), a, at(q), lambda p: (p.x, p.y, 1 - p.c)) for q in range(3)]


def _reduce_swap(src, dst, k):
    return [_Copy(src, lambda p: (pl.ds(0, N_CHIPS), _half_rows(k, 1 - p.c)), dst, lambda p: None,
                  lambda p: (p.x, p.y, 1 - p.c))]


def _reduce_ici(src, dst, only=(0, 1, 2)):
    def slab(q):
        def index(p):
            px, py = _other_chips(p)[q]
            return 2 * px + py
        return index
    return [_Copy(src, slab(q), dst, lambda p, q=q: q, lambda p, q=q: (*_other_chips(p)[q], p.c)) for q in only]


def _reduce_share(a, layer, k):
    at = lambda p: (layer, _half_rows(k, p.c))
    return [_Copy(a, at, a, at, lambda p: (p.x, p.y, 1 - p.c))]


def _merge_plans(a, b):
    off = len(a.arrays)
    moved = tuple(cp._replace(src=cp.src + off, dst=cp.dst + off) for cp in b.copies)
    return _Plan(a.arrays + b.arrays, a.copies + moved)


def _broadcast_copies(src, dst):
    me = lambda p: 2 * p.chip + p.c

    def peer(k):
        return lambda p: (1 - p.x if (k >> 2) & 1 else p.x, 1 - p.y if (k >> 1) & 1 else p.y,
                          1 - p.c if k & 1 else p.c)

    return [_Copy(src, lambda p: None, dst, me, peer(k)) for k in range(1, N_DEV)]


def _place_own_slot(x, me_idx):
    r = x.shape[0]
    tr = _tile(r, (512, 256, 128, 64, 32, 16, 8))

    def body(me_ref, x_ref, o_ref):
        o_ref[...] = x_ref[...]

    return pl.pallas_call(
        body, name="place_own_slot",
        grid_spec=pltpu.PrefetchScalarGridSpec(
            num_scalar_prefetch=1, grid=(r // tr,),
            in_specs=[pl.BlockSpec((tr, LANES), lambda i, me: (i, 0))],
            out_specs=pl.BlockSpec((None, tr, LANES), lambda i, me: (me[0], i, 0))),
        out_shape=jax.ShapeDtypeStruct((N_DEV, r, LANES), F32),
        compiler_params=_params(1, parallel=1),
    )(me_idx, x)


def _all_to_all(x, name):
    def body(x_ref, y_ref, send_sems, recv_sems, own_sem):
        p = _my_place()
        me = 2 * p.chip + p.c
        own = pltpu.make_async_copy(x_ref.at[me], y_ref.at[me], own_sem)
        own.start()
        copies = []
        for k in range(1, N_DEV):
            px = 1 - p.x if (k >> 2) & 1 else p.x
            py = 1 - p.y if (k >> 1) & 1 else p.y
            pc = 1 - p.c if k & 1 else p.c
            peer = 4 * px + 2 * py + pc
            cp = pltpu.make_async_remote_copy(
                src_ref=x_ref.at[peer], dst_ref=y_ref.at[me],
                send_sem=send_sems.at[k - 1], recv_sem=recv_sems.at[k - 1],
                device_id=(px, py, pc), device_id_type=MESH_ID)
            cp.start()
            copies.append(cp)
        for cp in copies:
            cp.wait()
        own.wait()

    return pl.pallas_call(
        body, name=name,
        out_shape=jax.ShapeDtypeStruct(x.shape, x.dtype),
        in_specs=[HBM_SPEC], out_specs=HBM_SPEC,
        scratch_shapes=[pltpu.SemaphoreType.DMA((N_DEV - 1,)), pltpu.SemaphoreType.DMA((N_DEV - 1,)),
                        pltpu.SemaphoreType.DMA],
    )(x)


def _place_own_shard(w, layer, chip_idx):
    _, k, n = w.shape
    tr = _tile(k, (512, 256, 128))

    def body(c_ref, w_ref, g_ref):
        g_ref[...] = w_ref[...].astype(BF16)

    return pl.pallas_call(
        body, name="place_own_shard",
        grid_spec=pltpu.PrefetchScalarGridSpec(
            num_scalar_prefetch=1, grid=(k // tr,),
            in_specs=[pl.BlockSpec((None, tr, n), lambda r, c: (layer, r, 0))],
            out_specs=pl.BlockSpec((None, tr, n), lambda r, c: (c[0], r, 0))),
        out_shape=jax.ShapeDtypeStruct((N_CHIPS, k, n), BF16),
        compiler_params=_params(1, parallel=1),
    )(chip_idx, w)


def _matmul(name, operands, in_specs, grid, dims, out_shape, out_specs, epilogue, a_prologue=None,
            aliases=None, plan=None):
    n_in = len(operands)

    def body(*refs):
        a = refs[0][...]
        if a_prologue is not None:
            a = a_prologue(a)
        acc = lax.dot_general(a.astype(BF16), refs[1][...].astype(BF16), dims, preferred_element_type=F32)
        epilogue(acc, refs[2:n_in], refs[n_in:])

    return _call(body, name=name, grid=grid, in_specs=in_specs, out_specs=out_specs, out_shape=out_shape,
                 operands=operands, aliases=aliases, parallel=2, plan=plan)


def _store_cast(acc, extra, outs):
    outs[0][...] = acc.astype(outs[0].dtype)


def _in_proj(h, w, plan=None):
    s, d = h.shape
    nl = w.shape[-1]
    tm = _tile(s, (1024, 512, 256))
    tn = _tile(nl, (1024, 768, 512, 384, 256, 128))
    per = nl // tn
    return _matmul(
        "in_proj", (h, w),
        [pl.BlockSpec((tm, d), lambda i, j: (i, 0)),
         pl.BlockSpec((None, d, tn), lambda i, j: (j // per, 0, j % per))],
        (s // tm, N_CHIPS * per), NN,
        jax.ShapeDtypeStruct((s, N_CHIPS * nl), BF16),
        pl.BlockSpec((tm, tn), lambda i, j: (i, j)), _store_cast, plan=plan)


def _modulated_norm(xv, g, scale, shift):
    rstd = lax.rsqrt(jnp.mean(xv * xv, axis=-1, keepdims=True) + EPS)
    return ((xv * rstd) * g * (1.0 + scale) + shift).astype(BF16)


def _out_proj_residual(y, w2, x, gate, next_norm, plan=None):
    s, wdt = y.shape
    d = w2.shape[-1]
    tm = _tile(s, (256, 128))

    def epilogue(acc, extra, outs):
        x_new = extra[0][...] + extra[1][...] * acc
        outs[0][...] = x_new
        outs[1][...] = acc.astype(BF16)
        if next_norm is not None:
            outs[2][...] = _modulated_norm(x_new, extra[2][...], extra[3][...], extra[4][...])

    rows = pl.BlockSpec((tm, d), lambda i, j: (i, 0))
    vec = pl.BlockSpec((1, d), lambda i, j: (0, 0))
    n_vec = 1 + (3 if next_norm is not None else 0)
    n_act = 1 + (1 if next_norm is not None else 0)
    return _matmul(
        "out_proj", (y, w2, x, gate) + (tuple(next_norm) if next_norm is not None else ()),
        [pl.BlockSpec((tm, wdt), lambda i, j: (i, 0)),
         pl.BlockSpec((wdt, d), lambda i, j: (0, 0), pipeline_mode=pl.Buffered(1)),
         rows] + [vec] * n_vec,
        (s // tm, 1), NN,
        [jax.ShapeDtypeStruct((s, d), F32)] + [jax.ShapeDtypeStruct((s, d), BF16)] * n_act,
        [rows] * (1 + n_act), epilogue, plan=plan)


def _out_proj_bwd_act(dout, w2, plan=None):
    s, d = dout.shape
    wdt = w2.shape[0]
    tm = _tile(s, (1024, 512, 256))
    tn = _tile(wdt, (1024, 512, 256, 128))
    return _matmul(
        "out_proj_dy", (dout, w2),
        [pl.BlockSpec((tm, d), lambda i, j: (i, 0)),
         pl.BlockSpec((tn, d), lambda i, j: (j, 0))],
        (s // tm, wdt // tn), NT,
        jax.ShapeDtypeStruct((s, wdt), BF16),
        pl.BlockSpec((tm, tn), lambda i, j: (i, j)), _store_cast, plan=plan)


def _out_proj_bwd_w(y, dout, plan=None):
    s, wdt = y.shape
    d = dout.shape[1]
    tm = _tile(wdt, (1024, 512, 256, 128))
    tn = _tile(d, (1024, 512, 256, 128))
    return _matmul(
        "out_proj_dw", (y, dout),
        [pl.BlockSpec((s, tm), lambda i, j: (0, i)),
         pl.BlockSpec((s, tn), lambda i, j: (0, j))],
        (wdt // tm, d // tn), TN,
        jax.ShapeDtypeStruct((wdt, d), BF16),
        pl.BlockSpec((tm, tn), lambda i, j: (i, j)), _store_cast, plan=plan)


def _in_proj_bwd_act(dproj, w, plan=None):
    s, n_all = dproj.shape
    d, nl = w.shape[1], w.shape[2]
    tm = _tile(s, (1024, 512, 256))
    tn = _tile(d, (512, 256, 128))

    def body(a_ref, w_ref, o_ref):
        acc = None
        for q in range(N_CHIPS):
            part = lax.dot_general(a_ref[:, q * nl:(q + 1) * nl], w_ref[q], NT, preferred_element_type=F32)
            acc = part if acc is None else acc + part
        o_ref[...] = acc.astype(BF16)

    return _call(
        body, name="in_proj_dh", grid=(s // tm, d // tn),
        in_specs=[pl.BlockSpec((tm, n_all), lambda i, j: (i, 0), pipeline_mode=pl.Buffered(1)),
                  pl.BlockSpec((N_CHIPS, tn, nl), lambda i, j: (0, j, 0))],
        out_specs=pl.BlockSpec((tm, tn), lambda i, j: (i, j)),
        out_shape=jax.ShapeDtypeStruct((s, d), BF16),
        operands=(dproj, w), parallel=2, plan=plan)


def _in_proj_bwd_w(h, dproj, nl, plan=None):
    s, d = h.shape
    tm = _tile(d, (1024, 512, 256, 128))
    tn = _tile(nl, (1024, 768, 512, 384, 256, 128))
    per = nl // tn
    return _matmul(
        "in_proj_dw", (h, dproj),
        [pl.BlockSpec((s, tm), lambda i, j: (0, i)),
         pl.BlockSpec((s, tn), lambda i, j: (0, j))],
        (d // tm, N_CHIPS * per), TN,
        jax.ShapeDtypeStruct((N_CHIPS, d, nl), BF16),
        pl.BlockSpec((None, tm, tn), lambda i, j: (j // per, i, j % per)), _store_cast, plan=plan)


def _mod_fwd(c_all, w_mod, bias, layer):
    nb, d = c_all.shape
    nl = w_mod.shape[-1]
    tn = _tile(nl, (768, 512, 384, 256, 128))

    def epilogue(acc, extra, outs):
        outs[0][...] = acc + extra[0][...]

    return _matmul(
        "mod_fwd", (c_all, w_mod, bias),
        [pl.BlockSpec((nb, d), lambda i, j: (0, 0)),
         pl.BlockSpec((None, d, tn), lambda i, j: (layer, 0, j)),
         pl.BlockSpec((1, tn), lambda i, j: (0, j))],
        (1, nl // tn), NN,
        jax.ShapeDtypeStruct((nb, nl), F32),
        pl.BlockSpec((nb, tn), lambda i, j: (0, j)), epilogue, a_prologue=_silu)


def _mod_bwd_w(c_all, dm_pair):
    nb, d = c_all.shape
    nl = dm_pair.shape[-1]
    tm = _tile(d, (1024, 512, 256, 128))
    tn = _tile(nl, (768, 512, 384, 256, 128))

    def body(c_ref, dm_ref, o_ref):
        o_ref[...] = lax.dot_general(_silu(c_ref[...]).astype(BF16), dm_ref[...].astype(BF16), TN,
                                     preferred_element_type=F32)

    return _call(
        body, name="mod_dw", grid=(2, d // tm, nl // tn),
        in_specs=[pl.BlockSpec((nb, tm), lambda l, i, j: (0, i)),
                  pl.BlockSpec((None, nb, tn), lambda l, i, j: (l, 0, j))],
        out_specs=pl.BlockSpec((None, tm, tn), lambda l, i, j: (l, i, j)),
        out_shape=jax.ShapeDtypeStruct((2, d, nl), F32),
        operands=(c_all, dm_pair), parallel=3)


def _rows_call(name, body, operands, in_specs, out_shape, out_specs, n_tiles):
    return pl.pallas_call(
        body, name=name, grid=(n_tiles,), in_specs=in_specs, out_specs=out_specs, out_shape=out_shape,
        compiler_params=_params(1),
    )(*operands)


def _row_spec(tr, width):
    return pl.BlockSpec((tr, width), lambda i: (i, 0))


def _vec_spec(width):
    return pl.BlockSpec((1, width), lambda i: (0, 0))


def _accumulate(ref, val):
    first = pl.program_id(0) == 0

    @pl.when(first)
    def _():
        ref[...] = val

    @pl.when(jnp.logical_not(first))
    def _():
        ref[...] += val


def _prenorm(x, g, scale, shift):
    s, d = x.shape
    tr = _tile(s, (256, 128))

    def body(x_ref, g_ref, sc_ref, sh_ref, h_ref):
        h_ref[...] = _modulated_norm(x_ref[...], g_ref[...], sc_ref[...], sh_ref[...])

    return _rows_call("prenorm", body, (x, g, scale, shift),
                      [_row_spec(tr, d), _vec_spec(d), _vec_spec(d), _vec_spec(d)],
                      jax.ShapeDtypeStruct((s, d), BF16), _row_spec(tr, d), s // tr)


def _gate_grads(dxv, out_ref, gate_ref, dout_ref, dgate_ref):
    dout_ref[...] = (gate_ref[...] * dxv).astype(BF16)
    _accumulate(dgate_ref, jnp.sum(dxv * out_ref[...].astype(F32), axis=0, keepdims=True))


def _prenorm_bwd(x, dh, dres, g, scale, below=None):
    s, d = x.shape
    tr = _tile(s, (256, 128))

    def body(x_ref, dh_ref, dres_ref, g_ref, sc_ref, *rest):
        dx_ref, dshift_ref, dscale_ref, dg_ref = rest[-6:-2] if below else rest
        xv = x_ref[...]
        dhv = dh_ref[...].astype(F32)
        rstd = lax.rsqrt(jnp.mean(xv * xv, axis=-1, keepdims=True) + EPS)
        xhat = xv * rstd
        gv = g_ref[...]
        one_sc = 1.0 + sc_ref[...]
        dxhat = dhv * gv * one_sc
        dxv = dres_ref[...] + rstd * (dxhat - xhat * jnp.mean(dxhat * xhat, axis=-1, keepdims=True))
        dx_ref[...] = dxv
        _accumulate(dshift_ref, jnp.sum(dhv, axis=0, keepdims=True))
        _accumulate(dscale_ref, jnp.sum(dhv * xhat * gv, axis=0, keepdims=True))
        _accumulate(dg_ref, jnp.sum(dhv * xhat * one_sc, axis=0, keepdims=True))
        if below:
            _gate_grads(dxv, rest[0], rest[1], rest[-2], rest[-1])

    vec = jax.ShapeDtypeStruct((1, d), F32)
    operands = (x, dh, dres, g, scale) + (tuple(below) if below else ())
    in_specs = [_row_spec(tr, d), _row_spec(tr, d), _row_spec(tr, d), _vec_spec(d), _vec_spec(d)]
    out_shape = [jax.ShapeDtypeStruct((s, d), F32), vec, vec, vec]
    out_specs = [_row_spec(tr, d), _vec_spec(d), _vec_spec(d), _vec_spec(d)]
    if below:
        in_specs += [_row_spec(tr, d), _vec_spec(d)]
        out_shape += [jax.ShapeDtypeStruct((s, d), BF16), vec]
        out_specs += [_row_spec(tr, d), _vec_spec(d)]
    return _rows_call("prenorm_bwd", body, operands, in_specs, out_shape, out_specs, s // tr)


def _final_loss(x, target, g, out_below, gate_below):
    s, d = x.shape
    tr = _tile(s, (256, 128))
    n_tiles = s // tr

    def body(x_ref, t_ref, g_ref, out_ref, gate_ref, loss_ref, dx_ref, dg_ref, dout_ref, dgate_ref, acc_ref):
        xv = x_ref[...]
        rstd = lax.rsqrt(jnp.mean(xv * xv, axis=-1, keepdims=True) + EPS)
        xhat = xv * rstd
        gv = g_ref[...]
        err = xhat * gv - t_ref[...]
        dy = err * (1.0 / d)
        dxhat = dy * gv
        dxv = rstd * (dxhat - xhat * jnp.mean(dxhat * xhat, axis=-1, keepdims=True))
        dx_ref[...] = dxv
        _accumulate(dg_ref, jnp.sum(dy * xhat, axis=0, keepdims=True))
        _accumulate(acc_ref, jnp.sum(err * err, axis=0, keepdims=True))
        _gate_grads(dxv, out_ref, gate_ref, dout_ref, dgate_ref)

        @pl.when(pl.program_id(0) == n_tiles - 1)
        def _():
            loss_ref[...] = (0.5 / d) * jnp.sum(acc_ref[...], axis=1, keepdims=True)

    vec = jax.ShapeDtypeStruct((1, d), F32)
    return pl.pallas_call(
        body, name="final_loss", grid=(n_tiles,),
        in_specs=[_row_spec(tr, d), _row_spec(tr, d), _vec_spec(d), _row_spec(tr, d), _vec_spec(d)],
        out_specs=[pl.BlockSpec((1, 1), lambda i: (0, 0)), _row_spec(tr, d), _vec_spec(d), _row_spec(tr, d),
                   _vec_spec(d)],
        out_shape=[jax.ShapeDtypeStruct((1, 1), F32), jax.ShapeDtypeStruct((s, d), F32), vec,
                   jax.ShapeDtypeStruct((s, d), BF16), vec],
        scratch_shapes=[pltpu.VMEM((1, d), F32)],
        compiler_params=_params(1),
    )(x, target, g, out_below, gate_below)


def _rope(t, cos, sin):
    return t * cos + pltpu.roll(t, HEAD_DIM // 2, axis=1) * sin


def _unrope(dt, cos, sin):
    return dt * cos + pltpu.roll(dt * sin, HEAD_DIM // 2, axis=1)


def _band_blocks(s, dil):
    sub = s // dil
    kw = min(K_WINDOW, sub)

    def rows(r, start, n):
        if dil == 1:
            return pl.ds(pl.multiple_of(start, RADIUS), n)
        return pl.ds(r + dil * start, n, stride=dil)

    def window(idx):
        nb = sub // Q_BLOCK
        r, b = idx // nb, idx % nb
        q0 = b * Q_BLOCK
        start = jnp.clip(q0 - RADIUS, 0, sub - kw)
        ahead = (lax.broadcasted_iota(jnp.int32, (Q_BLOCK, kw), 1)
                 - lax.broadcasted_iota(jnp.int32, (Q_BLOCK, kw), 0)) + (start - q0 + RADIUS)
        valid = lax.bitcast_convert_type(ahead, jnp.uint32) <= 2 * RADIUS
        return rows(r, q0, Q_BLOCK), rows(r, start, kw), valid

    return window


def _store_column_tiles(tiles, dst_ref, sems, col_blocks):
    rows = tiles.shape[1]
    copies = []
    for g, cb in enumerate(col_blocks):
        cols = pl.ds(pl.multiple_of(cb * LANES, LANES), LANES)
        cp = pltpu.make_async_copy(tiles.at[g], dst_ref.at[pl.ds(0, rows), cols], sems.at[g])
        cp.start()
        copies.append(cp)
    for cp in copies:
        cp.wait()


def _head_col(s, group, nh):
    return pl.BlockSpec((s, HEAD_DIM), lambda h: (0, group * nh + h), pipeline_mode=pl.Buffered(1))


def _attn_fwd(proj, cos, sin, aw, plan=None):
    s = proj.shape[0]
    nh = aw // HEAD_DIM
    scale = HEAD_DIM ** -0.5
    n_blocks = s // Q_BLOCK

    def body(q_ref, k_ref, v_ref, cos_ref, sin_ref, attn_ref, lse_ref, qf, kf, vf, acc):
        cosv, sinv = cos_ref[...], sin_ref[...]
        qf[...] = _rope(q_ref[...].astype(F32), cosv, sinv) * scale
        kf[...] = _rope(k_ref[...].astype(F32), cosv, sinv)
        vf[...] = v_ref[...].astype(F32)

        for pattern, dil in enumerate(DILATIONS):
            window = _band_blocks(s, dil)

            def block(idx, carry, window=window, first=(pattern == 0)):
                q_rows, k_rows, valid = window(idx)
                q = qf[q_rows, :].astype(BF16)
                kk = kf[k_rows, :].astype(BF16)
                vv = vf[k_rows, :].astype(BF16)
                sc = lax.dot_general(q, kk, NT, preferred_element_type=F32)
                sc = jnp.where(valid, sc, NEG_INF)
                m = jnp.max(sc, axis=1, keepdims=True)
                p = jnp.exp(sc - m)
                den = jnp.sum(p, axis=1, keepdims=True)
                o = lax.dot_general(p.astype(BF16), vv, NN, preferred_element_type=F32) / den
                lse = jnp.broadcast_to(m + jnp.log(den), (Q_BLOCK, HEAD_DIM))
                if first:
                    acc[q_rows, :] = o
                    lse_ref[q_rows, :] = lse
                else:
                    lse_old = lse_ref[q_rows, :]
                    top = jnp.maximum(lse_old, lse)
                    w_old, w_new = jnp.exp(lse_old - top), jnp.exp(lse - top)
                    tot = w_old + w_new
                    acc[q_rows, :] = (acc[q_rows, :] * w_old + o * w_new) / tot
                    lse_ref[q_rows, :] = top + jnp.log(tot)
                return carry

            lax.fori_loop(0, n_blocks, block, 0, unroll=ATTN_UNROLL)

        attn_ref[...] = acc[...].astype(BF16)

    table = pl.BlockSpec((s, HEAD_DIM), lambda h: (0, 0), pipeline_mode=pl.Buffered(1))
    out = pl.BlockSpec((s, HEAD_DIM), lambda h: (0, h))
    return _call(
        body, name="attn_fwd", grid=(nh,),
        in_specs=[_head_col(s, 0, nh), _head_col(s, 1, nh), _head_col(s, 2, nh), table, table],
        out_specs=[out, out],
        out_shape=[jax.ShapeDtypeStruct((s, aw), BF16), jax.ShapeDtypeStruct((s, aw), F32)],
        scratch_shapes=[pltpu.VMEM((s, HEAD_DIM), F32)] * 4,
        operands=(proj, proj, proj, cos, sin), parallel=1, plan=plan)


def _attn_bwd(proj, cos, sin, dy, attn, lse, aw, plan=None):
    s = proj.shape[0]
    nh = aw // HEAD_DIM
    scale = HEAD_DIM ** -0.5
    n_blocks = s // Q_BLOCK

    def body(q_ref, k_ref, v_ref, za_ref, cos_ref, sin_ref, dy_ref, attn_ref, lse_ref,
             dproj_ref, qf, kf, vf, dof, delta, dqa, dka, dva, tiles, tile_sems):
        cosv, sinv = cos_ref[...], sin_ref[...]
        qf[...] = _rope(q_ref[...].astype(F32), cosv, sinv) * scale
        kf[...] = _rope(k_ref[...].astype(F32), cosv, sinv)
        vf[...] = v_ref[...].astype(F32)
        dyv, zav, attnv = dy_ref[...].astype(F32), za_ref[...].astype(F32), attn_ref[...].astype(F32)
        silu_za, dsilu_za = _silu_and_grad(zav)
        do_all = dyv * silu_za
        dof[...] = do_all
        tiles[3] = (dyv * attnv * dsilu_za).astype(BF16)
        delta[...] = jnp.broadcast_to(jnp.sum(do_all * attnv, axis=1, keepdims=True), (s, HEAD_DIM))
        dqa[...] = jnp.zeros_like(dqa)
        dka[...] = jnp.zeros_like(dka)
        dva[...] = jnp.zeros_like(dva)

        for dil in DILATIONS:
            window = _band_blocks(s, dil)

            def block(idx, carry, window=window):
                q_rows, k_rows, valid = window(idx)
                q = qf[q_rows, :].astype(BF16)
                kk = kf[k_rows, :].astype(BF16)
                vv = vf[k_rows, :].astype(BF16)
                dov = dof[q_rows, :].astype(BF16)
                lse_q = lse_ref[q_rows, :][:, 0:1]
                delta_q = delta[q_rows, :][:, 0:1]
                sc = lax.dot_general(q, kk, NT, preferred_element_type=F32)
                p = jnp.where(valid, jnp.exp(sc - lse_q), 0.0)
                dp = lax.dot_general(dov, vv, NT, preferred_element_type=F32)
                ds = (p * (dp - delta_q)).astype(BF16)
                dqa[q_rows, :] += lax.dot_general(ds, kk, NN, preferred_element_type=F32)
                dka[k_rows, :] += lax.dot_general(ds, q, TN, preferred_element_type=F32)
                dva[k_rows, :] += lax.dot_general(p.astype(BF16), dov, TN, preferred_element_type=F32)
                return carry

            lax.fori_loop(0, n_blocks, block, 0, unroll=ATTN_UNROLL)

        tiles[0] = (_unrope(dqa[...], cosv, sinv) * scale).astype(BF16)
        tiles[1] = _unrope(dka[...], cosv, sinv).astype(BF16)
        tiles[2] = dva[...].astype(BF16)
        _store_column_tiles(tiles, dproj_ref, tile_sems, [g * nh + pl.program_id(0) for g in range(4)])

    own = pl.BlockSpec((s, HEAD_DIM), lambda h: (0, h), pipeline_mode=pl.Buffered(1))
    table = pl.BlockSpec((s, HEAD_DIM), lambda h: (0, 0), pipeline_mode=pl.Buffered(1))
    return _call(
        body, name="attn_bwd", grid=(nh,),
        in_specs=[_head_col(s, 0, nh), _head_col(s, 1, nh), _head_col(s, 2, nh), _head_col(s, 3, nh),
                  table, table, own, own, own],
        out_specs=HBM_SPEC,
        out_shape=jax.ShapeDtypeStruct((s, 8 * aw), BF16),
        scratch_shapes=[pltpu.VMEM((s, HEAD_DIM), F32)] * 8 + [
            pltpu.VMEM((4, s, HEAD_DIM), BF16), pltpu.SemaphoreType.DMA((4,))],
        operands=(proj, proj, proj, proj, cos, sin, dy, attn, lse), plan=plan)


def _rope_tables(s):
    half = HEAD_DIM // 2
    inv = np.float32(ROPE_THETA) ** (-np.arange(half, dtype=np.float32) / np.float32(half))
    ang = np.arange(s, dtype=np.float32)[:, None] * inv[None, :]
    cos, sin = np.cos(ang), np.sin(ang)
    return (jnp.asarray(np.concatenate([cos, cos], axis=-1), F32),
            jnp.asarray(np.concatenate([-sin, sin], axis=-1), F32))


def _conv_chunks(s):
    for k in range(s // CONV_ROWS):
        lo = max(0, k * CONV_ROWS - CONV_HALO)
        hi = min(s, (k + 1) * CONV_ROWS + CONV_HALO)
        yield k * CONV_ROWS, lo, hi


def _neighbours(p, lo, s):
    n = p.shape[0]
    row = lo + lax.broadcasted_iota(jnp.int32, p.shape, 0)
    prev = jnp.where(row == 0, 0.0, pltpu.roll(p, 1, axis=0))
    nxt = jnp.where(row == s - 1, 0.0, pltpu.roll(p, n - 1, axis=0))
    return prev, nxt


def _ab_mix(attn, proj, conv_w, aw):
    s = proj.shape[0]
    nt = aw // LANES

    def col(group, sel):
        return pl.BlockSpec((s, LANES), lambda i: (0, group * nt + sel(i)))

    a_sel = lambda i: jnp.minimum(i, nt - 1)
    b_sel = lambda i: jnp.maximum(i - nt, 0)

    def body(attn_ref, za_ref, ub_ref, gb_ref, gc_ref, zb_ref, w_ref, y_ref):
        i = pl.program_id(0)

        @pl.when(i < nt)
        def _():
            y_ref[...] = (attn_ref[...].astype(F32) * _silu(za_ref[...].astype(F32))).astype(BF16)

        @pl.when(i >= nt)
        def _():
            w = w_ref[...]
            for c0, lo, hi in _conv_chunks(s):
                p = gc_ref[lo:hi, :].astype(F32) * ub_ref[lo:hi, :].astype(F32)
                prev, nxt = _neighbours(p, lo, s)
                cv = w[0:1, :] * prev + w[1:2, :] * p + w[2:3, :] * nxt
                yb = gb_ref[lo:hi, :].astype(F32) * cv * _silu(zb_ref[lo:hi, :].astype(F32))
                y_ref[c0:c0 + CONV_ROWS, :] = yb[c0 - lo:c0 - lo + CONV_ROWS, :].astype(BF16)

    return pl.pallas_call(
        body, name="ab_mix", grid=(2 * nt,),
        in_specs=[pl.BlockSpec((s, LANES), lambda i: (0, a_sel(i))),
                  col(3, a_sel), col(4, b_sel), col(5, b_sel), col(6, b_sel), col(7, b_sel),
                  pl.BlockSpec((3, LANES), lambda i: (0, b_sel(i)))],
        out_specs=pl.BlockSpec((s, LANES), lambda i: (0, i)),
        out_shape=jax.ShapeDtypeStruct((s, 2 * aw), BF16),
        compiler_params=_params(1),
    )(attn, proj, proj, proj, proj, proj, conv_w)


def _conv_bwd(dproj, dy, proj, conv_w, aw):
    s = proj.shape[0]
    nt = aw // LANES

    def col(group):
        return pl.BlockSpec((s, LANES), lambda i: (0, group * nt + i))

    def body(dyb_ref, ub_ref, gb_ref, gc_ref, zb_ref, w_ref, dproj_in, dproj_ref, dw_ref, tiles, tile_sems):
        w = w_ref[...]
        dw = [jnp.zeros((1, LANES), F32) for _ in range(3)]
        for c0, lo, hi in _conv_chunks(s):
            ctr = slice(c0 - lo, c0 - lo + CONV_ROWS)
            out_rows = slice(c0, c0 + CONV_ROWS)
            ub = ub_ref[lo:hi, :].astype(F32)
            gc = gc_ref[lo:hi, :].astype(F32)
            gb = gb_ref[lo:hi, :].astype(F32)
            zb = zb_ref[lo:hi, :].astype(F32)
            dyb = dyb_ref[lo:hi, :].astype(F32)
            p = gc * ub
            prev, nxt = _neighbours(p, lo, s)
            cv = w[0:1, :] * prev + w[1:2, :] * p + w[2:3, :] * nxt
            sz, dsz = _silu_and_grad(zb)
            dcv = dyb * gb * sz
            dprev, dnxt = _neighbours(dcv, lo, s)
            dp = w[0:1, :] * dnxt + w[1:2, :] * dcv + w[2:3, :] * dprev
            for t, nb in enumerate((prev, p, nxt)):
                dw[t] = dw[t] + jnp.sum((dcv * nb)[ctr, :], axis=0, keepdims=True)
            tiles[0, out_rows, :] = (dp * gc)[ctr, :].astype(BF16)
            tiles[1, out_rows, :] = (dyb * cv * sz)[ctr, :].astype(BF16)
            tiles[2, out_rows, :] = (dp * ub)[ctr, :].astype(BF16)
            tiles[3, out_rows, :] = (dyb * gb * cv * dsz)[ctr, :].astype(BF16)
        dw_ref[...] = jnp.concatenate(dw, axis=0)
        _store_column_tiles(tiles, dproj_ref, tile_sems, [(4 + g) * nt + pl.program_id(0) for g in range(4)])

    return pl.pallas_call(
        body, name="conv_bwd", grid=(nt,),
        in_specs=[pl.BlockSpec((s, LANES), lambda i: (0, nt + i)),
                  col(4), col(5), col(6), col(7),
                  pl.BlockSpec((3, LANES), lambda i: (0, i)), HBM_SPEC],
        out_specs=[HBM_SPEC, pl.BlockSpec((3, LANES), lambda i: (0, i))],
        out_shape=[jax.ShapeDtypeStruct(dproj.shape, dproj.dtype), jax.ShapeDtypeStruct((3, aw), F32)],
        scratch_shapes=[pltpu.VMEM((4, s, LANES), BF16), pltpu.SemaphoreType.DMA((4,))],
        input_output_aliases={6: 0},
        compiler_params=_params(1),
    )(dy, proj, proj, proj, proj, conv_w, dproj)


def _sgu_norm(gv, ln_g, ln_b):
    mu = jnp.mean(gv, axis=-1, keepdims=True)
    xc = gv - mu
    rstd = lax.rsqrt(jnp.mean(xc * xc, axis=-1, keepdims=True) + EPS)
    vhat = xc * rstd
    return vhat, rstd, vhat * ln_g + ln_b


def _sgu_fwd(uvz, ln_g, ln_b, w_s, b_s, cw):
    s = uvz.shape[0]
    tr = 2 * CHUNK if s % (2 * CHUNK) == 0 else CHUNK
    gw = cw // N_GROUPS

    def body(u_ref, v_ref, z_ref, g_ref, b_ref, ws_ref, bs_ref, y_ref):
        _, _, vn = _sgu_norm(_gelu(v_ref[...].astype(F32)), g_ref[...], b_ref[...])
        vn = vn.astype(BF16)
        for ch in range(tr // CHUNK):
            rows = slice(ch * CHUNK, (ch + 1) * CHUNK)
            for grp in range(N_GROUPS):
                cols = slice(grp * gw, (grp + 1) * gw)
                mixed = lax.dot_general(ws_ref[grp], vn[rows, cols], NN, preferred_element_type=F32) + bs_ref[grp]
                y_ref[rows, cols] = (_gelu(u_ref[rows, cols].astype(F32)) * mixed
                                     * _silu(z_ref[rows, cols].astype(F32))).astype(BF16)

    full3 = lambda shape: pl.BlockSpec(shape, lambda i: (0, 0, 0))
    return pl.pallas_call(
        body, name="sgu_fwd", grid=(s // tr,),
        in_specs=[pl.BlockSpec((tr, cw), lambda i: (i, 0)), pl.BlockSpec((tr, cw), lambda i: (i, 1)),
                  pl.BlockSpec((tr, cw), lambda i: (i, 2)), _vec_spec(cw), _vec_spec(cw),
                  full3(w_s.shape), full3(b_s.shape)],
        out_specs=pl.BlockSpec((tr, cw), lambda i: (i, 0)),
        out_shape=jax.ShapeDtypeStruct((s, cw), BF16),
        compiler_params=_params(1, parallel=1),
    )(uvz, uvz, uvz, ln_g, ln_b, w_s, b_s)


def _sgu_bwd(uvz, dy, ln_g, ln_b, w_s, b_s, cw, plan=None):
    s = uvz.shape[0]
    tr = 2 * CHUNK if s % (2 * CHUNK) == 0 else CHUNK
    gw = cw // N_GROUPS

    def body(u_ref, v_ref, z_ref, dy_ref, g_ref, b_ref, ws_ref, bs_ref,
             duvz_ref, dws_ref, dbs_ref, dg_ref, db_ref, dvn_ref):
        vv = v_ref[...].astype(F32)
        gvec = g_ref[...]
        gelu_v, dgelu_v = _gelu_and_grad(vv)
        vhat, rstd, vn = _sgu_norm(gelu_v, gvec, b_ref[...])
        vn = vn.astype(BF16)
        first = pl.program_id(0) == 0

        @pl.when(first)
        def _():
            dws_ref[...] = jnp.zeros_like(dws_ref)
            dbs_ref[...] = jnp.zeros_like(dbs_ref)

        for ch in range(tr // CHUNK):
            rows = slice(ch * CHUNK, (ch + 1) * CHUNK)
            for grp in range(N_GROUPS):
                cols = slice(grp * gw, (grp + 1) * gw)
                vn_g = vn[rows, cols]
                mixed = lax.dot_general(ws_ref[grp], vn_g, NN, preferred_element_type=F32) + bs_ref[grp]
                uu = u_ref[rows, cols].astype(F32)
                zz = z_ref[rows, cols].astype(F32)
                dyv = dy_ref[rows, cols].astype(F32)
                (gu, dgu), (sz, dsz) = _gelu_and_grad(uu), _silu_and_grad(zz)
                duvz_ref[rows, grp * gw:(grp + 1) * gw] = (dyv * mixed * sz * dgu).astype(BF16)
                duvz_ref[rows, 2 * cw + grp * gw:2 * cw + (grp + 1) * gw] = (dyv * gu * mixed * dsz).astype(BF16)
                dmixed = dyv * gu * sz
                dm16 = dmixed.astype(BF16)
                dws_ref[grp] += lax.dot_general(dm16, vn_g, NT, preferred_element_type=F32)
                dbs_ref[grp] += jnp.broadcast_to(jnp.sum(dmixed, axis=1, keepdims=True), (CHUNK, LANES))
                dvn_ref[rows, cols] = lax.dot_general(ws_ref[grp], dm16, TN, preferred_element_type=F32)

        dvn = dvn_ref[...]
        _accumulate(dg_ref, jnp.sum(dvn * vhat, axis=0, keepdims=True))
        _accumulate(db_ref, jnp.sum(dvn, axis=0, keepdims=True))
        dvhat = dvn * gvec
        dgv = rstd * (dvhat - jnp.mean(dvhat, axis=-1, keepdims=True)
                      - vhat * jnp.mean(dvhat * vhat, axis=-1, keepdims=True))
        duvz_ref[:, cw:2 * cw] = (dgv * dgelu_v).astype(BF16)

    full3 = lambda shape: pl.BlockSpec(shape, lambda i: (0, 0, 0))
    acc3 = jax.ShapeDtypeStruct((N_GROUPS, CHUNK, LANES), F32)
    vec = jax.ShapeDtypeStruct((1, cw), F32)
    row = pl.BlockSpec((tr, cw), lambda i: (i, 0))
    return _call(
        body, name="sgu_bwd", grid=(s // tr,),
        in_specs=[row, pl.BlockSpec((tr, cw), lambda i: (i, 1)), pl.BlockSpec((tr, cw), lambda i: (i, 2)),
                  row, _vec_spec(cw), _vec_spec(cw), full3(w_s.shape), full3(b_s.shape)],
        out_specs=[pl.BlockSpec((tr, 3 * cw), lambda i: (i, 0)), full3((N_GROUPS, CHUNK, LANES)),
                   full3((N_GROUPS, CHUNK, LANES)), _vec_spec(cw), _vec_spec(cw)],
        out_shape=[jax.ShapeDtypeStruct((s, 3 * cw), BF16), acc3, acc3, vec, vec],
        scratch_shapes=[pltpu.VMEM((tr, cw), F32)],
        operands=(uvz, uvz, uvz, dy, ln_g, ln_b, w_s, b_s), plan=plan)


def _flat_rows(a):
    return a.reshape(-1, a.shape[-1])


def _add_sibling(grad, recv, core_idx):
    nchip, k, n = grad.shape
    tr = _tile(k // 2, (256, 128))
    nb = (k // 2) // tr

    def body(c_ref, g_ref, r_ref, o_ref):
        o_ref[...] = (g_ref[...].astype(F32) + r_ref[...].astype(F32)).astype(BF16)

    return pl.pallas_call(
        body, name="add_sibling",
        grid_spec=pltpu.PrefetchScalarGridSpec(
            num_scalar_prefetch=1, grid=(nchip, nb),
            in_specs=[pl.BlockSpec((None, tr, n), lambda q, i, c: (q, c[0] * nb + i, 0)),
                      pl.BlockSpec((None, tr, n), lambda q, i, c: (q, i, 0))],
            out_specs=pl.BlockSpec((None, tr, n), lambda q, i, c: (q, i, 0))),
        out_shape=jax.ShapeDtypeStruct((nchip, k // 2, n), BF16),
        compiler_params=_params(2, parallel=2),
    )(core_idx, grad, recv)


def _sum_chips(own, others, reduced, layer, place_idx):
    _, kh, n = own.shape
    tr = _tile(kh, (256, 128))
    nb = kh // tr

    def body(place_ref, own_ref, oth_ref, red_ref, o_ref):
        acc = own_ref[...].astype(F32)
        for q in range(3):
            acc = acc + oth_ref[q].astype(F32)
        o_ref[...] = acc

    return pl.pallas_call(
        body, name="sum_chips",
        grid_spec=pltpu.PrefetchScalarGridSpec(
            num_scalar_prefetch=1, grid=(nb,),
            in_specs=[pl.BlockSpec((None, tr, n), lambda i, p: (p[0], i, 0)),
                      pl.BlockSpec((3, tr, n), lambda i, p: (0, i, 0)),
                      HBM_SPEC],
            out_specs=pl.BlockSpec((None, tr, n), lambda i, p: (layer, p[1] * nb + i, 0))),
        out_shape=jax.ShapeDtypeStruct(reduced.shape, reduced.dtype),
        input_output_aliases={3: 0},
        compiler_params=_params(1, parallel=1),
    )(place_idx, own, others, reduced)


def _sum_devices(parts, plan=None):
    nd, r, _ = parts.shape
    tr = _tile(r, (512, 256, 128, 64, 32, 16, 8))

    def body(p_ref, o_ref):
        acc = p_ref[0]
        for q in range(1, nd):
            acc = acc + p_ref[q]
        o_ref[...] = acc

    return _call(
        body, name="sum_devices", grid=(r // tr,),
        in_specs=[pl.BlockSpec((nd, tr, LANES), lambda i: (0, i, 0))],
        out_specs=pl.BlockSpec((tr, LANES), lambda i: (i, 0)),
        out_shape=jax.ShapeDtypeStruct((r, LANES), F32),
        operands=(parts,), parallel=1, plan=plan)


def _adamw(w, g, m, v, also_grad=False):
    r, n = w.shape
    tr = _tile(r, [p for p in (1024, 512, 256, 128, 64, 32, 16, 8) if p * n <= ELEMENTWISE_BLOCK])
    n_out = 4 if also_grad else 3

    def body(w_ref, g_ref, m_ref, v_ref, d_ref, nm_ref, nv_ref, *g_out):
        gv = g_ref[...]
        if also_grad:
            g_out[0][...] = gv
        nm = ADAM_B1 * m_ref[...] + (1.0 - ADAM_B1) * gv
        nv = ADAM_B2 * v_ref[...] + (1.0 - ADAM_B2) * (gv * gv)
        m_hat = nm / (1.0 - ADAM_B1 ** ADAM_STEP)
        v_hat = nv / (1.0 - ADAM_B2 ** ADAM_STEP)
        d_ref[...] = -ADAM_LR * (m_hat / (jnp.sqrt(v_hat) + ADAM_EPS) + ADAM_WD * w_ref[...])
        nm_ref[...] = nm
        nv_ref[...] = nv

    spec = pl.BlockSpec((tr, n), lambda i: (i, 0))
    shp = jax.ShapeDtypeStruct((r, n), F32)
    return _call(
        body, name="adamw", grid=(r // tr,),
        in_specs=[spec] * 4, out_specs=[spec] * n_out, out_shape=[shp] * n_out,
        operands=(w, g, m, v), parallel=1)


def _pack(arrays, row_multiple=8):
    flat = [a.reshape(-1) for a in arrays]
    sizes = [f.shape[0] for f in flat]
    total = sum(sizes)
    unit = LANES * row_multiple
    padded = -(-total // unit) * unit
    if padded > total:
        flat.append(jnp.zeros((padded - total,), F32))
    offsets = [sum(sizes[:i]) for i in range(len(sizes))]
    return jnp.concatenate(flat).reshape(-1, LANES), offsets


def _unpack(packed, offsets, shapes):
    flat = packed.reshape(-1)
    return [flat[o:o + math.prod(s)].reshape(s) for o, s in zip(offsets, shapes)]


def kernel(x, c, ab_norm_g, ab_w_mod, ab_b_mod, ab_w_in, ab_conv_w, ab_w_out, sg_norm_g, sg_w_mod, sg_b_mod, sg_w_in, sg_ln_g, sg_ln_b, sg_w_s, sg_b_s, sg_w_out, final_norm_g, loss_target, m_ab_norm_g, m_ab_w_mod, m_ab_b_mod, m_ab_w_in, m_ab_conv_w, m_ab_w_out, m_sg_norm_g, m_sg_w_mod, m_sg_b_mod, m_sg_w_in, m_sg_ln_g, m_sg_ln_b, m_sg_w_s, m_sg_b_s, m_sg_w_out, m_final_norm_g, v_ab_norm_g, v_ab_w_mod, v_ab_b_mod, v_ab_w_in, v_ab_conv_w, v_ab_w_out, v_sg_norm_g, v_sg_w_mod, v_sg_b_mod, v_sg_w_in, v_sg_ln_g, v_sg_ln_b, v_sg_w_s, v_sg_b_s, v_sg_w_out, v_final_norm_g):
    s, d = x.shape[1], x.shape[2]
    aw = d // 2
    cw = d
    mod_l = ab_w_mod.shape[-1]
    x0 = x[0]
    target = loss_target[0]
    mx, my, mc = lax.axis_index("x"), lax.axis_index("y"), lax.axis_index("c")
    chip = 2 * mx + my
    chip_idx = jnp.reshape(chip, (1,)).astype(jnp.int32)
    core_idx = jnp.reshape(mc, (1,)).astype(jnp.int32)
    place_idx = jnp.stack([chip, mc]).astype(jnp.int32)

    win = [_place_own_shard(ab_w_in if L % 2 == 0 else sg_w_in, L // 2, chip_idx) for L in range(4)]
    wout = [_place_own_shard(ab_w_out if L % 2 == 0 else sg_w_out, L // 2, chip_idx) for L in range(4)]
    k_in, k_out = d, wout[0].shape[1]

    def gather_plan(ici=(), pass_on=()):
        arrays, copies, names = [], [], []
        for kind, L, only in ici:
            arrays.append(win[L] if kind == "in" else wout[L])
            names.append((kind, L))
            copies += _gather_ici(len(arrays) - 1, k_in if kind == "in" else k_out, only)
        for kind, L in pass_on:
            arrays.append(win[L] if kind == "in" else wout[L])
            names.append((kind, L))
            copies += _gather_pass_on(len(arrays) - 1, k_in if kind == "in" else k_out)
        return _Plan(tuple(arrays), tuple(copies)), names

    def absorb(plan_and_names, updated):
        _, names = plan_and_names
        for pos, (kind, L) in enumerate(names):
            if kind == "in":
                win[L] = updated[pos]
            else:
                wout[L] = updated[pos]

    win[0] = _comm_stages("gather_first_w_in", [win[0]], [_gather_ici(0, k_in), _gather_pass_on(0, k_in)],
                          chained=True)[0]

    small_local = [c[0], ab_conv_w, sg_norm_g, sg_ln_g, sg_ln_b]
    small_shapes = [a.shape for a in small_local]
    payload, small_off = _pack(small_local)
    gathered = _all_to_all(jnp.broadcast_to(payload[None], (N_DEV,) + payload.shape), "gather_small")
    per_dev = [_unpack(gathered[b], small_off, small_shapes) for b in range(N_DEV)]
    c_all = jnp.stack([per_dev[b][0] for b in range(N_DEV)])

    def from_chips(idx, axis):
        return jnp.concatenate([per_dev[2 * q][idx] for q in range(N_CHIPS)], axis=axis)

    conv_w_full = from_chips(1, 2)
    sg_norm_g_full = from_chips(2, 1)
    sg_ln_g_full = from_chips(3, 1)
    sg_ln_b_full = from_chips(4, 1)

    ab_b_local = lax.dynamic_slice_in_dim(ab_b_mod, chip * mod_l, mod_l, axis=1)
    mod_rows = []
    for layer in range(4):
        i = layer // 2
        w_mod, bias = (ab_w_mod, ab_b_local) if layer % 2 == 0 else (sg_w_mod, sg_b_mod)
        mod_rows.append(_mod_fwd(c_all, w_mod, bias[i:i + 1], i))
    mod_local = jnp.stack(mod_rows, axis=1)
    mod_recv = _all_to_all(mod_local.reshape(N_DEV, -1, LANES), "exchange_mod")
    mod_recv = mod_recv.reshape(N_DEV, 4, mod_l)
    mod_full = jnp.concatenate([mod_recv[2 * q] for q in range(N_CHIPS)], axis=-1)
    shifts = [mod_full[l:l + 1, :d] for l in range(4)]
    scales = [mod_full[l:l + 1, d:2 * d] for l in range(4)]
    gates = [mod_full[l:l + 1, 2 * d:] for l in range(4)]

    cos, sin = _rope_tables(s)
    w_s16 = sg_w_s.astype(BF16)
    b_s3 = sg_b_s[..., None]

    near, far, everyone = (0, 1), (2,), (0, 1, 2)
    fwd_comm = {
        ("in_proj", 0): ([("in", 1, near), ("out", 0, everyone)], []),
        ("attn", 0): ([("in", 1, far), ("out", 1, everyone), ("in", 2, near)], [("out", 0)]),
        ("out_proj", 0): ([], [("in", 1), ("out", 1)]),
        ("in_proj", 1): ([("in", 2, far), ("out", 2, everyone)], []),
        ("out_proj", 1): ([], [("in", 2), ("out", 2)]),
        ("in_proj", 2): ([("in", 3, everyone)], []),
        ("attn", 2): ([("out", 3, everyone)], []),
        ("out_proj", 2): ([], [("in", 3), ("out", 3)]),
    }

    def carried(key, fn, *args):
        if key not in fwd_comm:
            return fn(*args)
        pn = gather_plan(*fwd_comm[key])
        res, updated = fn(*args, plan=pn[0])
        absorb(pn, updated)
        return res

    def norm_params(layer):
        g = ab_norm_g if layer % 2 == 0 else sg_norm_g_full
        return g[layer // 2:layer // 2 + 1], scales[layer], shifts[layer]

    saved = []
    xs = x0
    h = _prenorm(xs, *norm_params(0))
    for layer in range(4):
        i = layer // 2
        next_norm = norm_params(layer + 1) if layer < 3 else None
        if layer % 2 == 0:
            proj = carried(("in_proj", layer), _in_proj, h, win[layer])
            attn, lse = carried(("attn", layer), _attn_fwd, proj, cos, sin, aw)
            y = _ab_mix(attn, proj, conv_w_full[i], aw)
            res = carried(("out_proj", layer), _out_proj_residual, y, wout[layer].reshape(-1, d), xs,
                          gates[layer], next_norm)
            saved.append((xs, h, proj, y, res[1], attn, lse))
        else:
            uvz = carried(("in_proj", layer), _in_proj, h, win[layer])
            y = _sgu_fwd(uvz, sg_ln_g_full[i:i + 1], sg_ln_b_full[i:i + 1], w_s16[i], b_s3[i], cw)
            res = carried(("out_proj", layer), _out_proj_residual, y, wout[layer].reshape(-1, d), xs,
                          gates[layer], next_norm)
            saved.append((xs, h, uvz, y, res[1]))
        xs = res[0]
        h = res[2] if next_norm is not None else None

    loss11, dx, d_final_g, dout, dgate = _final_loss(xs, target, final_norm_g[None], saved[3][4], gates[3])
    loss = lax.psum(loss11[0, 0], ("x", "y", "c"))

    reduced = {"in": [lax.empty((2,) + w.shape[1:], F32) for w in (ab_w_in, sg_w_in)],
               "out": [lax.empty((2,) + w.shape[1:], F32) for w in (ab_w_out, sg_w_out)]}
    grads = {}
    stage = {}
    k_of = {"in": k_in, "out": k_out}

    def swap_plan(which):
        arrays, copies = [], []
        for kind, L in which:
            g = grads[kind, L]
            arrays += [g, jax.ShapeDtypeStruct((N_CHIPS, g.shape[1] // 2, g.shape[2]), BF16)]
            copies += _reduce_swap(len(arrays) - 2, len(arrays) - 1, k_of[kind])
        return _Plan(tuple(arrays), tuple(copies))

    def after_swap(which, updated):
        for pos, (kind, L) in enumerate(which):
            stage[kind, L] = _add_sibling(grads[kind, L], updated[2 * pos + 1], core_idx)

    def ici_plan(pieces):
        arrays, copies = [], []
        for kind, L, only in pieces:
            cs = stage[kind, L]
            arrays += [cs, stage.get((kind, L, "recv"), jax.ShapeDtypeStruct((3,) + cs.shape[1:], BF16))]
            copies += _reduce_ici(len(arrays) - 2, len(arrays) - 1, only)
        return _Plan(tuple(arrays), tuple(copies))

    def after_ici(pieces, updated):
        for pos, (kind, L, _) in enumerate(pieces):
            stage[kind, L, "recv"] = updated[2 * pos + 1]

    def sum_layer(L):
        for kind in ("in", "out"):
            reduced[kind][L % 2] = _sum_chips(stage[kind, L], stage[kind, L, "recv"], reduced[kind][L % 2],
                                              L // 2, place_idx)

    def share_plan(L):
        arrays = (reduced["in"][L % 2], reduced["out"][L % 2])
        copies = _reduce_share(0, L // 2, k_in) + _reduce_share(1, L // 2, k_out)
        return _Plan(arrays, tuple(copies))

    def after_share(L, updated):
        reduced["in"][L % 2], reduced["out"][L % 2] = updated[0], updated[1]

    all_chips = (0, 1, 2)
    dmods = [None] * 4
    d_ab_norm_g, d_sg_norm_g = [None, None], [None, None]
    d_conv_w, d_ln_g, d_ln_b, d_w_s, d_b_s = ([None, None] for _ in range(5))
    for layer in reversed(range(4)):
        i = layer // 2
        prev = layer + 1
        busy = prev < 4
        if layer % 2 == 0:
            xs, h, proj, y, out, attn, lse = saved[layer]
        else:
            xs, h, uvz, y, out = saved[layer]
        below = (saved[layer - 1][4], gates[layer - 1]) if layer > 0 else None
        w2 = wout[layer].reshape(-1, d)
        grads["out", layer] = _out_proj_bwd_w(y, dout).reshape(N_CHIPS, -1, d)
        if busy:
            swapped = [("in", prev), ("out", prev)] + ([("out", 0)] if layer == 0 else [])
            dy, updated = _out_proj_bwd_act(dout, w2, plan=swap_plan(swapped))
            after_swap(swapped, updated)
        else:
            dy = _out_proj_bwd_act(dout, w2)
        if layer % 2 == 0:
            if busy:
                pieces = [("in", prev, all_chips), ("out", prev, all_chips)]
                pieces += [("out", 0, all_chips)] if layer == 0 else []
                dact, updated = _attn_bwd(proj, cos, sin, dy, attn, lse, aw, plan=ici_plan(pieces))
                after_ici(pieces, updated)
                sum_layer(prev)
            else:
                dact = _attn_bwd(proj, cos, sin, dy, attn, lse, aw)
            dact, d_conv_w[i] = _conv_bwd(dact, dy, proj, conv_w_full[i], aw)
            if layer == 0:
                gating, gating_off = _pack([jnp.stack(d_w_s), jnp.stack(d_b_s)], PACKED_ROW_BLOCK)
                slots = _place_own_slot(gating, jnp.reshape(2 * chip + mc, (1,)).astype(jnp.int32))
                sharing = share_plan(prev)
                plan = _merge_plans(sharing, _Plan((gating, slots), tuple(_broadcast_copies(0, 1))))
                grads["in", 0], updated = _in_proj_bwd_w(h, dact, win[0].shape[-1], plan=plan)
                after_share(prev, updated)
                all_gating = updated[len(sharing.arrays) + 1]
                plan = swap_plan([("in", 0)])
                after_swap([("in", 0)], _comm_stages("grads_to_sibling", plan.arrays, [plan.copies]))
                pieces = [("in", 0, all_chips)]
                dh, updated = _in_proj_bwd_act(dact, win[0], plan=ici_plan(pieces))
                after_ici(pieces, updated)
            elif busy:
                dh, updated = _in_proj_bwd_act(dact, win[layer], plan=share_plan(prev))
                after_share(prev, updated)
                grads["in", layer] = _in_proj_bwd_w(h, dact, win[layer].shape[-1])
            else:
                dh = _in_proj_bwd_act(dact, win[layer])
                grads["in", layer] = _in_proj_bwd_w(h, dact, win[layer].shape[-1])
            norm_g = ab_norm_g[i:i + 1]
        else:
            sgu_args = (uvz, dy, sg_ln_g_full[i:i + 1], sg_ln_b_full[i:i + 1], w_s16[i], b_s3[i], cw)
            if busy:
                pieces = [("in", prev, (0, 1)), ("out", prev, all_chips)]
                res, updated = _sgu_bwd(*sgu_args, plan=ici_plan(pieces))
                after_ici(pieces, updated)
            else:
                res = _sgu_bwd(*sgu_args)
            dact, d_w_s[i], db_wide, d_ln_g[i], d_ln_b[i] = res
            d_b_s[i] = db_wide[:, :, 0]
            if busy:
                pieces = [("in", prev, (2,))]
                dh, updated = _in_proj_bwd_act(dact, win[layer], plan=ici_plan(pieces))
                after_ici(pieces, updated)
                sum_layer(prev)
                grads["in", layer], updated = _in_proj_bwd_w(h, dact, win[layer].shape[-1], plan=share_plan(prev))
                after_share(prev, updated)
            else:
                dh = _in_proj_bwd_act(dact, win[layer])
                grads["in", layer] = _in_proj_bwd_w(h, dact, win[layer].shape[-1])
            norm_g = sg_norm_g_full[i:i + 1]
        res = _prenorm_bwd(xs, dh, dx, norm_g, scales[layer], below)
        dx, dshift, dscale, d_norm_g = res[:4]
        (d_ab_norm_g if layer % 2 == 0 else d_sg_norm_g)[i] = d_norm_g
        dmods[layer] = jnp.concatenate([dshift, dscale, dgate], axis=1)
        if below:
            dout, dgate = res[4:]
    grad_x = dx[None]

    partial_list = [jnp.concatenate(dmods, axis=0),
                    jnp.concatenate(d_ab_norm_g, axis=0), jnp.concatenate(d_sg_norm_g, axis=0), d_final_g[0],
                    jnp.stack(d_conv_w), jnp.concatenate(d_ln_g, axis=0), jnp.concatenate(d_ln_b, axis=0)]
    partial_shapes = [a.shape for a in partial_list]
    partials, part_off = _pack(partial_list, PACKED_ROW_BLOCK)
    all_partials = _all_to_all(jnp.broadcast_to(partials[None], (N_DEV,) + partials.shape), "gather_partials")
    sum_layer(0)
    summed_packed, updated = _sum_devices(all_partials, plan=share_plan(0))
    after_share(0, updated)
    summed = _unpack(summed_packed, part_off, partial_shapes)
    g_mod_bias, g_ab_norm_g, g_sg_norm_g_full, g_final_g, g_conv_full, g_ln_g_full, g_ln_b_full = summed
    g_w_s, g_b_s = _unpack(_sum_devices(all_gating), gating_off, [sg_w_s.shape, sg_b_s.shape])
    dm_all = jnp.stack([_unpack(all_partials[b], part_off[:1], partial_shapes[:1])[0] for b in range(N_DEV)])
    dm_local = lax.dynamic_slice_in_dim(dm_all, chip * mod_l, mod_l, axis=2)

    def chip_cols(a, axis):
        width = a.shape[axis] // N_CHIPS
        return lax.dynamic_slice_in_dim(a, chip * width, width, axis=axis)

    g_ab_b_mod = jnp.stack([g_mod_bias[0], g_mod_bias[2]])
    g_sg_b_mod = chip_cols(jnp.stack([g_mod_bias[1], g_mod_bias[3]]), 1)
    g_ab_w_mod = _mod_bwd_w(c_all, jnp.stack([dm_local[:, 0], dm_local[:, 2]]))
    g_sg_w_mod = _mod_bwd_w(c_all, jnp.stack([dm_local[:, 1], dm_local[:, 3]]))
    g_conv = chip_cols(g_conv_full, 2)
    g_sg_norm_g = chip_cols(g_sg_norm_g_full, 1)
    g_ln_g = chip_cols(g_ln_g_full, 1)
    g_ln_b = chip_cols(g_ln_b_full, 1)

    def step_big(w, g, m, v, also_grad=False):
        res = _adamw(_flat_rows(w), _flat_rows(g), _flat_rows(m), _flat_rows(v), also_grad)
        return tuple(a.reshape(w.shape) for a in res)

    big_out = {
        "ab_w_mod": step_big(ab_w_mod, g_ab_w_mod, m_ab_w_mod, v_ab_w_mod),
        "ab_w_in": step_big(ab_w_in, reduced["in"][0], m_ab_w_in, v_ab_w_in, True),
        "ab_w_out": step_big(ab_w_out, reduced["out"][0], m_ab_w_out, v_ab_w_out, True),
        "sg_w_mod": step_big(sg_w_mod, g_sg_w_mod, m_sg_w_mod, v_sg_w_mod),
        "sg_w_in": step_big(sg_w_in, reduced["in"][1], m_sg_w_in, v_sg_w_in, True),
        "sg_w_out": step_big(sg_w_out, reduced["out"][1], m_sg_w_out, v_sg_w_out, True),
    }
    g_ab_w_in, g_ab_w_out = big_out["ab_w_in"][3], big_out["ab_w_out"][3]
    g_sg_w_in, g_sg_w_out = big_out["sg_w_in"][3], big_out["sg_w_out"][3]
    small_names = ["ab_norm_g", "ab_b_mod", "ab_conv_w", "sg_norm_g", "sg_b_mod", "sg_ln_g", "sg_ln_b",
                   "sg_w_s", "sg_b_s", "final_norm_g"]
    small_w = [ab_norm_g, ab_b_mod, ab_conv_w, sg_norm_g, sg_b_mod, sg_ln_g, sg_ln_b, sg_w_s, sg_b_s, final_norm_g]
    small_g = [g_ab_norm_g, g_ab_b_mod, g_conv, g_sg_norm_g, g_sg_b_mod, g_ln_g, g_ln_b, g_w_s, g_b_s, g_final_g]
    small_m = [m_ab_norm_g, m_ab_b_mod, m_ab_conv_w, m_sg_norm_g, m_sg_b_mod, m_sg_ln_g, m_sg_ln_b, m_sg_w_s,
               m_sg_b_s, m_final_norm_g]
    small_v = [v_ab_norm_g, v_ab_b_mod, v_ab_conv_w, v_sg_norm_g, v_sg_b_mod, v_sg_ln_g, v_sg_ln_b, v_sg_w_s,
               v_sg_b_s, v_final_norm_g]
    shapes = [a.shape for a in small_w]
    pw, off = _pack(small_w, PACKED_ROW_BLOCK)
    pg, _ = _pack(small_g, PACKED_ROW_BLOCK)
    pm, _ = _pack(small_m, PACKED_ROW_BLOCK)
    pv, _ = _pack(small_v, PACKED_ROW_BLOCK)
    pd, pnm, pnv = _adamw(pw, pg, pm, pv)
    small_out = {}
    for name, dl, nm, nv in zip(small_names, _unpack(pd, off, shapes), _unpack(pnm, off, shapes),
                                _unpack(pnv, off, shapes)):
        small_out[name] = (dl, nm, nv)

    grad_of = {
        "ab_norm_g": g_ab_norm_g, "ab_w_mod": g_ab_w_mod, "ab_b_mod": g_ab_b_mod, "ab_w_in": g_ab_w_in,
        "ab_conv_w": g_conv, "ab_w_out": g_ab_w_out, "sg_norm_g": g_sg_norm_g, "sg_w_mod": g_sg_w_mod,
        "sg_b_mod": g_sg_b_mod, "sg_w_in": g_sg_w_in, "sg_ln_g": g_ln_g, "sg_ln_b": g_ln_b, "sg_w_s": g_w_s,
        "sg_b_s": g_b_s, "sg_w_out": g_sg_w_out, "final_norm_g": g_final_g,
    }
    order = ["ab_norm_g", "ab_w_mod", "ab_b_mod", "ab_w_in", "ab_conv_w", "ab_w_out", "sg_norm_g", "sg_w_mod",
             "sg_b_mod", "sg_w_in", "sg_ln_g", "sg_ln_b", "sg_w_s", "sg_b_s", "sg_w_out", "final_norm_g"]
    steps = {**big_out, **small_out}
    return (loss, grad_x, *[grad_of[n] for n in order], *[steps[n][0] for n in order],
            *[steps[n][1] for n in order], *[steps[n][2] for n in order])
```

```python
import math
from typing import Any, Callable, NamedTuple

import jax
import jax.numpy as jnp
import numpy as np
from jax import lax
from jax.experimental import pallas as pl
from jax.experimental.pallas import tpu as pltpu

F32 = jnp.float32
BF16 = jnp.bfloat16

HEAD_DIM = 128
RADIUS = 64
DILATIONS = (1, 4, 16)
Q_BLOCK = 256
K_WINDOW = Q_BLOCK + 2 * RADIUS
ATTN_UNROLL = 8
ROPE_THETA = 10000.0
NEG_INF = -1e30
N_GROUPS = 8
CHUNK = 128
EPS = 1e-6
CONV_ROWS = 512
CONV_HALO = 16
LANES = 128
ELEMENTWISE_BLOCK = 512 * 1024
PACKED_ROW_BLOCK = 512
N_DEV = 8
N_CHIPS = 4

ADAM_LR = 0.001
ADAM_B1 = 0.9
ADAM_B2 = 0.999
ADAM_EPS = 1e-08
ADAM_WD = 0.01
ADAM_STEP = 10

VMEM_LIMIT_V7X = 56 * 1024 * 1024

MESH_ID = pl.DeviceIdType.MESH
HBM_SPEC = pl.BlockSpec(memory_space=pltpu.HBM)

NN = (((1,), (0,)), ((), ()))
NT = (((1,), (1,)), ((), ()))
TN = (((0,), (0,)), ((), ()))


def _params(n_grid, parallel=0):
    sem = tuple(["parallel"] * parallel + ["arbitrary"] * (n_grid - parallel))
    return pltpu.CompilerParams(dimension_semantics=sem, vmem_limit_bytes=VMEM_LIMIT_V7X)


def _tile(n, prefs):
    for p in prefs:
        if n % p == 0:
            return p
    return n


def _sigmoid(z):
    return 1.0 / (1.0 + jnp.exp(-z))


def _silu(z):
    return z * _sigmoid(z)


_GELU_K = math.sqrt(2.0 / math.pi)
_GELU_C = 0.044715


def _gelu(u):
    return 0.5 * u * (1.0 + jnp.tanh(_GELU_K * (u + _GELU_C * u * u * u)))


def _gelu_and_grad(u):
    t = jnp.tanh(_GELU_K * (u + _GELU_C * u * u * u))
    half = 0.5 * (1.0 + t)
    return u * half, half + 0.5 * u * (1.0 - t * t) * _GELU_K * (1.0 + 3.0 * _GELU_C * u * u)


def _silu_and_grad(z):
    s = _sigmoid(z)
    return z * s, s * (1.0 + z * (1.0 - s))


class _Place(NamedTuple):
    x: Any
    y: Any
    c: Any
    chip: Any


def _my_place():
    mx, my, mc = lax.axis_index("x"), lax.axis_index("y"), lax.axis_index("c")
    return _Place(mx, my, mc, 2 * mx + my)


def _other_chips(p):
    return [(1 - p.x, p.y), (p.x, 1 - p.y), (1 - p.x, 1 - p.y)]


class _Copy(NamedTuple):
    src: int
    src_at: Callable
    dst: int
    dst_at: Callable
    peer: Callable


class _Plan(NamedTuple):
    arrays: tuple
    copies: tuple


def _view(ref, index):
    return ref if index is None else ref.at[index]


def _plan_io(plan):
    ins = [k for k, a in enumerate(plan.arrays) if not isinstance(a, jax.ShapeDtypeStruct)]
    written = sorted({cp.dst for cp in plan.copies})
    return ins, written


def _descriptors(plan, in_refs, out_refs, send_sems, recv_sems):
    ins, written = _plan_io(plan)
    place = _my_place()
    return [
        pltpu.make_async_remote_copy(
            src_ref=_view(in_refs[ins.index(cp.src)], cp.src_at(place)),
            dst_ref=_view(out_refs[written.index(cp.dst)], cp.dst_at(place)),
            send_sem=send_sems.at[k], recv_sem=recv_sems.at[k],
            device_id=cp.peer(place), device_id_type=MESH_ID)
        for k, cp in enumerate(plan.copies)]


def _plan_operands(plan, n_in, n_out):
    ins, written = _plan_io(plan)
    operands = [plan.arrays[k] for k in ins]
    out_shape = [jax.ShapeDtypeStruct(plan.arrays[k].shape, plan.arrays[k].dtype) for k in written]
    aliases = {n_in + ins.index(k): n_out + pos for pos, k in enumerate(written) if k in ins}
    n = len(plan.copies)
    sems = [pltpu.SemaphoreType.DMA((n,)), pltpu.SemaphoreType.DMA((n,))]
    return operands, out_shape, aliases, sems, written


def _call(body, *, name, grid, in_specs, out_specs, out_shape, operands, scratch_shapes=(), aliases=None,
          parallel=0, plan=None):
    single = not isinstance(out_shape, (list, tuple))
    out_shape = [out_shape] if single else list(out_shape)
    out_specs = [out_specs] if single else list(out_specs)
    if plan is None:
        res = pl.pallas_call(
            body, name=name, grid=grid, in_specs=list(in_specs), out_specs=out_specs, out_shape=out_shape,
            scratch_shapes=list(scratch_shapes), input_output_aliases=aliases or {},
            compiler_params=_params(len(grid), parallel=parallel),
        )(*operands)
        return res[0] if single else res

    n_in, n_out, n_scr = len(operands), len(out_shape), len(scratch_shapes)
    p_operands, p_out_shape, p_aliases, sems, written = _plan_operands(plan, n_in, n_out)
    n_pin, n_pout = len(p_operands), len(p_out_shape)

    def wrapped(*refs):
        ins = refs[:n_in]
        p_in = refs[n_in:n_in + n_pin]
        outs = refs[n_in + n_pin:n_in + n_pin + n_out]
        p_out = refs[n_in + n_pin + n_out:n_in + n_pin + n_out + n_pout]
        scratch = refs[n_in + n_pin + n_out + n_pout:n_in + n_pin + n_out + n_pout + n_scr]
        send_sems, recv_sems = refs[-2:]
        ids = [pl.program_id(a) for a in range(len(grid))]
        first = ids[0] == 0
        last = ids[0] == grid[0] - 1
        for a in range(1, len(grid)):
            first = jnp.logical_and(first, ids[a] == 0)
            last = jnp.logical_and(last, ids[a] == grid[a] - 1)

        @pl.when(first)
        def _():
            for cp in _descriptors(plan, p_in, p_out, send_sems, recv_sems):
                cp.start()

        body(*ins, *outs, *scratch)

        @pl.when(last)
        def _():
            for cp in _descriptors(plan, p_in, p_out, send_sems, recv_sems):
                cp.wait()

    res = pl.pallas_call(
        wrapped, name=name, grid=grid,
        in_specs=list(in_specs) + [HBM_SPEC] * n_pin,
        out_specs=out_specs + [HBM_SPEC] * n_pout,
        out_shape=out_shape + p_out_shape,
        scratch_shapes=list(scratch_shapes) + sems,
        input_output_aliases={**(aliases or {}), **p_aliases},
        compiler_params=_params(len(grid)),
    )(*operands, *p_operands)
    outs = res[0] if single else res[:n_out]
    return outs, dict(zip(written, res[n_out:]))


def _comm_stages(name, arrays, stages, chained=False):
    plan = _Plan(tuple(arrays), tuple(cp for st in stages for cp in st))
    p_operands, p_out_shape, p_aliases, sems, written = _plan_operands(plan, 0, 0)
    n_pin = len(p_operands)

    def body(*refs):
        p_in = refs[:n_pin]
        p_out = refs[n_pin:n_pin + len(written)]
        send_sems, recv_sems = refs[-2:]
        all_copies = _descriptors(plan, p_in, p_out, send_sems, recv_sems)
        if chained:
            n = len(stages[0])
            for cp in all_copies[:n]:
                cp.start()
            for k in range(n):
                all_copies[k].wait()
                all_copies[n + k].start()
            for cp in all_copies[n:]:
                cp.wait()
            return
        base = 0
        for st in stages:
            for cp in all_copies[base:base + len(st)]:
                cp.start()
            for cp in all_copies[base:base + len(st)]:
                cp.wait()
            base += len(st)

    res = pl.pallas_call(
        body, name=name, in_specs=[HBM_SPEC] * n_pin, out_specs=[HBM_SPEC] * len(written),
        out_shape=p_out_shape, scratch_shapes=sems, input_output_aliases=p_aliases,
    )(*p_operands)
    return dict(zip(written, res))


def _half_rows(k, c):
    return pl.ds(c * (k // 2), k // 2)


def _gather_ici(a, k, only=(0, 1, 2)):
    own = lambda p: (p.chip, _half_rows(k, p.c))
    return [_Copy(a, own, a, own, lambda p, q=q: (*_other_chips(p)[q], p.c)) for q in only]


def _gather_pass_on(a, k):
    def at(q):
        def index(p):
            px, py = _other_chips(p)[q]
            return (2 * px + py, _half_rows(k, p.c))
        return index
    return [_Copy(a, at(q), a, at(q), lambda p: (p.x, p.y, 1 - p.c)) for q in range(3)]


def _reduce_swap(src, dst, k):
    return [_Copy(src, lambda p: (pl.ds(0, N_CHIPS), _half_rows(k, 1 - p.c)), dst, lambda p: None,
                  lambda p: (p.x, p.y, 1 - p.c))]


def _reduce_ici(src, dst, only=(0, 1, 2)):
    def slab(q):
        def index(p):
            px, py = _other_chips(p)[q]
            return 2 * px + py
        return index
    return [_Copy(src, slab(q), dst, lambda p, q=q: q, lambda p, q=q: (*_other_chips(p)[q], p.c)) for q in only]


def _reduce_share(a, layer, k):
    at = lambda p: (layer, _half_rows(k, p.c))
    return [_Copy(a, at, a, at, lambda p: (p.x, p.y, 1 - p.c))]


def _merge_plans(a, b):
    off = len(a.arrays)
    moved = tuple(cp._replace(src=cp.src + off, dst=cp.dst + off) for cp in b.copies)
    return _Plan(a.arrays + b.arrays, a.copies + moved)


def _broadcast_copies(src, dst):
    me = lambda p: 2 * p.chip + p.c

    def peer(k):
        return lambda p: (1 - p.x if (k >> 2) & 1 else p.x, 1 - p.y if (k >> 1) & 1 else p.y,
                          1 - p.c if k & 1 else p.c)

    return [_Copy(src, lambda p: None, dst, me, peer(k)) for k in range(1, N_DEV)]


def _place_own_slot(x, me_idx):
    r = x.shape[0]
    tr = _tile(r, (512, 256, 128, 64, 32, 16, 8))

    def body(me_ref, x_ref, o_ref):
        o_ref[...] = x_ref[...]

    return pl.pallas_call(
        body, name="place_own_slot",
        grid_spec=pltpu.PrefetchScalarGridSpec(
            num_scalar_prefetch=1, grid=(r // tr,),
            in_specs=[pl.BlockSpec((tr, LANES), lambda i, me: (i, 0))],
            out_specs=pl.BlockSpec((None, tr, LANES), lambda i, me: (me[0], i, 0))),
        out_shape=jax.ShapeDtypeStruct((N_DEV, r, LANES), F32),
        compiler_params=_params(1, parallel=1),
    )(me_idx, x)


def _all_to_all(x, name):
    def body(x_ref, y_ref, send_sems, recv_sems, own_sem):
        p = _my_place()
        me = 2 * p.chip + p.c
        own = pltpu.make_async_copy(x_ref.at[me], y_ref.at[me], own_sem)
        own.start()
        copies = []
        for k in range(1, N_DEV):
            px = 1 - p.x if (k >> 2) & 1 else p.x
            py = 1 - p.y if (k >> 1) & 1 else p.y
            pc = 1 - p.c if k & 1 else p.c
            peer = 4 * px + 2 * py + pc
            cp = pltpu.make_async_remote_copy(
                src_ref=x_ref.at[peer], dst_ref=y_ref.at[me],
                send_sem=send_sems.at[k - 1], recv_sem=recv_sems.at[k - 1],
                device_id=(px, py, pc), device_id_type=MESH_ID)
            cp.start()
            copies.append(cp)
        for cp in copies:
            cp.wait()
        own.wait()

    return pl.pallas_call(
        body, name=name,
        out_shape=jax.ShapeDtypeStruct(x.shape, x.dtype),
        in_specs=[HBM_SPEC], out_specs=HBM_SPEC,
        scratch_shapes=[pltpu.SemaphoreType.DMA((N_DEV - 1,)), pltpu.SemaphoreType.DMA((N_DEV - 1,)),
                        pltpu.SemaphoreType.DMA],
    )(x)


def _place_own_shard(w, layer, chip_idx):
    _, k, n = w.shape
    tr = _tile(k, (512, 256, 128))

    def body(c_ref, w_ref, g_ref):
        g_ref[...] = w_ref[...].astype(BF16)

    return pl.pallas_call(
        body, name="place_own_shard",
        grid_spec=pltpu.PrefetchScalarGridSpec(
            num_scalar_prefetch=1, grid=(k // tr,),
            in_specs=[pl.BlockSpec((None, tr, n), lambda r, c: (layer, r, 0))],
            out_specs=pl.BlockSpec((None, tr, n), lambda r, c: (c[0], r, 0))),
        out_shape=jax.ShapeDtypeStruct((N_CHIPS, k, n), BF16),
        compiler_params=_params(1, parallel=1),
    )(chip_idx, w)


def _matmul(name, operands, in_specs, grid, dims, out_shape, out_specs, epilogue, a_prologue=None,
            aliases=None, plan=None):
    n_in = len(operands)

    def body(*refs):
        a = refs[0][...]
        if a_prologue is not None:
            a = a_prologue(a)
        acc = lax.dot_general(a.astype(BF16), refs[1][...].astype(BF16), dims, preferred_element_type=F32)
        epilogue(acc, refs[2:n_in], refs[n_in:])

    return _call(body, name=name, grid=grid, in_specs=in_specs, out_specs=out_specs, out_shape=out_shape,
                 operands=operands, aliases=aliases, parallel=2, plan=plan)


def _store_cast(acc, extra, outs):
    outs[0][...] = acc.astype(outs[0].dtype)


def _in_proj(h, w, plan=None):
    s, d = h.shape
    nl = w.shape[-1]
    tm = _tile(s, (1024, 512, 256))
    tn = _tile(nl, (1024, 768, 512, 384, 256, 128))
    per = nl // tn
    return _matmul(
        "in_proj", (h, w),
        [pl.BlockSpec((tm, d), lambda i, j: (i, 0)),
         pl.BlockSpec((None, d, tn), lambda i, j: (j // per, 0, j % per))],
        (s // tm, N_CHIPS * per), NN,
        jax.ShapeDtypeStruct((s, N_CHIPS * nl), BF16),
        pl.BlockSpec((tm, tn), lambda i, j: (i, j)), _store_cast, plan=plan)


def _modulated_norm(xv, g, scale, shift):
    rstd = lax.rsqrt(jnp.mean(xv * xv, axis=-1, keepdims=True) + EPS)
    return ((xv * rstd) * g * (1.0 + scale) + shift).astype(BF16)


def _out_proj_residual(y, w2, x, gate, next_norm, plan=None):
    s, wdt = y.shape
    d = w2.shape[-1]
    tm = _tile(s, (256, 128))

    def epilogue(acc, extra, outs):
        x_new = extra[0][...] + extra[1][...] * acc
        outs[0][...] = x_new
        outs[1][...] = acc.astype(BF16)
        if next_norm is not None:
            outs[2][...] = _modulated_norm(x_new, extra[2][...], extra[3][...], extra[4][...])

    rows = pl.BlockSpec((tm, d), lambda i, j: (i, 0))
    vec = pl.BlockSpec((1, d), lambda i, j: (0, 0))
    n_vec = 1 + (3 if next_norm is not None else 0)
    n_act = 1 + (1 if next_norm is not None else 0)
    return _matmul(
        "out_proj", (y, w2, x, gate) + (tuple(next_norm) if next_norm is not None else ()),
        [pl.BlockSpec((tm, wdt), lambda i, j: (i, 0)),
         pl.BlockSpec((wdt, d), lambda i, j: (0, 0), pipeline_mode=pl.Buffered(1)),
         rows] + [vec] * n_vec,
        (s // tm, 1), NN,
        [jax.ShapeDtypeStruct((s, d), F32)] + [jax.ShapeDtypeStruct((s, d), BF16)] * n_act,
        [rows] * (1 + n_act), epilogue, plan=plan)


def _out_proj_bwd_act(dout, w2, plan=None):
    s, d = dout.shape
    wdt = w2.shape[0]
    tm = _tile(s, (1024, 512, 256))
    tn = _tile(wdt, (1024, 512, 256, 128))
    return _matmul(
        "out_proj_dy", (dout, w2),
        [pl.BlockSpec((tm, d), lambda i, j: (i, 0)),
         pl.BlockSpec((tn, d), lambda i, j: (j, 0))],
        (s // tm, wdt // tn), NT,
        jax.ShapeDtypeStruct((s, wdt), BF16),
        pl.BlockSpec((tm, tn), lambda i, j: (i, j)), _store_cast, plan=plan)


def _out_proj_bwd_w(y, dout, plan=None):
    s, wdt = y.shape
    d = dout.shape[1]
    tm = _tile(wdt, (1024, 512, 256, 128))
    tn = _tile(d, (1024, 512, 256, 128))
    return _matmul(
        "out_proj_dw", (y, dout),
        [pl.BlockSpec((s, tm), lambda i, j: (0, i)),
         pl.BlockSpec((s, tn), lambda i, j: (0, j))],
        (wdt // tm, d // tn), TN,
        jax.ShapeDtypeStruct((wdt, d), BF16),
        pl.BlockSpec((tm, tn), lambda i, j: (i, j)), _store_cast, plan=plan)


def _in_proj_bwd_act(dproj, w, plan=None):
    s, n_all = dproj.shape
    d, nl = w.shape[1], w.shape[2]
    tm = _tile(s, (512, 256))
    tn = _tile(d, (512, 256, 128))

    def body(a_ref, w_ref, o_ref):
        acc = None
        for q in range(N_CHIPS):
            part = lax.dot_general(a_ref[:, q * nl:(q + 1) * nl], w_ref[q], NT, preferred_element_type=F32)
            acc = part if acc is None else acc + part
        o_ref[...] = acc.astype(BF16)

    return _call(
        body, name="in_proj_dh", grid=(s // tm, d // tn),
        in_specs=[pl.BlockSpec((tm, n_all), lambda i, j: (i, 0)),
                  pl.BlockSpec((N_CHIPS, tn, nl), lambda i, j: (0, j, 0))],
        out_specs=pl.BlockSpec((tm, tn), lambda i, j: (i, j)),
        out_shape=jax.ShapeDtypeStruct((s, d), BF16),
        operands=(dproj, w), parallel=2, plan=plan)


def _in_proj_bwd_w(h, dproj, nl, plan=None):
    s, d = h.shape
    tm = _tile(d, (1024, 512, 256, 128))
    tn = _tile(nl, (1024, 768, 512, 384, 256, 128))
    per = nl // tn
    return _matmul(
        "in_proj_dw", (h, dproj),
        [pl.BlockSpec((s, tm), lambda i, j: (0, i)),
         pl.BlockSpec((s, tn), lambda i, j: (0, j))],
        (d // tm, N_CHIPS * per), TN,
        jax.ShapeDtypeStruct((N_CHIPS, d, nl), BF16),
        pl.BlockSpec((None, tm, tn), lambda i, j: (j // per, i, j % per)), _store_cast, plan=plan)


def _mod_fwd(c_all, w_mod, bias, layer):
    nb, d = c_all.shape
    nl = w_mod.shape[-1]
    tn = _tile(nl, (768, 512, 384, 256, 128))

    def epilogue(acc, extra, outs):
        outs[0][...] = acc + extra[0][...]

    return _matmul(
        "mod_fwd", (c_all, w_mod, bias),
        [pl.BlockSpec((nb, d), lambda i, j: (0, 0)),
         pl.BlockSpec((None, d, tn), lambda i, j: (layer, 0, j)),
         pl.BlockSpec((1, tn), lambda i, j: (0, j))],
        (1, nl // tn), NN,
        jax.ShapeDtypeStruct((nb, nl), F32),
        pl.BlockSpec((nb, tn), lambda i, j: (0, j)), epilogue, a_prologue=_silu)


def _mod_bwd_w(c_all, dm_pair):
    nb, d = c_all.shape
    nl = dm_pair.shape[-1]
    tm = _tile(d, (1024, 512, 256, 128))
    tn = _tile(nl, (768, 512, 384, 256, 128))

    def body(c_ref, dm_ref, o_ref):
        o_ref[...] = lax.dot_general(_silu(c_ref[...]).astype(BF16), dm_ref[...].astype(BF16), TN,
                                     preferred_element_type=F32)

    return _call(
        body, name="mod_dw", grid=(2, d // tm, nl // tn),
        in_specs=[pl.BlockSpec((nb, tm), lambda l, i, j: (0, i)),
                  pl.BlockSpec((None, nb, tn), lambda l, i, j: (l, 0, j))],
        out_specs=pl.BlockSpec((None, tm, tn), lambda l, i, j: (l, i, j)),
        out_shape=jax.ShapeDtypeStruct((2, d, nl), F32),
        operands=(c_all, dm_pair), parallel=3)


def _rows_call(name, body, operands, in_specs, out_shape, out_specs, n_tiles):
    return pl.pallas_call(
        body, name=name, grid=(n_tiles,), in_specs=in_specs, out_specs=out_specs, out_shape=out_shape,
        compiler_params=_params(1),
    )(*operands)


def _row_spec(tr, width):
    return pl.BlockSpec((tr, width), lambda i: (i, 0))


def _vec_spec(width):
    return pl.BlockSpec((1, width), lambda i: (0, 0))


def _accumulate(ref, val):
    first = pl.program_id(0) == 0

    @pl.when(first)
    def _():
        ref[...] = val

    @pl.when(jnp.logical_not(first))
    def _():
        ref[...] += val


def _prenorm(x, g, scale, shift):
    s, d = x.shape
    tr = _tile(s, (256, 128))

    def body(x_ref, g_ref, sc_ref, sh_ref, h_ref):
        h_ref[...] = _modulated_norm(x_ref[...], g_ref[...], sc_ref[...], sh_ref[...])

    return _rows_call("prenorm", body, (x, g, scale, shift),
                      [_row_spec(tr, d), _vec_spec(d), _vec_spec(d), _vec_spec(d)],
                      jax.ShapeDtypeStruct((s, d), BF16), _row_spec(tr, d), s // tr)


def _gate_grads(dxv, out_ref, gate_ref, dout_ref, dgate_ref):
    dout_ref[...] = (gate_ref[...] * dxv).astype(BF16)
    _accumulate(dgate_ref, jnp.sum(dxv * out_ref[...].astype(F32), axis=0, keepdims=True))


def _prenorm_bwd(x, dh, dres, g, scale, below=None):
    s, d = x.shape
    tr = _tile(s, (256, 128))

    def body(x_ref, dh_ref, dres_ref, g_ref, sc_ref, *rest):
        dx_ref, dshift_ref, dscale_ref, dg_ref = rest[-6:-2] if below else rest
        xv = x_ref[...]
        dhv = dh_ref[...].astype(F32)
        rstd = lax.rsqrt(jnp.mean(xv * xv, axis=-1, keepdims=True) + EPS)
        xhat = xv * rstd
        gv = g_ref[...]
        one_sc = 1.0 + sc_ref[...]
        dxhat = dhv * gv * one_sc
        dxv = dres_ref[...] + rstd * (dxhat - xhat * jnp.mean(dxhat * xhat, axis=-1, keepdims=True))
        dx_ref[...] = dxv
        _accumulate(dshift_ref, jnp.sum(dhv, axis=0, keepdims=True))
        _accumulate(dscale_ref, jnp.sum(dhv * xhat * gv, axis=0, keepdims=True))
        _accumulate(dg_ref, jnp.sum(dhv * xhat * one_sc, axis=0, keepdims=True))
        if below:
            _gate_grads(dxv, rest[0], rest[1], rest[-2], rest[-1])

    vec = jax.ShapeDtypeStruct((1, d), F32)
    operands = (x, dh, dres, g, scale) + (tuple(below) if below else ())
    in_specs = [_row_spec(tr, d), _row_spec(tr, d), _row_spec(tr, d), _vec_spec(d), _vec_spec(d)]
    out_shape = [jax.ShapeDtypeStruct((s, d), F32), vec, vec, vec]
    out_specs = [_row_spec(tr, d), _vec_spec(d), _vec_spec(d), _vec_spec(d)]
    if below:
        in_specs += [_row_spec(tr, d), _vec_spec(d)]
        out_shape += [jax.ShapeDtypeStruct((s, d), BF16), vec]
        out_specs += [_row_spec(tr, d), _vec_spec(d)]
    return _rows_call("prenorm_bwd", body, operands, in_specs, out_shape, out_specs, s // tr)


def _final_loss(x, target, g, out_below, gate_below):
    s, d = x.shape
    tr = _tile(s, (256, 128))
    n_tiles = s // tr

    def body(x_ref, t_ref, g_ref, out_ref, gate_ref, loss_ref, dx_ref, dg_ref, dout_ref, dgate_ref, acc_ref):
        xv = x_ref[...]
        rstd = lax.rsqrt(jnp.mean(xv * xv, axis=-1, keepdims=True) + EPS)
        xhat = xv * rstd
        gv = g_ref[...]
        err = xhat * gv - t_ref[...]
        dy = err * (1.0 / d)
        dxhat = dy * gv
        dxv = rstd * (dxhat - xhat * jnp.mean(dxhat * xhat, axis=-1, keepdims=True))
        dx_ref[...] = dxv
        _accumulate(dg_ref, jnp.sum(dy * xhat, axis=0, keepdims=True))
        _accumulate(acc_ref, jnp.sum(err * err, axis=0, keepdims=True))
        _gate_grads(dxv, out_ref, gate_ref, dout_ref, dgate_ref)

        @pl.when(pl.program_id(0) == n_tiles - 1)
        def _():
            loss_ref[...] = (0.5 / d) * jnp.sum(acc_ref[...], axis=1, keepdims=True)

    vec = jax.ShapeDtypeStruct((1, d), F32)
    return pl.pallas_call(
        body, name="final_loss", grid=(n_tiles,),
        in_specs=[_row_spec(tr, d), _row_spec(tr, d), _vec_spec(d), _row_spec(tr, d), _vec_spec(d)],
        out_specs=[pl.BlockSpec((1, 1), lambda i: (0, 0)), _row_spec(tr, d), _vec_spec(d), _row_spec(tr, d),
                   _vec_spec(d)],
        out_shape=[jax.ShapeDtypeStruct((1, 1), F32), jax.ShapeDtypeStruct((s, d), F32), vec,
                   jax.ShapeDtypeStruct((s, d), BF16), vec],
        scratch_shapes=[pltpu.VMEM((1, d), F32)],
        compiler_params=_params(1),
    )(x, target, g, out_below, gate_below)


def _rope(t, cos, sin):
    return t * cos + pltpu.roll(t, HEAD_DIM // 2, axis=1) * sin


def _unrope(dt, cos, sin):
    return dt * cos + pltpu.roll(dt * sin, HEAD_DIM // 2, axis=1)


def _band_blocks(s, dil):
    sub = s // dil
    kw = min(K_WINDOW, sub)

    def rows(r, start, n):
        if dil == 1:
            return pl.ds(pl.multiple_of(start, RADIUS), n)
        return pl.ds(r + dil * start, n, stride=dil)

    def window(idx):
        nb = sub // Q_BLOCK
        r, b = idx // nb, idx % nb
        q0 = b * Q_BLOCK
        start = jnp.clip(q0 - RADIUS, 0, sub - kw)
        ahead = (lax.broadcasted_iota(jnp.int32, (Q_BLOCK, kw), 1)
                 - lax.broadcasted_iota(jnp.int32, (Q_BLOCK, kw), 0)) + (start - q0 + RADIUS)
        valid = lax.bitcast_convert_type(ahead, jnp.uint32) <= 2 * RADIUS
        return rows(r, q0, Q_BLOCK), rows(r, start, kw), valid

    return window


def _store_column_tiles(tiles, dst_ref, sems, col_blocks):
    rows = tiles.shape[1]
    copies = []
    for g, cb in enumerate(col_blocks):
        cols = pl.ds(pl.multiple_of(cb * LANES, LANES), LANES)
        cp = pltpu.make_async_copy(tiles.at[g], dst_ref.at[pl.ds(0, rows), cols], sems.at[g])
        cp.start()
        copies.append(cp)
    for cp in copies:
        cp.wait()


def _head_col(s, group, nh):
    return pl.BlockSpec((s, HEAD_DIM), lambda h: (0, group * nh + h), pipeline_mode=pl.Buffered(1))


def _attn_fwd(proj, cos, sin, aw, plan=None):
    s = proj.shape[0]
    nh = aw // HEAD_DIM
    scale = HEAD_DIM ** -0.5
    n_blocks = s // Q_BLOCK

    def body(q_ref, k_ref, v_ref, cos_ref, sin_ref, attn_ref, lse_ref, qf, kf, vf, acc):
        cosv, sinv = cos_ref[...], sin_ref[...]
        qf[...] = _rope(q_ref[...].astype(F32), cosv, sinv) * scale
        kf[...] = _rope(k_ref[...].astype(F32), cosv, sinv)
        vf[...] = v_ref[...].astype(F32)

        for pattern, dil in enumerate(DILATIONS):
            window = _band_blocks(s, dil)

            def block(idx, carry, window=window, first=(pattern == 0)):
                q_rows, k_rows, valid = window(idx)
                q = qf[q_rows, :].astype(BF16)
                kk = kf[k_rows, :].astype(BF16)
                vv = vf[k_rows, :].astype(BF16)
                sc = lax.dot_general(q, kk, NT, preferred_element_type=F32)
                sc = jnp.where(valid, sc, NEG_INF)
                m = jnp.max(sc, axis=1, keepdims=True)
                p = jnp.exp(sc - m)
                den = jnp.sum(p, axis=1, keepdims=True)
                o = lax.dot_general(p.astype(BF16), vv, NN, preferred_element_type=F32) / den
                lse = jnp.broadcast_to(m + jnp.log(den), (Q_BLOCK, HEAD_DIM))
                if first:
                    acc[q_rows, :] = o
                    lse_ref[q_rows, :] = lse
                else:
                    lse_old = lse_ref[q_rows, :]
                    top = jnp.maximum(lse_old, lse)
                    w_old, w_new = jnp.exp(lse_old - top), jnp.exp(lse - top)
                    tot = w_old + w_new
                    acc[q_rows, :] = (acc[q_rows, :] * w_old + o * w_new) / tot
                    lse_ref[q_rows, :] = top + jnp.log(tot)
                return carry

            lax.fori_loop(0, n_blocks, block, 0, unroll=ATTN_UNROLL)

        attn_ref[...] = acc[...].astype(BF16)

    table = pl.BlockSpec((s, HEAD_DIM), lambda h: (0, 0), pipeline_mode=pl.Buffered(1))
    out = pl.BlockSpec((s, HEAD_DIM), lambda h: (0, h))
    return _call(
        body, name="attn_fwd", grid=(nh,),
        in_specs=[_head_col(s, 0, nh), _head_col(s, 1, nh), _head_col(s, 2, nh), table, table],
        out_specs=[out, out],
        out_shape=[jax.ShapeDtypeStruct((s, aw), BF16), jax.ShapeDtypeStruct((s, aw), F32)],
        scratch_shapes=[pltpu.VMEM((s, HEAD_DIM), F32)] * 4,
        operands=(proj, proj, proj, cos, sin), parallel=1, plan=plan)


def _attn_bwd(proj, cos, sin, dy, attn, lse, aw, plan=None):
    s = proj.shape[0]
    nh = aw // HEAD_DIM
    scale = HEAD_DIM ** -0.5
    n_blocks = s // Q_BLOCK

    def body(q_ref, k_ref, v_ref, za_ref, cos_ref, sin_ref, dy_ref, attn_ref, lse_ref,
             dproj_ref, qf, kf, vf, dof, delta, dqa, dka, dva, tiles, tile_sems):
        cosv, sinv = cos_ref[...], sin_ref[...]
        qf[...] = _rope(q_ref[...].astype(F32), cosv, sinv) * scale
        kf[...] = _rope(k_ref[...].astype(F32), cosv, sinv)
        vf[...] = v_ref[...].astype(F32)
        dyv, zav, attnv = dy_ref[...].astype(F32), za_ref[...].astype(F32), attn_ref[...].astype(F32)
        silu_za, dsilu_za = _silu_and_grad(zav)
        do_all = dyv * silu_za
        dof[...] = do_all
        tiles[3] = (dyv * attnv * dsilu_za).astype(BF16)
        delta[...] = jnp.broadcast_to(jnp.sum(do_all * attnv, axis=1, keepdims=True), (s, HEAD_DIM))
        dqa[...] = jnp.zeros_like(dqa)
        dka[...] = jnp.zeros_like(dka)
        dva[...] = jnp.zeros_like(dva)

        for dil in DILATIONS:
            window = _band_blocks(s, dil)

            def block(idx, carry, window=window):
                q_rows, k_rows, valid = window(idx)
                q = qf[q_rows, :].astype(BF16)
                kk = kf[k_rows, :].astype(BF16)
                vv = vf[k_rows, :].astype(BF16)
                dov = dof[q_rows, :].astype(BF16)
                lse_q = lse_ref[q_rows, :][:, 0:1]
                delta_q = delta[q_rows, :][:, 0:1]
                sc = lax.dot_general(q, kk, NT, preferred_element_type=F32)
                p = jnp.where(valid, jnp.exp(sc - lse_q), 0.0)
                dp = lax.dot_general(dov, vv, NT, preferred_element_type=F32)
                ds = (p * (dp - delta_q)).astype(BF16)
                dqa[q_rows, :] += lax.dot_general(ds, kk, NN, preferred_element_type=F32)
                dka[k_rows, :] += lax.dot_general(ds, q, TN, preferred_element_type=F32)
                dva[k_rows, :] += lax.dot_general(p.astype(BF16), dov, TN, preferred_element_type=F32)
                return carry

            lax.fori_loop(0, n_blocks, block, 0, unroll=ATTN_UNROLL)

        tiles[0] = (_unrope(dqa[...], cosv, sinv) * scale).astype(BF16)
        tiles[1] = _unrope(dka[...], cosv, sinv).astype(BF16)
        tiles[2] = dva[...].astype(BF16)
        _store_column_tiles(tiles, dproj_ref, tile_sems, [g * nh + pl.program_id(0) for g in range(4)])

    own = pl.BlockSpec((s, HEAD_DIM), lambda h: (0, h), pipeline_mode=pl.Buffered(1))
    table = pl.BlockSpec((s, HEAD_DIM), lambda h: (0, 0), pipeline_mode=pl.Buffered(1))
    return _call(
        body, name="attn_bwd", grid=(nh,),
        in_specs=[_head_col(s, 0, nh), _head_col(s, 1, nh), _head_col(s, 2, nh), _head_col(s, 3, nh),
                  table, table, own, own, own],
        out_specs=HBM_SPEC,
        out_shape=jax.ShapeDtypeStruct((s, 8 * aw), BF16),
        scratch_shapes=[pltpu.VMEM((s, HEAD_DIM), F32)] * 8 + [
            pltpu.VMEM((4, s, HEAD_DIM), BF16), pltpu.SemaphoreType.DMA((4,))],
        operands=(proj, proj, proj, proj, cos, sin, dy, attn, lse), plan=plan)


def _rope_tables(s):
    half = HEAD_DIM // 2
    inv = np.float32(ROPE_THETA) ** (-np.arange(half, dtype=np.float32) / np.float32(half))
    ang = np.arange(s, dtype=np.float32)[:, None] * inv[None, :]
    cos, sin = np.cos(ang), np.sin(ang)
    return (jnp.asarray(np.concatenate([cos, cos], axis=-1), F32),
            jnp.asarray(np.concatenate([-sin, sin], axis=-1), F32))


def _conv_chunks(s):
    for k in range(s // CONV_ROWS):
        lo = max(0, k * CONV_ROWS - CONV_HALO)
        hi = min(s, (k + 1) * CONV_ROWS + CONV_HALO)
        yield k * CONV_ROWS, lo, hi


def _neighbours(p, lo, s):
    n = p.shape[0]
    row = lo + lax.broadcasted_iota(jnp.int32, p.shape, 0)
    prev = jnp.where(row == 0, 0.0, pltpu.roll(p, 1, axis=0))
    nxt = jnp.where(row == s - 1, 0.0, pltpu.roll(p, n - 1, axis=0))
    return prev, nxt


def _ab_mix(attn, proj, conv_w, aw):
    s = proj.shape[0]
    nt = aw // LANES

    def col(group, sel):
        return pl.BlockSpec((s, LANES), lambda i: (0, group * nt + sel(i)))

    a_sel = lambda i: jnp.minimum(i, nt - 1)
    b_sel = lambda i: jnp.maximum(i - nt, 0)

    def body(attn_ref, za_ref, ub_ref, gb_ref, gc_ref, zb_ref, w_ref, y_ref):
        i = pl.program_id(0)

        @pl.when(i < nt)
        def _():
            y_ref[...] = (attn_ref[...].astype(F32) * _silu(za_ref[...].astype(F32))).astype(BF16)

        @pl.when(i >= nt)
        def _():
            w = w_ref[...]
            for c0, lo, hi in _conv_chunks(s):
                p = gc_ref[lo:hi, :].astype(F32) * ub_ref[lo:hi, :].astype(F32)
                prev, nxt = _neighbours(p, lo, s)
                cv = w[0:1, :] * prev + w[1:2, :] * p + w[2:3, :] * nxt
                yb = gb_ref[lo:hi, :].astype(F32) * cv * _silu(zb_ref[lo:hi, :].astype(F32))
                y_ref[c0:c0 + CONV_ROWS, :] = yb[c0 - lo:c0 - lo + CONV_ROWS, :].astype(BF16)

    return pl.pallas_call(
        body, name="ab_mix", grid=(2 * nt,),
        in_specs=[pl.BlockSpec((s, LANES), lambda i: (0, a_sel(i))),
                  col(3, a_sel), col(4, b_sel), col(5, b_sel), col(6, b_sel), col(7, b_sel),
                  pl.BlockSpec((3, LANES), lambda i: (0, b_sel(i)))],
        out_specs=pl.BlockSpec((s, LANES), lambda i: (0, i)),
        out_shape=jax.ShapeDtypeStruct((s, 2 * aw), BF16),
        compiler_params=_params(1),
    )(attn, proj, proj, proj, proj, proj, conv_w)


def _conv_bwd(dproj, dy, proj, conv_w, aw):
    s = proj.shape[0]
    nt = aw // LANES

    def col(group):
        return pl.BlockSpec((s, LANES), lambda i: (0, group * nt + i))

    def body(dyb_ref, ub_ref, gb_ref, gc_ref, zb_ref, w_ref, dproj_in, dproj_ref, dw_ref, tiles, tile_sems):
        w = w_ref[...]
        dw = [jnp.zeros((1, LANES), F32) for _ in range(3)]
        for c0, lo, hi in _conv_chunks(s):
            ctr = slice(c0 - lo, c0 - lo + CONV_ROWS)
            out_rows = slice(c0, c0 + CONV_ROWS)
            ub = ub_ref[lo:hi, :].astype(F32)
            gc = gc_ref[lo:hi, :].astype(F32)
            gb = gb_ref[lo:hi, :].astype(F32)
            zb = zb_ref[lo:hi, :].astype(F32)
            dyb = dyb_ref[lo:hi, :].astype(F32)
            p = gc * ub
            prev, nxt = _neighbours(p, lo, s)
            cv = w[0:1, :] * prev + w[1:2, :] * p + w[2:3, :] * nxt
            sz, dsz = _silu_and_grad(zb)
            dcv = dyb * gb * sz
            dprev, dnxt = _neighbours(dcv, lo, s)
            dp = w[0:1, :] * dnxt + w[1:2, :] * dcv + w[2:3, :] * dprev
            for t, nb in enumerate((prev, p, nxt)):
                dw[t] = dw[t] + jnp.sum((dcv * nb)[ctr, :], axis=0, keepdims=True)
            tiles[0, out_rows, :] = (dp * gc)[ctr, :].astype(BF16)
            tiles[1, out_rows, :] = (dyb * cv * sz)[ctr, :].astype(BF16)
            tiles[2, out_rows, :] = (dp * ub)[ctr, :].astype(BF16)
            tiles[3, out_rows, :] = (dyb * gb * cv * dsz)[ctr, :].astype(BF16)
        dw_ref[...] = jnp.concatenate(dw, axis=0)
        _store_column_tiles(tiles, dproj_ref, tile_sems, [(4 + g) * nt + pl.program_id(0) for g in range(4)])

    return pl.pallas_call(
        body, name="conv_bwd", grid=(nt,),
        in_specs=[pl.BlockSpec((s, LANES), lambda i: (0, nt + i)),
                  col(4), col(5), col(6), col(7),
                  pl.BlockSpec((3, LANES), lambda i: (0, i)), HBM_SPEC],
        out_specs=[HBM_SPEC, pl.BlockSpec((3, LANES), lambda i: (0, i))],
        out_shape=[jax.ShapeDtypeStruct(dproj.shape, dproj.dtype), jax.ShapeDtypeStruct((3, aw), F32)],
        scratch_shapes=[pltpu.VMEM((4, s, LANES), BF16), pltpu.SemaphoreType.DMA((4,))],
        input_output_aliases={6: 0},
        compiler_params=_params(1),
    )(dy, proj, proj, proj, proj, conv_w, dproj)


def _sgu_norm(gv, ln_g, ln_b):
    mu = jnp.mean(gv, axis=-1, keepdims=True)
    xc = gv - mu
    rstd = lax.rsqrt(jnp.mean(xc * xc, axis=-1, keepdims=True) + EPS)
    vhat = xc * rstd
    return vhat, rstd, vhat * ln_g + ln_b


def _sgu_fwd(uvz, ln_g, ln_b, w_s, b_s, cw):
    s = uvz.shape[0]
    tr = 2 * CHUNK if s % (2 * CHUNK) == 0 else CHUNK
    gw = cw // N_GROUPS

    def body(u_ref, v_ref, z_ref, g_ref, b_ref, ws_ref, bs_ref, y_ref):
        _, _, vn = _sgu_norm(_gelu(v_ref[...].astype(F32)), g_ref[...], b_ref[...])
        vn = vn.astype(BF16)
        for ch in range(tr // CHUNK):
            rows = slice(ch * CHUNK, (ch + 1) * CHUNK)
            for grp in range(N_GROUPS):
                cols = slice(grp * gw, (grp + 1) * gw)
                mixed = lax.dot_general(ws_ref[grp], vn[rows, cols], NN, preferred_element_type=F32) + bs_ref[grp]
                y_ref[rows, cols] = (_gelu(u_ref[rows, cols].astype(F32)) * mixed
                                     * _silu(z_ref[rows, cols].astype(F32))).astype(BF16)

    full3 = lambda shape: pl.BlockSpec(shape, lambda i: (0, 0, 0))
    return pl.pallas_call(
        body, name="sgu_fwd", grid=(s // tr,),
        in_specs=[pl.BlockSpec((tr, cw), lambda i: (i, 0)), pl.BlockSpec((tr, cw), lambda i: (i, 1)),
                  pl.BlockSpec((tr, cw), lambda i: (i, 2)), _vec_spec(cw), _vec_spec(cw),
                  full3(w_s.shape), full3(b_s.shape)],
        out_specs=pl.BlockSpec((tr, cw), lambda i: (i, 0)),
        out_shape=jax.ShapeDtypeStruct((s, cw), BF16),
        compiler_params=_params(1, parallel=1),
    )(uvz, uvz, uvz, ln_g, ln_b, w_s, b_s)


def _sgu_bwd(uvz, dy, ln_g, ln_b, w_s, b_s, cw, plan=None):
    s = uvz.shape[0]
    tr = 2 * CHUNK if s % (2 * CHUNK) == 0 else CHUNK
    gw = cw // N_GROUPS

    def body(u_ref, v_ref, z_ref, dy_ref, g_ref, b_ref, ws_ref, bs_ref,
             duvz_ref, dws_ref, dbs_ref, dg_ref, db_ref, dvn_ref):
        vv = v_ref[...].astype(F32)
        gvec = g_ref[...]
        gelu_v, dgelu_v = _gelu_and_grad(vv)
        vhat, rstd, vn = _sgu_norm(gelu_v, gvec, b_ref[...])
        vn = vn.astype(BF16)
        first = pl.program_id(0) == 0

        @pl.when(first)
        def _():
            dws_ref[...] = jnp.zeros_like(dws_ref)
            dbs_ref[...] = jnp.zeros_like(dbs_ref)

        for ch in range(tr // CHUNK):
            rows = slice(ch * CHUNK, (ch + 1) * CHUNK)
            for grp in range(N_GROUPS):
                cols = slice(grp * gw, (grp + 1) * gw)
                vn_g = vn[rows, cols]
                mixed = lax.dot_general(ws_ref[grp], vn_g, NN, preferred_element_type=F32) + bs_ref[grp]
                uu = u_ref[rows, cols].astype(F32)
                zz = z_ref[rows, cols].astype(F32)
                dyv = dy_ref[rows, cols].astype(F32)
                (gu, dgu), (sz, dsz) = _gelu_and_grad(uu), _silu_and_grad(zz)
                duvz_ref[rows, grp * gw:(grp + 1) * gw] = (dyv * mixed * sz * dgu).astype(BF16)
                duvz_ref[rows, 2 * cw + grp * gw:2 * cw + (grp + 1) * gw] = (dyv * gu * mixed * dsz).astype(BF16)
                dmixed = dyv * gu * sz
                dm16 = dmixed.astype(BF16)
                dws_ref[grp] += lax.dot_general(dm16, vn_g, NT, preferred_element_type=F32)
                dbs_ref[grp] += jnp.broadcast_to(jnp.sum(dmixed, axis=1, keepdims=True), (CHUNK, LANES))
                dvn_ref[rows, cols] = lax.dot_general(ws_ref[grp], dm16, TN, preferred_element_type=F32)

        dvn = dvn_ref[...]
        _accumulate(dg_ref, jnp.sum(dvn * vhat, axis=0, keepdims=True))
        _accumulate(db_ref, jnp.sum(dvn, axis=0, keepdims=True))
        dvhat = dvn * gvec
        dgv = rstd * (dvhat - jnp.mean(dvhat, axis=-1, keepdims=True)
                      - vhat * jnp.mean(dvhat * vhat, axis=-1, keepdims=True))
        duvz_ref[:, cw:2 * cw] = (dgv * dgelu_v).astype(BF16)

    full3 = lambda shape: pl.BlockSpec(shape, lambda i: (0, 0, 0))
    acc3 = jax.ShapeDtypeStruct((N_GROUPS, CHUNK, LANES), F32)
    vec = jax.ShapeDtypeStruct((1, cw), F32)
    row = pl.BlockSpec((tr, cw), lambda i: (i, 0))
    return _call(
        body, name="sgu_bwd", grid=(s // tr,),
        in_specs=[row, pl.BlockSpec((tr, cw), lambda i: (i, 1)), pl.BlockSpec((tr, cw), lambda i: (i, 2)),
                  row, _vec_spec(cw), _vec_spec(cw), full3(w_s.shape), full3(b_s.shape)],
        out_specs=[pl.BlockSpec((tr, 3 * cw), lambda i: (i, 0)), full3((N_GROUPS, CHUNK, LANES)),
                   full3((N_GROUPS, CHUNK, LANES)), _vec_spec(cw), _vec_spec(cw)],
        out_shape=[jax.ShapeDtypeStruct((s, 3 * cw), BF16), acc3, acc3, vec, vec],
        scratch_shapes=[pltpu.VMEM((tr, cw), F32)],
        operands=(uvz, uvz, uvz, dy, ln_g, ln_b, w_s, b_s), plan=plan)


def _flat_rows(a):
    return a.reshape(-1, a.shape[-1])


def _add_sibling(grad, recv, core_idx):
    nchip, k, n = grad.shape
    tr = _tile(k // 2, (256, 128))
    nb = (k // 2) // tr

    def body(c_ref, g_ref, r_ref, o_ref):
        o_ref[...] = (g_ref[...].astype(F32) + r_ref[...].astype(F32)).astype(BF16)

    return pl.pallas_call(
        body, name="add_sibling",
        grid_spec=pltpu.PrefetchScalarGridSpec(
            num_scalar_prefetch=1, grid=(nchip, nb),
            in_specs=[pl.BlockSpec((None, tr, n), lambda q, i, c: (q, c[0] * nb + i, 0)),
                      pl.BlockSpec((None, tr, n), lambda q, i, c: (q, i, 0))],
            out_specs=pl.BlockSpec((None, tr, n), lambda q, i, c: (q, i, 0))),
        out_shape=jax.ShapeDtypeStruct((nchip, k // 2, n), BF16),
        compiler_params=_params(2, parallel=2),
    )(core_idx, grad, recv)


def _sum_chips(own, others, reduced, layer, place_idx):
    _, kh, n = own.shape
    tr = _tile(kh, (256, 128))
    nb = kh // tr

    def body(place_ref, own_ref, oth_ref, red_ref, o_ref):
        acc = own_ref[...].astype(F32)
        for q in range(3):
            acc = acc + oth_ref[q].astype(F32)
        o_ref[...] = acc

    return pl.pallas_call(
        body, name="sum_chips",
        grid_spec=pltpu.PrefetchScalarGridSpec(
            num_scalar_prefetch=1, grid=(nb,),
            in_specs=[pl.BlockSpec((None, tr, n), lambda i, p: (p[0], i, 0)),
                      pl.BlockSpec((3, tr, n), lambda i, p: (0, i, 0)),
                      HBM_SPEC],
            out_specs=pl.BlockSpec((None, tr, n), lambda i, p: (layer, p[1] * nb + i, 0))),
        out_shape=jax.ShapeDtypeStruct(reduced.shape, reduced.dtype),
        input_output_aliases={3: 0},
        compiler_params=_params(1, parallel=1),
    )(place_idx, own, others, reduced)


def _sum_devices(parts, plan=None):
    nd, r, _ = parts.shape
    tr = _tile(r, (512, 256, 128, 64, 32, 16, 8))

    def body(p_ref, o_ref):
        acc = p_ref[0]
        for q in range(1, nd):
            acc = acc + p_ref[q]
        o_ref[...] = acc

    return _call(
        body, name="sum_devices", grid=(r // tr,),
        in_specs=[pl.BlockSpec((nd, tr, LANES), lambda i: (0, i, 0))],
        out_specs=pl.BlockSpec((tr, LANES), lambda i: (i, 0)),
        out_shape=jax.ShapeDtypeStruct((r, LANES), F32),
        operands=(parts,), parallel=1, plan=plan)


def _adamw(w, g, m, v, also_grad=False):
    r, n = w.shape
    tr = _tile(r, [p for p in (1024, 512, 256, 128, 64, 32, 16, 8) if p * n <= ELEMENTWISE_BLOCK])
    n_out = 4 if also_grad else 3

    def body(w_ref, g_ref, m_ref, v_ref, d_ref, nm_ref, nv_ref, *g_out):
        gv = g_ref[...]
        if also_grad:
            g_out[0][...] = gv
        nm = ADAM_B1 * m_ref[...] + (1.0 - ADAM_B1) * gv
        nv = ADAM_B2 * v_ref[...] + (1.0 - ADAM_B2) * (gv * gv)
        m_hat = nm / (1.0 - ADAM_B1 ** ADAM_STEP)
        v_hat = nv / (1.0 - ADAM_B2 ** ADAM_STEP)
        d_ref[...] = -ADAM_LR * (m_hat / (jnp.sqrt(v_hat) + ADAM_EPS) + ADAM_WD * w_ref[...])
        nm_ref[...] = nm
        nv_ref[...] = nv

    spec = pl.BlockSpec((tr, n), lambda i: (i, 0))
    shp = jax.ShapeDtypeStruct((r, n), F32)
    return _call(
        body, name="adamw", grid=(r // tr,),
        in_specs=[spec] * 4, out_specs=[spec] * n_out, out_shape=[shp] * n_out,
        operands=(w, g, m, v), parallel=1)


def _pack(arrays, row_multiple=8):
    flat = [a.reshape(-1) for a in arrays]
    sizes = [f.shape[0] for f in flat]
    total = sum(sizes)
    unit = LANES * row_multiple
    padded = -(-total // unit) * unit
    if padded > total:
        flat.append(jnp.zeros((padded - total,), F32))
    offsets = [sum(sizes[:i]) for i in range(len(sizes))]
    return jnp.concatenate(flat).reshape(-1, LANES), offsets


def _unpack(packed, offsets, shapes):
    flat = packed.reshape(-1)
    return [flat[o:o + math.prod(s)].reshape(s) for o, s in zip(offsets, shapes)]


def kernel(x, c, ab_norm_g, ab_w_mod, ab_b_mod, ab_w_in, ab_conv_w, ab_w_out, sg_norm_g, sg_w_mod, sg_b_mod, sg_w_in, sg_ln_g, sg_ln_b, sg_w_s, sg_b_s, sg_w_out, final_norm_g, loss_target, m_ab_norm_g, m_ab_w_mod, m_ab_b_mod, m_ab_w_in, m_ab_conv_w, m_ab_w_out, m_sg_norm_g, m_sg_w_mod, m_sg_b_mod, m_sg_w_in, m_sg_ln_g, m_sg_ln_b, m_sg_w_s, m_sg_b_s, m_sg_w_out, m_final_norm_g, v_ab_norm_g, v_ab_w_mod, v_ab_b_mod, v_ab_w_in, v_ab_conv_w, v_ab_w_out, v_sg_norm_g, v_sg_w_mod, v_sg_b_mod, v_sg_w_in, v_sg_ln_g, v_sg_ln_b, v_sg_w_s, v_sg_b_s, v_sg_w_out, v_final_norm_g):
    s, d = x.shape[1], x.shape[2]
    aw = d // 2
    cw = d
    mod_l = ab_w_mod.shape[-1]
    x0 = x[0]
    target = loss_target[0]
    mx, my, mc = lax.axis_index("x"), lax.axis_index("y"), lax.axis_index("c")
    chip = 2 * mx + my
    chip_idx = jnp.reshape(chip, (1,)).astype(jnp.int32)
    core_idx = jnp.reshape(mc, (1,)).astype(jnp.int32)
    place_idx = jnp.stack([chip, mc]).astype(jnp.int32)

    win = [_place_own_shard(ab_w_in if L % 2 == 0 else sg_w_in, L // 2, chip_idx) for L in range(4)]
    wout = [_place_own_shard(ab_w_out if L % 2 == 0 else sg_w_out, L // 2, chip_idx) for L in range(4)]
    k_in, k_out = d, wout[0].shape[1]

    def gather_plan(ici=(), pass_on=()):
        arrays, copies, names = [], [], []
        for kind, L, only in ici:
            arrays.append(win[L] if kind == "in" else wout[L])
            names.append((kind, L))
            copies += _gather_ici(len(arrays) - 1, k_in if kind == "in" else k_out, only)
        for kind, L in pass_on:
            arrays.append(win[L] if kind == "in" else wout[L])
            names.append((kind, L))
            copies += _gather_pass_on(len(arrays) - 1, k_in if kind == "in" else k_out)
        return _Plan(tuple(arrays), tuple(copies)), names

    def absorb(plan_and_names, updated):
        _, names = plan_and_names
        for pos, (kind, L) in enumerate(names):
            if kind == "in":
                win[L] = updated[pos]
            else:
                wout[L] = updated[pos]

    win[0] = _comm_stages("gather_first_w_in", [win[0]], [_gather_ici(0, k_in), _gather_pass_on(0, k_in)],
                          chained=True)[0]

    small_local = [c[0], ab_conv_w, sg_norm_g, sg_ln_g, sg_ln_b]
    small_shapes = [a.shape for a in small_local]
    payload, small_off = _pack(small_local)
    gathered = _all_to_all(jnp.broadcast_to(payload[None], (N_DEV,) + payload.shape), "gather_small")
    per_dev = [_unpack(gathered[b], small_off, small_shapes) for b in range(N_DEV)]
    c_all = jnp.stack([per_dev[b][0] for b in range(N_DEV)])

    def from_chips(idx, axis):
        return jnp.concatenate([per_dev[2 * q][idx] for q in range(N_CHIPS)], axis=axis)

    conv_w_full = from_chips(1, 2)
    sg_norm_g_full = from_chips(2, 1)
    sg_ln_g_full = from_chips(3, 1)
    sg_ln_b_full = from_chips(4, 1)

    ab_b_local = lax.dynamic_slice_in_dim(ab_b_mod, chip * mod_l, mod_l, axis=1)
    mod_rows = []
    for layer in range(4):
        i = layer // 2
        w_mod, bias = (ab_w_mod, ab_b_local) if layer % 2 == 0 else (sg_w_mod, sg_b_mod)
        mod_rows.append(_mod_fwd(c_all, w_mod, bias[i:i + 1], i))
    mod_local = jnp.stack(mod_rows, axis=1)
    mod_recv = _all_to_all(mod_local.reshape(N_DEV, -1, LANES), "exchange_mod")
    mod_recv = mod_recv.reshape(N_DEV, 4, mod_l)
    mod_full = jnp.concatenate([mod_recv[2 * q] for q in range(N_CHIPS)], axis=-1)
    shifts = [mod_full[l:l + 1, :d] for l in range(4)]
    scales = [mod_full[l:l + 1, d:2 * d] for l in range(4)]
    gates = [mod_full[l:l + 1, 2 * d:] for l in range(4)]

    cos, sin = _rope_tables(s)
    w_s16 = sg_w_s.astype(BF16)
    b_s3 = sg_b_s[..., None]

    near, far, everyone = (0, 1), (2,), (0, 1, 2)
    fwd_comm = {
        ("in_proj", 0): ([("in", 1, near), ("out", 0, everyone), ("out", 2, everyone)], []),
        ("attn", 0): ([("in", 1, far), ("out", 1, everyone), ("in", 2, near)], [("out", 0)]),
        ("out_proj", 0): ([], [("in", 1), ("out", 1)]),
        ("in_proj", 1): ([("in", 2, far)], []),
        ("out_proj", 1): ([], [("in", 2), ("out", 2)]),
        ("in_proj", 2): ([("in", 3, everyone)], []),
        ("attn", 2): ([("out", 3, everyone)], []),
        ("out_proj", 2): ([], [("in", 3), ("out", 3)]),
    }

    def carried(key, fn, *args):
        if key not in fwd_comm:
            return fn(*args)
        pn = gather_plan(*fwd_comm[key])
        res, updated = fn(*args, plan=pn[0])
        absorb(pn, updated)
        return res

    def norm_params(layer):
        g = ab_norm_g if layer % 2 == 0 else sg_norm_g_full
        return g[layer // 2:layer // 2 + 1], scales[layer], shifts[layer]

    saved = []
    xs = x0
    h = _prenorm(xs, *norm_params(0))
    for layer in range(4):
        i = layer // 2
        next_norm = norm_params(layer + 1) if layer < 3 else None
        if layer % 2 == 0:
            proj = carried(("in_proj", layer), _in_proj, h, win[layer])
            attn, lse = carried(("attn", layer), _attn_fwd, proj, cos, sin, aw)
            y = _ab_mix(attn, proj, conv_w_full[i], aw)
            res = carried(("out_proj", layer), _out_proj_residual, y, wout[layer].reshape(-1, d), xs,
                          gates[layer], next_norm)
            saved.append((xs, h, proj, y, res[1], attn, lse))
        else:
            uvz = carried(("in_proj", layer), _in_proj, h, win[layer])
            y = _sgu_fwd(uvz, sg_ln_g_full[i:i + 1], sg_ln_b_full[i:i + 1], w_s16[i], b_s3[i], cw)
            res = carried(("out_proj", layer), _out_proj_residual, y, wout[layer].reshape(-1, d), xs,
                          gates[layer], next_norm)
            saved.append((xs, h, uvz, y, res[1]))
        xs = res[0]
        h = res[2] if next_norm is not None else None

    loss11, dx, d_final_g, dout, dgate = _final_loss(xs, target, final_norm_g[None], saved[3][4], gates[3])
    loss = lax.psum(loss11[0, 0], ("x", "y", "c"))

    reduced = {"in": [lax.empty((2,) + w.shape[1:], F32) for w in (ab_w_in, sg_w_in)],
               "out": [lax.empty((2,) + w.shape[1:], F32) for w in (ab_w_out, sg_w_out)]}
    grads = {}
    stage = {}
    k_of = {"in": k_in, "out": k_out}

    def swap_plan(which):
        arrays, copies = [], []
        for kind, L in which:
            g = grads[kind, L]
            arrays += [g, jax.ShapeDtypeStruct((N_CHIPS, g.shape[1] // 2, g.shape[2]), BF16)]
            copies += _reduce_swap(len(arrays) - 2, len(arrays) - 1, k_of[kind])
        return _Plan(tuple(arrays), tuple(copies))

    def after_swap(which, updated):
        for pos, (kind, L) in enumerate(which):
            stage[kind, L] = _add_sibling(grads[kind, L], updated[2 * pos + 1], core_idx)

    def ici_plan(pieces):
        arrays, copies = [], []
        for kind, L, only in pieces:
            cs = stage[kind, L]
            arrays += [cs, stage.get((kind, L, "recv"), jax.ShapeDtypeStruct((3,) + cs.shape[1:], BF16))]
            copies += _reduce_ici(len(arrays) - 2, len(arrays) - 1, only)
        return _Plan(tuple(arrays), tuple(copies))

    def after_ici(pieces, updated):
        for pos, (kind, L, _) in enumerate(pieces):
            stage[kind, L, "recv"] = updated[2 * pos + 1]

    def sum_layer(L):
        for kind in ("in", "out"):
            reduced[kind][L % 2] = _sum_chips(stage[kind, L], stage[kind, L, "recv"], reduced[kind][L % 2],
                                              L // 2, place_idx)

    def share_plan(L):
        arrays = (reduced["in"][L % 2], reduced["out"][L % 2])
        copies = _reduce_share(0, L // 2, k_in) + _reduce_share(1, L // 2, k_out)
        return _Plan(arrays, tuple(copies))

    def after_share(L, updated):
        reduced["in"][L % 2], reduced["out"][L % 2] = updated[0], updated[1]

    all_chips = (0, 1, 2)
    dmods = [None] * 4
    d_ab_norm_g, d_sg_norm_g = [None, None], [None, None]
    d_conv_w, d_ln_g, d_ln_b, d_w_s, d_b_s = ([None, None] for _ in range(5))
    for layer in reversed(range(4)):
        i = layer // 2
        prev = layer + 1
        busy = prev < 4
        if layer % 2 == 0:
            xs, h, proj, y, out, attn, lse = saved[layer]
        else:
            xs, h, uvz, y, out = saved[layer]
        below = (saved[layer - 1][4], gates[layer - 1]) if layer > 0 else None
        w2 = wout[layer].reshape(-1, d)
        grads["out", layer] = _out_proj_bwd_w(y, dout).reshape(N_CHIPS, -1, d)
        if busy:
            swapped = [("in", prev), ("out", prev)] + ([("out", 0)] if layer == 0 else [])
            dy, updated = _out_proj_bwd_act(dout, w2, plan=swap_plan(swapped))
            after_swap(swapped, updated)
        else:
            dy = _out_proj_bwd_act(dout, w2)
        if layer % 2 == 0:
            if busy:
                pieces = [("in", prev, all_chips), ("out", prev, all_chips)]
                pieces += [("out", 0, all_chips)] if layer == 0 else []
                dact, updated = _attn_bwd(proj, cos, sin, dy, attn, lse, aw, plan=ici_plan(pieces))
                after_ici(pieces, updated)
                sum_layer(prev)
            else:
                dact = _attn_bwd(proj, cos, sin, dy, attn, lse, aw)
            dact, d_conv_w[i] = _conv_bwd(dact, dy, proj, conv_w_full[i], aw)
            if layer == 0:
                gating, gating_off = _pack([jnp.stack(d_w_s), jnp.stack(d_b_s)], PACKED_ROW_BLOCK)
                slots = _place_own_slot(gating, jnp.reshape(2 * chip + mc, (1,)).astype(jnp.int32))
                sharing = share_plan(prev)
                plan = _merge_plans(sharing, _Plan((gating, slots), tuple(_broadcast_copies(0, 1))))
                grads["in", 0], updated = _in_proj_bwd_w(h, dact, win[0].shape[-1], plan=plan)
                after_share(prev, updated)
                all_gating = updated[len(sharing.arrays) + 1]
                plan = swap_plan([("in", 0)])
                after_swap([("in", 0)], _comm_stages("grads_to_sibling", plan.arrays, [plan.copies]))
                pieces = [("in", 0, all_chips)]
                dh, updated = _in_proj_bwd_act(dact, win[0], plan=ici_plan(pieces))
                after_ici(pieces, updated)
            elif busy:
                dh, updated = _in_proj_bwd_act(dact, win[layer], plan=share_plan(prev))
                after_share(prev, updated)
                grads["in", layer] = _in_proj_bwd_w(h, dact, win[layer].shape[-1])
            else:
                dh = _in_proj_bwd_act(dact, win[layer])
                grads["in", layer] = _in_proj_bwd_w(h, dact, win[layer].shape[-1])
            norm_g = ab_norm_g[i:i + 1]
        else:
            sgu_args = (uvz, dy, sg_ln_g_full[i:i + 1], sg_ln_b_full[i:i + 1], w_s16[i], b_s3[i], cw)
            if busy:
                pieces = [("in", prev, (0, 1))]
                res, updated = _sgu_bwd(*sgu_args, plan=ici_plan(pieces))
                after_ici(pieces, updated)
            else:
                res = _sgu_bwd(*sgu_args)
            dact, d_w_s[i], db_wide, d_ln_g[i], d_ln_b[i] = res
            d_b_s[i] = db_wide[:, :, 0]
            if busy:
                pieces = [("in", prev, (2,)), ("out", prev, all_chips)]
                dh, updated = _in_proj_bwd_act(dact, win[layer], plan=ici_plan(pieces))
                after_ici(pieces, updated)
                sum_layer(prev)
                grads["in", layer], updated = _in_proj_bwd_w(h, dact, win[layer].shape[-1], plan=share_plan(prev))
                after_share(prev, updated)
            else:
                dh = _in_proj_bwd_act(dact, win[layer])
                grads["in", layer] = _in_proj_bwd_w(h, dact, win[layer].shape[-1])
            norm_g = sg_norm_g_full[i:i + 1]
        res = _prenorm_bwd(xs, dh, dx, norm_g, scales[layer], below)
        dx, dshift, dscale, d_norm_g = res[:4]
        (d_ab_norm_g if layer % 2 == 0 else d_sg_norm_g)[i] = d_norm_g
        dmods[layer] = jnp.concatenate([dshift, dscale, dgate], axis=1)
        if below:
            dout, dgate = res[4:]
    grad_x = dx[None]

    partial_list = [jnp.concatenate(dmods, axis=0),
                    jnp.concatenate(d_ab_norm_g, axis=0), jnp.concatenate(d_sg_norm_g, axis=0), d_final_g[0],
                    jnp.stack(d_conv_w), jnp.concatenate(d_ln_g, axis=0), jnp.concatenate(d_ln_b, axis=0)]
    partial_shapes = [a.shape for a in partial_list]
    partials, part_off = _pack(partial_list, PACKED_ROW_BLOCK)
    all_partials = _all_to_all(jnp.broadcast_to(partials[None], (N_DEV,) + partials.shape), "gather_partials")
    sum_layer(0)
    summed_packed, updated = _sum_devices(all_partials, plan=share_plan(0))
    after_share(0, updated)
    summed = _unpack(summed_packed, part_off, partial_shapes)
    g_mod_bias, g_ab_norm_g, g_sg_norm_g_full, g_final_g, g_conv_full, g_ln_g_full, g_ln_b_full = summed
    g_w_s, g_b_s = _unpack(_sum_devices(all_gating), gating_off, [sg_w_s.shape, sg_b_s.shape])
    dm_all = jnp.stack([_unpack(all_partials[b], part_off[:1], partial_shapes[:1])[0] for b in range(N_DEV)])
    dm_local = lax.dynamic_slice_in_dim(dm_all, chip * mod_l, mod_l, axis=2)

    def chip_cols(a, axis):
        width = a.shape[axis] // N_CHIPS
        return lax.dynamic_slice_in_dim(a, chip * width, width, axis=axis)

    g_ab_b_mod = jnp.stack([g_mod_bias[0], g_mod_bias[2]])
    g_sg_b_mod = chip_cols(jnp.stack([g_mod_bias[1], g_mod_bias[3]]), 1)
    g_ab_w_mod = _mod_bwd_w(c_all, jnp.stack([dm_local[:, 0], dm_local[:, 2]]))
    g_sg_w_mod = _mod_bwd_w(c_all, jnp.stack([dm_local[:, 1], dm_local[:, 3]]))
    g_conv = chip_cols(g_conv_full, 2)
    g_sg_norm_g = chip_cols(g_sg_norm_g_full, 1)
    g_ln_g = chip_cols(g_ln_g_full, 1)
    g_ln_b = chip_cols(g_ln_b_full, 1)

    def step_big(w, g, m, v, also_grad=False):
        res = _adamw(_flat_rows(w), _flat_rows(g), _flat_rows(m), _flat_rows(v), also_grad)
        return tuple(a.reshape(w.shape) for a in res)

    big_out = {
        "ab_w_mod": step_big(ab_w_mod, g_ab_w_mod, m_ab_w_mod, v_ab_w_mod),
        "ab_w_in": step_big(ab_w_in, reduced["in"][0], m_ab_w_in, v_ab_w_in, True),
        "ab_w_out": step_big(ab_w_out, reduced["out"][0], m_ab_w_out, v_ab_w_out, True),
        "sg_w_mod": step_big(sg_w_mod, g_sg_w_mod, m_sg_w_mod, v_sg_w_mod),
        "sg_w_in": step_big(sg_w_in, reduced["in"][1], m_sg_w_in, v_sg_w_in, True),
        "sg_w_out": step_big(sg_w_out, reduced["out"][1], m_sg_w_out, v_sg_w_out, True),
    }
    g_ab_w_in, g_ab_w_out = big_out["ab_w_in"][3], big_out["ab_w_out"][3]
    g_sg_w_in, g_sg_w_out = big_out["sg_w_in"][3], big_out["sg_w_out"][3]
    small_names = ["ab_norm_g", "ab_b_mod", "ab_conv_w", "sg_norm_g", "sg_b_mod", "sg_ln_g", "sg_ln_b",
                   "sg_w_s", "sg_b_s", "final_norm_g"]
    small_w = [ab_norm_g, ab_b_mod, ab_conv_w, sg_norm_g, sg_b_mod, sg_ln_g, sg_ln_b, sg_w_s, sg_b_s, final_norm_g]
    small_g = [g_ab_norm_g, g_ab_b_mod, g_conv, g_sg_norm_g, g_sg_b_mod, g_ln_g, g_ln_b, g_w_s, g_b_s, g_final_g]
    small_m = [m_ab_norm_g, m_ab_b_mod, m_ab_conv_w, m_sg_norm_g, m_sg_b_mod, m_sg_ln_g, m_sg_ln_b, m_sg_w_s,
               m_sg_b_s, m_final_norm_g]
    small_v = [v_ab_norm_g, v_ab_b_mod, v_ab_conv_w, v_sg_norm_g, v_sg_b_mod, v_sg_ln_g, v_sg_ln_b, v_sg_w_s,
               v_sg_b_s, v_final_norm_g]
    shapes = [a.shape for a in small_w]
    pw, off = _pack(small_w, PACKED_ROW_BLOCK)
    pg, _ = _pack(small_g, PACKED_ROW_BLOCK)
    pm, _ = _pack(small_m, PACKED_ROW_BLOCK)
    pv, _ = _pack(small_v, PACKED_ROW_BLOCK)
    pd, pnm, pnv = _adamw(pw, pg, pm, pv)
    small_out = {}
    for name, dl, nm, nv in zip(small_names, _unpack(pd, off, shapes), _unpack(pnm, off, shapes),
                                _unpack(pnv, off, shapes)):
        small_out[name] = (dl, nm, nv)

    grad_of = {
        "ab_norm_g": g_ab_norm_g, "ab_w_mod": g_ab_w_mod, "ab_b_mod": g_ab_b_mod, "ab_w_in": g_ab_w_in,
        "ab_conv_w": g_conv, "ab_w_out": g_ab_w_out, "sg_norm_g": g_sg_norm_g, "sg_w_mod": g_sg_w_mod,
        "sg_b_mod": g_sg_b_mod, "sg_w_in": g_sg_w_in, "sg_ln_g": g_ln_g, "sg_ln_b": g_ln_b, "sg_w_s": g_w_s,
        "sg_b_s": g_b_s, "sg_w_out": g_sg_w_out, "final_norm_g": g_final_g,
    }
    order = ["ab_norm_g", "ab_w_mod", "ab_b_mod", "ab_w_in", "ab_conv_w", "ab_w_out", "sg_norm_g", "sg_w_mod",
             "sg_b_mod", "sg_w_in", "sg_ln_g", "sg_ln_b", "sg_w_s", "sg_b_s", "sg_w_out", "final_norm_g"]
    steps = {**big_out, **small_out}
    return (loss, grad_x, *[grad_of[n] for n in order], *[steps[n][0] for n in order],
            *[steps[n][1] for n in order], *[steps[n][2] for n in order])
```

```python
import math
from typing import Any, Callable, NamedTuple

import jax
import jax.numpy as jnp
import numpy as np
from jax import lax
from jax.experimental import pallas as pl
from jax.experimental.pallas import tpu as pltpu

F32 = jnp.float32
BF16 = jnp.bfloat16

HEAD_DIM = 128
RADIUS = 64
DILATIONS = (1, 4, 16)
Q_BLOCK = 256
K_WINDOW = Q_BLOCK + 2 * RADIUS
ATTN_UNROLL = 8
ROPE_THETA = 10000.0
NEG_INF = -1e30
N_GROUPS = 8
CHUNK = 128
EPS = 1e-6
CONV_ROWS = 512
CONV_HALO = 16
LANES = 128
ELEMENTWISE_BLOCK = 512 * 1024
PACKED_ROW_BLOCK = 512
N_DEV = 8
N_CHIPS = 4

ADAM_LR = 0.001
ADAM_B1 = 0.9
ADAM_B2 = 0.999
ADAM_EPS = 1e-08
ADAM_WD = 0.01
ADAM_STEP = 10

VMEM_LIMIT_V7X = 56 * 1024 * 1024

MESH_ID = pl.DeviceIdType.MESH
HBM_SPEC = pl.BlockSpec(memory_space=pltpu.HBM)

NN = (((1,), (0,)), ((), ()))
NT = (((1,), (1,)), ((), ()))
TN = (((0,), (0,)), ((), ()))


def _params(n_grid, parallel=0):
    sem = tuple(["parallel"] * parallel + ["arbitrary"] * (n_grid - parallel))
    return pltpu.CompilerParams(dimension_semantics=sem, vmem_limit_bytes=VMEM_LIMIT_V7X)


def _tile(n, prefs):
    for p in prefs:
        if n % p == 0:
            return p
    return n


def _sigmoid(z):
    return 1.0 / (1.0 + jnp.exp(-z))


def _silu(z):
    return z * _sigmoid(z)


_GELU_K = math.sqrt(2.0 / math.pi)
_GELU_C = 0.044715


def _gelu(u):
    return 0.5 * u * (1.0 + jnp.tanh(_GELU_K * (u + _GELU_C * u * u * u)))


def _gelu_and_grad(u):
    t = jnp.tanh(_GELU_K * (u + _GELU_C * u * u * u))
    half = 0.5 * (1.0 + t)
    return u * half, half + 0.5 * u * (1.0 - t * t) * _GELU_K * (1.0 + 3.0 * _GELU_C * u * u)


def _silu_and_grad(z):
    s = _sigmoid(z)
    return z * s, s * (1.0 + z * (1.0 - s))


class _Place(NamedTuple):
    x: Any
    y: Any
    c: Any
    chip: Any


def _my_place():
    mx, my, mc = lax.axis_index("x"), lax.axis_index("y"), lax.axis_index("c")
    return _Place(mx, my, mc, 2 * mx + my)


def _other_chips(p):
    return [(1 - p.x, p.y), (p.x, 1 - p.y), (1 - p.x, 1 - p.y)]


class _Copy(NamedTuple):
    src: int
    src_at: Callable
    dst: int
    dst_at: Callable
    peer: Callable


class _Plan(NamedTuple):
    arrays: tuple
    copies: tuple


def _view(ref, index):
    return ref if index is None else ref.at[index]


def _plan_io(plan):
    ins = [k for k, a in enumerate(plan.arrays) if not isinstance(a, jax.ShapeDtypeStruct)]
    written = sorted({cp.dst for cp in plan.copies})
    return ins, written


def _descriptors(plan, in_refs, out_refs, send_sems, recv_sems):
    ins, written = _plan_io(plan)
    place = _my_place()
    return [
        pltpu.make_async_remote_copy(
            src_ref=_view(in_refs[ins.index(cp.src)], cp.src_at(place)),
            dst_ref=_view(out_refs[written.index(cp.dst)], cp.dst_at(place)),
            send_sem=send_sems.at[k], recv_sem=recv_sems.at[k],
            device_id=cp.peer(place), device_id_type=MESH_ID)
        for k, cp in enumerate(plan.copies)]


def _plan_operands(plan, n_in, n_out):
    ins, written = _plan_io(plan)
    operands = [plan.arrays[k] for k in ins]
    out_shape = [jax.ShapeDtypeStruct(plan.arrays[k].shape, plan.arrays[k].dtype) for k in written]
    aliases = {n_in + ins.index(k): n_out + pos for pos, k in enumerate(written) if k in ins}
    n = len(plan.copies)
    sems = [pltpu.SemaphoreType.DMA((n,)), pltpu.SemaphoreType.DMA((n,))]
    return operands, out_shape, aliases, sems, written


def _call(body, *, name, grid, in_specs, out_specs, out_shape, operands, scratch_shapes=(), aliases=None,
          parallel=0, plan=None):
    single = not isinstance(out_shape, (list, tuple))
    out_shape = [out_shape] if single else list(out_shape)
    out_specs = [out_specs] if single else list(out_specs)
    if plan is None:
        res = pl.pallas_call(
            body, name=name, grid=grid, in_specs=list(in_specs), out_specs=out_specs, out_shape=out_shape,
            scratch_shapes=list(scratch_shapes), input_output_aliases=aliases or {},
            compiler_params=_params(len(grid), parallel=parallel),
        )(*operands)
        return res[0] if single else res

    n_in, n_out, n_scr = len(operands), len(out_shape), len(scratch_shapes)
    p_operands, p_out_shape, p_aliases, sems, written = _plan_operands(plan, n_in, n_out)
    n_pin, n_pout = len(p_operands), len(p_out_shape)

    def wrapped(*refs):
        ins = refs[:n_in]
        p_in = refs[n_in:n_in + n_pin]
        outs = refs[n_in + n_pin:n_in + n_pin + n_out]
        p_out = refs[n_in + n_pin + n_out:n_in + n_pin + n_out + n_pout]
        scratch = refs[n_in + n_pin + n_out + n_pout:n_in + n_pin + n_out + n_pout + n_scr]
        send_sems, recv_sems = refs[-2:]
        ids = [pl.program_id(a) for a in range(len(grid))]
        first = ids[0] == 0
        last = ids[0] == grid[0] - 1
        for a in range(1, len(grid)):
            first = jnp.logical_and(first, ids[a] == 0)
            last = jnp.logical_and(last, ids[a] == grid[a] - 1)

        @pl.when(first)
        def _():
            for cp in _descriptors(plan, p_in, p_out, send_sems, recv_sems):
                cp.start()

        body(*ins, *outs, *scratch)

        @pl.when(last)
        def _():
            for cp in _descriptors(plan, p_in, p_out, send_sems, recv_sems):
                cp.wait()

    res = pl.pallas_call(
        wrapped, name=name, grid=grid,
        in_specs=list(in_specs) + [HBM_SPEC] * n_pin,
        out_specs=out_specs + [HBM_SPEC] * n_pout,
        out_shape=out_shape + p_out_shape,
        scratch_shapes=list(scratch_shapes) + sems,
        input_output_aliases={**(aliases or {}), **p_aliases},
        compiler_params=_params(len(grid)),
    )(*operands, *p_operands)
    outs = res[0] if single else res[:n_out]
    return outs, dict(zip(written, res[n_out:]))


def _comm_stages(name, arrays, stages, chained=False):
    plan = _Plan(tuple(arrays), tuple(cp for st in stages for cp in st))
    p_operands, p_out_shape, p_aliases, sems, written = _plan_operands(plan, 0, 0)
    n_pin = len(p_operands)

    def body(*refs):
        p_in = refs[:n_pin]
        p_out = refs[n_pin:n_pin + len(written)]
        send_sems, recv_sems = refs[-2:]
        all_copies = _descriptors(plan, p_in, p_out, send_sems, recv_sems)
        if chained:
            n = len(stages[0])
            for cp in all_copies[:n]:
                cp.start()
            for k in range(n):
                all_copies[k].wait()
                all_copies[n + k].start()
            for cp in all_copies[n:]:
                cp.wait()
            return
        base = 0
        for st in stages:
            for cp in all_copies[base:base + len(st)]:
                cp.start()
            for cp in all_copies[base:base + len(st)]:
                cp.wait()
            base += len(st)

    res = pl.pallas_call(
        body, name=name, in_specs=[HBM_SPEC] * n_pin, out_specs=[HBM_SPEC] * len(written),
        out_shape=p_out_shape, scratch_shapes=sems, input_output_aliases=p_aliases,
    )(*p_operands)
    return dict(zip(written, res))


def _half_rows(k, c):
    return pl.ds(c * (k // 2), k // 2)


def _gather_ici(a, k, only=(0, 1, 2)):
    own = lambda p: (p.chip, _half_rows(k, p.c))
    return [_Copy(a, own, a, own, lambda p, q=q: (*_other_chips(p)[q], p.c)) for q in only]


def _gather_pass_on(a, k):
    def at(q):
        def index(p):
            px, py = _other_chips(p)[q]
            return (2 * px + py, _half_rows(k, p.c))
        return index
    return [_Copy(a, at(q), a, at(q), lambda p: (p.x, p.y, 1 - p.c)) for q in range(3)]


def _reduce_swap(src, dst, k):
    return [_Copy(src, lambda p: (pl.ds(0, N_CHIPS), _half_rows(k, 1 - p.c)), dst, lambda p: None,
                  lambda p: (p.x, p.y, 1 - p.c))]


def _reduce_ici(src, dst, only=(0, 1, 2)):
    def slab(q):
        def index(p):
            px, py = _other_chips(p)[q]
            return 2 * px + py
        return index
    return [_Copy(src, slab(q), dst, lambda p, q=q: q, lambda p, q=q: (*_other_chips(p)[q], p.c)) for q in only]


def _reduce_share(a, layer, k):
    at = lambda p: (layer, _half_rows(k, p.c))
    return [_Copy(a, at, a, at, lambda p: (p.x, p.y, 1 - p.c))]


def _merge_plans(a, b):
    off = len(a.arrays)
    moved = tuple(cp._replace(src=cp.src + off, dst=cp.dst + off) for cp in b.copies)
    return _Plan(a.arrays + b.arrays, a.copies + moved)


def _broadcast_copies(src, dst):
    me = lambda p: 2 * p.chip + p.c

    def peer(k):
        return lambda p: (1 - p.x if (k >> 2) & 1 else p.x, 1 - p.y if (k >> 1) & 1 else p.y,
                          1 - p.c if k & 1 else p.c)

    return [_Copy(src, lambda p: None, dst, me, peer(k)) for k in range(1, N_DEV)]


def _place_own_slot(x, me_idx):
    r = x.shape[0]
    tr = _tile(r, (512, 256, 128, 64, 32, 16, 8))

    def body(me_ref, x_ref, o_ref):
        o_ref[...] = x_ref[...]

    return pl.pallas_call(
        body, name="place_own_slot",
        grid_spec=pltpu.PrefetchScalarGridSpec(
            num_scalar_prefetch=1, grid=(r // tr,),
            in_specs=[pl.BlockSpec((tr, LANES), lambda i, me: (i, 0))],
            out_specs=pl.BlockSpec((None, tr, LANES), lambda i, me: (me[0], i, 0))),
        out_shape=jax.ShapeDtypeStruct((N_DEV, r, LANES), F32),
        compiler_params=_params(1, parallel=1),
    )(me_idx, x)


def _all_to_all(x, name):
    def body(x_ref, y_ref, send_sems, recv_sems, own_sem):
        p = _my_place()
        me = 2 * p.chip + p.c
        own = pltpu.make_async_copy(x_ref.at[me], y_ref.at[me], own_sem)
        own.start()
        copies = []
        for k in range(1, N_DEV):
            px = 1 - p.x if (k >> 2) & 1 else p.x
            py = 1 - p.y if (k >> 1) & 1 else p.y
            pc = 1 - p.c if k & 1 else p.c
            peer = 4 * px + 2 * py + pc
            cp = pltpu.make_async_remote_copy(
                src_ref=x_ref.at[peer], dst_ref=y_ref.at[me],
                send_sem=send_sems.at[k - 1], recv_sem=recv_sems.at[k - 1],
                device_id=(px, py, pc), device_id_type=MESH_ID)
            cp.start()
            copies.append(cp)
        for cp in copies:
            cp.wait()
        own.wait()

    return pl.pallas_call(
        body, name=name,
        out_shape=jax.ShapeDtypeStruct(x.shape, x.dtype),
        in_specs=[HBM_SPEC], out_specs=HBM_SPEC,
        scratch_shapes=[pltpu.SemaphoreType.DMA((N_DEV - 1,)), pltpu.SemaphoreType.DMA((N_DEV - 1,)),
                        pltpu.SemaphoreType.DMA],
    )(x)


def _place_own_shard(w, layer, chip_idx):
    _, k, n = w.shape
    tr = _tile(k, (512, 256, 128))

    def body(c_ref, w_ref, g_ref):
        g_ref[...] = w_ref[...].astype(BF16)

    return pl.pallas_call(
        body, name="place_own_shard",
        grid_spec=pltpu.PrefetchScalarGridSpec(
            num_scalar_prefetch=1, grid=(k // tr,),
            in_specs=[pl.BlockSpec((None, tr, n), lambda r, c: (layer, r, 0))],
            out_specs=pl.BlockSpec((None, tr, n), lambda r, c: (c[0], r, 0))),
        out_shape=jax.ShapeDtypeStruct((N_CHIPS, k, n), BF16),
        compiler_params=_params(1, parallel=1),
    )(chip_idx, w)


def _matmul(name, operands, in_specs, grid, dims, out_shape, out_specs, epilogue, a_prologue=None,
            aliases=None, plan=None):
    n_in = len(operands)

    def body(*refs):
        a = refs[0][...]
        if a_prologue is not None:
            a = a_prologue(a)
        acc = lax.dot_general(a.astype(BF16), refs[1][...].astype(BF16), dims, preferred_element_type=F32)
        epilogue(acc, refs[2:n_in], refs[n_in:])

    return _call(body, name=name, grid=grid, in_specs=in_specs, out_specs=out_specs, out_shape=out_shape,
                 operands=operands, aliases=aliases, parallel=2, plan=plan)


def _store_cast(acc, extra, outs):
    outs[0][...] = acc.astype(outs[0].dtype)


def _in_proj(h, w, plan=None):
    s, d = h.shape
    nl = w.shape[-1]
    tm = _tile(s, (1024, 512, 256))
    tn = _tile(nl, (1024, 768, 512, 384, 256, 128))
    per = nl // tn
    return _matmul(
        "in_proj", (h, w),
        [pl.BlockSpec((tm, d), lambda i, j: (i, 0)),
         pl.BlockSpec((None, d, tn), lambda i, j: (j // per, 0, j % per))],
        (s // tm, N_CHIPS * per), NN,
        jax.ShapeDtypeStruct((s, N_CHIPS * nl), BF16),
        pl.BlockSpec((tm, tn), lambda i, j: (i, j)), _store_cast, plan=plan)


def _modulated_norm(xv, g, scale, shift):
    rstd = lax.rsqrt(jnp.mean(xv * xv, axis=-1, keepdims=True) + EPS)
    return ((xv * rstd) * g * (1.0 + scale) + shift).astype(BF16)


def _out_proj_residual(y, w2, x, gate, next_norm, plan=None):
    s, wdt = y.shape
    d = w2.shape[-1]
    tm = _tile(s, (256, 128))

    def epilogue(acc, extra, outs):
        x_new = extra[0][...] + extra[1][...] * acc
        outs[0][...] = x_new
        outs[1][...] = acc.astype(BF16)
        if next_norm is not None:
            outs[2][...] = _modulated_norm(x_new, extra[2][...], extra[3][...], extra[4][...])

    rows = pl.BlockSpec((tm, d), lambda i, j: (i, 0))
    vec = pl.BlockSpec((1, d), lambda i, j: (0, 0))
    n_vec = 1 + (3 if next_norm is not None else 0)
    n_act = 1 + (1 if next_norm is not None else 0)
    return _matmul(
        "out_proj", (y, w2, x, gate) + (tuple(next_norm) if next_norm is not None else ()),
        [pl.BlockSpec((tm, wdt), lambda i, j: (i, 0)),
         pl.BlockSpec((wdt, d), lambda i, j: (0, 0), pipeline_mode=pl.Buffered(1)),
         rows] + [vec] * n_vec,
        (s // tm, 1), NN,
        [jax.ShapeDtypeStruct((s, d), F32)] + [jax.ShapeDtypeStruct((s, d), BF16)] * n_act,
        [rows] * (1 + n_act), epilogue, plan=plan)


def _out_proj_bwd_act(dout, w2, plan=None):
    s, d = dout.shape
    wdt = w2.shape[0]
    tm = _tile(s, (1024, 512, 256))
    tn = _tile(wdt, (1024, 512, 256, 128))
    return _matmul(
        "out_proj_dy", (dout, w2),
        [pl.BlockSpec((tm, d), lambda i, j: (i, 0)),
         pl.BlockSpec((tn, d), lambda i, j: (j, 0))],
        (s // tm, wdt // tn), NT,
        jax.ShapeDtypeStruct((s, wdt), BF16),
        pl.BlockSpec((tm, tn), lambda i, j: (i, j)), _store_cast, plan=plan)


def _out_proj_bwd_w(y, dout, plan=None):
    s, wdt = y.shape
    d = dout.shape[1]
    tm = _tile(wdt, (1024, 512, 256, 128))
    tn = _tile(d, (1024, 512, 256, 128))
    return _matmul(
        "out_proj_dw", (y, dout),
        [pl.BlockSpec((s, tm), lambda i, j: (0, i)),
         pl.BlockSpec((s, tn), lambda i, j: (0, j))],
        (wdt // tm, d // tn), TN,
        jax.ShapeDtypeStruct((wdt, d), BF16),
        pl.BlockSpec((tm, tn), lambda i, j: (i, j)), _store_cast, plan=plan)


def _in_proj_bwd_act(dproj, w, plan=None):
    s, n_all = dproj.shape
    d, nl = w.shape[1], w.shape[2]
    tm = _tile(s, (512, 256))
    tn = _tile(d, (512, 256, 128))

    def body(a_ref, w_ref, o_ref):
        acc = None
        for q in range(N_CHIPS):
            part = lax.dot_general(a_ref[:, q * nl:(q + 1) * nl], w_ref[q], NT, preferred_element_type=F32)
            acc = part if acc is None else acc + part
        o_ref[...] = acc.astype(BF16)

    return _call(
        body, name="in_proj_dh", grid=(s // tm, d // tn),
        in_specs=[pl.BlockSpec((tm, n_all), lambda i, j: (i, 0)),
                  pl.BlockSpec((N_CHIPS, tn, nl), lambda i, j: (0, j, 0))],
        out_specs=pl.BlockSpec((tm, tn), lambda i, j: (i, j)),
        out_shape=jax.ShapeDtypeStruct((s, d), BF16),
        operands=(dproj, w), parallel=2, plan=plan)


def _in_proj_bwd_w(h, dproj, nl, plan=None):
    s, d = h.shape
    tm = _tile(d, (1024, 512, 256, 128))
    tn = _tile(nl, (1024, 768, 512, 384, 256, 128))
    per = nl // tn
    return _matmul(
        "in_proj_dw", (h, dproj),
        [pl.BlockSpec((s, tm), lambda i, j: (0, i)),
         pl.BlockSpec((s, tn), lambda i, j: (0, j))],
        (d // tm, N_CHIPS * per), TN,
        jax.ShapeDtypeStruct((N_CHIPS, d, nl), BF16),
        pl.BlockSpec((None, tm, tn), lambda i, j: (j // per, i, j % per)), _store_cast, plan=plan)


def _mod_fwd(c_all, w_mod, bias, layer):
    nb, d = c_all.shape
    nl = w_mod.shape[-1]
    tn = _tile(nl, (768, 512, 384, 256, 128))

    def epilogue(acc, extra, outs):
        outs[0][...] = acc + extra[0][...]

    return _matmul(
        "mod_fwd", (c_all, w_mod, bias),
        [pl.BlockSpec((nb, d), lambda i, j: (0, 0)),
         pl.BlockSpec((None, d, tn), lambda i, j: (layer, 0, j)),
         pl.BlockSpec((1, tn), lambda i, j: (0, j))],
        (1, nl // tn), NN,
        jax.ShapeDtypeStruct((nb, nl), F32),
        pl.BlockSpec((nb, tn), lambda i, j: (0, j)), epilogue, a_prologue=_silu)


def _mod_bwd_w(c_all, dm_pair):
    nb, d = c_all.shape
    nl = dm_pair.shape[-1]
    tm = _tile(d, (1024, 512, 256, 128))
    tn = _tile(nl, (768, 512, 384, 256, 128))

    def body(c_ref, dm_ref, o_ref):
        o_ref[...] = lax.dot_general(_silu(c_ref[...]).astype(BF16), dm_ref[...].astype(BF16), TN,
                                     preferred_element_type=F32)

    return _call(
        body, name="mod_dw", grid=(2, d // tm, nl // tn),
        in_specs=[pl.BlockSpec((nb, tm), lambda l, i, j: (0, i)),
                  pl.BlockSpec((None, nb, tn), lambda l, i, j: (l, 0, j))],
        out_specs=pl.BlockSpec((None, tm, tn), lambda l, i, j: (l, i, j)),
        out_shape=jax.ShapeDtypeStruct((2, d, nl), F32),
        operands=(c_all, dm_pair), parallel=3)


def _rows_call(name, body, operands, in_specs, out_shape, out_specs, n_tiles):
    return pl.pallas_call(
        body, name=name, grid=(n_tiles,), in_specs=in_specs, out_specs=out_specs, out_shape=out_shape,
        compiler_params=_params(1),
    )(*operands)


def _row_spec(tr, width):
    return pl.BlockSpec((tr, width), lambda i: (i, 0))


def _vec_spec(width):
    return pl.BlockSpec((1, width), lambda i: (0, 0))


def _accumulate(ref, val):
    first = pl.program_id(0) == 0

    @pl.when(first)
    def _():
        ref[...] = val

    @pl.when(jnp.logical_not(first))
    def _():
        ref[...] += val


def _prenorm(x, g, scale, shift):
    s, d = x.shape
    tr = _tile(s, (256, 128))

    def body(x_ref, g_ref, sc_ref, sh_ref, h_ref):
        h_ref[...] = _modulated_norm(x_ref[...], g_ref[...], sc_ref[...], sh_ref[...])

    return _rows_call("prenorm", body, (x, g, scale, shift),
                      [_row_spec(tr, d), _vec_spec(d), _vec_spec(d), _vec_spec(d)],
                      jax.ShapeDtypeStruct((s, d), BF16), _row_spec(tr, d), s // tr)


def _gate_grads(dxv, out_ref, gate_ref, dout_ref, dgate_ref):
    dout_ref[...] = (gate_ref[...] * dxv).astype(BF16)
    _accumulate(dgate_ref, jnp.sum(dxv * out_ref[...].astype(F32), axis=0, keepdims=True))


def _prenorm_bwd(x, dh, dres, g, scale, below=None):
    s, d = x.shape
    tr = _tile(s, (512, 256, 128))

    def body(x_ref, dh_ref, dres_ref, g_ref, sc_ref, *rest):
        dx_ref, dshift_ref, dscale_ref, dg_ref = rest[-6:-2] if below else rest
        xv = x_ref[...]
        dhv = dh_ref[...].astype(F32)
        rstd = lax.rsqrt(jnp.mean(xv * xv, axis=-1, keepdims=True) + EPS)
        xhat = xv * rstd
        gv = g_ref[...]
        one_sc = 1.0 + sc_ref[...]
        dxhat = dhv * gv * one_sc
        dxv = dres_ref[...] + rstd * (dxhat - xhat * jnp.mean(dxhat * xhat, axis=-1, keepdims=True))
        dx_ref[...] = dxv
        _accumulate(dshift_ref, jnp.sum(dhv, axis=0, keepdims=True))
        _accumulate(dscale_ref, jnp.sum(dhv * xhat * gv, axis=0, keepdims=True))
        _accumulate(dg_ref, jnp.sum(dhv * xhat * one_sc, axis=0, keepdims=True))
        if below:
            _gate_grads(dxv, rest[0], rest[1], rest[-2], rest[-1])

    vec = jax.ShapeDtypeStruct((1, d), F32)
    operands = (x, dh, dres, g, scale) + (tuple(below) if below else ())
    in_specs = [_row_spec(tr, d), _row_spec(tr, d), _row_spec(tr, d), _vec_spec(d), _vec_spec(d)]
    out_shape = [jax.ShapeDtypeStruct((s, d), F32), vec, vec, vec]
    out_specs = [_row_spec(tr, d), _vec_spec(d), _vec_spec(d), _vec_spec(d)]
    if below:
        in_specs += [_row_spec(tr, d), _vec_spec(d)]
        out_shape += [jax.ShapeDtypeStruct((s, d), BF16), vec]
        out_specs += [_row_spec(tr, d), _vec_spec(d)]
    return _rows_call("prenorm_bwd", body, operands, in_specs, out_shape, out_specs, s // tr)


def _final_loss(x, target, g, out_below, gate_below):
    s, d = x.shape
    tr = _tile(s, (512, 256, 128))
    n_tiles = s // tr

    def body(x_ref, t_ref, g_ref, out_ref, gate_ref, loss_ref, dx_ref, dg_ref, dout_ref, dgate_ref, acc_ref):
        xv = x_ref[...]
        rstd = lax.rsqrt(jnp.mean(xv * xv, axis=-1, keepdims=True) + EPS)
        xhat = xv * rstd
        gv = g_ref[...]
        err = xhat * gv - t_ref[...]
        dy = err * (1.0 / d)
        dxhat = dy * gv
        dxv = rstd * (dxhat - xhat * jnp.mean(dxhat * xhat, axis=-1, keepdims=True))
        dx_ref[...] = dxv
        _accumulate(dg_ref, jnp.sum(dy * xhat, axis=0, keepdims=True))
        _accumulate(acc_ref, jnp.sum(err * err, axis=0, keepdims=True))
        _gate_grads(dxv, out_ref, gate_ref, dout_ref, dgate_ref)

        @pl.when(pl.program_id(0) == n_tiles - 1)
        def _():
            loss_ref[...] = (0.5 / d) * jnp.sum(acc_ref[...], axis=1, keepdims=True)

    vec = jax.ShapeDtypeStruct((1, d), F32)
    return pl.pallas_call(
        body, name="final_loss", grid=(n_tiles,),
        in_specs=[_row_spec(tr, d), _row_spec(tr, d), _vec_spec(d), _row_spec(tr, d), _vec_spec(d)],
        out_specs=[pl.BlockSpec((1, 1), lambda i: (0, 0)), _row_spec(tr, d), _vec_spec(d), _row_spec(tr, d),
                   _vec_spec(d)],
        out_shape=[jax.ShapeDtypeStruct((1, 1), F32), jax.ShapeDtypeStruct((s, d), F32), vec,
                   jax.ShapeDtypeStruct((s, d), BF16), vec],
        scratch_shapes=[pltpu.VMEM((1, d), F32)],
        compiler_params=_params(1),
    )(x, target, g, out_below, gate_below)


def _rope(t, cos, sin):
    return t * cos + pltpu.roll(t, HEAD_DIM // 2, axis=1) * sin


def _unrope(dt, cos, sin):
    return dt * cos + pltpu.roll(dt * sin, HEAD_DIM // 2, axis=1)


def _band_blocks(s, dil):
    sub = s // dil
    kw = min(K_WINDOW, sub)

    def rows(r, start, n):
        if dil == 1:
            return pl.ds(pl.multiple_of(start, RADIUS), n)
        return pl.ds(r + dil * start, n, stride=dil)

    def window(idx):
        nb = sub // Q_BLOCK
        r, b = idx // nb, idx % nb
        q0 = b * Q_BLOCK
        start = jnp.clip(q0 - RADIUS, 0, sub - kw)
        ahead = (lax.broadcasted_iota(jnp.int32, (Q_BLOCK, kw), 1)
                 - lax.broadcasted_iota(jnp.int32, (Q_BLOCK, kw), 0)) + (start - q0 + RADIUS)
        valid = lax.bitcast_convert_type(ahead, jnp.uint32) <= 2 * RADIUS
        return rows(r, q0, Q_BLOCK), rows(r, start, kw), valid

    return window


def _store_column_tiles(tiles, dst_ref, sems, col_blocks):
    rows = tiles.shape[1]
    copies = []
    for g, cb in enumerate(col_blocks):
        cols = pl.ds(pl.multiple_of(cb * LANES, LANES), LANES)
        cp = pltpu.make_async_copy(tiles.at[g], dst_ref.at[pl.ds(0, rows), cols], sems.at[g])
        cp.start()
        copies.append(cp)
    for cp in copies:
        cp.wait()


def _head_col(s, group, nh):
    return pl.BlockSpec((s, HEAD_DIM), lambda h: (0, group * nh + h), pipeline_mode=pl.Buffered(1))


def _attn_fwd(proj, cos, sin, aw, plan=None):
    s = proj.shape[0]
    nh = aw // HEAD_DIM
    scale = HEAD_DIM ** -0.5
    n_blocks = s // Q_BLOCK

    def body(q_ref, k_ref, v_ref, cos_ref, sin_ref, attn_ref, lse_ref, qf, kf, vf, acc):
        cosv, sinv = cos_ref[...], sin_ref[...]
        qf[...] = _rope(q_ref[...].astype(F32), cosv, sinv) * scale
        kf[...] = _rope(k_ref[...].astype(F32), cosv, sinv)
        vf[...] = v_ref[...].astype(F32)

        for pattern, dil in enumerate(DILATIONS):
            window = _band_blocks(s, dil)

            def block(idx, carry, window=window, first=(pattern == 0)):
                q_rows, k_rows, valid = window(idx)
                q = qf[q_rows, :].astype(BF16)
                kk = kf[k_rows, :].astype(BF16)
                vv = vf[k_rows, :].astype(BF16)
                sc = lax.dot_general(q, kk, NT, preferred_element_type=F32)
                sc = jnp.where(valid, sc, NEG_INF)
                m = jnp.max(sc, axis=1, keepdims=True)
                p = jnp.exp(sc - m)
                den = jnp.sum(p, axis=1, keepdims=True)
                o = lax.dot_general(p.astype(BF16), vv, NN, preferred_element_type=F32) / den
                lse = jnp.broadcast_to(m + jnp.log(den), (Q_BLOCK, HEAD_DIM))
                if first:
                    acc[q_rows, :] = o
                    lse_ref[q_rows, :] = lse
                else:
                    lse_old = lse_ref[q_rows, :]
                    top = jnp.maximum(lse_old, lse)
                    w_old, w_new = jnp.exp(lse_old - top), jnp.exp(lse - top)
                    tot = w_old + w_new
                    acc[q_rows, :] = (acc[q_rows, :] * w_old + o * w_new) / tot
                    lse_ref[q_rows, :] = top + jnp.log(tot)
                return carry

            lax.fori_loop(0, n_blocks, block, 0, unroll=ATTN_UNROLL)

        attn_ref[...] = acc[...].astype(BF16)

    table = pl.BlockSpec((s, HEAD_DIM), lambda h: (0, 0), pipeline_mode=pl.Buffered(1))
    out = pl.BlockSpec((s, HEAD_DIM), lambda h: (0, h))
    return _call(
        body, name="attn_fwd", grid=(nh,),
        in_specs=[_head_col(s, 0, nh), _head_col(s, 1, nh), _head_col(s, 2, nh), table, table],
        out_specs=[out, out],
        out_shape=[jax.ShapeDtypeStruct((s, aw), BF16), jax.ShapeDtypeStruct((s, aw), F32)],
        scratch_shapes=[pltpu.VMEM((s, HEAD_DIM), F32)] * 4,
        operands=(proj, proj, proj, cos, sin), parallel=1, plan=plan)


def _attn_bwd(proj, cos, sin, dy, attn, lse, aw, plan=None):
    s = proj.shape[0]
    nh = aw // HEAD_DIM
    scale = HEAD_DIM ** -0.5
    n_blocks = s // Q_BLOCK

    def body(q_ref, k_ref, v_ref, za_ref, cos_ref, sin_ref, dy_ref, attn_ref, lse_ref,
             dproj_ref, qf, kf, vf, dof, delta, dqa, dka, dva, tiles, tile_sems):
        cosv, sinv = cos_ref[...], sin_ref[...]
        qf[...] = _rope(q_ref[...].astype(F32), cosv, sinv) * scale
        kf[...] = _rope(k_ref[...].astype(F32), cosv, sinv)
        vf[...] = v_ref[...].astype(F32)
        dyv, zav, attnv = dy_ref[...].astype(F32), za_ref[...].astype(F32), attn_ref[...].astype(F32)
        silu_za, dsilu_za = _silu_and_grad(zav)
        do_all = dyv * silu_za
        dof[...] = do_all
        tiles[3] = (dyv * attnv * dsilu_za).astype(BF16)
        delta[...] = jnp.broadcast_to(jnp.sum(do_all * attnv, axis=1, keepdims=True), (s, HEAD_DIM))
        dqa[...] = jnp.zeros_like(dqa)
        dka[...] = jnp.zeros_like(dka)
        dva[...] = jnp.zeros_like(dva)

        for dil in DILATIONS:
            window = _band_blocks(s, dil)

            def block(idx, carry, window=window):
                q_rows, k_rows, valid = window(idx)
                q = qf[q_rows, :].astype(BF16)
                kk = kf[k_rows, :].astype(BF16)
                vv = vf[k_rows, :].astype(BF16)
                dov = dof[q_rows, :].astype(BF16)
                lse_q = lse_ref[q_rows, :][:, 0:1]
                delta_q = delta[q_rows, :][:, 0:1]
                sc = lax.dot_general(q, kk, NT, preferred_element_type=F32)
                p = jnp.where(valid, jnp.exp(sc - lse_q), 0.0)
                dp = lax.dot_general(dov, vv, NT, preferred_element_type=F32)
                ds = (p * (dp - delta_q)).astype(BF16)
                dqa[q_rows, :] += lax.dot_general(ds, kk, NN, preferred_element_type=F32)
                dka[k_rows, :] += lax.dot_general(ds, q, TN, preferred_element_type=F32)
                dva[k_rows, :] += lax.dot_general(p.astype(BF16), dov, TN, preferred_element_type=F32)
                return carry

            lax.fori_loop(0, n_blocks, block, 0, unroll=ATTN_UNROLL)

        tiles[0] = (_unrope(dqa[...], cosv, sinv) * scale).astype(BF16)
        tiles[1] = _unrope(dka[...], cosv, sinv).astype(BF16)
        tiles[2] = dva[...].astype(BF16)
        _store_column_tiles(tiles, dproj_ref, tile_sems, [g * nh + pl.program_id(0) for g in range(4)])

    own = pl.BlockSpec((s, HEAD_DIM), lambda h: (0, h), pipeline_mode=pl.Buffered(1))
    table = pl.BlockSpec((s, HEAD_DIM), lambda h: (0, 0), pipeline_mode=pl.Buffered(1))
    return _call(
        body, name="attn_bwd", grid=(nh,),
        in_specs=[_head_col(s, 0, nh), _head_col(s, 1, nh), _head_col(s, 2, nh), _head_col(s, 3, nh),
                  table, table, own, own, own],
        out_specs=HBM_SPEC,
        out_shape=jax.ShapeDtypeStruct((s, 8 * aw), BF16),
        scratch_shapes=[pltpu.VMEM((s, HEAD_DIM), F32)] * 8 + [
            pltpu.VMEM((4, s, HEAD_DIM), BF16), pltpu.SemaphoreType.DMA((4,))],
        operands=(proj, proj, proj, proj, cos, sin, dy, attn, lse), plan=plan)


def _rope_tables(s):
    half = HEAD_DIM // 2
    inv = np.float32(ROPE_THETA) ** (-np.arange(half, dtype=np.float32) / np.float32(half))
    ang = np.arange(s, dtype=np.float32)[:, None] * inv[None, :]
    cos, sin = np.cos(ang), np.sin(ang)
    return (jnp.asarray(np.concatenate([cos, cos], axis=-1), F32),
            jnp.asarray(np.concatenate([-sin, sin], axis=-1), F32))


def _conv_chunks(s):
    for k in range(s // CONV_ROWS):
        lo = max(0, k * CONV_ROWS - CONV_HALO)
        hi = min(s, (k + 1) * CONV_ROWS + CONV_HALO)
        yield k * CONV_ROWS, lo, hi


def _neighbours(p, lo, s):
    n = p.shape[0]
    row = lo + lax.broadcasted_iota(jnp.int32, p.shape, 0)
    prev = jnp.where(row == 0, 0.0, pltpu.roll(p, 1, axis=0))
    nxt = jnp.where(row == s - 1, 0.0, pltpu.roll(p, n - 1, axis=0))
    return prev, nxt


def _ab_mix(attn, proj, conv_w, aw, plan=None):
    s = proj.shape[0]
    nt = aw // LANES

    def col(group, sel):
        return pl.BlockSpec((s, LANES), lambda i: (0, group * nt + sel(i)))

    a_sel = lambda i: jnp.minimum(i, nt - 1)
    b_sel = lambda i: jnp.maximum(i - nt, 0)

    def body(attn_ref, za_ref, ub_ref, gb_ref, gc_ref, zb_ref, w_ref, y_ref):
        i = pl.program_id(0)

        @pl.when(i < nt)
        def _():
            y_ref[...] = (attn_ref[...].astype(F32) * _silu(za_ref[...].astype(F32))).astype(BF16)

        @pl.when(i >= nt)
        def _():
            w = w_ref[...]
            for c0, lo, hi in _conv_chunks(s):
                p = gc_ref[lo:hi, :].astype(F32) * ub_ref[lo:hi, :].astype(F32)
                prev, nxt = _neighbours(p, lo, s)
                cv = w[0:1, :] * prev + w[1:2, :] * p + w[2:3, :] * nxt
                yb = gb_ref[lo:hi, :].astype(F32) * cv * _silu(zb_ref[lo:hi, :].astype(F32))
                y_ref[c0:c0 + CONV_ROWS, :] = yb[c0 - lo:c0 - lo + CONV_ROWS, :].astype(BF16)

    return _call(
        body, name="ab_mix", grid=(2 * nt,),
        in_specs=[pl.BlockSpec((s, LANES), lambda i: (0, a_sel(i))),
                  col(3, a_sel), col(4, b_sel), col(5, b_sel), col(6, b_sel), col(7, b_sel),
                  pl.BlockSpec((3, LANES), lambda i: (0, b_sel(i)))],
        out_specs=pl.BlockSpec((s, LANES), lambda i: (0, i)),
        out_shape=jax.ShapeDtypeStruct((s, 2 * aw), BF16),
        operands=(attn, proj, proj, proj, proj, proj, conv_w), plan=plan)


def _conv_bwd(dproj, dy, proj, conv_w, aw):
    s = proj.shape[0]
    nt = aw // LANES

    def col(group):
        return pl.BlockSpec((s, LANES), lambda i: (0, group * nt + i))

    def body(dyb_ref, ub_ref, gb_ref, gc_ref, zb_ref, w_ref, dproj_in, dproj_ref, dw_ref, tiles, tile_sems):
        w = w_ref[...]
        dw = [jnp.zeros((1, LANES), F32) for _ in range(3)]
        for c0, lo, hi in _conv_chunks(s):
            ctr = slice(c0 - lo, c0 - lo + CONV_ROWS)
            out_rows = slice(c0, c0 + CONV_ROWS)
            ub = ub_ref[lo:hi, :].astype(F32)
            gc = gc_ref[lo:hi, :].astype(F32)
            gb = gb_ref[lo:hi, :].astype(F32)
            zb = zb_ref[lo:hi, :].astype(F32)
            dyb = dyb_ref[lo:hi, :].astype(F32)
            p = gc * ub
            prev, nxt = _neighbours(p, lo, s)
            cv = w[0:1, :] * prev + w[1:2, :] * p + w[2:3, :] * nxt
            sz, dsz = _silu_and_grad(zb)
            dcv = dyb * gb * sz
            dprev, dnxt = _neighbours(dcv, lo, s)
            dp = w[0:1, :] * dnxt + w[1:2, :] * dcv + w[2:3, :] * dprev
            for t, nb in enumerate((prev, p, nxt)):
                dw[t] = dw[t] + jnp.sum((dcv * nb)[ctr, :], axis=0, keepdims=True)
            tiles[0, out_rows, :] = (dp * gc)[ctr, :].astype(BF16)
            tiles[1, out_rows, :] = (dyb * cv * sz)[ctr, :].astype(BF16)
            tiles[2, out_rows, :] = (dp * ub)[ctr, :].astype(BF16)
            tiles[3, out_rows, :] = (dyb * gb * cv * dsz)[ctr, :].astype(BF16)
        dw_ref[...] = jnp.concatenate(dw, axis=0)
        _store_column_tiles(tiles, dproj_ref, tile_sems, [(4 + g) * nt + pl.program_id(0) for g in range(4)])

    return pl.pallas_call(
        body, name="conv_bwd", grid=(nt,),
        in_specs=[pl.BlockSpec((s, LANES), lambda i: (0, nt + i)),
                  col(4), col(5), col(6), col(7),
                  pl.BlockSpec((3, LANES), lambda i: (0, i)), HBM_SPEC],
        out_specs=[HBM_SPEC, pl.BlockSpec((3, LANES), lambda i: (0, i))],
        out_shape=[jax.ShapeDtypeStruct(dproj.shape, dproj.dtype), jax.ShapeDtypeStruct((3, aw), F32)],
        scratch_shapes=[pltpu.VMEM((4, s, LANES), BF16), pltpu.SemaphoreType.DMA((4,))],
        input_output_aliases={6: 0},
        compiler_params=_params(1),
    )(dy, proj, proj, proj, proj, conv_w, dproj)


def _sgu_norm(gv, ln_g, ln_b):
    mu = jnp.mean(gv, axis=-1, keepdims=True)
    xc = gv - mu
    rstd = lax.rsqrt(jnp.mean(xc * xc, axis=-1, keepdims=True) + EPS)
    vhat = xc * rstd
    return vhat, rstd, vhat * ln_g + ln_b


def _sgu_fwd(uvz, ln_g, ln_b, w_s, b_s, cw):
    s = uvz.shape[0]
    tr = 2 * CHUNK if s % (2 * CHUNK) == 0 else CHUNK
    gw = cw // N_GROUPS

    def body(u_ref, v_ref, z_ref, g_ref, b_ref, ws_ref, bs_ref, y_ref):
        _, _, vn = _sgu_norm(_gelu(v_ref[...].astype(F32)), g_ref[...], b_ref[...])
        vn = vn.astype(BF16)
        for ch in range(tr // CHUNK):
            rows = slice(ch * CHUNK, (ch + 1) * CHUNK)
            for grp in range(N_GROUPS):
                cols = slice(grp * gw, (grp + 1) * gw)
                mixed = lax.dot_general(ws_ref[grp], vn[rows, cols], NN, preferred_element_type=F32) + bs_ref[grp]
                y_ref[rows, cols] = (_gelu(u_ref[rows, cols].astype(F32)) * mixed
                                     * _silu(z_ref[rows, cols].astype(F32))).astype(BF16)

    full3 = lambda shape: pl.BlockSpec(shape, lambda i: (0, 0, 0))
    return pl.pallas_call(
        body, name="sgu_fwd", grid=(s // tr,),
        in_specs=[pl.BlockSpec((tr, cw), lambda i: (i, 0)), pl.BlockSpec((tr, cw), lambda i: (i, 1)),
                  pl.BlockSpec((tr, cw), lambda i: (i, 2)), _vec_spec(cw), _vec_spec(cw),
                  full3(w_s.shape), full3(b_s.shape)],
        out_specs=pl.BlockSpec((tr, cw), lambda i: (i, 0)),
        out_shape=jax.ShapeDtypeStruct((s, cw), BF16),
        compiler_params=_params(1, parallel=1),
    )(uvz, uvz, uvz, ln_g, ln_b, w_s, b_s)


def _sgu_bwd(uvz, dy, ln_g, ln_b, w_s, b_s, cw, plan=None):
    s = uvz.shape[0]
    tr = 2 * CHUNK if s % (2 * CHUNK) == 0 else CHUNK
    gw = cw // N_GROUPS

    def body(u_ref, v_ref, z_ref, dy_ref, g_ref, b_ref, ws_ref, bs_ref,
             duvz_ref, dws_ref, dbs_ref, dg_ref, db_ref, dvn_ref):
        vv = v_ref[...].astype(F32)
        gvec = g_ref[...]
        gelu_v, dgelu_v = _gelu_and_grad(vv)
        vhat, rstd, vn = _sgu_norm(gelu_v, gvec, b_ref[...])
        vn = vn.astype(BF16)
        first = pl.program_id(0) == 0

        @pl.when(first)
        def _():
            dws_ref[...] = jnp.zeros_like(dws_ref)
            dbs_ref[...] = jnp.zeros_like(dbs_ref)

        for ch in range(tr // CHUNK):
            rows = slice(ch * CHUNK, (ch + 1) * CHUNK)
            for grp in range(N_GROUPS):
                cols = slice(grp * gw, (grp + 1) * gw)
                vn_g = vn[rows, cols]
                mixed = lax.dot_general(ws_ref[grp], vn_g, NN, preferred_element_type=F32) + bs_ref[grp]
                uu = u_ref[rows, cols].astype(F32)
                zz = z_ref[rows, cols].astype(F32)
                dyv = dy_ref[rows, cols].astype(F32)
                (gu, dgu), (sz, dsz) = _gelu_and_grad(uu), _silu_and_grad(zz)
                duvz_ref[rows, grp * gw:(grp + 1) * gw] = (dyv * mixed * sz * dgu).astype(BF16)
                duvz_ref[rows, 2 * cw + grp * gw:2 * cw + (grp + 1) * gw] = (dyv * gu * mixed * dsz).astype(BF16)
                dmixed = dyv * gu * sz
                dm16 = dmixed.astype(BF16)
                dws_ref[grp] += lax.dot_general(dm16, vn_g, NT, preferred_element_type=F32)
                dbs_ref[grp] += jnp.broadcast_to(jnp.sum(dmixed, axis=1, keepdims=True), (CHUNK, LANES))
                dvn_ref[rows, cols] = lax.dot_general(ws_ref[grp], dm16, TN, preferred_element_type=F32)

        dvn = dvn_ref[...]
        _accumulate(dg_ref, jnp.sum(dvn * vhat, axis=0, keepdims=True))
        _accumulate(db_ref, jnp.sum(dvn, axis=0, keepdims=True))
        dvhat = dvn * gvec
        dgv = rstd * (dvhat - jnp.mean(dvhat, axis=-1, keepdims=True)
                      - vhat * jnp.mean(dvhat * vhat, axis=-1, keepdims=True))
        duvz_ref[:, cw:2 * cw] = (dgv * dgelu_v).astype(BF16)

    full3 = lambda shape: pl.BlockSpec(shape, lambda i: (0, 0, 0))
    acc3 = jax.ShapeDtypeStruct((N_GROUPS, CHUNK, LANES), F32)
    vec = jax.ShapeDtypeStruct((1, cw), F32)
    row = pl.BlockSpec((tr, cw), lambda i: (i, 0))
    return _call(
        body, name="sgu_bwd", grid=(s // tr,),
        in_specs=[row, pl.BlockSpec((tr, cw), lambda i: (i, 1)), pl.BlockSpec((tr, cw), lambda i: (i, 2)),
                  row, _vec_spec(cw), _vec_spec(cw), full3(w_s.shape), full3(b_s.shape)],
        out_specs=[pl.BlockSpec((tr, 3 * cw), lambda i: (i, 0)), full3((N_GROUPS, CHUNK, LANES)),
                   full3((N_GROUPS, CHUNK, LANES)), _vec_spec(cw), _vec_spec(cw)],
        out_shape=[jax.ShapeDtypeStruct((s, 3 * cw), BF16), acc3, acc3, vec, vec],
        scratch_shapes=[pltpu.VMEM((tr, cw), F32)],
        operands=(uvz, uvz, uvz, dy, ln_g, ln_b, w_s, b_s), plan=plan)


def _flat_rows(a):
    return a.reshape(-1, a.shape[-1])


def _add_sibling(grad, recv, core_idx):
    nchip, k, n = grad.shape
    tr = _tile(k // 2, (256, 128))
    nb = (k // 2) // tr

    def body(c_ref, g_ref, r_ref, o_ref):
        o_ref[...] = (g_ref[...].astype(F32) + r_ref[...].astype(F32)).astype(BF16)

    return pl.pallas_call(
        body, name="add_sibling",
        grid_spec=pltpu.PrefetchScalarGridSpec(
            num_scalar_prefetch=1, grid=(nchip, nb),
            in_specs=[pl.BlockSpec((None, tr, n), lambda q, i, c: (q, c[0] * nb + i, 0)),
                      pl.BlockSpec((None, tr, n), lambda q, i, c: (q, i, 0))],
            out_specs=pl.BlockSpec((None, tr, n), lambda q, i, c: (q, i, 0))),
        out_shape=jax.ShapeDtypeStruct((nchip, k // 2, n), BF16),
        compiler_params=_params(2, parallel=2),
    )(core_idx, grad, recv)


def _sum_chips(own, others, reduced, layer, place_idx):
    _, kh, n = own.shape
    tr = _tile(kh, (256, 128))
    nb = kh // tr

    def body(place_ref, own_ref, oth_ref, red_ref, o_ref):
        acc = own_ref[...].astype(F32)
        for q in range(3):
            acc = acc + oth_ref[q].astype(F32)
        o_ref[...] = acc

    return pl.pallas_call(
        body, name="sum_chips",
        grid_spec=pltpu.PrefetchScalarGridSpec(
            num_scalar_prefetch=1, grid=(nb,),
            in_specs=[pl.BlockSpec((None, tr, n), lambda i, p: (p[0], i, 0)),
                      pl.BlockSpec((3, tr, n), lambda i, p: (0, i, 0)),
                      HBM_SPEC],
            out_specs=pl.BlockSpec((None, tr, n), lambda i, p: (layer, p[1] * nb + i, 0))),
        out_shape=jax.ShapeDtypeStruct(reduced.shape, reduced.dtype),
        input_output_aliases={3: 0},
        compiler_params=_params(1, parallel=1),
    )(place_idx, own, others, reduced)


def _sum_devices(parts, plan=None):
    nd, r, _ = parts.shape
    tr = _tile(r, (512, 256, 128, 64, 32, 16, 8))

    def body(p_ref, o_ref):
        acc = p_ref[0]
        for q in range(1, nd):
            acc = acc + p_ref[q]
        o_ref[...] = acc

    return _call(
        body, name="sum_devices", grid=(r // tr,),
        in_specs=[pl.BlockSpec((nd, tr, LANES), lambda i: (0, i, 0))],
        out_specs=pl.BlockSpec((tr, LANES), lambda i: (i, 0)),
        out_shape=jax.ShapeDtypeStruct((r, LANES), F32),
        operands=(parts,), parallel=1, plan=plan)


def _adamw(w, g, m, v, also_grad=False):
    r, n = w.shape
    tr = _tile(r, [p for p in (1024, 512, 256, 128, 64, 32, 16, 8) if p * n <= ELEMENTWISE_BLOCK])
    n_out = 4 if also_grad else 3

    def body(w_ref, g_ref, m_ref, v_ref, d_ref, nm_ref, nv_ref, *g_out):
        gv = g_ref[...]
        if also_grad:
            g_out[0][...] = gv
        nm = ADAM_B1 * m_ref[...] + (1.0 - ADAM_B1) * gv
        nv = ADAM_B2 * v_ref[...] + (1.0 - ADAM_B2) * (gv * gv)
        m_hat = nm / (1.0 - ADAM_B1 ** ADAM_STEP)
        v_hat = nv / (1.0 - ADAM_B2 ** ADAM_STEP)
        d_ref[...] = -ADAM_LR * (m_hat / (jnp.sqrt(v_hat) + ADAM_EPS) + ADAM_WD * w_ref[...])
        nm_ref[...] = nm
        nv_ref[...] = nv

    spec = pl.BlockSpec((tr, n), lambda i: (i, 0))
    shp = jax.ShapeDtypeStruct((r, n), F32)
    return _call(
        body, name="adamw", grid=(r // tr,),
        in_specs=[spec] * 4, out_specs=[spec] * n_out, out_shape=[shp] * n_out,
        operands=(w, g, m, v), parallel=1)


def _pack(arrays, row_multiple=8):
    flat = [a.reshape(-1) for a in arrays]
    sizes = [f.shape[0] for f in flat]
    total = sum(sizes)
    unit = LANES * row_multiple
    padded = -(-total // unit) * unit
    if padded > total:
        flat.append(jnp.zeros((padded - total,), F32))
    offsets = [sum(sizes[:i]) for i in range(len(sizes))]
    return jnp.concatenate(flat).reshape(-1, LANES), offsets


def _unpack(packed, offsets, shapes):
    flat = packed.reshape(-1)
    return [flat[o:o + math.prod(s)].reshape(s) for o, s in zip(offsets, shapes)]


def kernel(x, c, ab_norm_g, ab_w_mod, ab_b_mod, ab_w_in, ab_conv_w, ab_w_out, sg_norm_g, sg_w_mod, sg_b_mod, sg_w_in, sg_ln_g, sg_ln_b, sg_w_s, sg_b_s, sg_w_out, final_norm_g, loss_target, m_ab_norm_g, m_ab_w_mod, m_ab_b_mod, m_ab_w_in, m_ab_conv_w, m_ab_w_out, m_sg_norm_g, m_sg_w_mod, m_sg_b_mod, m_sg_w_in, m_sg_ln_g, m_sg_ln_b, m_sg_w_s, m_sg_b_s, m_sg_w_out, m_final_norm_g, v_ab_norm_g, v_ab_w_mod, v_ab_b_mod, v_ab_w_in, v_ab_conv_w, v_ab_w_out, v_sg_norm_g, v_sg_w_mod, v_sg_b_mod, v_sg_w_in, v_sg_ln_g, v_sg_ln_b, v_sg_w_s, v_sg_b_s, v_sg_w_out, v_final_norm_g):
    s, d = x.shape[1], x.shape[2]
    aw = d // 2
    cw = d
    mod_l = ab_w_mod.shape[-1]
    x0 = x[0]
    target = loss_target[0]
    mx, my, mc = lax.axis_index("x"), lax.axis_index("y"), lax.axis_index("c")
    chip = 2 * mx + my
    chip_idx = jnp.reshape(chip, (1,)).astype(jnp.int32)
    core_idx = jnp.reshape(mc, (1,)).astype(jnp.int32)
    place_idx = jnp.stack([chip, mc]).astype(jnp.int32)

    win = [_place_own_shard(ab_w_in if L % 2 == 0 else sg_w_in, L // 2, chip_idx) for L in range(4)]
    wout = [_place_own_shard(ab_w_out if L % 2 == 0 else sg_w_out, L // 2, chip_idx) for L in range(4)]
    k_in, k_out = d, wout[0].shape[1]

    def gather_plan(ici=(), pass_on=()):
        arrays, copies, names = [], [], []
        for kind, L, only in ici:
            arrays.append(win[L] if kind == "in" else wout[L])
            names.append((kind, L))
            copies += _gather_ici(len(arrays) - 1, k_in if kind == "in" else k_out, only)
        for kind, L in pass_on:
            arrays.append(win[L] if kind == "in" else wout[L])
            names.append((kind, L))
            copies += _gather_pass_on(len(arrays) - 1, k_in if kind == "in" else k_out)
        return _Plan(tuple(arrays), tuple(copies)), names

    def absorb(plan_and_names, updated):
        _, names = plan_and_names
        for pos, (kind, L) in enumerate(names):
            if kind == "in":
                win[L] = updated[pos]
            else:
                wout[L] = updated[pos]

    win[0] = _comm_stages("gather_first_w_in", [win[0]], [_gather_ici(0, k_in), _gather_pass_on(0, k_in)],
                          chained=True)[0]

    small_local = [c[0], ab_conv_w, sg_norm_g, sg_ln_g, sg_ln_b]
    small_shapes = [a.shape for a in small_local]
    payload, small_off = _pack(small_local)
    gathered = _all_to_all(jnp.broadcast_to(payload[None], (N_DEV,) + payload.shape), "gather_small")
    per_dev = [_unpack(gathered[b], small_off, small_shapes) for b in range(N_DEV)]
    c_all = jnp.stack([per_dev[b][0] for b in range(N_DEV)])

    def from_chips(idx, axis):
        return jnp.concatenate([per_dev[2 * q][idx] for q in range(N_CHIPS)], axis=axis)

    conv_w_full = from_chips(1, 2)
    sg_norm_g_full = from_chips(2, 1)
    sg_ln_g_full = from_chips(3, 1)
    sg_ln_b_full = from_chips(4, 1)

    ab_b_local = lax.dynamic_slice_in_dim(ab_b_mod, chip * mod_l, mod_l, axis=1)
    mod_rows = []
    for layer in range(4):
        i = layer // 2
        w_mod, bias = (ab_w_mod, ab_b_local) if layer % 2 == 0 else (sg_w_mod, sg_b_mod)
        mod_rows.append(_mod_fwd(c_all, w_mod, bias[i:i + 1], i))
    mod_local = jnp.stack(mod_rows, axis=1)
    mod_recv = _all_to_all(mod_local.reshape(N_DEV, -1, LANES), "exchange_mod")
    mod_recv = mod_recv.reshape(N_DEV, 4, mod_l)
    mod_full = jnp.concatenate([mod_recv[2 * q] for q in range(N_CHIPS)], axis=-1)
    shifts = [mod_full[l:l + 1, :d] for l in range(4)]
    scales = [mod_full[l:l + 1, d:2 * d] for l in range(4)]
    gates = [mod_full[l:l + 1, 2 * d:] for l in range(4)]

    cos, sin = _rope_tables(s)
    w_s16 = sg_w_s.astype(BF16)
    b_s3 = sg_b_s[..., None]

    near, far, everyone = (0, 1), (2,), (0, 1, 2)
    fwd_comm = {
        ("in_proj", 0): ([("in", 1, near), ("out", 0, everyone)], []),
        ("attn", 0): ([("in", 1, far), ("out", 1, everyone), ("in", 2, near)], [("out", 0)]),
        ("out_proj", 0): ([], [("in", 1), ("out", 1)]),
        ("in_proj", 1): ([("in", 2, far)], []),
        ("out_proj", 1): ([], [("in", 2)]),
        ("in_proj", 2): ([("in", 3, everyone)], []),
        ("attn", 2): ([("out", 3, everyone), ("out", 2, everyone)], []),
        ("ab_mix", 2): ([], [("out", 2)]),
        ("out_proj", 2): ([], [("in", 3), ("out", 3)]),
    }

    def carried(key, fn, *args):
        if key not in fwd_comm:
            return fn(*args)
        pn = gather_plan(*fwd_comm[key])
        res, updated = fn(*args, plan=pn[0])
        absorb(pn, updated)
        return res

    def norm_params(layer):
        g = ab_norm_g if layer % 2 == 0 else sg_norm_g_full
        return g[layer // 2:layer // 2 + 1], scales[layer], shifts[layer]

    saved = []
    xs = x0
    h = _prenorm(xs, *norm_params(0))
    for layer in range(4):
        i = layer // 2
        next_norm = norm_params(layer + 1) if layer < 3 else None
        if layer % 2 == 0:
            proj = carried(("in_proj", layer), _in_proj, h, win[layer])
            attn, lse = carried(("attn", layer), _attn_fwd, proj, cos, sin, aw)
            y = carried(("ab_mix", layer), _ab_mix, attn, proj, conv_w_full[i], aw)
            res = carried(("out_proj", layer), _out_proj_residual, y, wout[layer].reshape(-1, d), xs,
                          gates[layer], next_norm)
            saved.append((xs, h, proj, y, res[1], attn, lse))
        else:
            uvz = carried(("in_proj", layer), _in_proj, h, win[layer])
            y = _sgu_fwd(uvz, sg_ln_g_full[i:i + 1], sg_ln_b_full[i:i + 1], w_s16[i], b_s3[i], cw)
            res = carried(("out_proj", layer), _out_proj_residual, y, wout[layer].reshape(-1, d), xs,
                          gates[layer], next_norm)
            saved.append((xs, h, uvz, y, res[1]))
        xs = res[0]
        h = res[2] if next_norm is not None else None

    loss11, dx, d_final_g, dout, dgate = _final_loss(xs, target, final_norm_g[None], saved[3][4], gates[3])
    loss = lax.psum(loss11[0, 0], ("x", "y", "c"))

    reduced = {"in": [lax.empty((2,) + w.shape[1:], F32) for w in (ab_w_in, sg_w_in)],
               "out": [lax.empty((2,) + w.shape[1:], F32) for w in (ab_w_out, sg_w_out)]}
    grads = {}
    stage = {}
    k_of = {"in": k_in, "out": k_out}

    def swap_plan(which):
        arrays, copies = [], []
        for kind, L in which:
            g = grads[kind, L]
            arrays += [g, jax.ShapeDtypeStruct((N_CHIPS, g.shape[1] // 2, g.shape[2]), BF16)]
            copies += _reduce_swap(len(arrays) - 2, len(arrays) - 1, k_of[kind])
        return _Plan(tuple(arrays), tuple(copies))

    def after_swap(which, updated):
        for pos, (kind, L) in enumerate(which):
            stage[kind, L] = _add_sibling(grads[kind, L], updated[2 * pos + 1], core_idx)

    def ici_plan(pieces):
        arrays, copies = [], []
        for kind, L, only in pieces:
            cs = stage[kind, L]
            arrays += [cs, stage.get((kind, L, "recv"), jax.ShapeDtypeStruct((3,) + cs.shape[1:], BF16))]
            copies += _reduce_ici(len(arrays) - 2, len(arrays) - 1, only)
        return _Plan(tuple(arrays), tuple(copies))

    def after_ici(pieces, updated):
        for pos, (kind, L, _) in enumerate(pieces):
            stage[kind, L, "recv"] = updated[2 * pos + 1]

    def sum_layer(L):
        for kind in ("in", "out"):
            reduced[kind][L % 2] = _sum_chips(stage[kind, L], stage[kind, L, "recv"], reduced[kind][L % 2],
                                              L // 2, place_idx)

    def share_plan(L):
        arrays = (reduced["in"][L % 2], reduced["out"][L % 2])
        copies = _reduce_share(0, L // 2, k_in) + _reduce_share(1, L // 2, k_out)
        return _Plan(arrays, tuple(copies))

    def after_share(L, updated):
        reduced["in"][L % 2], reduced["out"][L % 2] = updated[0], updated[1]

    all_chips = (0, 1, 2)
    dmods = [None] * 4
    d_ab_norm_g, d_sg_norm_g = [None, None], [None, None]
    d_conv_w, d_ln_g, d_ln_b, d_w_s, d_b_s = ([None, None] for _ in range(5))
    for layer in reversed(range(4)):
        i = layer // 2
        prev = layer + 1
        busy = prev < 4
        if layer % 2 == 0:
            xs, h, proj, y, out, attn, lse = saved[layer]
        else:
            xs, h, uvz, y, out = saved[layer]
        below = (saved[layer - 1][4], gates[layer - 1]) if layer > 0 else None
        w2 = wout[layer].reshape(-1, d)
        grads["out", layer] = _out_proj_bwd_w(y, dout).reshape(N_CHIPS, -1, d)
        if busy:
            swapped = [("in", prev), ("out", prev)] + ([("out", 0)] if layer == 0 else [])
            dy, updated = _out_proj_bwd_act(dout, w2, plan=swap_plan(swapped))
            after_swap(swapped, updated)
        else:
            dy = _out_proj_bwd_act(dout, w2)
        if layer % 2 == 0:
            if busy:
                pieces = [("in", prev, all_chips), ("out", prev, all_chips)]
                pieces += [("out", 0, all_chips)] if layer == 0 else []
                dact, updated = _attn_bwd(proj, cos, sin, dy, attn, lse, aw, plan=ici_plan(pieces))
                after_ici(pieces, updated)
                sum_layer(prev)
            else:
                dact = _attn_bwd(proj, cos, sin, dy, attn, lse, aw)
            dact, d_conv_w[i] = _conv_bwd(dact, dy, proj, conv_w_full[i], aw)
            if layer == 0:
                gating, gating_off = _pack([jnp.stack(d_w_s), jnp.stack(d_b_s)], PACKED_ROW_BLOCK)
                slots = _place_own_slot(gating, jnp.reshape(2 * chip + mc, (1,)).astype(jnp.int32))
                sharing = share_plan(prev)
                plan = _merge_plans(sharing, _Plan((gating, slots), tuple(_broadcast_copies(0, 1))))
                grads["in", 0], updated = _in_proj_bwd_w(h, dact, win[0].shape[-1], plan=plan)
                after_share(prev, updated)
                all_gating = updated[len(sharing.arrays) + 1]
                plan = swap_plan([("in", 0)])
                after_swap([("in", 0)], _comm_stages("grads_to_sibling", plan.arrays, [plan.copies]))
                pieces = [("in", 0, all_chips)]
                dh, updated = _in_proj_bwd_act(dact, win[0], plan=ici_plan(pieces))
                after_ici(pieces, updated)
            elif busy:
                dh, updated = _in_proj_bwd_act(dact, win[layer], plan=share_plan(prev))
                after_share(prev, updated)
                grads["in", layer] = _in_proj_bwd_w(h, dact, win[layer].shape[-1])
            else:
                dh = _in_proj_bwd_act(dact, win[layer])
                grads["in", layer] = _in_proj_bwd_w(h, dact, win[layer].shape[-1])
            norm_g = ab_norm_g[i:i + 1]
        else:
            sgu_args = (uvz, dy, sg_ln_g_full[i:i + 1], sg_ln_b_full[i:i + 1], w_s16[i], b_s3[i], cw)
            if busy:
                pieces = [("in", prev, (0, 1))]
                res, updated = _sgu_bwd(*sgu_args, plan=ici_plan(pieces))
                after_ici(pieces, updated)
            else:
                res = _sgu_bwd(*sgu_args)
            dact, d_w_s[i], db_wide, d_ln_g[i], d_ln_b[i] = res
            d_b_s[i] = db_wide[:, :, 0]
            if busy:
                pieces = [("in", prev, (2,)), ("out", prev, all_chips)]
                dh, updated = _in_proj_bwd_act(dact, win[layer], plan=ici_plan(pieces))
                after_ici(pieces, updated)
                sum_layer(prev)
                grads["in", layer], updated = _in_proj_bwd_w(h, dact, win[layer].shape[-1], plan=share_plan(prev))
                after_share(prev, updated)
            else:
                dh = _in_proj_bwd_act(dact, win[layer])
                grads["in", layer] = _in_proj_bwd_w(h, dact, win[layer].shape[-1])
            norm_g = sg_norm_g_full[i:i + 1]
        res = _prenorm_bwd(xs, dh, dx, norm_g, scales[layer], below)
        dx, dshift, dscale, d_norm_g = res[:4]
        (d_ab_norm_g if layer % 2 == 0 else d_sg_norm_g)[i] = d_norm_g
        dmods[layer] = jnp.concatenate([dshift, dscale, dgate], axis=1)
        if below:
            dout, dgate = res[4:]
    grad_x = dx[None]

    partial_list = [jnp.concatenate(dmods, axis=0),
                    jnp.concatenate(d_ab_norm_g, axis=0), jnp.concatenate(d_sg_norm_g, axis=0), d_final_g[0],
                    jnp.stack(d_conv_w), jnp.concatenate(d_ln_g, axis=0), jnp.concatenate(d_ln_b, axis=0)]
    partial_shapes = [a.shape for a in partial_list]
    partials, part_off = _pack(partial_list, PACKED_ROW_BLOCK)
    all_partials = _all_to_all(jnp.broadcast_to(partials[None], (N_DEV,) + partials.shape), "gather_partials")
    sum_layer(0)
    summed_packed, updated = _sum_devices(all_partials, plan=share_plan(0))
    after_share(0, updated)
    summed = _unpack(summed_packed, part_off, partial_shapes)
    g_mod_bias, g_ab_norm_g, g_sg_norm_g_full, g_final_g, g_conv_full, g_ln_g_full, g_ln_b_full = summed
    g_w_s, g_b_s = _unpack(_sum_devices(all_gating), gating_off, [sg_w_s.shape, sg_b_s.shape])
    dm_all = jnp.stack([_unpack(all_partials[b], part_off[:1], partial_shapes[:1])[0] for b in range(N_DEV)])
    dm_local = lax.dynamic_slice_in_dim(dm_all, chip * mod_l, mod_l, axis=2)

    def chip_cols(a, axis):
        width = a.shape[axis] // N_CHIPS
        return lax.dynamic_slice_in_dim(a, chip * width, width, axis=axis)

    g_ab_b_mod = jnp.stack([g_mod_bias[0], g_mod_bias[2]])
    g_sg_b_mod = chip_cols(jnp.stack([g_mod_bias[1], g_mod_bias[3]]), 1)
    g_ab_w_mod = _mod_bwd_w(c_all, jnp.stack([dm_local[:, 0], dm_local[:, 2]]))
    g_sg_w_mod = _mod_bwd_w(c_all, jnp.stack([dm_local[:, 1], dm_local[:, 3]]))
    g_conv = chip_cols(g_conv_full, 2)
    g_sg_norm_g = chip_cols(g_sg_norm_g_full, 1)
    g_ln_g = chip_cols(g_ln_g_full, 1)
    g_ln_b = chip_cols(g_ln_b_full, 1)

    def step_big(w, g, m, v, also_grad=False):
        res = _adamw(_flat_rows(w), _flat_rows(g), _flat_rows(m), _flat_rows(v), also_grad)
        return tuple(a.reshape(w.shape) for a in res)

    big_out = {
        "ab_w_mod": step_big(ab_w_mod, g_ab_w_mod, m_ab_w_mod, v_ab_w_mod),
        "ab_w_in": step_big(ab_w_in, reduced["in"][0], m_ab_w_in, v_ab_w_in, True),
        "ab_w_out": step_big(ab_w_out, reduced["out"][0], m_ab_w_out, v_ab_w_out, True),
        "sg_w_mod": step_big(sg_w_mod, g_sg_w_mod, m_sg_w_mod, v_sg_w_mod),
        "sg_w_in": step_big(sg_w_in, reduced["in"][1], m_sg_w_in, v_sg_w_in, True),
        "sg_w_out": step_big(sg_w_out, reduced["out"][1], m_sg_w_out, v_sg_w_out, True),
    }
    g_ab_w_in, g_ab_w_out = big_out["ab_w_in"][3], big_out["ab_w_out"][3]
    g_sg_w_in, g_sg_w_out = big_out["sg_w_in"][3], big_out["sg_w_out"][3]
    small_names = ["ab_norm_g", "ab_b_mod", "ab_conv_w", "sg_norm_g", "sg_b_mod", "sg_ln_g", "sg_ln_b",
                   "sg_w_s", "sg_b_s", "final_norm_g"]
    small_w = [ab_norm_g, ab_b_mod, ab_conv_w, sg_norm_g, sg_b_mod, sg_ln_g, sg_ln_b, sg_w_s, sg_b_s, final_norm_g]
    small_g = [g_ab_norm_g, g_ab_b_mod, g_conv, g_sg_norm_g, g_sg_b_mod, g_ln_g, g_ln_b, g_w_s, g_b_s, g_final_g]
    small_m = [m_ab_norm_g, m_ab_b_mod, m_ab_conv_w, m_sg_norm_g, m_sg_b_mod, m_sg_ln_g, m_sg_ln_b, m_sg_w_s,
               m_sg_b_s, m_final_norm_g]
    small_v = [v_ab_norm_g, v_ab_b_mod, v_ab_conv_w, v_sg_norm_g, v_sg_b_mod, v_sg_ln_g, v_sg_ln_b, v_sg_w_s,
               v_sg_b_s, v_final_norm_g]
    shapes = [a.shape for a in small_w]
    pw, off = _pack(small_w, PACKED_ROW_BLOCK)
    pg, _ = _pack(small_g, PACKED_ROW_BLOCK)
    pm, _ = _pack(small_m, PACKED_ROW_BLOCK)
    pv, _ = _pack(small_v, PACKED_ROW_BLOCK)
    pd, pnm, pnv = _adamw(pw, pg, pm, pv)
    small_out = {}
    for name, dl, nm, nv in zip(small_names, _unpack(pd, off, shapes), _unpack(pnm, off, shapes),
                                _unpack(pnv, off, shapes)):
        small_out[name] = (dl, nm, nv)

    grad_of = {
        "ab_norm_g": g_ab_norm_g, "ab_w_mod": g_ab_w_mod, "ab_b_mod": g_ab_b_mod, "ab_w_in": g_ab_w_in,
        "ab_conv_w": g_conv, "ab_w_out": g_ab_w_out, "sg_norm_g": g_sg_norm_g, "sg_w_mod": g_sg_w_mod,
        "sg_b_mod": g_sg_b_mod, "sg_w_in": g_sg_w_in, "sg_ln_g": g_ln_g, "sg_ln_b": g_ln_b, "sg_w_s": g_w_s,
        "sg_b_s": g_b_s, "sg_w_out": g_sg_w_out, "final_norm_g": g_final_g,
    }
    order = ["ab_norm_g", "ab_w_mod", "ab_b_mod", "ab_w_in", "ab_conv_w", "ab_w_out", "sg_norm_g", "sg_w_mod",
             "sg_b_mod", "sg_w_in", "sg_ln_g", "sg_ln_b", "sg_w_s", "sg_b_s", "sg_w_out", "final_norm_g"]
    steps = {**big_out, **small_out}
    return (loss, grad_x, *[grad_of[n] for n in order], *[steps[n][0] for n in order],
            *[steps[n][1] for n in order], *[steps[n][2] for n in order])
```

```python
import math
from typing import Any, Callable, NamedTuple

import jax
import jax.numpy as jnp
import numpy as np
from jax import lax
from jax.experimental import pallas as pl
from jax.experimental.pallas import tpu as pltpu

F32 = jnp.float32
BF16 = jnp.bfloat16

HEAD_DIM = 128
RADIUS = 64
DILATIONS = (1, 4, 16)
Q_BLOCK = 256
K_WINDOW = Q_BLOCK + 2 * RADIUS
ATTN_UNROLL = 8
ROPE_THETA = 10000.0
NEG_INF = -1e30
N_GROUPS = 8
CHUNK = 128
EPS = 1e-6
CONV_ROWS = 512
CONV_HALO = 16
LANES = 128
ELEMENTWISE_BLOCK = 512 * 1024
PACKED_ROW_BLOCK = 512
N_DEV = 8
N_CHIPS = 4

ADAM_LR = 0.001
ADAM_B1 = 0.9
ADAM_B2 = 0.999
ADAM_EPS = 1e-08
ADAM_WD = 0.01
ADAM_STEP = 10

VMEM_LIMIT_V7X = 56 * 1024 * 1024

MESH_ID = pl.DeviceIdType.MESH
HBM_SPEC = pl.BlockSpec(memory_space=pltpu.HBM)

NN = (((1,), (0,)), ((), ()))
NT = (((1,), (1,)), ((), ()))
TN = (((0,), (0,)), ((), ()))


def _params(n_grid, parallel=0):
    sem = tuple(["parallel"] * parallel + ["arbitrary"] * (n_grid - parallel))
    return pltpu.CompilerParams(dimension_semantics=sem, vmem_limit_bytes=VMEM_LIMIT_V7X)


def _tile(n, prefs):
    for p in prefs:
        if n % p == 0:
            return p
    return n


def _sigmoid(z):
    return 1.0 / (1.0 + jnp.exp(-z))


def _silu(z):
    return z * _sigmoid(z)


_GELU_K = math.sqrt(2.0 / math.pi)
_GELU_C = 0.044715


def _gelu(u):
    return 0.5 * u * (1.0 + jnp.tanh(_GELU_K * (u + _GELU_C * u * u * u)))


def _gelu_and_grad(u):
    t = jnp.tanh(_GELU_K * (u + _GELU_C * u * u * u))
    half = 0.5 * (1.0 + t)
    return u * half, half + 0.5 * u * (1.0 - t * t) * _GELU_K * (1.0 + 3.0 * _GELU_C * u * u)


def _silu_and_grad(z):
    s = _sigmoid(z)
    return z * s, s * (1.0 + z * (1.0 - s))


class _Place(NamedTuple):
    x: Any
    y: Any
    c: Any
    chip: Any


def _my_place():
    mx, my, mc = lax.axis_index("x"), lax.axis_index("y"), lax.axis_index("c")
    return _Place(mx, my, mc, 2 * mx + my)


def _other_chips(p):
    return [(1 - p.x, p.y), (p.x, 1 - p.y), (1 - p.x, 1 - p.y)]


class _Copy(NamedTuple):
    src: int
    src_at: Callable
    dst: int
    dst_at: Callable
    peer: Callable


class _Plan(NamedTuple):
    arrays: tuple
    copies: tuple


def _view(ref, index):
    return ref if index is None else ref.at[index]


def _plan_io(plan):
    ins = [k for k, a in enumerate(plan.arrays) if not isinstance(a, jax.ShapeDtypeStruct)]
    written = sorted({cp.dst for cp in plan.copies})
    return ins, written


def _descriptors(plan, in_refs, out_refs, send_sems, recv_sems):
    ins, written = _plan_io(plan)
    place = _my_place()
    return [
        pltpu.make_async_remote_copy(
            src_ref=_view(in_refs[ins.index(cp.src)], cp.src_at(place)),
            dst_ref=_view(out_refs[written.index(cp.dst)], cp.dst_at(place)),
            send_sem=send_sems.at[k], recv_sem=recv_sems.at[k],
            device_id=cp.peer(place), device_id_type=MESH_ID)
        for k, cp in enumerate(plan.copies)]


def _plan_operands(plan, n_in, n_out):
    ins, written = _plan_io(plan)
    operands = [plan.arrays[k] for k in ins]
    out_shape = [jax.ShapeDtypeStruct(plan.arrays[k].shape, plan.arrays[k].dtype) for k in written]
    aliases = {n_in + ins.index(k): n_out + pos for pos, k in enumerate(written) if k in ins}
    n = len(plan.copies)
    sems = [pltpu.SemaphoreType.DMA((n,)), pltpu.SemaphoreType.DMA((n,))]
    return operands, out_shape, aliases, sems, written


def _call(body, *, name, grid, in_specs, out_specs, out_shape, operands, scratch_shapes=(), aliases=None,
          parallel=0, plan=None):
    single = not isinstance(out_shape, (list, tuple))
    out_shape = [out_shape] if single else list(out_shape)
    out_specs = [out_specs] if single else list(out_specs)
    if plan is None:
        res = pl.pallas_call(
            body, name=name, grid=grid, in_specs=list(in_specs), out_specs=out_specs, out_shape=out_shape,
            scratch_shapes=list(scratch_shapes), input_output_aliases=aliases or {},
            compiler_params=_params(len(grid), parallel=parallel),
        )(*operands)
        return res[0] if single else res

    n_in, n_out, n_scr = len(operands), len(out_shape), len(scratch_shapes)
    p_operands, p_out_shape, p_aliases, sems, written = _plan_operands(plan, n_in, n_out)
    n_pin, n_pout = len(p_operands), len(p_out_shape)

    def wrapped(*refs):
        ins = refs[:n_in]
        p_in = refs[n_in:n_in + n_pin]
        outs = refs[n_in + n_pin:n_in + n_pin + n_out]
        p_out = refs[n_in + n_pin + n_out:n_in + n_pin + n_out + n_pout]
        scratch = refs[n_in + n_pin + n_out + n_pout:n_in + n_pin + n_out + n_pout + n_scr]
        send_sems, recv_sems = refs[-2:]
        ids = [pl.program_id(a) for a in range(len(grid))]
        first = ids[0] == 0
        last = ids[0] == grid[0] - 1
        for a in range(1, len(grid)):
            first = jnp.logical_and(first, ids[a] == 0)
            last = jnp.logical_and(last, ids[a] == grid[a] - 1)

        @pl.when(first)
        def _():
            for cp in _descriptors(plan, p_in, p_out, send_sems, recv_sems):
                cp.start()

        body(*ins, *outs, *scratch)

        @pl.when(last)
        def _():
            for cp in _descriptors(plan, p_in, p_out, send_sems, recv_sems):
                cp.wait()

    res = pl.pallas_call(
        wrapped, name=name, grid=grid,
        in_specs=list(in_specs) + [HBM_SPEC] * n_pin,
        out_specs=out_specs + [HBM_SPEC] * n_pout,
        out_shape=out_shape + p_out_shape,
        scratch_shapes=list(scratch_shapes) + sems,
        input_output_aliases={**(aliases or {}), **p_aliases},
        compiler_params=_params(len(grid)),
    )(*operands, *p_operands)
    outs = res[0] if single else res[:n_out]
    return outs, dict(zip(written, res[n_out:]))


def _comm_stages(name, arrays, stages, chained=False):
    plan = _Plan(tuple(arrays), tuple(cp for st in stages for cp in st))
    p_operands, p_out_shape, p_aliases, sems, written = _plan_operands(plan, 0, 0)
    n_pin = len(p_operands)

    def body(*refs):
        p_in = refs[:n_pin]
        p_out = refs[n_pin:n_pin + len(written)]
        send_sems, recv_sems = refs[-2:]
        all_copies = _descriptors(plan, p_in, p_out, send_sems, recv_sems)
        if chained:
            n = len(stages[0])
            for cp in all_copies[:n]:
                cp.start()
            for k in range(n):
                all_copies[k].wait()
                all_copies[n + k].start()
            for cp in all_copies[n:]:
                cp.wait()
            return
        base = 0
        for st in stages:
            for cp in all_copies[base:base + len(st)]:
                cp.start()
            for cp in all_copies[base:base + len(st)]:
                cp.wait()
            base += len(st)

    res = pl.pallas_call(
        body, name=name, in_specs=[HBM_SPEC] * n_pin, out_specs=[HBM_SPEC] * len(written),
        out_shape=p_out_shape, scratch_shapes=sems, input_output_aliases=p_aliases,
    )(*p_operands)
    return dict(zip(written, res))


def _half_rows(k, c):
    return pl.ds(c * (k // 2), k // 2)


def _gather_ici(a, k, only=(0, 1, 2)):
    own = lambda p: (p.chip, _half_rows(k, p.c))
    return [_Copy(a, own, a, own, lambda p, q=q: (*_other_chips(p)[q], p.c)) for q in only]


def _gather_pass_on(a, k):
    def at(q):
        def index(p):
            px, py = _other_chips(p)[q]
            return (2 * px + py, _half_rows(k, p.c))
        return index
    return [_Copy(a, at(q), a, at(q), lambda p: (p.x, p.y, 1 - p.c)) for q in range(3)]


def _reduce_swap(src, dst, k):
    return [_Copy(src, lambda p: (pl.ds(0, N_CHIPS), _half_rows(k, 1 - p.c)), dst, lambda p: None,
                  lambda p: (p.x, p.y, 1 - p.c))]


def _reduce_ici(src, dst, only=(0, 1, 2)):
    def slab(q):
        def index(p):
            px, py = _other_chips(p)[q]
            return 2 * px + py
        return index
    return [_Copy(src, slab(q), dst, lambda p, q=q: q, lambda p, q=q: (*_other_chips(p)[q], p.c)) for q in only]


def _reduce_share(a, layer, k):
    at = lambda p: (layer, _half_rows(k, p.c))
    return [_Copy(a, at, a, at, lambda p: (p.x, p.y, 1 - p.c))]


def _merge_plans(a, b):
    off = len(a.arrays)
    moved = tuple(cp._replace(src=cp.src + off, dst=cp.dst + off) for cp in b.copies)
    return _Plan(a.arrays + b.arrays, a.copies + moved)


def _broadcast_copies(src, dst):
    me = lambda p: 2 * p.chip + p.c

    def peer(k):
        return lambda p: (1 - p.x if (k >> 2) & 1 else p.x, 1 - p.y if (k >> 1) & 1 else p.y,
                          1 - p.c if k & 1 else p.c)

    return [_Copy(src, lambda p: None, dst, me, peer(k)) for k in range(1, N_DEV)]


def _place_own_slot(x, me_idx):
    r = x.shape[0]
    tr = _tile(r, (512, 256, 128, 64, 32, 16, 8))

    def body(me_ref, x_ref, o_ref):
        o_ref[...] = x_ref[...]

    return pl.pallas_call(
        body, name="place_own_slot",
        grid_spec=pltpu.PrefetchScalarGridSpec(
            num_scalar_prefetch=1, grid=(r // tr,),
            in_specs=[pl.BlockSpec((tr, LANES), lambda i, me: (i, 0))],
            out_specs=pl.BlockSpec((None, tr, LANES), lambda i, me: (me[0], i, 0))),
        out_shape=jax.ShapeDtypeStruct((N_DEV, r, LANES), F32),
        compiler_params=_params(1, parallel=1),
    )(me_idx, x)


def _all_to_all(x, name):
    def body(x_ref, y_ref, send_sems, recv_sems, own_sem):
        p = _my_place()
        me = 2 * p.chip + p.c
        own = pltpu.make_async_copy(x_ref.at[me], y_ref.at[me], own_sem)
        own.start()
        copies = []
        for k in range(1, N_DEV):
            px = 1 - p.x if (k >> 2) & 1 else p.x
            py = 1 - p.y if (k >> 1) & 1 else p.y
            pc = 1 - p.c if k & 1 else p.c
            peer = 4 * px + 2 * py + pc
            cp = pltpu.make_async_remote_copy(
                src_ref=x_ref.at[peer], dst_ref=y_ref.at[me],
                send_sem=send_sems.at[k - 1], recv_sem=recv_sems.at[k - 1],
                device_id=(px, py, pc), device_id_type=MESH_ID)
            cp.start()
            copies.append(cp)
        for cp in copies:
            cp.wait()
        own.wait()

    return pl.pallas_call(
        body, name=name,
        out_shape=jax.ShapeDtypeStruct(x.shape, x.dtype),
        in_specs=[HBM_SPEC], out_specs=HBM_SPEC,
        scratch_shapes=[pltpu.SemaphoreType.DMA((N_DEV - 1,)), pltpu.SemaphoreType.DMA((N_DEV - 1,)),
                        pltpu.SemaphoreType.DMA],
    )(x)


def _place_own_shard(w, layer, chip_idx):
    _, k, n = w.shape
    tr = _tile(k, (512, 256, 128))

    def body(c_ref, w_ref, g_ref):
        g_ref[...] = w_ref[...].astype(BF16)

    return pl.pallas_call(
        body, name="place_own_shard",
        grid_spec=pltpu.PrefetchScalarGridSpec(
            num_scalar_prefetch=1, grid=(k // tr,),
            in_specs=[pl.BlockSpec((None, tr, n), lambda r, c: (layer, r, 0))],
            out_specs=pl.BlockSpec((None, tr, n), lambda r, c: (c[0], r, 0))),
        out_shape=jax.ShapeDtypeStruct((N_CHIPS, k, n), BF16),
        compiler_params=_params(1, parallel=1),
    )(chip_idx, w)


def _matmul(name, operands, in_specs, grid, dims, out_shape, out_specs, epilogue, a_prologue=None,
            aliases=None, plan=None):
    n_in = len(operands)

    def body(*refs):
        a = refs[0][...]
        if a_prologue is not None:
            a = a_prologue(a)
        acc = lax.dot_general(a.astype(BF16), refs[1][...].astype(BF16), dims, preferred_element_type=F32)
        epilogue(acc, refs[2:n_in], refs[n_in:])

    return _call(body, name=name, grid=grid, in_specs=in_specs, out_specs=out_specs, out_shape=out_shape,
                 operands=operands, aliases=aliases, parallel=2, plan=plan)


def _store_cast(acc, extra, outs):
    outs[0][...] = acc.astype(outs[0].dtype)


def _in_proj(h, w, plan=None):
    s, d = h.shape
    nl = w.shape[-1]
    tm = _tile(s, (1024, 512, 256))
    tn = _tile(nl, (1024, 768, 512, 384, 256, 128))
    per = nl // tn
    return _matmul(
        "in_proj", (h, w),
        [pl.BlockSpec((tm, d), lambda i, j: (i, 0)),
         pl.BlockSpec((None, d, tn), lambda i, j: (j // per, 0, j % per))],
        (s // tm, N_CHIPS * per), NN,
        jax.ShapeDtypeStruct((s, N_CHIPS * nl), BF16),
        pl.BlockSpec((tm, tn), lambda i, j: (i, j)), _store_cast, plan=plan)


def _modulated_norm(xv, g, scale, shift):
    rstd = lax.rsqrt(jnp.mean(xv * xv, axis=-1, keepdims=True) + EPS)
    return ((xv * rstd) * g * (1.0 + scale) + shift).astype(BF16)


def _out_proj_residual(y, w2, x, gate, next_norm, plan=None):
    s, wdt = y.shape
    d = w2.shape[-1]
    tm = _tile(s, (256, 128))

    def epilogue(acc, extra, outs):
        x_new = extra[0][...] + extra[1][...] * acc
        outs[0][...] = x_new
        outs[1][...] = acc.astype(BF16)
        if next_norm is not None:
            outs[2][...] = _modulated_norm(x_new, extra[2][...], extra[3][...], extra[4][...])

    rows = pl.BlockSpec((tm, d), lambda i, j: (i, 0))
    vec = pl.BlockSpec((1, d), lambda i, j: (0, 0))
    n_vec = 1 + (3 if next_norm is not None else 0)
    n_act = 1 + (1 if next_norm is not None else 0)
    return _matmul(
        "out_proj", (y, w2, x, gate) + (tuple(next_norm) if next_norm is not None else ()),
        [pl.BlockSpec((tm, wdt), lambda i, j: (i, 0)),
         pl.BlockSpec((wdt, d), lambda i, j: (0, 0), pipeline_mode=pl.Buffered(1)),
         rows] + [vec] * n_vec,
        (s // tm, 1), NN,
        [jax.ShapeDtypeStruct((s, d), F32)] + [jax.ShapeDtypeStruct((s, d), BF16)] * n_act,
        [rows] * (1 + n_act), epilogue, plan=plan)


def _out_proj_bwd_act(dout, w2, plan=None):
    s, d = dout.shape
    wdt = w2.shape[0]
    tm = _tile(s, (1024, 512, 256))
    tn = _tile(wdt, (1024, 512, 256, 128))
    return _matmul(
        "out_proj_dy", (dout, w2),
        [pl.BlockSpec((tm, d), lambda i, j: (i, 0)),
         pl.BlockSpec((tn, d), lambda i, j: (j, 0))],
        (s // tm, wdt // tn), NT,
        jax.ShapeDtypeStruct((s, wdt), BF16),
        pl.BlockSpec((tm, tn), lambda i, j: (i, j)), _store_cast, plan=plan)


def _out_proj_bwd_w(y, dout, plan=None):
    s, wdt = y.shape
    d = dout.shape[1]
    tm = _tile(wdt, (1024, 512, 256, 128))
    tn = _tile(d, (1024, 512, 256, 128))
    return _matmul(
        "out_proj_dw", (y, dout),
        [pl.BlockSpec((s, tm), lambda i, j: (0, i)),
         pl.BlockSpec((s, tn), lambda i, j: (0, j))],
        (wdt // tm, d // tn), TN,
        jax.ShapeDtypeStruct((wdt, d), BF16),
        pl.BlockSpec((tm, tn), lambda i, j: (i, j)), _store_cast, plan=plan)


def _in_proj_bwd_act(dproj, w, plan=None):
    s, n_all = dproj.shape
    d, nl = w.shape[1], w.shape[2]
    tm = _tile(s, (512, 256))
    tn = _tile(d, (512, 256, 128))

    def body(a_ref, w_ref, o_ref):
        acc = None
        for q in range(N_CHIPS):
            part = lax.dot_general(a_ref[:, q * nl:(q + 1) * nl], w_ref[q], NT, preferred_element_type=F32)
            acc = part if acc is None else acc + part
        o_ref[...] = acc.astype(BF16)

    return _call(
        body, name="in_proj_dh", grid=(s // tm, d // tn),
        in_specs=[pl.BlockSpec((tm, n_all), lambda i, j: (i, 0)),
                  pl.BlockSpec((N_CHIPS, tn, nl), lambda i, j: (0, j, 0))],
        out_specs=pl.BlockSpec((tm, tn), lambda i, j: (i, j)),
        out_shape=jax.ShapeDtypeStruct((s, d), BF16),
        operands=(dproj, w), parallel=2, plan=plan)


def _in_proj_bwd_w(h, dproj, nl, plan=None):
    s, d = h.shape
    tm = _tile(d, (1024, 512, 256, 128))
    tn = _tile(nl, (1024, 768, 512, 384, 256, 128))
    per = nl // tn
    return _matmul(
        "in_proj_dw", (h, dproj),
        [pl.BlockSpec((s, tm), lambda i, j: (0, i)),
         pl.BlockSpec((s, tn), lambda i, j: (0, j))],
        (d // tm, N_CHIPS * per), TN,
        jax.ShapeDtypeStruct((N_CHIPS, d, nl), BF16),
        pl.BlockSpec((None, tm, tn), lambda i, j: (j // per, i, j % per)), _store_cast, plan=plan)


def _mod_fwd(c_all, w_mod, bias, layer):
    nb, d = c_all.shape
    nl = w_mod.shape[-1]
    tn = _tile(nl, (768, 512, 384, 256, 128))

    def epilogue(acc, extra, outs):
        outs[0][...] = acc + extra[0][...]

    return _matmul(
        "mod_fwd", (c_all, w_mod, bias),
        [pl.BlockSpec((nb, d), lambda i, j: (0, 0)),
         pl.BlockSpec((None, d, tn), lambda i, j: (layer, 0, j)),
         pl.BlockSpec((1, tn), lambda i, j: (0, j))],
        (1, nl // tn), NN,
        jax.ShapeDtypeStruct((nb, nl), F32),
        pl.BlockSpec((nb, tn), lambda i, j: (0, j)), epilogue, a_prologue=_silu)


def _mod_bwd_w(c_all, dm_pair):
    nb, d = c_all.shape
    nl = dm_pair.shape[-1]
    tm = _tile(d, (1024, 512, 256, 128))
    tn = _tile(nl, (768, 512, 384, 256, 128))

    def body(c_ref, dm_ref, o_ref):
        o_ref[...] = lax.dot_general(_silu(c_ref[...]).astype(BF16), dm_ref[...].astype(BF16), TN,
                                     preferred_element_type=F32)

    return _call(
        body, name="mod_dw", grid=(2, d // tm, nl // tn),
        in_specs=[pl.BlockSpec((nb, tm), lambda l, i, j: (0, i)),
                  pl.BlockSpec((None, nb, tn), lambda l, i, j: (l, 0, j))],
        out_specs=pl.BlockSpec((None, tm, tn), lambda l, i, j: (l, i, j)),
        out_shape=jax.ShapeDtypeStruct((2, d, nl), F32),
        operands=(c_all, dm_pair), parallel=3)


def _rows_call(name, body, operands, in_specs, out_shape, out_specs, n_tiles):
    return pl.pallas_call(
        body, name=name, grid=(n_tiles,), in_specs=in_specs, out_specs=out_specs, out_shape=out_shape,
        compiler_params=_params(1),
    )(*operands)


def _row_spec(tr, width):
    return pl.BlockSpec((tr, width), lambda i: (i, 0))


def _vec_spec(width):
    return pl.BlockSpec((1, width), lambda i: (0, 0))


def _accumulate(ref, val):
    first = pl.program_id(0) == 0

    @pl.when(first)
    def _():
        ref[...] = val

    @pl.when(jnp.logical_not(first))
    def _():
        ref[...] += val


def _prenorm(x, g, scale, shift):
    s, d = x.shape
    tr = _tile(s, (256, 128))

    def body(x_ref, g_ref, sc_ref, sh_ref, h_ref):
        h_ref[...] = _modulated_norm(x_ref[...], g_ref[...], sc_ref[...], sh_ref[...])

    return _rows_call("prenorm", body, (x, g, scale, shift),
                      [_row_spec(tr, d), _vec_spec(d), _vec_spec(d), _vec_spec(d)],
                      jax.ShapeDtypeStruct((s, d), BF16), _row_spec(tr, d), s // tr)


def _gate_grads(dxv, out_ref, gate_ref, dout_ref, dgate_ref):
    dout_ref[...] = (gate_ref[...] * dxv).astype(BF16)
    _accumulate(dgate_ref, jnp.sum(dxv * out_ref[...].astype(F32), axis=0, keepdims=True))


def _prenorm_bwd(x, dh, dres, g, scale, below=None):
    s, d = x.shape
    tr = _tile(s, (512, 256, 128))

    def body(x_ref, dh_ref, dres_ref, g_ref, sc_ref, *rest):
        dx_ref, dshift_ref, dscale_ref, dg_ref = rest[-6:-2] if below else rest
        xv = x_ref[...]
        dhv = dh_ref[...].astype(F32)
        rstd = lax.rsqrt(jnp.mean(xv * xv, axis=-1, keepdims=True) + EPS)
        xhat = xv * rstd
        gv = g_ref[...]
        one_sc = 1.0 + sc_ref[...]
        dxhat = dhv * gv * one_sc
        dxv = dres_ref[...] + rstd * (dxhat - xhat * jnp.mean(dxhat * xhat, axis=-1, keepdims=True))
        dx_ref[...] = dxv
        _accumulate(dshift_ref, jnp.sum(dhv, axis=0, keepdims=True))
        _accumulate(dscale_ref, jnp.sum(dhv * xhat * gv, axis=0, keepdims=True))
        _accumulate(dg_ref, jnp.sum(dhv * xhat * one_sc, axis=0, keepdims=True))
        if below:
            _gate_grads(dxv, rest[0], rest[1], rest[-2], rest[-1])

    vec = jax.ShapeDtypeStruct((1, d), F32)
    operands = (x, dh, dres, g, scale) + (tuple(below) if below else ())
    in_specs = [_row_spec(tr, d), _row_spec(tr, d), _row_spec(tr, d), _vec_spec(d), _vec_spec(d)]
    out_shape = [jax.ShapeDtypeStruct((s, d), F32), vec, vec, vec]
    out_specs = [_row_spec(tr, d), _vec_spec(d), _vec_spec(d), _vec_spec(d)]
    if below:
        in_specs += [_row_spec(tr, d), _vec_spec(d)]
        out_shape += [jax.ShapeDtypeStruct((s, d), BF16), vec]
        out_specs += [_row_spec(tr, d), _vec_spec(d)]
    return _rows_call("prenorm_bwd", body, operands, in_specs, out_shape, out_specs, s // tr)


def _final_loss(x, target, g, out_below, gate_below):
    s, d = x.shape
    tr = _tile(s, (512, 256, 128))
    n_tiles = s // tr

    def body(x_ref, t_ref, g_ref, out_ref, gate_ref, loss_ref, dx_ref, dg_ref, dout_ref, dgate_ref, acc_ref):
        xv = x_ref[...]
        rstd = lax.rsqrt(jnp.mean(xv * xv, axis=-1, keepdims=True) + EPS)
        xhat = xv * rstd
        gv = g_ref[...]
        err = xhat * gv - t_ref[...]
        dy = err * (1.0 / d)
        dxhat = dy * gv
        dxv = rstd * (dxhat - xhat * jnp.mean(dxhat * xhat, axis=-1, keepdims=True))
        dx_ref[...] = dxv
        _accumulate(dg_ref, jnp.sum(dy * xhat, axis=0, keepdims=True))
        _accumulate(acc_ref, jnp.sum(err * err, axis=0, keepdims=True))
        _gate_grads(dxv, out_ref, gate_ref, dout_ref, dgate_ref)

        @pl.when(pl.program_id(0) == n_tiles - 1)
        def _():
            loss_ref[...] = (0.5 / d) * jnp.sum(acc_ref[...], axis=1, keepdims=True)

    vec = jax.ShapeDtypeStruct((1, d), F32)
    return pl.pallas_call(
        body, name="final_loss", grid=(n_tiles,),
        in_specs=[_row_spec(tr, d), _row_spec(tr, d), _vec_spec(d), _row_spec(tr, d), _vec_spec(d)],
        out_specs=[pl.BlockSpec((1, 1), lambda i: (0, 0)), _row_spec(tr, d), _vec_spec(d), _row_spec(tr, d),
                   _vec_spec(d)],
        out_shape=[jax.ShapeDtypeStruct((1, 1), F32), jax.ShapeDtypeStruct((s, d), F32), vec,
                   jax.ShapeDtypeStruct((s, d), BF16), vec],
        scratch_shapes=[pltpu.VMEM((1, d), F32)],
        compiler_params=_params(1),
    )(x, target, g, out_below, gate_below)


def _rope(t, cos, sin):
    return t * cos + pltpu.roll(t, HEAD_DIM // 2, axis=1) * sin


def _unrope(dt, cos, sin):
    return dt * cos + pltpu.roll(dt * sin, HEAD_DIM // 2, axis=1)


def _band_blocks(s, dil):
    sub = s // dil
    kw = min(K_WINDOW, sub)

    def rows(r, start, n):
        if dil == 1:
            return pl.ds(pl.multiple_of(start, RADIUS), n)
        return pl.ds(r + dil * start, n, stride=dil)

    def window(idx):
        nb = sub // Q_BLOCK
        r, b = idx // nb, idx % nb
        q0 = b * Q_BLOCK
        start = jnp.clip(q0 - RADIUS, 0, sub - kw)
        ahead = (lax.broadcasted_iota(jnp.int32, (Q_BLOCK, kw), 1)
                 - lax.broadcasted_iota(jnp.int32, (Q_BLOCK, kw), 0)) + (start - q0 + RADIUS)
        valid = lax.bitcast_convert_type(ahead, jnp.uint32) <= 2 * RADIUS
        return rows(r, q0, Q_BLOCK), rows(r, start, kw), valid

    return window


def _store_column_tiles(tiles, dst_ref, sems, col_blocks):
    rows = tiles.shape[1]
    copies = []
    for g, cb in enumerate(col_blocks):
        cols = pl.ds(pl.multiple_of(cb * LANES, LANES), LANES)
        cp = pltpu.make_async_copy(tiles.at[g], dst_ref.at[pl.ds(0, rows), cols], sems.at[g])
        cp.start()
        copies.append(cp)
    for cp in copies:
        cp.wait()


def _head_col(s, group, nh):
    return pl.BlockSpec((s, HEAD_DIM), lambda h: (0, group * nh + h), pipeline_mode=pl.Buffered(1))


def _attn_fwd(proj, cos, sin, aw, plan=None):
    s = proj.shape[0]
    nh = aw // HEAD_DIM
    scale = HEAD_DIM ** -0.5
    n_blocks = s // Q_BLOCK

    def body(q_ref, k_ref, v_ref, cos_ref, sin_ref, attn_ref, lse_ref, qf, kf, vf, acc):
        cosv, sinv = cos_ref[...], sin_ref[...]
        qf[...] = _rope(q_ref[...].astype(F32), cosv, sinv) * scale
        kf[...] = _rope(k_ref[...].astype(F32), cosv, sinv)
        vf[...] = v_ref[...].astype(F32)

        for pattern, dil in enumerate(DILATIONS):
            window = _band_blocks(s, dil)

            def block(idx, carry, window=window, first=(pattern == 0)):
                q_rows, k_rows, valid = window(idx)
                q = qf[q_rows, :].astype(BF16)
                kk = kf[k_rows, :].astype(BF16)
                vv = vf[k_rows, :].astype(BF16)
                sc = lax.dot_general(q, kk, NT, preferred_element_type=F32)
                sc = jnp.where(valid, sc, NEG_INF)
                m = jnp.max(sc, axis=1, keepdims=True)
                p = jnp.exp(sc - m)
                den = jnp.sum(p, axis=1, keepdims=True)
                o = lax.dot_general(p.astype(BF16), vv, NN, preferred_element_type=F32) / den
                lse = jnp.broadcast_to(m + jnp.log(den), (Q_BLOCK, HEAD_DIM))
                if first:
                    acc[q_rows, :] = o
                    lse_ref[q_rows, :] = lse
                else:
                    lse_old = lse_ref[q_rows, :]
                    top = jnp.maximum(lse_old, lse)
                    w_old, w_new = jnp.exp(lse_old - top), jnp.exp(lse - top)
                    tot = w_old + w_new
                    acc[q_rows, :] = (acc[q_rows, :] * w_old + o * w_new) / tot
                    lse_ref[q_rows, :] = top + jnp.log(tot)
                return carry

            lax.fori_loop(0, n_blocks, block, 0, unroll=ATTN_UNROLL)

        attn_ref[...] = acc[...].astype(BF16)

    table = pl.BlockSpec((s, HEAD_DIM), lambda h: (0, 0), pipeline_mode=pl.Buffered(1))
    out = pl.BlockSpec((s, HEAD_DIM), lambda h: (0, h))
    return _call(
        body, name="attn_fwd", grid=(nh,),
        in_specs=[_head_col(s, 0, nh), _head_col(s, 1, nh), _head_col(s, 2, nh), table, table],
        out_specs=[out, out],
        out_shape=[jax.ShapeDtypeStruct((s, aw), BF16), jax.ShapeDtypeStruct((s, aw), F32)],
        scratch_shapes=[pltpu.VMEM((s, HEAD_DIM), F32)] * 4,
        operands=(proj, proj, proj, cos, sin), parallel=1, plan=plan)


def _attn_bwd(proj, cos, sin, dy, attn, lse, aw, plan=None):
    s = proj.shape[0]
    nh = aw // HEAD_DIM
    scale = HEAD_DIM ** -0.5
    n_blocks = s // Q_BLOCK

    def body(q_ref, k_ref, v_ref, za_ref, cos_ref, sin_ref, dy_ref, attn_ref, lse_ref,
             dproj_ref, qf, kf, vf, dof, delta, dqa, dka, dva, tiles, tile_sems):
        cosv, sinv = cos_ref[...], sin_ref[...]
        qf[...] = _rope(q_ref[...].astype(F32), cosv, sinv) * scale
        kf[...] = _rope(k_ref[...].astype(F32), cosv, sinv)
        vf[...] = v_ref[...].astype(F32)
        dyv, zav, attnv = dy_ref[...].astype(F32), za_ref[...].astype(F32), attn_ref[...].astype(F32)
        silu_za, dsilu_za = _silu_and_grad(zav)
        do_all = dyv * silu_za
        dof[...] = do_all
        tiles[3] = (dyv * attnv * dsilu_za).astype(BF16)
        delta[...] = jnp.broadcast_to(jnp.sum(do_all * attnv, axis=1, keepdims=True), (s, HEAD_DIM))
        dqa[...] = jnp.zeros_like(dqa)
        dka[...] = jnp.zeros_like(dka)
        dva[...] = jnp.zeros_like(dva)

        for dil in DILATIONS:
            window = _band_blocks(s, dil)

            def block(idx, carry, window=window):
                q_rows, k_rows, valid = window(idx)
                q = qf[q_rows, :].astype(BF16)
                kk = kf[k_rows, :].astype(BF16)
                vv = vf[k_rows, :].astype(BF16)
                dov = dof[q_rows, :].astype(BF16)
                lse_q = lse_ref[q_rows, :][:, 0:1]
                delta_q = delta[q_rows, :][:, 0:1]
                sc = lax.dot_general(q, kk, NT, preferred_element_type=F32)
                p = jnp.where(valid, jnp.exp(sc - lse_q), 0.0)
                dp = lax.dot_general(dov, vv, NT, preferred_element_type=F32)
                ds = (p * (dp - delta_q)).astype(BF16)
                dqa[q_rows, :] += lax.dot_general(ds, kk, NN, preferred_element_type=F32)
                dka[k_rows, :] += lax.dot_general(ds, q, TN, preferred_element_type=F32)
                dva[k_rows, :] += lax.dot_general(p.astype(BF16), dov, TN, preferred_element_type=F32)
                return carry

            lax.fori_loop(0, n_blocks, block, 0, unroll=ATTN_UNROLL)

        tiles[0] = (_unrope(dqa[...], cosv, sinv) * scale).astype(BF16)
        tiles[1] = _unrope(dka[...], cosv, sinv).astype(BF16)
        tiles[2] = dva[...].astype(BF16)
        _store_column_tiles(tiles, dproj_ref, tile_sems, [g * nh + pl.program_id(0) for g in range(4)])

    own = pl.BlockSpec((s, HEAD_DIM), lambda h: (0, h), pipeline_mode=pl.Buffered(1))
    table = pl.BlockSpec((s, HEAD_DIM), lambda h: (0, 0), pipeline_mode=pl.Buffered(1))
    return _call(
        body, name="attn_bwd", grid=(nh,),
        in_specs=[_head_col(s, 0, nh), _head_col(s, 1, nh), _head_col(s, 2, nh), _head_col(s, 3, nh),
                  table, table, own, own, own],
        out_specs=HBM_SPEC,
        out_shape=jax.ShapeDtypeStruct((s, 8 * aw), BF16),
        scratch_shapes=[pltpu.VMEM((s, HEAD_DIM), F32)] * 8 + [
            pltpu.VMEM((4, s, HEAD_DIM), BF16), pltpu.SemaphoreType.DMA((4,))],
        operands=(proj, proj, proj, proj, cos, sin, dy, attn, lse), plan=plan)


def _rope_tables(s):
    half = HEAD_DIM // 2
    inv = np.float32(ROPE_THETA) ** (-np.arange(half, dtype=np.float32) / np.float32(half))
    ang = np.arange(s, dtype=np.float32)[:, None] * inv[None, :]
    cos, sin = np.cos(ang), np.sin(ang)
    return (jnp.asarray(np.concatenate([cos, cos], axis=-1), F32),
            jnp.asarray(np.concatenate([-sin, sin], axis=-1), F32))


def _conv_chunks(s):
    for k in range(s // CONV_ROWS):
        lo = max(0, k * CONV_ROWS - CONV_HALO)
        hi = min(s, (k + 1) * CONV_ROWS + CONV_HALO)
        yield k * CONV_ROWS, lo, hi


def _neighbours(p, lo, s):
    n = p.shape[0]
    row = lo + lax.broadcasted_iota(jnp.int32, p.shape, 0)
    prev = jnp.where(row == 0, 0.0, pltpu.roll(p, 1, axis=0))
    nxt = jnp.where(row == s - 1, 0.0, pltpu.roll(p, n - 1, axis=0))
    return prev, nxt


def _ab_mix(attn, proj, conv_w, aw, plan=None):
    s = proj.shape[0]
    nt = aw // LANES

    def col(group, sel):
        return pl.BlockSpec((s, LANES), lambda i: (0, group * nt + sel(i)))

    a_sel = lambda i: jnp.minimum(i, nt - 1)
    b_sel = lambda i: jnp.maximum(i - nt, 0)

    def body(attn_ref, za_ref, ub_ref, gb_ref, gc_ref, zb_ref, w_ref, y_ref):
        i = pl.program_id(0)

        @pl.when(i < nt)
        def _():
            y_ref[...] = (attn_ref[...].astype(F32) * _silu(za_ref[...].astype(F32))).astype(BF16)

        @pl.when(i >= nt)
        def _():
            w = w_ref[...]
            for c0, lo, hi in _conv_chunks(s):
                p = gc_ref[lo:hi, :].astype(F32) * ub_ref[lo:hi, :].astype(F32)
                prev, nxt = _neighbours(p, lo, s)
                cv = w[0:1, :] * prev + w[1:2, :] * p + w[2:3, :] * nxt
                yb = gb_ref[lo:hi, :].astype(F32) * cv * _silu(zb_ref[lo:hi, :].astype(F32))
                y_ref[c0:c0 + CONV_ROWS, :] = yb[c0 - lo:c0 - lo + CONV_ROWS, :].astype(BF16)

    return _call(
        body, name="ab_mix", grid=(2 * nt,),
        in_specs=[pl.BlockSpec((s, LANES), lambda i: (0, a_sel(i))),
                  col(3, a_sel), col(4, b_sel), col(5, b_sel), col(6, b_sel), col(7, b_sel),
                  pl.BlockSpec((3, LANES), lambda i: (0, b_sel(i)))],
        out_specs=pl.BlockSpec((s, LANES), lambda i: (0, i)),
        out_shape=jax.ShapeDtypeStruct((s, 2 * aw), BF16),
        operands=(attn, proj, proj, proj, proj, proj, conv_w), plan=plan)


def _conv_bwd(dproj, dy, proj, conv_w, aw):
    s = proj.shape[0]
    nt = aw // LANES

    def col(group):
        return pl.BlockSpec((s, LANES), lambda i: (0, group * nt + i))

    def body(dyb_ref, ub_ref, gb_ref, gc_ref, zb_ref, w_ref, dproj_in, dproj_ref, dw_ref, tiles, tile_sems):
        w = w_ref[...]
        dw = [jnp.zeros((1, LANES), F32) for _ in range(3)]
        for c0, lo, hi in _conv_chunks(s):
            ctr = slice(c0 - lo, c0 - lo + CONV_ROWS)
            out_rows = slice(c0, c0 + CONV_ROWS)
            ub = ub_ref[lo:hi, :].astype(F32)
            gc = gc_ref[lo:hi, :].astype(F32)
            gb = gb_ref[lo:hi, :].astype(F32)
            zb = zb_ref[lo:hi, :].astype(F32)
            dyb = dyb_ref[lo:hi, :].astype(F32)
            p = gc * ub
            prev, nxt = _neighbours(p, lo, s)
            cv = w[0:1, :] * prev + w[1:2, :] * p + w[2:3, :] * nxt
            sz, dsz = _silu_and_grad(zb)
            dcv = dyb * gb * sz
            dprev, dnxt = _neighbours(dcv, lo, s)
            dp = w[0:1, :] * dnxt + w[1:2, :] * dcv + w[2:3, :] * dprev
            for t, nb in enumerate((prev, p, nxt)):
                dw[t] = dw[t] + jnp.sum((dcv * nb)[ctr, :], axis=0, keepdims=True)
            tiles[0, out_rows, :] = (dp * gc)[ctr, :].astype(BF16)
            tiles[1, out_rows, :] = (dyb * cv * sz)[ctr, :].astype(BF16)
            tiles[2, out_rows, :] = (dp * ub)[ctr, :].astype(BF16)
            tiles[3, out_rows, :] = (dyb * gb * cv * dsz)[ctr, :].astype(BF16)
        dw_ref[...] = jnp.concatenate(dw, axis=0)
        _store_column_tiles(tiles, dproj_ref, tile_sems, [(4 + g) * nt + pl.program_id(0) for g in range(4)])

    return pl.pallas_call(
        body, name="conv_bwd", grid=(nt,),
        in_specs=[pl.BlockSpec((s, LANES), lambda i: (0, nt + i)),
                  col(4), col(5), col(6), col(7),
                  pl.BlockSpec((3, LANES), lambda i: (0, i)), HBM_SPEC],
        out_specs=[HBM_SPEC, pl.BlockSpec((3, LANES), lambda i: (0, i))],
        out_shape=[jax.ShapeDtypeStruct(dproj.shape, dproj.dtype), jax.ShapeDtypeStruct((3, aw), F32)],
        scratch_shapes=[pltpu.VMEM((4, s, LANES), BF16), pltpu.SemaphoreType.DMA((4,))],
        input_output_aliases={6: 0},
        compiler_params=_params(1),
    )(dy, proj, proj, proj, proj, conv_w, dproj)


def _sgu_norm(gv, ln_g, ln_b):
    mu = jnp.mean(gv, axis=-1, keepdims=True)
    xc = gv - mu
    rstd = lax.rsqrt(jnp.mean(xc * xc, axis=-1, keepdims=True) + EPS)
    vhat = xc * rstd
    return vhat, rstd, vhat * ln_g + ln_b


def _sgu_fwd(uvz, ln_g, ln_b, w_s, b_s, cw):
    s = uvz.shape[0]
    tr = 2 * CHUNK if s % (2 * CHUNK) == 0 else CHUNK
    gw = cw // N_GROUPS

    def body(u_ref, v_ref, z_ref, g_ref, b_ref, ws_ref, bs_ref, y_ref):
        _, _, vn = _sgu_norm(_gelu(v_ref[...].astype(F32)), g_ref[...], b_ref[...])
        vn = vn.astype(BF16)
        for ch in range(tr // CHUNK):
            rows = slice(ch * CHUNK, (ch + 1) * CHUNK)
            for grp in range(N_GROUPS):
                cols = slice(grp * gw, (grp + 1) * gw)
                mixed = lax.dot_general(ws_ref[grp], vn[rows, cols], NN, preferred_element_type=F32) + bs_ref[grp]
                y_ref[rows, cols] = (_gelu(u_ref[rows, cols].astype(F32)) * mixed
                                     * _silu(z_ref[rows, cols].astype(F32))).astype(BF16)

    full3 = lambda shape: pl.BlockSpec(shape, lambda i: (0, 0, 0))
    return pl.pallas_call(
        body, name="sgu_fwd", grid=(s // tr,),
        in_specs=[pl.BlockSpec((tr, cw), lambda i: (i, 0)), pl.BlockSpec((tr, cw), lambda i: (i, 1)),
                  pl.BlockSpec((tr, cw), lambda i: (i, 2)), _vec_spec(cw), _vec_spec(cw),
                  full3(w_s.shape), full3(b_s.shape)],
        out_specs=pl.BlockSpec((tr, cw), lambda i: (i, 0)),
        out_shape=jax.ShapeDtypeStruct((s, cw), BF16),
        compiler_params=_params(1, parallel=1),
    )(uvz, uvz, uvz, ln_g, ln_b, w_s, b_s)


def _sgu_bwd(uvz, dy, ln_g, ln_b, w_s, b_s, cw, plan=None):
    s = uvz.shape[0]
    tr = 2 * CHUNK if s % (2 * CHUNK) == 0 else CHUNK
    gw = cw // N_GROUPS

    def body(u_ref, v_ref, z_ref, dy_ref, g_ref, b_ref, ws_ref, bs_ref,
             duvz_ref, dws_ref, dbs_ref, dg_ref, db_ref, dvn_ref):
        vv = v_ref[...].astype(F32)
        gvec = g_ref[...]
        gelu_v, dgelu_v = _gelu_and_grad(vv)
        vhat, rstd, vn = _sgu_norm(gelu_v, gvec, b_ref[...])
        vn = vn.astype(BF16)
        first = pl.program_id(0) == 0

        @pl.when(first)
        def _():
            dws_ref[...] = jnp.zeros_like(dws_ref)
            dbs_ref[...] = jnp.zeros_like(dbs_ref)

        for ch in range(tr // CHUNK):
            rows = slice(ch * CHUNK, (ch + 1) * CHUNK)
            for grp in range(N_GROUPS):
                cols = slice(grp * gw, (grp + 1) * gw)
                vn_g = vn[rows, cols]
                mixed = lax.dot_general(ws_ref[grp], vn_g, NN, preferred_element_type=F32) + bs_ref[grp]
                uu = u_ref[rows, cols].astype(F32)
                zz = z_ref[rows, cols].astype(F32)
                dyv = dy_ref[rows, cols].astype(F32)
                (gu, dgu), (sz, dsz) = _gelu_and_grad(uu), _silu_and_grad(zz)
                duvz_ref[rows, grp * gw:(grp + 1) * gw] = (dyv * mixed * sz * dgu).astype(BF16)
                duvz_ref[rows, 2 * cw + grp * gw:2 * cw + (grp + 1) * gw] = (dyv * gu * mixed * dsz).astype(BF16)
                dmixed = dyv * gu * sz
                dm16 = dmixed.astype(BF16)
                dws_ref[grp] += lax.dot_general(dm16, vn_g, NT, preferred_element_type=F32)
                dbs_ref[grp] += jnp.broadcast_to(jnp.sum(dmixed, axis=1, keepdims=True), (CHUNK, LANES))
                dvn_ref[rows, cols] = lax.dot_general(ws_ref[grp], dm16, TN, preferred_element_type=F32)

        dvn = dvn_ref[...]
        _accumulate(dg_ref, jnp.sum(dvn * vhat, axis=0, keepdims=True))
        _accumulate(db_ref, jnp.sum(dvn, axis=0, keepdims=True))
        dvhat = dvn * gvec
        dgv = rstd * (dvhat - jnp.mean(dvhat, axis=-1, keepdims=True)
                      - vhat * jnp.mean(dvhat * vhat, axis=-1, keepdims=True))
        duvz_ref[:, cw:2 * cw] = (dgv * dgelu_v).astype(BF16)

    full3 = lambda shape: pl.BlockSpec(shape, lambda i: (0, 0, 0))
    acc3 = jax.ShapeDtypeStruct((N_GROUPS, CHUNK, LANES), F32)
    vec = jax.ShapeDtypeStruct((1, cw), F32)
    row = pl.BlockSpec((tr, cw), lambda i: (i, 0))
    return _call(
        body, name="sgu_bwd", grid=(s // tr,),
        in_specs=[row, pl.BlockSpec((tr, cw), lambda i: (i, 1)), pl.BlockSpec((tr, cw), lambda i: (i, 2)),
                  row, _vec_spec(cw), _vec_spec(cw), full3(w_s.shape), full3(b_s.shape)],
        out_specs=[pl.BlockSpec((tr, 3 * cw), lambda i: (i, 0)), full3((N_GROUPS, CHUNK, LANES)),
                   full3((N_GROUPS, CHUNK, LANES)), _vec_spec(cw), _vec_spec(cw)],
        out_shape=[jax.ShapeDtypeStruct((s, 3 * cw), BF16), acc3, acc3, vec, vec],
        scratch_shapes=[pltpu.VMEM((tr, cw), F32)],
        operands=(uvz, uvz, uvz, dy, ln_g, ln_b, w_s, b_s), plan=plan)


def _flat_rows(a):
    return a.reshape(-1, a.shape[-1])


def _add_sibling(grad, recv, core_idx):
    nchip, k, n = grad.shape
    tr = _tile(k // 2, (256, 128))
    nb = (k // 2) // tr

    def body(c_ref, g_ref, r_ref, o_ref):
        o_ref[...] = (g_ref[...].astype(F32) + r_ref[...].astype(F32)).astype(BF16)

    return pl.pallas_call(
        body, name="add_sibling",
        grid_spec=pltpu.PrefetchScalarGridSpec(
            num_scalar_prefetch=1, grid=(nchip, nb),
            in_specs=[pl.BlockSpec((None, tr, n), lambda q, i, c: (q, c[0] * nb + i, 0)),
                      pl.BlockSpec((None, tr, n), lambda q, i, c: (q, i, 0))],
            out_specs=pl.BlockSpec((None, tr, n), lambda q, i, c: (q, i, 0))),
        out_shape=jax.ShapeDtypeStruct((nchip, k // 2, n), BF16),
        compiler_params=_params(2, parallel=2),
    )(core_idx, grad, recv)


def _sum_chips(own, others, reduced, layer, place_idx):
    _, kh, n = own.shape
    tr = _tile(kh, (256, 128))
    nb = kh // tr

    def body(place_ref, own_ref, oth_ref, red_ref, o_ref):
        acc = own_ref[...].astype(F32)
        for q in range(3):
            acc = acc + oth_ref[q].astype(F32)
        o_ref[...] = acc

    return pl.pallas_call(
        body, name="sum_chips",
        grid_spec=pltpu.PrefetchScalarGridSpec(
            num_scalar_prefetch=1, grid=(nb,),
            in_specs=[pl.BlockSpec((None, tr, n), lambda i, p: (p[0], i, 0)),
                      pl.BlockSpec((3, tr, n), lambda i, p: (0, i, 0)),
                      HBM_SPEC],
            out_specs=pl.BlockSpec((None, tr, n), lambda i, p: (layer, p[1] * nb + i, 0))),
        out_shape=jax.ShapeDtypeStruct(reduced.shape, reduced.dtype),
        input_output_aliases={3: 0},
        compiler_params=_params(1, parallel=1),
    )(place_idx, own, others, reduced)


def _sum_devices(parts, plan=None):
    nd, r, _ = parts.shape
    tr = _tile(r, (512, 256, 128, 64, 32, 16, 8))

    def body(p_ref, o_ref):
        acc = p_ref[0]
        for q in range(1, nd):
            acc = acc + p_ref[q]
        o_ref[...] = acc

    return _call(
        body, name="sum_devices", grid=(r // tr,),
        in_specs=[pl.BlockSpec((nd, tr, LANES), lambda i: (0, i, 0))],
        out_specs=pl.BlockSpec((tr, LANES), lambda i: (i, 0)),
        out_shape=jax.ShapeDtypeStruct((r, LANES), F32),
        operands=(parts,), parallel=1, plan=plan)


def _adamw(w, g, m, v, also_grad=False):
    r, n = w.shape
    tr = _tile(r, [p for p in (1024, 512, 256, 128, 64, 32, 16, 8) if p * n <= ELEMENTWISE_BLOCK])
    n_out = 4 if also_grad else 3

    def body(w_ref, g_ref, m_ref, v_ref, d_ref, nm_ref, nv_ref, *g_out):
        gv = g_ref[...]
        if also_grad:
            g_out[0][...] = gv
        nm = ADAM_B1 * m_ref[...] + (1.0 - ADAM_B1) * gv
        nv = ADAM_B2 * v_ref[...] + (1.0 - ADAM_B2) * (gv * gv)
        m_hat = nm / (1.0 - ADAM_B1 ** ADAM_STEP)
        v_hat = nv / (1.0 - ADAM_B2 ** ADAM_STEP)
        d_ref[...] = -ADAM_LR * (m_hat / (jnp.sqrt(v_hat) + ADAM_EPS) + ADAM_WD * w_ref[...])
        nm_ref[...] = nm
        nv_ref[...] = nv

    spec = pl.BlockSpec((tr, n), lambda i: (i, 0))
    shp = jax.ShapeDtypeStruct((r, n), F32)
    return _call(
        body, name="adamw", grid=(r // tr,),
        in_specs=[spec] * 4, out_specs=[spec] * n_out, out_shape=[shp] * n_out,
        operands=(w, g, m, v), parallel=1)


def _pack(arrays, row_multiple=8):
    flat = [a.reshape(-1) for a in arrays]
    sizes = [f.shape[0] for f in flat]
    total = sum(sizes)
    unit = LANES * row_multiple
    padded = -(-total // unit) * unit
    if padded > total:
        flat.append(jnp.zeros((padded - total,), F32))
    offsets = [sum(sizes[:i]) for i in range(len(sizes))]
    return jnp.concatenate(flat).reshape(-1, LANES), offsets


def _unpack(packed, offsets, shapes):
    flat = packed.reshape(-1)
    return [flat[o:o + math.prod(s)].reshape(s) for o, s in zip(offsets, shapes)]


def kernel(x, c, ab_norm_g, ab_w_mod, ab_b_mod, ab_w_in, ab_conv_w, ab_w_out, sg_norm_g, sg_w_mod, sg_b_mod, sg_w_in, sg_ln_g, sg_ln_b, sg_w_s, sg_b_s, sg_w_out, final_norm_g, loss_target, m_ab_norm_g, m_ab_w_mod, m_ab_b_mod, m_ab_w_in, m_ab_conv_w, m_ab_w_out, m_sg_norm_g, m_sg_w_mod, m_sg_b_mod, m_sg_w_in, m_sg_ln_g, m_sg_ln_b, m_sg_w_s, m_sg_b_s, m_sg_w_out, m_final_norm_g, v_ab_norm_g, v_ab_w_mod, v_ab_b_mod, v_ab_w_in, v_ab_conv_w, v_ab_w_out, v_sg_norm_g, v_sg_w_mod, v_sg_b_mod, v_sg_w_in, v_sg_ln_g, v_sg_ln_b, v_sg_w_s, v_sg_b_s, v_sg_w_out, v_final_norm_g):
    s, d = x.shape[1], x.shape[2]
    aw = d // 2
    cw = d
    mod_l = ab_w_mod.shape[-1]
    x0 = x[0]
    target = loss_target[0]
    mx, my, mc = lax.axis_index("x"), lax.axis_index("y"), lax.axis_index("c")
    chip = 2 * mx + my
    chip_idx = jnp.reshape(chip, (1,)).astype(jnp.int32)
    core_idx = jnp.reshape(mc, (1,)).astype(jnp.int32)
    place_idx = jnp.stack([chip, mc]).astype(jnp.int32)

    win = [_place_own_shard(ab_w_in if L % 2 == 0 else sg_w_in, L // 2, chip_idx) for L in range(4)]
    wout = [_place_own_shard(ab_w_out if L % 2 == 0 else sg_w_out, L // 2, chip_idx) for L in range(4)]
    k_in, k_out = d, wout[0].shape[1]

    def gather_plan(ici=(), pass_on=()):
        arrays, copies, names = [], [], []
        for kind, L, only in ici:
            arrays.append(win[L] if kind == "in" else wout[L])
            names.append((kind, L))
            copies += _gather_ici(len(arrays) - 1, k_in if kind == "in" else k_out, only)
        for kind, L in pass_on:
            arrays.append(win[L] if kind == "in" else wout[L])
            names.append((kind, L))
            copies += _gather_pass_on(len(arrays) - 1, k_in if kind == "in" else k_out)
        return _Plan(tuple(arrays), tuple(copies)), names

    def absorb(plan_and_names, updated):
        _, names = plan_and_names
        for pos, (kind, L) in enumerate(names):
            if kind == "in":
                win[L] = updated[pos]
            else:
                wout[L] = updated[pos]

    win[0] = _comm_stages("gather_first_w_in", [win[0]], [_gather_ici(0, k_in), _gather_pass_on(0, k_in)],
                          chained=True)[0]

    small_local = [c[0], ab_conv_w, sg_norm_g, sg_ln_g, sg_ln_b]
    small_shapes = [a.shape for a in small_local]
    payload, small_off = _pack(small_local)
    gathered = _all_to_all(jnp.broadcast_to(payload[None], (N_DEV,) + payload.shape), "gather_small")
    per_dev = [_unpack(gathered[b], small_off, small_shapes) for b in range(N_DEV)]
    c_all = jnp.stack([per_dev[b][0] for b in range(N_DEV)])

    def from_chips(idx, axis):
        return jnp.concatenate([per_dev[2 * q][idx] for q in range(N_CHIPS)], axis=axis)

    conv_w_full = from_chips(1, 2)
    sg_norm_g_full = from_chips(2, 1)
    sg_ln_g_full = from_chips(3, 1)
    sg_ln_b_full = from_chips(4, 1)

    ab_b_local = lax.dynamic_slice_in_dim(ab_b_mod, chip * mod_l, mod_l, axis=1)
    mod_rows = []
    for layer in range(4):
        i = layer // 2
        w_mod, bias = (ab_w_mod, ab_b_local) if layer % 2 == 0 else (sg_w_mod, sg_b_mod)
        mod_rows.append(_mod_fwd(c_all, w_mod, bias[i:i + 1], i))
    mod_local = jnp.stack(mod_rows, axis=1)
    mod_recv = _all_to_all(mod_local.reshape(N_DEV, -1, LANES), "exchange_mod")
    mod_recv = mod_recv.reshape(N_DEV, 4, mod_l)
    mod_full = jnp.concatenate([mod_recv[2 * q] for q in range(N_CHIPS)], axis=-1)
    shifts = [mod_full[l:l + 1, :d] for l in range(4)]
    scales = [mod_full[l:l + 1, d:2 * d] for l in range(4)]
    gates = [mod_full[l:l + 1, 2 * d:] for l in range(4)]

    cos, sin = _rope_tables(s)
    w_s16 = sg_w_s.astype(BF16)
    b_s3 = sg_b_s[..., None]

    near, far, everyone = (0, 1), (2,), (0, 1, 2)
    fwd_comm = {
        ("in_proj", 0): ([("in", 1, near), ("out", 0, everyone), ("out", 1, everyone)], []),
        ("attn", 0): ([("in", 1, far), ("in", 2, near)], [("out", 0)]),
        ("out_proj", 0): ([], [("in", 1), ("out", 1)]),
        ("in_proj", 1): ([("in", 2, far)], []),
        ("out_proj", 1): ([], [("in", 2)]),
        ("in_proj", 2): ([("in", 3, everyone)], []),
        ("attn", 2): ([("out", 3, everyone), ("out", 2, everyone)], []),
        ("ab_mix", 2): ([], [("out", 2)]),
        ("out_proj", 2): ([], [("in", 3), ("out", 3)]),
    }

    def carried(key, fn, *args):
        if key not in fwd_comm:
            return fn(*args)
        pn = gather_plan(*fwd_comm[key])
        res, updated = fn(*args, plan=pn[0])
        absorb(pn, updated)
        return res

    def norm_params(layer):
        g = ab_norm_g if layer % 2 == 0 else sg_norm_g_full
        return g[layer // 2:layer // 2 + 1], scales[layer], shifts[layer]

    saved = []
    xs = x0
    h = _prenorm(xs, *norm_params(0))
    for layer in range(4):
        i = layer // 2
        next_norm = norm_params(layer + 1) if layer < 3 else None
        if layer % 2 == 0:
            proj = carried(("in_proj", layer), _in_proj, h, win[layer])
            attn, lse = carried(("attn", layer), _attn_fwd, proj, cos, sin, aw)
            y = carried(("ab_mix", layer), _ab_mix, attn, proj, conv_w_full[i], aw)
            res = carried(("out_proj", layer), _out_proj_residual, y, wout[layer].reshape(-1, d), xs,
                          gates[layer], next_norm)
            saved.append((xs, h, proj, y, res[1], attn, lse))
        else:
            uvz = carried(("in_proj", layer), _in_proj, h, win[layer])
            y = _sgu_fwd(uvz, sg_ln_g_full[i:i + 1], sg_ln_b_full[i:i + 1], w_s16[i], b_s3[i], cw)
            res = carried(("out_proj", layer), _out_proj_residual, y, wout[layer].reshape(-1, d), xs,
                          gates[layer], next_norm)
            saved.append((xs, h, uvz, y, res[1]))
        xs = res[0]
        h = res[2] if next_norm is not None else None

    loss11, dx, d_final_g, dout, dgate = _final_loss(xs, target, final_norm_g[None], saved[3][4], gates[3])
    loss = lax.psum(loss11[0, 0], ("x", "y", "c"))

    reduced = {"in": [lax.empty((2,) + w.shape[1:], F32) for w in (ab_w_in, sg_w_in)],
               "out": [lax.empty((2,) + w.shape[1:], F32) for w in (ab_w_out, sg_w_out)]}
    grads = {}
    stage = {}
    k_of = {"in": k_in, "out": k_out}

    def swap_plan(which):
        arrays, copies = [], []
        for kind, L in which:
            g = grads[kind, L]
            arrays += [g, jax.ShapeDtypeStruct((N_CHIPS, g.shape[1] // 2, g.shape[2]), BF16)]
            copies += _reduce_swap(len(arrays) - 2, len(arrays) - 1, k_of[kind])
        return _Plan(tuple(arrays), tuple(copies))

    def after_swap(which, updated):
        for pos, (kind, L) in enumerate(which):
            stage[kind, L] = _add_sibling(grads[kind, L], updated[2 * pos + 1], core_idx)

    def ici_plan(pieces):
        arrays, copies = [], []
        for kind, L, only in pieces:
            cs = stage[kind, L]
            arrays += [cs, stage.get((kind, L, "recv"), jax.ShapeDtypeStruct((3,) + cs.shape[1:], BF16))]
            copies += _reduce_ici(len(arrays) - 2, len(arrays) - 1, only)
        return _Plan(tuple(arrays), tuple(copies))

    def after_ici(pieces, updated):
        for pos, (kind, L, _) in enumerate(pieces):
            stage[kind, L, "recv"] = updated[2 * pos + 1]

    def sum_layer(L):
        for kind in ("in", "out"):
            reduced[kind][L % 2] = _sum_chips(stage[kind, L], stage[kind, L, "recv"], reduced[kind][L % 2],
                                              L // 2, place_idx)

    def share_plan(L):
        arrays = (reduced["in"][L % 2], reduced["out"][L % 2])
        copies = _reduce_share(0, L // 2, k_in) + _reduce_share(1, L // 2, k_out)
        return _Plan(arrays, tuple(copies))

    def after_share(L, updated):
        reduced["in"][L % 2], reduced["out"][L % 2] = updated[0], updated[1]

    all_chips = (0, 1, 2)
    dmods = [None] * 4
    d_ab_norm_g, d_sg_norm_g = [None, None], [None, None]
    d_conv_w, d_ln_g, d_ln_b, d_w_s, d_b_s = ([None, None] for _ in range(5))
    for layer in reversed(range(4)):
        i = layer // 2
        prev = layer + 1
        busy = prev < 4
        if layer % 2 == 0:
            xs, h, proj, y, out, attn, lse = saved[layer]
        else:
            xs, h, uvz, y, out = saved[layer]
        below = (saved[layer - 1][4], gates[layer - 1]) if layer > 0 else None
        w2 = wout[layer].reshape(-1, d)
        grads["out", layer] = _out_proj_bwd_w(y, dout).reshape(N_CHIPS, -1, d)
        if busy:
            swapped = [("in", prev), ("out", prev)] + ([("out", 0)] if layer == 0 else [])
            dy, updated = _out_proj_bwd_act(dout, w2, plan=swap_plan(swapped))
            after_swap(swapped, updated)
        else:
            dy = _out_proj_bwd_act(dout, w2)
        if layer % 2 == 0:
            if busy:
                pieces = [("in", prev, all_chips), ("out", prev, all_chips)]
                pieces += [("out", 0, all_chips)] if layer == 0 else []
                dact, updated = _attn_bwd(proj, cos, sin, dy, attn, lse, aw, plan=ici_plan(pieces))
                after_ici(pieces, updated)
                sum_layer(prev)
            else:
                dact = _attn_bwd(proj, cos, sin, dy, attn, lse, aw)
            dact, d_conv_w[i] = _conv_bwd(dact, dy, proj, conv_w_full[i], aw)
            if layer == 0:
                gating, gating_off = _pack([jnp.stack(d_w_s), jnp.stack(d_b_s)], PACKED_ROW_BLOCK)
                slots = _place_own_slot(gating, jnp.reshape(2 * chip + mc, (1,)).astype(jnp.int32))
                sharing = share_plan(prev)
                plan = _merge_plans(sharing, _Plan((gating, slots), tuple(_broadcast_copies(0, 1))))
                grads["in", 0], updated = _in_proj_bwd_w(h, dact, win[0].shape[-1], plan=plan)
                after_share(prev, updated)
                all_gating = updated[len(sharing.arrays) + 1]
                plan = swap_plan([("in", 0)])
                after_swap([("in", 0)], _comm_stages("grads_to_sibling", plan.arrays, [plan.copies]))
                pieces = [("in", 0, all_chips)]
                dh, updated = _in_proj_bwd_act(dact, win[0], plan=ici_plan(pieces))
                after_ici(pieces, updated)
            elif busy:
                dh, updated = _in_proj_bwd_act(dact, win[layer], plan=share_plan(prev))
                after_share(prev, updated)
                grads["in", layer] = _in_proj_bwd_w(h, dact, win[layer].shape[-1])
            else:
                dh = _in_proj_bwd_act(dact, win[layer])
                grads["in", layer] = _in_proj_bwd_w(h, dact, win[layer].shape[-1])
            norm_g = ab_norm_g[i:i + 1]
        else:
            sgu_args = (uvz, dy, sg_ln_g_full[i:i + 1], sg_ln_b_full[i:i + 1], w_s16[i], b_s3[i], cw)
            if busy:
                pieces = [("in", prev, (0, 1))]
                res, updated = _sgu_bwd(*sgu_args, plan=ici_plan(pieces))
                after_ici(pieces, updated)
            else:
                res = _sgu_bwd(*sgu_args)
            dact, d_w_s[i], db_wide, d_ln_g[i], d_ln_b[i] = res
            d_b_s[i] = db_wide[:, :, 0]
            if busy:
                pieces = [("in", prev, (2,)), ("out", prev, all_chips)]
                dh, updated = _in_proj_bwd_act(dact, win[layer], plan=ici_plan(pieces))
                after_ici(pieces, updated)
                sum_layer(prev)
                grads["in", layer], updated = _in_proj_bwd_w(h, dact, win[layer].shape[-1], plan=share_plan(prev))
                after_share(prev, updated)
            else:
                dh = _in_proj_bwd_act(dact, win[layer])
                grads["in", layer] = _in_proj_bwd_w(h, dact, win[layer].shape[-1])
            norm_g = sg_norm_g_full[i:i + 1]
        res = _prenorm_bwd(xs, dh, dx, norm_g, scales[layer], below)
        dx, dshift, dscale, d_norm_g = res[:4]
        (d_ab_norm_g if layer % 2 == 0 else d_sg_norm_g)[i] = d_norm_g
        dmods[layer] = jnp.concatenate([dshift, dscale, dgate], axis=1)
        if below:
            dout, dgate = res[4:]
    grad_x = dx[None]

    partial_list = [jnp.concatenate(dmods, axis=0),
                    jnp.concatenate(d_ab_norm_g, axis=0), jnp.concatenate(d_sg_norm_g, axis=0), d_final_g[0],
                    jnp.stack(d_conv_w), jnp.concatenate(d_ln_g, axis=0), jnp.concatenate(d_ln_b, axis=0)]
    partial_shapes = [a.shape for a in partial_list]
    partials, part_off = _pack(partial_list, PACKED_ROW_BLOCK)
    all_partials = _all_to_all(jnp.broadcast_to(partials[None], (N_DEV,) + partials.shape), "gather_partials")
    sum_layer(0)
    summed_packed, updated = _sum_devices(all_partials, plan=share_plan(0))
    after_share(0, updated)
    summed = _unpack(summed_packed, part_off, partial_shapes)
    g_mod_bias, g_ab_norm_g, g_sg_norm_g_full, g_final_g, g_conv_full, g_ln_g_full, g_ln_b_full = summed
    g_w_s, g_b_s = _unpack(_sum_devices(all_gating), gating_off, [sg_w_s.shape, sg_b_s.shape])
    dm_all = jnp.stack([_unpack(all_partials[b], part_off[:1], partial_shapes[:1])[0] for b in range(N_DEV)])
    dm_local = lax.dynamic_slice_in_dim(dm_all, chip * mod_l, mod_l, axis=2)

    def chip_cols(a, axis):
        width = a.shape[axis] // N_CHIPS
        return lax.dynamic_slice_in_dim(a, chip * width, width, axis=axis)

    g_ab_b_mod = jnp.stack([g_mod_bias[0], g_mod_bias[2]])
    g_sg_b_mod = chip_cols(jnp.stack([g_mod_bias[1], g_mod_bias[3]]), 1)
    g_ab_w_mod = _mod_bwd_w(c_all, jnp.stack([dm_local[:, 0], dm_local[:, 2]]))
    g_sg_w_mod = _mod_bwd_w(c_all, jnp.stack([dm_local[:, 1], dm_local[:, 3]]))
    g_conv = chip_cols(g_conv_full, 2)
    g_sg_norm_g = chip_cols(g_sg_norm_g_full, 1)
    g_ln_g = chip_cols(g_ln_g_full, 1)
    g_ln_b = chip_cols(g_ln_b_full, 1)

    def step_big(w, g, m, v, also_grad=False):
        res = _adamw(_flat_rows(w), _flat_rows(g), _flat_rows(m), _flat_rows(v), also_grad)
        return tuple(a.reshape(w.shape) for a in res)

    big_out = {
        "ab_w_mod": step_big(ab_w_mod, g_ab_w_mod, m_ab_w_mod, v_ab_w_mod),
        "ab_w_in": step_big(ab_w_in, reduced["in"][0], m_ab_w_in, v_ab_w_in, True),
        "ab_w_out": step_big(ab_w_out, reduced["out"][0], m_ab_w_out, v_ab_w_out, True),
        "sg_w_mod": step_big(sg_w_mod, g_sg_w_mod, m_sg_w_mod, v_sg_w_mod),
        "sg_w_in": step_big(sg_w_in, reduced["in"][1], m_sg_w_in, v_sg_w_in, True),
        "sg_w_out": step_big(sg_w_out, reduced["out"][1], m_sg_w_out, v_sg_w_out, True),
    }
    g_ab_w_in, g_ab_w_out = big_out["ab_w_in"][3], big_out["ab_w_out"][3]
    g_sg_w_in, g_sg_w_out = big_out["sg_w_in"][3], big_out["sg_w_out"][3]
    small_names = ["ab_norm_g", "ab_b_mod", "ab_conv_w", "sg_norm_g", "sg_b_mod", "sg_ln_g", "sg_ln_b",
                   "sg_w_s", "sg_b_s", "final_norm_g"]
    small_w = [ab_norm_g, ab_b_mod, ab_conv_w, sg_norm_g, sg_b_mod, sg_ln_g, sg_ln_b, sg_w_s, sg_b_s, final_norm_g]
    small_g = [g_ab_norm_g, g_ab_b_mod, g_conv, g_sg_norm_g, g_sg_b_mod, g_ln_g, g_ln_b, g_w_s, g_b_s, g_final_g]
    small_m = [m_ab_norm_g, m_ab_b_mod, m_ab_conv_w, m_sg_norm_g, m_sg_b_mod, m_sg_ln_g, m_sg_ln_b, m_sg_w_s,
               m_sg_b_s, m_final_norm_g]
    small_v = [v_ab_norm_g, v_ab_b_mod, v_ab_conv_w, v_sg_norm_g, v_sg_b_mod, v_sg_ln_g, v_sg_ln_b, v_sg_w_s,
               v_sg_b_s, v_final_norm_g]
    shapes = [a.shape for a in small_w]
    pw, off = _pack(small_w, PACKED_ROW_BLOCK)
    pg, _ = _pack(small_g, PACKED_ROW_BLOCK)
    pm, _ = _pack(small_m, PACKED_ROW_BLOCK)
    pv, _ = _pack(small_v, PACKED_ROW_BLOCK)
    pd, pnm, pnv = _adamw(pw, pg, pm, pv)
    small_out = {}
    for name, dl, nm, nv in zip(small_names, _unpack(pd, off, shapes), _unpack(pnm, off, shapes),
                                _unpack(pnv, off, shapes)):
        small_out[name] = (dl, nm, nv)

    grad_of = {
        "ab_norm_g": g_ab_norm_g, "ab_w_mod": g_ab_w_mod, "ab_b_mod": g_ab_b_mod, "ab_w_in": g_ab_w_in,
        "ab_conv_w": g_conv, "ab_w_out": g_ab_w_out, "sg_norm_g": g_sg_norm_g, "sg_w_mod": g_sg_w_mod,
        "sg_b_mod": g_sg_b_mod, "sg_w_in": g_sg_w_in, "sg_ln_g": g_ln_g, "sg_ln_b": g_ln_b, "sg_w_s": g_w_s,
        "sg_b_s": g_b_s, "sg_w_out": g_sg_w_out, "final_norm_g": g_final_g,
    }
    order = ["ab_norm_g", "ab_w_mod", "ab_b_mod", "ab_w_in", "ab_conv_w", "ab_w_out", "sg_norm_g", "sg_w_mod",
             "sg_b_mod", "sg_w_in", "sg_ln_g", "sg_ln_b", "sg_w_s", "sg_b_s", "sg_w_out", "final_norm_g"]
    steps = {**big_out, **small_out}
    return (loss, grad_x, *[grad_of[n] for n in order], *[steps[n][0] for n in order],
            *[steps[n][1] for n in order], *[steps[n][2] for n in order])
```

```python
import math
from typing import Any, Callable, NamedTuple

import jax
import jax.numpy as jnp
import numpy as np
from jax import lax
from jax.experimental import pallas as pl
from jax.experimental.pallas import tpu as pltpu

F32 = jnp.float32
BF16 = jnp.bfloat16

HEAD_DIM = 128
RADIUS = 64
DILATIONS = (1, 4, 16)
Q_BLOCK = 256
K_WINDOW = Q_BLOCK + 2 * RADIUS
ATTN_UNROLL = 8
ROPE_THETA = 10000.0
NEG_INF = -1e30
N_GROUPS = 8
CHUNK = 128
EPS = 1e-6
CONV_ROWS = 512
CONV_HALO = 16
LANES = 128
ELEMENTWISE_BLOCK = 512 * 1024
PACKED_ROW_BLOCK = 512
N_DEV = 8
N_CHIPS = 4

ADAM_LR = 0.001
ADAM_B1 = 0.9
ADAM_B2 = 0.999
ADAM_EPS = 1e-08
ADAM_WD = 0.01
ADAM_STEP = 10

VMEM_LIMIT_V7X = 56 * 1024 * 1024

MESH_ID = pl.DeviceIdType.MESH
HBM_SPEC = pl.BlockSpec(memory_space=pltpu.HBM)

NN = (((1,), (0,)), ((), ()))
NT = (((1,), (1,)), ((), ()))
TN = (((0,), (0,)), ((), ()))


def _params(n_grid, parallel=0):
    sem = tuple(["parallel"] * parallel + ["arbitrary"] * (n_grid - parallel))
    return pltpu.CompilerParams(dimension_semantics=sem, vmem_limit_bytes=VMEM_LIMIT_V7X)


def _tile(n, prefs):
    for p in prefs:
        if n % p == 0:
            return p
    return n


def _sigmoid(z):
    return 1.0 / (1.0 + jnp.exp(-z))


def _silu(z):
    return z * _sigmoid(z)


_GELU_K = math.sqrt(2.0 / math.pi)
_GELU_C = 0.044715


def _gelu(u):
    return 0.5 * u * (1.0 + jnp.tanh(_GELU_K * (u + _GELU_C * u * u * u)))


def _gelu_and_grad(u):
    t = jnp.tanh(_GELU_K * (u + _GELU_C * u * u * u))
    half = 0.5 * (1.0 + t)
    return u * half, half + 0.5 * u * (1.0 - t * t) * _GELU_K * (1.0 + 3.0 * _GELU_C * u * u)


def _silu_and_grad(z):
    s = _sigmoid(z)
    return z * s, s * (1.0 + z * (1.0 - s))


class _Place(NamedTuple):
    x: Any
    y: Any
    c: Any
    chip: Any


def _my_place():
    mx, my, mc = lax.axis_index("x"), lax.axis_index("y"), lax.axis_index("c")
    return _Place(mx, my, mc, 2 * mx + my)


def _other_chips(p):
    return [(1 - p.x, p.y), (p.x, 1 - p.y), (1 - p.x, 1 - p.y)]


class _Copy(NamedTuple):
    src: int
    src_at: Callable
    dst: int
    dst_at: Callable
    peer: Callable


class _Plan(NamedTuple):
    arrays: tuple
    copies: tuple


def _view(ref, index):
    return ref if index is None else ref.at[index]


def _plan_io(plan):
    ins = [k for k, a in enumerate(plan.arrays) if not isinstance(a, jax.ShapeDtypeStruct)]
    written = sorted({cp.dst for cp in plan.copies})
    return ins, written


def _descriptors(plan, in_refs, out_refs, send_sems, recv_sems):
    ins, written = _plan_io(plan)
    place = _my_place()
    return [
        pltpu.make_async_remote_copy(
            src_ref=_view(in_refs[ins.index(cp.src)], cp.src_at(place)),
            dst_ref=_view(out_refs[written.index(cp.dst)], cp.dst_at(place)),
            send_sem=send_sems.at[k], recv_sem=recv_sems.at[k],
            device_id=cp.peer(place), device_id_type=MESH_ID)
        for k, cp in enumerate(plan.copies)]


def _plan_operands(plan, n_in, n_out):
    ins, written = _plan_io(plan)
    operands = [plan.arrays[k] for k in ins]
    out_shape = [jax.ShapeDtypeStruct(plan.arrays[k].shape, plan.arrays[k].dtype) for k in written]
    aliases = {n_in + ins.index(k): n_out + pos for pos, k in enumerate(written) if k in ins}
    n = len(plan.copies)
    sems = [pltpu.SemaphoreType.DMA((n,)), pltpu.SemaphoreType.DMA((n,))]
    return operands, out_shape, aliases, sems, written


def _call(body, *, name, grid, in_specs, out_specs, out_shape, operands, scratch_shapes=(), aliases=None,
          parallel=0, plan=None):
    single = not isinstance(out_shape, (list, tuple))
    out_shape = [out_shape] if single else list(out_shape)
    out_specs = [out_specs] if single else list(out_specs)
    if plan is None:
        res = pl.pallas_call(
            body, name=name, grid=grid, in_specs=list(in_specs), out_specs=out_specs, out_shape=out_shape,
            scratch_shapes=list(scratch_shapes), input_output_aliases=aliases or {},
            compiler_params=_params(len(grid), parallel=parallel),
        )(*operands)
        return res[0] if single else res

    n_in, n_out, n_scr = len(operands), len(out_shape), len(scratch_shapes)
    p_operands, p_out_shape, p_aliases, sems, written = _plan_operands(plan, n_in, n_out)
    n_pin, n_pout = len(p_operands), len(p_out_shape)

    def wrapped(*refs):
        ins = refs[:n_in]
        p_in = refs[n_in:n_in + n_pin]
        outs = refs[n_in + n_pin:n_in + n_pin + n_out]
        p_out = refs[n_in + n_pin + n_out:n_in + n_pin + n_out + n_pout]
        scratch = refs[n_in + n_pin + n_out + n_pout:n_in + n_pin + n_out + n_pout + n_scr]
        send_sems, recv_sems = refs[-2:]
        ids = [pl.program_id(a) for a in range(len(grid))]
        first = ids[0] == 0
        last = ids[0] == grid[0] - 1
        for a in range(1, len(grid)):
            first = jnp.logical_and(first, ids[a] == 0)
            last = jnp.logical_and(last, ids[a] == grid[a] - 1)

        @pl.when(first)
        def _():
            for cp in _descriptors(plan, p_in, p_out, send_sems, recv_sems):
                cp.start()

        body(*ins, *outs, *scratch)

        @pl.when(last)
        def _():
            for cp in _descriptors(plan, p_in, p_out, send_sems, recv_sems):
                cp.wait()

    res = pl.pallas_call(
        wrapped, name=name, grid=grid,
        in_specs=list(in_specs) + [HBM_SPEC] * n_pin,
        out_specs=out_specs + [HBM_SPEC] * n_pout,
        out_shape=out_shape + p_out_shape,
        scratch_shapes=list(scratch_shapes) + sems,
        input_output_aliases={**(aliases or {}), **p_aliases},
        compiler_params=_params(len(grid)),
    )(*operands, *p_operands)
    outs = res[0] if single else res[:n_out]
    return outs, dict(zip(written, res[n_out:]))


def _comm_stages(name, arrays, stages, chained=False):
    plan = _Plan(tuple(arrays), tuple(cp for st in stages for cp in st))
    p_operands, p_out_shape, p_aliases, sems, written = _plan_operands(plan, 0, 0)
    n_pin = len(p_operands)

    def body(*refs):
        p_in = refs[:n_pin]
        p_out = refs[n_pin:n_pin + len(written)]
        send_sems, recv_sems = refs[-2:]
        all_copies = _descriptors(plan, p_in, p_out, send_sems, recv_sems)
        if chained:
            n = len(stages[0])
            for cp in all_copies[:n]:
                cp.start()
            for k in range(n):
                all_copies[k].wait()
                all_copies[n + k].start()
            for cp in all_copies[n:]:
                cp.wait()
            return
        base = 0
        for st in stages:
            for cp in all_copies[base:base + len(st)]:
                cp.start()
            for cp in all_copies[base:base + len(st)]:
                cp.wait()
            base += len(st)

    res = pl.pallas_call(
        body, name=name, in_specs=[HBM_SPEC] * n_pin, out_specs=[HBM_SPEC] * len(written),
        out_shape=p_out_shape, scratch_shapes=sems, input_output_aliases=p_aliases,
    )(*p_operands)
    return dict(zip(written, res))


def _start_copies(name, array, copies):
    n = len(copies)
    plan = _Plan((array,), tuple(copies))

    def body(a_ref, send_sems, recv_sems, a_thru, token):
        for cp in _descriptors(plan, (a_ref,), (a_thru,), send_sems, recv_sems):
            cp.start()
        token[...] = jnp.zeros_like(token)

    return pl.pallas_call(
        body, name=name,
        out_shape=(pltpu.SemaphoreType.DMA((n,)), pltpu.SemaphoreType.DMA((n,)),
                   pltpu.HBM(array.shape, array.dtype), jax.ShapeDtypeStruct((8, LANES), F32)),
        in_specs=[HBM_SPEC],
        out_specs=(pl.BlockSpec(memory_space=pltpu.SEMAPHORE), pl.BlockSpec(memory_space=pltpu.SEMAPHORE),
                   HBM_SPEC, pl.BlockSpec(memory_space=pltpu.VMEM)),
        input_output_aliases={0: 2},
        compiler_params=pltpu.CompilerParams(has_side_effects=pltpu.SideEffectType.DATAFLOW_SIDE_EFFECTING),
    )(pltpu.with_memory_space_constraint(array, pltpu.HBM))


def _wait_copies(name, array, send_sems, recv_sems, copies, after):
    plan = _Plan((array,), tuple(copies))

    def body(a_ref, send_ref, recv_ref, after_ref, a_out):
        for cp in _descriptors(plan, (a_ref,), (a_out,), send_ref, recv_ref):
            cp.wait_send()
            cp.wait_recv()

    return pl.pallas_call(
        body, name=name,
        out_shape=pltpu.HBM(array.shape, array.dtype),
        in_specs=[HBM_SPEC, pl.BlockSpec(memory_space=pltpu.SEMAPHORE), pl.BlockSpec(memory_space=pltpu.SEMAPHORE),
                  pl.BlockSpec(memory_space=pl.ANY)],
        out_specs=HBM_SPEC,
        input_output_aliases={0: 0},
        compiler_params=pltpu.CompilerParams(has_side_effects=pltpu.SideEffectType.DATAFLOW_SIDE_EFFECTING),
    )(array, send_sems, recv_sems, after)


def _half_rows(k, c):
    return pl.ds(c * (k // 2), k // 2)


def _gather_ici(a, k, only=(0, 1, 2)):
    own = lambda p: (p.chip, _half_rows(k, p.c))
    return [_Copy(a, own, a, own, lambda p, q=q: (*_other_chips(p)[q], p.c)) for q in only]


def _gather_pass_on(a, k):
    def at(q):
        def index(p):
            px, py = _other_chips(p)[q]
            return (2 * px + py, _half_rows(k, p.c))
        return index
    return [_Copy(a, at(q), a, at(q), lambda p: (p.x, p.y, 1 - p.c)) for q in range(3)]


def _reduce_swap(src, dst, k):
    return [_Copy(src, lambda p: (pl.ds(0, N_CHIPS), _half_rows(k, 1 - p.c)), dst, lambda p: None,
                  lambda p: (p.x, p.y, 1 - p.c))]


def _reduce_ici(src, dst, only=(0, 1, 2)):
    def slab(q):
        def index(p):
            px, py = _other_chips(p)[q]
            return 2 * px + py
        return index
    return [_Copy(src, slab(q), dst, lambda p, q=q: q, lambda p, q=q: (*_other_chips(p)[q], p.c)) for q in only]


def _reduce_share(a, layer, k):
    at = lambda p: (layer, _half_rows(k, p.c))
    return [_Copy(a, at, a, at, lambda p: (p.x, p.y, 1 - p.c))]


def _merge_plans(a, b):
    off = len(a.arrays)
    moved = tuple(cp._replace(src=cp.src + off, dst=cp.dst + off) for cp in b.copies)
    return _Plan(a.arrays + b.arrays, a.copies + moved)


def _broadcast_copies(src, dst):
    me = lambda p: 2 * p.chip + p.c

    def peer(k):
        return lambda p: (1 - p.x if (k >> 2) & 1 else p.x, 1 - p.y if (k >> 1) & 1 else p.y,
                          1 - p.c if k & 1 else p.c)

    return [_Copy(src, lambda p: None, dst, me, peer(k)) for k in range(1, N_DEV)]


def _place_own_slot(x, me_idx):
    r = x.shape[0]
    tr = _tile(r, (512, 256, 128, 64, 32, 16, 8))

    def body(me_ref, x_ref, o_ref):
        o_ref[...] = x_ref[...]

    return pl.pallas_call(
        body, name="place_own_slot",
        grid_spec=pltpu.PrefetchScalarGridSpec(
            num_scalar_prefetch=1, grid=(r // tr,),
            in_specs=[pl.BlockSpec((tr, LANES), lambda i, me: (i, 0))],
            out_specs=pl.BlockSpec((None, tr, LANES), lambda i, me: (me[0], i, 0))),
        out_shape=jax.ShapeDtypeStruct((N_DEV, r, LANES), F32),
        compiler_params=_params(1, parallel=1),
    )(me_idx, x)


def _all_to_all(x, name):
    def body(x_ref, y_ref, send_sems, recv_sems, own_sem):
        p = _my_place()
        me = 2 * p.chip + p.c
        own = pltpu.make_async_copy(x_ref.at[me], y_ref.at[me], own_sem)
        own.start()
        copies = []
        for k in range(1, N_DEV):
            px = 1 - p.x if (k >> 2) & 1 else p.x
            py = 1 - p.y if (k >> 1) & 1 else p.y
            pc = 1 - p.c if k & 1 else p.c
            peer = 4 * px + 2 * py + pc
            cp = pltpu.make_async_remote_copy(
                src_ref=x_ref.at[peer], dst_ref=y_ref.at[me],
                send_sem=send_sems.at[k - 1], recv_sem=recv_sems.at[k - 1],
                device_id=(px, py, pc), device_id_type=MESH_ID)
            cp.start()
            copies.append(cp)
        for cp in copies:
            cp.wait()
        own.wait()

    return pl.pallas_call(
        body, name=name,
        out_shape=jax.ShapeDtypeStruct(x.shape, x.dtype),
        in_specs=[HBM_SPEC], out_specs=HBM_SPEC,
        scratch_shapes=[pltpu.SemaphoreType.DMA((N_DEV - 1,)), pltpu.SemaphoreType.DMA((N_DEV - 1,)),
                        pltpu.SemaphoreType.DMA],
    )(x)


def _place_own_shard(w, layer, chip_idx):
    _, k, n = w.shape
    tr = _tile(k, (512, 256, 128))

    def body(c_ref, w_ref, g_ref):
        g_ref[...] = w_ref[...].astype(BF16)

    return pl.pallas_call(
        body, name="place_own_shard",
        grid_spec=pltpu.PrefetchScalarGridSpec(
            num_scalar_prefetch=1, grid=(k // tr,),
            in_specs=[pl.BlockSpec((None, tr, n), lambda r, c: (layer, r, 0))],
            out_specs=pl.BlockSpec((None, tr, n), lambda r, c: (c[0], r, 0))),
        out_shape=jax.ShapeDtypeStruct((N_CHIPS, k, n), BF16),
        compiler_params=_params(1, parallel=1),
    )(chip_idx, w)


def _matmul(name, operands, in_specs, grid, dims, out_shape, out_specs, epilogue, a_prologue=None,
            aliases=None, plan=None):
    n_in = len(operands)

    def body(*refs):
        a = refs[0][...]
        if a_prologue is not None:
            a = a_prologue(a)
        acc = lax.dot_general(a.astype(BF16), refs[1][...].astype(BF16), dims, preferred_element_type=F32)
        epilogue(acc, refs[2:n_in], refs[n_in:])

    return _call(body, name=name, grid=grid, in_specs=in_specs, out_specs=out_specs, out_shape=out_shape,
                 operands=operands, aliases=aliases, parallel=2, plan=plan)


def _store_cast(acc, extra, outs):
    outs[0][...] = acc.astype(outs[0].dtype)


def _in_proj(h, w, plan=None):
    s, d = h.shape
    nl = w.shape[-1]
    tm = _tile(s, (1024, 512, 256))
    tn = _tile(nl, (1024, 768, 512, 384, 256, 128))
    per = nl // tn
    return _matmul(
        "in_proj", (h, w),
        [pl.BlockSpec((tm, d), lambda i, j: (i, 0)),
         pl.BlockSpec((None, d, tn), lambda i, j: (j // per, 0, j % per))],
        (s // tm, N_CHIPS * per), NN,
        jax.ShapeDtypeStruct((s, N_CHIPS * nl), BF16),
        pl.BlockSpec((tm, tn), lambda i, j: (i, j)), _store_cast, plan=plan)


def _modulated_norm(xv, g, scale, shift):
    rstd = lax.rsqrt(jnp.mean(xv * xv, axis=-1, keepdims=True) + EPS)
    return ((xv * rstd) * g * (1.0 + scale) + shift).astype(BF16)


def _out_proj_residual(y, w2, x, gate, next_norm, plan=None):
    s, wdt = y.shape
    d = w2.shape[-1]
    tm = _tile(s, (256, 128))

    def epilogue(acc, extra, outs):
        x_new = extra[0][...] + extra[1][...] * acc
        outs[0][...] = x_new
        outs[1][...] = acc.astype(BF16)
        if next_norm is not None:
            outs[2][...] = _modulated_norm(x_new, extra[2][...], extra[3][...], extra[4][...])

    rows = pl.BlockSpec((tm, d), lambda i, j: (i, 0))
    vec = pl.BlockSpec((1, d), lambda i, j: (0, 0))
    n_vec = 1 + (3 if next_norm is not None else 0)
    n_act = 1 + (1 if next_norm is not None else 0)
    return _matmul(
        "out_proj", (y, w2, x, gate) + (tuple(next_norm) if next_norm is not None else ()),
        [pl.BlockSpec((tm, wdt), lambda i, j: (i, 0)),
         pl.BlockSpec((wdt, d), lambda i, j: (0, 0), pipeline_mode=pl.Buffered(1)),
         rows] + [vec] * n_vec,
        (s // tm, 1), NN,
        [jax.ShapeDtypeStruct((s, d), F32)] + [jax.ShapeDtypeStruct((s, d), BF16)] * n_act,
        [rows] * (1 + n_act), epilogue, plan=plan)


def _out_proj_bwd_act(dout, w2, plan=None):
    s, d = dout.shape
    wdt = w2.shape[0]
    tm = _tile(s, (1024, 512, 256))
    tn = _tile(wdt, (1024, 512, 256, 128))
    return _matmul(
        "out_proj_dy", (dout, w2),
        [pl.BlockSpec((tm, d), lambda i, j: (i, 0)),
         pl.BlockSpec((tn, d), lambda i, j: (j, 0))],
        (s // tm, wdt // tn), NT,
        jax.ShapeDtypeStruct((s, wdt), BF16),
        pl.BlockSpec((tm, tn), lambda i, j: (i, j)), _store_cast, plan=plan)


def _out_proj_bwd_w(y, dout, plan=None):
    s, wdt = y.shape
    d = dout.shape[1]
    tm = _tile(wdt, (1024, 512, 256, 128))
    tn = _tile(d, (1024, 512, 256, 128))
    return _matmul(
        "out_proj_dw", (y, dout),
        [pl.BlockSpec((s, tm), lambda i, j: (0, i)),
         pl.BlockSpec((s, tn), lambda i, j: (0, j))],
        (wdt // tm, d // tn), TN,
        jax.ShapeDtypeStruct((wdt, d), BF16),
        pl.BlockSpec((tm, tn), lambda i, j: (i, j)), _store_cast, plan=plan)


def _in_proj_bwd_act(dproj, w, plan=None):
    s, n_all = dproj.shape
    d, nl = w.shape[1], w.shape[2]
    tm = _tile(s, (512, 256))
    tn = _tile(d, (512, 256, 128))

    def body(a_ref, w_ref, o_ref):
        acc = None
        for q in range(N_CHIPS):
            part = lax.dot_general(a_ref[:, q * nl:(q + 1) * nl], w_ref[q], NT, preferred_element_type=F32)
            acc = part if acc is None else acc + part
        o_ref[...] = acc.astype(BF16)

    return _call(
        body, name="in_proj_dh", grid=(s // tm, d // tn),
        in_specs=[pl.BlockSpec((tm, n_all), lambda i, j: (i, 0)),
                  pl.BlockSpec((N_CHIPS, tn, nl), lambda i, j: (0, j, 0))],
        out_specs=pl.BlockSpec((tm, tn), lambda i, j: (i, j)),
        out_shape=jax.ShapeDtypeStruct((s, d), BF16),
        operands=(dproj, w), parallel=2, plan=plan)


def _in_proj_bwd_w(h, dproj, nl, plan=None):
    s, d = h.shape
    tm = _tile(d, (1024, 512, 256, 128))
    tn = _tile(nl, (1024, 768, 512, 384, 256, 128))
    per = nl // tn
    return _matmul(
        "in_proj_dw", (h, dproj),
        [pl.BlockSpec((s, tm), lambda i, j: (0, i)),
         pl.BlockSpec((s, tn), lambda i, j: (0, j))],
        (d // tm, N_CHIPS * per), TN,
        jax.ShapeDtypeStruct((N_CHIPS, d, nl), BF16),
        pl.BlockSpec((None, tm, tn), lambda i, j: (j // per, i, j % per)), _store_cast, plan=plan)


def _mod_fwd(c_all, w_mod, bias, layer):
    nb, d = c_all.shape
    nl = w_mod.shape[-1]
    tn = _tile(nl, (768, 512, 384, 256, 128))

    def epilogue(acc, extra, outs):
        outs[0][...] = acc + extra[0][...]

    return _matmul(
        "mod_fwd", (c_all, w_mod, bias),
        [pl.BlockSpec((nb, d), lambda i, j: (0, 0)),
         pl.BlockSpec((None, d, tn), lambda i, j: (layer, 0, j)),
         pl.BlockSpec((1, tn), lambda i, j: (0, j))],
        (1, nl // tn), NN,
        jax.ShapeDtypeStruct((nb, nl), F32),
        pl.BlockSpec((nb, tn), lambda i, j: (0, j)), epilogue, a_prologue=_silu)


def _mod_bwd_w(c_all, dm_pair):
    nb, d = c_all.shape
    nl = dm_pair.shape[-1]
    tm = _tile(d, (1024, 512, 256, 128))
    tn = _tile(nl, (768, 512, 384, 256, 128))

    def body(c_ref, dm_ref, o_ref):
        o_ref[...] = lax.dot_general(_silu(c_ref[...]).astype(BF16), dm_ref[...].astype(BF16), TN,
                                     preferred_element_type=F32)

    return _call(
        body, name="mod_dw", grid=(2, d // tm, nl // tn),
        in_specs=[pl.BlockSpec((nb, tm), lambda l, i, j: (0, i)),
                  pl.BlockSpec((None, nb, tn), lambda l, i, j: (l, 0, j))],
        out_specs=pl.BlockSpec((None, tm, tn), lambda l, i, j: (l, i, j)),
        out_shape=jax.ShapeDtypeStruct((2, d, nl), F32),
        operands=(c_all, dm_pair), parallel=3)


def _rows_call(name, body, operands, in_specs, out_shape, out_specs, n_tiles):
    return pl.pallas_call(
        body, name=name, grid=(n_tiles,), in_specs=in_specs, out_specs=out_specs, out_shape=out_shape,
        compiler_params=_params(1),
    )(*operands)


def _row_spec(tr, width):
    return pl.BlockSpec((tr, width), lambda i: (i, 0))


def _vec_spec(width):
    return pl.BlockSpec((1, width), lambda i: (0, 0))


def _accumulate(ref, val):
    first = pl.program_id(0) == 0

    @pl.when(first)
    def _():
        ref[...] = val

    @pl.when(jnp.logical_not(first))
    def _():
        ref[...] += val


def _prenorm(x, g, scale, shift):
    s, d = x.shape
    tr = _tile(s, (256, 128))

    def body(x_ref, g_ref, sc_ref, sh_ref, h_ref):
        h_ref[...] = _modulated_norm(x_ref[...], g_ref[...], sc_ref[...], sh_ref[...])

    return _rows_call("prenorm", body, (x, g, scale, shift),
                      [_row_spec(tr, d), _vec_spec(d), _vec_spec(d), _vec_spec(d)],
                      jax.ShapeDtypeStruct((s, d), BF16), _row_spec(tr, d), s // tr)


def _gate_grads(dxv, out_ref, gate_ref, dout_ref, dgate_ref):
    dout_ref[...] = (gate_ref[...] * dxv).astype(BF16)
    _accumulate(dgate_ref, jnp.sum(dxv * out_ref[...].astype(F32), axis=0, keepdims=True))


def _prenorm_bwd(x, dh, dres, g, scale, below=None):
    s, d = x.shape
    tr = _tile(s, (512, 256, 128))

    def body(x_ref, dh_ref, dres_ref, g_ref, sc_ref, *rest):
        dx_ref, dshift_ref, dscale_ref, dg_ref = rest[-6:-2] if below else rest
        xv = x_ref[...]
        dhv = dh_ref[...].astype(F32)
        rstd = lax.rsqrt(jnp.mean(xv * xv, axis=-1, keepdims=True) + EPS)
        xhat = xv * rstd
        gv = g_ref[...]
        one_sc = 1.0 + sc_ref[...]
        dxhat = dhv * gv * one_sc
        dxv = dres_ref[...] + rstd * (dxhat - xhat * jnp.mean(dxhat * xhat, axis=-1, keepdims=True))
        dx_ref[...] = dxv
        _accumulate(dshift_ref, jnp.sum(dhv, axis=0, keepdims=True))
        _accumulate(dscale_ref, jnp.sum(dhv * xhat * gv, axis=0, keepdims=True))
        _accumulate(dg_ref, jnp.sum(dhv * xhat * one_sc, axis=0, keepdims=True))
        if below:
            _gate_grads(dxv, rest[0], rest[1], rest[-2], rest[-1])

    vec = jax.ShapeDtypeStruct((1, d), F32)
    operands = (x, dh, dres, g, scale) + (tuple(below) if below else ())
    in_specs = [_row_spec(tr, d), _row_spec(tr, d), _row_spec(tr, d), _vec_spec(d), _vec_spec(d)]
    out_shape = [jax.ShapeDtypeStruct((s, d), F32), vec, vec, vec]
    out_specs = [_row_spec(tr, d), _vec_spec(d), _vec_spec(d), _vec_spec(d)]
    if below:
        in_specs += [_row_spec(tr, d), _vec_spec(d)]
        out_shape += [jax.ShapeDtypeStruct((s, d), BF16), vec]
        out_specs += [_row_spec(tr, d), _vec_spec(d)]
    return _rows_call("prenorm_bwd", body, operands, in_specs, out_shape, out_specs, s // tr)


def _final_loss(x, target, g, out_below, gate_below):
    s, d = x.shape
    tr = _tile(s, (512, 256, 128))
    n_tiles = s // tr

    def body(x_ref, t_ref, g_ref, out_ref, gate_ref, loss_ref, dx_ref, dg_ref, dout_ref, dgate_ref, acc_ref):
        xv = x_ref[...]
        rstd = lax.rsqrt(jnp.mean(xv * xv, axis=-1, keepdims=True) + EPS)
        xhat = xv * rstd
        gv = g_ref[...]
        err = xhat * gv - t_ref[...]
        dy = err * (1.0 / d)
        dxhat = dy * gv
        dxv = rstd * (dxhat - xhat * jnp.mean(dxhat * xhat, axis=-1, keepdims=True))
        dx_ref[...] = dxv
        _accumulate(dg_ref, jnp.sum(dy * xhat, axis=0, keepdims=True))
        _accumulate(acc_ref, jnp.sum(err * err, axis=0, keepdims=True))
        _gate_grads(dxv, out_ref, gate_ref, dout_ref, dgate_ref)

        @pl.when(pl.program_id(0) == n_tiles - 1)
        def _():
            loss_ref[...] = (0.5 / d) * jnp.sum(acc_ref[...], axis=1, keepdims=True)

    vec = jax.ShapeDtypeStruct((1, d), F32)
    return pl.pallas_call(
        body, name="final_loss", grid=(n_tiles,),
        in_specs=[_row_spec(tr, d), _row_spec(tr, d), _vec_spec(d), _row_spec(tr, d), _vec_spec(d)],
        out_specs=[pl.BlockSpec((1, 1), lambda i: (0, 0)), _row_spec(tr, d), _vec_spec(d), _row_spec(tr, d),
                   _vec_spec(d)],
        out_shape=[jax.ShapeDtypeStruct((1, 1), F32), jax.ShapeDtypeStruct((s, d), F32), vec,
                   jax.ShapeDtypeStruct((s, d), BF16), vec],
        scratch_shapes=[pltpu.VMEM((1, d), F32)],
        compiler_params=_params(1),
    )(x, target, g, out_below, gate_below)


def _rope(t, cos, sin):
    return t * cos + pltpu.roll(t, HEAD_DIM // 2, axis=1) * sin


def _unrope(dt, cos, sin):
    return dt * cos + pltpu.roll(dt * sin, HEAD_DIM // 2, axis=1)


def _band_blocks(s, dil):
    sub = s // dil
    kw = min(K_WINDOW, sub)

    def rows(r, start, n):
        if dil == 1:
            return pl.ds(pl.multiple_of(start, RADIUS), n)
        return pl.ds(r + dil * start, n, stride=dil)

    def window(idx):
        nb = sub // Q_BLOCK
        r, b = idx // nb, idx % nb
        q0 = b * Q_BLOCK
        start = jnp.clip(q0 - RADIUS, 0, sub - kw)
        ahead = (lax.broadcasted_iota(jnp.int32, (Q_BLOCK, kw), 1)
                 - lax.broadcasted_iota(jnp.int32, (Q_BLOCK, kw), 0)) + (start - q0 + RADIUS)
        valid = lax.bitcast_convert_type(ahead, jnp.uint32) <= 2 * RADIUS
        return rows(r, q0, Q_BLOCK), rows(r, start, kw), valid

    return window


def _store_column_tiles(tiles, dst_ref, sems, col_blocks):
    rows = tiles.shape[1]
    copies = []
    for g, cb in enumerate(col_blocks):
        cols = pl.ds(pl.multiple_of(cb * LANES, LANES), LANES)
        cp = pltpu.make_async_copy(tiles.at[g], dst_ref.at[pl.ds(0, rows), cols], sems.at[g])
        cp.start()
        copies.append(cp)
    for cp in copies:
        cp.wait()


def _head_col(s, group, nh):
    return pl.BlockSpec((s, HEAD_DIM), lambda h: (0, group * nh + h), pipeline_mode=pl.Buffered(1))


def _attn_fwd(proj, cos, sin, aw, plan=None):
    s = proj.shape[0]
    nh = aw // HEAD_DIM
    scale = HEAD_DIM ** -0.5
    n_blocks = s // Q_BLOCK

    def body(q_ref, k_ref, v_ref, cos_ref, sin_ref, attn_ref, lse_ref, qf, kf, vf, acc):
        cosv, sinv = cos_ref[...], sin_ref[...]
        qf[...] = _rope(q_ref[...].astype(F32), cosv, sinv) * scale
        kf[...] = _rope(k_ref[...].astype(F32), cosv, sinv)
        vf[...] = v_ref[...].astype(F32)

        for pattern, dil in enumerate(DILATIONS):
            window = _band_blocks(s, dil)

            def block(idx, carry, window=window, first=(pattern == 0)):
                q_rows, k_rows, valid = window(idx)
                q = qf[q_rows, :].astype(BF16)
                kk = kf[k_rows, :].astype(BF16)
                vv = vf[k_rows, :].astype(BF16)
                sc = lax.dot_general(q, kk, NT, preferred_element_type=F32)
                sc = jnp.where(valid, sc, NEG_INF)
                m = jnp.max(sc, axis=1, keepdims=True)
                p = jnp.exp(sc - m)
                den = jnp.sum(p, axis=1, keepdims=True)
                o = lax.dot_general(p.astype(BF16), vv, NN, preferred_element_type=F32) / den
                lse = jnp.broadcast_to(m + jnp.log(den), (Q_BLOCK, HEAD_DIM))
                if first:
                    acc[q_rows, :] = o
                    lse_ref[q_rows, :] = lse
                else:
                    lse_old = lse_ref[q_rows, :]
                    top = jnp.maximum(lse_old, lse)
                    w_old, w_new = jnp.exp(lse_old - top), jnp.exp(lse - top)
                    tot = w_old + w_new
                    acc[q_rows, :] = (acc[q_rows, :] * w_old + o * w_new) / tot
                    lse_ref[q_rows, :] = top + jnp.log(tot)
                return carry

            lax.fori_loop(0, n_blocks, block, 0, unroll=ATTN_UNROLL)

        attn_ref[...] = acc[...].astype(BF16)

    table = pl.BlockSpec((s, HEAD_DIM), lambda h: (0, 0), pipeline_mode=pl.Buffered(1))
    out = pl.BlockSpec((s, HEAD_DIM), lambda h: (0, h))
    return _call(
        body, name="attn_fwd", grid=(nh,),
        in_specs=[_head_col(s, 0, nh), _head_col(s, 1, nh), _head_col(s, 2, nh), table, table],
        out_specs=[out, out],
        out_shape=[jax.ShapeDtypeStruct((s, aw), BF16), jax.ShapeDtypeStruct((s, aw), F32)],
        scratch_shapes=[pltpu.VMEM((s, HEAD_DIM), F32)] * 4,
        operands=(proj, proj, proj, cos, sin), parallel=1, plan=plan)


def _attn_bwd(proj, cos, sin, dy, attn, lse, aw, plan=None):
    s = proj.shape[0]
    nh = aw // HEAD_DIM
    scale = HEAD_DIM ** -0.5
    n_blocks = s // Q_BLOCK

    def body(q_ref, k_ref, v_ref, za_ref, cos_ref, sin_ref, dy_ref, attn_ref, lse_ref,
             dproj_ref, qf, kf, vf, dof, delta, dqa, dka, dva, tiles, tile_sems):
        cosv, sinv = cos_ref[...], sin_ref[...]
        qf[...] = _rope(q_ref[...].astype(F32), cosv, sinv) * scale
        kf[...] = _rope(k_ref[...].astype(F32), cosv, sinv)
        vf[...] = v_ref[...].astype(F32)
        dyv, zav, attnv = dy_ref[...].astype(F32), za_ref[...].astype(F32), attn_ref[...].astype(F32)
        silu_za, dsilu_za = _silu_and_grad(zav)
        do_all = dyv * silu_za
        dof[...] = do_all
        tiles[3] = (dyv * attnv * dsilu_za).astype(BF16)
        delta[...] = jnp.broadcast_to(jnp.sum(do_all * attnv, axis=1, keepdims=True), (s, HEAD_DIM))
        dqa[...] = jnp.zeros_like(dqa)
        dka[...] = jnp.zeros_like(dka)
        dva[...] = jnp.zeros_like(dva)

        for dil in DILATIONS:
            window = _band_blocks(s, dil)

            def block(idx, carry, window=window):
                q_rows, k_rows, valid = window(idx)
                q = qf[q_rows, :].astype(BF16)
                kk = kf[k_rows, :].astype(BF16)
                vv = vf[k_rows, :].astype(BF16)
                dov = dof[q_rows, :].astype(BF16)
                lse_q = lse_ref[q_rows, :][:, 0:1]
                delta_q = delta[q_rows, :][:, 0:1]
                sc = lax.dot_general(q, kk, NT, preferred_element_type=F32)
                p = jnp.where(valid, jnp.exp(sc - lse_q), 0.0)
                dp = lax.dot_general(dov, vv, NT, preferred_element_type=F32)
                ds = (p * (dp - delta_q)).astype(BF16)
                dqa[q_rows, :] += lax.dot_general(ds, kk, NN, preferred_element_type=F32)
                dka[k_rows, :] += lax.dot_general(ds, q, TN, preferred_element_type=F32)
                dva[k_rows, :] += lax.dot_general(p.astype(BF16), dov, TN, preferred_element_type=F32)
                return carry

            lax.fori_loop(0, n_blocks, block, 0, unroll=ATTN_UNROLL)

        tiles[0] = (_unrope(dqa[...], cosv, sinv) * scale).astype(BF16)
        tiles[1] = _unrope(dka[...], cosv, sinv).astype(BF16)
        tiles[2] = dva[...].astype(BF16)
        _store_column_tiles(tiles, dproj_ref, tile_sems, [g * nh + pl.program_id(0) for g in range(4)])

    own = pl.BlockSpec((s, HEAD_DIM), lambda h: (0, h), pipeline_mode=pl.Buffered(1))
    table = pl.BlockSpec((s, HEAD_DIM), lambda h: (0, 0), pipeline_mode=pl.Buffered(1))
    return _call(
        body, name="attn_bwd", grid=(nh,),
        in_specs=[_head_col(s, 0, nh), _head_col(s, 1, nh), _head_col(s, 2, nh), _head_col(s, 3, nh),
                  table, table, own, own, own],
        out_specs=HBM_SPEC,
        out_shape=jax.ShapeDtypeStruct((s, 8 * aw), BF16),
        scratch_shapes=[pltpu.VMEM((s, HEAD_DIM), F32)] * 8 + [
            pltpu.VMEM((4, s, HEAD_DIM), BF16), pltpu.SemaphoreType.DMA((4,))],
        operands=(proj, proj, proj, proj, cos, sin, dy, attn, lse), plan=plan)


def _rope_tables(s):
    half = HEAD_DIM // 2
    inv = np.float32(ROPE_THETA) ** (-np.arange(half, dtype=np.float32) / np.float32(half))
    ang = np.arange(s, dtype=np.float32)[:, None] * inv[None, :]
    cos, sin = np.cos(ang), np.sin(ang)
    return (jnp.asarray(np.concatenate([cos, cos], axis=-1), F32),
            jnp.asarray(np.concatenate([-sin, sin], axis=-1), F32))


def _conv_chunks(s):
    for k in range(s // CONV_ROWS):
        lo = max(0, k * CONV_ROWS - CONV_HALO)
        hi = min(s, (k + 1) * CONV_ROWS + CONV_HALO)
        yield k * CONV_ROWS, lo, hi


def _neighbours(p, lo, s):
    n = p.shape[0]
    row = lo + lax.broadcasted_iota(jnp.int32, p.shape, 0)
    prev = jnp.where(row == 0, 0.0, pltpu.roll(p, 1, axis=0))
    nxt = jnp.where(row == s - 1, 0.0, pltpu.roll(p, n - 1, axis=0))
    return prev, nxt


def _ab_mix(attn, proj, conv_w, aw, plan=None):
    s = proj.shape[0]
    nt = aw // LANES

    def col(group, sel):
        return pl.BlockSpec((s, LANES), lambda i: (0, group * nt + sel(i)))

    a_sel = lambda i: jnp.minimum(i, nt - 1)
    b_sel = lambda i: jnp.maximum(i - nt, 0)

    def body(attn_ref, za_ref, ub_ref, gb_ref, gc_ref, zb_ref, w_ref, y_ref):
        i = pl.program_id(0)

        @pl.when(i < nt)
        def _():
            y_ref[...] = (attn_ref[...].astype(F32) * _silu(za_ref[...].astype(F32))).astype(BF16)

        @pl.when(i >= nt)
        def _():
            w = w_ref[...]
            for c0, lo, hi in _conv_chunks(s):
                p = gc_ref[lo:hi, :].astype(F32) * ub_ref[lo:hi, :].astype(F32)
                prev, nxt = _neighbours(p, lo, s)
                cv = w[0:1, :] * prev + w[1:2, :] * p + w[2:3, :] * nxt
                yb = gb_ref[lo:hi, :].astype(F32) * cv * _silu(zb_ref[lo:hi, :].astype(F32))
                y_ref[c0:c0 + CONV_ROWS, :] = yb[c0 - lo:c0 - lo + CONV_ROWS, :].astype(BF16)

    return _call(
        body, name="ab_mix", grid=(2 * nt,),
        in_specs=[pl.BlockSpec((s, LANES), lambda i: (0, a_sel(i))),
                  col(3, a_sel), col(4, b_sel), col(5, b_sel), col(6, b_sel), col(7, b_sel),
                  pl.BlockSpec((3, LANES), lambda i: (0, b_sel(i)))],
        out_specs=pl.BlockSpec((s, LANES), lambda i: (0, i)),
        out_shape=jax.ShapeDtypeStruct((s, 2 * aw), BF16),
        operands=(attn, proj, proj, proj, proj, proj, conv_w), plan=plan)


def _conv_bwd(dproj, dy, proj, conv_w, aw):
    s = proj.shape[0]
    nt = aw // LANES

    def col(group):
        return pl.BlockSpec((s, LANES), lambda i: (0, group * nt + i))

    def body(dyb_ref, ub_ref, gb_ref, gc_ref, zb_ref, w_ref, dproj_in, dproj_ref, dw_ref, tiles, tile_sems):
        w = w_ref[...]
        dw = [jnp.zeros((1, LANES), F32) for _ in range(3)]
        for c0, lo, hi in _conv_chunks(s):
            ctr = slice(c0 - lo, c0 - lo + CONV_ROWS)
            out_rows = slice(c0, c0 + CONV_ROWS)
            ub = ub_ref[lo:hi, :].astype(F32)
            gc = gc_ref[lo:hi, :].astype(F32)
            gb = gb_ref[lo:hi, :].astype(F32)
            zb = zb_ref[lo:hi, :].astype(F32)
            dyb = dyb_ref[lo:hi, :].astype(F32)
            p = gc * ub
            prev, nxt = _neighbours(p, lo, s)
            cv = w[0:1, :] * prev + w[1:2, :] * p + w[2:3, :] * nxt
            sz, dsz = _silu_and_grad(zb)
            dcv = dyb * gb * sz
            dprev, dnxt = _neighbours(dcv, lo, s)
            dp = w[0:1, :] * dnxt + w[1:2, :] * dcv + w[2:3, :] * dprev
            for t, nb in enumerate((prev, p, nxt)):
                dw[t] = dw[t] + jnp.sum((dcv * nb)[ctr, :], axis=0, keepdims=True)
            tiles[0, out_rows, :] = (dp * gc)[ctr, :].astype(BF16)
            tiles[1, out_rows, :] = (dyb * cv * sz)[ctr, :].astype(BF16)
            tiles[2, out_rows, :] = (dp * ub)[ctr, :].astype(BF16)
            tiles[3, out_rows, :] = (dyb * gb * cv * dsz)[ctr, :].astype(BF16)
        dw_ref[...] = jnp.concatenate(dw, axis=0)
        _store_column_tiles(tiles, dproj_ref, tile_sems, [(4 + g) * nt + pl.program_id(0) for g in range(4)])

    return pl.pallas_call(
        body, name="conv_bwd", grid=(nt,),
        in_specs=[pl.BlockSpec((s, LANES), lambda i: (0, nt + i)),
                  col(4), col(5), col(6), col(7),
                  pl.BlockSpec((3, LANES), lambda i: (0, i)), HBM_SPEC],
        out_specs=[HBM_SPEC, pl.BlockSpec((3, LANES), lambda i: (0, i))],
        out_shape=[jax.ShapeDtypeStruct(dproj.shape, dproj.dtype), jax.ShapeDtypeStruct((3, aw), F32)],
        scratch_shapes=[pltpu.VMEM((4, s, LANES), BF16), pltpu.SemaphoreType.DMA((4,))],
        input_output_aliases={6: 0},
        compiler_params=_params(1),
    )(dy, proj, proj, proj, proj, conv_w, dproj)


def _sgu_norm(gv, ln_g, ln_b):
    mu = jnp.mean(gv, axis=-1, keepdims=True)
    xc = gv - mu
    rstd = lax.rsqrt(jnp.mean(xc * xc, axis=-1, keepdims=True) + EPS)
    vhat = xc * rstd
    return vhat, rstd, vhat * ln_g + ln_b


def _sgu_fwd(uvz, ln_g, ln_b, w_s, b_s, cw):
    s = uvz.shape[0]
    tr = 2 * CHUNK if s % (2 * CHUNK) == 0 else CHUNK
    gw = cw // N_GROUPS

    def body(u_ref, v_ref, z_ref, g_ref, b_ref, ws_ref, bs_ref, y_ref):
        _, _, vn = _sgu_norm(_gelu(v_ref[...].astype(F32)), g_ref[...], b_ref[...])
        vn = vn.astype(BF16)
        for ch in range(tr // CHUNK):
            rows = slice(ch * CHUNK, (ch + 1) * CHUNK)
            for grp in range(N_GROUPS):
                cols = slice(grp * gw, (grp + 1) * gw)
                mixed = lax.dot_general(ws_ref[grp], vn[rows, cols], NN, preferred_element_type=F32) + bs_ref[grp]
                y_ref[rows, cols] = (_gelu(u_ref[rows, cols].astype(F32)) * mixed
                                     * _silu(z_ref[rows, cols].astype(F32))).astype(BF16)

    full3 = lambda shape: pl.BlockSpec(shape, lambda i: (0, 0, 0))
    return pl.pallas_call(
        body, name="sgu_fwd", grid=(s // tr,),
        in_specs=[pl.BlockSpec((tr, cw), lambda i: (i, 0)), pl.BlockSpec((tr, cw), lambda i: (i, 1)),
                  pl.BlockSpec((tr, cw), lambda i: (i, 2)), _vec_spec(cw), _vec_spec(cw),
                  full3(w_s.shape), full3(b_s.shape)],
        out_specs=pl.BlockSpec((tr, cw), lambda i: (i, 0)),
        out_shape=jax.ShapeDtypeStruct((s, cw), BF16),
        compiler_params=_params(1, parallel=1),
    )(uvz, uvz, uvz, ln_g, ln_b, w_s, b_s)


def _sgu_bwd(uvz, dy, ln_g, ln_b, w_s, b_s, cw, plan=None):
    s = uvz.shape[0]
    tr = 2 * CHUNK if s % (2 * CHUNK) == 0 else CHUNK
    gw = cw // N_GROUPS

    def body(u_ref, v_ref, z_ref, dy_ref, g_ref, b_ref, ws_ref, bs_ref,
             duvz_ref, dws_ref, dbs_ref, dg_ref, db_ref, dvn_ref):
        vv = v_ref[...].astype(F32)
        gvec = g_ref[...]
        gelu_v, dgelu_v = _gelu_and_grad(vv)
        vhat, rstd, vn = _sgu_norm(gelu_v, gvec, b_ref[...])
        vn = vn.astype(BF16)
        first = pl.program_id(0) == 0

        @pl.when(first)
        def _():
            dws_ref[...] = jnp.zeros_like(dws_ref)
            dbs_ref[...] = jnp.zeros_like(dbs_ref)

        for ch in range(tr // CHUNK):
            rows = slice(ch * CHUNK, (ch + 1) * CHUNK)
            for grp in range(N_GROUPS):
                cols = slice(grp * gw, (grp + 1) * gw)
                vn_g = vn[rows, cols]
                mixed = lax.dot_general(ws_ref[grp], vn_g, NN, preferred_element_type=F32) + bs_ref[grp]
                uu = u_ref[rows, cols].astype(F32)
                zz = z_ref[rows, cols].astype(F32)
                dyv = dy_ref[rows, cols].astype(F32)
                (gu, dgu), (sz, dsz) = _gelu_and_grad(uu), _silu_and_grad(zz)
                duvz_ref[rows, grp * gw:(grp + 1) * gw] = (dyv * mixed * sz * dgu).astype(BF16)
                duvz_ref[rows, 2 * cw + grp * gw:2 * cw + (grp + 1) * gw] = (dyv * gu * mixed * dsz).astype(BF16)
                dmixed = dyv * gu * sz
                dm16 = dmixed.astype(BF16)
                dws_ref[grp] += lax.dot_general(dm16, vn_g, NT, preferred_element_type=F32)
                dbs_ref[grp] += jnp.broadcast_to(jnp.sum(dmixed, axis=1, keepdims=True), (CHUNK, LANES))
                dvn_ref[rows, cols] = lax.dot_general(ws_ref[grp], dm16, TN, preferred_element_type=F32)

        dvn = dvn_ref[...]
        _accumulate(dg_ref, jnp.sum(dvn * vhat, axis=0, keepdims=True))
        _accumulate(db_ref, jnp.sum(dvn, axis=0, keepdims=True))
        dvhat = dvn * gvec
        dgv = rstd * (dvhat - jnp.mean(dvhat, axis=-1, keepdims=True)
                      - vhat * jnp.mean(dvhat * vhat, axis=-1, keepdims=True))
        duvz_ref[:, cw:2 * cw] = (dgv * dgelu_v).astype(BF16)

    full3 = lambda shape: pl.BlockSpec(shape, lambda i: (0, 0, 0))
    acc3 = jax.ShapeDtypeStruct((N_GROUPS, CHUNK, LANES), F32)
    vec = jax.ShapeDtypeStruct((1, cw), F32)
    row = pl.BlockSpec((tr, cw), lambda i: (i, 0))
    return _call(
        body, name="sgu_bwd", grid=(s // tr,),
        in_specs=[row, pl.BlockSpec((tr, cw), lambda i: (i, 1)), pl.BlockSpec((tr, cw), lambda i: (i, 2)),
                  row, _vec_spec(cw), _vec_spec(cw), full3(w_s.shape), full3(b_s.shape)],
        out_specs=[pl.BlockSpec((tr, 3 * cw), lambda i: (i, 0)), full3((N_GROUPS, CHUNK, LANES)),
                   full3((N_GROUPS, CHUNK, LANES)), _vec_spec(cw), _vec_spec(cw)],
        out_shape=[jax.ShapeDtypeStruct((s, 3 * cw), BF16), acc3, acc3, vec, vec],
        scratch_shapes=[pltpu.VMEM((tr, cw), F32)],
        operands=(uvz, uvz, uvz, dy, ln_g, ln_b, w_s, b_s), plan=plan)


def _flat_rows(a):
    return a.reshape(-1, a.shape[-1])


def _add_sibling(grad, recv, core_idx):
    nchip, k, n = grad.shape
    tr = _tile(k // 2, (256, 128))
    nb = (k // 2) // tr

    def body(c_ref, g_ref, r_ref, o_ref):
        o_ref[...] = (g_ref[...].astype(F32) + r_ref[...].astype(F32)).astype(BF16)

    return pl.pallas_call(
        body, name="add_sibling",
        grid_spec=pltpu.PrefetchScalarGridSpec(
            num_scalar_prefetch=1, grid=(nchip, nb),
            in_specs=[pl.BlockSpec((None, tr, n), lambda q, i, c: (q, c[0] * nb + i, 0)),
                      pl.BlockSpec((None, tr, n), lambda q, i, c: (q, i, 0))],
            out_specs=pl.BlockSpec((None, tr, n), lambda q, i, c: (q, i, 0))),
        out_shape=jax.ShapeDtypeStruct((nchip, k // 2, n), BF16),
        compiler_params=_params(2, parallel=2),
    )(core_idx, grad, recv)


def _sum_chips(own, others, reduced, layer, place_idx):
    _, kh, n = own.shape
    tr = _tile(kh, (256, 128))
    nb = kh // tr

    def body(place_ref, own_ref, oth_ref, red_ref, o_ref):
        acc = own_ref[...].astype(F32)
        for q in range(3):
            acc = acc + oth_ref[q].astype(F32)
        o_ref[...] = acc

    return pl.pallas_call(
        body, name="sum_chips",
        grid_spec=pltpu.PrefetchScalarGridSpec(
            num_scalar_prefetch=1, grid=(nb,),
            in_specs=[pl.BlockSpec((None, tr, n), lambda i, p: (p[0], i, 0)),
                      pl.BlockSpec((3, tr, n), lambda i, p: (0, i, 0)),
                      HBM_SPEC],
            out_specs=pl.BlockSpec((None, tr, n), lambda i, p: (layer, p[1] * nb + i, 0))),
        out_shape=jax.ShapeDtypeStruct(reduced.shape, reduced.dtype),
        input_output_aliases={3: 0},
        compiler_params=_params(1, parallel=1),
    )(place_idx, own, others, reduced)


def _sum_devices(parts, plan=None):
    nd, r, _ = parts.shape
    tr = _tile(r, (512, 256, 128, 64, 32, 16, 8))

    def body(p_ref, o_ref):
        acc = p_ref[0]
        for q in range(1, nd):
            acc = acc + p_ref[q]
        o_ref[...] = acc

    return _call(
        body, name="sum_devices", grid=(r // tr,),
        in_specs=[pl.BlockSpec((nd, tr, LANES), lambda i: (0, i, 0))],
        out_specs=pl.BlockSpec((tr, LANES), lambda i: (i, 0)),
        out_shape=jax.ShapeDtypeStruct((r, LANES), F32),
        operands=(parts,), parallel=1, plan=plan)


def _adamw(w, g, m, v, also_grad=False):
    r, n = w.shape
    tr = _tile(r, [p for p in (1024, 512, 256, 128, 64, 32, 16, 8) if p * n <= ELEMENTWISE_BLOCK])
    n_out = 4 if also_grad else 3

    def body(w_ref, g_ref, m_ref, v_ref, d_ref, nm_ref, nv_ref, *g_out):
        gv = g_ref[...]
        if also_grad:
            g_out[0][...] = gv
        nm = ADAM_B1 * m_ref[...] + (1.0 - ADAM_B1) * gv
        nv = ADAM_B2 * v_ref[...] + (1.0 - ADAM_B2) * (gv * gv)
        m_hat = nm / (1.0 - ADAM_B1 ** ADAM_STEP)
        v_hat = nv / (1.0 - ADAM_B2 ** ADAM_STEP)
        d_ref[...] = -ADAM_LR * (m_hat / (jnp.sqrt(v_hat) + ADAM_EPS) + ADAM_WD * w_ref[...])
        nm_ref[...] = nm
        nv_ref[...] = nv

    spec = pl.BlockSpec((tr, n), lambda i: (i, 0))
    shp = jax.ShapeDtypeStruct((r, n), F32)
    return _call(
        body, name="adamw", grid=(r // tr,),
        in_specs=[spec] * 4, out_specs=[spec] * n_out, out_shape=[shp] * n_out,
        operands=(w, g, m, v), parallel=1)


def _pack(arrays, row_multiple=8):
    flat = [a.reshape(-1) for a in arrays]
    sizes = [f.shape[0] for f in flat]
    total = sum(sizes)
    unit = LANES * row_multiple
    padded = -(-total // unit) * unit
    if padded > total:
        flat.append(jnp.zeros((padded - total,), F32))
    offsets = [sum(sizes[:i]) for i in range(len(sizes))]
    return jnp.concatenate(flat).reshape(-1, LANES), offsets


def _unpack(packed, offsets, shapes):
    flat = packed.reshape(-1)
    return [flat[o:o + math.prod(s)].reshape(s) for o, s in zip(offsets, shapes)]


def kernel(x, c, ab_norm_g, ab_w_mod, ab_b_mod, ab_w_in, ab_conv_w, ab_w_out, sg_norm_g, sg_w_mod, sg_b_mod, sg_w_in, sg_ln_g, sg_ln_b, sg_w_s, sg_b_s, sg_w_out, final_norm_g, loss_target, m_ab_norm_g, m_ab_w_mod, m_ab_b_mod, m_ab_w_in, m_ab_conv_w, m_ab_w_out, m_sg_norm_g, m_sg_w_mod, m_sg_b_mod, m_sg_w_in, m_sg_ln_g, m_sg_ln_b, m_sg_w_s, m_sg_b_s, m_sg_w_out, m_final_norm_g, v_ab_norm_g, v_ab_w_mod, v_ab_b_mod, v_ab_w_in, v_ab_conv_w, v_ab_w_out, v_sg_norm_g, v_sg_w_mod, v_sg_b_mod, v_sg_w_in, v_sg_ln_g, v_sg_ln_b, v_sg_w_s, v_sg_b_s, v_sg_w_out, v_final_norm_g):
    s, d = x.shape[1], x.shape[2]
    aw = d // 2
    cw = d
    mod_l = ab_w_mod.shape[-1]
    x0 = x[0]
    target = loss_target[0]
    mx, my, mc = lax.axis_index("x"), lax.axis_index("y"), lax.axis_index("c")
    chip = 2 * mx + my
    chip_idx = jnp.reshape(chip, (1,)).astype(jnp.int32)
    core_idx = jnp.reshape(mc, (1,)).astype(jnp.int32)
    place_idx = jnp.stack([chip, mc]).astype(jnp.int32)

    win = [_place_own_shard(ab_w_in if L % 2 == 0 else sg_w_in, L // 2, chip_idx) for L in range(4)]
    wout = [_place_own_shard(ab_w_out if L % 2 == 0 else sg_w_out, L // 2, chip_idx) for L in range(4)]
    k_in, k_out = d, wout[0].shape[1]

    def gather_plan(ici=(), pass_on=()):
        arrays, copies, names = [], [], []
        for kind, L, only in ici:
            arrays.append(win[L] if kind == "in" else wout[L])
            names.append((kind, L))
            copies += _gather_ici(len(arrays) - 1, k_in if kind == "in" else k_out, only)
        for kind, L in pass_on:
            arrays.append(win[L] if kind == "in" else wout[L])
            names.append((kind, L))
            copies += _gather_pass_on(len(arrays) - 1, k_in if kind == "in" else k_out)
        return _Plan(tuple(arrays), tuple(copies)), names

    def absorb(plan_and_names, updated):
        _, names = plan_and_names
        for pos, (kind, L) in enumerate(names):
            if kind == "in":
                win[L] = updated[pos]
            else:
                wout[L] = updated[pos]

    first_copies = _gather_ici(0, k_in)
    first_send, first_recv, win[0], token = _start_copies("gather_first_start", win[0], first_copies)

    small_local = [c[0], ab_conv_w, sg_norm_g, sg_ln_g, sg_ln_b]
    small_shapes = [a.shape for a in small_local]
    payload, small_off = _pack(small_local)
    payload = payload + token[0, 0]
    gathered = _all_to_all(jnp.broadcast_to(payload[None], (N_DEV,) + payload.shape), "gather_small")
    per_dev = [_unpack(gathered[b], small_off, small_shapes) for b in range(N_DEV)]
    c_all = jnp.stack([per_dev[b][0] for b in range(N_DEV)])

    def from_chips(idx, axis):
        return jnp.concatenate([per_dev[2 * q][idx] for q in range(N_CHIPS)], axis=axis)

    conv_w_full = from_chips(1, 2)
    sg_norm_g_full = from_chips(2, 1)
    sg_ln_g_full = from_chips(3, 1)
    sg_ln_b_full = from_chips(4, 1)

    ab_b_local = lax.dynamic_slice_in_dim(ab_b_mod, chip * mod_l, mod_l, axis=1)
    mod_rows = []
    for layer in range(4):
        i = layer // 2
        w_mod, bias = (ab_w_mod, ab_b_local) if layer % 2 == 0 else (sg_w_mod, sg_b_mod)
        mod_rows.append(_mod_fwd(c_all, w_mod, bias[i:i + 1], i))
    mod_local = jnp.stack(mod_rows, axis=1)
    mod_recv = _all_to_all(mod_local.reshape(N_DEV, -1, LANES), "exchange_mod")
    mod_recv = mod_recv.reshape(N_DEV, 4, mod_l)
    mod_full = jnp.concatenate([mod_recv[2 * q] for q in range(N_CHIPS)], axis=-1)
    shifts = [mod_full[l:l + 1, :d] for l in range(4)]
    scales = [mod_full[l:l + 1, d:2 * d] for l in range(4)]
    gates = [mod_full[l:l + 1, 2 * d:] for l in range(4)]

    cos, sin = _rope_tables(s)
    w_s16 = sg_w_s.astype(BF16)
    b_s3 = sg_b_s[..., None]

    near, far, everyone = (0, 1), (2,), (0, 1, 2)
    fwd_comm = {
        ("in_proj", 0): ([("in", 1, near), ("out", 0, everyone), ("out", 1, everyone)], []),
        ("attn", 0): ([("in", 1, far), ("in", 2, near)], [("out", 0)]),
        ("out_proj", 0): ([], [("in", 1), ("out", 1)]),
        ("in_proj", 1): ([("in", 2, far)], []),
        ("out_proj", 1): ([], [("in", 2)]),
        ("in_proj", 2): ([("in", 3, everyone)], []),
        ("attn", 2): ([("out", 3, everyone), ("out", 2, everyone)], []),
        ("ab_mix", 2): ([], [("out", 2)]),
        ("out_proj", 2): ([], [("in", 3), ("out", 3)]),
    }

    def carried(key, fn, *args):
        if key not in fwd_comm:
            return fn(*args)
        pn = gather_plan(*fwd_comm[key])
        res, updated = fn(*args, plan=pn[0])
        absorb(pn, updated)
        return res

    def norm_params(layer):
        g = ab_norm_g if layer % 2 == 0 else sg_norm_g_full
        return g[layer // 2:layer // 2 + 1], scales[layer], shifts[layer]

    saved = []
    xs = x0
    h = _prenorm(xs, *norm_params(0))
    win[0] = _wait_copies("gather_first_wait", win[0], first_send, first_recv, first_copies, h)
    win[0] = _comm_stages("gather_first_pass_on", [win[0]], [_gather_pass_on(0, k_in)])[0]
    for layer in range(4):
        i = layer // 2
        next_norm = norm_params(layer + 1) if layer < 3 else None
        if layer % 2 == 0:
            proj = carried(("in_proj", layer), _in_proj, h, win[layer])
            attn, lse = carried(("attn", layer), _attn_fwd, proj, cos, sin, aw)
            y = carried(("ab_mix", layer), _ab_mix, attn, proj, conv_w_full[i], aw)
            res = carried(("out_proj", layer), _out_proj_residual, y, wout[layer].reshape(-1, d), xs,
                          gates[layer], next_norm)
            saved.append((xs, h, proj, y, res[1], attn, lse))
        else:
            uvz = carried(("in_proj", layer), _in_proj, h, win[layer])
            y = _sgu_fwd(uvz, sg_ln_g_full[i:i + 1], sg_ln_b_full[i:i + 1], w_s16[i], b_s3[i], cw)
            res = carried(("out_proj", layer), _out_proj_residual, y, wout[layer].reshape(-1, d), xs,
                          gates[layer], next_norm)
            saved.append((xs, h, uvz, y, res[1]))
        xs = res[0]
        h = res[2] if next_norm is not None else None

    loss11, dx, d_final_g, dout, dgate = _final_loss(xs, target, final_norm_g[None], saved[3][4], gates[3])
    loss = lax.psum(loss11[0, 0], ("x", "y", "c"))

    reduced = {"in": [lax.empty((2,) + w.shape[1:], F32) for w in (ab_w_in, sg_w_in)],
               "out": [lax.empty((2,) + w.shape[1:], F32) for w in (ab_w_out, sg_w_out)]}
    grads = {}
    stage = {}
    k_of = {"in": k_in, "out": k_out}

    def swap_plan(which):
        arrays, copies = [], []
        for kind, L in which:
            g = grads[kind, L]
            arrays += [g, jax.ShapeDtypeStruct((N_CHIPS, g.shape[1] // 2, g.shape[2]), BF16)]
            copies += _reduce_swap(len(arrays) - 2, len(arrays) - 1, k_of[kind])
        return _Plan(tuple(arrays), tuple(copies))

    def after_swap(which, updated):
        for pos, (kind, L) in enumerate(which):
            stage[kind, L] = _add_sibling(grads[kind, L], updated[2 * pos + 1], core_idx)

    def ici_plan(pieces):
        arrays, copies = [], []
        for kind, L, only in pieces:
            cs = stage[kind, L]
            arrays += [cs, stage.get((kind, L, "recv"), jax.ShapeDtypeStruct((3,) + cs.shape[1:], BF16))]
            copies += _reduce_ici(len(arrays) - 2, len(arrays) - 1, only)
        return _Plan(tuple(arrays), tuple(copies))

    def after_ici(pieces, updated):
        for pos, (kind, L, _) in enumerate(pieces):
            stage[kind, L, "recv"] = updated[2 * pos + 1]

    def sum_layer(L):
        for kind in ("in", "out"):
            reduced[kind][L % 2] = _sum_chips(stage[kind, L], stage[kind, L, "recv"], reduced[kind][L % 2],
                                              L // 2, place_idx)

    def share_plan(L):
        arrays = (reduced["in"][L % 2], reduced["out"][L % 2])
        copies = _reduce_share(0, L // 2, k_in) + _reduce_share(1, L // 2, k_out)
        return _Plan(arrays, tuple(copies))

    def after_share(L, updated):
        reduced["in"][L % 2], reduced["out"][L % 2] = updated[0], updated[1]

    all_chips = (0, 1, 2)
    dmods = [None] * 4
    d_ab_norm_g, d_sg_norm_g = [None, None], [None, None]
    d_conv_w, d_ln_g, d_ln_b, d_w_s, d_b_s = ([None, None] for _ in range(5))
    for layer in reversed(range(4)):
        i = layer // 2
        prev = layer + 1
        busy = prev < 4
        if layer % 2 == 0:
            xs, h, proj, y, out, attn, lse = saved[layer]
        else:
            xs, h, uvz, y, out = saved[layer]
        below = (saved[layer - 1][4], gates[layer - 1]) if layer > 0 else None
        w2 = wout[layer].reshape(-1, d)
        grads["out", layer] = _out_proj_bwd_w(y, dout).reshape(N_CHIPS, -1, d)
        if busy:
            swapped = [("in", prev), ("out", prev)] + ([("out", 0)] if layer == 0 else [])
            dy, updated = _out_proj_bwd_act(dout, w2, plan=swap_plan(swapped))
            after_swap(swapped, updated)
        else:
            dy = _out_proj_bwd_act(dout, w2)
        if layer % 2 == 0:
            if busy:
                pieces = [("in", prev, all_chips), ("out", prev, all_chips)]
                pieces += [("out", 0, all_chips)] if layer == 0 else []
                dact, updated = _attn_bwd(proj, cos, sin, dy, attn, lse, aw, plan=ici_plan(pieces))
                after_ici(pieces, updated)
                sum_layer(prev)
            else:
                dact = _attn_bwd(proj, cos, sin, dy, attn, lse, aw)
            dact, d_conv_w[i] = _conv_bwd(dact, dy, proj, conv_w_full[i], aw)
            if layer == 0:
                gating, gating_off = _pack([jnp.stack(d_w_s), jnp.stack(d_b_s)], PACKED_ROW_BLOCK)
                slots = _place_own_slot(gating, jnp.reshape(2 * chip + mc, (1,)).astype(jnp.int32))
                sharing = share_plan(prev)
                plan = _merge_plans(sharing, _Plan((gating, slots), tuple(_broadcast_copies(0, 1))))
                grads["in", 0], updated = _in_proj_bwd_w(h, dact, win[0].shape[-1], plan=plan)
                after_share(prev, updated)
                all_gating = updated[len(sharing.arrays) + 1]
                plan = swap_plan([("in", 0)])
                after_swap([("in", 0)], _comm_stages("grads_to_sibling", plan.arrays, [plan.copies]))
                pieces = [("in", 0, all_chips)]
                dh, updated = _in_proj_bwd_act(dact, win[0], plan=ici_plan(pieces))
                after_ici(pieces, updated)
            elif busy:
                dh, updated = _in_proj_bwd_act(dact, win[layer], plan=share_plan(prev))
                after_share(prev, updated)
                grads["in", layer] = _in_proj_bwd_w(h, dact, win[layer].shape[-1])
            else:
                dh = _in_proj_bwd_act(dact, win[layer])
                grads["in", layer] = _in_proj_bwd_w(h, dact, win[layer].shape[-1])
            norm_g = ab_norm_g[i:i + 1]
        else:
            sgu_args = (uvz, dy, sg_ln_g_full[i:i + 1], sg_ln_b_full[i:i + 1], w_s16[i], b_s3[i], cw)
            if busy:
                pieces = [("in", prev, (0, 1))]
                res, updated = _sgu_bwd(*sgu_args, plan=ici_plan(pieces))
                after_ici(pieces, updated)
            else:
                res = _sgu_bwd(*sgu_args)
            dact, d_w_s[i], db_wide, d_ln_g[i], d_ln_b[i] = res
            d_b_s[i] = db_wide[:, :, 0]
            if busy:
                pieces = [("in", prev, (2,)), ("out", prev, all_chips)]
                dh, updated = _in_proj_bwd_act(dact, win[layer], plan=ici_plan(pieces))
                after_ici(pieces, updated)
                sum_layer(prev)
                grads["in", layer], updated = _in_proj_bwd_w(h, dact, win[layer].shape[-1], plan=share_plan(prev))
                after_share(prev, updated)
            else:
                dh = _in_proj_bwd_act(dact, win[layer])
                grads["in", layer] = _in_proj_bwd_w(h, dact, win[layer].shape[-1])
            norm_g = sg_norm_g_full[i:i + 1]
        res = _prenorm_bwd(xs, dh, dx, norm_g, scales[layer], below)
        dx, dshift, dscale, d_norm_g = res[:4]
        (d_ab_norm_g if layer % 2 == 0 else d_sg_norm_g)[i] = d_norm_g
        dmods[layer] = jnp.concatenate([dshift, dscale, dgate], axis=1)
        if below:
            dout, dgate = res[4:]
    grad_x = dx[None]

    partial_list = [jnp.concatenate(dmods, axis=0),
                    jnp.concatenate(d_ab_norm_g, axis=0), jnp.concatenate(d_sg_norm_g, axis=0), d_final_g[0],
                    jnp.stack(d_conv_w), jnp.concatenate(d_ln_g, axis=0), jnp.concatenate(d_ln_b, axis=0)]
    partial_shapes = [a.shape for a in partial_list]
    partials, part_off = _pack(partial_list, PACKED_ROW_BLOCK)
    all_partials = _all_to_all(jnp.broadcast_to(partials[None], (N_DEV,) + partials.shape), "gather_partials")
    sum_layer(0)
    summed_packed, updated = _sum_devices(all_partials, plan=share_plan(0))
    after_share(0, updated)
    summed = _unpack(summed_packed, part_off, partial_shapes)
    g_mod_bias, g_ab_norm_g, g_sg_norm_g_full, g_final_g, g_conv_full, g_ln_g_full, g_ln_b_full = summed
    g_w_s, g_b_s = _unpack(_sum_devices(all_gating), gating_off, [sg_w_s.shape, sg_b_s.shape])
    dm_all = jnp.stack([_unpack(all_partials[b], part_off[:1], partial_shapes[:1])[0] for b in range(N_DEV)])
    dm_local = lax.dynamic_slice_in_dim(dm_all, chip * mod_l, mod_l, axis=2)

    def chip_cols(a, axis):
        width = a.shape[axis] // N_CHIPS
        return lax.dynamic_slice_in_dim(a, chip * width, width, axis=axis)

    g_ab_b_mod = jnp.stack([g_mod_bias[0], g_mod_bias[2]])
    g_sg_b_mod = chip_cols(jnp.stack([g_mod_bias[1], g_mod_bias[3]]), 1)
    g_ab_w_mod = _mod_bwd_w(c_all, jnp.stack([dm_local[:, 0], dm_local[:, 2]]))
    g_sg_w_mod = _mod_bwd_w(c_all, jnp.stack([dm_local[:, 1], dm_local[:, 3]]))
    g_conv = chip_cols(g_conv_full, 2)
    g_sg_norm_g = chip_cols(g_sg_norm_g_full, 1)
    g_ln_g = chip_cols(g_ln_g_full, 1)
    g_ln_b = chip_cols(g_ln_b_full, 1)

    def step_big(w, g, m, v, also_grad=False):
        res = _adamw(_flat_rows(w), _flat_rows(g), _flat_rows(m), _flat_rows(v), also_grad)
        return tuple(a.reshape(w.shape) for a in res)

    big_out = {
        "ab_w_mod": step_big(ab_w_mod, g_ab_w_mod, m_ab_w_mod, v_ab_w_mod),
        "ab_w_in": step_big(ab_w_in, reduced["in"][0], m_ab_w_in, v_ab_w_in, True),
        "ab_w_out": step_big(ab_w_out, reduced["out"][0], m_ab_w_out, v_ab_w_out, True),
        "sg_w_mod": step_big(sg_w_mod, g_sg_w_mod, m_sg_w_mod, v_sg_w_mod),
        "sg_w_in": step_big(sg_w_in, reduced["in"][1], m_sg_w_in, v_sg_w_in, True),
        "sg_w_out": step_big(sg_w_out, reduced["out"][1], m_sg_w_out, v_sg_w_out, True),
    }
    g_ab_w_in, g_ab_w_out = big_out["ab_w_in"][3], big_out["ab_w_out"][3]
    g_sg_w_in, g_sg_w_out = big_out["sg_w_in"][3], big_out["sg_w_out"][3]
    small_names = ["ab_norm_g", "ab_b_mod", "ab_conv_w", "sg_norm_g", "sg_b_mod", "sg_ln_g", "sg_ln_b",
                   "sg_w_s", "sg_b_s", "final_norm_g"]
    small_w = [ab_norm_g, ab_b_mod, ab_conv_w, sg_norm_g, sg_b_mod, sg_ln_g, sg_ln_b, sg_w_s, sg_b_s, final_norm_g]
    small_g = [g_ab_norm_g, g_ab_b_mod, g_conv, g_sg_norm_g, g_sg_b_mod, g_ln_g, g_ln_b, g_w_s, g_b_s, g_final_g]
    small_m = [m_ab_norm_g, m_ab_b_mod, m_ab_conv_w, m_sg_norm_g, m_sg_b_mod, m_sg_ln_g, m_sg_ln_b, m_sg_w_s,
               m_sg_b_s, m_final_norm_g]
    small_v = [v_ab_norm_g, v_ab_b_mod, v_ab_conv_w, v_sg_norm_g, v_sg_b_mod, v_sg_ln_g, v_sg_ln_b, v_sg_w_s,
               v_sg_b_s, v_final_norm_g]
    shapes = [a.shape for a in small_w]
    pw, off = _pack(small_w, PACKED_ROW_BLOCK)
    pg, _ = _pack(small_g, PACKED_ROW_BLOCK)
    pm, _ = _pack(small_m, PACKED_ROW_BLOCK)
    pv, _ = _pack(small_v, PACKED_ROW_BLOCK)
    pd, pnm, pnv = _adamw(pw, pg, pm, pv)
    small_out = {}
    for name, dl, nm, nv in zip(small_names, _unpack(pd, off, shapes), _unpack(pnm, off, shapes),
                                _unpack(pnv, off, shapes)):
        small_out[name] = (dl, nm, nv)

    grad_of = {
        "ab_norm_g": g_ab_norm_g, "ab_w_mod": g_ab_w_mod, "ab_b_mod": g_ab_b_mod, "ab_w_in": g_ab_w_in,
        "ab_conv_w": g_conv, "ab_w_out": g_ab_w_out, "sg_norm_g": g_sg_norm_g, "sg_w_mod": g_sg_w_mod,
        "sg_b_mod": g_sg_b_mod, "sg_w_in": g_sg_w_in, "sg_ln_g": g_ln_g, "sg_ln_b": g_ln_b, "sg_w_s": g_w_s,
        "sg_b_s": g_b_s, "sg_w_out": g_sg_w_out, "final_norm_g": g_final_g,
    }
    order = ["ab_norm_g", "ab_w_mod", "ab_b_mod", "ab_w_in", "ab_conv_w", "ab_w_out", "sg_norm_g", "sg_w_mod",
             "sg_b_mod", "sg_w_in", "sg_ln_g", "sg_ln_b", "sg_w_s", "sg_b_s", "sg_w_out", "final_norm_g"]
    steps = {**big_out, **small_out}
    return (loss, grad_x, *[grad_of[n] for n in order], *[steps[n][0] for n in order],
            *[steps[n][1] for n in order], *[steps[n][2] for n in order])
```
